```python
import math
import jax
import jax.numpy as jnp
from jax import lax
import numpy as np

D_MODEL = 1024
BATCH = 8
SEQ = 2048
DEPTH = 2

F32 = jnp.float32
GRID_W = 64
CTX_LEN = 256
N_MOD = 6
NORM_EPS = 1e-6
ROPE_BASE = 10000.0
CONV_W = 4

LRU_WIDTH = 512
LRU_BLOCKS = 8
LRU_BLOCK = LRU_WIDTH // LRU_BLOCKS
LRU_C = 8.0

HEAD_DIM = 64
WIN_HEADS = 8
WIN_KV_HEADS = 2
WINDOW = 128
WBLK = 128

DIFF_HEADS = 4
DIFF_DH = 64
QBLK = 128

SSD_HEADS = 8
SSD_HEAD_DIM = 64
SSD_INNER = SSD_HEADS * SSD_HEAD_DIM
SSD_GROUPS = 2
SSD_STATE = 128
SSD_CHUNK = 128
SSD_CONV_DIM = SSD_INNER + 2 * SSD_GROUPS * SSD_STATE

N_EXPERTS = 64
N_EXPERT_GROUPS = 8
TOPK_GROUPS = 4
TOP_K = 8
D_EXPERT = 256
D_SHARED = 256
ROUTED_SCALE = 2.5
EXPERT_BLK = 256

N_EVEN = (DEPTH + 1) // 2
N_ODD = DEPTH // 2

AB_Q = WIN_HEADS * HEAD_DIM
AB_KV = WIN_KV_HEADS * HEAD_DIM
AB_CTX_START = LRU_WIDTH + AB_Q
AB_IN = AB_CTX_START + LRU_WIDTH + 2 * AB_KV
AB_SPLITS = (LRU_WIDTH, AB_CTX_START, AB_CTX_START + LRU_WIDTH, AB_CTX_START + LRU_WIDTH + AB_KV)
AB_CTX_SPLITS = (LRU_WIDTH, LRU_WIDTH + AB_KV)
AB_OUT = LRU_WIDTH + AB_Q

CD_QK = DIFF_HEADS * 2 * DIFF_DH
CD_V = DIFF_HEADS * 2 * DIFF_DH
CD_CTX_START = CD_QK + SSD_INNER
CD_IN = CD_CTX_START + CD_QK + CD_V + SSD_CONV_DIM + 2 * SSD_HEADS
CD_SPLITS = (CD_QK, CD_CTX_START, CD_CTX_START + CD_QK, CD_CTX_START + CD_QK + CD_V,
             CD_CTX_START + CD_QK + CD_V + SSD_CONV_DIM)
CD_CTX_SPLITS = (CD_QK, CD_QK + CD_V, CD_QK + CD_V + SSD_CONV_DIM)
CD_OUT = CD_V + SSD_INNER

kernel_name = 'hybrid_diffusion_rglru_swa_diffattn_ssd_moe'


def rms_norm(x, g):
    x32 = x.astype(F32)
    y = x32 * lax.rsqrt(jnp.mean(x32 * x32, axis=-1, keepdims=True) + NORM_EPS)
    return (y * g.astype(F32)).astype(x.dtype)


def modulate(h, shift, scale):
    return h * (1.0 + scale) + shift


def axial_rope_tables(n_tok, dim, dtype):
    rows = n_tok // GRID_W
    row = jnp.repeat(jnp.arange(rows), GRID_W).astype(F32)
    col = jnp.tile(jnp.arange(GRID_W), rows).astype(F32)
    n = dim // 4
    inv = ROPE_BASE ** (-jnp.arange(n, dtype=F32) / n)
    ang = jnp.concatenate([row[:, None] * inv, col[:, None] * inv], axis=-1)
    return jnp.cos(ang).astype(dtype), jnp.sin(ang).astype(dtype)


def apply_rope(x, cos, sin):
    half = x.shape[-1] // 2
    x1, x2 = x[..., :half], x[..., half:]
    c, s = cos[:, None, :], sin[:, None, :]
    return jnp.concatenate([x1 * c - x2 * s, x1 * s + x2 * c], axis=-1)


def dwconv_centred(u, w, b):
    ch = u.shape[-1]
    y = lax.conv_general_dilated(u, w[:, None, :].astype(u.dtype), window_strides=(1,),
                                 padding=[((CONV_W - 1) // 2, CONV_W // 2)],
                                 dimension_numbers=('NWC', 'WIO', 'NWC'), feature_group_count=ch)
    return y + b


def linear_scan(a, b, h0, reverse):
    def combine(left, right):
        a1, b1 = left
        a2, b2 = right
        return a1 * a2, a2 * b1 + b2
    a_cum, h = lax.associative_scan(combine, (a, b), axis=1, reverse=reverse)
    h = h + a_cum * h0[:, None, :]
    h_final = h[:, 0] if reverse else h[:, -1]
    return h, h_final


def rglru_coeffs(u, w_r, b_r, w_i, b_i, lam):
    bsz, n_tok, _ = u.shape
    ub = u.reshape(bsz, n_tok, LRU_BLOCKS, LRU_BLOCK)
    r = jax.nn.sigmoid(jnp.einsum('blnc,ncd->blnd', ub, w_r).reshape(bsz, n_tok, LRU_WIDTH).astype(F32)
                       + b_r.astype(F32))
    i = jax.nn.sigmoid(jnp.einsum('blnc,ncd->blnd', ub, w_i).reshape(bsz, n_tok, LRU_WIDTH).astype(F32)
                       + b_i.astype(F32))
    log_a = -LRU_C * r * jax.nn.softplus(-lam.astype(F32))
    a = jnp.exp(log_a)
    bx = jnp.sqrt(-jnp.expm1(2.0 * log_a)) * i * u.astype(F32)
    return a, bx


def window_attention(q, k, v, k_ctx, v_ctx, sink):
    bsz, s_len, kv, grp, hd = q.shape
    c_len = k_ctx.shape[1]
    nb = s_len // WBLK
    qb = q.reshape(bsz, nb, WBLK, kv, grp, hd)

    def band(t):
        tp = jnp.pad(t, ((0, 0), (WBLK, WBLK), (0, 0), (0, 0))).reshape(bsz, nb + 2, WBLK, kv, hd)
        return jnp.concatenate([tp[:, :-2], tp[:, 1:-1], tp[:, 2:]], axis=2)

    kb, vb = band(k), band(v)
    scale = hd ** -0.5
    s_band = jnp.einsum('bnqhgd,bnkhd->bnhgqk', qb, kb).astype(F32) * scale
    qpos = jnp.arange(nb)[:, None, None] * WBLK + jnp.arange(WBLK)[None, :, None]
    kpos = jnp.arange(nb)[:, None, None] * WBLK - WBLK + jnp.arange(3 * WBLK)[None, None, :]
    valid = (jnp.abs(qpos - kpos) <= WINDOW) & (kpos >= 0) & (kpos < s_len)
    s_band = jnp.where(valid[None, :, None, None], s_band, -jnp.inf)
    s_ctx = jnp.einsum('bnqhgd,bkhd->bnhgqk', qb, k_ctx).astype(F32) * scale
    s_sink = jnp.broadcast_to(sink.astype(F32)[None, None, :, :, None, None], s_ctx.shape[:-1] + (1,))
    p = jax.nn.softmax(jnp.concatenate([s_sink, s_ctx, s_band], axis=-1), axis=-1).astype(v.dtype)
    o = (jnp.einsum('bnhgqk,bkhd->bnqhgd', p[..., 1:1 + c_len], v_ctx)
         + jnp.einsum('bnhgqk,bnkhd->bnqhgd', p[..., 1 + c_len:], vb))
    return o.reshape(bsz, s_len, kv * grp * hd)


def context_sink_attention(q, k, v, sink):
    bsz, c_len, kv, grp, hd = q.shape
    s = jnp.einsum('bqhgd,bkhd->bhgqk', q, k).astype(F32) * hd ** -0.5
    s_sink = jnp.broadcast_to(sink.astype(F32)[None, :, :, None, None], s.shape[:-1] + (1,))
    p = jax.nn.softmax(jnp.concatenate([s_sink, s], axis=-1), axis=-1)[..., 1:].astype(v.dtype)
    return jnp.einsum('bhgqk,bkhd->bqhgd', p, v).reshape(bsz, c_len, kv * grp * hd)


def diff_attention(q, k, v, lam, lam_init, subln_g):
    s = jnp.einsum('bqhmd,bkhmd->bhmqk', q, k).astype(F32) * DIFF_DH ** -0.5
    p = jax.nn.softmax(s, axis=-1)
    w = (p[:, :, 0] - lam * p[:, :, 1]).astype(v.dtype)
    o = jnp.einsum('bhqk,bkhe->bqhe', w, v)
    return rms_norm(o, subln_g) * (1.0 - lam_init)


def ssd_inputs(xbc, conv_w, conv_b):
    bsz, n_tok, _ = xbc.shape
    u = jax.nn.silu(dwconv_centred(xbc, conv_w, conv_b)).astype(F32)
    xs, bm, cm = jnp.split(u, (SSD_INNER, SSD_INNER + SSD_GROUPS * SSD_STATE), axis=-1)
    return (xs.reshape(bsz, n_tok, SSD_HEADS, SSD_HEAD_DIM),
            bm.reshape(bsz, n_tok, SSD_GROUPS, SSD_STATE),
            cm.reshape(bsz, n_tok, SSD_GROUPS, SSD_STATE))


def ssd_chunked(x, dt, a, bm, cm, h0, with_output):
    bsz, n_tok, n_h, n_p = x.shape
    n_g, n_n = bm.shape[2], bm.shape[3]
    n_r = n_h // n_g
    nc = n_tok // SSD_CHUNK
    q = SSD_CHUNK
    xc = x.reshape(bsz, nc, q, n_g, n_r, n_p)
    dtc = dt.reshape(bsz, nc, q, n_g, n_r)
    bc = bm.reshape(bsz, nc, q, n_g, n_n)
    acum = jnp.cumsum(dtc * a.reshape(n_g, n_r), axis=2)
    to_end = jnp.exp(acum[:, :, -1:] - acum) * dtc
    states = jnp.einsum('bclgn,bclgrp->bcgrpn', bc, xc * to_end[..., None])
    chunk_decay = jnp.exp(acum[:, :, -1])

    def step(h, inp):
        s_c, dcy = inp
        return dcy[..., None, None] * h + s_c, h

    h_fin, h_in = lax.scan(step, h0.reshape(bsz, n_g, n_r, n_p, n_n),
                           (jnp.moveaxis(states, 1, 0), jnp.moveaxis(chunk_decay, 1, 0)))
    h_fin = h_fin.reshape(bsz, n_h, n_p, n_n)
    if not with_output:
        return None, h_fin
    h_in = jnp.moveaxis(h_in, 0, 1)
    cc = cm.reshape(bsz, nc, q, n_g, n_n)
    seg = acum[:, :, :, None] - acum[:, :, None, :]
    lower = jnp.tril(jnp.ones((q, q), bool))[:, :, None, None]
    decay = jnp.exp(jnp.where(lower, seg, -jnp.inf))
    cb = jnp.einsum('bclgn,bcsgn->bclsg', cc, bc)
    w = cb[..., None] * decay * dtc[:, :, None]
    y = jnp.einsum('bclsgr,bcsgrp->bclgrp', w, xc)
    y = y + jnp.einsum('bclgn,bcgrpn->bclgrp', cc, h_in) * jnp.exp(acum)[..., None]
    return y.reshape(bsz, n_tok, n_h, n_p), h_fin


def ssd_dir(x, dt, a, bm, cm, h0, reverse, with_output):
    if reverse:
        x, dt, bm, cm = (jnp.flip(t, axis=1) for t in (x, dt, bm, cm))
    y, h_fin = ssd_chunked(x, dt, a, bm, cm, h0, with_output)
    if reverse and with_output:
        y = jnp.flip(y, axis=1)
    return y, h_fin


def gated_group_norm(y, z, g):
    bsz, n_tok = y.shape[:2]
    gs = SSD_INNER // SSD_GROUPS
    yz = y.reshape(bsz, n_tok, SSD_GROUPS, gs) * jax.nn.silu(z.astype(F32)).reshape(bsz, n_tok, SSD_GROUPS, gs)
    return rms_norm(yz, g.reshape(SSD_GROUPS, gs)).reshape(bsz, n_tok, SSD_INNER).astype(z.dtype)


def mixer_rglru_window(hl, hc, w_in, w_out, conv_w, conv_b, w_r, b_r, w_i, b_i, lam, sink, ctx_out):
    bsz, s_len, _ = hl.shape
    c_len = hc.shape[1]
    grp = WIN_HEADS // WIN_KV_HEADS
    gate_l, q_l, xa_l, k_l, v_l = jnp.split(hl @ w_in, AB_SPLITS, axis=-1)
    if ctx_out:
        gate_c, q_c, xa_c, k_c, v_c = jnp.split(hc @ w_in, AB_SPLITS, axis=-1)
    else:
        xa_c, k_c, v_c = jnp.split(hc @ w_in[:, AB_CTX_START:], AB_CTX_SPLITS, axis=-1)
    u_l = dwconv_centred(xa_l, conv_w, conv_b)
    u_c = dwconv_centred(xa_c, conv_w, conv_b)
    rec_l = jnp.zeros(u_l.shape, F32)
    rec_c = jnp.zeros(u_c.shape, F32)
    for d, rev in enumerate((False, True)):
        a_c, bx_c = rglru_coeffs(u_c, w_r[d], b_r[d], w_i[d], b_i[d], lam[d])
        h_c, h_c_final = linear_scan(a_c, bx_c, jnp.zeros((bsz, LRU_WIDTH), F32), rev)
        a_l, bx_l = rglru_coeffs(u_l, w_r[d], b_r[d], w_i[d], b_i[d], lam[d])
        h_l, _ = linear_scan(a_l, bx_l, h_c_final, rev)
        rec_l = rec_l + h_l
        if ctx_out:
            rec_c = rec_c + h_c
    cos, sin = axial_rope_tables(s_len, HEAD_DIM, hl.dtype)
    q_l = apply_rope(q_l.reshape(bsz, s_len, WIN_HEADS, HEAD_DIM), cos, sin).reshape(
        bsz, s_len, WIN_KV_HEADS, grp, HEAD_DIM)
    k_l = apply_rope(k_l.reshape(bsz, s_len, WIN_KV_HEADS, HEAD_DIM), cos, sin)
    v_l = v_l.reshape(bsz, s_len, WIN_KV_HEADS, HEAD_DIM)
    k_c = k_c.reshape(bsz, c_len, WIN_KV_HEADS, HEAD_DIM)
    v_c = v_c.reshape(bsz, c_len, WIN_KV_HEADS, HEAD_DIM)
    sink = sink.reshape(WIN_KV_HEADS, grp)
    att_l = window_attention(q_l, k_l, v_l, k_c, v_c, sink)
    lru_l = rec_l.astype(hl.dtype) * jax.nn.gelu(gate_l)
    out_l = jnp.concatenate([lru_l, att_l], axis=-1) @ w_out
    if not ctx_out:
        return out_l, None
    att_c = context_sink_attention(q_c.reshape(bsz, c_len, WIN_KV_HEADS, grp, HEAD_DIM), k_c, v_c, sink)
    lru_c = rec_c.astype(hc.dtype) * jax.nn.gelu(gate_c)
    out_c = jnp.concatenate([lru_c, att_c], axis=-1) @ w_out
    return out_l, out_c


def mixer_diff_ssd(hl, hc, w_in, w_out, lam_vecs, subln_g, conv_w, conv_b, dt_bias, a_log, d_skip,
                   norm_g, lam_init, ctx_out):
    bsz, s_len, _ = hl.shape
    c_len = hc.shape[1]
    q_l, z_l, k_l, v_l, xbc_l, dt_l = jnp.split(hl @ w_in, CD_SPLITS, axis=-1)
    if ctx_out:
        q_c, z_c, k_c, v_c, xbc_c, dt_c = jnp.split(hc @ w_in, CD_SPLITS, axis=-1)
    else:
        k_c, v_c, xbc_c, dt_c = jnp.split(hc @ w_in[:, CD_CTX_START:], CD_CTX_SPLITS, axis=-1)
    lv = lam_vecs.astype(F32)
    lam = jnp.exp(jnp.sum(lv[0] * lv[1])) - jnp.exp(jnp.sum(lv[2] * lv[3])) + lam_init
    cos, sin = axial_rope_tables(s_len, DIFF_DH, hl.dtype)

    def rope_qk(t):
        t = t.reshape(bsz, s_len, 2 * DIFF_HEADS, DIFF_DH)
        return apply_rope(t, cos, sin).reshape(bsz, s_len, DIFF_HEADS, 2, DIFF_DH)

    q_l, k_l = rope_qk(q_l), rope_qk(k_l)
    k_c = k_c.reshape(bsz, c_len, DIFF_HEADS, 2, DIFF_DH)
    v_c = v_c.reshape(bsz, c_len, DIFF_HEADS, 2 * DIFF_DH)
    k_all = jnp.concatenate([k_c, k_l], axis=1)
    v_all = jnp.concatenate([v_c, v_l.reshape(bsz, s_len, DIFF_HEADS, 2 * DIFF_DH)], axis=1)
    nb = s_len // QBLK
    q_blocks = jnp.moveaxis(q_l.reshape(bsz, nb, QBLK, DIFF_HEADS, 2, DIFF_DH), 1, 0)
    o_blocks = lax.map(lambda qb: diff_attention(qb, k_all, v_all, lam, lam_init, subln_g), q_blocks)
    diff_l = jnp.moveaxis(o_blocks, 0, 1).reshape(bsz, s_len, CD_V)
    x_l, bm_l, cm_l = ssd_inputs(xbc_l, conv_w, conv_b)
    x_c, bm_c, cm_c = ssd_inputs(xbc_c, conv_w, conv_b)
    dsk = d_skip.astype(F32)[:, None]
    y_l = dsk * x_l
    y_c = dsk * x_c if ctx_out else None
    h_zero = jnp.zeros((bsz, SSD_HEADS, SSD_HEAD_DIM, SSD_STATE), F32)
    for d, rev in enumerate((False, True)):
        a_d = -jnp.exp(a_log[d].astype(F32))
        cols = slice(d * SSD_HEADS, (d + 1) * SSD_HEADS)
        dt_c_d = jax.nn.softplus(dt_c[..., cols].astype(F32) + dt_bias[d].astype(F32))
        dt_l_d = jax.nn.softplus(dt_l[..., cols].astype(F32) + dt_bias[d].astype(F32))
        yc_d, h_c_final = ssd_dir(x_c, dt_c_d, a_d, bm_c, cm_c, h_zero, rev, ctx_out)
        yl_d, _ = ssd_dir(x_l, dt_l_d, a_d, bm_l, cm_l, h_c_final, rev, True)
        y_l = y_l + yl_d
        if ctx_out:
            y_c = y_c + yc_d
    ssd_l = gated_group_norm(y_l, z_l, norm_g)
    out_l = jnp.concatenate([diff_l, ssd_l], axis=-1) @ w_out
    if not ctx_out:
        return out_l, None
    diff_c = diff_attention(q_c.reshape(bsz, c_len, DIFF_HEADS, 2, DIFF_DH), k_c, v_c, lam, lam_init,
                            subln_g).reshape(bsz, c_len, CD_V)
    ssd_c = gated_group_norm(y_c, z_c, norm_g)
    out_c = jnp.concatenate([diff_c, ssd_c], axis=-1) @ w_out
    return out_l, out_c


def moe_ffn(h, w_router, b_router, w_gate, w_up, w_down, ws_gate, ws_up, ws_down):
    n_tok = h.shape[0]
    per = N_EXPERTS // N_EXPERT_GROUPS
    scores = jax.nn.sigmoid((h @ w_router).astype(F32))
    sel = scores + b_router.astype(F32)
    grp_score = jnp.sum(lax.top_k(sel.reshape(n_tok, N_EXPERT_GROUPS, per), 2)[0], axis=-1)
    _, top_grp = lax.top_k(grp_score, TOPK_GROUPS)
    grp_mask = jnp.any(top_grp[..., None] == jnp.arange(N_EXPERT_GROUPS), axis=1)
    sel = jnp.where(jnp.repeat(grp_mask, per, axis=1), sel, -jnp.inf)
    _, idx = lax.top_k(sel, TOP_K)
    gate = jnp.take_along_axis(scores, idx, axis=1)
    gate = gate / jnp.sum(gate, axis=-1, keepdims=True) * ROUTED_SCALE
    n_slots = n_tok * TOP_K
    e_flat = idx.reshape(-1)
    order = jnp.argsort(e_flat)
    e_sorted = e_flat[order]
    tok_sorted = (order // TOP_K).astype(jnp.int32)
    w_sorted = gate.reshape(-1)[order]
    counts = jnp.bincount(e_flat, length=N_EXPERTS)
    padded = (counts + EXPERT_BLK - 1) // EXPERT_BLK * EXPERT_BLK
    start = jnp.cumsum(counts) - counts
    pend = jnp.cumsum(padded)
    pstart = pend - padded
    dest = pstart[e_sorted] + jnp.arange(n_slots) - start[e_sorted]
    n_blocks = -(-n_slots // EXPERT_BLK) + N_EXPERTS
    cap = n_blocks * EXPERT_BLK
    buf_tok = jnp.zeros((cap,), jnp.int32).at[dest].set(tok_sorted)
    buf_w = jnp.zeros((cap,), F32).at[dest].set(w_sorted)
    blk_e = jnp.minimum(jnp.searchsorted(pend, jnp.arange(n_blocks) * EXPERT_BLK, side='right'), N_EXPERTS - 1)

    def body(acc, blk):
        e, tok, w = blk
        xb = h[tok]
        y = (jax.nn.silu(xb @ w_gate[e]) * (xb @ w_up[e])) @ w_down[e]
        return acc.at[tok].add(y * w[:, None].astype(y.dtype)), None

    routed, _ = lax.scan(body, jnp.zeros_like(h),
                         (blk_e, buf_tok.reshape(n_blocks, EXPERT_BLK), buf_w.reshape(n_blocks, EXPERT_BLK)))
    shared = (jax.nn.silu(h @ ws_gate) * (h @ ws_up)) @ ws_down
    return shared + routed


def setup_inputs(seed: int = 0) -> dict:
    key = jax.random.key(seed)
    keys = iter(jax.random.split(key, 64))

    def nrm(shape, scale):
        return jax.random.normal(next(keys), shape, F32) * scale

    def unif(shape, lo, hi):
        return jax.random.uniform(next(keys), shape, F32, lo, hi)

    dm = D_MODEL
    a0 = unif((N_EVEN, 2, LRU_WIDTH), 0.9, 0.999) ** (1.0 / LRU_C)
    dt0 = jnp.exp(unif((N_ODD, 2, SSD_HEADS), math.log(1e-3), math.log(1e-1)))
    return {
        'x': nrm((BATCH, SEQ, dm), 1.0),
        'c': nrm((BATCH, dm), 1.0),
        'ctx': nrm((BATCH, CTX_LEN, dm), 1.0),
        'c_ctx': nrm((dm,), 1.0),
        'w_mod': nrm((DEPTH, dm, N_MOD * dm), 0.5 * dm ** -0.5),
        'b_mod': nrm((DEPTH, N_MOD * dm), 0.02),
        'g_mix': 1.0 + nrm((DEPTH, dm), 0.02),
        'g_ffn': 1.0 + nrm((DEPTH, dm), 0.02),
        'g_final': 1.0 + nrm((dm,), 0.02),
        'ab_w_in': nrm((N_EVEN, dm, AB_IN), dm ** -0.5),
        'ab_w_out': nrm((N_EVEN, AB_OUT, dm), AB_OUT ** -0.5),
        'ab_conv_w': nrm((N_EVEN, CONV_W, LRU_WIDTH), CONV_W ** -0.5),
        'ab_conv_b': nrm((N_EVEN, LRU_WIDTH), 0.02),
        'ab_w_r': nrm((N_EVEN, 2, LRU_BLOCKS, LRU_BLOCK, LRU_BLOCK), LRU_BLOCK ** -0.5),
        'ab_b_r': nrm((N_EVEN, 2, LRU_WIDTH), 0.02),
        'ab_w_i': nrm((N_EVEN, 2, LRU_BLOCKS, LRU_BLOCK, LRU_BLOCK), LRU_BLOCK ** -0.5),
        'ab_b_i': nrm((N_EVEN, 2, LRU_WIDTH), 0.02),
        'ab_lam': jnp.log(a0) - jnp.log1p(-a0),
        'ab_sink': nrm((N_EVEN, WIN_HEADS), 0.5),
        'cd_w_in': nrm((N_ODD, dm, CD_IN), dm ** -0.5),
        'cd_w_out': nrm((N_ODD, CD_OUT, dm), CD_OUT ** -0.5),
        'cd_lam': nrm((N_ODD, 4, DIFF_DH), 0.1),
        'cd_subln_g': 1.0 + nrm((N_ODD, 2 * DIFF_DH), 0.02),
        'cd_conv_w': nrm((N_ODD, CONV_W, SSD_CONV_DIM), CONV_W ** -0.5),
        'cd_conv_b': nrm((N_ODD, SSD_CONV_DIM), 0.02),
        'cd_dt_bias': dt0 + jnp.log(-jnp.expm1(-dt0)),
        'cd_a_log': jnp.log(unif((N_ODD, 2, SSD_HEADS), 1.0, 16.0)),
        'cd_d_skip': 1.0 + nrm((N_ODD, SSD_HEADS), 0.1),
        'cd_norm_g': 1.0 + nrm((N_ODD, SSD_INNER), 0.02),
        'w_router': nrm((DEPTH, dm, N_EXPERTS), dm ** -0.5),
        'b_router': nrm((DEPTH, N_EXPERTS), 0.01),
        'w_e_gate': nrm((DEPTH, N_EXPERTS, dm, D_EXPERT), dm ** -0.5),
        'w_e_up': nrm((DEPTH, N_EXPERTS, dm, D_EXPERT), dm ** -0.5),
        'w_e_down': nrm((DEPTH, N_EXPERTS, D_EXPERT, dm), D_EXPERT ** -0.5),
        'ws_gate': nrm((DEPTH, dm, D_SHARED), dm ** -0.5),
        'ws_up': nrm((DEPTH, dm, D_SHARED), dm ** -0.5),
        'ws_down': nrm((DEPTH, D_SHARED, dm), D_SHARED ** -0.5),
    }


def reference(x, c, ctx, c_ctx, w_mod, b_mod, g_mix, g_ffn, g_final,
              ab_w_in, ab_w_out, ab_conv_w, ab_conv_b, ab_w_r, ab_b_r, ab_w_i, ab_b_i, ab_lam, ab_sink,
              cd_w_in, cd_w_out, cd_lam, cd_subln_g, cd_conv_w, cd_conv_b, cd_dt_bias, cd_a_log, cd_d_skip,
              cd_norm_g, w_router, b_router, w_e_gate, w_e_up, w_e_down, ws_gate, ws_up, ws_down):
    bsz, s_len, dm = x.shape
    c_len = ctx.shape[1]
    cs = jax.nn.silu(c)
    ccs = jax.nn.silu(c_ctx)
    for l in range(DEPTH):
        last = l == DEPTH - 1
        i = l // 2
        ml = (cs @ w_mod[l] + b_mod[l])[:, None, :]
        mc = ccs @ w_mod[l] + b_mod[l]
        sh1, sc1, g1, sh2, sc2, g2 = jnp.split(ml, N_MOD, axis=-1)
        csh1, csc1, cg1, csh2, csc2, cg2 = jnp.split(mc, N_MOD, axis=-1)
        hl = modulate(rms_norm(x, g_mix[l]), sh1, sc1)
        hc = modulate(rms_norm(ctx, g_mix[l]), csh1, csc1)
        if l % 2 == 0:
            ol, oc = mixer_rglru_window(hl, hc, ab_w_in[i], ab_w_out[i], ab_conv_w[i], ab_conv_b[i], ab_w_r[i],
                                        ab_b_r[i], ab_w_i[i], ab_b_i[i], ab_lam[i], ab_sink[i], not last)
        else:
            lam_init = 0.8 - 0.6 * math.exp(-0.3 * l)
            ol, oc = mixer_diff_ssd(hl, hc, cd_w_in[i], cd_w_out[i], cd_lam[i], cd_subln_g[i], cd_conv_w[i],
                                    cd_conv_b[i], cd_dt_bias[i], cd_a_log[i], cd_d_skip[i], cd_norm_g[i],
                                    lam_init, not last)
        x = x + g1 * ol
        hl = modulate(rms_norm(x, g_ffn[l]), sh2, sc2).reshape(bsz * s_len, dm)
        moe_w = (w_router[l], b_router[l], w_e_gate[l], w_e_up[l], w_e_down[l], ws_gate[l], ws_up[l], ws_down[l])
        if last:
            x = x + g2 * moe_ffn(hl, *moe_w).reshape(bsz, s_len, dm)
        else:
            ctx = ctx + cg1 * oc
            hc = modulate(rms_norm(ctx, g_ffn[l]), csh2, csc2).reshape(bsz * c_len, dm)
            f = moe_ffn(jnp.concatenate([hl, hc], axis=0), *moe_w)
            x = x + g2 * f[:bsz * s_len].reshape(bsz, s_len, dm)
            ctx = ctx + cg2 * f[bsz * s_len:].reshape(bsz, c_len, dm)
    return rms_norm(x, g_final)
```

```python
import functools
import math

import jax
import jax.numpy as jnp
from jax import lax
from jax.experimental import pallas as pl
from jax.experimental.pallas import tpu as pltpu

F32 = jnp.float32
BF16 = jnp.bfloat16
HIGHEST = lax.Precision.HIGHEST

GRID_W = 64
N_MOD = 6
NORM_EPS = 1e-6
ROPE_BASE = 10000.0
CONV_W = 4

LRU_WIDTH = 512
LRU_BLOCKS = 8
LRU_C = 8.0

HEAD_DIM = 64
WIN_HEADS = 8
WIN_KV_HEADS = 2
WINDOW = 128

DIFF_HEADS = 4
DIFF_DH = 64

SSD_HEADS = 8
SSD_HEAD_DIM = 64
SSD_INNER = SSD_HEADS * SSD_HEAD_DIM
SSD_GROUPS = 2
SSD_STATE = 128
SSD_CONV_DIM = SSD_INNER + 2 * SSD_GROUPS * SSD_STATE

N_EXPERTS = 64
N_EXPERT_GROUPS = 8
TOPK_GROUPS = 4
TOP_K = 8
D_EXPERT = 256
ROUTED_SCALE = 2.5

LANES = 128
SUBLANES = 8
MOD_ROWS = 16
TIME_TILE = 128
ROW_TILE = 256
MOE_TILE = 1024
VMEM_LIMIT = 48 * 1024 * 1024
NEG_BIG = -1e30


def _cparams(sem):
    return pltpu.CompilerParams(dimension_semantics=sem, vmem_limit_bytes=VMEM_LIMIT)


def _nt_dot(a, b):
    return lax.dot_general(a, b, (((1,), (1,)), ((), ())), preferred_element_type=F32)


def _softplus(x):
    return jnp.maximum(x, 0.0) + jnp.log1p(jnp.exp(-jnp.abs(x)))


def _silu(x):
    return x * jax.nn.sigmoid(x)


def _mod_kernel(c_ref, w_ref, b_ref, o_ref):
    c = c_ref[...]
    s = _silu(c)
    o_ref[...] = jnp.dot(s, w_ref[...], preferred_element_type=F32, precision=HIGHEST) + b_ref[...]


def _modulations(c_all, w_mod, b_mod):
    depth, d, _ = w_mod.shape
    return pl.pallas_call(
        _mod_kernel,
        grid=(depth, N_MOD),
        in_specs=[
            pl.BlockSpec((MOD_ROWS, d), lambda l, k: (0, 0)),
            pl.BlockSpec((None, d, d), lambda l, k: (l, 0, k)),
            pl.BlockSpec((None, 1, d), lambda l, k: (l, 0, k)),
        ],
        out_specs=pl.BlockSpec((None, MOD_ROWS, d), lambda l, k: (l, 0, k)),
        out_shape=jax.ShapeDtypeStruct((depth, MOD_ROWS, N_MOD * d), F32),
        compiler_params=_cparams(("arbitrary", "arbitrary")),
        name="adaln_modulation",
    )(c_all, w_mod, b_mod.reshape(depth, 1, N_MOD * d))


def _mod_spec(d, layer, chunk, row_fn):
    return pl.BlockSpec((None, 1, d), lambda b, i: (layer * MOD_ROWS + row_fn(b, i), 0, chunk))


def _norm_mod(x, g, sh, sc):
    ms = jnp.mean(x * x, axis=-1, keepdims=True)
    return (x * lax.rsqrt(ms + NORM_EPS) * g) * (1.0 + sc) + sh


def _rope(y, cos, sa, sb):
    n = y.shape[-1]
    half = HEAD_DIM // 2
    return y * cos + pltpu.roll(y, n - half, 1) * sa + pltpu.roll(y, half, 1) * sb


def _proj_kernel(*refs, ropes):
    n = len(ropes)
    x_ref, g_ref, sh_ref, sc_ref, cos_ref, sa_ref, sb_ref = refs[:7]
    w_refs = refs[7:7 + n]
    o_refs = refs[7 + n:]
    h = _norm_mod(x_ref[...], g_ref[...], sh_ref[...], sc_ref[...]).astype(BF16)
    for w_ref, o_ref, rope in zip(w_refs, o_refs, ropes):
        y = jnp.dot(h, w_ref[...], preferred_element_type=F32)
        if rope:
            w = y.shape[-1]
            y = _rope(y, cos_ref[:, :w], sa_ref[:, :w], sb_ref[:, :w])
        o_ref[...] = y.astype(o_ref.dtype)


def _project(xc, g, mods3, layer, nct, rope_tabs, groups):
    b, l, d = xc.shape
    tm = ROW_TILE
    row = lambda bb, i: jnp.where(i < nct, SUBLANES, bb)
    rw = rope_tabs[0].shape[-1]
    in_specs = [
        pl.BlockSpec((None, tm, d), lambda bb, i: (bb, i, 0)),
        pl.BlockSpec((1, d), lambda bb, i: (0, 0)),
        _mod_spec(d, layer, 0, row),
        _mod_spec(d, layer, 1, row),
    ] + [pl.BlockSpec((tm, rw), lambda bb, i: (i, 0))] * 3
    out_specs, out_shapes = [], []
    for w, _, dt, time_major in groups:
        n = w.shape[1]
        in_specs.append(pl.BlockSpec((d, n), lambda bb, i: (0, 0)))
        if time_major:
            out_specs.append(pl.BlockSpec((tm, n), lambda bb, i: (i, bb)))
            out_shapes.append(jax.ShapeDtypeStruct((l, b * n), dt))
        else:
            out_specs.append(pl.BlockSpec((None, tm, n), lambda bb, i: (bb, i, 0)))
            out_shapes.append(jax.ShapeDtypeStruct((b, l, n), dt))
    return pl.pallas_call(
        functools.partial(_proj_kernel, ropes=tuple(gp[1] for gp in groups)),
        grid=(b, l // tm),
        in_specs=in_specs,
        out_specs=out_specs,
        out_shape=out_shapes,
        compiler_params=_cparams(("arbitrary", "arbitrary")),
        name="norm_mod_project",
    )(xc, g.reshape(1, d), mods3, mods3, *rope_tabs, *[gp[0] for gp in groups])


def _rope_tables(c_len, s_len):
    rows = s_len // GRID_W
    row = jnp.repeat(jnp.arange(rows), GRID_W).astype(F32)
    col = jnp.tile(jnp.arange(GRID_W), rows).astype(F32)
    n = HEAD_DIM // 4
    inv = ROPE_BASE ** (-jnp.arange(n, dtype=F32) / n)
    ang = jnp.concatenate([row[:, None] * inv, col[:, None] * inv], axis=-1)
    cos, sin = jnp.cos(ang), jnp.sin(ang)
    zero = jnp.zeros_like(sin)
    reps = WIN_HEADS
    cos_t = jnp.tile(jnp.concatenate([cos, cos], axis=-1), (1, reps))
    sa_t = jnp.tile(jnp.concatenate([-sin, zero], axis=-1), (1, reps))
    sb_t = jnp.tile(jnp.concatenate([zero, sin], axis=-1), (1, reps))
    w = cos_t.shape[-1]
    pad1 = jnp.ones((c_len, w), F32)
    pad0 = jnp.zeros((c_len, w), F32)
    return (jnp.concatenate([pad1, cos_t], 0), jnp.concatenate([pad0, sa_t], 0),
            jnp.concatenate([pad0, sb_t], 0))


def _seq_tile(d, g, nct, nt):
    rev = jnp.where(g < nct, nct - 1 - g, nt - 1 - (g - nct))
    return jnp.where(d == 0, g, rev)


def _rglru_kernel(x_ref, xp_ref, xn_ref, cw_ref, cb_ref, w_ref, bias_ref, lam_ref, o_ref,
                  ext_scr, a_scr, b_scr, h_scr, *, ts, nct, nt, sub):
    d = pl.program_id(0)
    g = pl.program_id(1)
    tile = _seq_tile(d, g, nct, nt)
    bsz, width = h_scr.shape
    pv = jnp.where((tile == 0) | (tile == nct), 0.0, 1.0)
    nv = jnp.where((tile == nct - 1) | (tile == nt - 1), 0.0, 1.0)
    ext_scr[0:1] = xp_ref[...] * pv
    ext_scr[1:ts + 1] = x_ref[...]
    ext_scr[ts + 1:ts + 3] = xn_ref[...] * nv

    @pl.when(g == 0)
    def _():
        h_scr[...] = jnp.zeros_like(h_scr)

    neg_sp = -LRU_C * _softplus(-lam_ref[...])

    def prep(c, carry):
        r0 = pl.multiple_of(c * sub, sub)
        e = ext_scr[pl.ds(r0, sub + CONV_W - 1)]
        u = cb_ref[...] + cw_ref[0] * e[0:sub]
        for j in range(1, CONV_W):
            u = u + cw_ref[j] * e[j:j + sub]
        u2 = u.reshape(sub * bsz, width)
        gts = jnp.dot(u2.astype(BF16), w_ref[...], preferred_element_type=F32) + bias_ref[...]
        r = jax.nn.sigmoid(gts[:, :width])
        ig = jax.nn.sigmoid(gts[:, width:])
        log_a = neg_sp * r
        a = jnp.exp(log_a)
        mult = jnp.sqrt(1.0 - a * a)
        a_scr[pl.ds(r0, sub)] = a.reshape(sub, bsz, width)
        b_scr[pl.ds(r0, sub)] = (mult * ig * u2).reshape(sub, bsz, width)
        return carry

    lax.fori_loop(0, ts // sub, prep, 0)

    def step(t, h):
        tt = jnp.where(d == 0, t, ts - 1 - t)
        h = a_scr[tt] * h + b_scr[tt]
        o_ref[tt] = h
        return h

    h_scr[...] = lax.fori_loop(0, ts, step, h_scr[...], unroll=8)


def _rglru(xa_tm, conv_w, conv_b, w_gates, b_gates, lam, nct):
    l, bsz, width = xa_tm.shape
    ts = TIME_TILE
    nt = l // ts
    tile = lambda d, g: _seq_tile(d, g, nct, nt)
    kern = functools.partial(_rglru_kernel, ts=ts, nct=nct, nt=nt, sub=16)
    return pl.pallas_call(
        kern,
        grid=(2, nt),
        in_specs=[
            pl.BlockSpec((ts, bsz, width), lambda d, g: (tile(d, g), 0, 0)),
            pl.BlockSpec((1, bsz, width), lambda d, g: (jnp.maximum(tile(d, g) * ts - 1, 0), 0, 0)),
            pl.BlockSpec((2, bsz, width),
                         lambda d, g: (jnp.minimum((tile(d, g) + 1) * (ts // 2), l // 2 - 1), 0, 0)),
            pl.BlockSpec((CONV_W, 1, width), lambda d, g: (0, 0, 0)),
            pl.BlockSpec((1, width), lambda d, g: (0, 0)),
            pl.BlockSpec((None, width, 2 * width), lambda d, g: (d, 0, 0)),
            pl.BlockSpec((None, 1, 2 * width), lambda d, g: (d, 0, 0)),
            pl.BlockSpec((None, 1, width), lambda d, g: (d, 0, 0)),
        ],
        out_specs=pl.BlockSpec((None, ts, bsz, width), lambda d, g: (d, tile(d, g), 0, 0)),
        out_shape=jax.ShapeDtypeStruct((2, l, bsz, width), F32),
        scratch_shapes=[
            pltpu.VMEM((ts + CONV_W - 1, bsz, width), F32),
            pltpu.VMEM((ts, bsz, width), F32),
            pltpu.VMEM((ts, bsz, width), F32),
            pltpu.VMEM((bsz, width), F32),
        ],
        compiler_params=_cparams(("arbitrary", "arbitrary")),
        name="rglru_scan",
    )(xa_tm, xa_tm, xa_tm, conv_w.reshape(CONV_W, 1, width), conv_b.reshape(1, width),
      w_gates, b_gates, lam)


def _block_diag(w):
    nb, c, dd = w.shape
    eye = jnp.eye(nb, dtype=w.dtype)
    return (eye[:, None, :, None] * w[:, :, None, :]).reshape(nb * c, nb * dd)


def _win_attn_kernel(sink_ref, q_ref, k_ref, v_ref, o_ref, *, c_len, l_len, nqc):
    j = pl.program_id(1)
    blk = q_ref.shape[0]
    grp = WIN_HEADS // WIN_KV_HEADS
    band = 3 * blk

    def heads(body):
        for hk in range(WIN_KV_HEADS):
            ksl = slice(hk * HEAD_DIM, (hk + 1) * HEAD_DIM)
            for gq in range(grp):
                head = hk * grp + gq
                hsl = slice(head * HEAD_DIM, (head + 1) * HEAD_DIM)
                o_ref[:, hsl] = body(q_ref[:, hsl], ksl, sink_ref[head]).astype(o_ref.dtype)

    @pl.when(j < nqc)
    def _():
        def body(qh, ksl, sink):
            s = _nt_dot(qh, k_ref[0:c_len, ksl]) * (HEAD_DIM ** -0.5)
            m = jnp.maximum(jnp.max(s, axis=-1, keepdims=True), sink)
            p = jnp.exp(s - m)
            den = jnp.sum(p, axis=-1, keepdims=True) + jnp.exp(sink - m)
            o = jnp.dot(p.astype(BF16), v_ref[0:c_len, ksl], preferred_element_type=F32)
            return o / den
        heads(body)

    @pl.when(j >= nqc)
    def _():
        jb = j - nqc
        start = jnp.clip(c_len + (jb - 1) * blk, c_len - blk, l_len - band)
        start = pl.multiple_of(start, blk)
        qpos = jb * blk + lax.broadcasted_iota(jnp.int32, (blk, band), 0)
        kpos = start - c_len + lax.broadcasted_iota(jnp.int32, (blk, band), 1)
        valid = (jnp.abs(qpos - kpos) <= WINDOW) & (kpos >= 0)

        def body(qh, ksl, sink):
            sc = _nt_dot(qh, k_ref[0:c_len, ksl]) * (HEAD_DIM ** -0.5)
            sb = _nt_dot(qh, k_ref[pl.ds(start, band), ksl]) * (HEAD_DIM ** -0.5)
            sb = jnp.where(valid, sb, NEG_BIG)
            m = jnp.maximum(jnp.maximum(jnp.max(sc, axis=-1, keepdims=True),
                                        jnp.max(sb, axis=-1, keepdims=True)), sink)
            pc = jnp.exp(sc - m)
            pb = jnp.exp(sb - m)
            den = (jnp.sum(pc, axis=-1, keepdims=True) + jnp.sum(pb, axis=-1, keepdims=True)
                   + jnp.exp(sink - m))
            o = (jnp.dot(pc.astype(BF16), v_ref[0:c_len, ksl], preferred_element_type=F32)
                 + jnp.dot(pb.astype(BF16), v_ref[pl.ds(start, band), ksl], preferred_element_type=F32))
            return o / den
        heads(body)


def _win_attention(q, k, v, sink, c_len):
    b, l, qw = q.shape
    kw = k.shape[-1]
    blk = TIME_TILE
    kern = functools.partial(_win_attn_kernel, c_len=c_len, l_len=l, nqc=c_len // blk)
    return pl.pallas_call(
        kern,
        grid=(b, l // blk),
        in_specs=[
            pl.BlockSpec(memory_space=pltpu.SMEM),
            pl.BlockSpec((None, blk, qw), lambda bb, j: (bb, j, 0)),
            pl.BlockSpec((None, l, kw), lambda bb, j: (bb, 0, 0)),
            pl.BlockSpec((None, l, kw), lambda bb, j: (bb, 0, 0)),
        ],
        out_specs=pl.BlockSpec((None, blk, qw), lambda bb, j: (bb, j, 0)),
        out_shape=jax.ShapeDtypeStruct((b, l, qw), BF16),
        compiler_params=_cparams(("arbitrary", "arbitrary")),
        name="window_attention",
    )(sink, q, k, v)


def _out_even_kernel(x_ref, rec_ref, gate_ref, att_ref, wa_ref, wb_ref, g1_ref, o_ref):
    lru = (rec_ref[0] + rec_ref[1]) * jax.nn.gelu(gate_ref[...])
    y = (jnp.dot(lru.astype(BF16), wa_ref[...], preferred_element_type=F32)
         + jnp.dot(att_ref[...], wb_ref[...], preferred_element_type=F32))
    o_ref[...] = x_ref[...] + g1_ref[...] * y


def _out_even(xc, rec2, gate, att, w_a, w_b, mods3, layer, nct):
    b, l, d = xc.shape
    tm = ROW_TILE
    w = gate.shape[-1]
    row = lambda bb, i: jnp.where(i < nct, SUBLANES, bb)
    return pl.pallas_call(
        _out_even_kernel,
        grid=(b, l // tm),
        in_specs=[
            pl.BlockSpec((None, tm, d), lambda bb, i: (bb, i, 0)),
            pl.BlockSpec((2, tm, w), lambda bb, i: (0, i, bb)),
            pl.BlockSpec((None, tm, w), lambda bb, i: (bb, i, 0)),
            pl.BlockSpec((None, tm, att.shape[-1]), lambda bb, i: (bb, i, 0)),
            pl.BlockSpec(w_a.shape, lambda bb, i: (0, 0)),
            pl.BlockSpec(w_b.shape, lambda bb, i: (0, 0)),
            _mod_spec(d, layer, 2, row),
        ],
        out_specs=pl.BlockSpec((None, tm, d), lambda bb, i: (bb, i, 0)),
        out_shape=jax.ShapeDtypeStruct((b, l, d), F32),
        compiler_params=_cparams(("arbitrary", "arbitrary")),
        name="out_proj_even",
    )(xc, rec2, gate, att, w_a, w_b, mods3)


def _diff_attn_kernel(lam_ref, g_ref, q_ref, k_ref, v_ref, o_ref, *, lam_init):
    lv = lam_ref[...]
    lam = (jnp.exp(jnp.sum(lv[0:1] * lv[1:2], axis=-1, keepdims=True))
           - jnp.exp(jnp.sum(lv[2:3] * lv[3:4], axis=-1, keepdims=True)) + lam_init)
    vw = 2 * DIFF_DH

    def softmax(qm, km):
        s = _nt_dot(qm, km) * (DIFF_DH ** -0.5)
        m = jnp.max(s, axis=-1, keepdims=True)
        p = jnp.exp(s - m)
        return p / jnp.sum(p, axis=-1, keepdims=True)

    for h in range(DIFF_HEADS):
        lo = h * vw
        p0 = softmax(q_ref[:, lo:lo + DIFF_DH], k_ref[:, lo:lo + DIFF_DH])
        p1 = softmax(q_ref[:, lo + DIFF_DH:lo + vw], k_ref[:, lo + DIFF_DH:lo + vw])
        w = (p0 - lam * p1).astype(BF16)
        o = jnp.dot(w, v_ref[:, lo:lo + vw], preferred_element_type=F32)
        ms = jnp.mean(o * o, axis=-1, keepdims=True)
        o = o * lax.rsqrt(ms + NORM_EPS) * g_ref[...]
        o_ref[:, lo:lo + vw] = (o * (1.0 - lam_init)).astype(o_ref.dtype)


def _diff_attention(q, k, v, lam_vecs, subln_g, lam_init, c_len):
    b, l, w = q.shape
    tq = TIME_TILE
    s_len = l - c_len
    off = c_len // tq
    return pl.pallas_call(
        functools.partial(_diff_attn_kernel, lam_init=lam_init),
        grid=(b, s_len // tq),
        in_specs=[
            pl.BlockSpec(lam_vecs.shape, lambda bb, j: (0, 0)),
            pl.BlockSpec((1, 2 * DIFF_DH), lambda bb, j: (0, 0)),
            pl.BlockSpec((None, tq, w), lambda bb, j: (bb, j + off, 0)),
            pl.BlockSpec((None, l, w), lambda bb, j: (bb, 0, 0)),
            pl.BlockSpec((None, l, w), lambda bb, j: (bb, 0, 0)),
        ],
        out_specs=pl.BlockSpec((None, tq, w), lambda bb, j: (bb, j, 0)),
        out_shape=jax.ShapeDtypeStruct((b, s_len, w), BF16),
        compiler_params=_cparams(("arbitrary", "arbitrary")),
        name="diff_attention",
    )(lam_vecs, subln_g.reshape(1, -1), q, k, v)


def _ssd_kernel(x_ref, xp_ref, xn_ref, dt_ref, cw_ref, cb_ref, dtb_ref, alog_ref, dsk_ref, o_ref,
                ext_scr, st_scr, *, q, nct, nt):
    d = pl.program_id(0)
    g = pl.program_id(2)
    tile = _seq_tile(d, g, nct, nt)
    pv = jnp.where((tile == 0) | (tile == nct), 0.0, 1.0)
    nv = jnp.where((tile == nct - 1) | (tile == nt - 1), 0.0, 1.0)
    ext_scr[0:SUBLANES] = xp_ref[...] * pv
    ext_scr[SUBLANES:SUBLANES + q] = x_ref[...]
    ext_scr[SUBLANES + q:2 * SUBLANES + q] = xn_ref[...] * nv

    @pl.when(g == 0)
    def _():
        st_scr[...] = jnp.zeros_like(st_scr)

    u = cb_ref[...] + cw_ref[0] * ext_scr[SUBLANES - 1:SUBLANES - 1 + q, :]
    for j in range(1, CONV_W):
        u = u + cw_ref[j] * ext_scr[SUBLANES - 1 + j:SUBLANES - 1 + j + q, :]
    act = _silu(u)

    dtr = dt_ref[...]
    dtr = jnp.where(d == 0, dtr, pltpu.roll(dtr, LANES - SSD_HEADS, 1))
    dtv = _softplus(dtr + dtb_ref[...])
    head_lane = lax.broadcasted_iota(jnp.int32, (1, LANES), 1) < SSD_HEADS
    dta = dtv * jnp.where(head_lane, -jnp.exp(alog_ref[...]), 0.0)
    ri = lax.broadcasted_iota(jnp.int32, (q, q), 0)
    ci = lax.broadcasted_iota(jnp.int32, (q, q), 1)
    keep = jnp.where(d == 0, ri - ci, ci - ri) >= 0
    cum = jnp.dot(keep.astype(F32), dta, preferred_element_type=F32, precision=HIGHEST)
    tot = jnp.sum(dta, axis=0, keepdims=True)
    cum_t = cum.T
    dt_t = dtv.T
    to_end = jnp.exp(tot - cum) * dtv
    e_cum = jnp.exp(cum)
    e_tot = jnp.exp(tot)
    dskip = dsk_ref[...] * jnp.where(d == 0, 1.0, 0.0)

    hpg = SSD_HEADS // SSD_GROUPS
    for gi in range(SSD_GROUPS):
        b_g = act[:, SSD_INNER + gi * SSD_STATE:SSD_INNER + (gi + 1) * SSD_STATE]
        c_lo = SSD_INNER + SSD_GROUPS * SSD_STATE + gi * SSD_STATE
        c_g = act[:, c_lo:c_lo + SSD_STATE].astype(BF16)
        cb = _nt_dot(c_g, b_g.astype(BF16))
        b_gt = b_g.T.astype(BF16)
        for hh in range(hpg):
            h = gi * hpg + hh
            xs = act[:, h * SSD_HEAD_DIM:(h + 1) * SSD_HEAD_DIM]
            seg = cum[:, h:h + 1] - cum_t[h:h + 1, :]
            decay = jnp.exp(jnp.where(keep, seg, NEG_BIG))
            w = (cb * decay * dt_t[h:h + 1, :]).astype(BF16)
            state = st_scr[h]
            y = jnp.dot(w, xs.astype(BF16), preferred_element_type=F32)
            y = y + jnp.dot(c_g, state.astype(BF16), preferred_element_type=F32) * e_cum[:, h:h + 1]
            y = y + dskip[:, h * SSD_HEAD_DIM:(h + 1) * SSD_HEAD_DIM] * xs
            o_ref[:, h * SSD_HEAD_DIM:(h + 1) * SSD_HEAD_DIM] = y
            s_new = jnp.dot(b_gt, (xs * to_end[:, h:h + 1]).astype(BF16), preferred_element_type=F32)
            st_scr[h] = e_tot[:, h:h + 1] * state + s_new


def _ssd(xbc, dt, conv_w, conv_b, dt_bias, a_log, d_skip, nct):
    b, l, cd = xbc.shape
    q = TIME_TILE
    nt = l // q
    tile = lambda d, bb, g: _seq_tile(d, g, nct, nt)
    r8 = q // SUBLANES
    pad = LANES - SSD_HEADS
    dtb = jnp.pad(dt_bias, ((0, 0), (0, pad))).reshape(2, 1, LANES)
    alog = jnp.pad(a_log, ((0, 0), (0, pad))).reshape(2, 1, LANES)
    dsk = jnp.repeat(d_skip, SSD_HEAD_DIM).reshape(1, SSD_INNER)
    return pl.pallas_call(
        functools.partial(_ssd_kernel, q=q, nct=nct, nt=nt),
        grid=(2, b, nt),
        in_specs=[
            pl.BlockSpec((None, q, cd), lambda d, bb, g: (bb, tile(d, bb, g), 0)),
            pl.BlockSpec((None, SUBLANES, cd),
                         lambda d, bb, g: (bb, jnp.maximum(tile(d, bb, g) * r8 - 1, 0), 0)),
            pl.BlockSpec((None, SUBLANES, cd),
                         lambda d, bb, g: (bb, jnp.minimum((tile(d, bb, g) + 1) * r8, l // SUBLANES - 1), 0)),
            pl.BlockSpec((None, q, LANES), lambda d, bb, g: (bb, tile(d, bb, g), 0)),
            pl.BlockSpec((CONV_W, 1, cd), lambda d, bb, g: (0, 0, 0)),
            pl.BlockSpec((1, cd), lambda d, bb, g: (0, 0)),
            pl.BlockSpec((None, 1, LANES), lambda d, bb, g: (d, 0, 0)),
            pl.BlockSpec((None, 1, LANES), lambda d, bb, g: (d, 0, 0)),
            pl.BlockSpec((1, SSD_INNER), lambda d, bb, g: (0, 0)),
        ],
        out_specs=pl.BlockSpec((None, None, q, SSD_INNER), lambda d, bb, g: (d, bb, tile(d, bb, g), 0)),
        out_shape=jax.ShapeDtypeStruct((2, b, l, SSD_INNER), F32),
        scratch_shapes=[
            pltpu.VMEM((q + 2 * SUBLANES, cd), F32),
            pltpu.VMEM((SSD_HEADS, SSD_STATE, SSD_HEAD_DIM), F32),
        ],
        compiler_params=_cparams(("arbitrary", "arbitrary", "arbitrary")),
        name="ssd_chunked",
    )(xbc, xbc, xbc, dt, conv_w.reshape(CONV_W, 1, cd), conv_b.reshape(1, cd), dtb, alog, dsk)


def _out_odd_kernel(x_ref, diff_ref, y_ref, z_ref, ng_ref, wa_ref, wb_ref, g1_ref, o_ref):
    yz = (y_ref[0] + y_ref[1]) * _silu(z_ref[...])
    gs = SSD_INNER // SSD_GROUPS
    parts = []
    for gi in range(SSD_GROUPS):
        seg = yz[:, gi * gs:(gi + 1) * gs]
        ms = jnp.mean(seg * seg, axis=-1, keepdims=True)
        parts.append(seg * lax.rsqrt(ms + NORM_EPS) * ng_ref[:, gi * gs:(gi + 1) * gs])
    ssd = jnp.concatenate(parts, axis=-1).astype(BF16)
    y = (jnp.dot(diff_ref[...], wa_ref[...], preferred_element_type=F32)
         + jnp.dot(ssd, wb_ref[...], preferred_element_type=F32))
    o_ref[...] = x_ref[...] + g1_ref[...] * y


def _out_odd(xc, diff, y2, z, norm_g, w_a, w_b, mods3, layer, c_len):
    b, l, d = xc.shape
    s_len = l - c_len
    tm = ROW_TILE
    off = c_len // tm
    w = SSD_INNER
    row = lambda bb, i: bb
    return pl.pallas_call(
        _out_odd_kernel,
        grid=(b, s_len // tm),
        in_specs=[
            pl.BlockSpec((None, tm, d), lambda bb, i: (bb, i + off, 0)),
            pl.BlockSpec((None, tm, diff.shape[-1]), lambda bb, i: (bb, i, 0)),
            pl.BlockSpec((2, None, tm, w), lambda bb, i: (0, bb, i + off, 0)),
            pl.BlockSpec((None, tm, w), lambda bb, i: (bb, i + off, 0)),
            pl.BlockSpec((1, w), lambda bb, i: (0, 0)),
            pl.BlockSpec(w_a.shape, lambda bb, i: (0, 0)),
            pl.BlockSpec(w_b.shape, lambda bb, i: (0, 0)),
            _mod_spec(d, layer, 2, row),
        ],
        out_specs=pl.BlockSpec((None, tm, d), lambda bb, i: (bb, i, 0)),
        out_shape=jax.ShapeDtypeStruct((b, s_len, d), F32),
        compiler_params=_cparams(("arbitrary", "arbitrary")),
        name="out_proj_odd",
    )(xc, diff, y2, z, norm_g.reshape(1, w), w_a, w_b, mods3)


def _router_kernel(x_ref, g_ref, sh_ref, sc_ref, wr_ref, br_ref, h_ref, gate_ref):
    h = _norm_mod(x_ref[...], g_ref[...], sh_ref[...], sc_ref[...])
    h_ref[...] = h.astype(h_ref.dtype)
    tm = h.shape[0]
    per = N_EXPERTS // N_EXPERT_GROUPS
    logits = lax.dot_general(wr_ref[...], h, (((1,), (1,)), ((), ())),
                             preferred_element_type=F32, precision=HIGHEST)
    scores = jax.nn.sigmoid(logits)
    sel = scores + br_ref[...]
    sel3 = sel.reshape(N_EXPERT_GROUPS, per, tm)
    kio = lax.broadcasted_iota(jnp.int32, sel3.shape, 1)
    m1 = jnp.max(sel3, axis=1, keepdims=True)
    first = jnp.min(jnp.where(sel3 == m1, kio, per), axis=1, keepdims=True)
    m2 = jnp.max(jnp.where(kio == first, NEG_BIG, sel3), axis=1, keepdims=True)
    gs = m1 + m2
    gio = lax.broadcasted_iota(jnp.int32, gs.shape, 0)
    ahead = jnp.zeros(gs.shape, jnp.int32)
    for gp in range(N_EXPERT_GROUPS):
        other = gs[gp:gp + 1]
        ahead = ahead + jnp.where((other > gs) | ((other == gs) & (gp < gio)), 1, 0)
    grp_on = jnp.where(ahead < TOPK_GROUPS, 1.0, 0.0)
    selm = jnp.where(jnp.broadcast_to(grp_on, sel3.shape) > 0.5, sel3, NEG_BIG).reshape(N_EXPERTS, tm)
    eio = lax.broadcasted_iota(jnp.int32, selm.shape, 0)
    rank = jnp.zeros(selm.shape, jnp.int32)
    for e in range(N_EXPERTS):
        other = selm[e:e + 1, :]
        rank = rank + jnp.where((other > selm) | ((other == selm) & (e < eio)), 1, 0)
    gate = jnp.where(rank < TOP_K, scores, 0.0)
    gate = gate / jnp.sum(gate, axis=0, keepdims=True) * ROUTED_SCALE
    padded = jnp.concatenate([gate, jnp.zeros((LANES - N_EXPERTS, tm), F32)], axis=0)
    gate_ref[...] = padded.T


def _router(x, g, mods3, layer, row_fn, w_router_t, b_router):
    b, r, d = x.shape
    tm = ROW_TILE
    return pl.pallas_call(
        _router_kernel,
        grid=(b, r // tm),
        in_specs=[
            pl.BlockSpec((None, tm, d), lambda bb, i: (bb, i, 0)),
            pl.BlockSpec((1, d), lambda bb, i: (0, 0)),
            _mod_spec(d, layer, 3, row_fn),
            _mod_spec(d, layer, 4, row_fn),
            pl.BlockSpec(w_router_t.shape, lambda bb, i: (0, 0)),
            pl.BlockSpec((N_EXPERTS, 1), lambda bb, i: (0, 0)),
        ],
        out_specs=[
            pl.BlockSpec((None, tm, d), lambda bb, i: (bb, i, 0)),
            pl.BlockSpec((None, tm, LANES), lambda bb, i: (bb, i, 0)),
        ],
        out_shape=[jax.ShapeDtypeStruct((b, r, d), BF16), jax.ShapeDtypeStruct((b, r, LANES), F32)],
        compiler_params=_cparams(("arbitrary", "arbitrary")),
        name="moe_router",
    )(x, g.reshape(1, d), mods3, mods3, w_router_t, b_router.reshape(N_EXPERTS, 1))


def _experts_kernel(h_ref, gate_ref, wg_ref, wu_ref, wd_ref, sg_ref, su_ref, sd_ref, o_ref):
    e = pl.program_id(1)
    h = h_ref[...]

    @pl.when(e == 0)
    def _():
        a = jnp.dot(h, sg_ref[...], preferred_element_type=F32)
        u = jnp.dot(h, su_ref[...], preferred_element_type=F32)
        o_ref[...] = jnp.dot((_silu(a) * u).astype(BF16), sd_ref[...], preferred_element_type=F32)

    a = jnp.dot(h, wg_ref[...], preferred_element_type=F32)
    u = jnp.dot(h, wu_ref[...], preferred_element_type=F32)
    sel = (lax.broadcasted_iota(jnp.int32, (LANES, D_EXPERT), 0) == e).astype(F32)
    gcol = jnp.dot(gate_ref[...], sel, preferred_element_type=F32, precision=HIGHEST)
    m = (_silu(a) * u * gcol).astype(BF16)
    o_ref[...] += jnp.dot(m, wd_ref[...], preferred_element_type=F32)


def _experts(h2, gates, wg, wu, wd, sg, su, sd):
    t, d = h2.shape
    tm = MOE_TILE
    return pl.pallas_call(
        _experts_kernel,
        grid=(t // tm, N_EXPERTS),
        in_specs=[
            pl.BlockSpec((tm, d), lambda i, e: (i, 0)),
            pl.BlockSpec((tm, LANES), lambda i, e: (i, 0)),
            pl.BlockSpec((None, d, D_EXPERT), lambda i, e: (e, 0, 0)),
            pl.BlockSpec((None, d, D_EXPERT), lambda i, e: (e, 0, 0)),
            pl.BlockSpec((None, D_EXPERT, d), lambda i, e: (e, 0, 0)),
            pl.BlockSpec(sg.shape, lambda i, e: (0, 0)),
            pl.BlockSpec(su.shape, lambda i, e: (0, 0)),
            pl.BlockSpec(sd.shape, lambda i, e: (0, 0)),
        ],
        out_specs=pl.BlockSpec((tm, d), lambda i, e: (i, 0)),
        out_shape=jax.ShapeDtypeStruct((t, d), F32),
        compiler_params=_cparams(("arbitrary", "arbitrary")),
        name="moe_experts",
    )(h2, gates, wg, wu, wd, sg, su, sd)


def _residual_kernel(x_ref, f_ref, g2_ref, o_ref):
    o_ref[...] = x_ref[...] + g2_ref[...] * f_ref[...]


def _final_kernel(x_ref, f_ref, g2_ref, gf_ref, o_ref):
    x = x_ref[...] + g2_ref[...] * f_ref[...]
    ms = jnp.mean(x * x, axis=-1, keepdims=True)
    o_ref[...] = x * lax.rsqrt(ms + NORM_EPS) * gf_ref[...]


def _residual(x, f, mods3, layer, row_fn, g_final=None):
    b, r, d = x.shape
    tm = ROW_TILE
    tile = pl.BlockSpec((None, tm, d), lambda bb, i: (bb, i, 0))
    in_specs = [tile, tile, _mod_spec(d, layer, 5, row_fn)]
    args = [x, f, mods3]
    kern = _residual_kernel
    if g_final is not None:
        in_specs.append(pl.BlockSpec((1, d), lambda bb, i: (0, 0)))
        args.append(g_final.reshape(1, d))
        kern = _final_kernel
    return pl.pallas_call(
        kern,
        grid=(b, r // tm),
        in_specs=in_specs,
        out_specs=tile,
        out_shape=jax.ShapeDtypeStruct((b, r, d), F32),
        compiler_params=_cparams(("arbitrary", "arbitrary")),
        name="ffn_residual",
    )(*args)


def _moe(x, g_ffn, mods3, layer, row_fn, w_router, b_router, w_e_gate, w_e_up, w_e_down,
         ws_gate, ws_up, ws_down):
    b, r, d = x.shape
    wr_t = w_router.T
    h2, gates = _router(x, g_ffn, mods3, layer, row_fn, wr_t, b_router)
    f = _experts(h2.reshape(b * r, d), gates.reshape(b * r, LANES),
                 w_e_gate.astype(BF16), w_e_up.astype(BF16), w_e_down.astype(BF16),
                 ws_gate.astype(BF16), ws_up.astype(BF16), ws_down.astype(BF16))
    return f.reshape(b, r, d)


def kernel(x, c, ctx, c_ctx, w_mod, b_mod, g_mix, g_ffn, g_final, ab_w_in, ab_w_out, ab_conv_w, ab_conv_b, ab_w_r, ab_b_r, ab_w_i, ab_b_i, ab_lam, ab_sink, cd_w_in, cd_w_out, cd_lam, cd_subln_g, cd_conv_w, cd_conv_b, cd_dt_bias, cd_a_log, cd_d_skip, cd_norm_g, w_router, b_router, w_e_gate, w_e_up, w_e_down, ws_gate, ws_up, ws_down):
    bsz, s_len, d = x.shape
    c_len = ctx.shape[1]
    depth = w_mod.shape[0]
    assert depth == 2 and bsz == SUBLANES, "kernels are specialised to depth 2 and batch 8"
    assert c_len % ROW_TILE == 0 and s_len % ROW_TILE == 0
    nct_row = c_len // ROW_TILE
    nct_time = c_len // TIME_TILE

    c_all = jnp.concatenate([c, c_ctx[None], jnp.zeros((MOD_ROWS - bsz - 1, d), F32)], axis=0)
    mods3 = _modulations(c_all, w_mod, b_mod).reshape(depth * MOD_ROWS, 1, N_MOD * d)
    rope_tabs = _rope_tables(c_len, s_len)
    xc = jnp.concatenate([ctx, x], axis=1)
    row_mixed = lambda bb, i: jnp.where(i < nct_row, SUBLANES, bb)
    row_latent = lambda bb, i: bb

    w_in = ab_w_in[0].astype(BF16)
    q_hi = LRU_WIDTH + WIN_HEADS * HEAD_DIM
    x_hi = q_hi + LRU_WIDTH
    k_hi = x_hi + WIN_KV_HEADS * HEAD_DIM
    gate, q, xa, k, v = _project(xc, g_mix[0], mods3, 0, nct_row, rope_tabs, [
        (w_in[:, :LRU_WIDTH], False, F32, False),
        (w_in[:, LRU_WIDTH:q_hi], True, BF16, False),
        (w_in[:, q_hi:x_hi], False, F32, True),
        (w_in[:, x_hi:k_hi], True, BF16, False),
        (w_in[:, k_hi:], False, BF16, False),
    ])
    l_len = c_len + s_len
    w_gates = jnp.stack([jnp.concatenate([_block_diag(ab_w_r[0, dd]), _block_diag(ab_w_i[0, dd])], axis=1)
                         for dd in range(2)]).astype(BF16)
    b_gates = jnp.concatenate([ab_b_r[0], ab_b_i[0]], axis=-1).reshape(2, 1, 2 * LRU_WIDTH)
    rec = _rglru(xa.reshape(l_len, bsz, LRU_WIDTH), ab_conv_w[0], ab_conv_b[0], w_gates, b_gates,
                 ab_lam[0].reshape(2, 1, LRU_WIDTH), nct_time)
    att = _win_attention(q, k, v, ab_sink[0], c_len)
    w_out = ab_w_out[0].astype(BF16)
    xc = _out_even(xc, rec.reshape(2, l_len, bsz * LRU_WIDTH), gate, att, w_out[:LRU_WIDTH], w_out[LRU_WIDTH:],
                   mods3, 0, nct_row)
    f = _moe(xc, g_ffn[0], mods3, 0, row_mixed, w_router[0], b_router[0], w_e_gate[0], w_e_up[0], w_e_down[0],
             ws_gate[0], ws_up[0], ws_down[0])
    xc = _residual(xc, f, mods3, 0, row_mixed)

    w_in = cd_w_in[0].astype(BF16)
    qk = DIFF_HEADS * 2 * DIFF_DH
    z_hi = qk + SSD_INNER
    k_hi = z_hi + qk
    v_hi = k_hi + qk
    x_hi = v_hi + SSD_CONV_DIM
    w_dt = jnp.pad(w_in[:, x_hi:], ((0, 0), (0, LANES - 2 * SSD_HEADS)))
    q, z, k, v, xbc, dt = _project(xc, g_mix[1], mods3, 1, nct_row, rope_tabs, [
        (w_in[:, :qk], True, BF16, False),
        (w_in[:, qk:z_hi], False, F32, False),
        (w_in[:, z_hi:k_hi], True, BF16, False),
        (w_in[:, k_hi:v_hi], False, BF16, False),
        (w_in[:, v_hi:x_hi], False, F32, False),
        (w_dt, False, F32, False),
    ])
    lam_init = 0.8 - 0.6 * math.exp(-0.3 * 1)
    diff = _diff_attention(q, k, v, cd_lam[0], cd_subln_g[0], lam_init, c_len)
    y2 = _ssd(xbc, dt, cd_conv_w[0], cd_conv_b[0], cd_dt_bias[0], cd_a_log[0], cd_d_skip[0], nct_time)
    w_out = cd_w_out[0].astype(BF16)
    xl = _out_odd(xc, diff, y2, z, cd_norm_g[0], w_out[:qk], w_out[qk:], mods3, 1, c_len)
    f = _moe(xl, g_ffn[1], mods3, 1, row_latent, w_router[1], b_router[1], w_e_gate[1], w_e_up[1], w_e_down[1],
             ws_gate[1], ws_up[1], ws_down[1])
    return _residual(xl, f, mods3, 1, row_latent, g_final=g_final)
```

```python
import functools
import math

import jax
import jax.numpy as jnp
from jax import lax
from jax.experimental import pallas as pl
from jax.experimental.pallas import tpu as pltpu

F32 = jnp.float32
BF16 = jnp.bfloat16
HIGHEST = lax.Precision.HIGHEST

GRID_W = 64
N_MOD = 6
NORM_EPS = 1e-6
ROPE_BASE = 10000.0
CONV_W = 4

LRU_WIDTH = 512
LRU_BLOCKS = 8
LRU_C = 8.0

HEAD_DIM = 64
WIN_HEADS = 8
WIN_KV_HEADS = 2
WINDOW = 128

DIFF_HEADS = 4
DIFF_DH = 64

SSD_HEADS = 8
SSD_HEAD_DIM = 64
SSD_INNER = SSD_HEADS * SSD_HEAD_DIM
SSD_GROUPS = 2
SSD_STATE = 128
SSD_CONV_DIM = SSD_INNER + 2 * SSD_GROUPS * SSD_STATE

N_EXPERTS = 64
N_EXPERT_GROUPS = 8
TOPK_GROUPS = 4
TOP_K = 8
D_EXPERT = 256
ROUTED_SCALE = 2.5

LANES = 128
SUBLANES = 8
MOD_ROWS = 16
TIME_TILE = 128
ROW_TILE = 256
EXPERT_BLK = 256
VMEM_LIMIT = 48 * 1024 * 1024
NEG_BIG = -1e30


def _cparams(sem):
    return pltpu.CompilerParams(dimension_semantics=sem, vmem_limit_bytes=VMEM_LIMIT)


def _nt_dot(a, b):
    return lax.dot_general(a, b, (((1,), (1,)), ((), ())), preferred_element_type=F32)


def _softplus(x):
    return jnp.maximum(x, 0.0) + jnp.log1p(jnp.exp(-jnp.abs(x)))


def _silu(x):
    return x * jax.nn.sigmoid(x)


def _mod_kernel(c_ref, w_ref, b_ref, o_ref):
    c = c_ref[...]
    s = _silu(c)
    o_ref[...] = jnp.dot(s, w_ref[...], preferred_element_type=F32, precision=HIGHEST) + b_ref[...]


def _modulations(c_all, w_mod, b_mod):
    depth, d, _ = w_mod.shape
    return pl.pallas_call(
        _mod_kernel,
        grid=(depth, N_MOD),
        in_specs=[
            pl.BlockSpec((MOD_ROWS, d), lambda l, k: (0, 0)),
            pl.BlockSpec((None, d, d), lambda l, k: (l, 0, k)),
            pl.BlockSpec((None, 1, d), lambda l, k: (l, 0, k)),
        ],
        out_specs=pl.BlockSpec((None, MOD_ROWS, d), lambda l, k: (l, 0, k)),
        out_shape=jax.ShapeDtypeStruct((depth, MOD_ROWS, N_MOD * d), F32),
        compiler_params=_cparams(("arbitrary", "arbitrary")),
        name="adaln_modulation",
    )(c_all, w_mod, b_mod.reshape(depth, 1, N_MOD * d))


def _mod_spec(d, layer, chunk, row_fn):
    return pl.BlockSpec((None, 1, d), lambda b, i: (layer * MOD_ROWS + row_fn(b, i), 0, chunk))


def _norm_mod(x, g, sh, sc):
    ms = jnp.mean(x * x, axis=-1, keepdims=True)
    return (x * lax.rsqrt(ms + NORM_EPS) * g) * (1.0 + sc) + sh


def _rope(y, cos, sa, sb):
    n = y.shape[-1]
    half = HEAD_DIM // 2
    return y * cos + pltpu.roll(y, n - half, 1) * sa + pltpu.roll(y, half, 1) * sb


def _proj_kernel(*refs, ropes):
    n = len(ropes)
    x_ref, g_ref, sh_ref, sc_ref, cos_ref, sa_ref, sb_ref = refs[:7]
    w_refs = refs[7:7 + n]
    o_refs = refs[7 + n:]
    h = _norm_mod(x_ref[...], g_ref[...], sh_ref[...], sc_ref[...]).astype(BF16)
    for w_ref, o_ref, rope in zip(w_refs, o_refs, ropes):
        y = jnp.dot(h, w_ref[...], preferred_element_type=F32)
        if rope:
            w = y.shape[-1]
            y = _rope(y, cos_ref[:, :w], sa_ref[:, :w], sb_ref[:, :w])
        o_ref[...] = y.astype(o_ref.dtype)


def _project(xc, g, mods3, layer, nct, rope_tabs, groups):
    b, l, d = xc.shape
    tm = ROW_TILE
    row = lambda bb, i: jnp.where(i < nct, SUBLANES, bb)
    rw = rope_tabs[0].shape[-1]
    in_specs = [
        pl.BlockSpec((None, tm, d), lambda bb, i: (bb, i, 0)),
        pl.BlockSpec((1, d), lambda bb, i: (0, 0)),
        _mod_spec(d, layer, 0, row),
        _mod_spec(d, layer, 1, row),
    ] + [pl.BlockSpec((tm, rw), lambda bb, i: (i, 0))] * 3
    out_specs, out_shapes = [], []
    for w, _, dt, time_major in groups:
        n = w.shape[1]
        in_specs.append(pl.BlockSpec((d, n), lambda bb, i: (0, 0)))
        if time_major:
            out_specs.append(pl.BlockSpec((tm, n), lambda bb, i: (i, bb)))
            out_shapes.append(jax.ShapeDtypeStruct((l, b * n), dt))
        else:
            out_specs.append(pl.BlockSpec((None, tm, n), lambda bb, i: (bb, i, 0)))
            out_shapes.append(jax.ShapeDtypeStruct((b, l, n), dt))
    return pl.pallas_call(
        functools.partial(_proj_kernel, ropes=tuple(gp[1] for gp in groups)),
        grid=(b, l // tm),
        in_specs=in_specs,
        out_specs=out_specs,
        out_shape=out_shapes,
        compiler_params=_cparams(("arbitrary", "arbitrary")),
        name="norm_mod_project",
    )(xc, g.reshape(1, d), mods3, mods3, *rope_tabs, *[gp[0] for gp in groups])


def _rope_tables(c_len, s_len):
    rows = s_len // GRID_W
    row = jnp.repeat(jnp.arange(rows), GRID_W).astype(F32)
    col = jnp.tile(jnp.arange(GRID_W), rows).astype(F32)
    n = HEAD_DIM // 4
    inv = ROPE_BASE ** (-jnp.arange(n, dtype=F32) / n)
    ang = jnp.concatenate([row[:, None] * inv, col[:, None] * inv], axis=-1)
    cos, sin = jnp.cos(ang), jnp.sin(ang)
    zero = jnp.zeros_like(sin)
    reps = WIN_HEADS
    cos_t = jnp.tile(jnp.concatenate([cos, cos], axis=-1), (1, reps))
    sa_t = jnp.tile(jnp.concatenate([-sin, zero], axis=-1), (1, reps))
    sb_t = jnp.tile(jnp.concatenate([zero, sin], axis=-1), (1, reps))
    w = cos_t.shape[-1]
    pad1 = jnp.ones((c_len, w), F32)
    pad0 = jnp.zeros((c_len, w), F32)
    return (jnp.concatenate([pad1, cos_t], 0), jnp.concatenate([pad0, sa_t], 0),
            jnp.concatenate([pad0, sb_t], 0))


def _seq_tile(d, g, nct, nt):
    rev = jnp.where(g < nct, nct - 1 - g, nt - 1 - (g - nct))
    return jnp.where(d == 0, g, rev)


def _rglru_kernel(x_ref, xp_ref, xn_ref, cw_ref, cb_ref, w_ref, bias_ref, lam_ref, o_ref,
                  ext_scr, a_scr, b_scr, h_scr, *, ts, nct, nt, sub):
    d = pl.program_id(0)
    g = pl.program_id(1)
    tile = _seq_tile(d, g, nct, nt)
    bsz, width = h_scr.shape
    pv = jnp.where((tile == 0) | (tile == nct), 0.0, 1.0)
    nv = jnp.where((tile == nct - 1) | (tile == nt - 1), 0.0, 1.0)
    ext_scr[0:1] = xp_ref[...] * pv
    ext_scr[1:ts + 1] = x_ref[...]
    ext_scr[ts + 1:ts + 3] = xn_ref[...] * nv

    @pl.when(g == 0)
    def _():
        h_scr[...] = jnp.zeros_like(h_scr)

    neg_sp = -LRU_C * _softplus(-lam_ref[...])

    def prep(c, carry):
        r0 = pl.multiple_of(c * sub, sub)
        e = ext_scr[pl.ds(r0, sub + CONV_W - 1)]
        u = cb_ref[...] + cw_ref[0] * e[0:sub]
        for j in range(1, CONV_W):
            u = u + cw_ref[j] * e[j:j + sub]
        u2 = u.reshape(sub * bsz, width)
        gts = jnp.dot(u2.astype(BF16), w_ref[...], preferred_element_type=F32) + bias_ref[...]
        r = jax.nn.sigmoid(gts[:, :width])
        ig = jax.nn.sigmoid(gts[:, width:])
        log_a = neg_sp * r
        a = jnp.exp(log_a)
        mult = jnp.sqrt(1.0 - a * a)
        a_scr[pl.ds(r0, sub)] = a.reshape(sub, bsz, width)
        b_scr[pl.ds(r0, sub)] = (mult * ig * u2).reshape(sub, bsz, width)
        return carry

    lax.fori_loop(0, ts // sub, prep, 0)

    def step(t, h):
        tt = jnp.where(d == 0, t, ts - 1 - t)
        h = a_scr[tt] * h + b_scr[tt]
        o_ref[tt] = h
        return h

    h_scr[...] = lax.fori_loop(0, ts, step, h_scr[...], unroll=8)


def _rglru(xa_tm, conv_w, conv_b, w_gates, b_gates, lam, nct):
    l, bsz, width = xa_tm.shape
    ts = TIME_TILE
    nt = l // ts
    tile = lambda d, g: _seq_tile(d, g, nct, nt)
    kern = functools.partial(_rglru_kernel, ts=ts, nct=nct, nt=nt, sub=16)
    return pl.pallas_call(
        kern,
        grid=(2, nt),
        in_specs=[
            pl.BlockSpec((ts, bsz, width), lambda d, g: (tile(d, g), 0, 0)),
            pl.BlockSpec((1, bsz, width), lambda d, g: (jnp.maximum(tile(d, g) * ts - 1, 0), 0, 0)),
            pl.BlockSpec((2, bsz, width),
                         lambda d, g: (jnp.minimum((tile(d, g) + 1) * (ts // 2), l // 2 - 1), 0, 0)),
            pl.BlockSpec((CONV_W, 1, width), lambda d, g: (0, 0, 0)),
            pl.BlockSpec((1, width), lambda d, g: (0, 0)),
            pl.BlockSpec((None, width, 2 * width), lambda d, g: (d, 0, 0)),
            pl.BlockSpec((None, 1, 2 * width), lambda d, g: (d, 0, 0)),
            pl.BlockSpec((None, 1, width), lambda d, g: (d, 0, 0)),
        ],
        out_specs=pl.BlockSpec((None, ts, bsz, width), lambda d, g: (d, tile(d, g), 0, 0)),
        out_shape=jax.ShapeDtypeStruct((2, l, bsz, width), F32),
        scratch_shapes=[
            pltpu.VMEM((ts + CONV_W - 1, bsz, width), F32),
            pltpu.VMEM((ts, bsz, width), F32),
            pltpu.VMEM((ts, bsz, width), F32),
            pltpu.VMEM((bsz, width), F32),
        ],
        compiler_params=_cparams(("arbitrary", "arbitrary")),
        name="rglru_scan",
    )(xa_tm, xa_tm, xa_tm, conv_w.reshape(CONV_W, 1, width), conv_b.reshape(1, width),
      w_gates, b_gates, lam)


def _block_diag(w):
    nb, c, dd = w.shape
    eye = jnp.eye(nb, dtype=w.dtype)
    return (eye[:, None, :, None] * w[:, :, None, :]).reshape(nb * c, nb * dd)


def _win_attn_kernel(sink_ref, q_ref, k_ref, v_ref, o_ref, *, c_len, l_len, nqc):
    j = pl.program_id(1)
    blk = q_ref.shape[0]
    grp = WIN_HEADS // WIN_KV_HEADS
    band = 3 * blk

    def heads(body):
        for hk in range(WIN_KV_HEADS):
            ksl = slice(hk * HEAD_DIM, (hk + 1) * HEAD_DIM)
            for gq in range(grp):
                head = hk * grp + gq
                hsl = slice(head * HEAD_DIM, (head + 1) * HEAD_DIM)
                o_ref[:, hsl] = body(q_ref[:, hsl], ksl, sink_ref[head]).astype(o_ref.dtype)

    @pl.when(j < nqc)
    def _():
        def body(qh, ksl, sink):
            s = _nt_dot(qh, k_ref[0:c_len, ksl]) * (HEAD_DIM ** -0.5)
            m = jnp.maximum(jnp.max(s, axis=-1, keepdims=True), sink)
            p = jnp.exp(s - m)
            den = jnp.sum(p, axis=-1, keepdims=True) + jnp.exp(sink - m)
            o = jnp.dot(p.astype(BF16), v_ref[0:c_len, ksl], preferred_element_type=F32)
            return o / den
        heads(body)

    @pl.when(j >= nqc)
    def _():
        jb = j - nqc
        start = jnp.clip(c_len + (jb - 1) * blk, c_len - blk, l_len - band)
        start = pl.multiple_of(start, blk)
        qpos = jb * blk + lax.broadcasted_iota(jnp.int32, (blk, band), 0)
        kpos = start - c_len + lax.broadcasted_iota(jnp.int32, (blk, band), 1)
        valid = (jnp.abs(qpos - kpos) <= WINDOW) & (kpos >= 0)

        def body(qh, ksl, sink):
            sc = _nt_dot(qh, k_ref[0:c_len, ksl]) * (HEAD_DIM ** -0.5)
            sb = _nt_dot(qh, k_ref[pl.ds(start, band), ksl]) * (HEAD_DIM ** -0.5)
            sb = jnp.where(valid, sb, NEG_BIG)
            m = jnp.maximum(jnp.maximum(jnp.max(sc, axis=-1, keepdims=True),
                                        jnp.max(sb, axis=-1, keepdims=True)), sink)
            pc = jnp.exp(sc - m)
            pb = jnp.exp(sb - m)
            den = (jnp.sum(pc, axis=-1, keepdims=True) + jnp.sum(pb, axis=-1, keepdims=True)
                   + jnp.exp(sink - m))
            o = (jnp.dot(pc.astype(BF16), v_ref[0:c_len, ksl], preferred_element_type=F32)
                 + jnp.dot(pb.astype(BF16), v_ref[pl.ds(start, band), ksl], preferred_element_type=F32))
            return o / den
        heads(body)


def _win_attention(q, k, v, sink, c_len):
    b, l, qw = q.shape
    kw = k.shape[-1]
    blk = TIME_TILE
    kern = functools.partial(_win_attn_kernel, c_len=c_len, l_len=l, nqc=c_len // blk)
    return pl.pallas_call(
        kern,
        grid=(b, l // blk),
        in_specs=[
            pl.BlockSpec(memory_space=pltpu.SMEM),
            pl.BlockSpec((None, blk, qw), lambda bb, j: (bb, j, 0)),
            pl.BlockSpec((None, l, kw), lambda bb, j: (bb, 0, 0)),
            pl.BlockSpec((None, l, kw), lambda bb, j: (bb, 0, 0)),
        ],
        out_specs=pl.BlockSpec((None, blk, qw), lambda bb, j: (bb, j, 0)),
        out_shape=jax.ShapeDtypeStruct((b, l, qw), BF16),
        compiler_params=_cparams(("arbitrary", "arbitrary")),
        name="window_attention",
    )(sink, q, k, v)


def _out_even_kernel(x_ref, rec_ref, gate_ref, att_ref, wa_ref, wb_ref, g1_ref, o_ref):
    lru = (rec_ref[0] + rec_ref[1]) * jax.nn.gelu(gate_ref[...])
    y = (jnp.dot(lru.astype(BF16), wa_ref[...], preferred_element_type=F32)
         + jnp.dot(att_ref[...], wb_ref[...], preferred_element_type=F32))
    o_ref[...] = x_ref[...] + g1_ref[...] * y


def _out_even(xc, rec2, gate, att, w_a, w_b, mods3, layer, nct):
    b, l, d = xc.shape
    tm = ROW_TILE
    w = gate.shape[-1]
    row = lambda bb, i: jnp.where(i < nct, SUBLANES, bb)
    return pl.pallas_call(
        _out_even_kernel,
        grid=(b, l // tm),
        in_specs=[
            pl.BlockSpec((None, tm, d), lambda bb, i: (bb, i, 0)),
            pl.BlockSpec((2, tm, w), lambda bb, i: (0, i, bb)),
            pl.BlockSpec((None, tm, w), lambda bb, i: (bb, i, 0)),
            pl.BlockSpec((None, tm, att.shape[-1]), lambda bb, i: (bb, i, 0)),
            pl.BlockSpec(w_a.shape, lambda bb, i: (0, 0)),
            pl.BlockSpec(w_b.shape, lambda bb, i: (0, 0)),
            _mod_spec(d, layer, 2, row),
        ],
        out_specs=pl.BlockSpec((None, tm, d), lambda bb, i: (bb, i, 0)),
        out_shape=jax.ShapeDtypeStruct((b, l, d), F32),
        compiler_params=_cparams(("arbitrary", "arbitrary")),
        name="out_proj_even",
    )(xc, rec2, gate, att, w_a, w_b, mods3)


def _diff_attn_kernel(lam_ref, g_ref, q_ref, k_ref, v_ref, o_ref, *, lam_init):
    lv = lam_ref[...]
    lam = (jnp.exp(jnp.sum(lv[0:1] * lv[1:2], axis=-1, keepdims=True))
           - jnp.exp(jnp.sum(lv[2:3] * lv[3:4], axis=-1, keepdims=True)) + lam_init)
    vw = 2 * DIFF_DH

    def softmax(qm, km):
        s = _nt_dot(qm, km) * (DIFF_DH ** -0.5)
        m = jnp.max(s, axis=-1, keepdims=True)
        p = jnp.exp(s - m)
        return p / jnp.sum(p, axis=-1, keepdims=True)

    for h in range(DIFF_HEADS):
        lo = h * vw
        p0 = softmax(q_ref[:, lo:lo + DIFF_DH], k_ref[:, lo:lo + DIFF_DH])
        p1 = softmax(q_ref[:, lo + DIFF_DH:lo + vw], k_ref[:, lo + DIFF_DH:lo + vw])
        w = (p0 - lam * p1).astype(BF16)
        o = jnp.dot(w, v_ref[:, lo:lo + vw], preferred_element_type=F32)
        ms = jnp.mean(o * o, axis=-1, keepdims=True)
        o = o * lax.rsqrt(ms + NORM_EPS) * g_ref[...]
        o_ref[:, lo:lo + vw] = (o * (1.0 - lam_init)).astype(o_ref.dtype)


def _diff_attention(q, k, v, lam_vecs, subln_g, lam_init, c_len):
    b, l, w = q.shape
    tq = TIME_TILE
    s_len = l - c_len
    off = c_len // tq
    return pl.pallas_call(
        functools.partial(_diff_attn_kernel, lam_init=lam_init),
        grid=(b, s_len // tq),
        in_specs=[
            pl.BlockSpec(lam_vecs.shape, lambda bb, j: (0, 0)),
            pl.BlockSpec((1, 2 * DIFF_DH), lambda bb, j: (0, 0)),
            pl.BlockSpec((None, tq, w), lambda bb, j: (bb, j + off, 0)),
            pl.BlockSpec((None, l, w), lambda bb, j: (bb, 0, 0)),
            pl.BlockSpec((None, l, w), lambda bb, j: (bb, 0, 0)),
        ],
        out_specs=pl.BlockSpec((None, tq, w), lambda bb, j: (bb, j, 0)),
        out_shape=jax.ShapeDtypeStruct((b, s_len, w), BF16),
        compiler_params=_cparams(("arbitrary", "arbitrary")),
        name="diff_attention",
    )(lam_vecs, subln_g.reshape(1, -1), q, k, v)


def _ssd_kernel(x_ref, xp_ref, xn_ref, dt_ref, cw_ref, cb_ref, dtb_ref, alog_ref, dsk_ref, o_ref,
                ext_scr, st_scr, *, q, nct, nt):
    d = pl.program_id(0)
    g = pl.program_id(2)
    tile = _seq_tile(d, g, nct, nt)
    pv = jnp.where((tile == 0) | (tile == nct), 0.0, 1.0)
    nv = jnp.where((tile == nct - 1) | (tile == nt - 1), 0.0, 1.0)
    ext_scr[0:SUBLANES] = xp_ref[...] * pv
    ext_scr[SUBLANES:SUBLANES + q] = x_ref[...]
    ext_scr[SUBLANES + q:2 * SUBLANES + q] = xn_ref[...] * nv

    @pl.when(g == 0)
    def _():
        st_scr[...] = jnp.zeros_like(st_scr)

    u = cb_ref[...] + cw_ref[0] * ext_scr[SUBLANES - 1:SUBLANES - 1 + q, :]
    for j in range(1, CONV_W):
        u = u + cw_ref[j] * ext_scr[SUBLANES - 1 + j:SUBLANES - 1 + j + q, :]
    act = _silu(u)

    dtr = dt_ref[...]
    dtr = jnp.where(d == 0, dtr, pltpu.roll(dtr, LANES - SSD_HEADS, 1))
    dtv = _softplus(dtr + dtb_ref[...])
    head_lane = lax.broadcasted_iota(jnp.int32, (1, LANES), 1) < SSD_HEADS
    dta = dtv * jnp.where(head_lane, -jnp.exp(alog_ref[...]), 0.0)
    ri = lax.broadcasted_iota(jnp.int32, (q, q), 0)
    ci = lax.broadcasted_iota(jnp.int32, (q, q), 1)
    keep = jnp.where(d == 0, ri - ci, ci - ri) >= 0
    cum = jnp.dot(keep.astype(F32), dta, preferred_element_type=F32, precision=HIGHEST)
    tot = jnp.sum(dta, axis=0, keepdims=True)
    cum_t = cum.T
    dt_t = dtv.T
    to_end = jnp.exp(tot - cum) * dtv
    e_cum = jnp.exp(cum)
    e_tot = jnp.exp(tot)
    dskip = dsk_ref[...] * jnp.where(d == 0, 1.0, 0.0)

    hpg = SSD_HEADS // SSD_GROUPS
    for gi in range(SSD_GROUPS):
        b_g = act[:, SSD_INNER + gi * SSD_STATE:SSD_INNER + (gi + 1) * SSD_STATE]
        c_lo = SSD_INNER + SSD_GROUPS * SSD_STATE + gi * SSD_STATE
        c_g = act[:, c_lo:c_lo + SSD_STATE].astype(BF16)
        cb = _nt_dot(c_g, b_g.astype(BF16))
        b_gt = b_g.T.astype(BF16)
        for hh in range(hpg):
            h = gi * hpg + hh
            xs = act[:, h * SSD_HEAD_DIM:(h + 1) * SSD_HEAD_DIM]
            seg = cum[:, h:h + 1] - cum_t[h:h + 1, :]
            decay = jnp.exp(jnp.where(keep, seg, NEG_BIG))
            w = (cb * decay * dt_t[h:h + 1, :]).astype(BF16)
            state = st_scr[h]
            y = jnp.dot(w, xs.astype(BF16), preferred_element_type=F32)
            y = y + jnp.dot(c_g, state.astype(BF16), preferred_element_type=F32) * e_cum[:, h:h + 1]
            y = y + dskip[:, h * SSD_HEAD_DIM:(h + 1) * SSD_HEAD_DIM] * xs
            o_ref[:, h * SSD_HEAD_DIM:(h + 1) * SSD_HEAD_DIM] = y
            s_new = jnp.dot(b_gt, (xs * to_end[:, h:h + 1]).astype(BF16), preferred_element_type=F32)
            st_scr[h] = e_tot[:, h:h + 1] * state + s_new


def _ssd(xbc, dt, conv_w, conv_b, dt_bias, a_log, d_skip, nct):
    b, l, cd = xbc.shape
    q = TIME_TILE
    nt = l // q
    tile = lambda d, bb, g: _seq_tile(d, g, nct, nt)
    r8 = q // SUBLANES
    pad = LANES - SSD_HEADS
    dtb = jnp.pad(dt_bias, ((0, 0), (0, pad))).reshape(2, 1, LANES)
    alog = jnp.pad(a_log, ((0, 0), (0, pad))).reshape(2, 1, LANES)
    dsk = jnp.repeat(d_skip, SSD_HEAD_DIM).reshape(1, SSD_INNER)
    return pl.pallas_call(
        functools.partial(_ssd_kernel, q=q, nct=nct, nt=nt),
        grid=(2, b, nt),
        in_specs=[
            pl.BlockSpec((None, q, cd), lambda d, bb, g: (bb, tile(d, bb, g), 0)),
            pl.BlockSpec((None, SUBLANES, cd),
                         lambda d, bb, g: (bb, jnp.maximum(tile(d, bb, g) * r8 - 1, 0), 0)),
            pl.BlockSpec((None, SUBLANES, cd),
                         lambda d, bb, g: (bb, jnp.minimum((tile(d, bb, g) + 1) * r8, l // SUBLANES - 1), 0)),
            pl.BlockSpec((None, q, LANES), lambda d, bb, g: (bb, tile(d, bb, g), 0)),
            pl.BlockSpec((CONV_W, 1, cd), lambda d, bb, g: (0, 0, 0)),
            pl.BlockSpec((1, cd), lambda d, bb, g: (0, 0)),
            pl.BlockSpec((None, 1, LANES), lambda d, bb, g: (d, 0, 0)),
            pl.BlockSpec((None, 1, LANES), lambda d, bb, g: (d, 0, 0)),
            pl.BlockSpec((1, SSD_INNER), lambda d, bb, g: (0, 0)),
        ],
        out_specs=pl.BlockSpec((None, None, q, SSD_INNER), lambda d, bb, g: (d, bb, tile(d, bb, g), 0)),
        out_shape=jax.ShapeDtypeStruct((2, b, l, SSD_INNER), F32),
        scratch_shapes=[
            pltpu.VMEM((q + 2 * SUBLANES, cd), F32),
            pltpu.VMEM((SSD_HEADS, SSD_STATE, SSD_HEAD_DIM), F32),
        ],
        compiler_params=_cparams(("arbitrary", "arbitrary", "arbitrary")),
        name="ssd_chunked",
    )(xbc, xbc, xbc, dt, conv_w.reshape(CONV_W, 1, cd), conv_b.reshape(1, cd), dtb, alog, dsk)


def _out_odd_kernel(x_ref, diff_ref, y_ref, z_ref, ng_ref, wa_ref, wb_ref, g1_ref, o_ref):
    yz = (y_ref[0] + y_ref[1]) * _silu(z_ref[...])
    gs = SSD_INNER // SSD_GROUPS
    parts = []
    for gi in range(SSD_GROUPS):
        seg = yz[:, gi * gs:(gi + 1) * gs]
        ms = jnp.mean(seg * seg, axis=-1, keepdims=True)
        parts.append(seg * lax.rsqrt(ms + NORM_EPS) * ng_ref[:, gi * gs:(gi + 1) * gs])
    ssd = jnp.concatenate(parts, axis=-1).astype(BF16)
    y = (jnp.dot(diff_ref[...], wa_ref[...], preferred_element_type=F32)
         + jnp.dot(ssd, wb_ref[...], preferred_element_type=F32))
    o_ref[...] = x_ref[...] + g1_ref[...] * y


def _out_odd(xc, diff, y2, z, norm_g, w_a, w_b, mods3, layer, c_len):
    b, l, d = xc.shape
    s_len = l - c_len
    tm = ROW_TILE
    off = c_len // tm
    w = SSD_INNER
    row = lambda bb, i: bb
    return pl.pallas_call(
        _out_odd_kernel,
        grid=(b, s_len // tm),
        in_specs=[
            pl.BlockSpec((None, tm, d), lambda bb, i: (bb, i + off, 0)),
            pl.BlockSpec((None, tm, diff.shape[-1]), lambda bb, i: (bb, i, 0)),
            pl.BlockSpec((2, None, tm, w), lambda bb, i: (0, bb, i + off, 0)),
            pl.BlockSpec((None, tm, w), lambda bb, i: (bb, i + off, 0)),
            pl.BlockSpec((1, w), lambda bb, i: (0, 0)),
            pl.BlockSpec(w_a.shape, lambda bb, i: (0, 0)),
            pl.BlockSpec(w_b.shape, lambda bb, i: (0, 0)),
            _mod_spec(d, layer, 2, row),
        ],
        out_specs=pl.BlockSpec((None, tm, d), lambda bb, i: (bb, i, 0)),
        out_shape=jax.ShapeDtypeStruct((b, s_len, d), F32),
        compiler_params=_cparams(("arbitrary", "arbitrary")),
        name="out_proj_odd",
    )(xc, diff, y2, z, norm_g.reshape(1, w), w_a, w_b, mods3)


def _router_kernel(x_ref, g_ref, sh_ref, sc_ref, wr_ref, br_ref, h_ref, eid_ref, rnk_ref, gate_ref, cnt_ref,
                   carry_scr):
    @pl.when((pl.program_id(0) == 0) & (pl.program_id(1) == 0))
    def _():
        carry_scr[...] = jnp.zeros_like(carry_scr)

    h = _norm_mod(x_ref[...], g_ref[...], sh_ref[...], sc_ref[...])
    h_ref[...] = h
    tm = h.shape[0]
    per = N_EXPERTS // N_EXPERT_GROUPS
    logits = lax.dot_general(wr_ref[...], h, (((1,), (1,)), ((), ())),
                             preferred_element_type=F32, precision=HIGHEST)
    scores = jax.nn.sigmoid(logits)
    sel = scores + br_ref[...]
    sel3 = sel.reshape(N_EXPERT_GROUPS, per, tm)
    kio = lax.broadcasted_iota(jnp.int32, sel3.shape, 1)
    m1 = jnp.max(sel3, axis=1, keepdims=True)
    first = jnp.min(jnp.where(sel3 == m1, kio, per), axis=1, keepdims=True)
    m2 = jnp.max(jnp.where(kio == first, NEG_BIG, sel3), axis=1, keepdims=True)
    gs = m1 + m2
    gio = lax.broadcasted_iota(jnp.int32, gs.shape, 0)
    ahead = jnp.zeros(gs.shape, jnp.int32)
    for gp in range(N_EXPERT_GROUPS):
        other = gs[gp:gp + 1]
        ahead = ahead + jnp.where((other > gs) | ((other == gs) & (gp < gio)), 1, 0)
    grp_on = jnp.where(ahead < TOPK_GROUPS, 1.0, 0.0)
    selm = jnp.where(jnp.broadcast_to(grp_on, sel3.shape) > 0.5, sel3, NEG_BIG).reshape(N_EXPERTS, tm)
    eio = lax.broadcasted_iota(jnp.int32, selm.shape, 0)
    rank = jnp.zeros(selm.shape, jnp.int32)
    for e in range(N_EXPERTS):
        other = selm[e:e + 1, :]
        rank = rank + jnp.where((other > selm) | ((other == selm) & (e < eio)), 1, 0)
    chosen = rank < TOP_K
    gate = jnp.where(chosen, scores, 0.0)
    gate = gate / jnp.sum(gate, axis=0, keepdims=True) * ROUTED_SCALE
    cf = jnp.where(chosen, 1.0, 0.0)
    ti = lax.broadcasted_iota(jnp.int32, (tm, tm), 0)
    tj = lax.broadcasted_iota(jnp.int32, (tm, tm), 1)
    before = jnp.where(ti < tj, 1.0, 0.0).astype(BF16)
    in_expert = carry_scr[:, 0:1] + jnp.dot(cf.astype(BF16), before, preferred_element_type=F32)
    carry_scr[...] = carry_scr[...] + jnp.sum(cf, axis=1, keepdims=True)
    cnt_ref[...] = carry_scr[...]
    eio_f = eio.astype(F32)
    e_rows, r_rows, g_rows = [], [], []
    for k in range(TOP_K):
        hit = jnp.where(rank == k, 1.0, 0.0)
        e_rows.append(jnp.sum(hit * eio_f, axis=0, keepdims=True))
        r_rows.append(jnp.sum(hit * in_expert, axis=0, keepdims=True))
        g_rows.append(jnp.sum(hit * gate, axis=0, keepdims=True))
    eid_ref[...] = jnp.concatenate(e_rows, axis=0).astype(jnp.int32)
    rnk_ref[...] = jnp.concatenate(r_rows, axis=0).astype(jnp.int32)
    padded = jnp.concatenate(g_rows + [jnp.zeros((LANES - TOP_K, tm), F32)], axis=0)
    gate_ref[...] = padded.T


def _router(x, g, mods3, layer, row_fn, w_router_t, b_router):
    b, r, d = x.shape
    tm = ROW_TILE
    nt = r // tm
    slot = pl.BlockSpec((TOP_K, tm), lambda bb, i: (0, bb * nt + i))
    slot_shape = jax.ShapeDtypeStruct((TOP_K, b * r), jnp.int32)
    return pl.pallas_call(
        _router_kernel,
        grid=(b, nt),
        in_specs=[
            pl.BlockSpec((None, tm, d), lambda bb, i: (bb, i, 0)),
            pl.BlockSpec((1, d), lambda bb, i: (0, 0)),
            _mod_spec(d, layer, 3, row_fn),
            _mod_spec(d, layer, 4, row_fn),
            pl.BlockSpec(w_router_t.shape, lambda bb, i: (0, 0)),
            pl.BlockSpec((N_EXPERTS, 1), lambda bb, i: (0, 0)),
        ],
        out_specs=[
            pl.BlockSpec((None, tm, d), lambda bb, i: (bb, i, 0)),
            slot,
            slot,
            pl.BlockSpec((None, tm, LANES), lambda bb, i: (bb, i, 0)),
            pl.BlockSpec((N_EXPERTS, LANES), lambda bb, i: (0, 0)),
        ],
        out_shape=[jax.ShapeDtypeStruct((b, r, d), F32), slot_shape, slot_shape,
                   jax.ShapeDtypeStruct((b, r, LANES), F32), jax.ShapeDtypeStruct((N_EXPERTS, LANES), F32)],
        scratch_shapes=[pltpu.VMEM((N_EXPERTS, LANES), F32)],
        compiler_params=_cparams(("arbitrary", "arbitrary")),
        name="moe_router",
    )(x, g.reshape(1, d), mods3, mods3, w_router_t, b_router.reshape(N_EXPERTS, 1))


def _moe_plan(counts, n_rows):
    blk = EXPERT_BLK
    nb = n_rows // blk
    ends = jnp.cumsum(counts)
    starts = ends - counts
    first = jnp.arange(nb, dtype=jnp.int32) * blk
    e_lo = jnp.searchsorted(ends, first, side='right').astype(jnp.int32)
    e_hi = jnp.searchsorted(ends, first + (blk - 1), side='right').astype(jnp.int32)
    n_pair = e_hi - e_lo + 1
    p_end = jnp.cumsum(n_pair)
    p_start = p_end - n_pair
    i = jnp.arange(nb + N_EXPERTS - 1, dtype=jnp.int32)
    j = jnp.minimum(jnp.searchsorted(p_end, i, side='right'), nb - 1).astype(jnp.int32)
    valid = i < p_end[-1]
    e = jnp.where(valid, e_lo[j] + i - p_start[j], e_hi[nb - 1]).astype(jnp.int32)
    bounds = jnp.concatenate([starts, ends[-1:]]).astype(jnp.int32)
    return j, e, valid.astype(jnp.int32), bounds


def _dispatch_kernel(starts_ref, eid_ref, rnk_ref, h_ref, xs_ref, pos_ref, sem):
    tm = h_ref.shape[0]

    def issue(t, carry):
        for k in range(TOP_K):
            p = starts_ref[eid_ref[k, t]] + rnk_ref[k, t]
            pos_ref[k, t] = p
            pltpu.make_async_copy(h_ref.at[pl.ds(t, 1)], xs_ref.at[pl.ds(p, 1)], sem).start()
        return carry

    lax.fori_loop(0, tm, issue, 0)
    for _ in range(TOP_K):
        pltpu.make_async_copy(h_ref, xs_ref.at[pl.ds(0, tm)], sem).wait()


def _dispatch(h2, eid, rnk, starts):
    t, d = h2.shape
    tm = ROW_TILE
    slot = pl.BlockSpec((TOP_K, tm), lambda i: (0, i), memory_space=pltpu.SMEM)
    return pl.pallas_call(
        _dispatch_kernel,
        grid=(t // tm,),
        in_specs=[
            pl.BlockSpec(memory_space=pltpu.SMEM),
            slot,
            slot,
            pl.BlockSpec((tm, d), lambda i: (i, 0)),
        ],
        out_specs=[pl.BlockSpec(memory_space=pl.ANY), slot],
        out_shape=[jax.ShapeDtypeStruct((t * TOP_K, d), F32), jax.ShapeDtypeStruct((TOP_K, t), jnp.int32)],
        scratch_shapes=[pltpu.SemaphoreType.DMA],
        compiler_params=_cparams(("arbitrary",)),
        name="moe_dispatch",
    )(starts, eid, rnk, h2)


def _grouped_kernel(pb_ref, pe_ref, pv_ref, bnd_ref, xs_ref, wg_ref, wu_ref, wd_ref, y_ref, wgb, wub, wdb):
    i = pl.program_id(0)
    prev = jnp.maximum(i - 1, 0)
    j = pb_ref[i]
    e = pe_ref[i]
    blk = xs_ref.shape[0]

    @pl.when((i == 0) | (pb_ref[prev] != j))
    def _():
        y_ref[...] = jnp.zeros_like(y_ref)

    @pl.when((i == 0) | (pe_ref[prev] != e))
    def _():
        wgb[...] = wg_ref[...].astype(BF16)
        wub[...] = wu_ref[...].astype(BF16)
        wdb[...] = wd_ref[...].astype(BF16)

    @pl.when(pv_ref[i] == 1)
    def _():
        xb = xs_ref[...].astype(BF16)
        a = jnp.dot(xb, wgb[...], preferred_element_type=F32)
        u = jnp.dot(xb, wub[...], preferred_element_type=F32)
        rows = j * blk + lax.broadcasted_iota(jnp.int32, (blk, 1), 0)
        own = (rows >= bnd_ref[e]) & (rows < bnd_ref[e + 1])
        m = jnp.where(own, _silu(a) * u, 0.0).astype(BF16)
        y_ref[...] += jnp.dot(m, wdb[...], preferred_element_type=F32)


def _grouped(pb, pe, pv, bounds, xs, wg, wu, wd):
    p, d = xs.shape
    blk = EXPERT_BLK
    grid_spec = pltpu.PrefetchScalarGridSpec(
        num_scalar_prefetch=4,
        grid=(pb.shape[0],),
        in_specs=[
            pl.BlockSpec((blk, d), lambda i, pb, pe, pv, bnd: (pb[i], 0)),
            pl.BlockSpec((None, d, D_EXPERT), lambda i, pb, pe, pv, bnd: (pe[i], 0, 0)),
            pl.BlockSpec((None, d, D_EXPERT), lambda i, pb, pe, pv, bnd: (pe[i], 0, 0)),
            pl.BlockSpec((None, D_EXPERT, d), lambda i, pb, pe, pv, bnd: (pe[i], 0, 0)),
        ],
        out_specs=pl.BlockSpec((blk, d), lambda i, pb, pe, pv, bnd: (pb[i], 0)),
        scratch_shapes=[
            pltpu.VMEM((d, D_EXPERT), BF16),
            pltpu.VMEM((d, D_EXPERT), BF16),
            pltpu.VMEM((D_EXPERT, d), BF16),
        ],
    )
    return pl.pallas_call(
        _grouped_kernel,
        grid_spec=grid_spec,
        out_shape=jax.ShapeDtypeStruct((p, d), F32),
        compiler_params=_cparams(("arbitrary",)),
        name="moe_grouped_experts",
    )(pb, pe, pv, bounds, xs, wg, wu, wd)


def _combine_kernel(*refs, final):
    pos_ref, y_ref, gate_ref, h_ref, x_ref, g2_ref, sg_ref, su_ref, sd_ref = refs[:9]
    o_ref, buf, sem = refs[-3:]
    tm = x_ref.shape[0]

    def issue(t, carry):
        for k in range(TOP_K):
            pltpu.make_async_copy(y_ref.at[pl.ds(pos_ref[k, t], 1)], buf.at[k, pl.ds(t, 1)], sem).start()
        return carry

    lax.fori_loop(0, tm, issue, 0)
    hb = h_ref[...].astype(BF16)
    a = jnp.dot(hb, sg_ref[...], preferred_element_type=F32)
    u = jnp.dot(hb, su_ref[...], preferred_element_type=F32)
    acc = jnp.dot((_silu(a) * u).astype(BF16), sd_ref[...], preferred_element_type=F32)
    for k in range(TOP_K):
        pltpu.make_async_copy(y_ref.at[pl.ds(0, tm)], buf.at[k], sem).wait()
    g = gate_ref[...]
    for k in range(TOP_K):
        acc = acc + g[:, k:k + 1] * buf[k]
    x = x_ref[...] + g2_ref[...] * acc
    if final:
        gf_ref = refs[9]
        ms = jnp.mean(x * x, axis=-1, keepdims=True)
        x = x * lax.rsqrt(ms + NORM_EPS) * gf_ref[...]
    o_ref[...] = x


def _combine(pos, y, gates, h2, x, mods3, layer, row_fn, sg, su, sd, g_final=None):
    b, r, d = x.shape
    tm = ROW_TILE
    nt = r // tm
    tile = pl.BlockSpec((None, tm, d), lambda bb, i: (bb, i, 0))
    in_specs = [
        pl.BlockSpec((TOP_K, tm), lambda bb, i: (0, bb * nt + i), memory_space=pltpu.SMEM),
        pl.BlockSpec(memory_space=pl.ANY),
        pl.BlockSpec((None, tm, LANES), lambda bb, i: (bb, i, 0)),
        tile,
        tile,
        _mod_spec(d, layer, 5, row_fn),
        pl.BlockSpec(sg.shape, lambda bb, i: (0, 0)),
        pl.BlockSpec(su.shape, lambda bb, i: (0, 0)),
        pl.BlockSpec(sd.shape, lambda bb, i: (0, 0)),
    ]
    args = [pos, y, gates, h2, x, mods3, sg, su, sd]
    if g_final is not None:
        in_specs.append(pl.BlockSpec((1, d), lambda bb, i: (0, 0)))
        args.append(g_final.reshape(1, d))
    return pl.pallas_call(
        functools.partial(_combine_kernel, final=g_final is not None),
        grid=(b, nt),
        in_specs=in_specs,
        out_specs=tile,
        out_shape=jax.ShapeDtypeStruct((b, r, d), F32),
        scratch_shapes=[pltpu.VMEM((TOP_K, tm, d), F32), pltpu.SemaphoreType.DMA],
        compiler_params=_cparams(("arbitrary", "arbitrary")),
        name="moe_combine",
    )(*args)


def _moe(x, g_ffn, mods3, layer, row_fn, w_router, b_router, w_e_gate, w_e_up, w_e_down,
         ws_gate, ws_up, ws_down, g_final=None):
    b, r, d = x.shape
    h2, eid, rnk, gates, cnt = _router(x, g_ffn, mods3, layer, row_fn, w_router.T, b_router)
    pb, pe, pv, bounds = _moe_plan(cnt[:, 0].astype(jnp.int32), b * r * TOP_K)
    xs, pos = _dispatch(h2.reshape(b * r, d), eid, rnk, bounds[:N_EXPERTS])
    y = _grouped(pb, pe, pv, bounds, xs, w_e_gate, w_e_up, w_e_down)
    return _combine(pos, y, gates, h2, x, mods3, layer, row_fn,
                    ws_gate.astype(BF16), ws_up.astype(BF16), ws_down.astype(BF16), g_final)


def kernel(x, c, ctx, c_ctx, w_mod, b_mod, g_mix, g_ffn, g_final, ab_w_in, ab_w_out, ab_conv_w, ab_conv_b, ab_w_r, ab_b_r, ab_w_i, ab_b_i, ab_lam, ab_sink, cd_w_in, cd_w_out, cd_lam, cd_subln_g, cd_conv_w, cd_conv_b, cd_dt_bias, cd_a_log, cd_d_skip, cd_norm_g, w_router, b_router, w_e_gate, w_e_up, w_e_down, ws_gate, ws_up, ws_down):
    bsz, s_len, d = x.shape
    c_len = ctx.shape[1]
    depth = w_mod.shape[0]
    assert depth == 2 and bsz == SUBLANES, "kernels are specialised to depth 2 and batch 8"
    assert c_len % ROW_TILE == 0 and s_len % ROW_TILE == 0
    nct_row = c_len // ROW_TILE
    nct_time = c_len // TIME_TILE

    c_all = jnp.concatenate([c, c_ctx[None], jnp.zeros((MOD_ROWS - bsz - 1, d), F32)], axis=0)
    mods3 = _modulations(c_all, w_mod, b_mod).reshape(depth * MOD_ROWS, 1, N_MOD * d)
    rope_tabs = _rope_tables(c_len, s_len)
    xc = jnp.concatenate([ctx, x], axis=1)
    row_mixed = lambda bb, i: jnp.where(i < nct_row, SUBLANES, bb)
    row_latent = lambda bb, i: bb

    w_in = ab_w_in[0].astype(BF16)
    q_hi = LRU_WIDTH + WIN_HEADS * HEAD_DIM
    x_hi = q_hi + LRU_WIDTH
    k_hi = x_hi + WIN_KV_HEADS * HEAD_DIM
    gate, q, xa, k, v = _project(xc, g_mix[0], mods3, 0, nct_row, rope_tabs, [
        (w_in[:, :LRU_WIDTH], False, F32, False),
        (w_in[:, LRU_WIDTH:q_hi], True, BF16, False),
        (w_in[:, q_hi:x_hi], False, F32, True),
        (w_in[:, x_hi:k_hi], True, BF16, False),
        (w_in[:, k_hi:], False, BF16, False),
    ])
    l_len = c_len + s_len
    w_gates = jnp.stack([jnp.concatenate([_block_diag(ab_w_r[0, dd]), _block_diag(ab_w_i[0, dd])], axis=1)
                         for dd in range(2)]).astype(BF16)
    b_gates = jnp.concatenate([ab_b_r[0], ab_b_i[0]], axis=-1).reshape(2, 1, 2 * LRU_WIDTH)
    rec = _rglru(xa.reshape(l_len, bsz, LRU_WIDTH), ab_conv_w[0], ab_conv_b[0], w_gates, b_gates,
                 ab_lam[0].reshape(2, 1, LRU_WIDTH), nct_time)
    att = _win_attention(q, k, v, ab_sink[0], c_len)
    w_out = ab_w_out[0].astype(BF16)
    xc = _out_even(xc, rec.reshape(2, l_len, bsz * LRU_WIDTH), gate, att, w_out[:LRU_WIDTH], w_out[LRU_WIDTH:],
                   mods3, 0, nct_row)
    xc = _moe(xc, g_ffn[0], mods3, 0, row_mixed, w_router[0], b_router[0], w_e_gate[0], w_e_up[0], w_e_down[0],
              ws_gate[0], ws_up[0], ws_down[0])

    w_in = cd_w_in[0].astype(BF16)
    qk = DIFF_HEADS * 2 * DIFF_DH
    z_hi = qk + SSD_INNER
    k_hi = z_hi + qk
    v_hi = k_hi + qk
    x_hi = v_hi + SSD_CONV_DIM
    w_dt = jnp.pad(w_in[:, x_hi:], ((0, 0), (0, LANES - 2 * SSD_HEADS)))
    q, z, k, v, xbc, dt = _project(xc, g_mix[1], mods3, 1, nct_row, rope_tabs, [
        (w_in[:, :qk], True, BF16, False),
        (w_in[:, qk:z_hi], False, F32, False),
        (w_in[:, z_hi:k_hi], True, BF16, False),
        (w_in[:, k_hi:v_hi], False, BF16, False),
        (w_in[:, v_hi:x_hi], False, F32, False),
        (w_dt, False, F32, False),
    ])
    lam_init = 0.8 - 0.6 * math.exp(-0.3 * 1)
    diff = _diff_attention(q, k, v, cd_lam[0], cd_subln_g[0], lam_init, c_len)
    y2 = _ssd(xbc, dt, cd_conv_w[0], cd_conv_b[0], cd_dt_bias[0], cd_a_log[0], cd_d_skip[0], nct_time)
    w_out = cd_w_out[0].astype(BF16)
    xl = _out_odd(xc, diff, y2, z, cd_norm_g[0], w_out[:qk], w_out[qk:], mods3, 1, c_len)
    return _moe(xl, g_ffn[1], mods3, 1, row_latent, w_router[1], b_router[1], w_e_gate[1], w_e_up[1], w_e_down[1],
                ws_gate[1], ws_up[1], ws_down[1], g_final=g_final)
```

```python
import functools
import math

import jax
import jax.numpy as jnp
from jax import lax
from jax.experimental import pallas as pl
from jax.experimental.pallas import tpu as pltpu

F32 = jnp.float32
BF16 = jnp.bfloat16
HIGHEST = lax.Precision.HIGHEST

GRID_W = 64
N_MOD = 6
NORM_EPS = 1e-6
ROPE_BASE = 10000.0
CONV_W = 4

LRU_WIDTH = 512
LRU_BLOCKS = 8
LRU_C = 8.0

HEAD_DIM = 64
WIN_HEADS = 8
WIN_KV_HEADS = 2
WINDOW = 128

DIFF_HEADS = 4
DIFF_DH = 64

SSD_HEADS = 8
SSD_HEAD_DIM = 64
SSD_INNER = SSD_HEADS * SSD_HEAD_DIM
SSD_GROUPS = 2
SSD_STATE = 128
SSD_CONV_DIM = SSD_INNER + 2 * SSD_GROUPS * SSD_STATE

N_EXPERTS = 64
N_EXPERT_GROUPS = 8
TOPK_GROUPS = 4
TOP_K = 8
D_EXPERT = 256
ROUTED_SCALE = 2.5

LANES = 128
SUBLANES = 8
MOD_ROWS = 16
TIME_TILE = 128
ROW_TILE = 256
EXPERT_BLK = 256
VMEM_LIMIT = 48 * 1024 * 1024
NEG_BIG = -1e30


def _cparams(sem):
    return pltpu.CompilerParams(dimension_semantics=sem, vmem_limit_bytes=VMEM_LIMIT)


def _nt_dot(a, b):
    return lax.dot_general(a, b, (((1,), (1,)), ((), ())), preferred_element_type=F32)


def _softplus(x):
    return jnp.maximum(x, 0.0) + jnp.log1p(jnp.exp(-jnp.abs(x)))


def _silu(x):
    return x * jax.nn.sigmoid(x)


def _pack_bf16(x):
    half = x.shape[-1] // 2
    bits = pltpu.bitcast(x.astype(BF16).astype(F32), jnp.uint32)
    return bits[:, :half] | (bits[:, half:] >> 16)


def _unpack_bf16(w):
    hi = pltpu.bitcast(w & jnp.uint32(0xFFFF0000), F32)
    lo = pltpu.bitcast(w << 16, F32)
    return hi, lo


def _packed_dot(w, weight_ref):
    half = w.shape[-1]
    hi, lo = _unpack_bf16(w)
    return (jnp.dot(hi.astype(BF16), weight_ref[:half, :], preferred_element_type=F32)
            + jnp.dot(lo.astype(BF16), weight_ref[half:, :], preferred_element_type=F32))


def _mod_kernel(c_ref, w_ref, b_ref, o_ref):
    c = c_ref[...]
    s = _silu(c)
    o_ref[...] = jnp.dot(s, w_ref[...], preferred_element_type=F32, precision=HIGHEST) + b_ref[...]


def _modulations(c_all, w_mod, b_mod):
    depth, d, _ = w_mod.shape
    return pl.pallas_call(
        _mod_kernel,
        grid=(depth, N_MOD),
        in_specs=[
            pl.BlockSpec((MOD_ROWS, d), lambda l, k: (0, 0)),
            pl.BlockSpec((None, d, d), lambda l, k: (l, 0, k)),
            pl.BlockSpec((None, 1, d), lambda l, k: (l, 0, k)),
        ],
        out_specs=pl.BlockSpec((None, MOD_ROWS, d), lambda l, k: (l, 0, k)),
        out_shape=jax.ShapeDtypeStruct((depth, MOD_ROWS, N_MOD * d), F32),
        compiler_params=_cparams(("arbitrary", "arbitrary")),
        name="adaln_modulation",
    )(c_all, w_mod, b_mod.reshape(depth, 1, N_MOD * d))


def _mod_spec(d, layer, chunk, row_fn):
    return pl.BlockSpec((None, 1, d), lambda b, i: (layer * MOD_ROWS + row_fn(b, i), 0, chunk))


def _norm_mod(x, g, sh, sc):
    ms = jnp.mean(x * x, axis=-1, keepdims=True)
    return (x * lax.rsqrt(ms + NORM_EPS) * g) * (1.0 + sc) + sh


def _rope(y, cos, sa, sb):
    n = y.shape[-1]
    half = HEAD_DIM // 2
    return y * cos + pltpu.roll(y, n - half, 1) * sa + pltpu.roll(y, half, 1) * sb


def _proj_kernel(*refs, ropes):
    n = len(ropes)
    x_ref, g_ref, sh_ref, sc_ref, cos_ref, sa_ref, sb_ref = refs[:7]
    w_refs = refs[7:7 + n]
    o_refs = refs[7 + n:]
    h = _norm_mod(x_ref[...], g_ref[...], sh_ref[...], sc_ref[...]).astype(BF16)
    for w_ref, o_ref, rope in zip(w_refs, o_refs, ropes):
        y = jnp.dot(h, w_ref[...], preferred_element_type=F32)
        if rope:
            w = y.shape[-1]
            y = _rope(y, cos_ref[:, :w], sa_ref[:, :w], sb_ref[:, :w])
        o_ref[...] = y.astype(o_ref.dtype)


def _project(xc, g, mods3, layer, nct, rope_tabs, groups):
    b, l, d = xc.shape
    tm = ROW_TILE
    row = lambda bb, i: jnp.where(i < nct, SUBLANES, bb)
    rw = rope_tabs[0].shape[-1]
    in_specs = [
        pl.BlockSpec((None, tm, d), lambda bb, i: (bb, i, 0)),
        pl.BlockSpec((1, d), lambda bb, i: (0, 0)),
        _mod_spec(d, layer, 0, row),
        _mod_spec(d, layer, 1, row),
    ] + [pl.BlockSpec((tm, rw), lambda bb, i: (i, 0))] * 3
    out_specs, out_shapes = [], []
    for w, _, dt, time_major in groups:
        n = w.shape[1]
        in_specs.append(pl.BlockSpec((d, n), lambda bb, i: (0, 0)))
        if time_major:
            out_specs.append(pl.BlockSpec((tm, n), lambda bb, i: (i, bb)))
            out_shapes.append(jax.ShapeDtypeStruct((l, b * n), dt))
        else:
            out_specs.append(pl.BlockSpec((None, tm, n), lambda bb, i: (bb, i, 0)))
            out_shapes.append(jax.ShapeDtypeStruct((b, l, n), dt))
    return pl.pallas_call(
        functools.partial(_proj_kernel, ropes=tuple(gp[1] for gp in groups)),
        grid=(b, l // tm),
        in_specs=in_specs,
        out_specs=out_specs,
        out_shape=out_shapes,
        compiler_params=_cparams(("arbitrary", "arbitrary")),
        name="norm_mod_project",
    )(xc, g.reshape(1, d), mods3, mods3, *rope_tabs, *[gp[0] for gp in groups])


def _rope_tables(c_len, s_len):
    rows = s_len // GRID_W
    row = jnp.repeat(jnp.arange(rows), GRID_W).astype(F32)
    col = jnp.tile(jnp.arange(GRID_W), rows).astype(F32)
    n = HEAD_DIM // 4
    inv = ROPE_BASE ** (-jnp.arange(n, dtype=F32) / n)
    ang = jnp.concatenate([row[:, None] * inv, col[:, None] * inv], axis=-1)
    cos, sin = jnp.cos(ang), jnp.sin(ang)
    zero = jnp.zeros_like(sin)
    reps = WIN_HEADS
    cos_t = jnp.tile(jnp.concatenate([cos, cos], axis=-1), (1, reps))
    sa_t = jnp.tile(jnp.concatenate([-sin, zero], axis=-1), (1, reps))
    sb_t = jnp.tile(jnp.concatenate([zero, sin], axis=-1), (1, reps))
    w = cos_t.shape[-1]
    pad1 = jnp.ones((c_len, w), F32)
    pad0 = jnp.zeros((c_len, w), F32)
    return (jnp.concatenate([pad1, cos_t], 0), jnp.concatenate([pad0, sa_t], 0),
            jnp.concatenate([pad0, sb_t], 0))


def _seq_tile(d, g, nct, nt):
    rev = jnp.where(g < nct, nct - 1 - g, nt - 1 - (g - nct))
    return jnp.where(d == 0, g, rev)


def _rglru_kernel(x_ref, xp_ref, xn_ref, cw_ref, cb_ref, w_ref, bias_ref, lam_ref, o_ref,
                  ext_scr, a_scr, b_scr, h_scr, *, ts, nct, nt, sub):
    d = pl.program_id(0)
    g = pl.program_id(1)
    tile = _seq_tile(d, g, nct, nt)
    bsz, width = h_scr.shape
    pv = jnp.where((tile == 0) | (tile == nct), 0.0, 1.0)
    nv = jnp.where((tile == nct - 1) | (tile == nt - 1), 0.0, 1.0)
    ext_scr[0:1] = xp_ref[...] * pv
    ext_scr[1:ts + 1] = x_ref[...]
    ext_scr[ts + 1:ts + 3] = xn_ref[...] * nv

    @pl.when(g == 0)
    def _():
        h_scr[...] = jnp.zeros_like(h_scr)

    neg_sp = -LRU_C * _softplus(-lam_ref[...])

    def prep(c, carry):
        r0 = pl.multiple_of(c * sub, sub)
        e = ext_scr[pl.ds(r0, sub + CONV_W - 1)]
        u = cb_ref[...] + cw_ref[0] * e[0:sub]
        for j in range(1, CONV_W):
            u = u + cw_ref[j] * e[j:j + sub]
        u2 = u.reshape(sub * bsz, width)
        gts = jnp.dot(u2.astype(BF16), w_ref[...], preferred_element_type=F32) + bias_ref[...]
        r = jax.nn.sigmoid(gts[:, :width])
        ig = jax.nn.sigmoid(gts[:, width:])
        log_a = neg_sp * r
        a = jnp.exp(log_a)
        mult = jnp.sqrt(1.0 - a * a)
        a_scr[pl.ds(r0, sub)] = a.reshape(sub, bsz, width)
        b_scr[pl.ds(r0, sub)] = (mult * ig * u2).reshape(sub, bsz, width)
        return carry

    lax.fori_loop(0, ts // sub, prep, 0)

    def step(t, h):
        tt = jnp.where(d == 0, t, ts - 1 - t)
        h = a_scr[tt] * h + b_scr[tt]
        o_ref[tt] = h
        return h

    h_scr[...] = lax.fori_loop(0, ts, step, h_scr[...], unroll=8)


def _rglru(xa_tm, conv_w, conv_b, w_gates, b_gates, lam, nct):
    l, bsz, width = xa_tm.shape
    ts = TIME_TILE
    nt = l // ts
    tile = lambda d, g: _seq_tile(d, g, nct, nt)
    kern = functools.partial(_rglru_kernel, ts=ts, nct=nct, nt=nt, sub=16)
    return pl.pallas_call(
        kern,
        grid=(2, nt),
        in_specs=[
            pl.BlockSpec((ts, bsz, width), lambda d, g: (tile(d, g), 0, 0)),
            pl.BlockSpec((1, bsz, width), lambda d, g: (jnp.maximum(tile(d, g) * ts - 1, 0), 0, 0)),
            pl.BlockSpec((2, bsz, width),
                         lambda d, g: (jnp.minimum((tile(d, g) + 1) * (ts // 2), l // 2 - 1), 0, 0)),
            pl.BlockSpec((CONV_W, 1, width), lambda d, g: (0, 0, 0)),
            pl.BlockSpec((1, width), lambda d, g: (0, 0)),
            pl.BlockSpec((None, width, 2 * width), lambda d, g: (d, 0, 0)),
            pl.BlockSpec((None, 1, 2 * width), lambda d, g: (d, 0, 0)),
            pl.BlockSpec((None, 1, width), lambda d, g: (d, 0, 0)),
        ],
        out_specs=pl.BlockSpec((None, ts, bsz, width), lambda d, g: (d, tile(d, g), 0, 0)),
        out_shape=jax.ShapeDtypeStruct((2, l, bsz, width), F32),
        scratch_shapes=[
            pltpu.VMEM((ts + CONV_W - 1, bsz, width), F32),
            pltpu.VMEM((ts, bsz, width), F32),
            pltpu.VMEM((ts, bsz, width), F32),
            pltpu.VMEM((bsz, width), F32),
        ],
        compiler_params=_cparams(("arbitrary", "arbitrary")),
        name="rglru_scan",
    )(xa_tm, xa_tm, xa_tm, conv_w.reshape(CONV_W, 1, width), conv_b.reshape(1, width),
      w_gates, b_gates, lam)


def _block_diag(w):
    nb, c, dd = w.shape
    eye = jnp.eye(nb, dtype=w.dtype)
    return (eye[:, None, :, None] * w[:, :, None, :]).reshape(nb * c, nb * dd)


def _win_attn_kernel(sink_ref, q_ref, k_ref, v_ref, o_ref, *, c_len, l_len, nqc):
    j = pl.program_id(1)
    blk = q_ref.shape[0]
    grp = WIN_HEADS // WIN_KV_HEADS
    band = 3 * blk

    def heads(body):
        for hk in range(WIN_KV_HEADS):
            ksl = slice(hk * HEAD_DIM, (hk + 1) * HEAD_DIM)
            for gq in range(grp):
                head = hk * grp + gq
                hsl = slice(head * HEAD_DIM, (head + 1) * HEAD_DIM)
                o_ref[:, hsl] = body(q_ref[:, hsl], ksl, sink_ref[head]).astype(o_ref.dtype)

    @pl.when(j < nqc)
    def _():
        def body(qh, ksl, sink):
            s = _nt_dot(qh, k_ref[0:c_len, ksl]) * (HEAD_DIM ** -0.5)
            m = jnp.maximum(jnp.max(s, axis=-1, keepdims=True), sink)
            p = jnp.exp(s - m)
            den = jnp.sum(p, axis=-1, keepdims=True) + jnp.exp(sink - m)
            o = jnp.dot(p.astype(BF16), v_ref[0:c_len, ksl], preferred_element_type=F32)
            return o / den
        heads(body)

    @pl.when(j >= nqc)
    def _():
        jb = j - nqc
        start = jnp.clip(c_len + (jb - 1) * blk, c_len - blk, l_len - band)
        start = pl.multiple_of(start, blk)
        qpos = jb * blk + lax.broadcasted_iota(jnp.int32, (blk, band), 0)
        kpos = start - c_len + lax.broadcasted_iota(jnp.int32, (blk, band), 1)
        valid = (jnp.abs(qpos - kpos) <= WINDOW) & (kpos >= 0)

        def body(qh, ksl, sink):
            sc = _nt_dot(qh, k_ref[0:c_len, ksl]) * (HEAD_DIM ** -0.5)
            sb = _nt_dot(qh, k_ref[pl.ds(start, band), ksl]) * (HEAD_DIM ** -0.5)
            sb = jnp.where(valid, sb, NEG_BIG)
            m = jnp.maximum(jnp.maximum(jnp.max(sc, axis=-1, keepdims=True),
                                        jnp.max(sb, axis=-1, keepdims=True)), sink)
            pc = jnp.exp(sc - m)
            pb = jnp.exp(sb - m)
            den = (jnp.sum(pc, axis=-1, keepdims=True) + jnp.sum(pb, axis=-1, keepdims=True)
                   + jnp.exp(sink - m))
            o = (jnp.dot(pc.astype(BF16), v_ref[0:c_len, ksl], preferred_element_type=F32)
                 + jnp.dot(pb.astype(BF16), v_ref[pl.ds(start, band), ksl], preferred_element_type=F32))
            return o / den
        heads(body)


def _win_attention(q, k, v, sink, c_len):
    b, l, qw = q.shape
    kw = k.shape[-1]
    blk = TIME_TILE
    kern = functools.partial(_win_attn_kernel, c_len=c_len, l_len=l, nqc=c_len // blk)
    return pl.pallas_call(
        kern,
        grid=(b, l // blk),
        in_specs=[
            pl.BlockSpec(memory_space=pltpu.SMEM),
            pl.BlockSpec((None, blk, qw), lambda bb, j: (bb, j, 0)),
            pl.BlockSpec((None, l, kw), lambda bb, j: (bb, 0, 0)),
            pl.BlockSpec((None, l, kw), lambda bb, j: (bb, 0, 0)),
        ],
        out_specs=pl.BlockSpec((None, blk, qw), lambda bb, j: (bb, j, 0)),
        out_shape=jax.ShapeDtypeStruct((b, l, qw), BF16),
        compiler_params=_cparams(("arbitrary", "arbitrary")),
        name="window_attention",
    )(sink, q, k, v)


def _out_even_kernel(x_ref, rec_ref, gate_ref, att_ref, wa_ref, wb_ref, g1_ref, o_ref):
    lru = (rec_ref[0] + rec_ref[1]) * jax.nn.gelu(gate_ref[...])
    y = (jnp.dot(lru.astype(BF16), wa_ref[...], preferred_element_type=F32)
         + jnp.dot(att_ref[...], wb_ref[...], preferred_element_type=F32))
    o_ref[...] = x_ref[...] + g1_ref[...] * y


def _out_even(xc, rec2, gate, att, w_a, w_b, mods3, layer, nct):
    b, l, d = xc.shape
    tm = ROW_TILE
    w = gate.shape[-1]
    row = lambda bb, i: jnp.where(i < nct, SUBLANES, bb)
    return pl.pallas_call(
        _out_even_kernel,
        grid=(b, l // tm),
        in_specs=[
            pl.BlockSpec((None, tm, d), lambda bb, i: (bb, i, 0)),
            pl.BlockSpec((2, tm, w), lambda bb, i: (0, i, bb)),
            pl.BlockSpec((None, tm, w), lambda bb, i: (bb, i, 0)),
            pl.BlockSpec((None, tm, att.shape[-1]), lambda bb, i: (bb, i, 0)),
            pl.BlockSpec(w_a.shape, lambda bb, i: (0, 0)),
            pl.BlockSpec(w_b.shape, lambda bb, i: (0, 0)),
            _mod_spec(d, layer, 2, row),
        ],
        out_specs=pl.BlockSpec((None, tm, d), lambda bb, i: (bb, i, 0)),
        out_shape=jax.ShapeDtypeStruct((b, l, d), F32),
        compiler_params=_cparams(("arbitrary", "arbitrary")),
        name="out_proj_even",
    )(xc, rec2, gate, att, w_a, w_b, mods3)


def _diff_attn_kernel(lam_ref, g_ref, q_ref, k_ref, v_ref, o_ref, *, lam_init):
    lv = lam_ref[...]
    lam = (jnp.exp(jnp.sum(lv[0:1] * lv[1:2], axis=-1, keepdims=True))
           - jnp.exp(jnp.sum(lv[2:3] * lv[3:4], axis=-1, keepdims=True)) + lam_init)
    vw = 2 * DIFF_DH

    def softmax(qm, km):
        s = _nt_dot(qm, km) * (DIFF_DH ** -0.5)
        m = jnp.max(s, axis=-1, keepdims=True)
        p = jnp.exp(s - m)
        return p / jnp.sum(p, axis=-1, keepdims=True)

    for h in range(DIFF_HEADS):
        lo = h * vw
        p0 = softmax(q_ref[:, lo:lo + DIFF_DH], k_ref[:, lo:lo + DIFF_DH])
        p1 = softmax(q_ref[:, lo + DIFF_DH:lo + vw], k_ref[:, lo + DIFF_DH:lo + vw])
        w = (p0 - lam * p1).astype(BF16)
        o = jnp.dot(w, v_ref[:, lo:lo + vw], preferred_element_type=F32)
        ms = jnp.mean(o * o, axis=-1, keepdims=True)
        o = o * lax.rsqrt(ms + NORM_EPS) * g_ref[...]
        o_ref[:, lo:lo + vw] = (o * (1.0 - lam_init)).astype(o_ref.dtype)


def _diff_attention(q, k, v, lam_vecs, subln_g, lam_init, c_len):
    b, l, w = q.shape
    tq = TIME_TILE
    s_len = l - c_len
    off = c_len // tq
    return pl.pallas_call(
        functools.partial(_diff_attn_kernel, lam_init=lam_init),
        grid=(b, s_len // tq),
        in_specs=[
            pl.BlockSpec(lam_vecs.shape, lambda bb, j: (0, 0)),
            pl.BlockSpec((1, 2 * DIFF_DH), lambda bb, j: (0, 0)),
            pl.BlockSpec((None, tq, w), lambda bb, j: (bb, j + off, 0)),
            pl.BlockSpec((None, l, w), lambda bb, j: (bb, 0, 0)),
            pl.BlockSpec((None, l, w), lambda bb, j: (bb, 0, 0)),
        ],
        out_specs=pl.BlockSpec((None, tq, w), lambda bb, j: (bb, j, 0)),
        out_shape=jax.ShapeDtypeStruct((b, s_len, w), BF16),
        compiler_params=_cparams(("arbitrary", "arbitrary")),
        name="diff_attention",
    )(lam_vecs, subln_g.reshape(1, -1), q, k, v)


def _ssd_kernel(x_ref, xp_ref, xn_ref, dt_ref, cw_ref, cb_ref, dtb_ref, alog_ref, dsk_ref, o_ref,
                ext_scr, st_scr, *, q, nct, nt):
    d = pl.program_id(0)
    g = pl.program_id(2)
    tile = _seq_tile(d, g, nct, nt)
    pv = jnp.where((tile == 0) | (tile == nct), 0.0, 1.0)
    nv = jnp.where((tile == nct - 1) | (tile == nt - 1), 0.0, 1.0)
    ext_scr[0:SUBLANES] = xp_ref[...] * pv
    ext_scr[SUBLANES:SUBLANES + q] = x_ref[...]
    ext_scr[SUBLANES + q:2 * SUBLANES + q] = xn_ref[...] * nv

    @pl.when(g == 0)
    def _():
        st_scr[...] = jnp.zeros_like(st_scr)

    u = cb_ref[...] + cw_ref[0] * ext_scr[SUBLANES - 1:SUBLANES - 1 + q, :]
    for j in range(1, CONV_W):
        u = u + cw_ref[j] * ext_scr[SUBLANES - 1 + j:SUBLANES - 1 + j + q, :]
    act = _silu(u)

    dtr = dt_ref[...]
    dtr = jnp.where(d == 0, dtr, pltpu.roll(dtr, LANES - SSD_HEADS, 1))
    dtv = _softplus(dtr + dtb_ref[...])
    head_lane = lax.broadcasted_iota(jnp.int32, (1, LANES), 1) < SSD_HEADS
    dta = dtv * jnp.where(head_lane, -jnp.exp(alog_ref[...]), 0.0)
    ri = lax.broadcasted_iota(jnp.int32, (q, q), 0)
    ci = lax.broadcasted_iota(jnp.int32, (q, q), 1)
    keep = jnp.where(d == 0, ri - ci, ci - ri) >= 0
    cum = jnp.dot(keep.astype(F32), dta, preferred_element_type=F32, precision=HIGHEST)
    tot = jnp.sum(dta, axis=0, keepdims=True)
    cum_t = cum.T
    dt_t = dtv.T
    to_end = jnp.exp(tot - cum) * dtv
    e_cum = jnp.exp(cum)
    e_tot = jnp.exp(tot)
    dskip = dsk_ref[...] * jnp.where(d == 0, 1.0, 0.0)

    hpg = SSD_HEADS // SSD_GROUPS
    for gi in range(SSD_GROUPS):
        b_g = act[:, SSD_INNER + gi * SSD_STATE:SSD_INNER + (gi + 1) * SSD_STATE]
        c_lo = SSD_INNER + SSD_GROUPS * SSD_STATE + gi * SSD_STATE
        c_g = act[:, c_lo:c_lo + SSD_STATE].astype(BF16)
        cb = _nt_dot(c_g, b_g.astype(BF16))
        b_gt = b_g.T.astype(BF16)
        for hh in range(hpg):
            h = gi * hpg + hh
            xs = act[:, h * SSD_HEAD_DIM:(h + 1) * SSD_HEAD_DIM]
            seg = cum[:, h:h + 1] - cum_t[h:h + 1, :]
            decay = jnp.exp(jnp.where(keep, seg, NEG_BIG))
            w = (cb * decay * dt_t[h:h + 1, :]).astype(BF16)
            state = st_scr[h]
            y = jnp.dot(w, xs.astype(BF16), preferred_element_type=F32)
            y = y + jnp.dot(c_g, state.astype(BF16), preferred_element_type=F32) * e_cum[:, h:h + 1]
            y = y + dskip[:, h * SSD_HEAD_DIM:(h + 1) * SSD_HEAD_DIM] * xs
            o_ref[:, h * SSD_HEAD_DIM:(h + 1) * SSD_HEAD_DIM] = y
            s_new = jnp.dot(b_gt, (xs * to_end[:, h:h + 1]).astype(BF16), preferred_element_type=F32)
            st_scr[h] = e_tot[:, h:h + 1] * state + s_new


def _ssd(xbc, dt, conv_w, conv_b, dt_bias, a_log, d_skip, nct):
    b, l, cd = xbc.shape
    q = TIME_TILE
    nt = l // q
    tile = lambda d, bb, g: _seq_tile(d, g, nct, nt)
    r8 = q // SUBLANES
    pad = LANES - SSD_HEADS
    dtb = jnp.pad(dt_bias, ((0, 0), (0, pad))).reshape(2, 1, LANES)
    alog = jnp.pad(a_log, ((0, 0), (0, pad))).reshape(2, 1, LANES)
    dsk = jnp.repeat(d_skip, SSD_HEAD_DIM).reshape(1, SSD_INNER)
    return pl.pallas_call(
        functools.partial(_ssd_kernel, q=q, nct=nct, nt=nt),
        grid=(2, b, nt),
        in_specs=[
            pl.BlockSpec((None, q, cd), lambda d, bb, g: (bb, tile(d, bb, g), 0)),
            pl.BlockSpec((None, SUBLANES, cd),
                         lambda d, bb, g: (bb, jnp.maximum(tile(d, bb, g) * r8 - 1, 0), 0)),
            pl.BlockSpec((None, SUBLANES, cd),
                         lambda d, bb, g: (bb, jnp.minimum((tile(d, bb, g) + 1) * r8, l // SUBLANES - 1), 0)),
            pl.BlockSpec((None, q, LANES), lambda d, bb, g: (bb, tile(d, bb, g), 0)),
            pl.BlockSpec((CONV_W, 1, cd), lambda d, bb, g: (0, 0, 0)),
            pl.BlockSpec((1, cd), lambda d, bb, g: (0, 0)),
            pl.BlockSpec((None, 1, LANES), lambda d, bb, g: (d, 0, 0)),
            pl.BlockSpec((None, 1, LANES), lambda d, bb, g: (d, 0, 0)),
            pl.BlockSpec((1, SSD_INNER), lambda d, bb, g: (0, 0)),
        ],
        out_specs=pl.BlockSpec((None, None, q, SSD_INNER), lambda d, bb, g: (d, bb, tile(d, bb, g), 0)),
        out_shape=jax.ShapeDtypeStruct((2, b, l, SSD_INNER), F32),
        scratch_shapes=[
            pltpu.VMEM((q + 2 * SUBLANES, cd), F32),
            pltpu.VMEM((SSD_HEADS, SSD_STATE, SSD_HEAD_DIM), F32),
        ],
        compiler_params=_cparams(("arbitrary", "arbitrary", "arbitrary")),
        name="ssd_chunked",
    )(xbc, xbc, xbc, dt, conv_w.reshape(CONV_W, 1, cd), conv_b.reshape(1, cd), dtb, alog, dsk)


def _out_odd_kernel(x_ref, diff_ref, y_ref, z_ref, ng_ref, wa_ref, wb_ref, g1_ref, o_ref):
    yz = (y_ref[0] + y_ref[1]) * _silu(z_ref[...])
    gs = SSD_INNER // SSD_GROUPS
    parts = []
    for gi in range(SSD_GROUPS):
        seg = yz[:, gi * gs:(gi + 1) * gs]
        ms = jnp.mean(seg * seg, axis=-1, keepdims=True)
        parts.append(seg * lax.rsqrt(ms + NORM_EPS) * ng_ref[:, gi * gs:(gi + 1) * gs])
    ssd = jnp.concatenate(parts, axis=-1).astype(BF16)
    y = (jnp.dot(diff_ref[...], wa_ref[...], preferred_element_type=F32)
         + jnp.dot(ssd, wb_ref[...], preferred_element_type=F32))
    o_ref[...] = x_ref[...] + g1_ref[...] * y


def _out_odd(xc, diff, y2, z, norm_g, w_a, w_b, mods3, layer, c_len):
    b, l, d = xc.shape
    s_len = l - c_len
    tm = ROW_TILE
    off = c_len // tm
    w = SSD_INNER
    row = lambda bb, i: bb
    return pl.pallas_call(
        _out_odd_kernel,
        grid=(b, s_len // tm),
        in_specs=[
            pl.BlockSpec((None, tm, d), lambda bb, i: (bb, i + off, 0)),
            pl.BlockSpec((None, tm, diff.shape[-1]), lambda bb, i: (bb, i, 0)),
            pl.BlockSpec((2, None, tm, w), lambda bb, i: (0, bb, i + off, 0)),
            pl.BlockSpec((None, tm, w), lambda bb, i: (bb, i + off, 0)),
            pl.BlockSpec((1, w), lambda bb, i: (0, 0)),
            pl.BlockSpec(w_a.shape, lambda bb, i: (0, 0)),
            pl.BlockSpec(w_b.shape, lambda bb, i: (0, 0)),
            _mod_spec(d, layer, 2, row),
        ],
        out_specs=pl.BlockSpec((None, tm, d), lambda bb, i: (bb, i, 0)),
        out_shape=jax.ShapeDtypeStruct((b, s_len, d), F32),
        compiler_params=_cparams(("arbitrary", "arbitrary")),
        name="out_proj_odd",
    )(xc, diff, y2, z, norm_g.reshape(1, w), w_a, w_b, mods3)


def _router_kernel(x_ref, g_ref, sh_ref, sc_ref, wr_ref, br_ref, h_ref, eid_ref, rnk_ref, gate_ref, cnt_ref,
                   carry_scr):
    @pl.when((pl.program_id(0) == 0) & (pl.program_id(1) == 0))
    def _():
        carry_scr[...] = jnp.zeros_like(carry_scr)

    h = _norm_mod(x_ref[...], g_ref[...], sh_ref[...], sc_ref[...])
    h_ref[...] = _pack_bf16(h)
    tm = h.shape[0]
    per = N_EXPERTS // N_EXPERT_GROUPS
    logits = lax.dot_general(wr_ref[...], h, (((1,), (1,)), ((), ())),
                             preferred_element_type=F32, precision=HIGHEST)
    scores = jax.nn.sigmoid(logits)
    sel = scores + br_ref[...]
    sel3 = sel.reshape(N_EXPERT_GROUPS, per, tm)
    kio = lax.broadcasted_iota(jnp.int32, sel3.shape, 1)
    m1 = jnp.max(sel3, axis=1, keepdims=True)
    first = jnp.min(jnp.where(sel3 == m1, kio, per), axis=1, keepdims=True)
    m2 = jnp.max(jnp.where(kio == first, NEG_BIG, sel3), axis=1, keepdims=True)
    gs = m1 + m2
    gio = lax.broadcasted_iota(jnp.int32, gs.shape, 0)
    ahead = jnp.zeros(gs.shape, jnp.int32)
    for gp in range(N_EXPERT_GROUPS):
        other = gs[gp:gp + 1]
        ahead = ahead + jnp.where((other > gs) | ((other == gs) & (gp < gio)), 1, 0)
    grp_on = jnp.where(ahead < TOPK_GROUPS, 1.0, 0.0)
    selm = jnp.where(jnp.broadcast_to(grp_on, sel3.shape) > 0.5, sel3, NEG_BIG).reshape(N_EXPERTS, tm)
    eio = lax.broadcasted_iota(jnp.int32, selm.shape, 0)
    rank = jnp.zeros(selm.shape, jnp.int32)
    for e in range(N_EXPERTS):
        other = selm[e:e + 1, :]
        rank = rank + jnp.where((other > selm) | ((other == selm) & (e < eio)), 1, 0)
    chosen = rank < TOP_K
    gate = jnp.where(chosen, scores, 0.0)
    gate = gate / jnp.sum(gate, axis=0, keepdims=True) * ROUTED_SCALE
    cf = jnp.where(chosen, 1.0, 0.0)
    ti = lax.broadcasted_iota(jnp.int32, (tm, tm), 0)
    tj = lax.broadcasted_iota(jnp.int32, (tm, tm), 1)
    before = jnp.where(ti < tj, 1.0, 0.0).astype(BF16)
    in_expert = carry_scr[:, 0:1] + jnp.dot(cf.astype(BF16), before, preferred_element_type=F32)
    carry_scr[...] = carry_scr[...] + jnp.sum(cf, axis=1, keepdims=True)
    cnt_ref[...] = carry_scr[...]
    eio_f = eio.astype(F32)
    e_rows, r_rows, g_rows = [], [], []
    for k in range(TOP_K):
        hit = jnp.where(rank == k, 1.0, 0.0)
        e_rows.append(jnp.sum(hit * eio_f, axis=0, keepdims=True))
        r_rows.append(jnp.sum(hit * in_expert, axis=0, keepdims=True))
        g_rows.append(jnp.sum(hit * gate, axis=0, keepdims=True))
    eid_ref[...] = jnp.concatenate(e_rows, axis=0).astype(jnp.int32)
    rnk_ref[...] = jnp.concatenate(r_rows, axis=0).astype(jnp.int32)
    padded = jnp.concatenate(g_rows + [jnp.zeros((LANES - TOP_K, tm), F32)], axis=0)
    gate_ref[...] = padded.T


def _router(x, g, mods3, layer, row_fn, w_router_t, b_router):
    b, r, d = x.shape
    tm = ROW_TILE
    nt = r // tm
    slot = pl.BlockSpec((TOP_K, tm), lambda bb, i: (0, bb * nt + i))
    slot_shape = jax.ShapeDtypeStruct((TOP_K, b * r), jnp.int32)
    return pl.pallas_call(
        _router_kernel,
        grid=(b, nt),
        in_specs=[
            pl.BlockSpec((None, tm, d), lambda bb, i: (bb, i, 0)),
            pl.BlockSpec((1, d), lambda bb, i: (0, 0)),
            _mod_spec(d, layer, 3, row_fn),
            _mod_spec(d, layer, 4, row_fn),
            pl.BlockSpec(w_router_t.shape, lambda bb, i: (0, 0)),
            pl.BlockSpec((N_EXPERTS, 1), lambda bb, i: (0, 0)),
        ],
        out_specs=[
            pl.BlockSpec((None, tm, d // 2), lambda bb, i: (bb, i, 0)),
            slot,
            slot,
            pl.BlockSpec((None, tm, LANES), lambda bb, i: (bb, i, 0)),
            pl.BlockSpec((N_EXPERTS, LANES), lambda bb, i: (0, 0)),
        ],
        out_shape=[jax.ShapeDtypeStruct((b, r, d // 2), jnp.uint32), slot_shape, slot_shape,
                   jax.ShapeDtypeStruct((b, r, LANES), F32), jax.ShapeDtypeStruct((N_EXPERTS, LANES), F32)],
        scratch_shapes=[pltpu.VMEM((N_EXPERTS, LANES), F32)],
        compiler_params=_cparams(("arbitrary", "arbitrary")),
        name="moe_router",
    )(x, g.reshape(1, d), mods3, mods3, w_router_t, b_router.reshape(N_EXPERTS, 1))


def _moe_plan(counts, n_rows):
    blk = EXPERT_BLK
    nb = n_rows // blk
    ends = jnp.cumsum(counts)
    starts = ends - counts
    count_le = lambda sorted_vals, q: jnp.sum(sorted_vals[None, :] <= q[:, None], axis=1, dtype=jnp.int32)
    first = jnp.arange(nb, dtype=jnp.int32) * blk
    e_lo = count_le(ends, first)
    e_hi = count_le(ends, first + (blk - 1))
    n_pair = e_hi - e_lo + 1
    p_end = jnp.cumsum(n_pair)
    p_start = p_end - n_pair
    i = jnp.arange(nb + N_EXPERTS - 1, dtype=jnp.int32)
    j = jnp.minimum(count_le(p_end, i), nb - 1)
    valid = i < p_end[-1]
    e = jnp.where(valid, e_lo[j] + i - p_start[j], e_hi[nb - 1]).astype(jnp.int32)
    bounds = jnp.concatenate([starts, ends[-1:]]).astype(jnp.int32)
    return j, e, valid.astype(jnp.int32), bounds


def _positions_kernel(starts_ref, eid_ref, rnk_ref, pos_ref):
    eid = eid_ref[...]
    pos = rnk_ref[...]
    for e in range(N_EXPERTS):
        pos = pos + jnp.where(eid == e, starts_ref[e], 0)
    pos_ref[...] = pos


def _positions(eid, rnk, starts):
    full = pl.BlockSpec(eid.shape, lambda: (0, 0))
    return pl.pallas_call(
        _positions_kernel,
        in_specs=[pl.BlockSpec(memory_space=pltpu.SMEM), full, full],
        out_specs=full,
        out_shape=jax.ShapeDtypeStruct(eid.shape, jnp.int32),
        compiler_params=pltpu.CompilerParams(vmem_limit_bytes=VMEM_LIMIT),
        name="moe_positions",
    )(starts, eid, rnk)


def _dispatch_kernel(pos_ref, h_ref, xs_ref, sem):
    tm = h_ref.shape[0]

    def issue(t, carry):
        for k in range(TOP_K):
            pltpu.make_async_copy(h_ref.at[pl.ds(t, 1)], xs_ref.at[pl.ds(pos_ref[k, t], 1)], sem).start()
        return carry

    lax.fori_loop(0, tm, issue, 0)
    for _ in range(TOP_K):
        pltpu.make_async_copy(h_ref, xs_ref.at[pl.ds(0, tm)], sem).wait()


def _dispatch(h2, pos):
    t, w = h2.shape
    tm = ROW_TILE
    return pl.pallas_call(
        _dispatch_kernel,
        grid=(t // tm,),
        in_specs=[
            pl.BlockSpec((TOP_K, tm), lambda i: (0, i), memory_space=pltpu.SMEM),
            pl.BlockSpec((tm, w), lambda i: (i, 0)),
        ],
        out_specs=pl.BlockSpec(memory_space=pl.ANY),
        out_shape=jax.ShapeDtypeStruct((t * TOP_K, w), h2.dtype),
        scratch_shapes=[pltpu.SemaphoreType.DMA],
        compiler_params=_cparams(("arbitrary",)),
        name="moe_dispatch",
    )(pos, h2)


def _grouped_kernel(pb_ref, pe_ref, pv_ref, bnd_ref, xs_ref, wg_ref, wu_ref, wd_ref, y_ref, wgb, wub, wdb):
    i = pl.program_id(0)
    prev = jnp.maximum(i - 1, 0)
    j = pb_ref[i]
    e = pe_ref[i]
    blk = xs_ref.shape[0]

    @pl.when((i == 0) | (pb_ref[prev] != j))
    def _():
        y_ref[...] = jnp.zeros_like(y_ref)

    @pl.when((i == 0) | (pe_ref[prev] != e))
    def _():
        wgb[...] = wg_ref[...].astype(BF16)
        wub[...] = wu_ref[...].astype(BF16)
        wdb[...] = wd_ref[...].astype(BF16)

    @pl.when(pv_ref[i] == 1)
    def _():
        xw = xs_ref[...]
        a = _packed_dot(xw, wgb)
        u = _packed_dot(xw, wub)
        yv = jnp.dot((_silu(a) * u).astype(BF16), wdb[...], preferred_element_type=F32)
        rows = j * blk + lax.broadcasted_iota(jnp.int32, (blk, 1), 0)
        own = (rows >= bnd_ref[e]) & (rows < bnd_ref[e + 1])
        y_ref[...] = jnp.where(own, _pack_bf16(yv), y_ref[...])


def _grouped(pb, pe, pv, bounds, xs, wg, wu, wd):
    p, half = xs.shape
    d = 2 * half
    blk = EXPERT_BLK
    grid_spec = pltpu.PrefetchScalarGridSpec(
        num_scalar_prefetch=4,
        grid=(pb.shape[0],),
        in_specs=[
            pl.BlockSpec((blk, half), lambda i, pb, pe, pv, bnd: (pb[i], 0)),
            pl.BlockSpec((None, d, D_EXPERT), lambda i, pb, pe, pv, bnd: (pe[i], 0, 0)),
            pl.BlockSpec((None, d, D_EXPERT), lambda i, pb, pe, pv, bnd: (pe[i], 0, 0)),
            pl.BlockSpec((None, D_EXPERT, d), lambda i, pb, pe, pv, bnd: (pe[i], 0, 0)),
        ],
        out_specs=pl.BlockSpec((blk, half), lambda i, pb, pe, pv, bnd: (pb[i], 0)),
        scratch_shapes=[
            pltpu.VMEM((d, D_EXPERT), BF16),
            pltpu.VMEM((d, D_EXPERT), BF16),
            pltpu.VMEM((D_EXPERT, d), BF16),
        ],
    )
    return pl.pallas_call(
        _grouped_kernel,
        grid_spec=grid_spec,
        out_shape=jax.ShapeDtypeStruct((p, half), jnp.uint32),
        compiler_params=_cparams(("arbitrary",)),
        name="moe_grouped_experts",
    )(pb, pe, pv, bounds, xs, wg, wu, wd)


def _combine_kernel(*refs, final):
    pos_ref, y_ref, gate_ref, h_ref, x_ref, g2_ref, sg_ref, su_ref, sd_ref = refs[:9]
    o_ref, buf, sem = refs[-3:]
    tm = x_ref.shape[0]

    def issue(t, carry):
        for k in range(TOP_K):
            pltpu.make_async_copy(y_ref.at[pl.ds(pos_ref[k, t], 1)], buf.at[k, pl.ds(t, 1)], sem).start()
        return carry

    lax.fori_loop(0, tm, issue, 0)
    hw = h_ref[...]
    a = _packed_dot(hw, sg_ref)
    u = _packed_dot(hw, su_ref)
    acc = jnp.dot((_silu(a) * u).astype(BF16), sd_ref[...], preferred_element_type=F32)
    for k in range(TOP_K):
        pltpu.make_async_copy(y_ref.at[pl.ds(0, tm)], buf.at[k], sem).wait()
    g = gate_ref[...]
    half = hw.shape[-1]
    acc_hi = acc[:, :half]
    acc_lo = acc[:, half:]
    for k in range(TOP_K):
        hi, lo = _unpack_bf16(buf[k])
        acc_hi = acc_hi + g[:, k:k + 1] * hi
        acc_lo = acc_lo + g[:, k:k + 1] * lo
    x = x_ref[...] + g2_ref[...] * jnp.concatenate([acc_hi, acc_lo], axis=-1)
    if final:
        gf_ref = refs[9]
        ms = jnp.mean(x * x, axis=-1, keepdims=True)
        x = x * lax.rsqrt(ms + NORM_EPS) * gf_ref[...]
    o_ref[...] = x


def _combine(pos, y, gates, h2, x, mods3, layer, row_fn, sg, su, sd, g_final=None):
    b, r, d = x.shape
    tm = ROW_TILE
    nt = r // tm
    tile = pl.BlockSpec((None, tm, d), lambda bb, i: (bb, i, 0))
    in_specs = [
        pl.BlockSpec((TOP_K, tm), lambda bb, i: (0, bb * nt + i), memory_space=pltpu.SMEM),
        pl.BlockSpec(memory_space=pl.ANY),
        pl.BlockSpec((None, tm, LANES), lambda bb, i: (bb, i, 0)),
        pl.BlockSpec((None, tm, d // 2), lambda bb, i: (bb, i, 0)),
        tile,
        _mod_spec(d, layer, 5, row_fn),
        pl.BlockSpec(sg.shape, lambda bb, i: (0, 0)),
        pl.BlockSpec(su.shape, lambda bb, i: (0, 0)),
        pl.BlockSpec(sd.shape, lambda bb, i: (0, 0)),
    ]
    args = [pos, y, gates, h2, x, mods3, sg, su, sd]
    if g_final is not None:
        in_specs.append(pl.BlockSpec((1, d), lambda bb, i: (0, 0)))
        args.append(g_final.reshape(1, d))
    return pl.pallas_call(
        functools.partial(_combine_kernel, final=g_final is not None),
        grid=(b, nt),
        in_specs=in_specs,
        out_specs=tile,
        out_shape=jax.ShapeDtypeStruct((b, r, d), F32),
        scratch_shapes=[pltpu.VMEM((TOP_K, tm, d // 2), jnp.uint32), pltpu.SemaphoreType.DMA],
        compiler_params=_cparams(("arbitrary", "arbitrary")),
        name="moe_combine",
    )(*args)


def _moe(x, g_ffn, mods3, layer, row_fn, w_router, b_router, w_e_gate, w_e_up, w_e_down,
         ws_gate, ws_up, ws_down, g_final=None):
    b, r, d = x.shape
    h2, eid, rnk, gates, cnt = _router(x, g_ffn, mods3, layer, row_fn, w_router.T, b_router)
    pb, pe, pv, bounds = _moe_plan(cnt[:, 0].astype(jnp.int32), b * r * TOP_K)
    pos = _positions(eid, rnk, bounds[:N_EXPERTS])
    xs = _dispatch(h2.reshape(b * r, d // 2), pos)
    y = _grouped(pb, pe, pv, bounds, xs, w_e_gate, w_e_up, w_e_down)
    return _combine(pos, y, gates, h2, x, mods3, layer, row_fn,
                    ws_gate.astype(BF16), ws_up.astype(BF16), ws_down.astype(BF16), g_final)


def kernel(x, c, ctx, c_ctx, w_mod, b_mod, g_mix, g_ffn, g_final, ab_w_in, ab_w_out, ab_conv_w, ab_conv_b, ab_w_r, ab_b_r, ab_w_i, ab_b_i, ab_lam, ab_sink, cd_w_in, cd_w_out, cd_lam, cd_subln_g, cd_conv_w, cd_conv_b, cd_dt_bias, cd_a_log, cd_d_skip, cd_norm_g, w_router, b_router, w_e_gate, w_e_up, w_e_down, ws_gate, ws_up, ws_down):
    bsz, s_len, d = x.shape
    c_len = ctx.shape[1]
    depth = w_mod.shape[0]
    assert depth == 2 and bsz == SUBLANES, "kernels are specialised to depth 2 and batch 8"
    assert c_len % ROW_TILE == 0 and s_len % ROW_TILE == 0
    nct_row = c_len // ROW_TILE
    nct_time = c_len // TIME_TILE

    c_all = jnp.concatenate([c, c_ctx[None], jnp.zeros((MOD_ROWS - bsz - 1, d), F32)], axis=0)
    mods3 = _modulations(c_all, w_mod, b_mod).reshape(depth * MOD_ROWS, 1, N_MOD * d)
    rope_tabs = _rope_tables(c_len, s_len)
    xc = jnp.concatenate([ctx, x], axis=1)
    row_mixed = lambda bb, i: jnp.where(i < nct_row, SUBLANES, bb)
    row_latent = lambda bb, i: bb

    w_in = ab_w_in[0].astype(BF16)
    q_hi = LRU_WIDTH + WIN_HEADS * HEAD_DIM
    x_hi = q_hi + LRU_WIDTH
    k_hi = x_hi + WIN_KV_HEADS * HEAD_DIM
    gate, q, xa, k, v = _project(xc, g_mix[0], mods3, 0, nct_row, rope_tabs, [
        (w_in[:, :LRU_WIDTH], False, F32, False),
        (w_in[:, LRU_WIDTH:q_hi], True, BF16, False),
        (w_in[:, q_hi:x_hi], False, F32, True),
        (w_in[:, x_hi:k_hi], True, BF16, False),
        (w_in[:, k_hi:], False, BF16, False),
    ])
    l_len = c_len + s_len
    w_gates = jnp.stack([jnp.concatenate([_block_diag(ab_w_r[0, dd]), _block_diag(ab_w_i[0, dd])], axis=1)
                         for dd in range(2)]).astype(BF16)
    b_gates = jnp.concatenate([ab_b_r[0], ab_b_i[0]], axis=-1).reshape(2, 1, 2 * LRU_WIDTH)
    rec = _rglru(xa.reshape(l_len, bsz, LRU_WIDTH), ab_conv_w[0], ab_conv_b[0], w_gates, b_gates,
                 ab_lam[0].reshape(2, 1, LRU_WIDTH), nct_time)
    att = _win_attention(q, k, v, ab_sink[0], c_len)
    w_out = ab_w_out[0].astype(BF16)
    xc = _out_even(xc, rec.reshape(2, l_len, bsz * LRU_WIDTH), gate, att, w_out[:LRU_WIDTH], w_out[LRU_WIDTH:],
                   mods3, 0, nct_row)
    xc = _moe(xc, g_ffn[0], mods3, 0, row_mixed, w_router[0], b_router[0], w_e_gate[0], w_e_up[0], w_e_down[0],
              ws_gate[0], ws_up[0], ws_down[0])

    w_in = cd_w_in[0].astype(BF16)
    qk = DIFF_HEADS * 2 * DIFF_DH
    z_hi = qk + SSD_INNER
    k_hi = z_hi + qk
    v_hi = k_hi + qk
    x_hi = v_hi + SSD_CONV_DIM
    w_dt = jnp.pad(w_in[:, x_hi:], ((0, 0), (0, LANES - 2 * SSD_HEADS)))
    q, z, k, v, xbc, dt = _project(xc, g_mix[1], mods3, 1, nct_row, rope_tabs, [
        (w_in[:, :qk], True, BF16, False),
        (w_in[:, qk:z_hi], False, F32, False),
        (w_in[:, z_hi:k_hi], True, BF16, False),
        (w_in[:, k_hi:v_hi], False, BF16, False),
        (w_in[:, v_hi:x_hi], False, F32, False),
        (w_dt, False, F32, False),
    ])
    lam_init = 0.8 - 0.6 * math.exp(-0.3 * 1)
    diff = _diff_attention(q, k, v, cd_lam[0], cd_subln_g[0], lam_init, c_len)
    y2 = _ssd(xbc, dt, cd_conv_w[0], cd_conv_b[0], cd_dt_bias[0], cd_a_log[0], cd_d_skip[0], nct_time)
    w_out = cd_w_out[0].astype(BF16)
    xl = _out_odd(xc, diff, y2, z, cd_norm_g[0], w_out[:qk], w_out[qk:], mods3, 1, c_len)
    return _moe(xl, g_ffn[1], mods3, 1, row_latent, w_router[1], b_router[1], w_e_gate[1], w_e_up[1], w_e_down[1],
                ws_gate[1], ws_up[1], ws_down[1], g_final=g_final)
```

```python
import functools
import math

import jax
import jax.numpy as jnp
from jax import lax
from jax.experimental import pallas as pl
from jax.experimental.pallas import tpu as pltpu

F32 = jnp.float32
BF16 = jnp.bfloat16
HIGHEST = lax.Precision.HIGHEST

GRID_W = 64
N_MOD = 6
NORM_EPS = 1e-6
ROPE_BASE = 10000.0
CONV_W = 4

LRU_WIDTH = 512
LRU_BLOCKS = 8
LRU_C = 8.0

HEAD_DIM = 64
WIN_HEADS = 8
WIN_KV_HEADS = 2
WINDOW = 128

DIFF_HEADS = 4
DIFF_DH = 64

SSD_HEADS = 8
SSD_HEAD_DIM = 64
SSD_INNER = SSD_HEADS * SSD_HEAD_DIM
SSD_GROUPS = 2
SSD_STATE = 128
SSD_CONV_DIM = SSD_INNER + 2 * SSD_GROUPS * SSD_STATE

N_EXPERTS = 64
N_EXPERT_GROUPS = 8
TOPK_GROUPS = 4
TOP_K = 8
D_EXPERT = 256
ROUTED_SCALE = 2.5

LANES = 128
SUBLANES = 8
MOD_ROWS = 16
TIME_TILE = 128
ROW_TILE = 256
EXPERT_BLK = 512
VMEM_LIMIT = 48 * 1024 * 1024
NEG_BIG = -1e30


def _cparams(sem):
    return pltpu.CompilerParams(dimension_semantics=sem, vmem_limit_bytes=VMEM_LIMIT)


def _nt_dot(a, b):
    return lax.dot_general(a, b, (((1,), (1,)), ((), ())), preferred_element_type=F32)


def _softplus(x):
    return jnp.maximum(x, 0.0) + jnp.log1p(jnp.exp(-jnp.abs(x)))


def _silu(x):
    return x * jax.nn.sigmoid(x)


def _pack_bf16(x):
    half = x.shape[-1] // 2
    bits = pltpu.bitcast(x.astype(BF16).astype(F32), jnp.uint32)
    return bits[:, :half] | (bits[:, half:] >> 16)


def _unpack_bf16(w):
    hi = pltpu.bitcast(w & jnp.uint32(0xFFFF0000), F32)
    lo = pltpu.bitcast(w << 16, F32)
    return hi, lo


def _packed_dot(w, weight_ref):
    half = w.shape[-1]
    hi, lo = _unpack_bf16(w)
    return (jnp.dot(hi.astype(BF16), weight_ref[:half, :], preferred_element_type=F32)
            + jnp.dot(lo.astype(BF16), weight_ref[half:, :], preferred_element_type=F32))


def _mod_kernel(c_ref, w_ref, b_ref, o_ref):
    c = c_ref[...]
    s = _silu(c)
    o_ref[...] = jnp.dot(s, w_ref[...], preferred_element_type=F32, precision=HIGHEST) + b_ref[...]


def _modulations(c_all, w_mod, b_mod):
    depth, d, _ = w_mod.shape
    return pl.pallas_call(
        _mod_kernel,
        grid=(depth, N_MOD),
        in_specs=[
            pl.BlockSpec((MOD_ROWS, d), lambda l, k: (0, 0)),
            pl.BlockSpec((None, d, d), lambda l, k: (l, 0, k)),
            pl.BlockSpec((None, 1, d), lambda l, k: (l, 0, k)),
        ],
        out_specs=pl.BlockSpec((None, MOD_ROWS, d), lambda l, k: (l, 0, k)),
        out_shape=jax.ShapeDtypeStruct((depth, MOD_ROWS, N_MOD * d), F32),
        compiler_params=_cparams(("arbitrary", "arbitrary")),
        name="adaln_modulation",
    )(c_all, w_mod, b_mod.reshape(depth, 1, N_MOD * d))


def _mod_spec(d, layer, chunk, row_fn):
    return pl.BlockSpec((None, 1, d), lambda b, i: (layer * MOD_ROWS + row_fn(b, i), 0, chunk))


def _norm_mod(x, g, sh, sc):
    ms = jnp.mean(x * x, axis=-1, keepdims=True)
    return (x * lax.rsqrt(ms + NORM_EPS) * g) * (1.0 + sc) + sh


def _rope(y, cos, sa, sb):
    n = y.shape[-1]
    half = HEAD_DIM // 2
    return y * cos + pltpu.roll(y, n - half, 1) * sa + pltpu.roll(y, half, 1) * sb


def _proj_kernel(*refs, ropes):
    n = len(ropes)
    x_ref, g_ref, sh_ref, sc_ref, cos_ref, sa_ref, sb_ref = refs[:7]
    w_refs = refs[7:7 + n]
    o_refs = refs[7 + n:]
    h = _norm_mod(x_ref[...], g_ref[...], sh_ref[...], sc_ref[...]).astype(BF16)
    for w_ref, o_ref, rope in zip(w_refs, o_refs, ropes):
        y = jnp.dot(h, w_ref[...], preferred_element_type=F32)
        if rope:
            w = y.shape[-1]
            y = _rope(y, cos_ref[:, :w], sa_ref[:, :w], sb_ref[:, :w])
        o_ref[...] = y.astype(o_ref.dtype)


def _project(xc, g, mods3, layer, nct, rope_tabs, groups):
    b, l, d = xc.shape
    tm = ROW_TILE
    row = lambda bb, i: jnp.where(i < nct, SUBLANES, bb)
    rw = rope_tabs[0].shape[-1]
    in_specs = [
        pl.BlockSpec((None, tm, d), lambda bb, i: (bb, i, 0)),
        pl.BlockSpec((1, d), lambda bb, i: (0, 0)),
        _mod_spec(d, layer, 0, row),
        _mod_spec(d, layer, 1, row),
    ] + [pl.BlockSpec((tm, rw), lambda bb, i: (i, 0))] * 3
    out_specs, out_shapes = [], []
    for w, _, dt, time_major in groups:
        n = w.shape[1]
        in_specs.append(pl.BlockSpec((d, n), lambda bb, i: (0, 0)))
        if time_major:
            out_specs.append(pl.BlockSpec((tm, n), lambda bb, i: (i, bb)))
            out_shapes.append(jax.ShapeDtypeStruct((l, b * n), dt))
        else:
            out_specs.append(pl.BlockSpec((None, tm, n), lambda bb, i: (bb, i, 0)))
            out_shapes.append(jax.ShapeDtypeStruct((b, l, n), dt))
    return pl.pallas_call(
        functools.partial(_proj_kernel, ropes=tuple(gp[1] for gp in groups)),
        grid=(b, l // tm),
        in_specs=in_specs,
        out_specs=out_specs,
        out_shape=out_shapes,
        compiler_params=_cparams(("arbitrary", "arbitrary")),
        name="norm_mod_project",
    )(xc, g.reshape(1, d), mods3, mods3, *rope_tabs, *[gp[0] for gp in groups])


def _rope_tables(c_len, s_len):
    rows = s_len // GRID_W
    row = jnp.repeat(jnp.arange(rows), GRID_W).astype(F32)
    col = jnp.tile(jnp.arange(GRID_W), rows).astype(F32)
    n = HEAD_DIM // 4
    inv = ROPE_BASE ** (-jnp.arange(n, dtype=F32) / n)
    ang = jnp.concatenate([row[:, None] * inv, col[:, None] * inv], axis=-1)
    cos, sin = jnp.cos(ang), jnp.sin(ang)
    zero = jnp.zeros_like(sin)
    reps = WIN_HEADS
    cos_t = jnp.tile(jnp.concatenate([cos, cos], axis=-1), (1, reps))
    sa_t = jnp.tile(jnp.concatenate([-sin, zero], axis=-1), (1, reps))
    sb_t = jnp.tile(jnp.concatenate([zero, sin], axis=-1), (1, reps))
    w = cos_t.shape[-1]
    pad1 = jnp.ones((c_len, w), F32)
    pad0 = jnp.zeros((c_len, w), F32)
    return (jnp.concatenate([pad1, cos_t], 0), jnp.concatenate([pad0, sa_t], 0),
            jnp.concatenate([pad0, sb_t], 0))


def _seq_tile(d, g, nct, nt):
    rev = jnp.where(g < nct, nct - 1 - g, nt - 1 - (g - nct))
    return jnp.where(d == 0, g, rev)


def _rglru_kernel(x_ref, xp_ref, xn_ref, cw_ref, cb_ref, w_ref, bias_ref, lam_ref, o_ref,
                  ext_scr, a_scr, b_scr, h_scr, *, ts, nct, nt, sub):
    d = pl.program_id(0)
    g = pl.program_id(1)
    tile = _seq_tile(d, g, nct, nt)
    bsz, width = h_scr.shape
    pv = jnp.where((tile == 0) | (tile == nct), 0.0, 1.0)
    nv = jnp.where((tile == nct - 1) | (tile == nt - 1), 0.0, 1.0)
    ext_scr[0:1] = xp_ref[...] * pv
    ext_scr[1:ts + 1] = x_ref[...]
    ext_scr[ts + 1:ts + 3] = xn_ref[...] * nv

    @pl.when(g == 0)
    def _():
        h_scr[...] = jnp.zeros_like(h_scr)

    neg_sp = -LRU_C * _softplus(-lam_ref[...])

    def prep(c, carry):
        r0 = pl.multiple_of(c * sub, sub)
        e = ext_scr[pl.ds(r0, sub + CONV_W - 1)]
        u = cb_ref[...] + cw_ref[0] * e[0:sub]
        for j in range(1, CONV_W):
            u = u + cw_ref[j] * e[j:j + sub]
        u2 = u.reshape(sub * bsz, width)
        gts = jnp.dot(u2.astype(BF16), w_ref[...], preferred_element_type=F32) + bias_ref[...]
        r = jax.nn.sigmoid(gts[:, :width])
        ig = jax.nn.sigmoid(gts[:, width:])
        log_a = neg_sp * r
        a = jnp.exp(log_a)
        mult = jnp.sqrt(1.0 - a * a)
        a_scr[pl.ds(r0, sub)] = a.reshape(sub, bsz, width)
        b_scr[pl.ds(r0, sub)] = (mult * ig * u2).reshape(sub, bsz, width)
        return carry

    lax.fori_loop(0, ts // sub, prep, 0)

    def step(t, h):
        tt = jnp.where(d == 0, t, ts - 1 - t)
        h = a_scr[tt] * h + b_scr[tt]
        o_ref[tt] = h
        return h

    h_scr[...] = lax.fori_loop(0, ts, step, h_scr[...], unroll=8)


def _rglru(xa_tm, conv_w, conv_b, w_gates, b_gates, lam, nct):
    l, bsz, width = xa_tm.shape
    ts = TIME_TILE
    nt = l // ts
    tile = lambda d, g: _seq_tile(d, g, nct, nt)
    kern = functools.partial(_rglru_kernel, ts=ts, nct=nct, nt=nt, sub=16)
    return pl.pallas_call(
        kern,
        grid=(2, nt),
        in_specs=[
            pl.BlockSpec((ts, bsz, width), lambda d, g: (tile(d, g), 0, 0)),
            pl.BlockSpec((1, bsz, width), lambda d, g: (jnp.maximum(tile(d, g) * ts - 1, 0), 0, 0)),
            pl.BlockSpec((2, bsz, width),
                         lambda d, g: (jnp.minimum((tile(d, g) + 1) * (ts // 2), l // 2 - 1), 0, 0)),
            pl.BlockSpec((CONV_W, 1, width), lambda d, g: (0, 0, 0)),
            pl.BlockSpec((1, width), lambda d, g: (0, 0)),
            pl.BlockSpec((None, width, 2 * width), lambda d, g: (d, 0, 0)),
            pl.BlockSpec((None, 1, 2 * width), lambda d, g: (d, 0, 0)),
            pl.BlockSpec((None, 1, width), lambda d, g: (d, 0, 0)),
        ],
        out_specs=pl.BlockSpec((None, ts, bsz, width), lambda d, g: (d, tile(d, g), 0, 0)),
        out_shape=jax.ShapeDtypeStruct((2, l, bsz, width), F32),
        scratch_shapes=[
            pltpu.VMEM((ts + CONV_W - 1, bsz, width), F32),
            pltpu.VMEM((ts, bsz, width), F32),
            pltpu.VMEM((ts, bsz, width), F32),
            pltpu.VMEM((bsz, width), F32),
        ],
        compiler_params=_cparams(("arbitrary", "arbitrary")),
        name="rglru_scan",
    )(xa_tm, xa_tm, xa_tm, conv_w.reshape(CONV_W, 1, width), conv_b.reshape(1, width),
      w_gates, b_gates, lam)


def _block_diag(w):
    nb, c, dd = w.shape
    eye = jnp.eye(nb, dtype=w.dtype)
    return (eye[:, None, :, None] * w[:, :, None, :]).reshape(nb * c, nb * dd)


def _win_attn_kernel(sink_ref, q_ref, k_ref, v_ref, o_ref, *, c_len, l_len, nqc):
    j = pl.program_id(1)
    blk = q_ref.shape[0]
    grp = WIN_HEADS // WIN_KV_HEADS
    band = 3 * blk

    def heads(body):
        for hk in range(WIN_KV_HEADS):
            ksl = slice(hk * HEAD_DIM, (hk + 1) * HEAD_DIM)
            for gq in range(grp):
                head = hk * grp + gq
                hsl = slice(head * HEAD_DIM, (head + 1) * HEAD_DIM)
                o_ref[:, hsl] = body(q_ref[:, hsl], ksl, sink_ref[head]).astype(o_ref.dtype)

    @pl.when(j < nqc)
    def _():
        def body(qh, ksl, sink):
            s = _nt_dot(qh, k_ref[0:c_len, ksl]) * (HEAD_DIM ** -0.5)
            m = jnp.maximum(jnp.max(s, axis=-1, keepdims=True), sink)
            p = jnp.exp(s - m)
            den = jnp.sum(p, axis=-1, keepdims=True) + jnp.exp(sink - m)
            o = jnp.dot(p.astype(BF16), v_ref[0:c_len, ksl], preferred_element_type=F32)
            return o / den
        heads(body)

    @pl.when(j >= nqc)
    def _():
        jb = j - nqc
        start = jnp.clip(c_len + (jb - 1) * blk, c_len - blk, l_len - band)
        start = pl.multiple_of(start, blk)
        qpos = jb * blk + lax.broadcasted_iota(jnp.int32, (blk, band), 0)
        kpos = start - c_len + lax.broadcasted_iota(jnp.int32, (blk, band), 1)
        valid = (jnp.abs(qpos - kpos) <= WINDOW) & (kpos >= 0)

        def body(qh, ksl, sink):
            sc = _nt_dot(qh, k_ref[0:c_len, ksl]) * (HEAD_DIM ** -0.5)
            sb = _nt_dot(qh, k_ref[pl.ds(start, band), ksl]) * (HEAD_DIM ** -0.5)
            sb = jnp.where(valid, sb, NEG_BIG)
            m = jnp.maximum(jnp.maximum(jnp.max(sc, axis=-1, keepdims=True),
                                        jnp.max(sb, axis=-1, keepdims=True)), sink)
            pc = jnp.exp(sc - m)
            pb = jnp.exp(sb - m)
            den = (jnp.sum(pc, axis=-1, keepdims=True) + jnp.sum(pb, axis=-1, keepdims=True)
                   + jnp.exp(sink - m))
            o = (jnp.dot(pc.astype(BF16), v_ref[0:c_len, ksl], preferred_element_type=F32)
                 + jnp.dot(pb.astype(BF16), v_ref[pl.ds(start, band), ksl], preferred_element_type=F32))
            return o / den
        heads(body)


def _win_attention(q, k, v, sink, c_len):
    b, l, qw = q.shape
    kw = k.shape[-1]
    blk = TIME_TILE
    kern = functools.partial(_win_attn_kernel, c_len=c_len, l_len=l, nqc=c_len // blk)
    return pl.pallas_call(
        kern,
        grid=(b, l // blk),
        in_specs=[
            pl.BlockSpec(memory_space=pltpu.SMEM),
            pl.BlockSpec((None, blk, qw), lambda bb, j: (bb, j, 0)),
            pl.BlockSpec((None, l, kw), lambda bb, j: (bb, 0, 0)),
            pl.BlockSpec((None, l, kw), lambda bb, j: (bb, 0, 0)),
        ],
        out_specs=pl.BlockSpec((None, blk, qw), lambda bb, j: (bb, j, 0)),
        out_shape=jax.ShapeDtypeStruct((b, l, qw), BF16),
        compiler_params=_cparams(("arbitrary", "arbitrary")),
        name="window_attention",
    )(sink, q, k, v)


def _out_even_kernel(x_ref, rec_ref, gate_ref, att_ref, wa_ref, wb_ref, g1_ref, o_ref):
    lru = (rec_ref[0] + rec_ref[1]) * jax.nn.gelu(gate_ref[...])
    y = (jnp.dot(lru.astype(BF16), wa_ref[...], preferred_element_type=F32)
         + jnp.dot(att_ref[...], wb_ref[...], preferred_element_type=F32))
    o_ref[...] = x_ref[...] + g1_ref[...] * y


def _out_even(xc, rec2, gate, att, w_a, w_b, mods3, layer, nct):
    b, l, d = xc.shape
    tm = ROW_TILE
    w = gate.shape[-1]
    row = lambda bb, i: jnp.where(i < nct, SUBLANES, bb)
    return pl.pallas_call(
        _out_even_kernel,
        grid=(b, l // tm),
        in_specs=[
            pl.BlockSpec((None, tm, d), lambda bb, i: (bb, i, 0)),
            pl.BlockSpec((2, tm, w), lambda bb, i: (0, i, bb)),
            pl.BlockSpec((None, tm, w), lambda bb, i: (bb, i, 0)),
            pl.BlockSpec((None, tm, att.shape[-1]), lambda bb, i: (bb, i, 0)),
            pl.BlockSpec(w_a.shape, lambda bb, i: (0, 0)),
            pl.BlockSpec(w_b.shape, lambda bb, i: (0, 0)),
            _mod_spec(d, layer, 2, row),
        ],
        out_specs=pl.BlockSpec((None, tm, d), lambda bb, i: (bb, i, 0)),
        out_shape=jax.ShapeDtypeStruct((b, l, d), F32),
        compiler_params=_cparams(("arbitrary", "arbitrary")),
        name="out_proj_even",
    )(xc, rec2, gate, att, w_a, w_b, mods3)


def _diff_attn_kernel(lam_ref, g_ref, q_ref, k_ref, v_ref, o_ref, *, lam_init):
    lv = lam_ref[...]
    lam = (jnp.exp(jnp.sum(lv[0:1] * lv[1:2], axis=-1, keepdims=True))
           - jnp.exp(jnp.sum(lv[2:3] * lv[3:4], axis=-1, keepdims=True)) + lam_init)
    vw = 2 * DIFF_DH

    def softmax(qm, km):
        s = _nt_dot(qm, km) * (DIFF_DH ** -0.5)
        m = jnp.max(s, axis=-1, keepdims=True)
        p = jnp.exp(s - m)
        return p / jnp.sum(p, axis=-1, keepdims=True)

    for h in range(DIFF_HEADS):
        lo = h * vw
        p0 = softmax(q_ref[:, lo:lo + DIFF_DH], k_ref[:, lo:lo + DIFF_DH])
        p1 = softmax(q_ref[:, lo + DIFF_DH:lo + vw], k_ref[:, lo + DIFF_DH:lo + vw])
        w = (p0 - lam * p1).astype(BF16)
        o = jnp.dot(w, v_ref[:, lo:lo + vw], preferred_element_type=F32)
        ms = jnp.mean(o * o, axis=-1, keepdims=True)
        o = o * lax.rsqrt(ms + NORM_EPS) * g_ref[...]
        o_ref[:, lo:lo + vw] = (o * (1.0 - lam_init)).astype(o_ref.dtype)


def _diff_attention(q, k, v, lam_vecs, subln_g, lam_init, c_len):
    b, l, w = q.shape
    tq = TIME_TILE
    s_len = l - c_len
    off = c_len // tq
    return pl.pallas_call(
        functools.partial(_diff_attn_kernel, lam_init=lam_init),
        grid=(b, s_len // tq),
        in_specs=[
            pl.BlockSpec(lam_vecs.shape, lambda bb, j: (0, 0)),
            pl.BlockSpec((1, 2 * DIFF_DH), lambda bb, j: (0, 0)),
            pl.BlockSpec((None, tq, w), lambda bb, j: (bb, j + off, 0)),
            pl.BlockSpec((None, l, w), lambda bb, j: (bb, 0, 0)),
            pl.BlockSpec((None, l, w), lambda bb, j: (bb, 0, 0)),
        ],
        out_specs=pl.BlockSpec((None, tq, w), lambda bb, j: (bb, j, 0)),
        out_shape=jax.ShapeDtypeStruct((b, s_len, w), BF16),
        compiler_params=_cparams(("arbitrary", "arbitrary")),
        name="diff_attention",
    )(lam_vecs, subln_g.reshape(1, -1), q, k, v)


def _ssd_kernel(x_ref, xp_ref, xn_ref, dt_ref, cw_ref, cb_ref, dtb_ref, alog_ref, dsk_ref, o_ref,
                ext_scr, st_scr, *, q, nct, nt):
    d = pl.program_id(0)
    g = pl.program_id(2)
    tile = _seq_tile(d, g, nct, nt)
    pv = jnp.where((tile == 0) | (tile == nct), 0.0, 1.0)
    nv = jnp.where((tile == nct - 1) | (tile == nt - 1), 0.0, 1.0)
    ext_scr[0:SUBLANES] = xp_ref[...] * pv
    ext_scr[SUBLANES:SUBLANES + q] = x_ref[...]
    ext_scr[SUBLANES + q:2 * SUBLANES + q] = xn_ref[...] * nv

    @pl.when(g == 0)
    def _():
        st_scr[...] = jnp.zeros_like(st_scr)

    u = cb_ref[...] + cw_ref[0] * ext_scr[SUBLANES - 1:SUBLANES - 1 + q, :]
    for j in range(1, CONV_W):
        u = u + cw_ref[j] * ext_scr[SUBLANES - 1 + j:SUBLANES - 1 + j + q, :]
    act = _silu(u)

    dtr = dt_ref[...]
    dtr = jnp.where(d == 0, dtr, pltpu.roll(dtr, LANES - SSD_HEADS, 1))
    dtv = _softplus(dtr + dtb_ref[...])
    head_lane = lax.broadcasted_iota(jnp.int32, (1, LANES), 1) < SSD_HEADS
    dta = dtv * jnp.where(head_lane, -jnp.exp(alog_ref[...]), 0.0)
    ri = lax.broadcasted_iota(jnp.int32, (q, q), 0)
    ci = lax.broadcasted_iota(jnp.int32, (q, q), 1)
    keep = jnp.where(d == 0, ri - ci, ci - ri) >= 0
    cum = jnp.dot(keep.astype(F32), dta, preferred_element_type=F32, precision=HIGHEST)
    tot = jnp.sum(dta, axis=0, keepdims=True)
    cum_t = cum.T
    dt_t = dtv.T
    to_end = jnp.exp(tot - cum) * dtv
    e_cum = jnp.exp(cum)
    e_tot = jnp.exp(tot)
    dskip = dsk_ref[...] * jnp.where(d == 0, 1.0, 0.0)

    hpg = SSD_HEADS // SSD_GROUPS
    for gi in range(SSD_GROUPS):
        b_g = act[:, SSD_INNER + gi * SSD_STATE:SSD_INNER + (gi + 1) * SSD_STATE]
        c_lo = SSD_INNER + SSD_GROUPS * SSD_STATE + gi * SSD_STATE
        c_g = act[:, c_lo:c_lo + SSD_STATE].astype(BF16)
        cb = _nt_dot(c_g, b_g.astype(BF16))
        b_gt = b_g.T.astype(BF16)
        for hh in range(hpg):
            h = gi * hpg + hh
            xs = act[:, h * SSD_HEAD_DIM:(h + 1) * SSD_HEAD_DIM]
            seg = cum[:, h:h + 1] - cum_t[h:h + 1, :]
            decay = jnp.exp(jnp.where(keep, seg, NEG_BIG))
            w = (cb * decay * dt_t[h:h + 1, :]).astype(BF16)
            state = st_scr[h]
            y = jnp.dot(w, xs.astype(BF16), preferred_element_type=F32)
            y = y + jnp.dot(c_g, state.astype(BF16), preferred_element_type=F32) * e_cum[:, h:h + 1]
            y = y + dskip[:, h * SSD_HEAD_DIM:(h + 1) * SSD_HEAD_DIM] * xs
            o_ref[:, h * SSD_HEAD_DIM:(h + 1) * SSD_HEAD_DIM] = y
            s_new = jnp.dot(b_gt, (xs * to_end[:, h:h + 1]).astype(BF16), preferred_element_type=F32)
            st_scr[h] = e_tot[:, h:h + 1] * state + s_new


def _ssd(xbc, dt, conv_w, conv_b, dt_bias, a_log, d_skip, nct):
    b, l, cd = xbc.shape
    q = TIME_TILE
    nt = l // q
    tile = lambda d, bb, g: _seq_tile(d, g, nct, nt)
    r8 = q // SUBLANES
    pad = LANES - SSD_HEADS
    dtb = jnp.pad(dt_bias, ((0, 0), (0, pad))).reshape(2, 1, LANES)
    alog = jnp.pad(a_log, ((0, 0), (0, pad))).reshape(2, 1, LANES)
    dsk = jnp.repeat(d_skip, SSD_HEAD_DIM).reshape(1, SSD_INNER)
    return pl.pallas_call(
        functools.partial(_ssd_kernel, q=q, nct=nct, nt=nt),
        grid=(2, b, nt),
        in_specs=[
            pl.BlockSpec((None, q, cd), lambda d, bb, g: (bb, tile(d, bb, g), 0)),
            pl.BlockSpec((None, SUBLANES, cd),
                         lambda d, bb, g: (bb, jnp.maximum(tile(d, bb, g) * r8 - 1, 0), 0)),
            pl.BlockSpec((None, SUBLANES, cd),
                         lambda d, bb, g: (bb, jnp.minimum((tile(d, bb, g) + 1) * r8, l // SUBLANES - 1), 0)),
            pl.BlockSpec((None, q, LANES), lambda d, bb, g: (bb, tile(d, bb, g), 0)),
            pl.BlockSpec((CONV_W, 1, cd), lambda d, bb, g: (0, 0, 0)),
            pl.BlockSpec((1, cd), lambda d, bb, g: (0, 0)),
            pl.BlockSpec((None, 1, LANES), lambda d, bb, g: (d, 0, 0)),
            pl.BlockSpec((None, 1, LANES), lambda d, bb, g: (d, 0, 0)),
            pl.BlockSpec((1, SSD_INNER), lambda d, bb, g: (0, 0)),
        ],
        out_specs=pl.BlockSpec((None, None, q, SSD_INNER), lambda d, bb, g: (d, bb, tile(d, bb, g), 0)),
        out_shape=jax.ShapeDtypeStruct((2, b, l, SSD_INNER), F32),
        scratch_shapes=[
            pltpu.VMEM((q + 2 * SUBLANES, cd), F32),
            pltpu.VMEM((SSD_HEADS, SSD_STATE, SSD_HEAD_DIM), F32),
        ],
        compiler_params=_cparams(("arbitrary", "arbitrary", "arbitrary")),
        name="ssd_chunked",
    )(xbc, xbc, xbc, dt, conv_w.reshape(CONV_W, 1, cd), conv_b.reshape(1, cd), dtb, alog, dsk)


def _out_odd_kernel(x_ref, diff_ref, y_ref, z_ref, ng_ref, wa_ref, wb_ref, g1_ref, o_ref):
    yz = (y_ref[0] + y_ref[1]) * _silu(z_ref[...])
    gs = SSD_INNER // SSD_GROUPS
    parts = []
    for gi in range(SSD_GROUPS):
        seg = yz[:, gi * gs:(gi + 1) * gs]
        ms = jnp.mean(seg * seg, axis=-1, keepdims=True)
        parts.append(seg * lax.rsqrt(ms + NORM_EPS) * ng_ref[:, gi * gs:(gi + 1) * gs])
    ssd = jnp.concatenate(parts, axis=-1).astype(BF16)
    y = (jnp.dot(diff_ref[...], wa_ref[...], preferred_element_type=F32)
         + jnp.dot(ssd, wb_ref[...], preferred_element_type=F32))
    o_ref[...] = x_ref[...] + g1_ref[...] * y


def _out_odd(xc, diff, y2, z, norm_g, w_a, w_b, mods3, layer, c_len):
    b, l, d = xc.shape
    s_len = l - c_len
    tm = ROW_TILE
    off = c_len // tm
    w = SSD_INNER
    row = lambda bb, i: bb
    return pl.pallas_call(
        _out_odd_kernel,
        grid=(b, s_len // tm),
        in_specs=[
            pl.BlockSpec((None, tm, d), lambda bb, i: (bb, i + off, 0)),
            pl.BlockSpec((None, tm, diff.shape[-1]), lambda bb, i: (bb, i, 0)),
            pl.BlockSpec((2, None, tm, w), lambda bb, i: (0, bb, i + off, 0)),
            pl.BlockSpec((None, tm, w), lambda bb, i: (bb, i + off, 0)),
            pl.BlockSpec((1, w), lambda bb, i: (0, 0)),
            pl.BlockSpec(w_a.shape, lambda bb, i: (0, 0)),
            pl.BlockSpec(w_b.shape, lambda bb, i: (0, 0)),
            _mod_spec(d, layer, 2, row),
        ],
        out_specs=pl.BlockSpec((None, tm, d), lambda bb, i: (bb, i, 0)),
        out_shape=jax.ShapeDtypeStruct((b, s_len, d), F32),
        compiler_params=_cparams(("arbitrary", "arbitrary")),
        name="out_proj_odd",
    )(xc, diff, y2, z, norm_g.reshape(1, w), w_a, w_b, mods3)


def _router_kernel(x_ref, g_ref, sh_ref, sc_ref, wr_ref, br_ref, h_ref, eid_ref, rnk_ref, gate_ref, cnt_ref,
                   carry_scr):
    @pl.when((pl.program_id(0) == 0) & (pl.program_id(1) == 0))
    def _():
        carry_scr[...] = jnp.zeros_like(carry_scr)

    h = _norm_mod(x_ref[...], g_ref[...], sh_ref[...], sc_ref[...])
    h_ref[...] = _pack_bf16(h)
    tm = h.shape[0]
    per = N_EXPERTS // N_EXPERT_GROUPS
    logits = lax.dot_general(wr_ref[...], h, (((1,), (1,)), ((), ())),
                             preferred_element_type=F32, precision=HIGHEST)
    scores = jax.nn.sigmoid(logits)
    sel = scores + br_ref[...]
    sel3 = sel.reshape(N_EXPERT_GROUPS, per, tm)
    kio = lax.broadcasted_iota(jnp.int32, sel3.shape, 1)
    m1 = jnp.max(sel3, axis=1, keepdims=True)
    first = jnp.min(jnp.where(sel3 == m1, kio, per), axis=1, keepdims=True)
    m2 = jnp.max(jnp.where(kio == first, NEG_BIG, sel3), axis=1, keepdims=True)
    gs = m1 + m2
    gio = lax.broadcasted_iota(jnp.int32, gs.shape, 0)
    ahead = jnp.zeros(gs.shape, jnp.int32)
    for gp in range(N_EXPERT_GROUPS):
        other = gs[gp:gp + 1]
        ahead = ahead + jnp.where((other > gs) | ((other == gs) & (gp < gio)), 1, 0)
    grp_on = jnp.where(ahead < TOPK_GROUPS, 1.0, 0.0)
    selm = jnp.where(jnp.broadcast_to(grp_on, sel3.shape) > 0.5, sel3, NEG_BIG).reshape(N_EXPERTS, tm)
    eio = lax.broadcasted_iota(jnp.int32, selm.shape, 0)
    rank = jnp.zeros(selm.shape, jnp.int32)
    for e in range(N_EXPERTS):
        other = selm[e:e + 1, :]
        rank = rank + jnp.where((other > selm) | ((other == selm) & (e < eio)), 1, 0)
    chosen = rank < TOP_K
    gate = jnp.where(chosen, scores, 0.0)
    gate = gate / jnp.sum(gate, axis=0, keepdims=True) * ROUTED_SCALE
    cf = jnp.where(chosen, 1.0, 0.0)
    ti = lax.broadcasted_iota(jnp.int32, (tm, tm), 0)
    tj = lax.broadcasted_iota(jnp.int32, (tm, tm), 1)
    before = jnp.where(ti < tj, 1.0, 0.0).astype(BF16)
    in_expert = carry_scr[:, 0:1] + jnp.dot(cf.astype(BF16), before, preferred_element_type=F32)
    carry_scr[...] = carry_scr[...] + jnp.sum(cf, axis=1, keepdims=True)
    cnt_ref[...] = carry_scr[...]
    eio_f = eio.astype(F32)
    e_rows, r_rows, g_rows = [], [], []
    for k in range(TOP_K):
        hit = jnp.where(rank == k, 1.0, 0.0)
        e_rows.append(jnp.sum(hit * eio_f, axis=0, keepdims=True))
        r_rows.append(jnp.sum(hit * in_expert, axis=0, keepdims=True))
        g_rows.append(jnp.sum(hit * gate, axis=0, keepdims=True))
    eid_ref[...] = jnp.concatenate(e_rows, axis=0).astype(jnp.int32)
    rnk_ref[...] = jnp.concatenate(r_rows, axis=0).astype(jnp.int32)
    padded = jnp.concatenate(g_rows + [jnp.zeros((LANES - TOP_K, tm), F32)], axis=0)
    gate_ref[...] = padded.T


def _router(x, g, mods3, layer, row_fn, w_router_t, b_router):
    b, r, d = x.shape
    tm = ROW_TILE
    nt = r // tm
    slot = pl.BlockSpec((TOP_K, tm), lambda bb, i: (0, bb * nt + i))
    slot_shape = jax.ShapeDtypeStruct((TOP_K, b * r), jnp.int32)
    return pl.pallas_call(
        _router_kernel,
        grid=(b, nt),
        in_specs=[
            pl.BlockSpec((None, tm, d), lambda bb, i: (bb, i, 0)),
            pl.BlockSpec((1, d), lambda bb, i: (0, 0)),
            _mod_spec(d, layer, 3, row_fn),
            _mod_spec(d, layer, 4, row_fn),
            pl.BlockSpec(w_router_t.shape, lambda bb, i: (0, 0)),
            pl.BlockSpec((N_EXPERTS, 1), lambda bb, i: (0, 0)),
        ],
        out_specs=[
            pl.BlockSpec((None, tm, d // 2), lambda bb, i: (bb, i, 0)),
            slot,
            slot,
            pl.BlockSpec((None, tm, LANES), lambda bb, i: (bb, i, 0)),
            pl.BlockSpec((N_EXPERTS, LANES), lambda bb, i: (0, 0)),
        ],
        out_shape=[jax.ShapeDtypeStruct((b, r, d // 2), jnp.uint32), slot_shape, slot_shape,
                   jax.ShapeDtypeStruct((b, r, LANES), F32), jax.ShapeDtypeStruct((N_EXPERTS, LANES), F32)],
        scratch_shapes=[pltpu.VMEM((N_EXPERTS, LANES), F32)],
        compiler_params=_cparams(("arbitrary", "arbitrary")),
        name="moe_router",
    )(x, g.reshape(1, d), mods3, mods3, w_router_t, b_router.reshape(N_EXPERTS, 1))


def _moe_plan(counts, n_rows):
    blk = EXPERT_BLK
    nb = n_rows // blk
    ends = jnp.cumsum(counts)
    starts = ends - counts
    count_le = lambda sorted_vals, q: jnp.sum(sorted_vals[None, :] <= q[:, None], axis=1, dtype=jnp.int32)
    first = jnp.arange(nb, dtype=jnp.int32) * blk
    e_lo = count_le(ends, first)
    e_hi = count_le(ends, first + (blk - 1))
    n_pair = e_hi - e_lo + 1
    p_end = jnp.cumsum(n_pair)
    p_start = p_end - n_pair
    i = jnp.arange(nb + N_EXPERTS - 1, dtype=jnp.int32)
    j = jnp.minimum(count_le(p_end, i), nb - 1)
    valid = i < p_end[-1]
    e = jnp.where(valid, e_lo[j] + i - p_start[j], e_hi[nb - 1]).astype(jnp.int32)
    bounds = jnp.concatenate([starts, ends[-1:]]).astype(jnp.int32)
    return j, e, valid.astype(jnp.int32), bounds


def _positions_kernel(starts_ref, eid_ref, rnk_ref, pos_ref):
    eid = eid_ref[...]
    pos = rnk_ref[...]
    for e in range(N_EXPERTS):
        pos = pos + jnp.where(eid == e, starts_ref[e], 0)
    pos_ref[...] = pos


def _positions(eid, rnk, starts):
    full = pl.BlockSpec(eid.shape, lambda: (0, 0))
    return pl.pallas_call(
        _positions_kernel,
        in_specs=[pl.BlockSpec(memory_space=pltpu.SMEM), full, full],
        out_specs=full,
        out_shape=jax.ShapeDtypeStruct(eid.shape, jnp.int32),
        compiler_params=pltpu.CompilerParams(vmem_limit_bytes=VMEM_LIMIT),
        name="moe_positions",
    )(starts, eid, rnk)


def _dispatch_kernel(pos_ref, h_ref, xs_ref, sem):
    tm = h_ref.shape[0]

    def issue(t, carry):
        for k in range(TOP_K):
            pltpu.make_async_copy(h_ref.at[pl.ds(t, 1)], xs_ref.at[pl.ds(pos_ref[k, t], 1)],
                                  sem).start(priority=k % 2)
        return carry

    lax.fori_loop(0, tm, issue, 0)
    for _ in range(TOP_K):
        pltpu.make_async_copy(h_ref, xs_ref.at[pl.ds(0, tm)], sem).wait()


def _dispatch(h2, pos):
    t, w = h2.shape
    tm = ROW_TILE
    return pl.pallas_call(
        _dispatch_kernel,
        grid=(t // tm,),
        in_specs=[
            pl.BlockSpec((TOP_K, tm), lambda i: (0, i), memory_space=pltpu.SMEM),
            pl.BlockSpec((tm, w), lambda i: (i, 0)),
        ],
        out_specs=pl.BlockSpec(memory_space=pl.ANY),
        out_shape=jax.ShapeDtypeStruct((t * TOP_K, w), h2.dtype),
        scratch_shapes=[pltpu.SemaphoreType.DMA],
        compiler_params=_cparams(("arbitrary",)),
        name="moe_dispatch",
    )(pos, h2)


def _grouped_kernel(pb_ref, pe_ref, pv_ref, bnd_ref, xs_ref, wg_ref, wu_ref, wd_ref, y_ref, wgb, wub, wdb):
    i = pl.program_id(0)
    prev = jnp.maximum(i - 1, 0)
    j = pb_ref[i]
    e = pe_ref[i]
    blk = xs_ref.shape[0]

    @pl.when((i == 0) | (pb_ref[prev] != j))
    def _():
        y_ref[...] = jnp.zeros_like(y_ref)

    @pl.when((i == 0) | (pe_ref[prev] != e))
    def _():
        wgb[...] = wg_ref[...].astype(BF16)
        wub[...] = wu_ref[...].astype(BF16)
        wdb[...] = wd_ref[...].astype(BF16)

    @pl.when(pv_ref[i] == 1)
    def _():
        xw = xs_ref[...]
        a = _packed_dot(xw, wgb)
        u = _packed_dot(xw, wub)
        yv = jnp.dot((_silu(a) * u).astype(BF16), wdb[...], preferred_element_type=F32)
        rows = j * blk + lax.broadcasted_iota(jnp.int32, (blk, 1), 0)
        own = (rows >= bnd_ref[e]) & (rows < bnd_ref[e + 1])
        y_ref[...] = jnp.where(own, _pack_bf16(yv), y_ref[...])


def _grouped(pb, pe, pv, bounds, xs, wg, wu, wd, layer):
    p, half = xs.shape
    d = 2 * half
    blk = EXPERT_BLK
    grid_spec = pltpu.PrefetchScalarGridSpec(
        num_scalar_prefetch=4,
        grid=(pb.shape[0],),
        in_specs=[
            pl.BlockSpec((blk, half), lambda i, pb, pe, pv, bnd: (pb[i], 0)),
            pl.BlockSpec((None, None, d, D_EXPERT), lambda i, pb, pe, pv, bnd: (layer, pe[i], 0, 0)),
            pl.BlockSpec((None, None, d, D_EXPERT), lambda i, pb, pe, pv, bnd: (layer, pe[i], 0, 0)),
            pl.BlockSpec((None, None, D_EXPERT, d), lambda i, pb, pe, pv, bnd: (layer, pe[i], 0, 0)),
        ],
        out_specs=pl.BlockSpec((blk, half), lambda i, pb, pe, pv, bnd: (pb[i], 0)),
        scratch_shapes=[
            pltpu.VMEM((d, D_EXPERT), BF16),
            pltpu.VMEM((d, D_EXPERT), BF16),
            pltpu.VMEM((D_EXPERT, d), BF16),
        ],
    )
    return pl.pallas_call(
        _grouped_kernel,
        grid_spec=grid_spec,
        out_shape=jax.ShapeDtypeStruct((p, half), jnp.uint32),
        compiler_params=_cparams(("arbitrary",)),
        name="moe_grouped_experts",
    )(pb, pe, pv, bounds, xs, wg, wu, wd)


def _combine_kernel(*refs, final):
    pos_ref, y_ref, gate_ref, h_ref, x_ref, g2_ref, sg_ref, su_ref, sd_ref = refs[:9]
    o_ref, buf, sem = refs[-3:]
    tm = x_ref.shape[0]

    def issue(t, carry):
        for k in range(TOP_K):
            pltpu.make_async_copy(y_ref.at[pl.ds(pos_ref[k, t], 1)], buf.at[k, pl.ds(t, 1)],
                                  sem).start(priority=k % 2)
        return carry

    lax.fori_loop(0, tm, issue, 0)
    hw = h_ref[...]
    a = _packed_dot(hw, sg_ref)
    u = _packed_dot(hw, su_ref)
    acc = jnp.dot((_silu(a) * u).astype(BF16), sd_ref[...], preferred_element_type=F32)
    for k in range(TOP_K):
        pltpu.make_async_copy(y_ref.at[pl.ds(0, tm)], buf.at[k], sem).wait()
    g = gate_ref[...]
    half = hw.shape[-1]
    acc_hi = acc[:, :half]
    acc_lo = acc[:, half:]
    for k in range(TOP_K):
        hi, lo = _unpack_bf16(buf[k])
        acc_hi = acc_hi + g[:, k:k + 1] * hi
        acc_lo = acc_lo + g[:, k:k + 1] * lo
    x = x_ref[...] + g2_ref[...] * jnp.concatenate([acc_hi, acc_lo], axis=-1)
    if final:
        gf_ref = refs[9]
        ms = jnp.mean(x * x, axis=-1, keepdims=True)
        x = x * lax.rsqrt(ms + NORM_EPS) * gf_ref[...]
    o_ref[...] = x


def _combine(pos, y, gates, h2, x, mods3, layer, row_fn, sg, su, sd, g_final=None):
    b, r, d = x.shape
    tm = ROW_TILE
    nt = r // tm
    tile = pl.BlockSpec((None, tm, d), lambda bb, i: (bb, i, 0))
    in_specs = [
        pl.BlockSpec((TOP_K, tm), lambda bb, i: (0, bb * nt + i), memory_space=pltpu.SMEM),
        pl.BlockSpec(memory_space=pl.ANY),
        pl.BlockSpec((None, tm, LANES), lambda bb, i: (bb, i, 0)),
        pl.BlockSpec((None, tm, d // 2), lambda bb, i: (bb, i, 0)),
        tile,
        _mod_spec(d, layer, 5, row_fn),
        pl.BlockSpec(sg.shape, lambda bb, i: (0, 0)),
        pl.BlockSpec(su.shape, lambda bb, i: (0, 0)),
        pl.BlockSpec(sd.shape, lambda bb, i: (0, 0)),
    ]
    args = [pos, y, gates, h2, x, mods3, sg, su, sd]
    if g_final is not None:
        in_specs.append(pl.BlockSpec((1, d), lambda bb, i: (0, 0)))
        args.append(g_final.reshape(1, d))
    return pl.pallas_call(
        functools.partial(_combine_kernel, final=g_final is not None),
        grid=(b, nt),
        in_specs=in_specs,
        out_specs=tile,
        out_shape=jax.ShapeDtypeStruct((b, r, d), F32),
        scratch_shapes=[pltpu.VMEM((TOP_K, tm, d // 2), jnp.uint32), pltpu.SemaphoreType.DMA],
        compiler_params=_cparams(("arbitrary", "arbitrary")),
        name="moe_combine",
    )(*args)


def _moe(x, g_ffn, mods3, layer, row_fn, w_router, b_router, w_e_gate, w_e_up, w_e_down,
         ws_gate, ws_up, ws_down, g_final=None):
    b, r, d = x.shape
    h2, eid, rnk, gates, cnt = _router(x, g_ffn, mods3, layer, row_fn, w_router.T, b_router)
    pb, pe, pv, bounds = _moe_plan(cnt[:, 0].astype(jnp.int32), b * r * TOP_K)
    pos = _positions(eid, rnk, bounds[:N_EXPERTS])
    xs = _dispatch(h2.reshape(b * r, d // 2), pos)
    y = _grouped(pb, pe, pv, bounds, xs, w_e_gate, w_e_up, w_e_down, layer)
    return _combine(pos, y, gates, h2, x, mods3, layer, row_fn,
                    ws_gate.astype(BF16), ws_up.astype(BF16), ws_down.astype(BF16), g_final)


def kernel(x, c, ctx, c_ctx, w_mod, b_mod, g_mix, g_ffn, g_final, ab_w_in, ab_w_out, ab_conv_w, ab_conv_b, ab_w_r, ab_b_r, ab_w_i, ab_b_i, ab_lam, ab_sink, cd_w_in, cd_w_out, cd_lam, cd_subln_g, cd_conv_w, cd_conv_b, cd_dt_bias, cd_a_log, cd_d_skip, cd_norm_g, w_router, b_router, w_e_gate, w_e_up, w_e_down, ws_gate, ws_up, ws_down):
    bsz, s_len, d = x.shape
    c_len = ctx.shape[1]
    depth = w_mod.shape[0]
    assert depth == 2 and bsz == SUBLANES, "kernels are specialised to depth 2 and batch 8"
    assert c_len % ROW_TILE == 0 and s_len % ROW_TILE == 0
    nct_row = c_len // ROW_TILE
    nct_time = c_len // TIME_TILE

    c_all = jnp.concatenate([c, c_ctx[None], jnp.zeros((MOD_ROWS - bsz - 1, d), F32)], axis=0)
    mods3 = _modulations(c_all, w_mod, b_mod).reshape(depth * MOD_ROWS, 1, N_MOD * d)
    rope_tabs = _rope_tables(c_len, s_len)
    xc = jnp.concatenate([ctx, x], axis=1)
    row_mixed = lambda bb, i: jnp.where(i < nct_row, SUBLANES, bb)
    row_latent = lambda bb, i: bb

    w_in = ab_w_in[0].astype(BF16)
    q_hi = LRU_WIDTH + WIN_HEADS * HEAD_DIM
    x_hi = q_hi + LRU_WIDTH
    k_hi = x_hi + WIN_KV_HEADS * HEAD_DIM
    gate, q, xa, k, v = _project(xc, g_mix[0], mods3, 0, nct_row, rope_tabs, [
        (w_in[:, :LRU_WIDTH], False, F32, False),
        (w_in[:, LRU_WIDTH:q_hi], True, BF16, False),
        (w_in[:, q_hi:x_hi], False, F32, True),
        (w_in[:, x_hi:k_hi], True, BF16, False),
        (w_in[:, k_hi:], False, BF16, False),
    ])
    l_len = c_len + s_len
    w_gates = jnp.stack([jnp.concatenate([_block_diag(ab_w_r[0, dd]), _block_diag(ab_w_i[0, dd])], axis=1)
                         for dd in range(2)]).astype(BF16)
    b_gates = jnp.concatenate([ab_b_r[0], ab_b_i[0]], axis=-1).reshape(2, 1, 2 * LRU_WIDTH)
    rec = _rglru(xa.reshape(l_len, bsz, LRU_WIDTH), ab_conv_w[0], ab_conv_b[0], w_gates, b_gates,
                 ab_lam[0].reshape(2, 1, LRU_WIDTH), nct_time)
    att = _win_attention(q, k, v, ab_sink[0], c_len)
    w_out = ab_w_out[0].astype(BF16)
    xc = _out_even(xc, rec.reshape(2, l_len, bsz * LRU_WIDTH), gate, att, w_out[:LRU_WIDTH], w_out[LRU_WIDTH:],
                   mods3, 0, nct_row)
    xc = _moe(xc, g_ffn[0], mods3, 0, row_mixed, w_router[0], b_router[0], w_e_gate, w_e_up, w_e_down,
              ws_gate[0], ws_up[0], ws_down[0])

    w_in = cd_w_in[0].astype(BF16)
    qk = DIFF_HEADS * 2 * DIFF_DH
    z_hi = qk + SSD_INNER
    k_hi = z_hi + qk
    v_hi = k_hi + qk
    x_hi = v_hi + SSD_CONV_DIM
    w_dt = jnp.pad(w_in[:, x_hi:], ((0, 0), (0, LANES - 2 * SSD_HEADS)))
    q, z, k, v, xbc, dt = _project(xc, g_mix[1], mods3, 1, nct_row, rope_tabs, [
        (w_in[:, :qk], True, BF16, False),
        (w_in[:, qk:z_hi], False, F32, False),
        (w_in[:, z_hi:k_hi], True, BF16, False),
        (w_in[:, k_hi:v_hi], False, BF16, False),
        (w_in[:, v_hi:x_hi], False, F32, False),
        (w_dt, False, F32, False),
    ])
    lam_init = 0.8 - 0.6 * math.exp(-0.3 * 1)
    diff = _diff_attention(q, k, v, cd_lam[0], cd_subln_g[0], lam_init, c_len)
    y2 = _ssd(xbc, dt, cd_conv_w[0], cd_conv_b[0], cd_dt_bias[0], cd_a_log[0], cd_d_skip[0], nct_time)
    w_out = cd_w_out[0].astype(BF16)
    xl = _out_odd(xc, diff, y2, z, cd_norm_g[0], w_out[:qk], w_out[qk:], mods3, 1, c_len)
    return _moe(xl, g_ffn[1], mods3, 1, row_latent, w_router[1], b_router[1], w_e_gate, w_e_up, w_e_down,
                ws_gate[1], ws_up[1], ws_down[1], g_final=g_final)
```

```python
import functools
import math

import jax
import jax.numpy as jnp
from jax import lax
from jax.experimental import pallas as pl
from jax.experimental.pallas import tpu as pltpu

F32 = jnp.float32
BF16 = jnp.bfloat16
HIGHEST = lax.Precision.HIGHEST

GRID_W = 64
N_MOD = 6
NORM_EPS = 1e-6
ROPE_BASE = 10000.0
CONV_W = 4

LRU_WIDTH = 512
LRU_BLOCKS = 8
LRU_C = 8.0

HEAD_DIM = 64
WIN_HEADS = 8
WIN_KV_HEADS = 2
WINDOW = 128

DIFF_HEADS = 4
DIFF_DH = 64

SSD_HEADS = 8
SSD_HEAD_DIM = 64
SSD_INNER = SSD_HEADS * SSD_HEAD_DIM
SSD_GROUPS = 2
SSD_STATE = 128
SSD_CONV_DIM = SSD_INNER + 2 * SSD_GROUPS * SSD_STATE

N_EXPERTS = 64
N_EXPERT_GROUPS = 8
TOPK_GROUPS = 4
TOP_K = 8
D_EXPERT = 256
ROUTED_SCALE = 2.5

LANES = 128
SUBLANES = 8
MOD_ROWS = 16
TIME_TILE = 128
ROW_TILE = 256
EXPERT_BLK = 512
ROW_CHUNKS = 4
VMEM_LIMIT = 48 * 1024 * 1024
NEG_BIG = -1e30


def _cparams(sem):
    return pltpu.CompilerParams(dimension_semantics=sem, vmem_limit_bytes=VMEM_LIMIT)


def _nt_dot(a, b):
    return lax.dot_general(a, b, (((1,), (1,)), ((), ())), preferred_element_type=F32)


def _softplus(x):
    return jnp.maximum(x, 0.0) + jnp.log1p(jnp.exp(-jnp.abs(x)))


def _silu(x):
    return x * jax.nn.sigmoid(x)


def _pack_bf16(x):
    half = x.shape[-1] // 2
    bits = pltpu.bitcast(x.astype(BF16).astype(F32), jnp.uint32)
    return bits[:, :half] | (bits[:, half:] >> 16)


def _unpack_bf16(w):
    hi = pltpu.bitcast(w & jnp.uint32(0xFFFF0000), F32)
    lo = pltpu.bitcast(w << 16, F32)
    return hi, lo


def _store_tile_rows(ref, w):
    n = w.shape[0]
    chunks = w.shape[1] // LANES
    for j in range(SUBLANES):
        piece = w[:, j * LANES:(j + 1) * LANES] if j < chunks else jnp.zeros((n, LANES), w.dtype)
        ref[pl.ds(j, n, stride=SUBLANES), :] = piece


def _load_tile_rows(ref, n, chunks):
    return jnp.concatenate([ref[pl.ds(j, n, stride=SUBLANES), :] for j in range(chunks)], axis=1)


def _packed_dot(w, weight_ref):
    half = w.shape[-1]
    hi, lo = _unpack_bf16(w)
    return (jnp.dot(hi.astype(BF16), weight_ref[:half, :], preferred_element_type=F32)
            + jnp.dot(lo.astype(BF16), weight_ref[half:, :], preferred_element_type=F32))


def _mod_kernel(c_ref, w_ref, b_ref, o_ref):
    c = c_ref[...]
    s = _silu(c)
    o_ref[...] = jnp.dot(s, w_ref[...], preferred_element_type=F32, precision=HIGHEST) + b_ref[...]


def _modulations(c_all, w_mod, b_mod):
    depth, d, _ = w_mod.shape
    return pl.pallas_call(
        _mod_kernel,
        grid=(depth, N_MOD),
        in_specs=[
            pl.BlockSpec((MOD_ROWS, d), lambda l, k: (0, 0)),
            pl.BlockSpec((None, d, d), lambda l, k: (l, 0, k)),
            pl.BlockSpec((None, 1, d), lambda l, k: (l, 0, k)),
        ],
        out_specs=pl.BlockSpec((None, MOD_ROWS, d), lambda l, k: (l, 0, k)),
        out_shape=jax.ShapeDtypeStruct((depth, MOD_ROWS, N_MOD * d), F32),
        compiler_params=_cparams(("arbitrary", "arbitrary")),
        name="adaln_modulation",
    )(c_all, w_mod, b_mod.reshape(depth, 1, N_MOD * d))


def _mod_spec(d, layer, chunk, row_fn):
    return pl.BlockSpec((None, 1, d), lambda b, i: (layer * MOD_ROWS + row_fn(b, i), 0, chunk))


def _norm_mod(x, g, sh, sc):
    ms = jnp.mean(x * x, axis=-1, keepdims=True)
    return (x * lax.rsqrt(ms + NORM_EPS) * g) * (1.0 + sc) + sh


def _rope(y, cos, sa, sb):
    n = y.shape[-1]
    half = HEAD_DIM // 2
    return y * cos + pltpu.roll(y, n - half, 1) * sa + pltpu.roll(y, half, 1) * sb


def _proj_kernel(*refs, ropes):
    n = len(ropes)
    x_ref, g_ref, sh_ref, sc_ref, cos_ref, sa_ref, sb_ref = refs[:7]
    w_refs = refs[7:7 + n]
    o_refs = refs[7 + n:]
    h = _norm_mod(x_ref[...], g_ref[...], sh_ref[...], sc_ref[...]).astype(BF16)
    for w_ref, o_ref, rope in zip(w_refs, o_refs, ropes):
        y = jnp.dot(h, w_ref[...], preferred_element_type=F32)
        if rope:
            w = y.shape[-1]
            y = _rope(y, cos_ref[:, :w], sa_ref[:, :w], sb_ref[:, :w])
        o_ref[...] = y.astype(o_ref.dtype)


def _project(xc, g, mods3, layer, nct, rope_tabs, groups):
    b, l, d = xc.shape
    tm = ROW_TILE
    row = lambda bb, i: jnp.where(i < nct, SUBLANES, bb)
    rw = rope_tabs[0].shape[-1]
    in_specs = [
        pl.BlockSpec((None, tm, d), lambda bb, i: (bb, i, 0)),
        pl.BlockSpec((1, d), lambda bb, i: (0, 0)),
        _mod_spec(d, layer, 0, row),
        _mod_spec(d, layer, 1, row),
    ] + [pl.BlockSpec((tm, rw), lambda bb, i: (i, 0))] * 3
    out_specs, out_shapes = [], []
    for w, _, dt, time_major in groups:
        n = w.shape[1]
        in_specs.append(pl.BlockSpec((d, n), lambda bb, i: (0, 0)))
        if time_major:
            out_specs.append(pl.BlockSpec((tm, n), lambda bb, i: (i, bb)))
            out_shapes.append(jax.ShapeDtypeStruct((l, b * n), dt))
        else:
            out_specs.append(pl.BlockSpec((None, tm, n), lambda bb, i: (bb, i, 0)))
            out_shapes.append(jax.ShapeDtypeStruct((b, l, n), dt))
    return pl.pallas_call(
        functools.partial(_proj_kernel, ropes=tuple(gp[1] for gp in groups)),
        grid=(b, l // tm),
        in_specs=in_specs,
        out_specs=out_specs,
        out_shape=out_shapes,
        compiler_params=_cparams(("arbitrary", "arbitrary")),
        name="norm_mod_project",
    )(xc, g.reshape(1, d), mods3, mods3, *rope_tabs, *[gp[0] for gp in groups])


def _rope_tables(c_len, s_len):
    rows = s_len // GRID_W
    row = jnp.repeat(jnp.arange(rows), GRID_W).astype(F32)
    col = jnp.tile(jnp.arange(GRID_W), rows).astype(F32)
    n = HEAD_DIM // 4
    inv = ROPE_BASE ** (-jnp.arange(n, dtype=F32) / n)
    ang = jnp.concatenate([row[:, None] * inv, col[:, None] * inv], axis=-1)
    cos, sin = jnp.cos(ang), jnp.sin(ang)
    zero = jnp.zeros_like(sin)
    reps = WIN_HEADS
    cos_t = jnp.tile(jnp.concatenate([cos, cos], axis=-1), (1, reps))
    sa_t = jnp.tile(jnp.concatenate([-sin, zero], axis=-1), (1, reps))
    sb_t = jnp.tile(jnp.concatenate([zero, sin], axis=-1), (1, reps))
    w = cos_t.shape[-1]
    pad1 = jnp.ones((c_len, w), F32)
    pad0 = jnp.zeros((c_len, w), F32)
    return (jnp.concatenate([pad1, cos_t], 0), jnp.concatenate([pad0, sa_t], 0),
            jnp.concatenate([pad0, sb_t], 0))


def _seq_tile(d, g, nct, nt):
    rev = jnp.where(g < nct, nct - 1 - g, nt - 1 - (g - nct))
    return jnp.where(d == 0, g, rev)


def _rglru_kernel(x_ref, xp_ref, xn_ref, cw_ref, cb_ref, w_ref, bias_ref, lam_ref, o_ref,
                  ext_scr, a_scr, b_scr, h_scr, *, ts, nct, nt, sub):
    d = pl.program_id(0)
    g = pl.program_id(1)
    tile = _seq_tile(d, g, nct, nt)
    bsz, width = h_scr.shape
    pv = jnp.where((tile == 0) | (tile == nct), 0.0, 1.0)
    nv = jnp.where((tile == nct - 1) | (tile == nt - 1), 0.0, 1.0)
    ext_scr[0:1] = xp_ref[...] * pv
    ext_scr[1:ts + 1] = x_ref[...]
    ext_scr[ts + 1:ts + 3] = xn_ref[...] * nv

    @pl.when(g == 0)
    def _():
        h_scr[...] = jnp.zeros_like(h_scr)

    neg_sp = -LRU_C * _softplus(-lam_ref[...])

    def prep(c, carry):
        r0 = pl.multiple_of(c * sub, sub)
        e = ext_scr[pl.ds(r0, sub + CONV_W - 1)]
        u = cb_ref[...] + cw_ref[0] * e[0:sub]
        for j in range(1, CONV_W):
            u = u + cw_ref[j] * e[j:j + sub]
        u2 = u.reshape(sub * bsz, width)
        gts = jnp.dot(u2.astype(BF16), w_ref[...], preferred_element_type=F32) + bias_ref[...]
        r = jax.nn.sigmoid(gts[:, :width])
        ig = jax.nn.sigmoid(gts[:, width:])
        log_a = neg_sp * r
        a = jnp.exp(log_a)
        mult = jnp.sqrt(1.0 - a * a)
        a_scr[pl.ds(r0, sub)] = a.reshape(sub, bsz, width)
        b_scr[pl.ds(r0, sub)] = (mult * ig * u2).reshape(sub, bsz, width)
        return carry

    lax.fori_loop(0, ts // sub, prep, 0)

    def step(t, h):
        tt = jnp.where(d == 0, t, ts - 1 - t)
        h = a_scr[tt] * h + b_scr[tt]
        o_ref[tt] = h
        return h

    h_scr[...] = lax.fori_loop(0, ts, step, h_scr[...], unroll=8)


def _rglru(xa_tm, conv_w, conv_b, w_gates, b_gates, lam, nct):
    l, bsz, width = xa_tm.shape
    ts = TIME_TILE
    nt = l // ts
    tile = lambda d, g: _seq_tile(d, g, nct, nt)
    kern = functools.partial(_rglru_kernel, ts=ts, nct=nct, nt=nt, sub=16)
    return pl.pallas_call(
        kern,
        grid=(2, nt),
        in_specs=[
            pl.BlockSpec((ts, bsz, width), lambda d, g: (tile(d, g), 0, 0)),
            pl.BlockSpec((1, bsz, width), lambda d, g: (jnp.maximum(tile(d, g) * ts - 1, 0), 0, 0)),
            pl.BlockSpec((2, bsz, width),
                         lambda d, g: (jnp.minimum((tile(d, g) + 1) * (ts // 2), l // 2 - 1), 0, 0)),
            pl.BlockSpec((CONV_W, 1, width), lambda d, g: (0, 0, 0)),
            pl.BlockSpec((1, width), lambda d, g: (0, 0)),
            pl.BlockSpec((None, width, 2 * width), lambda d, g: (d, 0, 0)),
            pl.BlockSpec((None, 1, 2 * width), lambda d, g: (d, 0, 0)),
            pl.BlockSpec((None, 1, width), lambda d, g: (d, 0, 0)),
        ],
        out_specs=pl.BlockSpec((None, ts, bsz, width), lambda d, g: (d, tile(d, g), 0, 0)),
        out_shape=jax.ShapeDtypeStruct((2, l, bsz, width), F32),
        scratch_shapes=[
            pltpu.VMEM((ts + CONV_W - 1, bsz, width), F32),
            pltpu.VMEM((ts, bsz, width), F32),
            pltpu.VMEM((ts, bsz, width), F32),
            pltpu.VMEM((bsz, width), F32),
        ],
        compiler_params=_cparams(("arbitrary", "arbitrary")),
        name="rglru_scan",
    )(xa_tm, xa_tm, xa_tm, conv_w.reshape(CONV_W, 1, width), conv_b.reshape(1, width),
      w_gates, b_gates, lam)


def _block_diag(w):
    nb, c, dd = w.shape
    eye = jnp.eye(nb, dtype=w.dtype)
    return (eye[:, None, :, None] * w[:, :, None, :]).reshape(nb * c, nb * dd)


def _win_attn_kernel(sink_ref, q_ref, k_ref, v_ref, o_ref, *, c_len, l_len, nqc):
    j = pl.program_id(1)
    blk = q_ref.shape[0]
    grp = WIN_HEADS // WIN_KV_HEADS
    band = 3 * blk

    def heads(body):
        for hk in range(WIN_KV_HEADS):
            ksl = slice(hk * HEAD_DIM, (hk + 1) * HEAD_DIM)
            for gq in range(grp):
                head = hk * grp + gq
                hsl = slice(head * HEAD_DIM, (head + 1) * HEAD_DIM)
                o_ref[:, hsl] = body(q_ref[:, hsl], ksl, sink_ref[head]).astype(o_ref.dtype)

    @pl.when(j < nqc)
    def _():
        def body(qh, ksl, sink):
            s = _nt_dot(qh, k_ref[0:c_len, ksl]) * (HEAD_DIM ** -0.5)
            m = jnp.maximum(jnp.max(s, axis=-1, keepdims=True), sink)
            p = jnp.exp(s - m)
            den = jnp.sum(p, axis=-1, keepdims=True) + jnp.exp(sink - m)
            o = jnp.dot(p.astype(BF16), v_ref[0:c_len, ksl], preferred_element_type=F32)
            return o / den
        heads(body)

    @pl.when(j >= nqc)
    def _():
        jb = j - nqc
        start = jnp.clip(c_len + (jb - 1) * blk, c_len - blk, l_len - band)
        start = pl.multiple_of(start, blk)
        qpos = jb * blk + lax.broadcasted_iota(jnp.int32, (blk, band), 0)
        kpos = start - c_len + lax.broadcasted_iota(jnp.int32, (blk, band), 1)
        valid = (jnp.abs(qpos - kpos) <= WINDOW) & (kpos >= 0)

        def body(qh, ksl, sink):
            sc = _nt_dot(qh, k_ref[0:c_len, ksl]) * (HEAD_DIM ** -0.5)
            sb = _nt_dot(qh, k_ref[pl.ds(start, band), ksl]) * (HEAD_DIM ** -0.5)
            sb = jnp.where(valid, sb, NEG_BIG)
            m = jnp.maximum(jnp.maximum(jnp.max(sc, axis=-1, keepdims=True),
                                        jnp.max(sb, axis=-1, keepdims=True)), sink)
            pc = jnp.exp(sc - m)
            pb = jnp.exp(sb - m)
            den = (jnp.sum(pc, axis=-1, keepdims=True) + jnp.sum(pb, axis=-1, keepdims=True)
                   + jnp.exp(sink - m))
            o = (jnp.dot(pc.astype(BF16), v_ref[0:c_len, ksl], preferred_element_type=F32)
                 + jnp.dot(pb.astype(BF16), v_ref[pl.ds(start, band), ksl], preferred_element_type=F32))
            return o / den
        heads(body)


def _win_attention(q, k, v, sink, c_len):
    b, l, qw = q.shape
    kw = k.shape[-1]
    blk = TIME_TILE
    kern = functools.partial(_win_attn_kernel, c_len=c_len, l_len=l, nqc=c_len // blk)
    return pl.pallas_call(
        kern,
        grid=(b, l // blk),
        in_specs=[
            pl.BlockSpec(memory_space=pltpu.SMEM),
            pl.BlockSpec((None, blk, qw), lambda bb, j: (bb, j, 0)),
            pl.BlockSpec((None, l, kw), lambda bb, j: (bb, 0, 0)),
            pl.BlockSpec((None, l, kw), lambda bb, j: (bb, 0, 0)),
        ],
        out_specs=pl.BlockSpec((None, blk, qw), lambda bb, j: (bb, j, 0)),
        out_shape=jax.ShapeDtypeStruct((b, l, qw), BF16),
        compiler_params=_cparams(("arbitrary", "arbitrary")),
        name="window_attention",
    )(sink, q, k, v)


def _out_even_kernel(x_ref, rec_ref, gate_ref, att_ref, wa_ref, wb_ref, g1_ref, o_ref):
    lru = (rec_ref[0] + rec_ref[1]) * jax.nn.gelu(gate_ref[...])
    y = (jnp.dot(lru.astype(BF16), wa_ref[...], preferred_element_type=F32)
         + jnp.dot(att_ref[...], wb_ref[...], preferred_element_type=F32))
    o_ref[...] = x_ref[...] + g1_ref[...] * y


def _out_even(xc, rec2, gate, att, w_a, w_b, mods3, layer, nct):
    b, l, d = xc.shape
    tm = ROW_TILE
    w = gate.shape[-1]
    row = lambda bb, i: jnp.where(i < nct, SUBLANES, bb)
    return pl.pallas_call(
        _out_even_kernel,
        grid=(b, l // tm),
        in_specs=[
            pl.BlockSpec((None, tm, d), lambda bb, i: (bb, i, 0)),
            pl.BlockSpec((2, tm, w), lambda bb, i: (0, i, bb)),
            pl.BlockSpec((None, tm, w), lambda bb, i: (bb, i, 0)),
            pl.BlockSpec((None, tm, att.shape[-1]), lambda bb, i: (bb, i, 0)),
            pl.BlockSpec(w_a.shape, lambda bb, i: (0, 0)),
            pl.BlockSpec(w_b.shape, lambda bb, i: (0, 0)),
            _mod_spec(d, layer, 2, row),
        ],
        out_specs=pl.BlockSpec((None, tm, d), lambda bb, i: (bb, i, 0)),
        out_shape=jax.ShapeDtypeStruct((b, l, d), F32),
        compiler_params=_cparams(("arbitrary", "arbitrary")),
        name="out_proj_even",
    )(xc, rec2, gate, att, w_a, w_b, mods3)


def _diff_attn_kernel(lam_ref, g_ref, q_ref, k_ref, v_ref, o_ref, *, lam_init):
    lv = lam_ref[...]
    lam = (jnp.exp(jnp.sum(lv[0:1] * lv[1:2], axis=-1, keepdims=True))
           - jnp.exp(jnp.sum(lv[2:3] * lv[3:4], axis=-1, keepdims=True)) + lam_init)
    vw = 2 * DIFF_DH

    def softmax(qm, km):
        s = _nt_dot(qm, km) * (DIFF_DH ** -0.5)
        m = jnp.max(s, axis=-1, keepdims=True)
        p = jnp.exp(s - m)
        return p / jnp.sum(p, axis=-1, keepdims=True)

    for h in range(DIFF_HEADS):
        lo = h * vw
        p0 = softmax(q_ref[:, lo:lo + DIFF_DH], k_ref[:, lo:lo + DIFF_DH])
        p1 = softmax(q_ref[:, lo + DIFF_DH:lo + vw], k_ref[:, lo + DIFF_DH:lo + vw])
        w = (p0 - lam * p1).astype(BF16)
        o = jnp.dot(w, v_ref[:, lo:lo + vw], preferred_element_type=F32)
        ms = jnp.mean(o * o, axis=-1, keepdims=True)
        o = o * lax.rsqrt(ms + NORM_EPS) * g_ref[...]
        o_ref[:, lo:lo + vw] = (o * (1.0 - lam_init)).astype(o_ref.dtype)


def _diff_attention(q, k, v, lam_vecs, subln_g, lam_init, c_len):
    b, l, w = q.shape
    tq = TIME_TILE
    s_len = l - c_len
    off = c_len // tq
    return pl.pallas_call(
        functools.partial(_diff_attn_kernel, lam_init=lam_init),
        grid=(b, s_len // tq),
        in_specs=[
            pl.BlockSpec(lam_vecs.shape, lambda bb, j: (0, 0)),
            pl.BlockSpec((1, 2 * DIFF_DH), lambda bb, j: (0, 0)),
            pl.BlockSpec((None, tq, w), lambda bb, j: (bb, j + off, 0)),
            pl.BlockSpec((None, l, w), lambda bb, j: (bb, 0, 0)),
            pl.BlockSpec((None, l, w), lambda bb, j: (bb, 0, 0)),
        ],
        out_specs=pl.BlockSpec((None, tq, w), lambda bb, j: (bb, j, 0)),
        out_shape=jax.ShapeDtypeStruct((b, s_len, w), BF16),
        compiler_params=_cparams(("arbitrary", "arbitrary")),
        name="diff_attention",
    )(lam_vecs, subln_g.reshape(1, -1), q, k, v)


def _ssd_kernel(x_ref, xp_ref, xn_ref, dt_ref, cw_ref, cb_ref, dtb_ref, alog_ref, dsk_ref, o_ref,
                ext_scr, st_scr, *, q, nct, nt):
    d = pl.program_id(0)
    g = pl.program_id(2)
    tile = _seq_tile(d, g, nct, nt)
    pv = jnp.where((tile == 0) | (tile == nct), 0.0, 1.0)
    nv = jnp.where((tile == nct - 1) | (tile == nt - 1), 0.0, 1.0)
    ext_scr[0:SUBLANES] = xp_ref[...] * pv
    ext_scr[SUBLANES:SUBLANES + q] = x_ref[...]
    ext_scr[SUBLANES + q:2 * SUBLANES + q] = xn_ref[...] * nv

    @pl.when(g == 0)
    def _():
        st_scr[...] = jnp.zeros_like(st_scr)

    u = cb_ref[...] + cw_ref[0] * ext_scr[SUBLANES - 1:SUBLANES - 1 + q, :]
    for j in range(1, CONV_W):
        u = u + cw_ref[j] * ext_scr[SUBLANES - 1 + j:SUBLANES - 1 + j + q, :]
    act = _silu(u)

    dtr = dt_ref[...]
    dtr = jnp.where(d == 0, dtr, pltpu.roll(dtr, LANES - SSD_HEADS, 1))
    dtv = _softplus(dtr + dtb_ref[...])
    head_lane = lax.broadcasted_iota(jnp.int32, (1, LANES), 1) < SSD_HEADS
    dta = dtv * jnp.where(head_lane, -jnp.exp(alog_ref[...]), 0.0)
    ri = lax.broadcasted_iota(jnp.int32, (q, q), 0)
    ci = lax.broadcasted_iota(jnp.int32, (q, q), 1)
    keep = jnp.where(d == 0, ri - ci, ci - ri) >= 0
    cum = jnp.dot(keep.astype(F32), dta, preferred_element_type=F32, precision=HIGHEST)
    tot = jnp.sum(dta, axis=0, keepdims=True)
    cum_t = cum.T
    dt_t = dtv.T
    to_end = jnp.exp(tot - cum) * dtv
    e_cum = jnp.exp(cum)
    e_tot = jnp.exp(tot)
    dskip = dsk_ref[...] * jnp.where(d == 0, 1.0, 0.0)

    hpg = SSD_HEADS // SSD_GROUPS
    for gi in range(SSD_GROUPS):
        b_g = act[:, SSD_INNER + gi * SSD_STATE:SSD_INNER + (gi + 1) * SSD_STATE]
        c_lo = SSD_INNER + SSD_GROUPS * SSD_STATE + gi * SSD_STATE
        c_g = act[:, c_lo:c_lo + SSD_STATE].astype(BF16)
        cb = _nt_dot(c_g, b_g.astype(BF16))
        b_gt = b_g.T.astype(BF16)
        for hh in range(hpg):
            h = gi * hpg + hh
            xs = act[:, h * SSD_HEAD_DIM:(h + 1) * SSD_HEAD_DIM]
            seg = cum[:, h:h + 1] - cum_t[h:h + 1, :]
            decay = jnp.exp(jnp.where(keep, seg, NEG_BIG))
            w = (cb * decay * dt_t[h:h + 1, :]).astype(BF16)
            state = st_scr[h]
            y = jnp.dot(w, xs.astype(BF16), preferred_element_type=F32)
            y = y + jnp.dot(c_g, state.astype(BF16), preferred_element_type=F32) * e_cum[:, h:h + 1]
            y = y + dskip[:, h * SSD_HEAD_DIM:(h + 1) * SSD_HEAD_DIM] * xs
            o_ref[:, h * SSD_HEAD_DIM:(h + 1) * SSD_HEAD_DIM] = y
            s_new = jnp.dot(b_gt, (xs * to_end[:, h:h + 1]).astype(BF16), preferred_element_type=F32)
            st_scr[h] = e_tot[:, h:h + 1] * state + s_new


def _ssd(xbc, dt, conv_w, conv_b, dt_bias, a_log, d_skip, nct):
    b, l, cd = xbc.shape
    q = TIME_TILE
    nt = l // q
    tile = lambda d, bb, g: _seq_tile(d, g, nct, nt)
    r8 = q // SUBLANES
    pad = LANES - SSD_HEADS
    dtb = jnp.pad(dt_bias, ((0, 0), (0, pad))).reshape(2, 1, LANES)
    alog = jnp.pad(a_log, ((0, 0), (0, pad))).reshape(2, 1, LANES)
    dsk = jnp.repeat(d_skip, SSD_HEAD_DIM).reshape(1, SSD_INNER)
    return pl.pallas_call(
        functools.partial(_ssd_kernel, q=q, nct=nct, nt=nt),
        grid=(2, b, nt),
        in_specs=[
            pl.BlockSpec((None, q, cd), lambda d, bb, g: (bb, tile(d, bb, g), 0)),
            pl.BlockSpec((None, SUBLANES, cd),
                         lambda d, bb, g: (bb, jnp.maximum(tile(d, bb, g) * r8 - 1, 0), 0)),
            pl.BlockSpec((None, SUBLANES, cd),
                         lambda d, bb, g: (bb, jnp.minimum((tile(d, bb, g) + 1) * r8, l // SUBLANES - 1), 0)),
            pl.BlockSpec((None, q, LANES), lambda d, bb, g: (bb, tile(d, bb, g), 0)),
            pl.BlockSpec((CONV_W, 1, cd), lambda d, bb, g: (0, 0, 0)),
            pl.BlockSpec((1, cd), lambda d, bb, g: (0, 0)),
            pl.BlockSpec((None, 1, LANES), lambda d, bb, g: (d, 0, 0)),
            pl.BlockSpec((None, 1, LANES), lambda d, bb, g: (d, 0, 0)),
            pl.BlockSpec((1, SSD_INNER), lambda d, bb, g: (0, 0)),
        ],
        out_specs=pl.BlockSpec((None, None, q, SSD_INNER), lambda d, bb, g: (d, bb, tile(d, bb, g), 0)),
        out_shape=jax.ShapeDtypeStruct((2, b, l, SSD_INNER), F32),
        scratch_shapes=[
            pltpu.VMEM((q + 2 * SUBLANES, cd), F32),
            pltpu.VMEM((SSD_HEADS, SSD_STATE, SSD_HEAD_DIM), F32),
        ],
        compiler_params=_cparams(("arbitrary", "arbitrary", "arbitrary")),
        name="ssd_chunked",
    )(xbc, xbc, xbc, dt, conv_w.reshape(CONV_W, 1, cd), conv_b.reshape(1, cd), dtb, alog, dsk)


def _out_odd_kernel(x_ref, diff_ref, y_ref, z_ref, ng_ref, wa_ref, wb_ref, g1_ref, o_ref):
    yz = (y_ref[0] + y_ref[1]) * _silu(z_ref[...])
    gs = SSD_INNER // SSD_GROUPS
    parts = []
    for gi in range(SSD_GROUPS):
        seg = yz[:, gi * gs:(gi + 1) * gs]
        ms = jnp.mean(seg * seg, axis=-1, keepdims=True)
        parts.append(seg * lax.rsqrt(ms + NORM_EPS) * ng_ref[:, gi * gs:(gi + 1) * gs])
    ssd = jnp.concatenate(parts, axis=-1).astype(BF16)
    y = (jnp.dot(diff_ref[...], wa_ref[...], preferred_element_type=F32)
         + jnp.dot(ssd, wb_ref[...], preferred_element_type=F32))
    o_ref[...] = x_ref[...] + g1_ref[...] * y


def _out_odd(xc, diff, y2, z, norm_g, w_a, w_b, mods3, layer, c_len):
    b, l, d = xc.shape
    s_len = l - c_len
    tm = ROW_TILE
    off = c_len // tm
    w = SSD_INNER
    row = lambda bb, i: bb
    return pl.pallas_call(
        _out_odd_kernel,
        grid=(b, s_len // tm),
        in_specs=[
            pl.BlockSpec((None, tm, d), lambda bb, i: (bb, i + off, 0)),
            pl.BlockSpec((None, tm, diff.shape[-1]), lambda bb, i: (bb, i, 0)),
            pl.BlockSpec((2, None, tm, w), lambda bb, i: (0, bb, i + off, 0)),
            pl.BlockSpec((None, tm, w), lambda bb, i: (bb, i + off, 0)),
            pl.BlockSpec((1, w), lambda bb, i: (0, 0)),
            pl.BlockSpec(w_a.shape, lambda bb, i: (0, 0)),
            pl.BlockSpec(w_b.shape, lambda bb, i: (0, 0)),
            _mod_spec(d, layer, 2, row),
        ],
        out_specs=pl.BlockSpec((None, tm, d), lambda bb, i: (bb, i, 0)),
        out_shape=jax.ShapeDtypeStruct((b, s_len, d), F32),
        compiler_params=_cparams(("arbitrary", "arbitrary")),
        name="out_proj_odd",
    )(xc, diff, y2, z, norm_g.reshape(1, w), w_a, w_b, mods3)


def _router_kernel(x_ref, g_ref, sh_ref, sc_ref, wr_ref, br_ref, h_ref, eid_ref, rnk_ref, gate_ref, cnt_ref,
                   carry_scr):
    @pl.when((pl.program_id(0) == 0) & (pl.program_id(1) == 0))
    def _():
        carry_scr[...] = jnp.zeros_like(carry_scr)

    h = _norm_mod(x_ref[...], g_ref[...], sh_ref[...], sc_ref[...])
    _store_tile_rows(h_ref, _pack_bf16(h))
    tm = h.shape[0]
    per = N_EXPERTS // N_EXPERT_GROUPS
    logits = lax.dot_general(wr_ref[...], h, (((1,), (1,)), ((), ())),
                             preferred_element_type=F32, precision=HIGHEST)
    scores = jax.nn.sigmoid(logits)
    sel = scores + br_ref[...]
    sel3 = sel.reshape(N_EXPERT_GROUPS, per, tm)
    kio = lax.broadcasted_iota(jnp.int32, sel3.shape, 1)
    m1 = jnp.max(sel3, axis=1, keepdims=True)
    first = jnp.min(jnp.where(sel3 == m1, kio, per), axis=1, keepdims=True)
    m2 = jnp.max(jnp.where(kio == first, NEG_BIG, sel3), axis=1, keepdims=True)
    gs = m1 + m2
    gio = lax.broadcasted_iota(jnp.int32, gs.shape, 0)
    ahead = jnp.zeros(gs.shape, jnp.int32)
    for gp in range(N_EXPERT_GROUPS):
        other = gs[gp:gp + 1]
        ahead = ahead + jnp.where((other > gs) | ((other == gs) & (gp < gio)), 1, 0)
    grp_on = jnp.where(ahead < TOPK_GROUPS, 1.0, 0.0)
    selm = jnp.where(jnp.broadcast_to(grp_on, sel3.shape) > 0.5, sel3, NEG_BIG).reshape(N_EXPERTS, tm)
    eio = lax.broadcasted_iota(jnp.int32, selm.shape, 0)
    rank = jnp.zeros(selm.shape, jnp.int32)
    for e in range(N_EXPERTS):
        other = selm[e:e + 1, :]
        rank = rank + jnp.where((other > selm) | ((other == selm) & (e < eio)), 1, 0)
    chosen = rank < TOP_K
    gate = jnp.where(chosen, scores, 0.0)
    gate = gate / jnp.sum(gate, axis=0, keepdims=True) * ROUTED_SCALE
    cf = jnp.where(chosen, 1.0, 0.0)
    ti = lax.broadcasted_iota(jnp.int32, (tm, tm), 0)
    tj = lax.broadcasted_iota(jnp.int32, (tm, tm), 1)
    before = jnp.where(ti < tj, 1.0, 0.0).astype(BF16)
    in_expert = carry_scr[:, 0:1] + jnp.dot(cf.astype(BF16), before, preferred_element_type=F32)
    carry_scr[...] = carry_scr[...] + jnp.sum(cf, axis=1, keepdims=True)
    cnt_ref[...] = carry_scr[...]
    eio_f = eio.astype(F32)
    e_rows, r_rows, g_rows = [], [], []
    for k in range(TOP_K):
        hit = jnp.where(rank == k, 1.0, 0.0)
        e_rows.append(jnp.sum(hit * eio_f, axis=0, keepdims=True))
        r_rows.append(jnp.sum(hit * in_expert, axis=0, keepdims=True))
        g_rows.append(jnp.sum(hit * gate, axis=0, keepdims=True))
    eid_ref[...] = jnp.concatenate(e_rows, axis=0).astype(jnp.int32)
    rnk_ref[...] = jnp.concatenate(r_rows, axis=0).astype(jnp.int32)
    padded = jnp.concatenate(g_rows + [jnp.zeros((LANES - TOP_K, tm), F32)], axis=0)
    gate_ref[...] = padded.T


def _router(x, g, mods3, layer, row_fn, w_router_t, b_router):
    b, r, d = x.shape
    tm = ROW_TILE
    nt = r // tm
    slot = pl.BlockSpec((TOP_K, tm), lambda bb, i: (0, bb * nt + i))
    slot_shape = jax.ShapeDtypeStruct((TOP_K, b * r), jnp.int32)
    return pl.pallas_call(
        _router_kernel,
        grid=(b, nt),
        in_specs=[
            pl.BlockSpec((None, tm, d), lambda bb, i: (bb, i, 0)),
            pl.BlockSpec((1, d), lambda bb, i: (0, 0)),
            _mod_spec(d, layer, 3, row_fn),
            _mod_spec(d, layer, 4, row_fn),
            pl.BlockSpec(w_router_t.shape, lambda bb, i: (0, 0)),
            pl.BlockSpec((N_EXPERTS, 1), lambda bb, i: (0, 0)),
        ],
        out_specs=[
            pl.BlockSpec((tm * SUBLANES, LANES), lambda bb, i: (bb * nt + i, 0)),
            slot,
            slot,
            pl.BlockSpec((None, tm, LANES), lambda bb, i: (bb, i, 0)),
            pl.BlockSpec((N_EXPERTS, LANES), lambda bb, i: (0, 0)),
        ],
        out_shape=[jax.ShapeDtypeStruct((b * r * SUBLANES, LANES), jnp.uint32), slot_shape, slot_shape,
                   jax.ShapeDtypeStruct((b, r, LANES), F32), jax.ShapeDtypeStruct((N_EXPERTS, LANES), F32)],
        scratch_shapes=[pltpu.VMEM((N_EXPERTS, LANES), F32)],
        compiler_params=_cparams(("arbitrary", "arbitrary")),
        name="moe_router",
    )(x, g.reshape(1, d), mods3, mods3, w_router_t, b_router.reshape(N_EXPERTS, 1))


def _moe_plan(counts, n_rows):
    blk = EXPERT_BLK
    nb = n_rows // blk
    ends = jnp.cumsum(counts)
    starts = ends - counts
    count_le = lambda sorted_vals, q: jnp.sum(sorted_vals[None, :] <= q[:, None], axis=1, dtype=jnp.int32)
    first = jnp.arange(nb, dtype=jnp.int32) * blk
    e_lo = count_le(ends, first)
    e_hi = count_le(ends, first + (blk - 1))
    n_pair = e_hi - e_lo + 1
    p_end = jnp.cumsum(n_pair)
    p_start = p_end - n_pair
    i = jnp.arange(nb + N_EXPERTS - 1, dtype=jnp.int32)
    j = jnp.minimum(count_le(p_end, i), nb - 1)
    valid = i < p_end[-1]
    e = jnp.where(valid, e_lo[j] + i - p_start[j], e_hi[nb - 1]).astype(jnp.int32)
    bounds = jnp.concatenate([starts, ends[-1:]]).astype(jnp.int32)
    return j, e, valid.astype(jnp.int32), bounds


def _positions_kernel(starts_ref, eid_ref, rnk_ref, pos_ref):
    eid = eid_ref[...]
    pos = rnk_ref[...]
    for e in range(N_EXPERTS):
        pos = pos + jnp.where(eid == e, starts_ref[e], 0)
    pos_ref[...] = pos * SUBLANES


def _positions(eid, rnk, starts):
    full = pl.BlockSpec(eid.shape, lambda: (0, 0))
    return pl.pallas_call(
        _positions_kernel,
        in_specs=[pl.BlockSpec(memory_space=pltpu.SMEM), full, full],
        out_specs=full,
        out_shape=jax.ShapeDtypeStruct(eid.shape, jnp.int32),
        compiler_params=pltpu.CompilerParams(vmem_limit_bytes=VMEM_LIMIT),
        name="moe_positions",
    )(starts, eid, rnk)


def _tile_row(ref, first):
    return ref.at[pl.ds(pl.multiple_of(first, SUBLANES), ROW_CHUNKS)]


def _dispatch_kernel(pos_ref, h_ref, xs_ref, sem):
    tm = h_ref.shape[0] // SUBLANES

    def issue(t, carry):
        src = _tile_row(h_ref, t * SUBLANES)
        for k in range(TOP_K):
            pltpu.make_async_copy(src, _tile_row(xs_ref, pos_ref[k, t]), sem).start(priority=k % 2)
        return carry

    lax.fori_loop(0, tm, issue, 0)
    done = pl.ds(0, tm * ROW_CHUNKS)
    for _ in range(TOP_K):
        pltpu.make_async_copy(h_ref.at[done], xs_ref.at[done], sem).wait()


def _dispatch(h2, pos):
    rows, w = h2.shape
    tm = ROW_TILE
    return pl.pallas_call(
        _dispatch_kernel,
        grid=(rows // (tm * SUBLANES),),
        in_specs=[
            pl.BlockSpec((TOP_K, tm), lambda i: (0, i), memory_space=pltpu.SMEM),
            pl.BlockSpec((tm * SUBLANES, w), lambda i: (i, 0)),
        ],
        out_specs=pl.BlockSpec(memory_space=pl.ANY),
        out_shape=jax.ShapeDtypeStruct((rows * TOP_K, w), h2.dtype),
        scratch_shapes=[pltpu.SemaphoreType.DMA],
        compiler_params=_cparams(("arbitrary",)),
        name="moe_dispatch",
    )(pos, h2)


def _grouped_kernel(pb_ref, pe_ref, pv_ref, bnd_ref, xs_ref, wg_ref, wu_ref, wd_ref, y_ref, wgb, wub, wdb):
    i = pl.program_id(0)
    prev = jnp.maximum(i - 1, 0)
    j = pb_ref[i]
    e = pe_ref[i]
    blk = xs_ref.shape[0] // SUBLANES

    @pl.when((i == 0) | (pb_ref[prev] != j))
    def _():
        y_ref[...] = jnp.zeros_like(y_ref)

    @pl.when((i == 0) | (pe_ref[prev] != e))
    def _():
        wgb[...] = wg_ref[...].astype(BF16)
        wub[...] = wu_ref[...].astype(BF16)
        wdb[...] = wd_ref[...].astype(BF16)

    @pl.when(pv_ref[i] == 1)
    def _():
        xw = _load_tile_rows(xs_ref, blk, ROW_CHUNKS)
        a = _packed_dot(xw, wgb)
        u = _packed_dot(xw, wub)
        yv = jnp.dot((_silu(a) * u).astype(BF16), wdb[...], preferred_element_type=F32)
        rows = j * blk + lax.broadcasted_iota(jnp.int32, (blk, 1), 0)
        own = (rows >= bnd_ref[e]) & (rows < bnd_ref[e + 1])
        yw = _pack_bf16(yv)
        for c in range(ROW_CHUNKS):
            sl = pl.ds(c, blk, stride=SUBLANES)
            y_ref[sl, :] = jnp.where(own, yw[:, c * LANES:(c + 1) * LANES], y_ref[sl, :])


def _grouped(pb, pe, pv, bounds, xs, wg, wu, wd, layer):
    p, half = xs.shape
    d = 2 * ROW_CHUNKS * LANES
    blk = EXPERT_BLK * SUBLANES
    grid_spec = pltpu.PrefetchScalarGridSpec(
        num_scalar_prefetch=4,
        grid=(pb.shape[0],),
        in_specs=[
            pl.BlockSpec((blk, half), lambda i, pb, pe, pv, bnd: (pb[i], 0)),
            pl.BlockSpec((None, None, d, D_EXPERT), lambda i, pb, pe, pv, bnd: (layer, pe[i], 0, 0)),
            pl.BlockSpec((None, None, d, D_EXPERT), lambda i, pb, pe, pv, bnd: (layer, pe[i], 0, 0)),
            pl.BlockSpec((None, None, D_EXPERT, d), lambda i, pb, pe, pv, bnd: (layer, pe[i], 0, 0)),
        ],
        out_specs=pl.BlockSpec((blk, half), lambda i, pb, pe, pv, bnd: (pb[i], 0)),
        scratch_shapes=[
            pltpu.VMEM((d, D_EXPERT), BF16),
            pltpu.VMEM((d, D_EXPERT), BF16),
            pltpu.VMEM((D_EXPERT, d), BF16),
        ],
    )
    return pl.pallas_call(
        _grouped_kernel,
        grid_spec=grid_spec,
        out_shape=jax.ShapeDtypeStruct((p, half), jnp.uint32),
        compiler_params=_cparams(("arbitrary",)),
        name="moe_grouped_experts",
    )(pb, pe, pv, bounds, xs, wg, wu, wd)


def _combine_kernel(*refs, final):
    pos_ref, y_ref, gate_ref, h_ref, x_ref, g2_ref, sg_ref, su_ref, sd_ref = refs[:9]
    o_ref, buf, sem = refs[-3:]
    tm = x_ref.shape[0]

    def issue(t, carry):
        for k in range(TOP_K):
            pltpu.make_async_copy(_tile_row(y_ref, pos_ref[k, t]), _tile_row(buf.at[k], t * SUBLANES),
                                  sem).start(priority=k % 2)
        return carry

    lax.fori_loop(0, tm, issue, 0)
    hw = _load_tile_rows(h_ref, tm, ROW_CHUNKS)
    a = _packed_dot(hw, sg_ref)
    u = _packed_dot(hw, su_ref)
    acc = jnp.dot((_silu(a) * u).astype(BF16), sd_ref[...], preferred_element_type=F32)
    done = pl.ds(0, tm * ROW_CHUNKS)
    for k in range(TOP_K):
        pltpu.make_async_copy(y_ref.at[done], buf.at[k, done], sem).wait()
    g = gate_ref[...]
    half = hw.shape[-1]
    acc_hi = acc[:, :half]
    acc_lo = acc[:, half:]
    for k in range(TOP_K):
        hi, lo = _unpack_bf16(_load_tile_rows(buf.at[k], tm, ROW_CHUNKS))
        acc_hi = acc_hi + g[:, k:k + 1] * hi
        acc_lo = acc_lo + g[:, k:k + 1] * lo
    x = x_ref[...] + g2_ref[...] * jnp.concatenate([acc_hi, acc_lo], axis=-1)
    if final:
        gf_ref = refs[9]
        ms = jnp.mean(x * x, axis=-1, keepdims=True)
        x = x * lax.rsqrt(ms + NORM_EPS) * gf_ref[...]
    o_ref[...] = x


def _combine(pos, y, gates, h2, x, mods3, layer, row_fn, sg, su, sd, g_final=None):
    b, r, d = x.shape
    tm = ROW_TILE
    nt = r // tm
    tile = pl.BlockSpec((None, tm, d), lambda bb, i: (bb, i, 0))
    in_specs = [
        pl.BlockSpec((TOP_K, tm), lambda bb, i: (0, bb * nt + i), memory_space=pltpu.SMEM),
        pl.BlockSpec(memory_space=pl.ANY),
        pl.BlockSpec((None, tm, LANES), lambda bb, i: (bb, i, 0)),
        pl.BlockSpec((tm * SUBLANES, LANES), lambda bb, i: (bb * nt + i, 0)),
        tile,
        _mod_spec(d, layer, 5, row_fn),
        pl.BlockSpec(sg.shape, lambda bb, i: (0, 0)),
        pl.BlockSpec(su.shape, lambda bb, i: (0, 0)),
        pl.BlockSpec(sd.shape, lambda bb, i: (0, 0)),
    ]
    args = [pos, y, gates, h2, x, mods3, sg, su, sd]
    if g_final is not None:
        in_specs.append(pl.BlockSpec((1, d), lambda bb, i: (0, 0)))
        args.append(g_final.reshape(1, d))
    return pl.pallas_call(
        functools.partial(_combine_kernel, final=g_final is not None),
        grid=(b, nt),
        in_specs=in_specs,
        out_specs=tile,
        out_shape=jax.ShapeDtypeStruct((b, r, d), F32),
        scratch_shapes=[pltpu.VMEM((TOP_K, tm * SUBLANES, LANES), jnp.uint32), pltpu.SemaphoreType.DMA],
        compiler_params=_cparams(("arbitrary", "arbitrary")),
        name="moe_combine",
    )(*args)


def _moe(x, g_ffn, mods3, layer, row_fn, w_router, b_router, w_e_gate, w_e_up, w_e_down,
         ws_gate, ws_up, ws_down, g_final=None):
    b, r, d = x.shape
    h2, eid, rnk, gates, cnt = _router(x, g_ffn, mods3, layer, row_fn, w_router.T, b_router)
    pb, pe, pv, bounds = _moe_plan(cnt[:, 0].astype(jnp.int32), b * r * TOP_K)
    pos = _positions(eid, rnk, bounds[:N_EXPERTS])
    xs = _dispatch(h2, pos)
    y = _grouped(pb, pe, pv, bounds, xs, w_e_gate, w_e_up, w_e_down, layer)
    return _combine(pos, y, gates, h2, x, mods3, layer, row_fn,
                    ws_gate.astype(BF16), ws_up.astype(BF16), ws_down.astype(BF16), g_final)


def kernel(x, c, ctx, c_ctx, w_mod, b_mod, g_mix, g_ffn, g_final, ab_w_in, ab_w_out, ab_conv_w, ab_conv_b, ab_w_r, ab_b_r, ab_w_i, ab_b_i, ab_lam, ab_sink, cd_w_in, cd_w_out, cd_lam, cd_subln_g, cd_conv_w, cd_conv_b, cd_dt_bias, cd_a_log, cd_d_skip, cd_norm_g, w_router, b_router, w_e_gate, w_e_up, w_e_down, ws_gate, ws_up, ws_down):
    bsz, s_len, d = x.shape
    c_len = ctx.shape[1]
    depth = w_mod.shape[0]
    assert depth == 2 and bsz == SUBLANES, "kernels are specialised to depth 2 and batch 8"
    assert c_len % ROW_TILE == 0 and s_len % ROW_TILE == 0
    nct_row = c_len // ROW_TILE
    nct_time = c_len // TIME_TILE

    c_all = jnp.concatenate([c, c_ctx[None], jnp.zeros((MOD_ROWS - bsz - 1, d), F32)], axis=0)
    mods3 = _modulations(c_all, w_mod, b_mod).reshape(depth * MOD_ROWS, 1, N_MOD * d)
    rope_tabs = _rope_tables(c_len, s_len)
    xc = jnp.concatenate([ctx, x], axis=1)
    row_mixed = lambda bb, i: jnp.where(i < nct_row, SUBLANES, bb)
    row_latent = lambda bb, i: bb

    w_in = ab_w_in[0].astype(BF16)
    q_hi = LRU_WIDTH + WIN_HEADS * HEAD_DIM
    x_hi = q_hi + LRU_WIDTH
    k_hi = x_hi + WIN_KV_HEADS * HEAD_DIM
    gate, q, xa, k, v = _project(xc, g_mix[0], mods3, 0, nct_row, rope_tabs, [
        (w_in[:, :LRU_WIDTH], False, F32, False),
        (w_in[:, LRU_WIDTH:q_hi], True, BF16, False),
        (w_in[:, q_hi:x_hi], False, F32, True),
        (w_in[:, x_hi:k_hi], True, BF16, False),
        (w_in[:, k_hi:], False, BF16, False),
    ])
    l_len = c_len + s_len
    w_gates = jnp.stack([jnp.concatenate([_block_diag(ab_w_r[0, dd]), _block_diag(ab_w_i[0, dd])], axis=1)
                         for dd in range(2)]).astype(BF16)
    b_gates = jnp.concatenate([ab_b_r[0], ab_b_i[0]], axis=-1).reshape(2, 1, 2 * LRU_WIDTH)
    rec = _rglru(xa.reshape(l_len, bsz, LRU_WIDTH), ab_conv_w[0], ab_conv_b[0], w_gates, b_gates,
                 ab_lam[0].reshape(2, 1, LRU_WIDTH), nct_time)
    att = _win_attention(q, k, v, ab_sink[0], c_len)
    w_out = ab_w_out[0].astype(BF16)
    xc = _out_even(xc, rec.reshape(2, l_len, bsz * LRU_WIDTH), gate, att, w_out[:LRU_WIDTH], w_out[LRU_WIDTH:],
                   mods3, 0, nct_row)
    xc = _moe(xc, g_ffn[0], mods3, 0, row_mixed, w_router[0], b_router[0], w_e_gate, w_e_up, w_e_down,
              ws_gate[0], ws_up[0], ws_down[0])

    w_in = cd_w_in[0].astype(BF16)
    qk = DIFF_HEADS * 2 * DIFF_DH
    z_hi = qk + SSD_INNER
    k_hi = z_hi + qk
    v_hi = k_hi + qk
    x_hi = v_hi + SSD_CONV_DIM
    w_dt = jnp.pad(w_in[:, x_hi:], ((0, 0), (0, LANES - 2 * SSD_HEADS)))
    q, z, k, v, xbc, dt = _project(xc, g_mix[1], mods3, 1, nct_row, rope_tabs, [
        (w_in[:, :qk], True, BF16, False),
        (w_in[:, qk:z_hi], False, F32, False),
        (w_in[:, z_hi:k_hi], True, BF16, False),
        (w_in[:, k_hi:v_hi], False, BF16, False),
        (w_in[:, v_hi:x_hi], False, F32, False),
        (w_dt, False, F32, False),
    ])
    lam_init = 0.8 - 0.6 * math.exp(-0.3 * 1)
    diff = _diff_attention(q, k, v, cd_lam[0], cd_subln_g[0], lam_init, c_len)
    y2 = _ssd(xbc, dt, cd_conv_w[0], cd_conv_b[0], cd_dt_bias[0], cd_a_log[0], cd_d_skip[0], nct_time)
    w_out = cd_w_out[0].astype(BF16)
    xl = _out_odd(xc, diff, y2, z, cd_norm_g[0], w_out[:qk], w_out[qk:], mods3, 1, c_len)
    return _moe(xl, g_ffn[1], mods3, 1, row_latent, w_router[1], b_router[1], w_e_gate, w_e_up, w_e_down,
                ws_gate[1], ws_up[1], ws_down[1], g_final=g_final)
```

```python
import functools
import math

import jax
import jax.numpy as jnp
from jax import lax
from jax.experimental import pallas as pl
from jax.experimental.pallas import tpu as pltpu

F32 = jnp.float32
BF16 = jnp.bfloat16
HIGHEST = lax.Precision.HIGHEST

GRID_W = 64
N_MOD = 6
NORM_EPS = 1e-6
ROPE_BASE = 10000.0
CONV_W = 4

LRU_WIDTH = 512
LRU_BLOCKS = 8
LRU_C = 8.0

HEAD_DIM = 64
WIN_HEADS = 8
WIN_KV_HEADS = 2
WINDOW = 128

DIFF_HEADS = 4
DIFF_DH = 64

SSD_HEADS = 8
SSD_HEAD_DIM = 64
SSD_INNER = SSD_HEADS * SSD_HEAD_DIM
SSD_GROUPS = 2
SSD_STATE = 128
SSD_CONV_DIM = SSD_INNER + 2 * SSD_GROUPS * SSD_STATE

N_EXPERTS = 64
N_EXPERT_GROUPS = 8
TOPK_GROUPS = 4
TOP_K = 8
D_EXPERT = 256
ROUTED_SCALE = 2.5

LANES = 128
SUBLANES = 8
MOD_ROWS = 16
TIME_TILE = 128
ROW_TILE = 256
EXPERT_BLK = 512
ROW_CHUNKS = 4
VMEM_LIMIT = 48 * 1024 * 1024
NEG_BIG = -1e30


def _cparams(sem):
    return pltpu.CompilerParams(dimension_semantics=sem, vmem_limit_bytes=VMEM_LIMIT)


def _nt_dot(a, b):
    return lax.dot_general(a, b, (((1,), (1,)), ((), ())), preferred_element_type=F32)


def _softplus(x):
    return jnp.maximum(x, 0.0) + jnp.log1p(jnp.exp(-jnp.abs(x)))


def _silu(x):
    return x * jax.nn.sigmoid(x)


def _pack_bf16(x):
    half = x.shape[-1] // 2
    bits = pltpu.bitcast(x.astype(BF16).astype(F32), jnp.uint32)
    return bits[:, :half] | (bits[:, half:] >> 16)


def _unpack_bf16(w):
    hi = pltpu.bitcast(w & jnp.uint32(0xFFFF0000), F32)
    lo = pltpu.bitcast(w << 16, F32)
    return hi, lo


def _store_chunk_rows(ref, w):
    n = w.shape[0]
    for j in range(ROW_CHUNKS):
        ref[pl.ds(j, n, stride=ROW_CHUNKS), :] = w[:, j * LANES:(j + 1) * LANES]


def _load_chunk_rows(ref, n):
    return jnp.concatenate([ref[pl.ds(j, n, stride=ROW_CHUNKS), :] for j in range(ROW_CHUNKS)], axis=1)


def _packed_dot(w, weight_ref):
    half = w.shape[-1]
    hi, lo = _unpack_bf16(w)
    return (jnp.dot(hi.astype(BF16), weight_ref[:half, :], preferred_element_type=F32)
            + jnp.dot(lo.astype(BF16), weight_ref[half:, :], preferred_element_type=F32))


def _mod_kernel(c_ref, w_ref, b_ref, o_ref):
    c = c_ref[...]
    s = _silu(c)
    o_ref[...] = jnp.dot(s, w_ref[...], preferred_element_type=F32, precision=HIGHEST) + b_ref[...]


def _modulations(c_all, w_mod, b_mod):
    depth, d, _ = w_mod.shape
    return pl.pallas_call(
        _mod_kernel,
        grid=(depth, N_MOD),
        in_specs=[
            pl.BlockSpec((MOD_ROWS, d), lambda l, k: (0, 0)),
            pl.BlockSpec((None, d, d), lambda l, k: (l, 0, k)),
            pl.BlockSpec((None, 1, d), lambda l, k: (l, 0, k)),
        ],
        out_specs=pl.BlockSpec((None, MOD_ROWS, d), lambda l, k: (l, 0, k)),
        out_shape=jax.ShapeDtypeStruct((depth, MOD_ROWS, N_MOD * d), F32),
        compiler_params=_cparams(("arbitrary", "arbitrary")),
        name="adaln_modulation",
    )(c_all, w_mod, b_mod.reshape(depth, 1, N_MOD * d))


def _mod_spec(d, layer, chunk, row_fn):
    return pl.BlockSpec((None, 1, d), lambda b, i: (layer * MOD_ROWS + row_fn(b, i), 0, chunk))


def _norm_mod(x, g, sh, sc):
    ms = jnp.mean(x * x, axis=-1, keepdims=True)
    return (x * lax.rsqrt(ms + NORM_EPS) * g) * (1.0 + sc) + sh


def _rope(y, cos, sa, sb):
    n = y.shape[-1]
    half = HEAD_DIM // 2
    return y * cos + pltpu.roll(y, n - half, 1) * sa + pltpu.roll(y, half, 1) * sb


def _proj_kernel(*refs, ropes):
    n = len(ropes)
    x_ref, g_ref, sh_ref, sc_ref, cos_ref, sa_ref, sb_ref = refs[:7]
    w_refs = refs[7:7 + n]
    o_refs = refs[7 + n:]
    h = _norm_mod(x_ref[...], g_ref[...], sh_ref[...], sc_ref[...]).astype(BF16)
    for w_ref, o_ref, rope in zip(w_refs, o_refs, ropes):
        y = jnp.dot(h, w_ref[...], preferred_element_type=F32)
        if rope:
            w = y.shape[-1]
            y = _rope(y, cos_ref[:, :w], sa_ref[:, :w], sb_ref[:, :w])
        o_ref[...] = y.astype(o_ref.dtype)


def _project(xc, g, mods3, layer, nct, rope_tabs, groups):
    b, l, d = xc.shape
    tm = ROW_TILE
    row = lambda bb, i: jnp.where(i < nct, SUBLANES, bb)
    rw = rope_tabs[0].shape[-1]
    in_specs = [
        pl.BlockSpec((None, tm, d), lambda bb, i: (bb, i, 0)),
        pl.BlockSpec((1, d), lambda bb, i: (0, 0)),
        _mod_spec(d, layer, 0, row),
        _mod_spec(d, layer, 1, row),
    ] + [pl.BlockSpec((tm, rw), lambda bb, i: (i, 0))] * 3
    out_specs, out_shapes = [], []
    for w, _, dt, time_major in groups:
        n = w.shape[1]
        in_specs.append(pl.BlockSpec((d, n), lambda bb, i: (0, 0)))
        if time_major:
            out_specs.append(pl.BlockSpec((tm, n), lambda bb, i: (i, bb)))
            out_shapes.append(jax.ShapeDtypeStruct((l, b * n), dt))
        else:
            out_specs.append(pl.BlockSpec((None, tm, n), lambda bb, i: (bb, i, 0)))
            out_shapes.append(jax.ShapeDtypeStruct((b, l, n), dt))
    return pl.pallas_call(
        functools.partial(_proj_kernel, ropes=tuple(gp[1] for gp in groups)),
        grid=(b, l // tm),
        in_specs=in_specs,
        out_specs=out_specs,
        out_shape=out_shapes,
        compiler_params=_cparams(("arbitrary", "arbitrary")),
        name="norm_mod_project",
    )(xc, g.reshape(1, d), mods3, mods3, *rope_tabs, *[gp[0] for gp in groups])


def _rope_tables(c_len, s_len):
    rows = s_len // GRID_W
    row = jnp.repeat(jnp.arange(rows), GRID_W).astype(F32)
    col = jnp.tile(jnp.arange(GRID_W), rows).astype(F32)
    n = HEAD_DIM // 4
    inv = ROPE_BASE ** (-jnp.arange(n, dtype=F32) / n)
    ang = jnp.concatenate([row[:, None] * inv, col[:, None] * inv], axis=-1)
    cos, sin = jnp.cos(ang), jnp.sin(ang)
    zero = jnp.zeros_like(sin)
    reps = WIN_HEADS
    cos_t = jnp.tile(jnp.concatenate([cos, cos], axis=-1), (1, reps))
    sa_t = jnp.tile(jnp.concatenate([-sin, zero], axis=-1), (1, reps))
    sb_t = jnp.tile(jnp.concatenate([zero, sin], axis=-1), (1, reps))
    w = cos_t.shape[-1]
    pad1 = jnp.ones((c_len, w), F32)
    pad0 = jnp.zeros((c_len, w), F32)
    return (jnp.concatenate([pad1, cos_t], 0), jnp.concatenate([pad0, sa_t], 0),
            jnp.concatenate([pad0, sb_t], 0))


def _seq_tile(d, g, nct, nt):
    rev = jnp.where(g < nct, nct - 1 - g, nt - 1 - (g - nct))
    return jnp.where(d == 0, g, rev)


def _rglru_kernel(x_ref, xp_ref, xn_ref, cw_ref, cb_ref, w_ref, bias_ref, lam_ref, o_ref,
                  ext_scr, a_scr, b_scr, h_scr, *, ts, nct, nt, sub):
    d = pl.program_id(0)
    g = pl.program_id(1)
    tile = _seq_tile(d, g, nct, nt)
    bsz, width = h_scr.shape
    pv = jnp.where((tile == 0) | (tile == nct), 0.0, 1.0)
    nv = jnp.where((tile == nct - 1) | (tile == nt - 1), 0.0, 1.0)
    ext_scr[0:1] = xp_ref[...] * pv
    ext_scr[1:ts + 1] = x_ref[...]
    ext_scr[ts + 1:ts + 3] = xn_ref[...] * nv

    @pl.when(g == 0)
    def _():
        h_scr[...] = jnp.zeros_like(h_scr)

    neg_sp = -LRU_C * _softplus(-lam_ref[...])

    def prep(c, carry):
        r0 = pl.multiple_of(c * sub, sub)
        e = ext_scr[pl.ds(r0, sub + CONV_W - 1)]
        u = cb_ref[...] + cw_ref[0] * e[0:sub]
        for j in range(1, CONV_W):
            u = u + cw_ref[j] * e[j:j + sub]
        u2 = u.reshape(sub * bsz, width)
        gts = jnp.dot(u2.astype(BF16), w_ref[...], preferred_element_type=F32) + bias_ref[...]
        r = jax.nn.sigmoid(gts[:, :width])
        ig = jax.nn.sigmoid(gts[:, width:])
        log_a = neg_sp * r
        a = jnp.exp(log_a)
        mult = jnp.sqrt(1.0 - a * a)
        a_scr[pl.ds(r0, sub)] = a.reshape(sub, bsz, width)
        b_scr[pl.ds(r0, sub)] = (mult * ig * u2).reshape(sub, bsz, width)
        return carry

    lax.fori_loop(0, ts // sub, prep, 0)

    def step(t, h):
        tt = jnp.where(d == 0, t, ts - 1 - t)
        h = a_scr[tt] * h + b_scr[tt]
        o_ref[tt] = h
        return h

    h_scr[...] = lax.fori_loop(0, ts, step, h_scr[...], unroll=8)


def _rglru(xa_tm, conv_w, conv_b, w_gates, b_gates, lam, nct):
    l, bsz, width = xa_tm.shape
    ts = TIME_TILE
    nt = l // ts
    tile = lambda d, g: _seq_tile(d, g, nct, nt)
    kern = functools.partial(_rglru_kernel, ts=ts, nct=nct, nt=nt, sub=16)
    return pl.pallas_call(
        kern,
        grid=(2, nt),
        in_specs=[
            pl.BlockSpec((ts, bsz, width), lambda d, g: (tile(d, g), 0, 0)),
            pl.BlockSpec((1, bsz, width), lambda d, g: (jnp.maximum(tile(d, g) * ts - 1, 0), 0, 0)),
            pl.BlockSpec((2, bsz, width),
                         lambda d, g: (jnp.minimum((tile(d, g) + 1) * (ts // 2), l // 2 - 1), 0, 0)),
            pl.BlockSpec((CONV_W, 1, width), lambda d, g: (0, 0, 0)),
            pl.BlockSpec((1, width), lambda d, g: (0, 0)),
            pl.BlockSpec((None, width, 2 * width), lambda d, g: (d, 0, 0)),
            pl.BlockSpec((None, 1, 2 * width), lambda d, g: (d, 0, 0)),
            pl.BlockSpec((None, 1, width), lambda d, g: (d, 0, 0)),
        ],
        out_specs=pl.BlockSpec((None, ts, bsz, width), lambda d, g: (d, tile(d, g), 0, 0)),
        out_shape=jax.ShapeDtypeStruct((2, l, bsz, width), F32),
        scratch_shapes=[
            pltpu.VMEM((ts + CONV_W - 1, bsz, width), F32),
            pltpu.VMEM((ts, bsz, width), F32),
            pltpu.VMEM((ts, bsz, width), F32),
            pltpu.VMEM((bsz, width), F32),
        ],
        compiler_params=_cparams(("arbitrary", "arbitrary")),
        name="rglru_scan",
    )(xa_tm, xa_tm, xa_tm, conv_w.reshape(CONV_W, 1, width), conv_b.reshape(1, width),
      w_gates, b_gates, lam)


def _block_diag(w):
    nb, c, dd = w.shape
    eye = jnp.eye(nb, dtype=w.dtype)
    return (eye[:, None, :, None] * w[:, :, None, :]).reshape(nb * c, nb * dd)


def _win_attn_kernel(sink_ref, q_ref, k_ref, v_ref, o_ref, *, c_len, l_len, nqc):
    j = pl.program_id(1)
    blk = q_ref.shape[0]
    grp = WIN_HEADS // WIN_KV_HEADS
    band = 3 * blk

    def heads(body):
        for hk in range(WIN_KV_HEADS):
            ksl = slice(hk * HEAD_DIM, (hk + 1) * HEAD_DIM)
            for gq in range(grp):
                head = hk * grp + gq
                hsl = slice(head * HEAD_DIM, (head + 1) * HEAD_DIM)
                o_ref[:, hsl] = body(q_ref[:, hsl], ksl, sink_ref[head]).astype(o_ref.dtype)

    @pl.when(j < nqc)
    def _():
        def body(qh, ksl, sink):
            s = _nt_dot(qh, k_ref[0:c_len, ksl]) * (HEAD_DIM ** -0.5)
            m = jnp.maximum(jnp.max(s, axis=-1, keepdims=True), sink)
            p = jnp.exp(s - m)
            den = jnp.sum(p, axis=-1, keepdims=True) + jnp.exp(sink - m)
            o = jnp.dot(p.astype(BF16), v_ref[0:c_len, ksl], preferred_element_type=F32)
            return o / den
        heads(body)

    @pl.when(j >= nqc)
    def _():
        jb = j - nqc
        start = jnp.clip(c_len + (jb - 1) * blk, c_len - blk, l_len - band)
        start = pl.multiple_of(start, blk)
        qpos = jb * blk + lax.broadcasted_iota(jnp.int32, (blk, band), 0)
        kpos = start - c_len + lax.broadcasted_iota(jnp.int32, (blk, band), 1)
        valid = (jnp.abs(qpos - kpos) <= WINDOW) & (kpos >= 0)

        def body(qh, ksl, sink):
            sc = _nt_dot(qh, k_ref[0:c_len, ksl]) * (HEAD_DIM ** -0.5)
            sb = _nt_dot(qh, k_ref[pl.ds(start, band), ksl]) * (HEAD_DIM ** -0.5)
            sb = jnp.where(valid, sb, NEG_BIG)
            m = jnp.maximum(jnp.maximum(jnp.max(sc, axis=-1, keepdims=True),
                                        jnp.max(sb, axis=-1, keepdims=True)), sink)
            pc = jnp.exp(sc - m)
            pb = jnp.exp(sb - m)
            den = (jnp.sum(pc, axis=-1, keepdims=True) + jnp.sum(pb, axis=-1, keepdims=True)
                   + jnp.exp(sink - m))
            o = (jnp.dot(pc.astype(BF16), v_ref[0:c_len, ksl], preferred_element_type=F32)
                 + jnp.dot(pb.astype(BF16), v_ref[pl.ds(start, band), ksl], preferred_element_type=F32))
            return o / den
        heads(body)


def _win_attention(q, k, v, sink, c_len):
    b, l, qw = q.shape
    kw = k.shape[-1]
    blk = TIME_TILE
    kern = functools.partial(_win_attn_kernel, c_len=c_len, l_len=l, nqc=c_len // blk)
    return pl.pallas_call(
        kern,
        grid=(b, l // blk),
        in_specs=[
            pl.BlockSpec(memory_space=pltpu.SMEM),
            pl.BlockSpec((None, blk, qw), lambda bb, j: (bb, j, 0)),
            pl.BlockSpec((None, l, kw), lambda bb, j: (bb, 0, 0)),
            pl.BlockSpec((None, l, kw), lambda bb, j: (bb, 0, 0)),
        ],
        out_specs=pl.BlockSpec((None, blk, qw), lambda bb, j: (bb, j, 0)),
        out_shape=jax.ShapeDtypeStruct((b, l, qw), BF16),
        compiler_params=_cparams(("arbitrary", "arbitrary")),
        name="window_attention",
    )(sink, q, k, v)


def _out_even_kernel(x_ref, rec_ref, gate_ref, att_ref, wa_ref, wb_ref, g1_ref, o_ref):
    lru = (rec_ref[0] + rec_ref[1]) * jax.nn.gelu(gate_ref[...])
    y = (jnp.dot(lru.astype(BF16), wa_ref[...], preferred_element_type=F32)
         + jnp.dot(att_ref[...], wb_ref[...], preferred_element_type=F32))
    o_ref[...] = x_ref[...] + g1_ref[...] * y


def _out_even(xc, rec2, gate, att, w_a, w_b, mods3, layer, nct):
    b, l, d = xc.shape
    tm = ROW_TILE
    w = gate.shape[-1]
    row = lambda bb, i: jnp.where(i < nct, SUBLANES, bb)
    return pl.pallas_call(
        _out_even_kernel,
        grid=(b, l // tm),
        in_specs=[
            pl.BlockSpec((None, tm, d), lambda bb, i: (bb, i, 0)),
            pl.BlockSpec((2, tm, w), lambda bb, i: (0, i, bb)),
            pl.BlockSpec((None, tm, w), lambda bb, i: (bb, i, 0)),
            pl.BlockSpec((None, tm, att.shape[-1]), lambda bb, i: (bb, i, 0)),
            pl.BlockSpec(w_a.shape, lambda bb, i: (0, 0)),
            pl.BlockSpec(w_b.shape, lambda bb, i: (0, 0)),
            _mod_spec(d, layer, 2, row),
        ],
        out_specs=pl.BlockSpec((None, tm, d), lambda bb, i: (bb, i, 0)),
        out_shape=jax.ShapeDtypeStruct((b, l, d), F32),
        compiler_params=_cparams(("arbitrary", "arbitrary")),
        name="out_proj_even",
    )(xc, rec2, gate, att, w_a, w_b, mods3)


def _diff_attn_kernel(lam_ref, g_ref, q_ref, k_ref, v_ref, o_ref, *, lam_init):
    lv = lam_ref[...]
    lam = (jnp.exp(jnp.sum(lv[0:1] * lv[1:2], axis=-1, keepdims=True))
           - jnp.exp(jnp.sum(lv[2:3] * lv[3:4], axis=-1, keepdims=True)) + lam_init)
    vw = 2 * DIFF_DH

    def softmax(qm, km):
        s = _nt_dot(qm, km) * (DIFF_DH ** -0.5)
        m = jnp.max(s, axis=-1, keepdims=True)
        p = jnp.exp(s - m)
        return p / jnp.sum(p, axis=-1, keepdims=True)

    for h in range(DIFF_HEADS):
        lo = h * vw
        p0 = softmax(q_ref[:, lo:lo + DIFF_DH], k_ref[:, lo:lo + DIFF_DH])
        p1 = softmax(q_ref[:, lo + DIFF_DH:lo + vw], k_ref[:, lo + DIFF_DH:lo + vw])
        w = (p0 - lam * p1).astype(BF16)
        o = jnp.dot(w, v_ref[:, lo:lo + vw], preferred_element_type=F32)
        ms = jnp.mean(o * o, axis=-1, keepdims=True)
        o = o * lax.rsqrt(ms + NORM_EPS) * g_ref[...]
        o_ref[:, lo:lo + vw] = (o * (1.0 - lam_init)).astype(o_ref.dtype)


def _diff_attention(q, k, v, lam_vecs, subln_g, lam_init, c_len):
    b, l, w = q.shape
    tq = TIME_TILE
    s_len = l - c_len
    off = c_len // tq
    return pl.pallas_call(
        functools.partial(_diff_attn_kernel, lam_init=lam_init),
        grid=(b, s_len // tq),
        in_specs=[
            pl.BlockSpec(lam_vecs.shape, lambda bb, j: (0, 0)),
            pl.BlockSpec((1, 2 * DIFF_DH), lambda bb, j: (0, 0)),
            pl.BlockSpec((None, tq, w), lambda bb, j: (bb, j + off, 0)),
            pl.BlockSpec((None, l, w), lambda bb, j: (bb, 0, 0)),
            pl.BlockSpec((None, l, w), lambda bb, j: (bb, 0, 0)),
        ],
        out_specs=pl.BlockSpec((None, tq, w), lambda bb, j: (bb, j, 0)),
        out_shape=jax.ShapeDtypeStruct((b, s_len, w), BF16),
        compiler_params=_cparams(("arbitrary", "arbitrary")),
        name="diff_attention",
    )(lam_vecs, subln_g.reshape(1, -1), q, k, v)


def _ssd_kernel(x_ref, xp_ref, xn_ref, dt_ref, cw_ref, cb_ref, dtb_ref, alog_ref, dsk_ref, o_ref,
                ext_scr, st_scr, *, q, nct, nt):
    d = pl.program_id(0)
    g = pl.program_id(2)
    tile = _seq_tile(d, g, nct, nt)
    pv = jnp.where((tile == 0) | (tile == nct), 0.0, 1.0)
    nv = jnp.where((tile == nct - 1) | (tile == nt - 1), 0.0, 1.0)
    ext_scr[0:SUBLANES] = xp_ref[...] * pv
    ext_scr[SUBLANES:SUBLANES + q] = x_ref[...]
    ext_scr[SUBLANES + q:2 * SUBLANES + q] = xn_ref[...] * nv

    @pl.when(g == 0)
    def _():
        st_scr[...] = jnp.zeros_like(st_scr)

    u = cb_ref[...] + cw_ref[0] * ext_scr[SUBLANES - 1:SUBLANES - 1 + q, :]
    for j in range(1, CONV_W):
        u = u + cw_ref[j] * ext_scr[SUBLANES - 1 + j:SUBLANES - 1 + j + q, :]
    act = _silu(u)

    dtr = dt_ref[...]
    dtr = jnp.where(d == 0, dtr, pltpu.roll(dtr, LANES - SSD_HEADS, 1))
    dtv = _softplus(dtr + dtb_ref[...])
    head_lane = lax.broadcasted_iota(jnp.int32, (1, LANES), 1) < SSD_HEADS
    dta = dtv * jnp.where(head_lane, -jnp.exp(alog_ref[...]), 0.0)
    ri = lax.broadcasted_iota(jnp.int32, (q, q), 0)
    ci = lax.broadcasted_iota(jnp.int32, (q, q), 1)
    keep = jnp.where(d == 0, ri - ci, ci - ri) >= 0
    cum = jnp.dot(keep.astype(F32), dta, preferred_element_type=F32, precision=HIGHEST)
    tot = jnp.sum(dta, axis=0, keepdims=True)
    cum_t = cum.T
    dt_t = dtv.T
    to_end = jnp.exp(tot - cum) * dtv
    e_cum = jnp.exp(cum)
    e_tot = jnp.exp(tot)
    dskip = dsk_ref[...] * jnp.where(d == 0, 1.0, 0.0)

    hpg = SSD_HEADS // SSD_GROUPS
    for gi in range(SSD_GROUPS):
        b_g = act[:, SSD_INNER + gi * SSD_STATE:SSD_INNER + (gi + 1) * SSD_STATE]
        c_lo = SSD_INNER + SSD_GROUPS * SSD_STATE + gi * SSD_STATE
        c_g = act[:, c_lo:c_lo + SSD_STATE].astype(BF16)
        cb = _nt_dot(c_g, b_g.astype(BF16))
        b_gt = b_g.T.astype(BF16)
        for hh in range(hpg):
            h = gi * hpg + hh
            xs = act[:, h * SSD_HEAD_DIM:(h + 1) * SSD_HEAD_DIM]
            seg = cum[:, h:h + 1] - cum_t[h:h + 1, :]
            decay = jnp.exp(jnp.where(keep, seg, NEG_BIG))
            w = (cb * decay * dt_t[h:h + 1, :]).astype(BF16)
            state = st_scr[h]
            y = jnp.dot(w, xs.astype(BF16), preferred_element_type=F32)
            y = y + jnp.dot(c_g, state.astype(BF16), preferred_element_type=F32) * e_cum[:, h:h + 1]
            y = y + dskip[:, h * SSD_HEAD_DIM:(h + 1) * SSD_HEAD_DIM] * xs
            o_ref[:, h * SSD_HEAD_DIM:(h + 1) * SSD_HEAD_DIM] = y
            s_new = jnp.dot(b_gt, (xs * to_end[:, h:h + 1]).astype(BF16), preferred_element_type=F32)
            st_scr[h] = e_tot[:, h:h + 1] * state + s_new


def _ssd(xbc, dt, conv_w, conv_b, dt_bias, a_log, d_skip, nct):
    b, l, cd = xbc.shape
    q = TIME_TILE
    nt = l // q
    tile = lambda d, bb, g: _seq_tile(d, g, nct, nt)
    r8 = q // SUBLANES
    pad = LANES - SSD_HEADS
    dtb = jnp.pad(dt_bias, ((0, 0), (0, pad))).reshape(2, 1, LANES)
    alog = jnp.pad(a_log, ((0, 0), (0, pad))).reshape(2, 1, LANES)
    dsk = jnp.repeat(d_skip, SSD_HEAD_DIM).reshape(1, SSD_INNER)
    return pl.pallas_call(
        functools.partial(_ssd_kernel, q=q, nct=nct, nt=nt),
        grid=(2, b, nt),
        in_specs=[
            pl.BlockSpec((None, q, cd), lambda d, bb, g: (bb, tile(d, bb, g), 0)),
            pl.BlockSpec((None, SUBLANES, cd),
                         lambda d, bb, g: (bb, jnp.maximum(tile(d, bb, g) * r8 - 1, 0), 0)),
            pl.BlockSpec((None, SUBLANES, cd),
                         lambda d, bb, g: (bb, jnp.minimum((tile(d, bb, g) + 1) * r8, l // SUBLANES - 1), 0)),
            pl.BlockSpec((None, q, LANES), lambda d, bb, g: (bb, tile(d, bb, g), 0)),
            pl.BlockSpec((CONV_W, 1, cd), lambda d, bb, g: (0, 0, 0)),
            pl.BlockSpec((1, cd), lambda d, bb, g: (0, 0)),
            pl.BlockSpec((None, 1, LANES), lambda d, bb, g: (d, 0, 0)),
            pl.BlockSpec((None, 1, LANES), lambda d, bb, g: (d, 0, 0)),
            pl.BlockSpec((1, SSD_INNER), lambda d, bb, g: (0, 0)),
        ],
        out_specs=pl.BlockSpec((None, None, q, SSD_INNER), lambda d, bb, g: (d, bb, tile(d, bb, g), 0)),
        out_shape=jax.ShapeDtypeStruct((2, b, l, SSD_INNER), F32),
        scratch_shapes=[
            pltpu.VMEM((q + 2 * SUBLANES, cd), F32),
            pltpu.VMEM((SSD_HEADS, SSD_STATE, SSD_HEAD_DIM), F32),
        ],
        compiler_params=_cparams(("arbitrary", "arbitrary", "arbitrary")),
        name="ssd_chunked",
    )(xbc, xbc, xbc, dt, conv_w.reshape(CONV_W, 1, cd), conv_b.reshape(1, cd), dtb, alog, dsk)


def _out_odd_kernel(x_ref, diff_ref, y_ref, z_ref, ng_ref, wa_ref, wb_ref, g1_ref, o_ref):
    yz = (y_ref[0] + y_ref[1]) * _silu(z_ref[...])
    gs = SSD_INNER // SSD_GROUPS
    parts = []
    for gi in range(SSD_GROUPS):
        seg = yz[:, gi * gs:(gi + 1) * gs]
        ms = jnp.mean(seg * seg, axis=-1, keepdims=True)
        parts.append(seg * lax.rsqrt(ms + NORM_EPS) * ng_ref[:, gi * gs:(gi + 1) * gs])
    ssd = jnp.concatenate(parts, axis=-1).astype(BF16)
    y = (jnp.dot(diff_ref[...], wa_ref[...], preferred_element_type=F32)
         + jnp.dot(ssd, wb_ref[...], preferred_element_type=F32))
    o_ref[...] = x_ref[...] + g1_ref[...] * y


def _out_odd(xc, diff, y2, z, norm_g, w_a, w_b, mods3, layer, c_len):
    b, l, d = xc.shape
    s_len = l - c_len
    tm = ROW_TILE
    off = c_len // tm
    w = SSD_INNER
    row = lambda bb, i: bb
    return pl.pallas_call(
        _out_odd_kernel,
        grid=(b, s_len // tm),
        in_specs=[
            pl.BlockSpec((None, tm, d), lambda bb, i: (bb, i + off, 0)),
            pl.BlockSpec((None, tm, diff.shape[-1]), lambda bb, i: (bb, i, 0)),
            pl.BlockSpec((2, None, tm, w), lambda bb, i: (0, bb, i + off, 0)),
            pl.BlockSpec((None, tm, w), lambda bb, i: (bb, i + off, 0)),
            pl.BlockSpec((1, w), lambda bb, i: (0, 0)),
            pl.BlockSpec(w_a.shape, lambda bb, i: (0, 0)),
            pl.BlockSpec(w_b.shape, lambda bb, i: (0, 0)),
            _mod_spec(d, layer, 2, row),
        ],
        out_specs=pl.BlockSpec((None, tm, d), lambda bb, i: (bb, i, 0)),
        out_shape=jax.ShapeDtypeStruct((b, s_len, d), F32),
        compiler_params=_cparams(("arbitrary", "arbitrary")),
        name="out_proj_odd",
    )(xc, diff, y2, z, norm_g.reshape(1, w), w_a, w_b, mods3)


def _router_kernel(x_ref, g_ref, sh_ref, sc_ref, wr_ref, br_ref, h_ref, eid_ref, rnk_ref, gate_ref, cnt_ref,
                   carry_scr):
    @pl.when((pl.program_id(0) == 0) & (pl.program_id(1) == 0))
    def _():
        carry_scr[...] = jnp.zeros_like(carry_scr)

    h = _norm_mod(x_ref[...], g_ref[...], sh_ref[...], sc_ref[...])
    _store_chunk_rows(h_ref, _pack_bf16(h))
    tm = h.shape[0]
    per = N_EXPERTS // N_EXPERT_GROUPS
    logits = lax.dot_general(wr_ref[...], h, (((1,), (1,)), ((), ())),
                             preferred_element_type=F32, precision=HIGHEST)
    scores = jax.nn.sigmoid(logits)
    sel = scores + br_ref[...]
    sel3 = sel.reshape(N_EXPERT_GROUPS, per, tm)
    kio = lax.broadcasted_iota(jnp.int32, sel3.shape, 1)
    m1 = jnp.max(sel3, axis=1, keepdims=True)
    first = jnp.min(jnp.where(sel3 == m1, kio, per), axis=1, keepdims=True)
    m2 = jnp.max(jnp.where(kio == first, NEG_BIG, sel3), axis=1, keepdims=True)
    gs = m1 + m2
    gio = lax.broadcasted_iota(jnp.int32, gs.shape, 0)
    ahead = jnp.zeros(gs.shape, jnp.int32)
    for gp in range(N_EXPERT_GROUPS):
        other = gs[gp:gp + 1]
        ahead = ahead + jnp.where((other > gs) | ((other == gs) & (gp < gio)), 1, 0)
    grp_on = jnp.where(ahead < TOPK_GROUPS, 1.0, 0.0)
    selm = jnp.where(jnp.broadcast_to(grp_on, sel3.shape) > 0.5, sel3, NEG_BIG).reshape(N_EXPERTS, tm)
    eio = lax.broadcasted_iota(jnp.int32, selm.shape, 0)
    rank = jnp.zeros(selm.shape, jnp.int32)
    for e in range(N_EXPERTS):
        other = selm[e:e + 1, :]
        rank = rank + jnp.where((other > selm) | ((other == selm) & (e < eio)), 1, 0)
    chosen = rank < TOP_K
    gate = jnp.where(chosen, scores, 0.0)
    gate = gate / jnp.sum(gate, axis=0, keepdims=True) * ROUTED_SCALE
    cf = jnp.where(chosen, 1.0, 0.0)
    ti = lax.broadcasted_iota(jnp.int32, (tm, tm), 0)
    tj = lax.broadcasted_iota(jnp.int32, (tm, tm), 1)
    before = jnp.where(ti < tj, 1.0, 0.0).astype(BF16)
    in_expert = carry_scr[:, 0:1] + jnp.dot(cf.astype(BF16), before, preferred_element_type=F32)
    carry_scr[...] = carry_scr[...] + jnp.sum(cf, axis=1, keepdims=True)
    cnt_ref[...] = carry_scr[...]
    eio_f = eio.astype(F32)
    e_rows, r_rows, g_rows = [], [], []
    for k in range(TOP_K):
        hit = jnp.where(rank == k, 1.0, 0.0)
        e_rows.append(jnp.sum(hit * eio_f, axis=0, keepdims=True))
        r_rows.append(jnp.sum(hit * in_expert, axis=0, keepdims=True))
        g_rows.append(jnp.sum(hit * gate, axis=0, keepdims=True))
    eid_ref[...] = jnp.concatenate(e_rows, axis=0).astype(jnp.int32)
    rnk_ref[...] = jnp.concatenate(r_rows, axis=0).astype(jnp.int32)
    padded = jnp.concatenate(g_rows + [jnp.zeros((LANES - TOP_K, tm), F32)], axis=0)
    gate_ref[...] = padded.T


def _router(x, g, mods3, layer, row_fn, w_router_t, b_router):
    b, r, d = x.shape
    tm = ROW_TILE
    nt = r // tm
    slot = pl.BlockSpec((TOP_K, tm), lambda bb, i: (0, bb * nt + i))
    slot_shape = jax.ShapeDtypeStruct((TOP_K, b * r), jnp.int32)
    return pl.pallas_call(
        _router_kernel,
        grid=(b, nt),
        in_specs=[
            pl.BlockSpec((None, tm, d), lambda bb, i: (bb, i, 0)),
            pl.BlockSpec((1, d), lambda bb, i: (0, 0)),
            _mod_spec(d, layer, 3, row_fn),
            _mod_spec(d, layer, 4, row_fn),
            pl.BlockSpec(w_router_t.shape, lambda bb, i: (0, 0)),
            pl.BlockSpec((N_EXPERTS, 1), lambda bb, i: (0, 0)),
        ],
        out_specs=[
            pl.BlockSpec((tm * ROW_CHUNKS, LANES), lambda bb, i: (bb * nt + i, 0)),
            slot,
            slot,
            pl.BlockSpec((None, tm, LANES), lambda bb, i: (bb, i, 0)),
            pl.BlockSpec((N_EXPERTS, LANES), lambda bb, i: (0, 0)),
        ],
        out_shape=[jax.ShapeDtypeStruct((b * r * ROW_CHUNKS, LANES), jnp.uint32), slot_shape, slot_shape,
                   jax.ShapeDtypeStruct((b, r, LANES), F32), jax.ShapeDtypeStruct((N_EXPERTS, LANES), F32)],
        scratch_shapes=[pltpu.VMEM((N_EXPERTS, LANES), F32)],
        compiler_params=_cparams(("arbitrary", "arbitrary")),
        name="moe_router",
    )(x, g.reshape(1, d), mods3, mods3, w_router_t, b_router.reshape(N_EXPERTS, 1))


def _moe_plan(counts, n_rows):
    blk = EXPERT_BLK
    nb = n_rows // blk
    ends = jnp.cumsum(counts)
    starts = ends - counts
    count_le = lambda sorted_vals, q: jnp.sum(sorted_vals[None, :] <= q[:, None], axis=1, dtype=jnp.int32)
    first = jnp.arange(nb, dtype=jnp.int32) * blk
    e_lo = count_le(ends, first)
    e_hi = count_le(ends, first + (blk - 1))
    n_pair = e_hi - e_lo + 1
    p_end = jnp.cumsum(n_pair)
    p_start = p_end - n_pair
    i = jnp.arange(nb + N_EXPERTS - 1, dtype=jnp.int32)
    j = jnp.minimum(count_le(p_end, i), nb - 1)
    valid = i < p_end[-1]
    e = jnp.where(valid, e_lo[j] + i - p_start[j], e_hi[nb - 1]).astype(jnp.int32)
    bounds = jnp.concatenate([starts, ends[-1:]]).astype(jnp.int32)
    return j, e, valid.astype(jnp.int32), bounds


def _positions_kernel(starts_ref, eid_ref, rnk_ref, pos_ref):
    eid = eid_ref[...]
    pos = rnk_ref[...]
    for e in range(N_EXPERTS):
        pos = pos + jnp.where(eid == e, starts_ref[e], 0)
    pos_ref[...] = pos * ROW_CHUNKS


def _positions(eid, rnk, starts):
    full = pl.BlockSpec(eid.shape, lambda: (0, 0))
    return pl.pallas_call(
        _positions_kernel,
        in_specs=[pl.BlockSpec(memory_space=pltpu.SMEM), full, full],
        out_specs=full,
        out_shape=jax.ShapeDtypeStruct(eid.shape, jnp.int32),
        compiler_params=pltpu.CompilerParams(vmem_limit_bytes=VMEM_LIMIT),
        name="moe_positions",
    )(starts, eid, rnk)


def _token_row(ref, first):
    return ref.at[pl.ds(pl.multiple_of(first, ROW_CHUNKS), ROW_CHUNKS)]


def _dispatch_kernel(pos_ref, h_ref, xs_ref, sem):
    tm = h_ref.shape[0] // ROW_CHUNKS

    def issue(t, carry):
        src = _token_row(h_ref, t * ROW_CHUNKS)
        for k in range(TOP_K):
            pltpu.make_async_copy(src, _token_row(xs_ref, pos_ref[k, t]), sem).start(priority=k % 2)
        return carry

    lax.fori_loop(0, tm, issue, 0)
    done = pl.ds(0, tm * ROW_CHUNKS)
    for _ in range(TOP_K):
        pltpu.make_async_copy(h_ref.at[done], xs_ref.at[done], sem).wait()


def _dispatch(h2, pos):
    rows, w = h2.shape
    tm = ROW_TILE
    return pl.pallas_call(
        _dispatch_kernel,
        grid=(rows // (tm * ROW_CHUNKS),),
        in_specs=[
            pl.BlockSpec((TOP_K, tm), lambda i: (0, i), memory_space=pltpu.SMEM),
            pl.BlockSpec((tm * ROW_CHUNKS, w), lambda i: (i, 0)),
        ],
        out_specs=pl.BlockSpec(memory_space=pl.ANY),
        out_shape=jax.ShapeDtypeStruct((rows * TOP_K, w), h2.dtype),
        scratch_shapes=[pltpu.SemaphoreType.DMA],
        compiler_params=_cparams(("arbitrary",)),
        name="moe_dispatch",
    )(pos, h2)


def _grouped_kernel(pb_ref, pe_ref, pv_ref, bnd_ref, xs_ref, wg_ref, wu_ref, wd_ref, y_ref, wgb, wub, wdb):
    i = pl.program_id(0)
    prev = jnp.maximum(i - 1, 0)
    j = pb_ref[i]
    e = pe_ref[i]
    blk = xs_ref.shape[0] // ROW_CHUNKS

    @pl.when((i == 0) | (pb_ref[prev] != j))
    def _():
        y_ref[...] = jnp.zeros_like(y_ref)

    @pl.when((i == 0) | (pe_ref[prev] != e))
    def _():
        wgb[...] = wg_ref[...].astype(BF16)
        wub[...] = wu_ref[...].astype(BF16)
        wdb[...] = wd_ref[...].astype(BF16)

    @pl.when(pv_ref[i] == 1)
    def _():
        xw = _load_chunk_rows(xs_ref, blk)
        a = _packed_dot(xw, wgb)
        u = _packed_dot(xw, wub)
        yv = jnp.dot((_silu(a) * u).astype(BF16), wdb[...], preferred_element_type=F32)
        rows = j * blk + lax.broadcasted_iota(jnp.int32, (blk, 1), 0)
        own = (rows >= bnd_ref[e]) & (rows < bnd_ref[e + 1])
        yw = _pack_bf16(yv)
        for c in range(ROW_CHUNKS):
            sl = pl.ds(c, blk, stride=ROW_CHUNKS)
            y_ref[sl, :] = jnp.where(own, yw[:, c * LANES:(c + 1) * LANES], y_ref[sl, :])


def _grouped(pb, pe, pv, bounds, xs, wg, wu, wd, layer):
    p, half = xs.shape
    d = 2 * ROW_CHUNKS * LANES
    blk = EXPERT_BLK * ROW_CHUNKS
    grid_spec = pltpu.PrefetchScalarGridSpec(
        num_scalar_prefetch=4,
        grid=(pb.shape[0],),
        in_specs=[
            pl.BlockSpec((blk, half), lambda i, pb, pe, pv, bnd: (pb[i], 0)),
            pl.BlockSpec((None, None, d, D_EXPERT), lambda i, pb, pe, pv, bnd: (layer, pe[i], 0, 0)),
            pl.BlockSpec((None, None, d, D_EXPERT), lambda i, pb, pe, pv, bnd: (layer, pe[i], 0, 0)),
            pl.BlockSpec((None, None, D_EXPERT, d), lambda i, pb, pe, pv, bnd: (layer, pe[i], 0, 0)),
        ],
        out_specs=pl.BlockSpec((blk, half), lambda i, pb, pe, pv, bnd: (pb[i], 0)),
        scratch_shapes=[
            pltpu.VMEM((d, D_EXPERT), BF16),
            pltpu.VMEM((d, D_EXPERT), BF16),
            pltpu.VMEM((D_EXPERT, d), BF16),
        ],
    )
    return pl.pallas_call(
        _grouped_kernel,
        grid_spec=grid_spec,
        out_shape=jax.ShapeDtypeStruct((p, half), jnp.uint32),
        compiler_params=_cparams(("arbitrary",)),
        name="moe_grouped_experts",
    )(pb, pe, pv, bounds, xs, wg, wu, wd)


def _combine_kernel(*refs, final):
    pos_ref, y_ref, gate_ref, h_ref, x_ref, g2_ref, sg_ref, su_ref, sd_ref = refs[:9]
    o_ref, buf, sem = refs[-3:]
    tm = x_ref.shape[0]

    def issue(t, carry):
        for k in range(TOP_K):
            pltpu.make_async_copy(_token_row(y_ref, pos_ref[k, t]), _token_row(buf.at[k], t * ROW_CHUNKS),
                                  sem).start(priority=k % 2)
        return carry

    lax.fori_loop(0, tm, issue, 0)
    hw = _load_chunk_rows(h_ref, tm)
    a = _packed_dot(hw, sg_ref)
    u = _packed_dot(hw, su_ref)
    acc = jnp.dot((_silu(a) * u).astype(BF16), sd_ref[...], preferred_element_type=F32)
    done = pl.ds(0, tm * ROW_CHUNKS)
    for k in range(TOP_K):
        pltpu.make_async_copy(y_ref.at[done], buf.at[k, done], sem).wait()
    g = gate_ref[...]
    half = hw.shape[-1]
    acc_hi = acc[:, :half]
    acc_lo = acc[:, half:]
    for k in range(TOP_K):
        hi, lo = _unpack_bf16(_load_chunk_rows(buf.at[k], tm))
        acc_hi = acc_hi + g[:, k:k + 1] * hi
        acc_lo = acc_lo + g[:, k:k + 1] * lo
    x = x_ref[...] + g2_ref[...] * jnp.concatenate([acc_hi, acc_lo], axis=-1)
    if final:
        gf_ref = refs[9]
        ms = jnp.mean(x * x, axis=-1, keepdims=True)
        x = x * lax.rsqrt(ms + NORM_EPS) * gf_ref[...]
    o_ref[...] = x


def _combine(pos, y, gates, h2, x, mods3, layer, row_fn, sg, su, sd, g_final=None):
    b, r, d = x.shape
    tm = ROW_TILE
    nt = r // tm
    tile = pl.BlockSpec((None, tm, d), lambda bb, i: (bb, i, 0))
    in_specs = [
        pl.BlockSpec((TOP_K, tm), lambda bb, i: (0, bb * nt + i), memory_space=pltpu.SMEM),
        pl.BlockSpec(memory_space=pl.ANY),
        pl.BlockSpec((None, tm, LANES), lambda bb, i: (bb, i, 0)),
        pl.BlockSpec((tm * ROW_CHUNKS, LANES), lambda bb, i: (bb * nt + i, 0)),
        tile,
        _mod_spec(d, layer, 5, row_fn),
        pl.BlockSpec(sg.shape, lambda bb, i: (0, 0)),
        pl.BlockSpec(su.shape, lambda bb, i: (0, 0)),
        pl.BlockSpec(sd.shape, lambda bb, i: (0, 0)),
    ]
    args = [pos, y, gates, h2, x, mods3, sg, su, sd]
    if g_final is not None:
        in_specs.append(pl.BlockSpec((1, d), lambda bb, i: (0, 0)))
        args.append(g_final.reshape(1, d))
    return pl.pallas_call(
        functools.partial(_combine_kernel, final=g_final is not None),
        grid=(b, nt),
        in_specs=in_specs,
        out_specs=tile,
        out_shape=jax.ShapeDtypeStruct((b, r, d), F32),
        scratch_shapes=[pltpu.VMEM((TOP_K, tm * ROW_CHUNKS, LANES), jnp.uint32), pltpu.SemaphoreType.DMA],
        compiler_params=_cparams(("arbitrary", "arbitrary")),
        name="moe_combine",
    )(*args)


def _moe(x, g_ffn, mods3, layer, row_fn, w_router, b_router, w_e_gate, w_e_up, w_e_down,
         ws_gate, ws_up, ws_down, g_final=None):
    b, r, d = x.shape
    h2, eid, rnk, gates, cnt = _router(x, g_ffn, mods3, layer, row_fn, w_router.T, b_router)
    pb, pe, pv, bounds = _moe_plan(cnt[:, 0].astype(jnp.int32), b * r * TOP_K)
    pos = _positions(eid, rnk, bounds[:N_EXPERTS])
    xs = _dispatch(h2, pos)
    y = _grouped(pb, pe, pv, bounds, xs, w_e_gate, w_e_up, w_e_down, layer)
    return _combine(pos, y, gates, h2, x, mods3, layer, row_fn,
                    ws_gate.astype(BF16), ws_up.astype(BF16), ws_down.astype(BF16), g_final)


def kernel(x, c, ctx, c_ctx, w_mod, b_mod, g_mix, g_ffn, g_final, ab_w_in, ab_w_out, ab_conv_w, ab_conv_b, ab_w_r, ab_b_r, ab_w_i, ab_b_i, ab_lam, ab_sink, cd_w_in, cd_w_out, cd_lam, cd_subln_g, cd_conv_w, cd_conv_b, cd_dt_bias, cd_a_log, cd_d_skip, cd_norm_g, w_router, b_router, w_e_gate, w_e_up, w_e_down, ws_gate, ws_up, ws_down):
    bsz, s_len, d = x.shape
    c_len = ctx.shape[1]
    depth = w_mod.shape[0]
    assert depth == 2 and bsz == SUBLANES, "kernels are specialised to depth 2 and batch 8"
    assert c_len % ROW_TILE == 0 and s_len % ROW_TILE == 0
    nct_row = c_len // ROW_TILE
    nct_time = c_len // TIME_TILE

    c_all = jnp.concatenate([c, c_ctx[None], jnp.zeros((MOD_ROWS - bsz - 1, d), F32)], axis=0)
    mods3 = _modulations(c_all, w_mod, b_mod).reshape(depth * MOD_ROWS, 1, N_MOD * d)
    rope_tabs = _rope_tables(c_len, s_len)
    xc = jnp.concatenate([ctx, x], axis=1)
    row_mixed = lambda bb, i: jnp.where(i < nct_row, SUBLANES, bb)
    row_latent = lambda bb, i: bb

    w_in = ab_w_in[0].astype(BF16)
    q_hi = LRU_WIDTH + WIN_HEADS * HEAD_DIM
    x_hi = q_hi + LRU_WIDTH
    k_hi = x_hi + WIN_KV_HEADS * HEAD_DIM
    gate, q, xa, k, v = _project(xc, g_mix[0], mods3, 0, nct_row, rope_tabs, [
        (w_in[:, :LRU_WIDTH], False, F32, False),
        (w_in[:, LRU_WIDTH:q_hi], True, BF16, False),
        (w_in[:, q_hi:x_hi], False, F32, True),
        (w_in[:, x_hi:k_hi], True, BF16, False),
        (w_in[:, k_hi:], False, BF16, False),
    ])
    l_len = c_len + s_len
    w_gates = jnp.stack([jnp.concatenate([_block_diag(ab_w_r[0, dd]), _block_diag(ab_w_i[0, dd])], axis=1)
                         for dd in range(2)]).astype(BF16)
    b_gates = jnp.concatenate([ab_b_r[0], ab_b_i[0]], axis=-1).reshape(2, 1, 2 * LRU_WIDTH)
    rec = _rglru(xa.reshape(l_len, bsz, LRU_WIDTH), ab_conv_w[0], ab_conv_b[0], w_gates, b_gates,
                 ab_lam[0].reshape(2, 1, LRU_WIDTH), nct_time)
    att = _win_attention(q, k, v, ab_sink[0], c_len)
    w_out = ab_w_out[0].astype(BF16)
    xc = _out_even(xc, rec.reshape(2, l_len, bsz * LRU_WIDTH), gate, att, w_out[:LRU_WIDTH], w_out[LRU_WIDTH:],
                   mods3, 0, nct_row)
    xc = _moe(xc, g_ffn[0], mods3, 0, row_mixed, w_router[0], b_router[0], w_e_gate, w_e_up, w_e_down,
              ws_gate[0], ws_up[0], ws_down[0])

    w_in = cd_w_in[0].astype(BF16)
    qk = DIFF_HEADS * 2 * DIFF_DH
    z_hi = qk + SSD_INNER
    k_hi = z_hi + qk
    v_hi = k_hi + qk
    x_hi = v_hi + SSD_CONV_DIM
    w_dt = jnp.pad(w_in[:, x_hi:], ((0, 0), (0, LANES - 2 * SSD_HEADS)))
    q, z, k, v, xbc, dt = _project(xc, g_mix[1], mods3, 1, nct_row, rope_tabs, [
        (w_in[:, :qk], True, BF16, False),
        (w_in[:, qk:z_hi], False, F32, False),
        (w_in[:, z_hi:k_hi], True, BF16, False),
        (w_in[:, k_hi:v_hi], False, BF16, False),
        (w_in[:, v_hi:x_hi], False, F32, False),
        (w_dt, False, F32, False),
    ])
    lam_init = 0.8 - 0.6 * math.exp(-0.3 * 1)
    diff = _diff_attention(q, k, v, cd_lam[0], cd_subln_g[0], lam_init, c_len)
    y2 = _ssd(xbc, dt, cd_conv_w[0], cd_conv_b[0], cd_dt_bias[0], cd_a_log[0], cd_d_skip[0], nct_time)
    w_out = cd_w_out[0].astype(BF16)
    xl = _out_odd(xc, diff, y2, z, cd_norm_g[0], w_out[:qk], w_out[qk:], mods3, 1, c_len)
    return _moe(xl, g_ffn[1], mods3, 1, row_latent, w_router[1], b_router[1], w_e_gate, w_e_up, w_e_down,
                ws_gate[1], ws_up[1], ws_down[1], g_final=g_final)
```

```python
import functools
import math

import jax
import jax.numpy as jnp
from jax import lax
from jax.experimental import pallas as pl
from jax.experimental.pallas import tpu as pltpu

F32 = jnp.float32
BF16 = jnp.bfloat16
HIGHEST = lax.Precision.HIGHEST

GRID_W = 64
N_MOD = 6
NORM_EPS = 1e-6
ROPE_BASE = 10000.0
CONV_W = 4

LRU_WIDTH = 512
LRU_BLOCKS = 8
LRU_C = 8.0

HEAD_DIM = 64
WIN_HEADS = 8
WIN_KV_HEADS = 2
WINDOW = 128

DIFF_HEADS = 4
DIFF_DH = 64

SSD_HEADS = 8
SSD_HEAD_DIM = 64
SSD_INNER = SSD_HEADS * SSD_HEAD_DIM
SSD_GROUPS = 2
SSD_STATE = 128
SSD_CONV_DIM = SSD_INNER + 2 * SSD_GROUPS * SSD_STATE

N_EXPERTS = 64
N_EXPERT_GROUPS = 8
TOPK_GROUPS = 4
TOP_K = 8
D_EXPERT = 256
ROUTED_SCALE = 2.5

LANES = 128
SUBLANES = 8
MOD_ROWS = 16
TIME_TILE = 128
ROW_TILE = 256
EXPERT_BLK = 512
ROW_CHUNKS = 4
VMEM_LIMIT = 48 * 1024 * 1024
NEG_BIG = -1e30
LOG2E = math.log2(math.e)


def _cparams(sem):
    return pltpu.CompilerParams(dimension_semantics=sem, vmem_limit_bytes=VMEM_LIMIT)


def _nt_dot(a, b):
    return lax.dot_general(a, b, (((1,), (1,)), ((), ())), preferred_element_type=F32)


def _softplus(x):
    return jnp.maximum(x, 0.0) + jnp.log1p(jnp.exp(-jnp.abs(x)))


def _silu(x):
    return x * jax.nn.sigmoid(x)


def _pack_bf16(x):
    half = x.shape[-1] // 2
    bits = pltpu.bitcast(x.astype(BF16).astype(F32), jnp.uint32)
    return bits[:, :half] | (bits[:, half:] >> 16)


def _unpack_bf16(w):
    hi = pltpu.bitcast(w & jnp.uint32(0xFFFF0000), F32)
    lo = pltpu.bitcast(w << 16, F32)
    return hi, lo


def _store_chunk_rows(ref, w):
    n = w.shape[0]
    for j in range(ROW_CHUNKS):
        ref[pl.ds(j, n, stride=ROW_CHUNKS), :] = w[:, j * LANES:(j + 1) * LANES]


def _load_chunk_rows(ref, n):
    return jnp.concatenate([ref[pl.ds(j, n, stride=ROW_CHUNKS), :] for j in range(ROW_CHUNKS)], axis=1)


def _packed_dot(w, weight_ref):
    half = w.shape[-1]
    hi, lo = _unpack_bf16(w)
    return (jnp.dot(hi.astype(BF16), weight_ref[:half, :], preferred_element_type=F32)
            + jnp.dot(lo.astype(BF16), weight_ref[half:, :], preferred_element_type=F32))


def _mod_kernel(c_ref, w_ref, b_ref, o_ref):
    c = c_ref[...]
    s = _silu(c)
    o_ref[...] = jnp.dot(s, w_ref[...], preferred_element_type=F32, precision=HIGHEST) + b_ref[...]


def _modulations(c_all, w_mod, b_mod):
    depth, d, _ = w_mod.shape
    return pl.pallas_call(
        _mod_kernel,
        grid=(depth, N_MOD),
        in_specs=[
            pl.BlockSpec((MOD_ROWS, d), lambda l, k: (0, 0)),
            pl.BlockSpec((None, d, d), lambda l, k: (l, 0, k)),
            pl.BlockSpec((None, 1, d), lambda l, k: (l, 0, k)),
        ],
        out_specs=pl.BlockSpec((None, MOD_ROWS, d), lambda l, k: (l, 0, k)),
        out_shape=jax.ShapeDtypeStruct((depth, MOD_ROWS, N_MOD * d), F32),
        compiler_params=_cparams(("arbitrary", "arbitrary")),
        name="adaln_modulation",
    )(c_all, w_mod, b_mod.reshape(depth, 1, N_MOD * d))


def _mod_spec(d, layer, chunk, row_fn):
    return pl.BlockSpec((None, 1, d), lambda b, i: (layer * MOD_ROWS + row_fn(b, i), 0, chunk))


def _norm_mod(x, g, sh, sc):
    ms = jnp.mean(x * x, axis=-1, keepdims=True)
    return (x * lax.rsqrt(ms + NORM_EPS) * g) * (1.0 + sc) + sh


def _rope(y, cos, sa, sb):
    n = y.shape[-1]
    half = HEAD_DIM // 2
    return y * cos + pltpu.roll(y, n - half, 1) * sa + pltpu.roll(y, half, 1) * sb


def _proj_kernel(*refs, ropes):
    n = len(ropes)
    x_ref, g_ref, sh_ref, sc_ref, cos_ref, sa_ref, sb_ref = refs[:7]
    w_refs = refs[7:7 + n]
    o_refs = refs[7 + n:]
    h = _norm_mod(x_ref[...], g_ref[...], sh_ref[...], sc_ref[...]).astype(BF16)
    for w_ref, o_ref, rope in zip(w_refs, o_refs, ropes):
        y = jnp.dot(h, w_ref[...], preferred_element_type=F32)
        if rope is not None:
            w = y.shape[-1]
            y = _rope(y, cos_ref[:, :w], sa_ref[:, :w], sb_ref[:, :w])
            if rope != 1.0:
                y = y * rope
        o_ref[...] = y.astype(o_ref.dtype)


def _project(xc, g, mods3, layer, nct, rope_tabs, groups):
    b, l, d = xc.shape
    tm = ROW_TILE
    row = lambda bb, i: jnp.where(i < nct, SUBLANES, bb)
    rw = rope_tabs[0].shape[-1]
    in_specs = [
        pl.BlockSpec((None, tm, d), lambda bb, i: (bb, i, 0)),
        pl.BlockSpec((1, d), lambda bb, i: (0, 0)),
        _mod_spec(d, layer, 0, row),
        _mod_spec(d, layer, 1, row),
    ] + [pl.BlockSpec((tm, rw), lambda bb, i: (i, 0))] * 3
    out_specs, out_shapes = [], []
    for w, _, dt, time_major in groups:
        n = w.shape[1]
        in_specs.append(pl.BlockSpec((d, n), lambda bb, i: (0, 0)))
        if time_major:
            out_specs.append(pl.BlockSpec((tm, n), lambda bb, i: (i, bb)))
            out_shapes.append(jax.ShapeDtypeStruct((l, b * n), dt))
        else:
            out_specs.append(pl.BlockSpec((None, tm, n), lambda bb, i: (bb, i, 0)))
            out_shapes.append(jax.ShapeDtypeStruct((b, l, n), dt))
    return pl.pallas_call(
        functools.partial(_proj_kernel, ropes=tuple(gp[1] for gp in groups)),
        grid=(b, l // tm),
        in_specs=in_specs,
        out_specs=out_specs,
        out_shape=out_shapes,
        compiler_params=_cparams(("arbitrary", "arbitrary")),
        name="norm_mod_project",
    )(xc, g.reshape(1, d), mods3, mods3, *rope_tabs, *[gp[0] for gp in groups])


def _rope_tables(c_len, s_len):
    rows = s_len // GRID_W
    row = jnp.repeat(jnp.arange(rows), GRID_W).astype(F32)
    col = jnp.tile(jnp.arange(GRID_W), rows).astype(F32)
    n = HEAD_DIM // 4
    inv = ROPE_BASE ** (-jnp.arange(n, dtype=F32) / n)
    ang = jnp.concatenate([row[:, None] * inv, col[:, None] * inv], axis=-1)
    cos, sin = jnp.cos(ang), jnp.sin(ang)
    zero = jnp.zeros_like(sin)
    reps = WIN_HEADS
    cos_t = jnp.tile(jnp.concatenate([cos, cos], axis=-1), (1, reps))
    sa_t = jnp.tile(jnp.concatenate([-sin, zero], axis=-1), (1, reps))
    sb_t = jnp.tile(jnp.concatenate([zero, sin], axis=-1), (1, reps))
    w = cos_t.shape[-1]
    pad1 = jnp.ones((c_len, w), F32)
    pad0 = jnp.zeros((c_len, w), F32)
    return (jnp.concatenate([pad1, cos_t], 0), jnp.concatenate([pad0, sa_t], 0),
            jnp.concatenate([pad0, sb_t], 0))


def _seq_tile(d, g, nct, nt):
    rev = jnp.where(g < nct, nct - 1 - g, nt - 1 - (g - nct))
    return jnp.where(d == 0, g, rev)


def _rglru_kernel(x_ref, xp_ref, xn_ref, cw_ref, cb_ref, w_ref, bias_ref, lam_ref, o_ref,
                  ext_scr, a_scr, b_scr, h_scr, *, ts, nct, nt, sub):
    d = pl.program_id(0)
    g = pl.program_id(1)
    tile = _seq_tile(d, g, nct, nt)
    bsz, width = h_scr.shape
    pv = jnp.where((tile == 0) | (tile == nct), 0.0, 1.0)
    nv = jnp.where((tile == nct - 1) | (tile == nt - 1), 0.0, 1.0)
    ext_scr[0:1] = xp_ref[...] * pv
    ext_scr[1:ts + 1] = x_ref[...]
    ext_scr[ts + 1:ts + 3] = xn_ref[...] * nv

    @pl.when(g == 0)
    def _():
        h_scr[...] = jnp.zeros_like(h_scr)

    neg_sp = -LRU_C * _softplus(-lam_ref[...])

    def prep(c, carry):
        r0 = pl.multiple_of(c * sub, sub)
        e = ext_scr[pl.ds(r0, sub + CONV_W - 1)]
        u = cb_ref[...] + cw_ref[0] * e[0:sub]
        for j in range(1, CONV_W):
            u = u + cw_ref[j] * e[j:j + sub]
        u2 = u.reshape(sub * bsz, width)
        gts = jnp.dot(u2.astype(BF16), w_ref[...], preferred_element_type=F32) + bias_ref[...]
        r = jax.nn.sigmoid(gts[:, :width])
        ig = jax.nn.sigmoid(gts[:, width:])
        log_a = neg_sp * r
        a = jnp.exp(log_a)
        mult = jnp.sqrt(1.0 - a * a)
        a_scr[pl.ds(r0, sub)] = a.reshape(sub, bsz, width)
        b_scr[pl.ds(r0, sub)] = (mult * ig * u2).reshape(sub, bsz, width)
        return carry

    lax.fori_loop(0, ts // sub, prep, 0)

    def step(t, h):
        tt = jnp.where(d == 0, t, ts - 1 - t)
        h = a_scr[tt] * h + b_scr[tt]
        o_ref[tt] = h
        return h

    h_scr[...] = lax.fori_loop(0, ts, step, h_scr[...], unroll=8)


def _rglru(xa_tm, conv_w, conv_b, w_gates, b_gates, lam, nct):
    l, bsz, width = xa_tm.shape
    ts = TIME_TILE
    nt = l // ts
    tile = lambda d, g: _seq_tile(d, g, nct, nt)
    kern = functools.partial(_rglru_kernel, ts=ts, nct=nct, nt=nt, sub=16)
    return pl.pallas_call(
        kern,
        grid=(2, nt),
        in_specs=[
            pl.BlockSpec((ts, bsz, width), lambda d, g: (tile(d, g), 0, 0)),
            pl.BlockSpec((1, bsz, width), lambda d, g: (jnp.maximum(tile(d, g) * ts - 1, 0), 0, 0)),
            pl.BlockSpec((2, bsz, width),
                         lambda d, g: (jnp.minimum((tile(d, g) + 1) * (ts // 2), l // 2 - 1), 0, 0)),
            pl.BlockSpec((CONV_W, 1, width), lambda d, g: (0, 0, 0)),
            pl.BlockSpec((1, width), lambda d, g: (0, 0)),
            pl.BlockSpec((None, width, 2 * width), lambda d, g: (d, 0, 0)),
            pl.BlockSpec((None, 1, 2 * width), lambda d, g: (d, 0, 0)),
            pl.BlockSpec((None, 1, width), lambda d, g: (d, 0, 0)),
        ],
        out_specs=pl.BlockSpec((None, ts, bsz, width), lambda d, g: (d, tile(d, g), 0, 0)),
        out_shape=jax.ShapeDtypeStruct((2, l, bsz, width), F32),
        scratch_shapes=[
            pltpu.VMEM((ts + CONV_W - 1, bsz, width), F32),
            pltpu.VMEM((ts, bsz, width), F32),
            pltpu.VMEM((ts, bsz, width), F32),
            pltpu.VMEM((bsz, width), F32),
        ],
        compiler_params=_cparams(("arbitrary", "arbitrary")),
        name="rglru_scan",
    )(xa_tm, xa_tm, xa_tm, conv_w.reshape(CONV_W, 1, width), conv_b.reshape(1, width),
      w_gates, b_gates, lam)


def _block_diag(w):
    nb, c, dd = w.shape
    eye = jnp.eye(nb, dtype=w.dtype)
    return (eye[:, None, :, None] * w[:, :, None, :]).reshape(nb * c, nb * dd)


def _win_attn_kernel(sink_ref, q_ref, k_ref, v_ref, o_ref, *, c_len, l_len, nqc):
    j = pl.program_id(1)
    blk = q_ref.shape[0]
    grp = WIN_HEADS // WIN_KV_HEADS
    band = 3 * blk

    def heads(body):
        for hk in range(WIN_KV_HEADS):
            ksl = slice(hk * HEAD_DIM, (hk + 1) * HEAD_DIM)
            for gq in range(grp):
                head = hk * grp + gq
                hsl = slice(head * HEAD_DIM, (head + 1) * HEAD_DIM)
                o_ref[:, hsl] = body(q_ref[:, hsl], ksl, sink_ref[head] * LOG2E).astype(o_ref.dtype)

    @pl.when(j < nqc)
    def _():
        def body(qh, ksl, sink):
            s = _nt_dot(qh, k_ref[0:c_len, ksl])
            m = jnp.maximum(jnp.max(s, axis=-1, keepdims=True), sink)
            p = jnp.exp2(s - m)
            den = jnp.sum(p, axis=-1, keepdims=True) + jnp.exp2(sink - m)
            o = jnp.dot(p.astype(BF16), v_ref[0:c_len, ksl], preferred_element_type=F32)
            return o / den
        heads(body)

    @pl.when(j >= nqc)
    def _():
        jb = j - nqc
        start = jnp.clip(c_len + (jb - 1) * blk, c_len - blk, l_len - band)
        start = pl.multiple_of(start, blk)
        qpos = jb * blk + lax.broadcasted_iota(jnp.int32, (blk, band), 0)
        kpos = start - c_len + lax.broadcasted_iota(jnp.int32, (blk, band), 1)
        valid = (jnp.abs(qpos - kpos) <= WINDOW) & (kpos >= 0)

        def body(qh, ksl, sink):
            sc = _nt_dot(qh, k_ref[0:c_len, ksl])
            sb = jnp.where(valid, _nt_dot(qh, k_ref[pl.ds(start, band), ksl]), NEG_BIG)
            m = jnp.maximum(jnp.maximum(jnp.max(sc, axis=-1, keepdims=True),
                                        jnp.max(sb, axis=-1, keepdims=True)), sink)
            pc = jnp.exp2(sc - m)
            pb = jnp.exp2(sb - m)
            den = (jnp.sum(pc, axis=-1, keepdims=True) + jnp.sum(pb, axis=-1, keepdims=True)
                   + jnp.exp2(sink - m))
            o = (jnp.dot(pc.astype(BF16), v_ref[0:c_len, ksl], preferred_element_type=F32)
                 + jnp.dot(pb.astype(BF16), v_ref[pl.ds(start, band), ksl], preferred_element_type=F32))
            return o / den
        heads(body)


def _win_attention(q, k, v, sink, c_len):
    b, l, qw = q.shape
    kw = k.shape[-1]
    blk = TIME_TILE
    kern = functools.partial(_win_attn_kernel, c_len=c_len, l_len=l, nqc=c_len // blk)
    return pl.pallas_call(
        kern,
        grid=(b, l // blk),
        in_specs=[
            pl.BlockSpec(memory_space=pltpu.SMEM),
            pl.BlockSpec((None, blk, qw), lambda bb, j: (bb, j, 0)),
            pl.BlockSpec((None, l, kw), lambda bb, j: (bb, 0, 0)),
            pl.BlockSpec((None, l, kw), lambda bb, j: (bb, 0, 0)),
        ],
        out_specs=pl.BlockSpec((None, blk, qw), lambda bb, j: (bb, j, 0)),
        out_shape=jax.ShapeDtypeStruct((b, l, qw), BF16),
        compiler_params=_cparams(("arbitrary", "arbitrary")),
        name="window_attention",
    )(sink, q, k, v)


def _out_even_kernel(x_ref, rec_ref, gate_ref, att_ref, wa_ref, wb_ref, g1_ref, o_ref):
    lru = (rec_ref[0] + rec_ref[1]) * jax.nn.gelu(gate_ref[...])
    y = (jnp.dot(lru.astype(BF16), wa_ref[...], preferred_element_type=F32)
         + jnp.dot(att_ref[...], wb_ref[...], preferred_element_type=F32))
    o_ref[...] = x_ref[...] + g1_ref[...] * y


def _out_even(xc, rec2, gate, att, w_a, w_b, mods3, layer, nct):
    b, l, d = xc.shape
    tm = ROW_TILE
    w = gate.shape[-1]
    row = lambda bb, i: jnp.where(i < nct, SUBLANES, bb)
    return pl.pallas_call(
        _out_even_kernel,
        grid=(b, l // tm),
        in_specs=[
            pl.BlockSpec((None, tm, d), lambda bb, i: (bb, i, 0)),
            pl.BlockSpec((2, tm, w), lambda bb, i: (0, i, bb)),
            pl.BlockSpec((None, tm, w), lambda bb, i: (bb, i, 0)),
            pl.BlockSpec((None, tm, att.shape[-1]), lambda bb, i: (bb, i, 0)),
            pl.BlockSpec(w_a.shape, lambda bb, i: (0, 0)),
            pl.BlockSpec(w_b.shape, lambda bb, i: (0, 0)),
            _mod_spec(d, layer, 2, row),
        ],
        out_specs=pl.BlockSpec((None, tm, d), lambda bb, i: (bb, i, 0)),
        out_shape=jax.ShapeDtypeStruct((b, l, d), F32),
        compiler_params=_cparams(("arbitrary", "arbitrary")),
        name="out_proj_even",
    )(xc, rec2, gate, att, w_a, w_b, mods3)


def _diff_attn_kernel(lam_ref, g_ref, q_ref, k_ref, v_ref, o_ref, *, lam_init):
    lv = lam_ref[...]
    lam = (jnp.exp(jnp.sum(lv[0:1] * lv[1:2], axis=-1, keepdims=True))
           - jnp.exp(jnp.sum(lv[2:3] * lv[3:4], axis=-1, keepdims=True)) + lam_init)
    vw = 2 * DIFF_DH

    def softmax_parts(qm, km):
        s = _nt_dot(qm, km)
        e = jnp.exp2(s - jnp.max(s, axis=-1, keepdims=True))
        return e, 1.0 / jnp.sum(e, axis=-1, keepdims=True)

    for h in range(DIFF_HEADS):
        lo = h * vw
        e0, r0 = softmax_parts(q_ref[:, lo:lo + DIFF_DH], k_ref[:, lo:lo + DIFF_DH])
        e1, r1 = softmax_parts(q_ref[:, lo + DIFF_DH:lo + vw], k_ref[:, lo + DIFF_DH:lo + vw])
        w = (e0 * r0 - e1 * (lam * r1)).astype(BF16)
        o = jnp.dot(w, v_ref[:, lo:lo + vw], preferred_element_type=F32)
        ms = jnp.mean(o * o, axis=-1, keepdims=True)
        o = o * lax.rsqrt(ms + NORM_EPS) * g_ref[...]
        o_ref[:, lo:lo + vw] = (o * (1.0 - lam_init)).astype(o_ref.dtype)


def _diff_attention(q, k, v, lam_vecs, subln_g, lam_init, c_len):
    b, l, w = q.shape
    tq = ROW_TILE
    s_len = l - c_len
    off = c_len // tq
    return pl.pallas_call(
        functools.partial(_diff_attn_kernel, lam_init=lam_init),
        grid=(b, s_len // tq),
        in_specs=[
            pl.BlockSpec(lam_vecs.shape, lambda bb, j: (0, 0)),
            pl.BlockSpec((1, 2 * DIFF_DH), lambda bb, j: (0, 0)),
            pl.BlockSpec((None, tq, w), lambda bb, j: (bb, j + off, 0)),
            pl.BlockSpec((None, l, w), lambda bb, j: (bb, 0, 0)),
            pl.BlockSpec((None, l, w), lambda bb, j: (bb, 0, 0)),
        ],
        out_specs=pl.BlockSpec((None, tq, w), lambda bb, j: (bb, j, 0)),
        out_shape=jax.ShapeDtypeStruct((b, s_len, w), BF16),
        compiler_params=_cparams(("arbitrary", "arbitrary")),
        name="diff_attention",
    )(lam_vecs, subln_g.reshape(1, -1), q, k, v)


def _ssd_kernel(x_ref, xp_ref, xn_ref, dt_ref, cw_ref, cb_ref, dtb_ref, alog_ref, dsk_ref, o_ref,
                ext_scr, st_scr, *, q, nct, nt):
    d = pl.program_id(0)
    g = pl.program_id(2)
    tile = _seq_tile(d, g, nct, nt)
    pv = jnp.where((tile == 0) | (tile == nct), 0.0, 1.0)
    nv = jnp.where((tile == nct - 1) | (tile == nt - 1), 0.0, 1.0)
    ext_scr[0:SUBLANES] = xp_ref[...] * pv
    ext_scr[SUBLANES:SUBLANES + q] = x_ref[...]
    ext_scr[SUBLANES + q:2 * SUBLANES + q] = xn_ref[...] * nv

    @pl.when(g == 0)
    def _():
        st_scr[...] = jnp.zeros_like(st_scr)

    u = cb_ref[...] + cw_ref[0] * ext_scr[SUBLANES - 1:SUBLANES - 1 + q, :]
    for j in range(1, CONV_W):
        u = u + cw_ref[j] * ext_scr[SUBLANES - 1 + j:SUBLANES - 1 + j + q, :]
    act = _silu(u)

    dtr = dt_ref[...]
    dtr = jnp.where(d == 0, dtr, pltpu.roll(dtr, LANES - SSD_HEADS, 1))
    dtv = _softplus(dtr + dtb_ref[...])
    head_lane = lax.broadcasted_iota(jnp.int32, (1, LANES), 1) < SSD_HEADS
    dta = dtv * jnp.where(head_lane, -jnp.exp(alog_ref[...]), 0.0)
    ri = lax.broadcasted_iota(jnp.int32, (q, q), 0)
    ci = lax.broadcasted_iota(jnp.int32, (q, q), 1)
    keep = jnp.where(d == 0, ri - ci, ci - ri) >= 0
    cum = jnp.dot(keep.astype(F32), dta, preferred_element_type=F32, precision=HIGHEST)
    tot = jnp.sum(dta, axis=0, keepdims=True)
    cum_t = cum.T
    dt_t = dtv.T
    to_end = jnp.exp(tot - cum) * dtv
    e_cum = jnp.exp(cum)
    e_tot = jnp.exp(tot)
    dskip = dsk_ref[...] * jnp.where(d == 0, 1.0, 0.0)

    hpg = SSD_HEADS // SSD_GROUPS
    for gi in range(SSD_GROUPS):
        b_g = act[:, SSD_INNER + gi * SSD_STATE:SSD_INNER + (gi + 1) * SSD_STATE]
        c_lo = SSD_INNER + SSD_GROUPS * SSD_STATE + gi * SSD_STATE
        c_g = act[:, c_lo:c_lo + SSD_STATE].astype(BF16)
        cb = _nt_dot(c_g, b_g.astype(BF16))
        b_gt = b_g.T.astype(BF16)
        for hh in range(hpg):
            h = gi * hpg + hh
            xs = act[:, h * SSD_HEAD_DIM:(h + 1) * SSD_HEAD_DIM]
            seg = cum[:, h:h + 1] - cum_t[h:h + 1, :]
            decay = jnp.exp(jnp.where(keep, seg, NEG_BIG))
            w = (cb * decay * dt_t[h:h + 1, :]).astype(BF16)
            state = st_scr[h]
            y = jnp.dot(w, xs.astype(BF16), preferred_element_type=F32)
            y = y + jnp.dot(c_g, state.astype(BF16), preferred_element_type=F32) * e_cum[:, h:h + 1]
            y = y + dskip[:, h * SSD_HEAD_DIM:(h + 1) * SSD_HEAD_DIM] * xs
            o_ref[:, h * SSD_HEAD_DIM:(h + 1) * SSD_HEAD_DIM] = y
            s_new = jnp.dot(b_gt, (xs * to_end[:, h:h + 1]).astype(BF16), preferred_element_type=F32)
            st_scr[h] = e_tot[:, h:h + 1] * state + s_new


def _ssd(xbc, dt, conv_w, conv_b, dt_bias, a_log, d_skip, nct):
    b, l, cd = xbc.shape
    q = TIME_TILE
    nt = l // q
    tile = lambda d, bb, g: _seq_tile(d, g, nct, nt)
    r8 = q // SUBLANES
    pad = LANES - SSD_HEADS
    dtb = jnp.pad(dt_bias, ((0, 0), (0, pad))).reshape(2, 1, LANES)
    alog = jnp.pad(a_log, ((0, 0), (0, pad))).reshape(2, 1, LANES)
    dsk = jnp.repeat(d_skip, SSD_HEAD_DIM).reshape(1, SSD_INNER)
    return pl.pallas_call(
        functools.partial(_ssd_kernel, q=q, nct=nct, nt=nt),
        grid=(2, b, nt),
        in_specs=[
            pl.BlockSpec((None, q, cd), lambda d, bb, g: (bb, tile(d, bb, g), 0)),
            pl.BlockSpec((None, SUBLANES, cd),
                         lambda d, bb, g: (bb, jnp.maximum(tile(d, bb, g) * r8 - 1, 0), 0)),
            pl.BlockSpec((None, SUBLANES, cd),
                         lambda d, bb, g: (bb, jnp.minimum((tile(d, bb, g) + 1) * r8, l // SUBLANES - 1), 0)),
            pl.BlockSpec((None, q, LANES), lambda d, bb, g: (bb, tile(d, bb, g), 0)),
            pl.BlockSpec((CONV_W, 1, cd), lambda d, bb, g: (0, 0, 0)),
            pl.BlockSpec((1, cd), lambda d, bb, g: (0, 0)),
            pl.BlockSpec((None, 1, LANES), lambda d, bb, g: (d, 0, 0)),
            pl.BlockSpec((None, 1, LANES), lambda d, bb, g: (d, 0, 0)),
            pl.BlockSpec((1, SSD_INNER), lambda d, bb, g: (0, 0)),
        ],
        out_specs=pl.BlockSpec((None, None, q, SSD_INNER), lambda d, bb, g: (d, bb, tile(d, bb, g), 0)),
        out_shape=jax.ShapeDtypeStruct((2, b, l, SSD_INNER), F32),
        scratch_shapes=[
            pltpu.VMEM((q + 2 * SUBLANES, cd), F32),
            pltpu.VMEM((SSD_HEADS, SSD_STATE, SSD_HEAD_DIM), F32),
        ],
        compiler_params=_cparams(("arbitrary", "arbitrary", "arbitrary")),
        name="ssd_chunked",
    )(xbc, xbc, xbc, dt, conv_w.reshape(CONV_W, 1, cd), conv_b.reshape(1, cd), dtb, alog, dsk)


def _out_odd_kernel(x_ref, diff_ref, y_ref, z_ref, ng_ref, wa_ref, wb_ref, g1_ref, o_ref):
    yz = (y_ref[0] + y_ref[1]) * _silu(z_ref[...])
    gs = SSD_INNER // SSD_GROUPS
    parts = []
    for gi in range(SSD_GROUPS):
        seg = yz[:, gi * gs:(gi + 1) * gs]
        ms = jnp.mean(seg * seg, axis=-1, keepdims=True)
        parts.append(seg * lax.rsqrt(ms + NORM_EPS) * ng_ref[:, gi * gs:(gi + 1) * gs])
    ssd = jnp.concatenate(parts, axis=-1).astype(BF16)
    y = (jnp.dot(diff_ref[...], wa_ref[...], preferred_element_type=F32)
         + jnp.dot(ssd, wb_ref[...], preferred_element_type=F32))
    o_ref[...] = x_ref[...] + g1_ref[...] * y


def _out_odd(xc, diff, y2, z, norm_g, w_a, w_b, mods3, layer, c_len):
    b, l, d = xc.shape
    s_len = l - c_len
    tm = ROW_TILE
    off = c_len // tm
    w = SSD_INNER
    row = lambda bb, i: bb
    return pl.pallas_call(
        _out_odd_kernel,
        grid=(b, s_len // tm),
        in_specs=[
            pl.BlockSpec((None, tm, d), lambda bb, i: (bb, i + off, 0)),
            pl.BlockSpec((None, tm, diff.shape[-1]), lambda bb, i: (bb, i, 0)),
            pl.BlockSpec((2, None, tm, w), lambda bb, i: (0, bb, i + off, 0)),
            pl.BlockSpec((None, tm, w), lambda bb, i: (bb, i + off, 0)),
            pl.BlockSpec((1, w), lambda bb, i: (0, 0)),
            pl.BlockSpec(w_a.shape, lambda bb, i: (0, 0)),
            pl.BlockSpec(w_b.shape, lambda bb, i: (0, 0)),
            _mod_spec(d, layer, 2, row),
        ],
        out_specs=pl.BlockSpec((None, tm, d), lambda bb, i: (bb, i, 0)),
        out_shape=jax.ShapeDtypeStruct((b, s_len, d), F32),
        compiler_params=_cparams(("arbitrary", "arbitrary")),
        name="out_proj_odd",
    )(xc, diff, y2, z, norm_g.reshape(1, w), w_a, w_b, mods3)


def _router_kernel(x_ref, g_ref, sh_ref, sc_ref, wr_ref, br_ref, h_ref, eid_ref, rnk_ref, gate_ref, cnt_ref,
                   carry_scr):
    @pl.when((pl.program_id(0) == 0) & (pl.program_id(1) == 0))
    def _():
        carry_scr[...] = jnp.zeros_like(carry_scr)

    h = _norm_mod(x_ref[...], g_ref[...], sh_ref[...], sc_ref[...])
    _store_chunk_rows(h_ref, _pack_bf16(h))
    tm = h.shape[0]
    per = N_EXPERTS // N_EXPERT_GROUPS
    logits = lax.dot_general(wr_ref[...], h, (((1,), (1,)), ((), ())),
                             preferred_element_type=F32, precision=HIGHEST)
    scores = jax.nn.sigmoid(logits)
    sel = scores + br_ref[...]
    sel3 = sel.reshape(N_EXPERT_GROUPS, per, tm)
    kio = lax.broadcasted_iota(jnp.int32, sel3.shape, 1)
    m1 = jnp.max(sel3, axis=1, keepdims=True)
    first = jnp.min(jnp.where(sel3 == m1, kio, per), axis=1, keepdims=True)
    m2 = jnp.max(jnp.where(kio == first, NEG_BIG, sel3), axis=1, keepdims=True)
    gs = m1 + m2
    gio = lax.broadcasted_iota(jnp.int32, gs.shape, 0)
    ahead = jnp.zeros(gs.shape, jnp.int32)
    for gp in range(N_EXPERT_GROUPS):
        other = gs[gp:gp + 1]
        ahead = ahead + jnp.where((other > gs) | ((other == gs) & (gp < gio)), 1, 0)
    grp_on = jnp.where(ahead < TOPK_GROUPS, 1.0, 0.0)
    selm = jnp.where(jnp.broadcast_to(grp_on, sel3.shape) > 0.5, sel3, NEG_BIG).reshape(N_EXPERTS, tm)
    eio = lax.broadcasted_iota(jnp.int32, selm.shape, 0)
    work = selm
    cf = jnp.zeros(selm.shape, F32)
    e_rows, s_rows = [], []
    for k in range(TOP_K):
        best = jnp.max(work, axis=0, keepdims=True)
        idx = jnp.min(jnp.where(work == best, eio, N_EXPERTS), axis=0, keepdims=True)
        hit = eio == idx
        cf = cf + jnp.where(hit, 1.0, 0.0)
        work = jnp.where(hit, NEG_BIG, work)
        e_rows.append(idx)
        s_rows.append(jnp.sum(jnp.where(hit, scores, 0.0), axis=0, keepdims=True))
    denom = s_rows[0]
    for s_k in s_rows[1:]:
        denom = denom + s_k
    g_rows = [s_k / denom * ROUTED_SCALE for s_k in s_rows]
    ti = lax.broadcasted_iota(jnp.int32, (tm, tm), 0)
    tj = lax.broadcasted_iota(jnp.int32, (tm, tm), 1)
    before = jnp.where(ti < tj, 1.0, 0.0).astype(BF16)
    in_expert = carry_scr[:, 0:1] + jnp.dot(cf.astype(BF16), before, preferred_element_type=F32)
    carry_scr[...] = carry_scr[...] + jnp.sum(cf, axis=1, keepdims=True)
    cnt_ref[...] = carry_scr[...]
    r_rows = [jnp.sum(jnp.where(eio == idx, in_expert, 0.0), axis=0, keepdims=True) for idx in e_rows]
    eid_ref[...] = jnp.concatenate(e_rows, axis=0)
    rnk_ref[...] = jnp.concatenate(r_rows, axis=0).astype(jnp.int32)
    padded = jnp.concatenate(g_rows + [jnp.zeros((LANES - TOP_K, tm), F32)], axis=0)
    gate_ref[...] = padded.T


def _router(x, g, mods3, layer, row_fn, w_router_t, b_router):
    b, r, d = x.shape
    tm = ROW_TILE
    nt = r // tm
    slot = pl.BlockSpec((TOP_K, tm), lambda bb, i: (0, bb * nt + i))
    slot_shape = jax.ShapeDtypeStruct((TOP_K, b * r), jnp.int32)
    return pl.pallas_call(
        _router_kernel,
        grid=(b, nt),
        in_specs=[
            pl.BlockSpec((None, tm, d), lambda bb, i: (bb, i, 0)),
            pl.BlockSpec((1, d), lambda bb, i: (0, 0)),
            _mod_spec(d, layer, 3, row_fn),
            _mod_spec(d, layer, 4, row_fn),
            pl.BlockSpec(w_router_t.shape, lambda bb, i: (0, 0)),
            pl.BlockSpec((N_EXPERTS, 1), lambda bb, i: (0, 0)),
        ],
        out_specs=[
            pl.BlockSpec((tm * ROW_CHUNKS, LANES), lambda bb, i: (bb * nt + i, 0)),
            slot,
            slot,
            pl.BlockSpec((None, tm, LANES), lambda bb, i: (bb, i, 0)),
            pl.BlockSpec((N_EXPERTS, LANES), lambda bb, i: (0, 0)),
        ],
        out_shape=[jax.ShapeDtypeStruct((b * r * ROW_CHUNKS, LANES), jnp.uint32), slot_shape, slot_shape,
                   jax.ShapeDtypeStruct((b, r, LANES), F32), jax.ShapeDtypeStruct((N_EXPERTS, LANES), F32)],
        scratch_shapes=[pltpu.VMEM((N_EXPERTS, LANES), F32)],
        compiler_params=_cparams(("arbitrary", "arbitrary")),
        name="moe_router",
    )(x, g.reshape(1, d), mods3, mods3, w_router_t, b_router.reshape(N_EXPERTS, 1))


def _moe_plan(counts, n_rows):
    blk = EXPERT_BLK
    nb = n_rows // blk
    ends = jnp.cumsum(counts)
    starts = ends - counts
    count_le = lambda sorted_vals, q: jnp.sum(sorted_vals[None, :] <= q[:, None], axis=1, dtype=jnp.int32)
    first = jnp.arange(nb, dtype=jnp.int32) * blk
    e_lo = count_le(ends, first)
    e_hi = count_le(ends, first + (blk - 1))
    n_pair = e_hi - e_lo + 1
    p_end = jnp.cumsum(n_pair)
    p_start = p_end - n_pair
    i = jnp.arange(nb + N_EXPERTS - 1, dtype=jnp.int32)
    j = jnp.minimum(count_le(p_end, i), nb - 1)
    valid = i < p_end[-1]
    e = jnp.where(valid, e_lo[j] + i - p_start[j], e_hi[nb - 1]).astype(jnp.int32)
    bounds = jnp.concatenate([starts, ends[-1:]]).astype(jnp.int32)
    return j, e, valid.astype(jnp.int32), bounds


def _positions_kernel(starts_ref, eid_ref, rnk_ref, pos_ref):
    eid = eid_ref[...]
    pos = rnk_ref[...]
    for e in range(N_EXPERTS):
        pos = pos + jnp.where(eid == e, starts_ref[e], 0)
    pos_ref[...] = pos * ROW_CHUNKS


def _positions(eid, rnk, starts):
    full = pl.BlockSpec(eid.shape, lambda: (0, 0))
    return pl.pallas_call(
        _positions_kernel,
        in_specs=[pl.BlockSpec(memory_space=pltpu.SMEM), full, full],
        out_specs=full,
        out_shape=jax.ShapeDtypeStruct(eid.shape, jnp.int32),
        compiler_params=pltpu.CompilerParams(vmem_limit_bytes=VMEM_LIMIT),
        name="moe_positions",
    )(starts, eid, rnk)


def _token_row(ref, first):
    return ref.at[pl.ds(pl.multiple_of(first, ROW_CHUNKS), ROW_CHUNKS)]


def _dispatch_kernel(pos_ref, h_ref, xs_ref, sem):
    tm = h_ref.shape[0] // ROW_CHUNKS

    def issue(t, carry):
        src = _token_row(h_ref, t * ROW_CHUNKS)
        for k in range(TOP_K):
            pltpu.make_async_copy(src, _token_row(xs_ref, pos_ref[k, t]), sem).start(priority=k % 2)
        return carry

    lax.fori_loop(0, tm, issue, 0)
    done = pl.ds(0, tm * ROW_CHUNKS)
    for _ in range(TOP_K):
        pltpu.make_async_copy(h_ref.at[done], xs_ref.at[done], sem).wait()


def _dispatch(h2, pos):
    rows, w = h2.shape
    tm = ROW_TILE
    return pl.pallas_call(
        _dispatch_kernel,
        grid=(rows // (tm * ROW_CHUNKS),),
        in_specs=[
            pl.BlockSpec((TOP_K, tm), lambda i: (0, i), memory_space=pltpu.SMEM),
            pl.BlockSpec((tm * ROW_CHUNKS, w), lambda i: (i, 0)),
        ],
        out_specs=pl.BlockSpec(memory_space=pl.ANY),
        out_shape=jax.ShapeDtypeStruct((rows * TOP_K, w), h2.dtype),
        scratch_shapes=[pltpu.SemaphoreType.DMA],
        compiler_params=_cparams(("arbitrary",)),
        name="moe_dispatch",
    )(pos, h2)


def _grouped_kernel(pb_ref, pe_ref, pv_ref, bnd_ref, xs_ref, wg_ref, wu_ref, wd_ref, y_ref, wgb, wub, wdb):
    i = pl.program_id(0)
    prev = jnp.maximum(i - 1, 0)
    j = pb_ref[i]
    e = pe_ref[i]
    blk = xs_ref.shape[0] // ROW_CHUNKS

    @pl.when((i == 0) | (pb_ref[prev] != j))
    def _():
        y_ref[...] = jnp.zeros_like(y_ref)

    @pl.when((i == 0) | (pe_ref[prev] != e))
    def _():
        wgb[...] = wg_ref[...].astype(BF16)
        wub[...] = wu_ref[...].astype(BF16)
        wdb[...] = wd_ref[...].astype(BF16)

    @pl.when(pv_ref[i] == 1)
    def _():
        xw = _load_chunk_rows(xs_ref, blk)
        a = _packed_dot(xw, wgb)
        u = _packed_dot(xw, wub)
        yv = jnp.dot((_silu(a) * u).astype(BF16), wdb[...], preferred_element_type=F32)
        rows = j * blk + lax.broadcasted_iota(jnp.int32, (blk, 1), 0)
        own = (rows >= bnd_ref[e]) & (rows < bnd_ref[e + 1])
        yw = _pack_bf16(yv)
        for c in range(ROW_CHUNKS):
            sl = pl.ds(c, blk, stride=ROW_CHUNKS)
            y_ref[sl, :] = jnp.where(own, yw[:, c * LANES:(c + 1) * LANES], y_ref[sl, :])


def _grouped(pb, pe, pv, bounds, xs, wg, wu, wd, layer):
    p, half = xs.shape
    d = 2 * ROW_CHUNKS * LANES
    blk = EXPERT_BLK * ROW_CHUNKS
    grid_spec = pltpu.PrefetchScalarGridSpec(
        num_scalar_prefetch=4,
        grid=(pb.shape[0],),
        in_specs=[
            pl.BlockSpec((blk, half), lambda i, pb, pe, pv, bnd: (pb[i], 0)),
            pl.BlockSpec((None, None, d, D_EXPERT), lambda i, pb, pe, pv, bnd: (layer, pe[i], 0, 0)),
            pl.BlockSpec((None, None, d, D_EXPERT), lambda i, pb, pe, pv, bnd: (layer, pe[i], 0, 0)),
            pl.BlockSpec((None, None, D_EXPERT, d), lambda i, pb, pe, pv, bnd: (layer, pe[i], 0, 0)),
        ],
        out_specs=pl.BlockSpec((blk, half), lambda i, pb, pe, pv, bnd: (pb[i], 0)),
        scratch_shapes=[
            pltpu.VMEM((d, D_EXPERT), BF16),
            pltpu.VMEM((d, D_EXPERT), BF16),
            pltpu.VMEM((D_EXPERT, d), BF16),
        ],
    )
    return pl.pallas_call(
        _grouped_kernel,
        grid_spec=grid_spec,
        out_shape=jax.ShapeDtypeStruct((p, half), jnp.uint32),
        compiler_params=_cparams(("arbitrary",)),
        name="moe_grouped_experts",
    )(pb, pe, pv, bounds, xs, wg, wu, wd)


def _combine_kernel(*refs, final):
    pos_ref, y_ref, gate_ref, h_ref, x_ref, g2_ref, sg_ref, su_ref, sd_ref = refs[:9]
    o_ref, buf, sem = refs[-3:]
    tm = x_ref.shape[0]

    def issue(t, carry):
        for k in range(TOP_K):
            pltpu.make_async_copy(_token_row(y_ref, pos_ref[k, t]), _token_row(buf.at[k], t * ROW_CHUNKS),
                                  sem).start(priority=k % 2)
        return carry

    lax.fori_loop(0, tm, issue, 0)
    hw = _load_chunk_rows(h_ref, tm)
    a = _packed_dot(hw, sg_ref)
    u = _packed_dot(hw, su_ref)
    acc = jnp.dot((_silu(a) * u).astype(BF16), sd_ref[...], preferred_element_type=F32)
    done = pl.ds(0, tm * ROW_CHUNKS)
    for k in range(TOP_K):
        pltpu.make_async_copy(y_ref.at[done], buf.at[k, done], sem).wait()
    g = gate_ref[...]
    half = hw.shape[-1]
    acc_hi = acc[:, :half]
    acc_lo = acc[:, half:]
    for k in range(TOP_K):
        hi, lo = _unpack_bf16(_load_chunk_rows(buf.at[k], tm))
        acc_hi = acc_hi + g[:, k:k + 1] * hi
        acc_lo = acc_lo + g[:, k:k + 1] * lo
    x = x_ref[...] + g2_ref[...] * jnp.concatenate([acc_hi, acc_lo], axis=-1)
    if final:
        gf_ref = refs[9]
        ms = jnp.mean(x * x, axis=-1, keepdims=True)
        x = x * lax.rsqrt(ms + NORM_EPS) * gf_ref[...]
    o_ref[...] = x


def _combine(pos, y, gates, h2, x, mods3, layer, row_fn, sg, su, sd, g_final=None):
    b, r, d = x.shape
    tm = ROW_TILE
    nt = r // tm
    tile = pl.BlockSpec((None, tm, d), lambda bb, i: (bb, i, 0))
    in_specs = [
        pl.BlockSpec((TOP_K, tm), lambda bb, i: (0, bb * nt + i), memory_space=pltpu.SMEM),
        pl.BlockSpec(memory_space=pl.ANY),
        pl.BlockSpec((None, tm, LANES), lambda bb, i: (bb, i, 0)),
        pl.BlockSpec((tm * ROW_CHUNKS, LANES), lambda bb, i: (bb * nt + i, 0)),
        tile,
        _mod_spec(d, layer, 5, row_fn),
        pl.BlockSpec(sg.shape, lambda bb, i: (0, 0)),
        pl.BlockSpec(su.shape, lambda bb, i: (0, 0)),
        pl.BlockSpec(sd.shape, lambda bb, i: (0, 0)),
    ]
    args = [pos, y, gates, h2, x, mods3, sg, su, sd]
    if g_final is not None:
        in_specs.append(pl.BlockSpec((1, d), lambda bb, i: (0, 0)))
        args.append(g_final.reshape(1, d))
    return pl.pallas_call(
        functools.partial(_combine_kernel, final=g_final is not None),
        grid=(b, nt),
        in_specs=in_specs,
        out_specs=tile,
        out_shape=jax.ShapeDtypeStruct((b, r, d), F32),
        scratch_shapes=[pltpu.VMEM((TOP_K, tm * ROW_CHUNKS, LANES), jnp.uint32), pltpu.SemaphoreType.DMA],
        compiler_params=_cparams(("arbitrary", "arbitrary")),
        name="moe_combine",
    )(*args)


def _moe(x, g_ffn, mods3, layer, row_fn, w_router, b_router, w_e_gate, w_e_up, w_e_down,
         ws_gate, ws_up, ws_down, g_final=None):
    b, r, d = x.shape
    h2, eid, rnk, gates, cnt = _router(x, g_ffn, mods3, layer, row_fn, w_router.T, b_router)
    pb, pe, pv, bounds = _moe_plan(cnt[:, 0].astype(jnp.int32), b * r * TOP_K)
    pos = _positions(eid, rnk, bounds[:N_EXPERTS])
    xs = _dispatch(h2, pos)
    y = _grouped(pb, pe, pv, bounds, xs, w_e_gate, w_e_up, w_e_down, layer)
    return _combine(pos, y, gates, h2, x, mods3, layer, row_fn,
                    ws_gate.astype(BF16), ws_up.astype(BF16), ws_down.astype(BF16), g_final)


def kernel(x, c, ctx, c_ctx, w_mod, b_mod, g_mix, g_ffn, g_final, ab_w_in, ab_w_out, ab_conv_w, ab_conv_b, ab_w_r, ab_b_r, ab_w_i, ab_b_i, ab_lam, ab_sink, cd_w_in, cd_w_out, cd_lam, cd_subln_g, cd_conv_w, cd_conv_b, cd_dt_bias, cd_a_log, cd_d_skip, cd_norm_g, w_router, b_router, w_e_gate, w_e_up, w_e_down, ws_gate, ws_up, ws_down):
    bsz, s_len, d = x.shape
    c_len = ctx.shape[1]
    depth = w_mod.shape[0]
    assert depth == 2 and bsz == SUBLANES, "kernels are specialised to depth 2 and batch 8"
    assert c_len % ROW_TILE == 0 and s_len % ROW_TILE == 0
    nct_row = c_len // ROW_TILE
    nct_time = c_len // TIME_TILE

    c_all = jnp.concatenate([c, c_ctx[None], jnp.zeros((MOD_ROWS - bsz - 1, d), F32)], axis=0)
    mods3 = _modulations(c_all, w_mod, b_mod).reshape(depth * MOD_ROWS, 1, N_MOD * d)
    rope_tabs = _rope_tables(c_len, s_len)
    xc = jnp.concatenate([ctx, x], axis=1)
    row_mixed = lambda bb, i: jnp.where(i < nct_row, SUBLANES, bb)
    row_latent = lambda bb, i: bb

    w_in = ab_w_in[0].astype(BF16)
    q_hi = LRU_WIDTH + WIN_HEADS * HEAD_DIM
    x_hi = q_hi + LRU_WIDTH
    k_hi = x_hi + WIN_KV_HEADS * HEAD_DIM
    gate, q, xa, k, v = _project(xc, g_mix[0], mods3, 0, nct_row, rope_tabs, [
        (w_in[:, :LRU_WIDTH], None, F32, False),
        (w_in[:, LRU_WIDTH:q_hi], HEAD_DIM ** -0.5 * LOG2E, BF16, False),
        (w_in[:, q_hi:x_hi], None, F32, True),
        (w_in[:, x_hi:k_hi], 1.0, BF16, False),
        (w_in[:, k_hi:], None, BF16, False),
    ])
    l_len = c_len + s_len
    w_gates = jnp.stack([jnp.concatenate([_block_diag(ab_w_r[0, dd]), _block_diag(ab_w_i[0, dd])], axis=1)
                         for dd in range(2)]).astype(BF16)
    b_gates = jnp.concatenate([ab_b_r[0], ab_b_i[0]], axis=-1).reshape(2, 1, 2 * LRU_WIDTH)
    rec = _rglru(xa.reshape(l_len, bsz, LRU_WIDTH), ab_conv_w[0], ab_conv_b[0], w_gates, b_gates,
                 ab_lam[0].reshape(2, 1, LRU_WIDTH), nct_time)
    att = _win_attention(q, k, v, ab_sink[0], c_len)
    w_out = ab_w_out[0].astype(BF16)
    xc = _out_even(xc, rec.reshape(2, l_len, bsz * LRU_WIDTH), gate, att, w_out[:LRU_WIDTH], w_out[LRU_WIDTH:],
                   mods3, 0, nct_row)
    xc = _moe(xc, g_ffn[0], mods3, 0, row_mixed, w_router[0], b_router[0], w_e_gate, w_e_up, w_e_down,
              ws_gate[0], ws_up[0], ws_down[0])

    w_in = cd_w_in[0].astype(BF16)
    qk = DIFF_HEADS * 2 * DIFF_DH
    z_hi = qk + SSD_INNER
    k_hi = z_hi + qk
    v_hi = k_hi + qk
    x_hi = v_hi + SSD_CONV_DIM
    w_dt = jnp.pad(w_in[:, x_hi:], ((0, 0), (0, LANES - 2 * SSD_HEADS)))
    q, z, k, v, xbc, dt = _project(xc, g_mix[1], mods3, 1, nct_row, rope_tabs, [
        (w_in[:, :qk], DIFF_DH ** -0.5 * LOG2E, BF16, False),
        (w_in[:, qk:z_hi], None, F32, False),
        (w_in[:, z_hi:k_hi], 1.0, BF16, False),
        (w_in[:, k_hi:v_hi], None, BF16, False),
        (w_in[:, v_hi:x_hi], None, F32, False),
        (w_dt, None, F32, False),
    ])
    lam_init = 0.8 - 0.6 * math.exp(-0.3 * 1)
    diff = _diff_attention(q, k, v, cd_lam[0], cd_subln_g[0], lam_init, c_len)
    y2 = _ssd(xbc, dt, cd_conv_w[0], cd_conv_b[0], cd_dt_bias[0], cd_a_log[0], cd_d_skip[0], nct_time)
    w_out = cd_w_out[0].astype(BF16)
    xl = _out_odd(xc, diff, y2, z, cd_norm_g[0], w_out[:qk], w_out[qk:], mods3, 1, c_len)
    return _moe(xl, g_ffn[1], mods3, 1, row_latent, w_router[1], b_router[1], w_e_gate, w_e_up, w_e_down,
                ws_gate[1], ws_up[1], ws_down[1], g_final=g_final)
```

```python
import functools
import math

import jax
import jax.numpy as jnp
from jax import lax
from jax.experimental import pallas as pl
from jax.experimental.pallas import tpu as pltpu

F32 = jnp.float32
BF16 = jnp.bfloat16
HIGHEST = lax.Precision.HIGHEST

GRID_W = 64
N_MOD = 6
NORM_EPS = 1e-6
ROPE_BASE = 10000.0
CONV_W = 4

LRU_WIDTH = 512
LRU_BLOCKS = 8
LRU_C = 8.0

HEAD_DIM = 64
WIN_HEADS = 8
WIN_KV_HEADS = 2
WINDOW = 128

DIFF_HEADS = 4
DIFF_DH = 64

SSD_HEADS = 8
SSD_HEAD_DIM = 64
SSD_INNER = SSD_HEADS * SSD_HEAD_DIM
SSD_GROUPS = 2
SSD_STATE = 128
SSD_CONV_DIM = SSD_INNER + 2 * SSD_GROUPS * SSD_STATE

N_EXPERTS = 64
N_EXPERT_GROUPS = 8
TOPK_GROUPS = 4
TOP_K = 8
D_EXPERT = 256
ROUTED_SCALE = 2.5

LANES = 128
SUBLANES = 8
MOD_ROWS = 16
TIME_TILE = 128
ROW_TILE = 256
EXPERT_BLK = 512
ROW_CHUNKS = 4
VMEM_LIMIT = 48 * 1024 * 1024
NEG_BIG = -1e30
LOG2E = math.log2(math.e)


def _cparams(sem):
    return pltpu.CompilerParams(dimension_semantics=sem, vmem_limit_bytes=VMEM_LIMIT)


def _nt_dot(a, b):
    return lax.dot_general(a, b, (((1,), (1,)), ((), ())), preferred_element_type=F32)


def _softplus(x):
    return jnp.maximum(x, 0.0) + jnp.log1p(jnp.exp(-jnp.abs(x)))


def _silu(x):
    return x * jax.nn.sigmoid(x)


def _pack_bf16(x):
    half = x.shape[-1] // 2
    bits = pltpu.bitcast(x.astype(BF16).astype(F32), jnp.uint32)
    return bits[:, :half] | (bits[:, half:] >> 16)


def _unpack_bf16(w):
    hi = pltpu.bitcast(w & jnp.uint32(0xFFFF0000), F32)
    lo = pltpu.bitcast(w << 16, F32)
    return hi, lo


def _store_chunk_rows(ref, w):
    n = w.shape[0]
    for j in range(ROW_CHUNKS):
        ref[pl.ds(j, n, stride=ROW_CHUNKS), :] = w[:, j * LANES:(j + 1) * LANES]


def _load_chunk_rows(ref, n):
    return jnp.concatenate([ref[pl.ds(j, n, stride=ROW_CHUNKS), :] for j in range(ROW_CHUNKS)], axis=1)


def _packed_dot(w, weight_ref):
    half = w.shape[-1]
    hi, lo = _unpack_bf16(w)
    return (jnp.dot(hi.astype(BF16), weight_ref[:half, :], preferred_element_type=F32)
            + jnp.dot(lo.astype(BF16), weight_ref[half:, :], preferred_element_type=F32))


def _mod_kernel(c_ref, w_ref, b_ref, o_ref):
    c = c_ref[...]
    s = _silu(c)
    o_ref[...] = jnp.dot(s, w_ref[...], preferred_element_type=F32, precision=HIGHEST) + b_ref[...]


def _modulations(c_all, w_mod, b_mod):
    depth, d, _ = w_mod.shape
    return pl.pallas_call(
        _mod_kernel,
        grid=(depth, N_MOD),
        in_specs=[
            pl.BlockSpec((MOD_ROWS, d), lambda l, k: (0, 0)),
            pl.BlockSpec((None, d, d), lambda l, k: (l, 0, k)),
            pl.BlockSpec((None, 1, d), lambda l, k: (l, 0, k)),
        ],
        out_specs=pl.BlockSpec((None, MOD_ROWS, d), lambda l, k: (l, 0, k)),
        out_shape=jax.ShapeDtypeStruct((depth, MOD_ROWS, N_MOD * d), F32),
        compiler_params=_cparams(("arbitrary", "arbitrary")),
        name="adaln_modulation",
    )(c_all, w_mod, b_mod.reshape(depth, 1, N_MOD * d))


def _mod_spec(d, layer, chunk, row_fn):
    return pl.BlockSpec((None, 1, d), lambda b, i: (layer * MOD_ROWS + row_fn(b, i), 0, chunk))


def _norm_mod(x, g, sh, sc):
    ms = jnp.mean(x * x, axis=-1, keepdims=True)
    return (x * lax.rsqrt(ms + NORM_EPS) * g) * (1.0 + sc) + sh


def _rope(y, cos, sa, sb):
    n = y.shape[-1]
    half = HEAD_DIM // 2
    return y * cos + pltpu.roll(y, n - half, 1) * sa + pltpu.roll(y, half, 1) * sb


def _proj_kernel(*refs, ropes):
    n = len(ropes)
    x_ref, g_ref, sh_ref, sc_ref, cos_ref, sa_ref, sb_ref = refs[:7]
    w_refs = refs[7:7 + n]
    o_refs = refs[7 + n:]
    h = _norm_mod(x_ref[...], g_ref[...], sh_ref[...], sc_ref[...]).astype(BF16)
    for w_ref, o_ref, rope in zip(w_refs, o_refs, ropes):
        y = jnp.dot(h, w_ref[...], preferred_element_type=F32)
        if rope is not None:
            w = y.shape[-1]
            y = _rope(y, cos_ref[:, :w], sa_ref[:, :w], sb_ref[:, :w])
            if rope != 1.0:
                y = y * rope
        o_ref[...] = y.astype(o_ref.dtype)


def _project(xc, g, mods3, layer, nct, rope_tabs, groups):
    b, l, d = xc.shape
    tm = ROW_TILE
    row = lambda bb, i: jnp.where(i < nct, SUBLANES, bb)
    rw = rope_tabs[0].shape[-1]
    in_specs = [
        pl.BlockSpec((None, tm, d), lambda bb, i: (bb, i, 0)),
        pl.BlockSpec((1, d), lambda bb, i: (0, 0)),
        _mod_spec(d, layer, 0, row),
        _mod_spec(d, layer, 1, row),
    ] + [pl.BlockSpec((tm, rw), lambda bb, i: (i, 0))] * 3
    out_specs, out_shapes = [], []
    for w, _, dt, time_major in groups:
        n = w.shape[1]
        in_specs.append(pl.BlockSpec((d, n), lambda bb, i: (0, 0)))
        if time_major:
            out_specs.append(pl.BlockSpec((tm, n), lambda bb, i: (i, bb)))
            out_shapes.append(jax.ShapeDtypeStruct((l, b * n), dt))
        else:
            out_specs.append(pl.BlockSpec((None, tm, n), lambda bb, i: (bb, i, 0)))
            out_shapes.append(jax.ShapeDtypeStruct((b, l, n), dt))
    return pl.pallas_call(
        functools.partial(_proj_kernel, ropes=tuple(gp[1] for gp in groups)),
        grid=(b, l // tm),
        in_specs=in_specs,
        out_specs=out_specs,
        out_shape=out_shapes,
        compiler_params=_cparams(("arbitrary", "arbitrary")),
        name="norm_mod_project",
    )(xc, g.reshape(1, d), mods3, mods3, *rope_tabs, *[gp[0] for gp in groups])


def _rope_tables(c_len, s_len):
    rows = s_len // GRID_W
    row = jnp.repeat(jnp.arange(rows), GRID_W).astype(F32)
    col = jnp.tile(jnp.arange(GRID_W), rows).astype(F32)
    n = HEAD_DIM // 4
    inv = ROPE_BASE ** (-jnp.arange(n, dtype=F32) / n)
    ang = jnp.concatenate([row[:, None] * inv, col[:, None] * inv], axis=-1)
    cos, sin = jnp.cos(ang), jnp.sin(ang)
    zero = jnp.zeros_like(sin)
    reps = WIN_HEADS
    cos_t = jnp.tile(jnp.concatenate([cos, cos], axis=-1), (1, reps))
    sa_t = jnp.tile(jnp.concatenate([-sin, zero], axis=-1), (1, reps))
    sb_t = jnp.tile(jnp.concatenate([zero, sin], axis=-1), (1, reps))
    w = cos_t.shape[-1]
    pad1 = jnp.ones((c_len, w), F32)
    pad0 = jnp.zeros((c_len, w), F32)
    return (jnp.concatenate([pad1, cos_t], 0), jnp.concatenate([pad0, sa_t], 0),
            jnp.concatenate([pad0, sb_t], 0))


def _seq_tile(d, g, nct, nt):
    rev = jnp.where(g < nct, nct - 1 - g, nt - 1 - (g - nct))
    return jnp.where(d == 0, g, rev)


def _rglru_kernel(x_ref, xp_ref, xn_ref, cw_ref, cb_ref, w_ref, bias_ref, lam_ref, o_ref,
                  ext_scr, a_scr, b_scr, h_scr, *, ts, nct, nt, sub):
    d = pl.program_id(0)
    g = pl.program_id(1)
    tile = _seq_tile(d, g, nct, nt)
    bsz, width = h_scr.shape
    pv = jnp.where((tile == 0) | (tile == nct), 0.0, 1.0)
    nv = jnp.where((tile == nct - 1) | (tile == nt - 1), 0.0, 1.0)
    ext_scr[0:1] = xp_ref[...] * pv
    ext_scr[1:ts + 1] = x_ref[...]
    ext_scr[ts + 1:ts + 3] = xn_ref[...] * nv

    @pl.when(g == 0)
    def _():
        h_scr[...] = jnp.zeros_like(h_scr)

    neg_sp = -LRU_C * _softplus(-lam_ref[...])

    def prep(c, carry):
        r0 = pl.multiple_of(c * sub, sub)
        e = ext_scr[pl.ds(r0, sub + CONV_W - 1)]
        u = cb_ref[...] + cw_ref[0] * e[0:sub]
        for j in range(1, CONV_W):
            u = u + cw_ref[j] * e[j:j + sub]
        u2 = u.reshape(sub * bsz, width)
        gts = jnp.dot(u2.astype(BF16), w_ref[...], preferred_element_type=F32) + bias_ref[...]
        r = jax.nn.sigmoid(gts[:, :width])
        ig = jax.nn.sigmoid(gts[:, width:])
        log_a = neg_sp * r
        a = jnp.exp(log_a)
        mult = jnp.sqrt(1.0 - a * a)
        a_scr[pl.ds(r0, sub)] = a.reshape(sub, bsz, width)
        b_scr[pl.ds(r0, sub)] = (mult * ig * u2).reshape(sub, bsz, width)
        return carry

    lax.fori_loop(0, ts // sub, prep, 0)

    def step(t, h):
        tt = jnp.where(d == 0, t, ts - 1 - t)
        h = a_scr[tt] * h + b_scr[tt]
        o_ref[tt] = h
        return h

    h_scr[...] = lax.fori_loop(0, ts, step, h_scr[...], unroll=8)


def _rglru(xa_tm, conv_w, conv_b, w_gates, b_gates, lam, nct):
    l, bsz, width = xa_tm.shape
    ts = TIME_TILE
    nt = l // ts
    tile = lambda d, g: _seq_tile(d, g, nct, nt)
    kern = functools.partial(_rglru_kernel, ts=ts, nct=nct, nt=nt, sub=16)
    return pl.pallas_call(
        kern,
        grid=(2, nt),
        in_specs=[
            pl.BlockSpec((ts, bsz, width), lambda d, g: (tile(d, g), 0, 0)),
            pl.BlockSpec((1, bsz, width), lambda d, g: (jnp.maximum(tile(d, g) * ts - 1, 0), 0, 0)),
            pl.BlockSpec((2, bsz, width),
                         lambda d, g: (jnp.minimum((tile(d, g) + 1) * (ts // 2), l // 2 - 1), 0, 0)),
            pl.BlockSpec((CONV_W, 1, width), lambda d, g: (0, 0, 0)),
            pl.BlockSpec((1, width), lambda d, g: (0, 0)),
            pl.BlockSpec((None, width, 2 * width), lambda d, g: (d, 0, 0)),
            pl.BlockSpec((None, 1, 2 * width), lambda d, g: (d, 0, 0)),
            pl.BlockSpec((None, 1, width), lambda d, g: (d, 0, 0)),
        ],
        out_specs=pl.BlockSpec((None, ts, bsz, width), lambda d, g: (d, tile(d, g), 0, 0)),
        out_shape=jax.ShapeDtypeStruct((2, l, bsz, width), F32),
        scratch_shapes=[
            pltpu.VMEM((ts + CONV_W - 1, bsz, width), F32),
            pltpu.VMEM((ts, bsz, width), F32),
            pltpu.VMEM((ts, bsz, width), F32),
            pltpu.VMEM((bsz, width), F32),
        ],
        compiler_params=_cparams(("arbitrary", "arbitrary")),
        name="rglru_scan",
    )(xa_tm, xa_tm, xa_tm, conv_w.reshape(CONV_W, 1, width), conv_b.reshape(1, width),
      w_gates, b_gates, lam)


def _block_diag(w):
    nb, c, dd = w.shape
    eye = jnp.eye(nb, dtype=w.dtype)
    return (eye[:, None, :, None] * w[:, :, None, :]).reshape(nb * c, nb * dd)


def _win_attn_kernel(sink_ref, q_ref, k_ref, v_ref, o_ref, *, c_len, l_len, nqc):
    j = pl.program_id(1)
    blk = q_ref.shape[0]
    grp = WIN_HEADS // WIN_KV_HEADS
    band = 3 * blk

    def heads(body):
        for hk in range(WIN_KV_HEADS):
            ksl = slice(hk * HEAD_DIM, (hk + 1) * HEAD_DIM)
            for gq in range(grp):
                head = hk * grp + gq
                hsl = slice(head * HEAD_DIM, (head + 1) * HEAD_DIM)
                o_ref[:, hsl] = body(q_ref[:, hsl], ksl, sink_ref[head] * LOG2E).astype(o_ref.dtype)

    @pl.when(j < nqc)
    def _():
        def body(qh, ksl, sink):
            s = _nt_dot(qh, k_ref[0:c_len, ksl])
            m = jnp.maximum(jnp.max(s, axis=-1, keepdims=True), sink)
            p = jnp.exp2(s - m)
            den = jnp.sum(p, axis=-1, keepdims=True) + jnp.exp2(sink - m)
            o = jnp.dot(p.astype(BF16), v_ref[0:c_len, ksl], preferred_element_type=F32)
            return o / den
        heads(body)

    @pl.when(j >= nqc)
    def _():
        jb = j - nqc
        start = jnp.clip(c_len + (jb - 1) * blk, c_len - blk, l_len - band)
        start = pl.multiple_of(start, blk)
        qpos = jb * blk + lax.broadcasted_iota(jnp.int32, (blk, band), 0)
        kpos = start - c_len + lax.broadcasted_iota(jnp.int32, (blk, band), 1)
        valid = (jnp.abs(qpos - kpos) <= WINDOW) & (kpos >= 0)

        def body(qh, ksl, sink):
            sc = _nt_dot(qh, k_ref[0:c_len, ksl])
            sb = jnp.where(valid, _nt_dot(qh, k_ref[pl.ds(start, band), ksl]), NEG_BIG)
            m = jnp.maximum(jnp.maximum(jnp.max(sc, axis=-1, keepdims=True),
                                        jnp.max(sb, axis=-1, keepdims=True)), sink)
            pc = jnp.exp2(sc - m)
            pb = jnp.exp2(sb - m)
            den = (jnp.sum(pc, axis=-1, keepdims=True) + jnp.sum(pb, axis=-1, keepdims=True)
                   + jnp.exp2(sink - m))
            o = (jnp.dot(pc.astype(BF16), v_ref[0:c_len, ksl], preferred_element_type=F32)
                 + jnp.dot(pb.astype(BF16), v_ref[pl.ds(start, band), ksl], preferred_element_type=F32))
            return o / den
        heads(body)


def _win_attention(q, k, v, sink, c_len):
    b, l, qw = q.shape
    kw = k.shape[-1]
    blk = TIME_TILE
    kern = functools.partial(_win_attn_kernel, c_len=c_len, l_len=l, nqc=c_len // blk)
    return pl.pallas_call(
        kern,
        grid=(b, l // blk),
        in_specs=[
            pl.BlockSpec(memory_space=pltpu.SMEM),
            pl.BlockSpec((None, blk, qw), lambda bb, j: (bb, j, 0)),
            pl.BlockSpec((None, l, kw), lambda bb, j: (bb, 0, 0)),
            pl.BlockSpec((None, l, kw), lambda bb, j: (bb, 0, 0)),
        ],
        out_specs=pl.BlockSpec((None, blk, qw), lambda bb, j: (bb, j, 0)),
        out_shape=jax.ShapeDtypeStruct((b, l, qw), BF16),
        compiler_params=_cparams(("arbitrary", "arbitrary")),
        name="window_attention",
    )(sink, q, k, v)


def _out_even_kernel(x_ref, rec_ref, gate_ref, att_ref, wa_ref, wb_ref, g1_ref, o_ref):
    lru = (rec_ref[0] + rec_ref[1]) * jax.nn.gelu(gate_ref[...])
    y = (jnp.dot(lru.astype(BF16), wa_ref[...], preferred_element_type=F32)
         + jnp.dot(att_ref[...], wb_ref[...], preferred_element_type=F32))
    o_ref[...] = x_ref[...] + g1_ref[...] * y


def _out_even(xc, rec2, gate, att, w_a, w_b, mods3, layer, nct):
    b, l, d = xc.shape
    tm = ROW_TILE
    w = gate.shape[-1]
    row = lambda bb, i: jnp.where(i < nct, SUBLANES, bb)
    return pl.pallas_call(
        _out_even_kernel,
        grid=(b, l // tm),
        in_specs=[
            pl.BlockSpec((None, tm, d), lambda bb, i: (bb, i, 0)),
            pl.BlockSpec((2, tm, w), lambda bb, i: (0, i, bb)),
            pl.BlockSpec((None, tm, w), lambda bb, i: (bb, i, 0)),
            pl.BlockSpec((None, tm, att.shape[-1]), lambda bb, i: (bb, i, 0)),
            pl.BlockSpec(w_a.shape, lambda bb, i: (0, 0)),
            pl.BlockSpec(w_b.shape, lambda bb, i: (0, 0)),
            _mod_spec(d, layer, 2, row),
        ],
        out_specs=pl.BlockSpec((None, tm, d), lambda bb, i: (bb, i, 0)),
        out_shape=jax.ShapeDtypeStruct((b, l, d), F32),
        compiler_params=_cparams(("arbitrary", "arbitrary")),
        name="out_proj_even",
    )(xc, rec2, gate, att, w_a, w_b, mods3)


def _diff_attn_kernel(lam_ref, g_ref, q_ref, k_ref, v_ref, o_ref, *, lam_init):
    lv = lam_ref[...]
    lam = (jnp.exp(jnp.sum(lv[0:1] * lv[1:2], axis=-1, keepdims=True))
           - jnp.exp(jnp.sum(lv[2:3] * lv[3:4], axis=-1, keepdims=True)) + lam_init)
    vw = 2 * DIFF_DH

    def softmax_parts(qm, km):
        s = _nt_dot(qm, km)
        e = jnp.exp2(s - jnp.max(s, axis=-1, keepdims=True))
        return e, 1.0 / jnp.sum(e, axis=-1, keepdims=True)

    for h in range(DIFF_HEADS):
        lo = h * vw
        e0, r0 = softmax_parts(q_ref[:, lo:lo + DIFF_DH], k_ref[:, lo:lo + DIFF_DH])
        e1, r1 = softmax_parts(q_ref[:, lo + DIFF_DH:lo + vw], k_ref[:, lo + DIFF_DH:lo + vw])
        w = (e0 * r0 - e1 * (lam * r1)).astype(BF16)
        o = jnp.dot(w, v_ref[:, lo:lo + vw], preferred_element_type=F32)
        ms = jnp.mean(o * o, axis=-1, keepdims=True)
        o = o * lax.rsqrt(ms + NORM_EPS) * g_ref[...]
        o_ref[:, lo:lo + vw] = (o * (1.0 - lam_init)).astype(o_ref.dtype)


def _diff_attention(q, k, v, lam_vecs, subln_g, lam_init, c_len):
    b, l, w = q.shape
    tq = ROW_TILE
    s_len = l - c_len
    off = c_len // tq
    return pl.pallas_call(
        functools.partial(_diff_attn_kernel, lam_init=lam_init),
        grid=(b, s_len // tq),
        in_specs=[
            pl.BlockSpec(lam_vecs.shape, lambda bb, j: (0, 0)),
            pl.BlockSpec((1, 2 * DIFF_DH), lambda bb, j: (0, 0)),
            pl.BlockSpec((None, tq, w), lambda bb, j: (bb, j + off, 0)),
            pl.BlockSpec((None, l, w), lambda bb, j: (bb, 0, 0)),
            pl.BlockSpec((None, l, w), lambda bb, j: (bb, 0, 0)),
        ],
        out_specs=pl.BlockSpec((None, tq, w), lambda bb, j: (bb, j, 0)),
        out_shape=jax.ShapeDtypeStruct((b, s_len, w), BF16),
        compiler_params=_cparams(("arbitrary", "arbitrary")),
        name="diff_attention",
    )(lam_vecs, subln_g.reshape(1, -1), q, k, v)


def _ssd_kernel(x_ref, xp_ref, xn_ref, dt_ref, cw_ref, cb_ref, dtb_ref, alog_ref, dsk_ref, o_ref,
                ext_scr, st_scr, *, q, nct, nt):
    d = pl.program_id(0)
    g = pl.program_id(2)
    tile = _seq_tile(d, g, nct, nt)
    pv = jnp.where((tile == 0) | (tile == nct), 0.0, 1.0)
    nv = jnp.where((tile == nct - 1) | (tile == nt - 1), 0.0, 1.0)
    ext_scr[0:SUBLANES] = xp_ref[...] * pv
    ext_scr[SUBLANES:SUBLANES + q] = x_ref[...]
    ext_scr[SUBLANES + q:2 * SUBLANES + q] = xn_ref[...] * nv

    @pl.when(g == 0)
    def _():
        st_scr[...] = jnp.zeros_like(st_scr)

    u = cb_ref[...] + cw_ref[0] * ext_scr[SUBLANES - 1:SUBLANES - 1 + q, :]
    for j in range(1, CONV_W):
        u = u + cw_ref[j] * ext_scr[SUBLANES - 1 + j:SUBLANES - 1 + j + q, :]
    act = _silu(u)

    dtr = dt_ref[...]
    dtr = jnp.where(d == 0, dtr, pltpu.roll(dtr, LANES - SSD_HEADS, 1))
    dtv = _softplus(dtr + dtb_ref[...])
    head_lane = lax.broadcasted_iota(jnp.int32, (1, LANES), 1) < SSD_HEADS
    dta = dtv * jnp.where(head_lane, -jnp.exp(alog_ref[...]), 0.0)
    ri = lax.broadcasted_iota(jnp.int32, (q, q), 0)
    ci = lax.broadcasted_iota(jnp.int32, (q, q), 1)
    keep = jnp.where(d == 0, ri - ci, ci - ri) >= 0
    cum = jnp.dot(keep.astype(F32), dta, preferred_element_type=F32, precision=HIGHEST)
    tot = jnp.sum(dta, axis=0, keepdims=True)
    cum_t = cum.T
    dt_t = dtv.T
    to_end = jnp.exp(tot - cum) * dtv
    e_cum = jnp.exp(cum)
    e_tot = jnp.exp(tot)
    dskip = dsk_ref[...] * jnp.where(d == 0, 1.0, 0.0)

    hpg = SSD_HEADS // SSD_GROUPS
    for gi in range(SSD_GROUPS):
        b_g = act[:, SSD_INNER + gi * SSD_STATE:SSD_INNER + (gi + 1) * SSD_STATE]
        c_lo = SSD_INNER + SSD_GROUPS * SSD_STATE + gi * SSD_STATE
        c_g = act[:, c_lo:c_lo + SSD_STATE].astype(BF16)
        cb = _nt_dot(c_g, b_g.astype(BF16))
        b_gt = b_g.T.astype(BF16)
        for hh in range(hpg):
            h = gi * hpg + hh
            xs = act[:, h * SSD_HEAD_DIM:(h + 1) * SSD_HEAD_DIM]
            seg = cum[:, h:h + 1] - cum_t[h:h + 1, :]
            decay = jnp.exp(jnp.where(keep, seg, NEG_BIG))
            w = (cb * decay * dt_t[h:h + 1, :]).astype(BF16)
            state = st_scr[h]
            y = jnp.dot(w, xs.astype(BF16), preferred_element_type=F32)
            y = y + jnp.dot(c_g, state.astype(BF16), preferred_element_type=F32) * e_cum[:, h:h + 1]
            y = y + dskip[:, h * SSD_HEAD_DIM:(h + 1) * SSD_HEAD_DIM] * xs
            o_ref[:, h * SSD_HEAD_DIM:(h + 1) * SSD_HEAD_DIM] = y
            s_new = jnp.dot(b_gt, (xs * to_end[:, h:h + 1]).astype(BF16), preferred_element_type=F32)
            st_scr[h] = e_tot[:, h:h + 1] * state + s_new


def _ssd(xbc, dt, conv_w, conv_b, dt_bias, a_log, d_skip, nct):
    b, l, cd = xbc.shape
    q = TIME_TILE
    nt = l // q
    tile = lambda d, bb, g: _seq_tile(d, g, nct, nt)
    r8 = q // SUBLANES
    pad = LANES - SSD_HEADS
    dtb = jnp.pad(dt_bias, ((0, 0), (0, pad))).reshape(2, 1, LANES)
    alog = jnp.pad(a_log, ((0, 0), (0, pad))).reshape(2, 1, LANES)
    dsk = jnp.repeat(d_skip, SSD_HEAD_DIM).reshape(1, SSD_INNER)
    return pl.pallas_call(
        functools.partial(_ssd_kernel, q=q, nct=nct, nt=nt),
        grid=(2, b, nt),
        in_specs=[
            pl.BlockSpec((None, q, cd), lambda d, bb, g: (bb, tile(d, bb, g), 0)),
            pl.BlockSpec((None, SUBLANES, cd),
                         lambda d, bb, g: (bb, jnp.maximum(tile(d, bb, g) * r8 - 1, 0), 0)),
            pl.BlockSpec((None, SUBLANES, cd),
                         lambda d, bb, g: (bb, jnp.minimum((tile(d, bb, g) + 1) * r8, l // SUBLANES - 1), 0)),
            pl.BlockSpec((None, q, LANES), lambda d, bb, g: (bb, tile(d, bb, g), 0)),
            pl.BlockSpec((CONV_W, 1, cd), lambda d, bb, g: (0, 0, 0)),
            pl.BlockSpec((1, cd), lambda d, bb, g: (0, 0)),
            pl.BlockSpec((None, 1, LANES), lambda d, bb, g: (d, 0, 0)),
            pl.BlockSpec((None, 1, LANES), lambda d, bb, g: (d, 0, 0)),
            pl.BlockSpec((1, SSD_INNER), lambda d, bb, g: (0, 0)),
        ],
        out_specs=pl.BlockSpec((None, None, q, SSD_INNER), lambda d, bb, g: (d, bb, tile(d, bb, g), 0)),
        out_shape=jax.ShapeDtypeStruct((2, b, l, SSD_INNER), F32),
        scratch_shapes=[
            pltpu.VMEM((q + 2 * SUBLANES, cd), F32),
            pltpu.VMEM((SSD_HEADS, SSD_STATE, SSD_HEAD_DIM), F32),
        ],
        compiler_params=_cparams(("arbitrary", "arbitrary", "arbitrary")),
        name="ssd_chunked",
    )(xbc, xbc, xbc, dt, conv_w.reshape(CONV_W, 1, cd), conv_b.reshape(1, cd), dtb, alog, dsk)


def _out_odd_kernel(x_ref, diff_ref, y_ref, z_ref, ng_ref, wa_ref, wb_ref, g1_ref, o_ref):
    yz = (y_ref[0] + y_ref[1]) * _silu(z_ref[...])
    gs = SSD_INNER // SSD_GROUPS
    parts = []
    for gi in range(SSD_GROUPS):
        seg = yz[:, gi * gs:(gi + 1) * gs]
        ms = jnp.mean(seg * seg, axis=-1, keepdims=True)
        parts.append(seg * lax.rsqrt(ms + NORM_EPS) * ng_ref[:, gi * gs:(gi + 1) * gs])
    ssd = jnp.concatenate(parts, axis=-1).astype(BF16)
    y = (jnp.dot(diff_ref[...], wa_ref[...], preferred_element_type=F32)
         + jnp.dot(ssd, wb_ref[...], preferred_element_type=F32))
    o_ref[...] = x_ref[...] + g1_ref[...] * y


def _out_odd(xc, diff, y2, z, norm_g, w_a, w_b, mods3, layer, c_len):
    b, l, d = xc.shape
    s_len = l - c_len
    tm = ROW_TILE
    off = c_len // tm
    w = SSD_INNER
    row = lambda bb, i: bb
    return pl.pallas_call(
        _out_odd_kernel,
        grid=(b, s_len // tm),
        in_specs=[
            pl.BlockSpec((None, tm, d), lambda bb, i: (bb, i + off, 0)),
            pl.BlockSpec((None, tm, diff.shape[-1]), lambda bb, i: (bb, i, 0)),
            pl.BlockSpec((2, None, tm, w), lambda bb, i: (0, bb, i + off, 0)),
            pl.BlockSpec((None, tm, w), lambda bb, i: (bb, i + off, 0)),
            pl.BlockSpec((1, w), lambda bb, i: (0, 0)),
            pl.BlockSpec(w_a.shape, lambda bb, i: (0, 0)),
            pl.BlockSpec(w_b.shape, lambda bb, i: (0, 0)),
            _mod_spec(d, layer, 2, row),
        ],
        out_specs=pl.BlockSpec((None, tm, d), lambda bb, i: (bb, i, 0)),
        out_shape=jax.ShapeDtypeStruct((b, s_len, d), F32),
        compiler_params=_cparams(("arbitrary", "arbitrary")),
        name="out_proj_odd",
    )(xc, diff, y2, z, norm_g.reshape(1, w), w_a, w_b, mods3)


def _router_kernel(x_ref, g_ref, sh_ref, sc_ref, wr_ref, br_ref, h_ref, eid_ref, rnk_ref, gate_ref, cnt_ref,
                   carry_scr):
    @pl.when((pl.program_id(0) == 0) & (pl.program_id(1) == 0))
    def _():
        carry_scr[...] = jnp.zeros_like(carry_scr)

    h = _norm_mod(x_ref[...], g_ref[...], sh_ref[...], sc_ref[...])
    _store_chunk_rows(h_ref, _pack_bf16(h))
    tm = h.shape[0]
    per = N_EXPERTS // N_EXPERT_GROUPS
    logits = lax.dot_general(wr_ref[...], h, (((1,), (1,)), ((), ())),
                             preferred_element_type=F32, precision=HIGHEST)
    scores = jax.nn.sigmoid(logits)
    sel = scores + br_ref[...]
    sel3 = sel.reshape(N_EXPERT_GROUPS, per, tm)
    kio = lax.broadcasted_iota(jnp.int32, sel3.shape, 1)
    m1 = jnp.max(sel3, axis=1, keepdims=True)
    first = jnp.min(jnp.where(sel3 == m1, kio, per), axis=1, keepdims=True)
    m2 = jnp.max(jnp.where(kio == first, NEG_BIG, sel3), axis=1, keepdims=True)
    gs = m1 + m2
    gio = lax.broadcasted_iota(jnp.int32, gs.shape, 0)
    ahead = jnp.zeros(gs.shape, jnp.int32)
    for gp in range(N_EXPERT_GROUPS):
        other = gs[gp:gp + 1]
        ahead = ahead + jnp.where((other > gs) | ((other == gs) & (gp < gio)), 1, 0)
    grp_on = jnp.where(ahead < TOPK_GROUPS, 1.0, 0.0)
    selm = jnp.where(jnp.broadcast_to(grp_on, sel3.shape) > 0.5, sel3, NEG_BIG).reshape(N_EXPERTS, tm)
    eio = lax.broadcasted_iota(jnp.int32, selm.shape, 0)
    work = selm
    cf = jnp.zeros(selm.shape, F32)
    e_rows, s_rows = [], []
    for k in range(TOP_K):
        best = jnp.max(work, axis=0, keepdims=True)
        idx = jnp.min(jnp.where(work == best, eio, N_EXPERTS), axis=0, keepdims=True)
        hit = eio == idx
        cf = cf + jnp.where(hit, 1.0, 0.0)
        work = jnp.where(hit, NEG_BIG, work)
        e_rows.append(idx)
        s_rows.append(jnp.sum(jnp.where(hit, scores, 0.0), axis=0, keepdims=True))
    denom = s_rows[0]
    for s_k in s_rows[1:]:
        denom = denom + s_k
    g_rows = [s_k / denom * ROUTED_SCALE for s_k in s_rows]
    ti = lax.broadcasted_iota(jnp.int32, (tm, tm), 0)
    tj = lax.broadcasted_iota(jnp.int32, (tm, tm), 1)
    before = jnp.where(ti < tj, 1.0, 0.0).astype(BF16)
    in_expert = carry_scr[:, 0:1] + jnp.dot(cf.astype(BF16), before, preferred_element_type=F32)
    carry_scr[...] = carry_scr[...] + jnp.sum(cf, axis=1, keepdims=True)
    cnt_ref[...] = carry_scr[...]
    r_rows = [jnp.sum(jnp.where(eio == idx, in_expert, 0.0), axis=0, keepdims=True) for idx in e_rows]
    eid_ref[...] = jnp.concatenate(e_rows, axis=0)
    rnk_ref[...] = jnp.concatenate(r_rows, axis=0).astype(jnp.int32)
    padded = jnp.concatenate(g_rows + [jnp.zeros((LANES - TOP_K, tm), F32)], axis=0)
    gate_ref[...] = padded.T


def _router(x, g, mods3, layer, row_fn, w_router_t, b_router):
    b, r, d = x.shape
    tm = ROW_TILE
    nt = r // tm
    slot = pl.BlockSpec((TOP_K, tm), lambda bb, i: (0, bb * nt + i))
    slot_shape = jax.ShapeDtypeStruct((TOP_K, b * r), jnp.int32)
    return pl.pallas_call(
        _router_kernel,
        grid=(b, nt),
        in_specs=[
            pl.BlockSpec((None, tm, d), lambda bb, i: (bb, i, 0)),
            pl.BlockSpec((1, d), lambda bb, i: (0, 0)),
            _mod_spec(d, layer, 3, row_fn),
            _mod_spec(d, layer, 4, row_fn),
            pl.BlockSpec(w_router_t.shape, lambda bb, i: (0, 0)),
            pl.BlockSpec((N_EXPERTS, 1), lambda bb, i: (0, 0)),
        ],
        out_specs=[
            pl.BlockSpec((tm * ROW_CHUNKS, LANES), lambda bb, i: (bb * nt + i, 0)),
            slot,
            slot,
            pl.BlockSpec((None, tm, LANES), lambda bb, i: (bb, i, 0)),
            pl.BlockSpec((N_EXPERTS, LANES), lambda bb, i: (0, 0)),
        ],
        out_shape=[jax.ShapeDtypeStruct((b * r * ROW_CHUNKS, LANES), jnp.uint32), slot_shape, slot_shape,
                   jax.ShapeDtypeStruct((b, r, LANES), F32), jax.ShapeDtypeStruct((N_EXPERTS, LANES), F32)],
        scratch_shapes=[pltpu.VMEM((N_EXPERTS, LANES), F32)],
        compiler_params=_cparams(("arbitrary", "arbitrary")),
        name="moe_router",
    )(x, g.reshape(1, d), mods3, mods3, w_router_t, b_router.reshape(N_EXPERTS, 1))


def _moe_plan(counts, n_rows):
    blk = EXPERT_BLK
    nb = n_rows // blk
    ends = jnp.cumsum(counts)
    starts = ends - counts
    count_le = lambda sorted_vals, q: jnp.sum(sorted_vals[None, :] <= q[:, None], axis=1, dtype=jnp.int32)
    first = jnp.arange(nb, dtype=jnp.int32) * blk
    e_lo = count_le(ends, first)
    e_hi = count_le(ends, first + (blk - 1))
    n_pair = e_hi - e_lo + 1
    p_end = jnp.cumsum(n_pair)
    p_start = p_end - n_pair
    i = jnp.arange(nb + N_EXPERTS - 1, dtype=jnp.int32)
    j = jnp.minimum(count_le(p_end, i), nb - 1)
    valid = i < p_end[-1]
    e = jnp.where(valid, e_lo[j] + i - p_start[j], e_hi[nb - 1]).astype(jnp.int32)
    bounds = jnp.concatenate([starts, ends[-1:]]).astype(jnp.int32)
    return j, e, valid.astype(jnp.int32), bounds


def _positions_kernel(starts_ref, eid_ref, rnk_ref, pos_ref):
    eid = eid_ref[...]
    pos = rnk_ref[...]
    for e in range(N_EXPERTS):
        pos = pos + jnp.where(eid == e, starts_ref[e], 0)
    pos_ref[...] = pos * ROW_CHUNKS


def _positions(eid, rnk, starts):
    full = pl.BlockSpec(eid.shape, lambda: (0, 0))
    return pl.pallas_call(
        _positions_kernel,
        in_specs=[pl.BlockSpec(memory_space=pltpu.SMEM), full, full],
        out_specs=full,
        out_shape=jax.ShapeDtypeStruct(eid.shape, jnp.int32),
        compiler_params=pltpu.CompilerParams(vmem_limit_bytes=VMEM_LIMIT),
        name="moe_positions",
    )(starts, eid, rnk)


def _token_row(ref, first):
    return ref.at[pl.ds(pl.multiple_of(first, ROW_CHUNKS), ROW_CHUNKS)]


def _dispatch_kernel(pos_ref, h_ref, xs_ref, sem):
    tm = h_ref.shape[0] // ROW_CHUNKS

    def issue(t, carry):
        src = _token_row(h_ref, t * ROW_CHUNKS)
        for k in range(TOP_K):
            pltpu.make_async_copy(src, _token_row(xs_ref, pos_ref[k, t]), sem).start(priority=k % 2)
        return carry

    lax.fori_loop(0, tm, issue, 0)
    done = pl.ds(0, tm * ROW_CHUNKS)
    for _ in range(TOP_K):
        pltpu.make_async_copy(h_ref.at[done], xs_ref.at[done], sem).wait()


def _dispatch(h2, pos):
    rows, w = h2.shape
    tm = ROW_TILE
    return pl.pallas_call(
        _dispatch_kernel,
        grid=(rows // (tm * ROW_CHUNKS),),
        in_specs=[
            pl.BlockSpec((TOP_K, tm), lambda i: (0, i), memory_space=pltpu.SMEM),
            pl.BlockSpec((tm * ROW_CHUNKS, w), lambda i: (i, 0)),
        ],
        out_specs=pl.BlockSpec(memory_space=pl.ANY),
        out_shape=jax.ShapeDtypeStruct((rows * TOP_K, w), h2.dtype),
        scratch_shapes=[pltpu.SemaphoreType.DMA],
        compiler_params=_cparams(("arbitrary",)),
        name="moe_dispatch",
    )(pos, h2)


def _grouped_kernel(pb_ref, pe_ref, pv_ref, bnd_ref, xs_ref, wg_ref, wu_ref, wd_ref, y_ref, wgb, wub, wdb):
    i = pl.program_id(0)
    prev = jnp.maximum(i - 1, 0)
    j = pb_ref[i]
    e = pe_ref[i]
    blk = xs_ref.shape[0] // ROW_CHUNKS

    @pl.when((i == 0) | (pb_ref[prev] != j))
    def _():
        y_ref[...] = jnp.zeros_like(y_ref)

    @pl.when((i == 0) | (pe_ref[prev] != e))
    def _():
        wgb[...] = wg_ref[...].astype(BF16)
        wub[...] = wu_ref[...].astype(BF16)
        wdb[...] = wd_ref[...].astype(BF16)

    @pl.when(pv_ref[i] == 1)
    def _():
        xw = _load_chunk_rows(xs_ref, blk)
        a = _packed_dot(xw, wgb)
        u = _packed_dot(xw, wub)
        yv = jnp.dot((_silu(a) * u).astype(BF16), wdb[...], preferred_element_type=F32)
        rows = j * blk + lax.broadcasted_iota(jnp.int32, (blk, 1), 0)
        own = (rows >= bnd_ref[e]) & (rows < bnd_ref[e + 1])
        yw = _pack_bf16(yv)
        for c in range(ROW_CHUNKS):
            sl = pl.ds(c, blk, stride=ROW_CHUNKS)
            y_ref[sl, :] = jnp.where(own, yw[:, c * LANES:(c + 1) * LANES], y_ref[sl, :])


def _grouped(pb, pe, pv, bounds, xs, wg, wu, wd, layer):
    p, half = xs.shape
    d = 2 * ROW_CHUNKS * LANES
    blk = EXPERT_BLK * ROW_CHUNKS
    grid_spec = pltpu.PrefetchScalarGridSpec(
        num_scalar_prefetch=4,
        grid=(pb.shape[0],),
        in_specs=[
            pl.BlockSpec((blk, half), lambda i, pb, pe, pv, bnd: (pb[i], 0)),
            pl.BlockSpec((None, None, d, D_EXPERT), lambda i, pb, pe, pv, bnd: (layer, pe[i], 0, 0)),
            pl.BlockSpec((None, None, d, D_EXPERT), lambda i, pb, pe, pv, bnd: (layer, pe[i], 0, 0)),
            pl.BlockSpec((None, None, D_EXPERT, d), lambda i, pb, pe, pv, bnd: (layer, pe[i], 0, 0)),
        ],
        out_specs=pl.BlockSpec((blk, half), lambda i, pb, pe, pv, bnd: (pb[i], 0)),
        scratch_shapes=[
            pltpu.VMEM((d, D_EXPERT), BF16),
            pltpu.VMEM((d, D_EXPERT), BF16),
            pltpu.VMEM((D_EXPERT, d), BF16),
        ],
    )
    return pl.pallas_call(
        _grouped_kernel,
        grid_spec=grid_spec,
        out_shape=jax.ShapeDtypeStruct((p, half), jnp.uint32),
        compiler_params=_cparams(("arbitrary",)),
        name="moe_grouped_experts",
    )(pb, pe, pv, bounds, xs, wg, wu, wd)


def _combine_kernel(*refs, final, n_steps):
    pos_ref, nxt_ref, y_ref, gate_ref, h_ref, x_ref, g2_ref, sg_ref, su_ref, sd_ref = refs[:10]
    o_ref, buf, sems = refs[-3:]
    tm = x_ref.shape[0]
    step = pl.program_id(0) * pl.num_programs(1) + pl.program_id(1)
    slot = step % 2

    def gather(p_ref, to):
        def issue(t, carry):
            for k in range(TOP_K):
                pltpu.make_async_copy(_token_row(y_ref, p_ref[k, t]),
                                      _token_row(buf.at[to, k], t * ROW_CHUNKS),
                                      sems.at[to]).start(priority=k % 2)
            return carry
        lax.fori_loop(0, tm, issue, 0)

    @pl.when(step == 0)
    def _():
        gather(pos_ref, 0)

    @pl.when(step + 1 < n_steps)
    def _():
        gather(nxt_ref, 1 - slot)

    hw = _load_chunk_rows(h_ref, tm)
    a = _packed_dot(hw, sg_ref)
    u = _packed_dot(hw, su_ref)
    acc = jnp.dot((_silu(a) * u).astype(BF16), sd_ref[...], preferred_element_type=F32)
    done = pl.ds(0, tm * ROW_CHUNKS)
    for k in range(TOP_K):
        pltpu.make_async_copy(y_ref.at[done], buf.at[slot, k, done], sems.at[slot]).wait()
    g = gate_ref[...]
    half = hw.shape[-1]
    acc_hi = acc[:, :half]
    acc_lo = acc[:, half:]
    for k in range(TOP_K):
        hi, lo = _unpack_bf16(_load_chunk_rows(buf.at[slot, k], tm))
        acc_hi = acc_hi + g[:, k:k + 1] * hi
        acc_lo = acc_lo + g[:, k:k + 1] * lo
    x = x_ref[...] + g2_ref[...] * jnp.concatenate([acc_hi, acc_lo], axis=-1)
    if final:
        gf_ref = refs[10]
        ms = jnp.mean(x * x, axis=-1, keepdims=True)
        x = x * lax.rsqrt(ms + NORM_EPS) * gf_ref[...]
    o_ref[...] = x


def _combine(pos, y, gates, h2, x, mods3, layer, row_fn, sg, su, sd, g_final=None):
    b, r, d = x.shape
    tm = ROW_TILE
    nt = r // tm
    n_steps = b * nt
    tile = pl.BlockSpec((None, tm, d), lambda bb, i: (bb, i, 0))
    in_specs = [
        pl.BlockSpec((TOP_K, tm), lambda bb, i: (0, bb * nt + i), memory_space=pltpu.SMEM),
        pl.BlockSpec((TOP_K, tm), lambda bb, i: (0, jnp.minimum(bb * nt + i + 1, n_steps - 1)),
                     memory_space=pltpu.SMEM),
        pl.BlockSpec(memory_space=pl.ANY),
        pl.BlockSpec((None, tm, LANES), lambda bb, i: (bb, i, 0)),
        pl.BlockSpec((tm * ROW_CHUNKS, LANES), lambda bb, i: (bb * nt + i, 0)),
        tile,
        _mod_spec(d, layer, 5, row_fn),
        pl.BlockSpec(sg.shape, lambda bb, i: (0, 0)),
        pl.BlockSpec(su.shape, lambda bb, i: (0, 0)),
        pl.BlockSpec(sd.shape, lambda bb, i: (0, 0)),
    ]
    args = [pos, pos, y, gates, h2, x, mods3, sg, su, sd]
    if g_final is not None:
        in_specs.append(pl.BlockSpec((1, d), lambda bb, i: (0, 0)))
        args.append(g_final.reshape(1, d))
    return pl.pallas_call(
        functools.partial(_combine_kernel, final=g_final is not None, n_steps=n_steps),
        grid=(b, nt),
        in_specs=in_specs,
        out_specs=tile,
        out_shape=jax.ShapeDtypeStruct((b, r, d), F32),
        scratch_shapes=[pltpu.VMEM((2, TOP_K, tm * ROW_CHUNKS, LANES), jnp.uint32),
                        pltpu.SemaphoreType.DMA((2,))],
        compiler_params=_cparams(("arbitrary", "arbitrary")),
        name="moe_combine",
    )(*args)


def _moe(x, g_ffn, mods3, layer, row_fn, w_router, b_router, w_e_gate, w_e_up, w_e_down,
         ws_gate, ws_up, ws_down, g_final=None):
    b, r, d = x.shape
    h2, eid, rnk, gates, cnt = _router(x, g_ffn, mods3, layer, row_fn, w_router.T, b_router)
    pb, pe, pv, bounds = _moe_plan(cnt[:, 0].astype(jnp.int32), b * r * TOP_K)
    pos = _positions(eid, rnk, bounds[:N_EXPERTS])
    xs = _dispatch(h2, pos)
    y = _grouped(pb, pe, pv, bounds, xs, w_e_gate, w_e_up, w_e_down, layer)
    return _combine(pos, y, gates, h2, x, mods3, layer, row_fn,
                    ws_gate.astype(BF16), ws_up.astype(BF16), ws_down.astype(BF16), g_final)


def kernel(x, c, ctx, c_ctx, w_mod, b_mod, g_mix, g_ffn, g_final, ab_w_in, ab_w_out, ab_conv_w, ab_conv_b, ab_w_r, ab_b_r, ab_w_i, ab_b_i, ab_lam, ab_sink, cd_w_in, cd_w_out, cd_lam, cd_subln_g, cd_conv_w, cd_conv_b, cd_dt_bias, cd_a_log, cd_d_skip, cd_norm_g, w_router, b_router, w_e_gate, w_e_up, w_e_down, ws_gate, ws_up, ws_down):
    bsz, s_len, d = x.shape
    c_len = ctx.shape[1]
    depth = w_mod.shape[0]
    assert depth == 2 and bsz == SUBLANES, "kernels are specialised to depth 2 and batch 8"
    assert c_len % ROW_TILE == 0 and s_len % ROW_TILE == 0
    nct_row = c_len // ROW_TILE
    nct_time = c_len // TIME_TILE

    c_all = jnp.concatenate([c, c_ctx[None], jnp.zeros((MOD_ROWS - bsz - 1, d), F32)], axis=0)
    mods3 = _modulations(c_all, w_mod, b_mod).reshape(depth * MOD_ROWS, 1, N_MOD * d)
    rope_tabs = _rope_tables(c_len, s_len)
    xc = jnp.concatenate([ctx, x], axis=1)
    row_mixed = lambda bb, i: jnp.where(i < nct_row, SUBLANES, bb)
    row_latent = lambda bb, i: bb

    w_in = ab_w_in[0].astype(BF16)
    q_hi = LRU_WIDTH + WIN_HEADS * HEAD_DIM
    x_hi = q_hi + LRU_WIDTH
    k_hi = x_hi + WIN_KV_HEADS * HEAD_DIM
    gate, q, xa, k, v = _project(xc, g_mix[0], mods3, 0, nct_row, rope_tabs, [
        (w_in[:, :LRU_WIDTH], None, F32, False),
        (w_in[:, LRU_WIDTH:q_hi], HEAD_DIM ** -0.5 * LOG2E, BF16, False),
        (w_in[:, q_hi:x_hi], None, F32, True),
        (w_in[:, x_hi:k_hi], 1.0, BF16, False),
        (w_in[:, k_hi:], None, BF16, False),
    ])
    l_len = c_len + s_len
    w_gates = jnp.stack([jnp.concatenate([_block_diag(ab_w_r[0, dd]), _block_diag(ab_w_i[0, dd])], axis=1)
                         for dd in range(2)]).astype(BF16)
    b_gates = jnp.concatenate([ab_b_r[0], ab_b_i[0]], axis=-1).reshape(2, 1, 2 * LRU_WIDTH)
    rec = _rglru(xa.reshape(l_len, bsz, LRU_WIDTH), ab_conv_w[0], ab_conv_b[0], w_gates, b_gates,
                 ab_lam[0].reshape(2, 1, LRU_WIDTH), nct_time)
    att = _win_attention(q, k, v, ab_sink[0], c_len)
    w_out = ab_w_out[0].astype(BF16)
    xc = _out_even(xc, rec.reshape(2, l_len, bsz * LRU_WIDTH), gate, att, w_out[:LRU_WIDTH], w_out[LRU_WIDTH:],
                   mods3, 0, nct_row)
    xc = _moe(xc, g_ffn[0], mods3, 0, row_mixed, w_router[0], b_router[0], w_e_gate, w_e_up, w_e_down,
              ws_gate[0], ws_up[0], ws_down[0])

    w_in = cd_w_in[0].astype(BF16)
    qk = DIFF_HEADS * 2 * DIFF_DH
    z_hi = qk + SSD_INNER
    k_hi = z_hi + qk
    v_hi = k_hi + qk
    x_hi = v_hi + SSD_CONV_DIM
    w_dt = jnp.pad(w_in[:, x_hi:], ((0, 0), (0, LANES - 2 * SSD_HEADS)))
    q, z, k, v, xbc, dt = _project(xc, g_mix[1], mods3, 1, nct_row, rope_tabs, [
        (w_in[:, :qk], DIFF_DH ** -0.5 * LOG2E, BF16, False),
        (w_in[:, qk:z_hi], None, F32, False),
        (w_in[:, z_hi:k_hi], 1.0, BF16, False),
        (w_in[:, k_hi:v_hi], None, BF16, False),
        (w_in[:, v_hi:x_hi], None, F32, False),
        (w_dt, None, F32, False),
    ])
    lam_init = 0.8 - 0.6 * math.exp(-0.3 * 1)
    diff = _diff_attention(q, k, v, cd_lam[0], cd_subln_g[0], lam_init, c_len)
    y2 = _ssd(xbc, dt, cd_conv_w[0], cd_conv_b[0], cd_dt_bias[0], cd_a_log[0], cd_d_skip[0], nct_time)
    w_out = cd_w_out[0].astype(BF16)
    xl = _out_odd(xc, diff, y2, z, cd_norm_g[0], w_out[:qk], w_out[qk:], mods3, 1, c_len)
    return _moe(xl, g_ffn[1], mods3, 1, row_latent, w_router[1], b_router[1], w_e_gate, w_e_up, w_e_down,
                ws_gate[1], ws_up[1], ws_down[1], g_final=g_final)
```

```python
import functools
import math

import jax
import jax.numpy as jnp
from jax import lax
from jax.experimental import pallas as pl
from jax.experimental.pallas import tpu as pltpu

F32 = jnp.float32
BF16 = jnp.bfloat16
HIGHEST = lax.Precision.HIGHEST

GRID_W = 64
N_MOD = 6
NORM_EPS = 1e-6
ROPE_BASE = 10000.0
CONV_W = 4

LRU_WIDTH = 512
LRU_BLOCKS = 8
LRU_C = 8.0

HEAD_DIM = 64
WIN_HEADS = 8
WIN_KV_HEADS = 2
WINDOW = 128

DIFF_HEADS = 4
DIFF_DH = 64

SSD_HEADS = 8
SSD_HEAD_DIM = 64
SSD_INNER = SSD_HEADS * SSD_HEAD_DIM
SSD_GROUPS = 2
SSD_STATE = 128
SSD_CONV_DIM = SSD_INNER + 2 * SSD_GROUPS * SSD_STATE

N_EXPERTS = 64
N_EXPERT_GROUPS = 8
TOPK_GROUPS = 4
TOP_K = 8
D_EXPERT = 256
ROUTED_SCALE = 2.5

LANES = 128
SUBLANES = 8
MOD_ROWS = 16
TIME_TILE = 128
ROW_TILE = 256
EXPERT_BLK = 1024
ROW_CHUNKS = 4
VMEM_LIMIT = 48 * 1024 * 1024
NEG_BIG = -1e30
LOG2E = math.log2(math.e)


def _cparams(sem):
    return pltpu.CompilerParams(dimension_semantics=sem, vmem_limit_bytes=VMEM_LIMIT)


def _nt_dot(a, b):
    return lax.dot_general(a, b, (((1,), (1,)), ((), ())), preferred_element_type=F32)


def _softplus(x):
    return jnp.maximum(x, 0.0) + jnp.log1p(jnp.exp(-jnp.abs(x)))


def _silu(x):
    return x * jax.nn.sigmoid(x)


def _pack_bf16(x):
    half = x.shape[-1] // 2
    bits = pltpu.bitcast(x.astype(BF16).astype(F32), jnp.uint32)
    return bits[:, :half] | (bits[:, half:] >> 16)


def _unpack_bf16(w):
    hi = pltpu.bitcast(w & jnp.uint32(0xFFFF0000), F32)
    lo = pltpu.bitcast(w << 16, F32)
    return hi, lo


def _store_chunk_rows(ref, w):
    n = w.shape[0]
    for j in range(ROW_CHUNKS):
        ref[pl.ds(j, n, stride=ROW_CHUNKS), :] = w[:, j * LANES:(j + 1) * LANES]


def _load_chunk_rows(ref, n):
    return jnp.concatenate([ref[pl.ds(j, n, stride=ROW_CHUNKS), :] for j in range(ROW_CHUNKS)], axis=1)


def _packed_dot(w, weight_ref):
    half = w.shape[-1]
    hi, lo = _unpack_bf16(w)
    return (jnp.dot(hi.astype(BF16), weight_ref[:half, :], preferred_element_type=F32)
            + jnp.dot(lo.astype(BF16), weight_ref[half:, :], preferred_element_type=F32))


def _mod_kernel(c_ref, w_ref, b_ref, o_ref):
    c = c_ref[...]
    s = _silu(c)
    o_ref[...] = jnp.dot(s, w_ref[...], preferred_element_type=F32, precision=HIGHEST) + b_ref[...]


def _modulations(c_all, w_mod, b_mod):
    depth, d, _ = w_mod.shape
    return pl.pallas_call(
        _mod_kernel,
        grid=(depth, N_MOD),
        in_specs=[
            pl.BlockSpec((MOD_ROWS, d), lambda l, k: (0, 0)),
            pl.BlockSpec((None, d, d), lambda l, k: (l, 0, k)),
            pl.BlockSpec((None, 1, d), lambda l, k: (l, 0, k)),
        ],
        out_specs=pl.BlockSpec((None, MOD_ROWS, d), lambda l, k: (l, 0, k)),
        out_shape=jax.ShapeDtypeStruct((depth, MOD_ROWS, N_MOD * d), F32),
        compiler_params=_cparams(("arbitrary", "arbitrary")),
        name="adaln_modulation",
    )(c_all, w_mod, b_mod.reshape(depth, 1, N_MOD * d))


def _mod_spec(d, layer, chunk, row_fn):
    return pl.BlockSpec((None, 1, d), lambda b, i: (layer * MOD_ROWS + row_fn(b, i), 0, chunk))


def _norm_mod(x, g, sh, sc):
    ms = jnp.mean(x * x, axis=-1, keepdims=True)
    return (x * lax.rsqrt(ms + NORM_EPS) * g) * (1.0 + sc) + sh


def _rope(y, cos, sa, sb):
    n = y.shape[-1]
    half = HEAD_DIM // 2
    return y * cos + pltpu.roll(y, n - half, 1) * sa + pltpu.roll(y, half, 1) * sb


def _proj_kernel(*refs, ropes):
    n = len(ropes)
    x_ref, g_ref, sh_ref, sc_ref, cos_ref, sa_ref, sb_ref = refs[:7]
    w_refs = refs[7:7 + n]
    o_refs = refs[7 + n:]
    h = _norm_mod(x_ref[...], g_ref[...], sh_ref[...], sc_ref[...]).astype(BF16)
    for w_ref, o_ref, rope in zip(w_refs, o_refs, ropes):
        y = jnp.dot(h, w_ref[...], preferred_element_type=F32)
        if rope is not None:
            w = y.shape[-1]
            y = _rope(y, cos_ref[:, :w], sa_ref[:, :w], sb_ref[:, :w])
            if rope != 1.0:
                y = y * rope
        o_ref[...] = y.astype(o_ref.dtype)


def _project(xc, g, mods3, layer, nct, rope_tabs, groups):
    b, l, d = xc.shape
    tm = ROW_TILE
    row = lambda bb, i: jnp.where(i < nct, SUBLANES, bb)
    rw = rope_tabs[0].shape[-1]
    in_specs = [
        pl.BlockSpec((None, tm, d), lambda bb, i: (bb, i, 0)),
        pl.BlockSpec((1, d), lambda bb, i: (0, 0)),
        _mod_spec(d, layer, 0, row),
        _mod_spec(d, layer, 1, row),
    ] + [pl.BlockSpec((tm, rw), lambda bb, i: (i, 0))] * 3
    out_specs, out_shapes = [], []
    for w, _, dt, time_major in groups:
        n = w.shape[1]
        in_specs.append(pl.BlockSpec((d, n), lambda bb, i: (0, 0)))
        if time_major:
            out_specs.append(pl.BlockSpec((tm, n), lambda bb, i: (i, bb)))
            out_shapes.append(jax.ShapeDtypeStruct((l, b * n), dt))
        else:
            out_specs.append(pl.BlockSpec((None, tm, n), lambda bb, i: (bb, i, 0)))
            out_shapes.append(jax.ShapeDtypeStruct((b, l, n), dt))
    return pl.pallas_call(
        functools.partial(_proj_kernel, ropes=tuple(gp[1] for gp in groups)),
        grid=(b, l // tm),
        in_specs=in_specs,
        out_specs=out_specs,
        out_shape=out_shapes,
        compiler_params=_cparams(("arbitrary", "arbitrary")),
        name="norm_mod_project",
    )(xc, g.reshape(1, d), mods3, mods3, *rope_tabs, *[gp[0] for gp in groups])


def _rope_tables(c_len, s_len):
    rows = s_len // GRID_W
    row = jnp.repeat(jnp.arange(rows), GRID_W).astype(F32)
    col = jnp.tile(jnp.arange(GRID_W), rows).astype(F32)
    n = HEAD_DIM // 4
    inv = ROPE_BASE ** (-jnp.arange(n, dtype=F32) / n)
    ang = jnp.concatenate([row[:, None] * inv, col[:, None] * inv], axis=-1)
    cos, sin = jnp.cos(ang), jnp.sin(ang)
    zero = jnp.zeros_like(sin)
    reps = WIN_HEADS
    cos_t = jnp.tile(jnp.concatenate([cos, cos], axis=-1), (1, reps))
    sa_t = jnp.tile(jnp.concatenate([-sin, zero], axis=-1), (1, reps))
    sb_t = jnp.tile(jnp.concatenate([zero, sin], axis=-1), (1, reps))
    w = cos_t.shape[-1]
    pad1 = jnp.ones((c_len, w), F32)
    pad0 = jnp.zeros((c_len, w), F32)
    return (jnp.concatenate([pad1, cos_t], 0), jnp.concatenate([pad0, sa_t], 0),
            jnp.concatenate([pad0, sb_t], 0))


def _seq_tile(d, g, nct, nt):
    rev = jnp.where(g < nct, nct - 1 - g, nt - 1 - (g - nct))
    return jnp.where(d == 0, g, rev)


def _rglru_kernel(x_ref, xp_ref, xn_ref, cw_ref, cb_ref, w_ref, bias_ref, lam_ref, o_ref,
                  ext_scr, a_scr, b_scr, h_scr, *, ts, nct, nt, sub):
    d = pl.program_id(0)
    g = pl.program_id(1)
    tile = _seq_tile(d, g, nct, nt)
    bsz, width = h_scr.shape
    pv = jnp.where((tile == 0) | (tile == nct), 0.0, 1.0)
    nv = jnp.where((tile == nct - 1) | (tile == nt - 1), 0.0, 1.0)
    ext_scr[0:1] = xp_ref[...] * pv
    ext_scr[1:ts + 1] = x_ref[...]
    ext_scr[ts + 1:ts + 3] = xn_ref[...] * nv

    @pl.when(g == 0)
    def _():
        h_scr[...] = jnp.zeros_like(h_scr)

    neg_sp = -LRU_C * _softplus(-lam_ref[...])

    def prep(c, carry):
        r0 = pl.multiple_of(c * sub, sub)
        e = ext_scr[pl.ds(r0, sub + CONV_W - 1)]
        u = cb_ref[...] + cw_ref[0] * e[0:sub]
        for j in range(1, CONV_W):
            u = u + cw_ref[j] * e[j:j + sub]
        u2 = u.reshape(sub * bsz, width)
        gts = jnp.dot(u2.astype(BF16), w_ref[...], preferred_element_type=F32) + bias_ref[...]
        r = jax.nn.sigmoid(gts[:, :width])
        ig = jax.nn.sigmoid(gts[:, width:])
        log_a = neg_sp * r
        a = jnp.exp(log_a)
        mult = jnp.sqrt(1.0 - a * a)
        a_scr[pl.ds(r0, sub)] = a.reshape(sub, bsz, width)
        b_scr[pl.ds(r0, sub)] = (mult * ig * u2).reshape(sub, bsz, width)
        return carry

    lax.fori_loop(0, ts // sub, prep, 0)

    def step(t, h):
        tt = jnp.where(d == 0, t, ts - 1 - t)
        h = a_scr[tt] * h + b_scr[tt]
        o_ref[tt] = h
        return h

    h_scr[...] = lax.fori_loop(0, ts, step, h_scr[...], unroll=8)


def _rglru(xa_tm, conv_w, conv_b, w_gates, b_gates, lam, nct):
    l, bsz, width = xa_tm.shape
    ts = TIME_TILE
    nt = l // ts
    tile = lambda d, g: _seq_tile(d, g, nct, nt)
    kern = functools.partial(_rglru_kernel, ts=ts, nct=nct, nt=nt, sub=16)
    return pl.pallas_call(
        kern,
        grid=(2, nt),
        in_specs=[
            pl.BlockSpec((ts, bsz, width), lambda d, g: (tile(d, g), 0, 0)),
            pl.BlockSpec((1, bsz, width), lambda d, g: (jnp.maximum(tile(d, g) * ts - 1, 0), 0, 0)),
            pl.BlockSpec((2, bsz, width),
                         lambda d, g: (jnp.minimum((tile(d, g) + 1) * (ts // 2), l // 2 - 1), 0, 0)),
            pl.BlockSpec((CONV_W, 1, width), lambda d, g: (0, 0, 0)),
            pl.BlockSpec((1, width), lambda d, g: (0, 0)),
            pl.BlockSpec((None, width, 2 * width), lambda d, g: (d, 0, 0)),
            pl.BlockSpec((None, 1, 2 * width), lambda d, g: (d, 0, 0)),
            pl.BlockSpec((None, 1, width), lambda d, g: (d, 0, 0)),
        ],
        out_specs=pl.BlockSpec((None, ts, bsz, width), lambda d, g: (d, tile(d, g), 0, 0)),
        out_shape=jax.ShapeDtypeStruct((2, l, bsz, width), F32),
        scratch_shapes=[
            pltpu.VMEM((ts + CONV_W - 1, bsz, width), F32),
            pltpu.VMEM((ts, bsz, width), F32),
            pltpu.VMEM((ts, bsz, width), F32),
            pltpu.VMEM((bsz, width), F32),
        ],
        compiler_params=_cparams(("arbitrary", "arbitrary")),
        name="rglru_scan",
    )(xa_tm, xa_tm, xa_tm, conv_w.reshape(CONV_W, 1, width), conv_b.reshape(1, width),
      w_gates, b_gates, lam)


def _block_diag(w):
    nb, c, dd = w.shape
    eye = jnp.eye(nb, dtype=w.dtype)
    return (eye[:, None, :, None] * w[:, :, None, :]).reshape(nb * c, nb * dd)


def _win_attn_kernel(sink_ref, q_ref, k_ref, v_ref, o_ref, *, c_len, l_len, nqc):
    j = pl.program_id(1)
    blk = q_ref.shape[0]
    grp = WIN_HEADS // WIN_KV_HEADS
    band = 3 * blk
    heads = [(h, slice(h * HEAD_DIM, (h + 1) * HEAD_DIM),
              slice((h // grp) * HEAD_DIM, (h // grp + 1) * HEAD_DIM)) for h in range(WIN_HEADS)]

    @pl.when(j < nqc)
    def _():
        logits = [_nt_dot(q_ref[:, hsl], k_ref[0:c_len, ksl]) for _, hsl, ksl in heads]
        probs, dens = [], []
        for (h, _, _), s in zip(heads, logits):
            sink = sink_ref[h] * LOG2E
            m = jnp.maximum(jnp.max(s, axis=-1, keepdims=True), sink)
            p = jnp.exp2(s - m)
            dens.append(jnp.sum(p, axis=-1, keepdims=True) + jnp.exp2(sink - m))
            probs.append(p.astype(BF16))
        outs = [jnp.dot(p, v_ref[0:c_len, ksl], preferred_element_type=F32) / den
                for (_, _, ksl), p, den in zip(heads, probs, dens)]
        o_ref[...] = jnp.concatenate(outs, axis=-1).astype(o_ref.dtype)

    @pl.when(j >= nqc)
    def _():
        jb = j - nqc
        start = jnp.clip(c_len + (jb - 1) * blk, c_len - blk, l_len - band)
        start = pl.multiple_of(start, blk)
        qpos = jb * blk + lax.broadcasted_iota(jnp.int32, (blk, band), 0)
        kpos = start - c_len + lax.broadcasted_iota(jnp.int32, (blk, band), 1)
        valid = (jnp.abs(qpos - kpos) <= WINDOW) & (kpos >= 0)
        lc = [_nt_dot(q_ref[:, hsl], k_ref[0:c_len, ksl]) for _, hsl, ksl in heads]
        lb = [jnp.where(valid, _nt_dot(q_ref[:, hsl], k_ref[pl.ds(start, band), ksl]), NEG_BIG)
              for _, hsl, ksl in heads]
        pcs, pbs, dens = [], [], []
        for (h, _, _), sc, sb in zip(heads, lc, lb):
            sink = sink_ref[h] * LOG2E
            m = jnp.maximum(jnp.maximum(jnp.max(sc, axis=-1, keepdims=True),
                                        jnp.max(sb, axis=-1, keepdims=True)), sink)
            pc = jnp.exp2(sc - m)
            pb = jnp.exp2(sb - m)
            dens.append(jnp.sum(pc, axis=-1, keepdims=True) + jnp.sum(pb, axis=-1, keepdims=True)
                        + jnp.exp2(sink - m))
            pcs.append(pc.astype(BF16))
            pbs.append(pb.astype(BF16))
        outs = [(jnp.dot(pc, v_ref[0:c_len, ksl], preferred_element_type=F32)
                 + jnp.dot(pb, v_ref[pl.ds(start, band), ksl], preferred_element_type=F32)) / den
                for (_, _, ksl), pc, pb, den in zip(heads, pcs, pbs, dens)]
        o_ref[...] = jnp.concatenate(outs, axis=-1).astype(o_ref.dtype)


def _win_attention(q, k, v, sink, c_len):
    b, l, qw = q.shape
    kw = k.shape[-1]
    blk = TIME_TILE
    kern = functools.partial(_win_attn_kernel, c_len=c_len, l_len=l, nqc=c_len // blk)
    return pl.pallas_call(
        kern,
        grid=(b, l // blk),
        in_specs=[
            pl.BlockSpec(memory_space=pltpu.SMEM),
            pl.BlockSpec((None, blk, qw), lambda bb, j: (bb, j, 0)),
            pl.BlockSpec((None, l, kw), lambda bb, j: (bb, 0, 0)),
            pl.BlockSpec((None, l, kw), lambda bb, j: (bb, 0, 0)),
        ],
        out_specs=pl.BlockSpec((None, blk, qw), lambda bb, j: (bb, j, 0)),
        out_shape=jax.ShapeDtypeStruct((b, l, qw), BF16),
        compiler_params=_cparams(("arbitrary", "arbitrary")),
        name="window_attention",
    )(sink, q, k, v)


def _out_even_kernel(x_ref, rec_ref, gate_ref, att_ref, wa_ref, wb_ref, g1_ref, o_ref):
    lru = (rec_ref[0] + rec_ref[1]) * jax.nn.gelu(gate_ref[...])
    y = (jnp.dot(lru.astype(BF16), wa_ref[...], preferred_element_type=F32)
         + jnp.dot(att_ref[...], wb_ref[...], preferred_element_type=F32))
    o_ref[...] = x_ref[...] + g1_ref[...] * y


def _out_even(xc, rec2, gate, att, w_a, w_b, mods3, layer, nct):
    b, l, d = xc.shape
    tm = ROW_TILE
    w = gate.shape[-1]
    row = lambda bb, i: jnp.where(i < nct, SUBLANES, bb)
    return pl.pallas_call(
        _out_even_kernel,
        grid=(b, l // tm),
        in_specs=[
            pl.BlockSpec((None, tm, d), lambda bb, i: (bb, i, 0)),
            pl.BlockSpec((2, tm, w), lambda bb, i: (0, i, bb)),
            pl.BlockSpec((None, tm, w), lambda bb, i: (bb, i, 0)),
            pl.BlockSpec((None, tm, att.shape[-1]), lambda bb, i: (bb, i, 0)),
            pl.BlockSpec(w_a.shape, lambda bb, i: (0, 0)),
            pl.BlockSpec(w_b.shape, lambda bb, i: (0, 0)),
            _mod_spec(d, layer, 2, row),
        ],
        out_specs=pl.BlockSpec((None, tm, d), lambda bb, i: (bb, i, 0)),
        out_shape=jax.ShapeDtypeStruct((b, l, d), F32),
        compiler_params=_cparams(("arbitrary", "arbitrary")),
        name="out_proj_even",
    )(xc, rec2, gate, att, w_a, w_b, mods3)


def _diff_attn_kernel(lam_ref, g_ref, q_ref, k_ref, v_ref, o_ref, *, lam_init):
    lv = lam_ref[...]
    lam = (jnp.exp(jnp.sum(lv[0:1] * lv[1:2], axis=-1, keepdims=True))
           - jnp.exp(jnp.sum(lv[2:3] * lv[3:4], axis=-1, keepdims=True)) + lam_init)
    vw = 2 * DIFF_DH

    def logits(h, mp):
        lo = h * vw + mp * DIFF_DH
        return _nt_dot(q_ref[:, lo:lo + DIFF_DH], k_ref[:, lo:lo + DIFF_DH])

    def softmax_parts(s):
        e = jnp.exp2(s - jnp.max(s, axis=-1, keepdims=True))
        return e, 1.0 / jnp.sum(e, axis=-1, keepdims=True)

    heads = range(DIFF_HEADS)
    ls = [(logits(h, 0), logits(h, 1)) for h in heads]
    ws = []
    for l0, l1 in ls:
        e0, r0 = softmax_parts(l0)
        e1, r1 = softmax_parts(l1)
        ws.append((e0 * r0 - e1 * (lam * r1)).astype(BF16))
    for h, w in zip(heads, ws):
        lo = h * vw
        o = jnp.dot(w, v_ref[:, lo:lo + vw], preferred_element_type=F32)
        ms = jnp.mean(o * o, axis=-1, keepdims=True)
        o = o * lax.rsqrt(ms + NORM_EPS) * g_ref[...]
        o_ref[:, lo:lo + vw] = (o * (1.0 - lam_init)).astype(o_ref.dtype)


def _diff_attention(q, k, v, lam_vecs, subln_g, lam_init, c_len):
    b, l, w = q.shape
    tq = ROW_TILE
    s_len = l - c_len
    off = c_len // tq
    return pl.pallas_call(
        functools.partial(_diff_attn_kernel, lam_init=lam_init),
        grid=(b, s_len // tq),
        in_specs=[
            pl.BlockSpec(lam_vecs.shape, lambda bb, j: (0, 0)),
            pl.BlockSpec((1, 2 * DIFF_DH), lambda bb, j: (0, 0)),
            pl.BlockSpec((None, tq, w), lambda bb, j: (bb, j + off, 0)),
            pl.BlockSpec((None, l, w), lambda bb, j: (bb, 0, 0)),
            pl.BlockSpec((None, l, w), lambda bb, j: (bb, 0, 0)),
        ],
        out_specs=pl.BlockSpec((None, tq, w), lambda bb, j: (bb, j, 0)),
        out_shape=jax.ShapeDtypeStruct((b, s_len, w), BF16),
        compiler_params=_cparams(("arbitrary", "arbitrary")),
        name="diff_attention",
    )(lam_vecs, subln_g.reshape(1, -1), q, k, v)


def _ssd_kernel(x_ref, xp_ref, xn_ref, dt_ref, cw_ref, cb_ref, dtb_ref, alog_ref, dsk_ref, o_ref,
                ext_scr, st_scr, *, q, nct, nt):
    d = pl.program_id(0)
    g = pl.program_id(2)
    tile = _seq_tile(d, g, nct, nt)
    pv = jnp.where((tile == 0) | (tile == nct), 0.0, 1.0)
    nv = jnp.where((tile == nct - 1) | (tile == nt - 1), 0.0, 1.0)
    ext_scr[0:SUBLANES] = xp_ref[...] * pv
    ext_scr[SUBLANES:SUBLANES + q] = x_ref[...]
    ext_scr[SUBLANES + q:2 * SUBLANES + q] = xn_ref[...] * nv

    @pl.when(g == 0)
    def _():
        st_scr[...] = jnp.zeros_like(st_scr)

    u = cb_ref[...] + cw_ref[0] * ext_scr[SUBLANES - 1:SUBLANES - 1 + q, :]
    for j in range(1, CONV_W):
        u = u + cw_ref[j] * ext_scr[SUBLANES - 1 + j:SUBLANES - 1 + j + q, :]
    act = _silu(u)

    dtr = dt_ref[...]
    dtr = jnp.where(d == 0, dtr, pltpu.roll(dtr, LANES - SSD_HEADS, 1))
    dtv = _softplus(dtr + dtb_ref[...])
    head_lane = lax.broadcasted_iota(jnp.int32, (1, LANES), 1) < SSD_HEADS
    dta = dtv * jnp.where(head_lane, -jnp.exp(alog_ref[...]), 0.0)
    ri = lax.broadcasted_iota(jnp.int32, (q, q), 0)
    ci = lax.broadcasted_iota(jnp.int32, (q, q), 1)
    keep = jnp.where(d == 0, ri - ci, ci - ri) >= 0
    cum = jnp.dot(keep.astype(F32), dta, preferred_element_type=F32, precision=HIGHEST)
    tot = jnp.sum(dta, axis=0, keepdims=True)
    cum_t = cum.T
    dt_t = dtv.T
    to_end = jnp.exp(tot - cum) * dtv
    e_cum = jnp.exp(cum)
    e_tot = jnp.exp(tot)
    dskip = dsk_ref[...] * jnp.where(d == 0, 1.0, 0.0)

    hpg = SSD_HEADS // SSD_GROUPS
    for gi in range(SSD_GROUPS):
        b_g = act[:, SSD_INNER + gi * SSD_STATE:SSD_INNER + (gi + 1) * SSD_STATE]
        c_lo = SSD_INNER + SSD_GROUPS * SSD_STATE + gi * SSD_STATE
        c_g = act[:, c_lo:c_lo + SSD_STATE].astype(BF16)
        cb = _nt_dot(c_g, b_g.astype(BF16))
        b_gt = b_g.T.astype(BF16)
        for hh in range(hpg):
            h = gi * hpg + hh
            xs = act[:, h * SSD_HEAD_DIM:(h + 1) * SSD_HEAD_DIM]
            seg = cum[:, h:h + 1] - cum_t[h:h + 1, :]
            decay = jnp.exp(jnp.where(keep, seg, NEG_BIG))
            w = (cb * decay * dt_t[h:h + 1, :]).astype(BF16)
            state = st_scr[h]
            y = jnp.dot(w, xs.astype(BF16), preferred_element_type=F32)
            y = y + jnp.dot(c_g, state.astype(BF16), preferred_element_type=F32) * e_cum[:, h:h + 1]
            y = y + dskip[:, h * SSD_HEAD_DIM:(h + 1) * SSD_HEAD_DIM] * xs
            o_ref[:, h * SSD_HEAD_DIM:(h + 1) * SSD_HEAD_DIM] = y
            s_new = jnp.dot(b_gt, (xs * to_end[:, h:h + 1]).astype(BF16), preferred_element_type=F32)
            st_scr[h] = e_tot[:, h:h + 1] * state + s_new


def _ssd(xbc, dt, conv_w, conv_b, dt_bias, a_log, d_skip, nct):
    b, l, cd = xbc.shape
    q = TIME_TILE
    nt = l // q
    tile = lambda d, bb, g: _seq_tile(d, g, nct, nt)
    r8 = q // SUBLANES
    pad = LANES - SSD_HEADS
    dtb = jnp.pad(dt_bias, ((0, 0), (0, pad))).reshape(2, 1, LANES)
    alog = jnp.pad(a_log, ((0, 0), (0, pad))).reshape(2, 1, LANES)
    dsk = jnp.repeat(d_skip, SSD_HEAD_DIM).reshape(1, SSD_INNER)
    return pl.pallas_call(
        functools.partial(_ssd_kernel, q=q, nct=nct, nt=nt),
        grid=(2, b, nt),
        in_specs=[
            pl.BlockSpec((None, q, cd), lambda d, bb, g: (bb, tile(d, bb, g), 0)),
            pl.BlockSpec((None, SUBLANES, cd),
                         lambda d, bb, g: (bb, jnp.maximum(tile(d, bb, g) * r8 - 1, 0), 0)),
            pl.BlockSpec((None, SUBLANES, cd),
                         lambda d, bb, g: (bb, jnp.minimum((tile(d, bb, g) + 1) * r8, l // SUBLANES - 1), 0)),
            pl.BlockSpec((None, q, LANES), lambda d, bb, g: (bb, tile(d, bb, g), 0)),
            pl.BlockSpec((CONV_W, 1, cd), lambda d, bb, g: (0, 0, 0)),
            pl.BlockSpec((1, cd), lambda d, bb, g: (0, 0)),
            pl.BlockSpec((None, 1, LANES), lambda d, bb, g: (d, 0, 0)),
            pl.BlockSpec((None, 1, LANES), lambda d, bb, g: (d, 0, 0)),
            pl.BlockSpec((1, SSD_INNER), lambda d, bb, g: (0, 0)),
        ],
        out_specs=pl.BlockSpec((None, None, q, SSD_INNER), lambda d, bb, g: (d, bb, tile(d, bb, g), 0)),
        out_shape=jax.ShapeDtypeStruct((2, b, l, SSD_INNER), F32),
        scratch_shapes=[
            pltpu.VMEM((q + 2 * SUBLANES, cd), F32),
            pltpu.VMEM((SSD_HEADS, SSD_STATE, SSD_HEAD_DIM), F32),
        ],
        compiler_params=_cparams(("arbitrary", "arbitrary", "arbitrary")),
        name="ssd_chunked",
    )(xbc, xbc, xbc, dt, conv_w.reshape(CONV_W, 1, cd), conv_b.reshape(1, cd), dtb, alog, dsk)


def _out_odd_kernel(x_ref, diff_ref, y_ref, z_ref, ng_ref, wa_ref, wb_ref, g1_ref, o_ref):
    yz = (y_ref[0] + y_ref[1]) * _silu(z_ref[...])
    gs = SSD_INNER // SSD_GROUPS
    parts = []
    for gi in range(SSD_GROUPS):
        seg = yz[:, gi * gs:(gi + 1) * gs]
        ms = jnp.mean(seg * seg, axis=-1, keepdims=True)
        parts.append(seg * lax.rsqrt(ms + NORM_EPS) * ng_ref[:, gi * gs:(gi + 1) * gs])
    ssd = jnp.concatenate(parts, axis=-1).astype(BF16)
    y = (jnp.dot(diff_ref[...], wa_ref[...], preferred_element_type=F32)
         + jnp.dot(ssd, wb_ref[...], preferred_element_type=F32))
    o_ref[...] = x_ref[...] + g1_ref[...] * y


def _out_odd(xc, diff, y2, z, norm_g, w_a, w_b, mods3, layer, c_len):
    b, l, d = xc.shape
    s_len = l - c_len
    tm = ROW_TILE
    off = c_len // tm
    w = SSD_INNER
    row = lambda bb, i: bb
    return pl.pallas_call(
        _out_odd_kernel,
        grid=(b, s_len // tm),
        in_specs=[
            pl.BlockSpec((None, tm, d), lambda bb, i: (bb, i + off, 0)),
            pl.BlockSpec((None, tm, diff.shape[-1]), lambda bb, i: (bb, i, 0)),
            pl.BlockSpec((2, None, tm, w), lambda bb, i: (0, bb, i + off, 0)),
            pl.BlockSpec((None, tm, w), lambda bb, i: (bb, i + off, 0)),
            pl.BlockSpec((1, w), lambda bb, i: (0, 0)),
            pl.BlockSpec(w_a.shape, lambda bb, i: (0, 0)),
            pl.BlockSpec(w_b.shape, lambda bb, i: (0, 0)),
            _mod_spec(d, layer, 2, row),
        ],
        out_specs=pl.BlockSpec((None, tm, d), lambda bb, i: (bb, i, 0)),
        out_shape=jax.ShapeDtypeStruct((b, s_len, d), F32),
        compiler_params=_cparams(("arbitrary", "arbitrary")),
        name="out_proj_odd",
    )(xc, diff, y2, z, norm_g.reshape(1, w), w_a, w_b, mods3)


def _router_kernel(x_ref, g_ref, sh_ref, sc_ref, wr_ref, br_ref, h_ref, eid_ref, rnk_ref, gate_ref, cnt_ref,
                   carry_scr):
    @pl.when((pl.program_id(0) == 0) & (pl.program_id(1) == 0))
    def _():
        carry_scr[...] = jnp.zeros_like(carry_scr)

    h = _norm_mod(x_ref[...], g_ref[...], sh_ref[...], sc_ref[...])
    _store_chunk_rows(h_ref, _pack_bf16(h))
    tm = h.shape[0]
    per = N_EXPERTS // N_EXPERT_GROUPS
    logits = lax.dot_general(wr_ref[...], h, (((1,), (1,)), ((), ())),
                             preferred_element_type=F32, precision=HIGHEST)
    scores = jax.nn.sigmoid(logits)
    sel = scores + br_ref[...]
    sel3 = sel.reshape(N_EXPERT_GROUPS, per, tm)
    kio = lax.broadcasted_iota(jnp.int32, sel3.shape, 1)
    m1 = jnp.max(sel3, axis=1, keepdims=True)
    first = jnp.min(jnp.where(sel3 == m1, kio, per), axis=1, keepdims=True)
    m2 = jnp.max(jnp.where(kio == first, NEG_BIG, sel3), axis=1, keepdims=True)
    gs = m1 + m2
    gio = lax.broadcasted_iota(jnp.int32, gs.shape, 0)
    ahead = jnp.zeros(gs.shape, jnp.int32)
    for gp in range(N_EXPERT_GROUPS):
        other = gs[gp:gp + 1]
        ahead = ahead + jnp.where((other > gs) | ((other == gs) & (gp < gio)), 1, 0)
    grp_on = jnp.where(ahead < TOPK_GROUPS, 1.0, 0.0)
    selm = jnp.where(jnp.broadcast_to(grp_on, sel3.shape) > 0.5, sel3, NEG_BIG).reshape(N_EXPERTS, tm)
    eio = lax.broadcasted_iota(jnp.int32, selm.shape, 0)
    work = selm
    cf = jnp.zeros(selm.shape, F32)
    e_rows, s_rows = [], []
    for k in range(TOP_K):
        best = jnp.max(work, axis=0, keepdims=True)
        idx = jnp.min(jnp.where(work == best, eio, N_EXPERTS), axis=0, keepdims=True)
        hit = eio == idx
        cf = cf + jnp.where(hit, 1.0, 0.0)
        work = jnp.where(hit, NEG_BIG, work)
        e_rows.append(idx)
        s_rows.append(jnp.sum(jnp.where(hit, scores, 0.0), axis=0, keepdims=True))
    denom = s_rows[0]
    for s_k in s_rows[1:]:
        denom = denom + s_k
    g_rows = [s_k / denom * ROUTED_SCALE for s_k in s_rows]
    ti = lax.broadcasted_iota(jnp.int32, (tm, tm), 0)
    tj = lax.broadcasted_iota(jnp.int32, (tm, tm), 1)
    before = jnp.where(ti < tj, 1.0, 0.0).astype(BF16)
    in_expert = carry_scr[:, 0:1] + jnp.dot(cf.astype(BF16), before, preferred_element_type=F32)
    carry_scr[...] = carry_scr[...] + jnp.sum(cf, axis=1, keepdims=True)
    cnt_ref[...] = carry_scr[...]
    r_rows = [jnp.sum(jnp.where(eio == idx, in_expert, 0.0), axis=0, keepdims=True) for idx in e_rows]
    eid_ref[...] = jnp.concatenate(e_rows, axis=0)
    rnk_ref[...] = jnp.concatenate(r_rows, axis=0).astype(jnp.int32)
    padded = jnp.concatenate(g_rows + [jnp.zeros((LANES - TOP_K, tm), F32)], axis=0)
    gate_ref[...] = padded.T


def _router(x, g, mods3, layer, row_fn, w_router_t, b_router):
    b, r, d = x.shape
    tm = ROW_TILE
    nt = r // tm
    slot = pl.BlockSpec((TOP_K, tm), lambda bb, i: (0, bb * nt + i))
    slot_shape = jax.ShapeDtypeStruct((TOP_K, b * r), jnp.int32)
    return pl.pallas_call(
        _router_kernel,
        grid=(b, nt),
        in_specs=[
            pl.BlockSpec((None, tm, d), lambda bb, i: (bb, i, 0)),
            pl.BlockSpec((1, d), lambda bb, i: (0, 0)),
            _mod_spec(d, layer, 3, row_fn),
            _mod_spec(d, layer, 4, row_fn),
            pl.BlockSpec(w_router_t.shape, lambda bb, i: (0, 0)),
            pl.BlockSpec((N_EXPERTS, 1), lambda bb, i: (0, 0)),
        ],
        out_specs=[
            pl.BlockSpec((tm * ROW_CHUNKS, LANES), lambda bb, i: (bb * nt + i, 0)),
            slot,
            slot,
            pl.BlockSpec((None, tm, LANES), lambda bb, i: (bb, i, 0)),
            pl.BlockSpec((N_EXPERTS, LANES), lambda bb, i: (0, 0)),
        ],
        out_shape=[jax.ShapeDtypeStruct((b * r * ROW_CHUNKS, LANES), jnp.uint32), slot_shape, slot_shape,
                   jax.ShapeDtypeStruct((b, r, LANES), F32), jax.ShapeDtypeStruct((N_EXPERTS, LANES), F32)],
        scratch_shapes=[pltpu.VMEM((N_EXPERTS, LANES), F32)],
        compiler_params=_cparams(("arbitrary", "arbitrary")),
        name="moe_router",
    )(x, g.reshape(1, d), mods3, mods3, w_router_t, b_router.reshape(N_EXPERTS, 1))


def _moe_plan(counts, n_rows):
    blk = EXPERT_BLK
    nb = n_rows // blk
    ends = jnp.cumsum(counts)
    starts = ends - counts
    count_le = lambda sorted_vals, q: jnp.sum(sorted_vals[None, :] <= q[:, None], axis=1, dtype=jnp.int32)
    first = jnp.arange(nb, dtype=jnp.int32) * blk
    e_lo = count_le(ends, first)
    e_hi = count_le(ends, first + (blk - 1))
    n_pair = e_hi - e_lo + 1
    p_end = jnp.cumsum(n_pair)
    p_start = p_end - n_pair
    i = jnp.arange(nb + N_EXPERTS - 1, dtype=jnp.int32)
    j = jnp.minimum(count_le(p_end, i), nb - 1)
    valid = i < p_end[-1]
    e = jnp.where(valid, e_lo[j] + i - p_start[j], e_hi[nb - 1]).astype(jnp.int32)
    bounds = jnp.concatenate([starts, ends[-1:]]).astype(jnp.int32)
    return j, e, valid.astype(jnp.int32), bounds


def _positions_kernel(starts_ref, eid_ref, rnk_ref, pos_ref):
    eid = eid_ref[...]
    pos = rnk_ref[...]
    for e in range(N_EXPERTS):
        pos = pos + jnp.where(eid == e, starts_ref[e], 0)
    pos_ref[...] = pos * ROW_CHUNKS


def _positions(eid, rnk, starts):
    full = pl.BlockSpec(eid.shape, lambda: (0, 0))
    return pl.pallas_call(
        _positions_kernel,
        in_specs=[pl.BlockSpec(memory_space=pltpu.SMEM), full, full],
        out_specs=full,
        out_shape=jax.ShapeDtypeStruct(eid.shape, jnp.int32),
        compiler_params=pltpu.CompilerParams(vmem_limit_bytes=VMEM_LIMIT),
        name="moe_positions",
    )(starts, eid, rnk)


def _token_row(ref, first):
    return ref.at[pl.ds(pl.multiple_of(first, ROW_CHUNKS), ROW_CHUNKS)]


def _dispatch_kernel(pos_ref, h_ref, xs_ref, sem):
    tm = h_ref.shape[0] // ROW_CHUNKS

    def issue(t, carry):
        src = _token_row(h_ref, t * ROW_CHUNKS)
        for k in range(TOP_K):
            pltpu.make_async_copy(src, _token_row(xs_ref, pos_ref[k, t]), sem).start(priority=k % 2)
        return carry

    lax.fori_loop(0, tm, issue, 0)
    done = pl.ds(0, tm * ROW_CHUNKS)
    for _ in range(TOP_K):
        pltpu.make_async_copy(h_ref.at[done], xs_ref.at[done], sem).wait()


def _dispatch(h2, pos):
    rows, w = h2.shape
    tm = ROW_TILE
    return pl.pallas_call(
        _dispatch_kernel,
        grid=(rows // (tm * ROW_CHUNKS),),
        in_specs=[
            pl.BlockSpec((TOP_K, tm), lambda i: (0, i), memory_space=pltpu.SMEM),
            pl.BlockSpec((tm * ROW_CHUNKS, w), lambda i: (i, 0)),
        ],
        out_specs=pl.BlockSpec(memory_space=pl.ANY),
        out_shape=jax.ShapeDtypeStruct((rows * TOP_K, w), h2.dtype),
        scratch_shapes=[pltpu.SemaphoreType.DMA],
        compiler_params=_cparams(("arbitrary",)),
        name="moe_dispatch",
    )(pos, h2)


def _grouped_kernel(pb_ref, pe_ref, pv_ref, bnd_ref, xs_ref, wg_ref, wu_ref, wd_ref, y_ref, wgb, wub, wdb):
    i = pl.program_id(0)
    prev = jnp.maximum(i - 1, 0)
    j = pb_ref[i]
    e = pe_ref[i]
    blk = xs_ref.shape[0] // ROW_CHUNKS

    @pl.when((i == 0) | (pb_ref[prev] != j))
    def _():
        y_ref[...] = jnp.zeros_like(y_ref)

    @pl.when((i == 0) | (pe_ref[prev] != e))
    def _():
        wgb[...] = wg_ref[...].astype(BF16)
        wub[...] = wu_ref[...].astype(BF16)
        wdb[...] = wd_ref[...].astype(BF16)

    @pl.when(pv_ref[i] == 1)
    def _():
        xw = _load_chunk_rows(xs_ref, blk)
        a = _packed_dot(xw, wgb)
        u = _packed_dot(xw, wub)
        yv = jnp.dot((_silu(a) * u).astype(BF16), wdb[...], preferred_element_type=F32)
        rows = j * blk + lax.broadcasted_iota(jnp.int32, (blk, 1), 0)
        own = (rows >= bnd_ref[e]) & (rows < bnd_ref[e + 1])
        yw = _pack_bf16(yv)
        for c in range(ROW_CHUNKS):
            sl = pl.ds(c, blk, stride=ROW_CHUNKS)
            y_ref[sl, :] = jnp.where(own, yw[:, c * LANES:(c + 1) * LANES], y_ref[sl, :])


def _grouped(pb, pe, pv, bounds, xs, wg, wu, wd, layer):
    p, half = xs.shape
    d = 2 * ROW_CHUNKS * LANES
    blk = EXPERT_BLK * ROW_CHUNKS
    grid_spec = pltpu.PrefetchScalarGridSpec(
        num_scalar_prefetch=4,
        grid=(pb.shape[0],),
        in_specs=[
            pl.BlockSpec((blk, half), lambda i, pb, pe, pv, bnd: (pb[i], 0)),
            pl.BlockSpec((None, None, d, D_EXPERT), lambda i, pb, pe, pv, bnd: (layer, pe[i], 0, 0)),
            pl.BlockSpec((None, None, d, D_EXPERT), lambda i, pb, pe, pv, bnd: (layer, pe[i], 0, 0)),
            pl.BlockSpec((None, None, D_EXPERT, d), lambda i, pb, pe, pv, bnd: (layer, pe[i], 0, 0)),
        ],
        out_specs=pl.BlockSpec((blk, half), lambda i, pb, pe, pv, bnd: (pb[i], 0)),
        scratch_shapes=[
            pltpu.VMEM((d, D_EXPERT), BF16),
            pltpu.VMEM((d, D_EXPERT), BF16),
            pltpu.VMEM((D_EXPERT, d), BF16),
        ],
    )
    return pl.pallas_call(
        _grouped_kernel,
        grid_spec=grid_spec,
        out_shape=jax.ShapeDtypeStruct((p, half), jnp.uint32),
        compiler_params=_cparams(("arbitrary",)),
        name="moe_grouped_experts",
    )(pb, pe, pv, bounds, xs, wg, wu, wd)


def _combine_kernel(*refs, final):
    pos_ref, y_ref, gate_ref, h_ref, x_ref, g2_ref, sg_ref, su_ref, sd_ref = refs[:9]
    o_ref, buf, sems = refs[-3:]
    tm = x_ref.shape[0]
    half_rows = tm // 2

    for part in range(2):
        def issue(t, carry, part=part):
            for k in range(TOP_K):
                pltpu.make_async_copy(_token_row(y_ref, pos_ref[k, t]), _token_row(buf.at[k], t * ROW_CHUNKS),
                                      sems.at[part]).start(priority=k % 2)
            return carry
        lax.fori_loop(part * half_rows, (part + 1) * half_rows, issue, 0)

    hw = _load_chunk_rows(h_ref, tm)
    a = _packed_dot(hw, sg_ref)
    u = _packed_dot(hw, su_ref)
    shared = jnp.dot((_silu(a) * u).astype(BF16), sd_ref[...], preferred_element_type=F32)
    half = hw.shape[-1]
    for part in range(2):
        r0 = part * half_rows
        rows = pl.ds(r0 * ROW_CHUNKS, half_rows * ROW_CHUNKS)
        for k in range(TOP_K):
            pltpu.make_async_copy(y_ref.at[rows], buf.at[k, rows], sems.at[part]).wait()
        g = gate_ref[r0:r0 + half_rows, :]
        acc_hi = shared[r0:r0 + half_rows, :half]
        acc_lo = shared[r0:r0 + half_rows, half:]
        for k in range(TOP_K):
            hi, lo = _unpack_bf16(_load_chunk_rows(buf.at[k, rows], half_rows))
            acc_hi = acc_hi + g[:, k:k + 1] * hi
            acc_lo = acc_lo + g[:, k:k + 1] * lo
        x = x_ref[r0:r0 + half_rows, :] + g2_ref[...] * jnp.concatenate([acc_hi, acc_lo], axis=-1)
        if final:
            gf_ref = refs[9]
            ms = jnp.mean(x * x, axis=-1, keepdims=True)
            x = x * lax.rsqrt(ms + NORM_EPS) * gf_ref[...]
        o_ref[r0:r0 + half_rows, :] = x


def _combine(pos, y, gates, h2, x, mods3, layer, row_fn, sg, su, sd, g_final=None):
    b, r, d = x.shape
    tm = ROW_TILE
    nt = r // tm
    tile = pl.BlockSpec((None, tm, d), lambda bb, i: (bb, i, 0))
    in_specs = [
        pl.BlockSpec((TOP_K, tm), lambda bb, i: (0, bb * nt + i), memory_space=pltpu.SMEM),
        pl.BlockSpec(memory_space=pl.ANY),
        pl.BlockSpec((None, tm, LANES), lambda bb, i: (bb, i, 0)),
        pl.BlockSpec((tm * ROW_CHUNKS, LANES), lambda bb, i: (bb * nt + i, 0)),
        tile,
        _mod_spec(d, layer, 5, row_fn),
        pl.BlockSpec(sg.shape, lambda bb, i: (0, 0)),
        pl.BlockSpec(su.shape, lambda bb, i: (0, 0)),
        pl.BlockSpec(sd.shape, lambda bb, i: (0, 0)),
    ]
    args = [pos, y, gates, h2, x, mods3, sg, su, sd]
    if g_final is not None:
        in_specs.append(pl.BlockSpec((1, d), lambda bb, i: (0, 0)))
        args.append(g_final.reshape(1, d))
    return pl.pallas_call(
        functools.partial(_combine_kernel, final=g_final is not None),
        grid=(b, nt),
        in_specs=in_specs,
        out_specs=tile,
        out_shape=jax.ShapeDtypeStruct((b, r, d), F32),
        scratch_shapes=[pltpu.VMEM((TOP_K, tm * ROW_CHUNKS, LANES), jnp.uint32), pltpu.SemaphoreType.DMA((2,))],
        compiler_params=_cparams(("arbitrary", "arbitrary")),
        name="moe_combine",
    )(*args)


def _moe(x, g_ffn, mods3, layer, row_fn, w_router, b_router, w_e_gate, w_e_up, w_e_down,
         ws_gate, ws_up, ws_down, g_final=None):
    b, r, d = x.shape
    h2, eid, rnk, gates, cnt = _router(x, g_ffn, mods3, layer, row_fn, w_router.T, b_router)
    pb, pe, pv, bounds = _moe_plan(cnt[:, 0].astype(jnp.int32), b * r * TOP_K)
    pos = _positions(eid, rnk, bounds[:N_EXPERTS])
    xs = _dispatch(h2, pos)
    y = _grouped(pb, pe, pv, bounds, xs, w_e_gate, w_e_up, w_e_down, layer)
    return _combine(pos, y, gates, h2, x, mods3, layer, row_fn,
                    ws_gate.astype(BF16), ws_up.astype(BF16), ws_down.astype(BF16), g_final)


def kernel(x, c, ctx, c_ctx, w_mod, b_mod, g_mix, g_ffn, g_final, ab_w_in, ab_w_out, ab_conv_w, ab_conv_b, ab_w_r, ab_b_r, ab_w_i, ab_b_i, ab_lam, ab_sink, cd_w_in, cd_w_out, cd_lam, cd_subln_g, cd_conv_w, cd_conv_b, cd_dt_bias, cd_a_log, cd_d_skip, cd_norm_g, w_router, b_router, w_e_gate, w_e_up, w_e_down, ws_gate, ws_up, ws_down):
    bsz, s_len, d = x.shape
    c_len = ctx.shape[1]
    depth = w_mod.shape[0]
    assert depth == 2 and bsz == SUBLANES, "kernels are specialised to depth 2 and batch 8"
    assert c_len % ROW_TILE == 0 and s_len % ROW_TILE == 0
    nct_row = c_len // ROW_TILE
    nct_time = c_len // TIME_TILE

    c_all = jnp.concatenate([c, c_ctx[None], jnp.zeros((MOD_ROWS - bsz - 1, d), F32)], axis=0)
    mods3 = _modulations(c_all, w_mod, b_mod).reshape(depth * MOD_ROWS, 1, N_MOD * d)
    rope_tabs = _rope_tables(c_len, s_len)
    xc = jnp.concatenate([ctx, x], axis=1)
    row_mixed = lambda bb, i: jnp.where(i < nct_row, SUBLANES, bb)
    row_latent = lambda bb, i: bb

    w_in = ab_w_in[0].astype(BF16)
    q_hi = LRU_WIDTH + WIN_HEADS * HEAD_DIM
    x_hi = q_hi + LRU_WIDTH
    k_hi = x_hi + WIN_KV_HEADS * HEAD_DIM
    gate, q, xa, k, v = _project(xc, g_mix[0], mods3, 0, nct_row, rope_tabs, [
        (w_in[:, :LRU_WIDTH], None, F32, False),
        (w_in[:, LRU_WIDTH:q_hi], HEAD_DIM ** -0.5 * LOG2E, BF16, False),
        (w_in[:, q_hi:x_hi], None, F32, True),
        (w_in[:, x_hi:k_hi], 1.0, BF16, False),
        (w_in[:, k_hi:], None, BF16, False),
    ])
    l_len = c_len + s_len
    w_gates = jnp.stack([jnp.concatenate([_block_diag(ab_w_r[0, dd]), _block_diag(ab_w_i[0, dd])], axis=1)
                         for dd in range(2)]).astype(BF16)
    b_gates = jnp.concatenate([ab_b_r[0], ab_b_i[0]], axis=-1).reshape(2, 1, 2 * LRU_WIDTH)
    rec = _rglru(xa.reshape(l_len, bsz, LRU_WIDTH), ab_conv_w[0], ab_conv_b[0], w_gates, b_gates,
                 ab_lam[0].reshape(2, 1, LRU_WIDTH), nct_time)
    att = _win_attention(q, k, v, ab_sink[0], c_len)
    w_out = ab_w_out[0].astype(BF16)
    xc = _out_even(xc, rec.reshape(2, l_len, bsz * LRU_WIDTH), gate, att, w_out[:LRU_WIDTH], w_out[LRU_WIDTH:],
                   mods3, 0, nct_row)
    xc = _moe(xc, g_ffn[0], mods3, 0, row_mixed, w_router[0], b_router[0], w_e_gate, w_e_up, w_e_down,
              ws_gate[0], ws_up[0], ws_down[0])

    w_in = cd_w_in[0].astype(BF16)
    qk = DIFF_HEADS * 2 * DIFF_DH
    z_hi = qk + SSD_INNER
    k_hi = z_hi + qk
    v_hi = k_hi + qk
    x_hi = v_hi + SSD_CONV_DIM
    w_dt = jnp.pad(w_in[:, x_hi:], ((0, 0), (0, LANES - 2 * SSD_HEADS)))
    q, z, k, v, xbc, dt = _project(xc, g_mix[1], mods3, 1, nct_row, rope_tabs, [
        (w_in[:, :qk], DIFF_DH ** -0.5 * LOG2E, BF16, False),
        (w_in[:, qk:z_hi], None, F32, False),
        (w_in[:, z_hi:k_hi], 1.0, BF16, False),
        (w_in[:, k_hi:v_hi], None, BF16, False),
        (w_in[:, v_hi:x_hi], None, F32, False),
        (w_dt, None, F32, False),
    ])
    lam_init = 0.8 - 0.6 * math.exp(-0.3 * 1)
    diff = _diff_attention(q, k, v, cd_lam[0], cd_subln_g[0], lam_init, c_len)
    y2 = _ssd(xbc, dt, cd_conv_w[0], cd_conv_b[0], cd_dt_bias[0], cd_a_log[0], cd_d_skip[0], nct_time)
    w_out = cd_w_out[0].astype(BF16)
    xl = _out_odd(xc, diff, y2, z, cd_norm_g[0], w_out[:qk], w_out[qk:], mods3, 1, c_len)
    return _moe(xl, g_ffn[1], mods3, 1, row_latent, w_router[1], b_router[1], w_e_gate, w_e_up, w_e_down,
                ws_gate[1], ws_up[1], ws_down[1], g_final=g_final)
```

```python
import functools
import math

import jax
import jax.numpy as jnp
from jax import lax
from jax.experimental import pallas as pl
from jax.experimental.pallas import tpu as pltpu

F32 = jnp.float32
BF16 = jnp.bfloat16
HIGHEST = lax.Precision.HIGHEST

GRID_W = 64
N_MOD = 6
NORM_EPS = 1e-6
ROPE_BASE = 10000.0
CONV_W = 4

LRU_WIDTH = 512
LRU_BLOCKS = 8
LRU_C = 8.0

HEAD_DIM = 64
WIN_HEADS = 8
WIN_KV_HEADS = 2
WINDOW = 128

DIFF_HEADS = 4
DIFF_DH = 64

SSD_HEADS = 8
SSD_HEAD_DIM = 64
SSD_INNER = SSD_HEADS * SSD_HEAD_DIM
SSD_GROUPS = 2
SSD_STATE = 128
SSD_CONV_DIM = SSD_INNER + 2 * SSD_GROUPS * SSD_STATE

N_EXPERTS = 64
N_EXPERT_GROUPS = 8
TOPK_GROUPS = 4
TOP_K = 8
D_EXPERT = 256
ROUTED_SCALE = 2.5

LANES = 128
SUBLANES = 8
MOD_ROWS = 16
TIME_TILE = 128
ROW_TILE = 256
EXPERT_BLK = 1024
ROW_CHUNKS = 4
VMEM_LIMIT = 48 * 1024 * 1024
NEG_BIG = -1e30
LOG2E = math.log2(math.e)


def _cparams(sem):
    return pltpu.CompilerParams(dimension_semantics=sem, vmem_limit_bytes=VMEM_LIMIT)


def _nt_dot(a, b):
    return lax.dot_general(a, b, (((1,), (1,)), ((), ())), preferred_element_type=F32)


def _softplus(x):
    return jnp.maximum(x, 0.0) + jnp.log1p(jnp.exp(-jnp.abs(x)))


def _silu(x):
    return x * jax.nn.sigmoid(x)


def _pack_bf16(x):
    half = x.shape[-1] // 2
    bits = pltpu.bitcast(x.astype(BF16).astype(F32), jnp.uint32)
    return bits[:, :half] | (bits[:, half:] >> 16)


def _unpack_bf16(w):
    hi = pltpu.bitcast(w & jnp.uint32(0xFFFF0000), F32)
    lo = pltpu.bitcast(w << 16, F32)
    return hi, lo


def _store_chunk_rows(ref, w):
    n = w.shape[0]
    for j in range(ROW_CHUNKS):
        ref[pl.ds(j, n, stride=ROW_CHUNKS), :] = w[:, j * LANES:(j + 1) * LANES]


def _load_chunk_rows(ref, n):
    return jnp.concatenate([ref[pl.ds(j, n, stride=ROW_CHUNKS), :] for j in range(ROW_CHUNKS)], axis=1)


def _packed_dot(w, weight_ref):
    half = w.shape[-1]
    hi, lo = _unpack_bf16(w)
    return (jnp.dot(hi.astype(BF16), weight_ref[:half, :], preferred_element_type=F32)
            + jnp.dot(lo.astype(BF16), weight_ref[half:, :], preferred_element_type=F32))


def _mod_kernel(c_ref, w_ref, b_ref, o_ref):
    c = c_ref[...]
    s = _silu(c)
    o_ref[...] = jnp.dot(s, w_ref[...], preferred_element_type=F32, precision=HIGHEST) + b_ref[...]


def _modulations(c_all, w_mod, b_mod):
    depth, d, _ = w_mod.shape
    return pl.pallas_call(
        _mod_kernel,
        grid=(depth, N_MOD),
        in_specs=[
            pl.BlockSpec((MOD_ROWS, d), lambda l, k: (0, 0)),
            pl.BlockSpec((None, d, d), lambda l, k: (l, 0, k)),
            pl.BlockSpec((None, 1, d), lambda l, k: (l, 0, k)),
        ],
        out_specs=pl.BlockSpec((None, MOD_ROWS, d), lambda l, k: (l, 0, k)),
        out_shape=jax.ShapeDtypeStruct((depth, MOD_ROWS, N_MOD * d), F32),
        compiler_params=_cparams(("arbitrary", "arbitrary")),
        name="adaln_modulation",
    )(c_all, w_mod, b_mod.reshape(depth, 1, N_MOD * d))


def _mod_spec(d, layer, chunk, row_fn):
    return pl.BlockSpec((None, 1, d), lambda b, i: (layer * MOD_ROWS + row_fn(b, i), 0, chunk))


def _norm_mod(x, g, sh, sc):
    ms = jnp.mean(x * x, axis=-1, keepdims=True)
    return (x * lax.rsqrt(ms + NORM_EPS) * g) * (1.0 + sc) + sh


def _rope(y, cos, sa, sb):
    n = y.shape[-1]
    half = HEAD_DIM // 2
    return y * cos + pltpu.roll(y, n - half, 1) * sa + pltpu.roll(y, half, 1) * sb


def _proj_kernel(*refs, ropes):
    n = len(ropes)
    x_ref, g_ref, sh_ref, sc_ref, cos_ref, sa_ref, sb_ref = refs[:7]
    w_refs = refs[7:7 + n]
    o_refs = refs[7 + n:]
    h = _norm_mod(x_ref[...], g_ref[...], sh_ref[...], sc_ref[...]).astype(BF16)
    for w_ref, o_ref, rope in zip(w_refs, o_refs, ropes):
        y = jnp.dot(h, w_ref[...], preferred_element_type=F32)
        if rope is not None:
            w = y.shape[-1]
            y = _rope(y, cos_ref[:, :w], sa_ref[:, :w], sb_ref[:, :w])
            if rope != 1.0:
                y = y * rope
        o_ref[...] = y.astype(o_ref.dtype)


def _project(xc, g, mods3, layer, nct, rope_tabs, groups):
    b, l, d = xc.shape
    tm = ROW_TILE
    mod_row = lambda i, bb: layer * MOD_ROWS + jnp.where(i < nct, SUBLANES, bb)
    rw = rope_tabs[0].shape[-1]
    in_specs = [
        pl.BlockSpec((None, tm, d), lambda i, bb: (bb, i, 0)),
        pl.BlockSpec((1, d), lambda i, bb: (0, 0)),
        pl.BlockSpec((None, 1, d), lambda i, bb: (mod_row(i, bb), 0, 0)),
        pl.BlockSpec((None, 1, d), lambda i, bb: (mod_row(i, bb), 0, 1)),
    ] + [pl.BlockSpec((tm, rw), lambda i, bb: (i, 0))] * 3
    out_specs, out_shapes = [], []
    for w, _, dt, time_major in groups:
        n = w.shape[1]
        in_specs.append(pl.BlockSpec((d, n), lambda i, bb: (0, 0)))
        if time_major:
            out_specs.append(pl.BlockSpec((tm, n), lambda i, bb: (i, bb)))
            out_shapes.append(jax.ShapeDtypeStruct((l, b * n), dt))
        else:
            out_specs.append(pl.BlockSpec((None, tm, n), lambda i, bb: (bb, i, 0)))
            out_shapes.append(jax.ShapeDtypeStruct((b, l, n), dt))
    return pl.pallas_call(
        functools.partial(_proj_kernel, ropes=tuple(gp[1] for gp in groups)),
        grid=(l // tm, b),
        in_specs=in_specs,
        out_specs=out_specs,
        out_shape=out_shapes,
        compiler_params=_cparams(("arbitrary", "arbitrary")),
        name="norm_mod_project",
    )(xc, g.reshape(1, d), mods3, mods3, *rope_tabs, *[gp[0] for gp in groups])


def _rope_tables(c_len, s_len):
    rows = s_len // GRID_W
    row = jnp.repeat(jnp.arange(rows), GRID_W).astype(F32)
    col = jnp.tile(jnp.arange(GRID_W), rows).astype(F32)
    n = HEAD_DIM // 4
    inv = ROPE_BASE ** (-jnp.arange(n, dtype=F32) / n)
    ang = jnp.concatenate([row[:, None] * inv, col[:, None] * inv], axis=-1)
    cos, sin = jnp.cos(ang), jnp.sin(ang)
    zero = jnp.zeros_like(sin)
    reps = WIN_HEADS
    cos_t = jnp.tile(jnp.concatenate([cos, cos], axis=-1), (1, reps))
    sa_t = jnp.tile(jnp.concatenate([-sin, zero], axis=-1), (1, reps))
    sb_t = jnp.tile(jnp.concatenate([zero, sin], axis=-1), (1, reps))
    w = cos_t.shape[-1]
    pad1 = jnp.ones((c_len, w), F32)
    pad0 = jnp.zeros((c_len, w), F32)
    return (jnp.concatenate([pad1, cos_t], 0), jnp.concatenate([pad0, sa_t], 0),
            jnp.concatenate([pad0, sb_t], 0))


def _seq_tile(d, g, nct, nt):
    rev = jnp.where(g < nct, nct - 1 - g, nt - 1 - (g - nct))
    return jnp.where(d == 0, g, rev)


def _rglru_kernel(x_ref, xp_ref, xn_ref, cw_ref, cb_ref, w_ref, bias_ref, lam_ref, o_ref,
                  ext_scr, a_scr, b_scr, h_scr, *, ts, nct, nt, sub):
    d = pl.program_id(0)
    g = pl.program_id(1)
    tile = _seq_tile(d, g, nct, nt)
    bsz, width = h_scr.shape
    pv = jnp.where((tile == 0) | (tile == nct), 0.0, 1.0)
    nv = jnp.where((tile == nct - 1) | (tile == nt - 1), 0.0, 1.0)
    ext_scr[0:1] = xp_ref[...] * pv
    ext_scr[1:ts + 1] = x_ref[...]
    ext_scr[ts + 1:ts + 3] = xn_ref[...] * nv

    @pl.when(g == 0)
    def _():
        h_scr[...] = jnp.zeros_like(h_scr)

    neg_sp = -LRU_C * _softplus(-lam_ref[...])

    def prep(c, carry):
        r0 = pl.multiple_of(c * sub, sub)
        e = ext_scr[pl.ds(r0, sub + CONV_W - 1)]
        u = cb_ref[...] + cw_ref[0] * e[0:sub]
        for j in range(1, CONV_W):
            u = u + cw_ref[j] * e[j:j + sub]
        u2 = u.reshape(sub * bsz, width)
        gts = jnp.dot(u2.astype(BF16), w_ref[...], preferred_element_type=F32) + bias_ref[...]
        r = jax.nn.sigmoid(gts[:, :width])
        ig = jax.nn.sigmoid(gts[:, width:])
        log_a = neg_sp * r
        a = jnp.exp(log_a)
        mult = jnp.sqrt(1.0 - a * a)
        a_scr[pl.ds(r0, sub)] = a.reshape(sub, bsz, width)
        b_scr[pl.ds(r0, sub)] = (mult * ig * u2).reshape(sub, bsz, width)
        return carry

    lax.fori_loop(0, ts // sub, prep, 0)

    def step(t, h):
        tt = jnp.where(d == 0, t, ts - 1 - t)
        h = a_scr[tt] * h + b_scr[tt]
        o_ref[tt] = h
        return h

    h_scr[...] = lax.fori_loop(0, ts, step, h_scr[...], unroll=8)


def _rglru(xa_tm, conv_w, conv_b, w_gates, b_gates, lam, nct):
    l, bsz, width = xa_tm.shape
    ts = TIME_TILE
    nt = l // ts
    tile = lambda d, g: _seq_tile(d, g, nct, nt)
    kern = functools.partial(_rglru_kernel, ts=ts, nct=nct, nt=nt, sub=16)
    return pl.pallas_call(
        kern,
        grid=(2, nt),
        in_specs=[
            pl.BlockSpec((ts, bsz, width), lambda d, g: (tile(d, g), 0, 0)),
            pl.BlockSpec((1, bsz, width), lambda d, g: (jnp.maximum(tile(d, g) * ts - 1, 0), 0, 0)),
            pl.BlockSpec((2, bsz, width),
                         lambda d, g: (jnp.minimum((tile(d, g) + 1) * (ts // 2), l // 2 - 1), 0, 0)),
            pl.BlockSpec((CONV_W, 1, width), lambda d, g: (0, 0, 0)),
            pl.BlockSpec((1, width), lambda d, g: (0, 0)),
            pl.BlockSpec((None, width, 2 * width), lambda d, g: (d, 0, 0)),
            pl.BlockSpec((None, 1, 2 * width), lambda d, g: (d, 0, 0)),
            pl.BlockSpec((None, 1, width), lambda d, g: (d, 0, 0)),
        ],
        out_specs=pl.BlockSpec((None, ts, bsz, width), lambda d, g: (d, tile(d, g), 0, 0)),
        out_shape=jax.ShapeDtypeStruct((2, l, bsz, width), F32),
        scratch_shapes=[
            pltpu.VMEM((ts + CONV_W - 1, bsz, width), F32),
            pltpu.VMEM((ts, bsz, width), F32),
            pltpu.VMEM((ts, bsz, width), F32),
            pltpu.VMEM((bsz, width), F32),
        ],
        compiler_params=_cparams(("arbitrary", "arbitrary")),
        name="rglru_scan",
    )(xa_tm, xa_tm, xa_tm, conv_w.reshape(CONV_W, 1, width), conv_b.reshape(1, width),
      w_gates, b_gates, lam)


def _block_diag(w):
    nb, c, dd = w.shape
    eye = jnp.eye(nb, dtype=w.dtype)
    return (eye[:, None, :, None] * w[:, :, None, :]).reshape(nb * c, nb * dd)


def _win_attn_kernel(sink_ref, q_ref, k_ref, v_ref, o_ref, *, c_len, l_len, nqc):
    j = pl.program_id(1)
    blk = q_ref.shape[0]
    grp = WIN_HEADS // WIN_KV_HEADS
    band = 3 * blk
    heads = [(h, slice(h * HEAD_DIM, (h + 1) * HEAD_DIM),
              slice((h // grp) * HEAD_DIM, (h // grp + 1) * HEAD_DIM)) for h in range(WIN_HEADS)]

    @pl.when(j < nqc)
    def _():
        logits = [_nt_dot(q_ref[:, hsl], k_ref[0:c_len, ksl]) for _, hsl, ksl in heads]
        probs, dens = [], []
        for (h, _, _), s in zip(heads, logits):
            sink = sink_ref[h] * LOG2E
            m = jnp.maximum(jnp.max(s, axis=-1, keepdims=True), sink)
            p = jnp.exp2(s - m)
            dens.append(jnp.sum(p, axis=-1, keepdims=True) + jnp.exp2(sink - m))
            probs.append(p.astype(BF16))
        outs = [jnp.dot(p, v_ref[0:c_len, ksl], preferred_element_type=F32) / den
                for (_, _, ksl), p, den in zip(heads, probs, dens)]
        o_ref[...] = jnp.concatenate(outs, axis=-1).astype(o_ref.dtype)

    @pl.when(j >= nqc)
    def _():
        jb = j - nqc
        start = jnp.clip(c_len + (jb - 1) * blk, c_len - blk, l_len - band)
        start = pl.multiple_of(start, blk)
        qpos = jb * blk + lax.broadcasted_iota(jnp.int32, (blk, band), 0)
        kpos = start - c_len + lax.broadcasted_iota(jnp.int32, (blk, band), 1)
        valid = (jnp.abs(qpos - kpos) <= WINDOW) & (kpos >= 0)
        lc = [_nt_dot(q_ref[:, hsl], k_ref[0:c_len, ksl]) for _, hsl, ksl in heads]
        lb = [jnp.where(valid, _nt_dot(q_ref[:, hsl], k_ref[pl.ds(start, band), ksl]), NEG_BIG)
              for _, hsl, ksl in heads]
        pcs, pbs, dens = [], [], []
        for (h, _, _), sc, sb in zip(heads, lc, lb):
            sink = sink_ref[h] * LOG2E
            m = jnp.maximum(jnp.maximum(jnp.max(sc, axis=-1, keepdims=True),
                                        jnp.max(sb, axis=-1, keepdims=True)), sink)
            pc = jnp.exp2(sc - m)
            pb = jnp.exp2(sb - m)
            dens.append(jnp.sum(pc, axis=-1, keepdims=True) + jnp.sum(pb, axis=-1, keepdims=True)
                        + jnp.exp2(sink - m))
            pcs.append(pc.astype(BF16))
            pbs.append(pb.astype(BF16))
        outs = [(jnp.dot(pc, v_ref[0:c_len, ksl], preferred_element_type=F32)
                 + jnp.dot(pb, v_ref[pl.ds(start, band), ksl], preferred_element_type=F32)) / den
                for (_, _, ksl), pc, pb, den in zip(heads, pcs, pbs, dens)]
        o_ref[...] = jnp.concatenate(outs, axis=-1).astype(o_ref.dtype)


def _win_attention(q, k, v, sink, c_len):
    b, l, qw = q.shape
    kw = k.shape[-1]
    blk = TIME_TILE
    kern = functools.partial(_win_attn_kernel, c_len=c_len, l_len=l, nqc=c_len // blk)
    return pl.pallas_call(
        kern,
        grid=(b, l // blk),
        in_specs=[
            pl.BlockSpec(memory_space=pltpu.SMEM),
            pl.BlockSpec((None, blk, qw), lambda bb, j: (bb, j, 0)),
            pl.BlockSpec((None, l, kw), lambda bb, j: (bb, 0, 0)),
            pl.BlockSpec((None, l, kw), lambda bb, j: (bb, 0, 0)),
        ],
        out_specs=pl.BlockSpec((None, blk, qw), lambda bb, j: (bb, j, 0)),
        out_shape=jax.ShapeDtypeStruct((b, l, qw), BF16),
        compiler_params=_cparams(("arbitrary", "arbitrary")),
        name="window_attention",
    )(sink, q, k, v)


def _out_even_kernel(x_ref, rec_ref, gate_ref, att_ref, wa_ref, wb_ref, g1_ref, o_ref):
    lru = (rec_ref[0] + rec_ref[1]) * jax.nn.gelu(gate_ref[...])
    y = (jnp.dot(lru.astype(BF16), wa_ref[...], preferred_element_type=F32)
         + jnp.dot(att_ref[...], wb_ref[...], preferred_element_type=F32))
    o_ref[...] = x_ref[...] + g1_ref[...] * y


def _out_even(xc, rec2, gate, att, w_a, w_b, mods3, layer, nct):
    b, l, d = xc.shape
    tm = ROW_TILE
    w = gate.shape[-1]
    row = lambda bb, i: jnp.where(i < nct, SUBLANES, bb)
    return pl.pallas_call(
        _out_even_kernel,
        grid=(b, l // tm),
        in_specs=[
            pl.BlockSpec((None, tm, d), lambda bb, i: (bb, i, 0)),
            pl.BlockSpec((2, tm, w), lambda bb, i: (0, i, bb)),
            pl.BlockSpec((None, tm, w), lambda bb, i: (bb, i, 0)),
            pl.BlockSpec((None, tm, att.shape[-1]), lambda bb, i: (bb, i, 0)),
            pl.BlockSpec(w_a.shape, lambda bb, i: (0, 0)),
            pl.BlockSpec(w_b.shape, lambda bb, i: (0, 0)),
            _mod_spec(d, layer, 2, row),
        ],
        out_specs=pl.BlockSpec((None, tm, d), lambda bb, i: (bb, i, 0)),
        out_shape=jax.ShapeDtypeStruct((b, l, d), F32),
        compiler_params=_cparams(("arbitrary", "arbitrary")),
        name="out_proj_even",
    )(xc, rec2, gate, att, w_a, w_b, mods3)


def _diff_attn_kernel(lam_ref, g_ref, q_ref, k_ref, v_ref, o_ref, *, lam_init):
    lv = lam_ref[...]
    lam = (jnp.exp(jnp.sum(lv[0:1] * lv[1:2], axis=-1, keepdims=True))
           - jnp.exp(jnp.sum(lv[2:3] * lv[3:4], axis=-1, keepdims=True)) + lam_init)
    vw = 2 * DIFF_DH

    def logits(h, mp):
        lo = h * vw + mp * DIFF_DH
        return _nt_dot(q_ref[:, lo:lo + DIFF_DH], k_ref[:, lo:lo + DIFF_DH])

    def softmax_parts(s):
        e = jnp.exp2(s - jnp.max(s, axis=-1, keepdims=True))
        return e, 1.0 / jnp.sum(e, axis=-1, keepdims=True)

    heads = range(DIFF_HEADS)
    ls = [(logits(h, 0), logits(h, 1)) for h in heads]
    ws = []
    for l0, l1 in ls:
        e0, r0 = softmax_parts(l0)
        e1, r1 = softmax_parts(l1)
        ws.append((e0 * r0 - e1 * (lam * r1)).astype(BF16))
    for h, w in zip(heads, ws):
        lo = h * vw
        o = jnp.dot(w, v_ref[:, lo:lo + vw], preferred_element_type=F32)
        ms = jnp.mean(o * o, axis=-1, keepdims=True)
        o = o * lax.rsqrt(ms + NORM_EPS) * g_ref[...]
        o_ref[:, lo:lo + vw] = (o * (1.0 - lam_init)).astype(o_ref.dtype)


def _diff_attention(q, k, v, lam_vecs, subln_g, lam_init, c_len):
    b, l, w = q.shape
    tq = ROW_TILE
    s_len = l - c_len
    off = c_len // tq
    return pl.pallas_call(
        functools.partial(_diff_attn_kernel, lam_init=lam_init),
        grid=(b, s_len // tq),
        in_specs=[
            pl.BlockSpec(lam_vecs.shape, lambda bb, j: (0, 0)),
            pl.BlockSpec((1, 2 * DIFF_DH), lambda bb, j: (0, 0)),
            pl.BlockSpec((None, tq, w), lambda bb, j: (bb, j + off, 0)),
            pl.BlockSpec((None, l, w), lambda bb, j: (bb, 0, 0)),
            pl.BlockSpec((None, l, w), lambda bb, j: (bb, 0, 0)),
        ],
        out_specs=pl.BlockSpec((None, tq, w), lambda bb, j: (bb, j, 0)),
        out_shape=jax.ShapeDtypeStruct((b, s_len, w), BF16),
        compiler_params=_cparams(("arbitrary", "arbitrary")),
        name="diff_attention",
    )(lam_vecs, subln_g.reshape(1, -1), q, k, v)


def _ssd_kernel(x_ref, xp_ref, xn_ref, dt_ref, cw_ref, cb_ref, dtb_ref, alog_ref, dsk_ref, o_ref,
                ext_scr, st_scr, *, q, nct, nt):
    d = pl.program_id(0)
    g = pl.program_id(2)
    tile = _seq_tile(d, g, nct, nt)
    pv = jnp.where((tile == 0) | (tile == nct), 0.0, 1.0)
    nv = jnp.where((tile == nct - 1) | (tile == nt - 1), 0.0, 1.0)
    ext_scr[0:SUBLANES] = xp_ref[...] * pv
    ext_scr[SUBLANES:SUBLANES + q] = x_ref[...]
    ext_scr[SUBLANES + q:2 * SUBLANES + q] = xn_ref[...] * nv

    @pl.when(g == 0)
    def _():
        st_scr[...] = jnp.zeros_like(st_scr)

    u = cb_ref[...] + cw_ref[0] * ext_scr[SUBLANES - 1:SUBLANES - 1 + q, :]
    for j in range(1, CONV_W):
        u = u + cw_ref[j] * ext_scr[SUBLANES - 1 + j:SUBLANES - 1 + j + q, :]
    act = _silu(u)

    dtr = dt_ref[...]
    dtr = jnp.where(d == 0, dtr, pltpu.roll(dtr, LANES - SSD_HEADS, 1))
    dtv = _softplus(dtr + dtb_ref[...])
    head_lane = lax.broadcasted_iota(jnp.int32, (1, LANES), 1) < SSD_HEADS
    dta = dtv * jnp.where(head_lane, -jnp.exp(alog_ref[...]), 0.0)
    ri = lax.broadcasted_iota(jnp.int32, (q, q), 0)
    ci = lax.broadcasted_iota(jnp.int32, (q, q), 1)
    keep = jnp.where(d == 0, ri - ci, ci - ri) >= 0
    cum = jnp.dot(keep.astype(F32), dta, preferred_element_type=F32, precision=HIGHEST)
    tot = jnp.sum(dta, axis=0, keepdims=True)
    cum_t = cum.T
    dt_t = dtv.T
    to_end = jnp.exp(tot - cum) * dtv
    e_cum = jnp.exp(cum)
    e_tot = jnp.exp(tot)
    dskip = dsk_ref[...] * jnp.where(d == 0, 1.0, 0.0)

    hpg = SSD_HEADS // SSD_GROUPS
    for gi in range(SSD_GROUPS):
        b_g = act[:, SSD_INNER + gi * SSD_STATE:SSD_INNER + (gi + 1) * SSD_STATE]
        c_lo = SSD_INNER + SSD_GROUPS * SSD_STATE + gi * SSD_STATE
        c_g = act[:, c_lo:c_lo + SSD_STATE].astype(BF16)
        cb = _nt_dot(c_g, b_g.astype(BF16))
        b_gt = b_g.T.astype(BF16)
        for hh in range(hpg):
            h = gi * hpg + hh
            xs = act[:, h * SSD_HEAD_DIM:(h + 1) * SSD_HEAD_DIM]
            seg = cum[:, h:h + 1] - cum_t[h:h + 1, :]
            decay = jnp.exp(jnp.where(keep, seg, NEG_BIG))
            w = (cb * decay * dt_t[h:h + 1, :]).astype(BF16)
            state = st_scr[h]
            y = jnp.dot(w, xs.astype(BF16), preferred_element_type=F32)
            y = y + jnp.dot(c_g, state.astype(BF16), preferred_element_type=F32) * e_cum[:, h:h + 1]
            y = y + dskip[:, h * SSD_HEAD_DIM:(h + 1) * SSD_HEAD_DIM] * xs
            o_ref[:, h * SSD_HEAD_DIM:(h + 1) * SSD_HEAD_DIM] = y
            s_new = jnp.dot(b_gt, (xs * to_end[:, h:h + 1]).astype(BF16), preferred_element_type=F32)
            st_scr[h] = e_tot[:, h:h + 1] * state + s_new


def _ssd(xbc, dt, conv_w, conv_b, dt_bias, a_log, d_skip, nct):
    b, l, cd = xbc.shape
    q = TIME_TILE
    nt = l // q
    tile = lambda d, bb, g: _seq_tile(d, g, nct, nt)
    r8 = q // SUBLANES
    pad = LANES - SSD_HEADS
    dtb = jnp.pad(dt_bias, ((0, 0), (0, pad))).reshape(2, 1, LANES)
    alog = jnp.pad(a_log, ((0, 0), (0, pad))).reshape(2, 1, LANES)
    dsk = jnp.repeat(d_skip, SSD_HEAD_DIM).reshape(1, SSD_INNER)
    return pl.pallas_call(
        functools.partial(_ssd_kernel, q=q, nct=nct, nt=nt),
        grid=(2, b, nt),
        in_specs=[
            pl.BlockSpec((None, q, cd), lambda d, bb, g: (bb, tile(d, bb, g), 0)),
            pl.BlockSpec((None, SUBLANES, cd),
                         lambda d, bb, g: (bb, jnp.maximum(tile(d, bb, g) * r8 - 1, 0), 0)),
            pl.BlockSpec((None, SUBLANES, cd),
                         lambda d, bb, g: (bb, jnp.minimum((tile(d, bb, g) + 1) * r8, l // SUBLANES - 1), 0)),
            pl.BlockSpec((None, q, LANES), lambda d, bb, g: (bb, tile(d, bb, g), 0)),
            pl.BlockSpec((CONV_W, 1, cd), lambda d, bb, g: (0, 0, 0)),
            pl.BlockSpec((1, cd), lambda d, bb, g: (0, 0)),
            pl.BlockSpec((None, 1, LANES), lambda d, bb, g: (d, 0, 0)),
            pl.BlockSpec((None, 1, LANES), lambda d, bb, g: (d, 0, 0)),
            pl.BlockSpec((1, SSD_INNER), lambda d, bb, g: (0, 0)),
        ],
        out_specs=pl.BlockSpec((None, None, q, SSD_INNER), lambda d, bb, g: (d, bb, tile(d, bb, g), 0)),
        out_shape=jax.ShapeDtypeStruct((2, b, l, SSD_INNER), F32),
        scratch_shapes=[
            pltpu.VMEM((q + 2 * SUBLANES, cd), F32),
            pltpu.VMEM((SSD_HEADS, SSD_STATE, SSD_HEAD_DIM), F32),
        ],
        compiler_params=_cparams(("arbitrary", "arbitrary", "arbitrary")),
        name="ssd_chunked",
    )(xbc, xbc, xbc, dt, conv_w.reshape(CONV_W, 1, cd), conv_b.reshape(1, cd), dtb, alog, dsk)


def _out_odd_kernel(x_ref, diff_ref, y_ref, z_ref, ng_ref, wa_ref, wb_ref, g1_ref, o_ref):
    yz = (y_ref[0] + y_ref[1]) * _silu(z_ref[...])
    gs = SSD_INNER // SSD_GROUPS
    parts = []
    for gi in range(SSD_GROUPS):
        seg = yz[:, gi * gs:(gi + 1) * gs]
        ms = jnp.mean(seg * seg, axis=-1, keepdims=True)
        parts.append(seg * lax.rsqrt(ms + NORM_EPS) * ng_ref[:, gi * gs:(gi + 1) * gs])
    ssd = jnp.concatenate(parts, axis=-1).astype(BF16)
    y = (jnp.dot(diff_ref[...], wa_ref[...], preferred_element_type=F32)
         + jnp.dot(ssd, wb_ref[...], preferred_element_type=F32))
    o_ref[...] = x_ref[...] + g1_ref[...] * y


def _out_odd(xc, diff, y2, z, norm_g, w_a, w_b, mods3, layer, c_len):
    b, l, d = xc.shape
    s_len = l - c_len
    tm = ROW_TILE
    off = c_len // tm
    w = SSD_INNER
    row = lambda bb, i: bb
    return pl.pallas_call(
        _out_odd_kernel,
        grid=(b, s_len // tm),
        in_specs=[
            pl.BlockSpec((None, tm, d), lambda bb, i: (bb, i + off, 0)),
            pl.BlockSpec((None, tm, diff.shape[-1]), lambda bb, i: (bb, i, 0)),
            pl.BlockSpec((2, None, tm, w), lambda bb, i: (0, bb, i + off, 0)),
            pl.BlockSpec((None, tm, w), lambda bb, i: (bb, i + off, 0)),
            pl.BlockSpec((1, w), lambda bb, i: (0, 0)),
            pl.BlockSpec(w_a.shape, lambda bb, i: (0, 0)),
            pl.BlockSpec(w_b.shape, lambda bb, i: (0, 0)),
            _mod_spec(d, layer, 2, row),
        ],
        out_specs=pl.BlockSpec((None, tm, d), lambda bb, i: (bb, i, 0)),
        out_shape=jax.ShapeDtypeStruct((b, s_len, d), F32),
        compiler_params=_cparams(("arbitrary", "arbitrary")),
        name="out_proj_odd",
    )(xc, diff, y2, z, norm_g.reshape(1, w), w_a, w_b, mods3)


def _router_kernel(x_ref, g_ref, sh_ref, sc_ref, wr_ref, br_ref, h_ref, eid_ref, rnk_ref, gate_ref, cnt_ref,
                   carry_scr):
    @pl.when((pl.program_id(0) == 0) & (pl.program_id(1) == 0))
    def _():
        carry_scr[...] = jnp.zeros_like(carry_scr)

    h = _norm_mod(x_ref[...], g_ref[...], sh_ref[...], sc_ref[...])
    _store_chunk_rows(h_ref, _pack_bf16(h))
    tm = h.shape[0]
    per = N_EXPERTS // N_EXPERT_GROUPS
    logits = lax.dot_general(wr_ref[...], h, (((1,), (1,)), ((), ())),
                             preferred_element_type=F32, precision=HIGHEST)
    scores = jax.nn.sigmoid(logits)
    sel = scores + br_ref[...]
    sel3 = sel.reshape(N_EXPERT_GROUPS, per, tm)
    kio = lax.broadcasted_iota(jnp.int32, sel3.shape, 1)
    m1 = jnp.max(sel3, axis=1, keepdims=True)
    first = jnp.min(jnp.where(sel3 == m1, kio, per), axis=1, keepdims=True)
    m2 = jnp.max(jnp.where(kio == first, NEG_BIG, sel3), axis=1, keepdims=True)
    gs = m1 + m2
    gio = lax.broadcasted_iota(jnp.int32, gs.shape, 0)
    ahead = jnp.zeros(gs.shape, jnp.int32)
    for gp in range(N_EXPERT_GROUPS):
        other = gs[gp:gp + 1]
        ahead = ahead + jnp.where((other > gs) | ((other == gs) & (gp < gio)), 1, 0)
    grp_on = jnp.where(ahead < TOPK_GROUPS, 1.0, 0.0)
    selm = jnp.where(jnp.broadcast_to(grp_on, sel3.shape) > 0.5, sel3, NEG_BIG).reshape(N_EXPERTS, tm)
    eio = lax.broadcasted_iota(jnp.int32, selm.shape, 0)
    work = selm
    cf = jnp.zeros(selm.shape, F32)
    e_rows, s_rows = [], []
    for k in range(TOP_K):
        best = jnp.max(work, axis=0, keepdims=True)
        idx = jnp.min(jnp.where(work == best, eio, N_EXPERTS), axis=0, keepdims=True)
        hit = eio == idx
        cf = cf + jnp.where(hit, 1.0, 0.0)
        work = jnp.where(hit, NEG_BIG, work)
        e_rows.append(idx)
        s_rows.append(jnp.sum(jnp.where(hit, scores, 0.0), axis=0, keepdims=True))
    denom = s_rows[0]
    for s_k in s_rows[1:]:
        denom = denom + s_k
    g_rows = [s_k / denom * ROUTED_SCALE for s_k in s_rows]
    ti = lax.broadcasted_iota(jnp.int32, (tm, tm), 0)
    tj = lax.broadcasted_iota(jnp.int32, (tm, tm), 1)
    before = jnp.where(ti < tj, 1.0, 0.0).astype(BF16)
    in_expert = carry_scr[:, 0:1] + jnp.dot(cf.astype(BF16), before, preferred_element_type=F32)
    carry_scr[...] = carry_scr[...] + jnp.sum(cf, axis=1, keepdims=True)
    cnt_ref[...] = carry_scr[...]
    r_rows = [jnp.sum(jnp.where(eio == idx, in_expert, 0.0), axis=0, keepdims=True) for idx in e_rows]
    eid_ref[...] = jnp.concatenate(e_rows, axis=0)
    rnk_ref[...] = jnp.concatenate(r_rows, axis=0).astype(jnp.int32)
    padded = jnp.concatenate(g_rows + [jnp.zeros((LANES - TOP_K, tm), F32)], axis=0)
    gate_ref[...] = padded.T


def _router(x, g, mods3, layer, row_fn, w_router_t, b_router):
    b, r, d = x.shape
    tm = ROW_TILE
    nt = r // tm
    slot = pl.BlockSpec((TOP_K, tm), lambda bb, i: (0, bb * nt + i))
    slot_shape = jax.ShapeDtypeStruct((TOP_K, b * r), jnp.int32)
    return pl.pallas_call(
        _router_kernel,
        grid=(b, nt),
        in_specs=[
            pl.BlockSpec((None, tm, d), lambda bb, i: (bb, i, 0)),
            pl.BlockSpec((1, d), lambda bb, i: (0, 0)),
            _mod_spec(d, layer, 3, row_fn),
            _mod_spec(d, layer, 4, row_fn),
            pl.BlockSpec(w_router_t.shape, lambda bb, i: (0, 0)),
            pl.BlockSpec((N_EXPERTS, 1), lambda bb, i: (0, 0)),
        ],
        out_specs=[
            pl.BlockSpec((tm * ROW_CHUNKS, LANES), lambda bb, i: (bb * nt + i, 0)),
            slot,
            slot,
            pl.BlockSpec((None, tm, LANES), lambda bb, i: (bb, i, 0)),
            pl.BlockSpec((N_EXPERTS, LANES), lambda bb, i: (0, 0)),
        ],
        out_shape=[jax.ShapeDtypeStruct((b * r * ROW_CHUNKS, LANES), jnp.uint32), slot_shape, slot_shape,
                   jax.ShapeDtypeStruct((b, r, LANES), F32), jax.ShapeDtypeStruct((N_EXPERTS, LANES), F32)],
        scratch_shapes=[pltpu.VMEM((N_EXPERTS, LANES), F32)],
        compiler_params=_cparams(("arbitrary", "arbitrary")),
        name="moe_router",
    )(x, g.reshape(1, d), mods3, mods3, w_router_t, b_router.reshape(N_EXPERTS, 1))


def _moe_plan(counts, n_rows):
    blk = EXPERT_BLK
    nb = n_rows // blk
    ends = jnp.cumsum(counts)
    starts = ends - counts
    count_le = lambda sorted_vals, q: jnp.sum(sorted_vals[None, :] <= q[:, None], axis=1, dtype=jnp.int32)
    first = jnp.arange(nb, dtype=jnp.int32) * blk
    e_lo = count_le(ends, first)
    e_hi = count_le(ends, first + (blk - 1))
    n_pair = e_hi - e_lo + 1
    p_end = jnp.cumsum(n_pair)
    p_start = p_end - n_pair
    i = jnp.arange(nb + N_EXPERTS - 1, dtype=jnp.int32)
    j = jnp.minimum(count_le(p_end, i), nb - 1)
    valid = i < p_end[-1]
    e = jnp.where(valid, e_lo[j] + i - p_start[j], e_hi[nb - 1]).astype(jnp.int32)
    bounds = jnp.concatenate([starts, ends[-1:]]).astype(jnp.int32)
    return j, e, valid.astype(jnp.int32), bounds


def _positions_kernel(starts_ref, eid_ref, rnk_ref, pos_ref):
    eid = eid_ref[...]
    pos = rnk_ref[...]
    for e in range(N_EXPERTS):
        pos = pos + jnp.where(eid == e, starts_ref[e], 0)
    pos_ref[...] = pos * ROW_CHUNKS


def _positions(eid, rnk, starts):
    full = pl.BlockSpec(eid.shape, lambda: (0, 0))
    return pl.pallas_call(
        _positions_kernel,
        in_specs=[pl.BlockSpec(memory_space=pltpu.SMEM), full, full],
        out_specs=full,
        out_shape=jax.ShapeDtypeStruct(eid.shape, jnp.int32),
        compiler_params=pltpu.CompilerParams(vmem_limit_bytes=VMEM_LIMIT),
        name="moe_positions",
    )(starts, eid, rnk)


def _token_row(ref, first):
    return ref.at[pl.ds(pl.multiple_of(first, ROW_CHUNKS), ROW_CHUNKS)]


def _dispatch_kernel(pos_ref, h_ref, xs_ref, sem):
    tm = h_ref.shape[0] // ROW_CHUNKS

    def issue(t, carry):
        src = _token_row(h_ref, t * ROW_CHUNKS)
        for k in range(TOP_K):
            pltpu.make_async_copy(src, _token_row(xs_ref, pos_ref[k, t]), sem).start(priority=k % 2)
        return carry

    lax.fori_loop(0, tm, issue, 0)
    done = pl.ds(0, tm * ROW_CHUNKS)
    for _ in range(TOP_K):
        pltpu.make_async_copy(h_ref.at[done], xs_ref.at[done], sem).wait()


def _dispatch(h2, pos):
    rows, w = h2.shape
    tm = ROW_TILE
    return pl.pallas_call(
        _dispatch_kernel,
        grid=(rows // (tm * ROW_CHUNKS),),
        in_specs=[
            pl.BlockSpec((TOP_K, tm), lambda i: (0, i), memory_space=pltpu.SMEM),
            pl.BlockSpec((tm * ROW_CHUNKS, w), lambda i: (i, 0)),
        ],
        out_specs=pl.BlockSpec(memory_space=pl.ANY),
        out_shape=jax.ShapeDtypeStruct((rows * TOP_K, w), h2.dtype),
        scratch_shapes=[pltpu.SemaphoreType.DMA],
        compiler_params=_cparams(("arbitrary",)),
        name="moe_dispatch",
    )(pos, h2)


def _grouped_kernel(pb_ref, pe_ref, pv_ref, bnd_ref, xs_ref, wg_ref, wu_ref, wd_ref, y_ref, wgb, wub, wdb):
    i = pl.program_id(0)
    prev = jnp.maximum(i - 1, 0)
    j = pb_ref[i]
    e = pe_ref[i]
    blk = xs_ref.shape[0] // ROW_CHUNKS

    @pl.when((i == 0) | (pb_ref[prev] != j))
    def _():
        y_ref[...] = jnp.zeros_like(y_ref)

    @pl.when((i == 0) | (pe_ref[prev] != e))
    def _():
        wgb[...] = wg_ref[...].astype(BF16)
        wub[...] = wu_ref[...].astype(BF16)
        wdb[...] = wd_ref[...].astype(BF16)

    @pl.when(pv_ref[i] == 1)
    def _():
        xw = _load_chunk_rows(xs_ref, blk)
        a = _packed_dot(xw, wgb)
        u = _packed_dot(xw, wub)
        yv = jnp.dot((_silu(a) * u).astype(BF16), wdb[...], preferred_element_type=F32)
        rows = j * blk + lax.broadcasted_iota(jnp.int32, (blk, 1), 0)
        own = (rows >= bnd_ref[e]) & (rows < bnd_ref[e + 1])
        yw = _pack_bf16(yv)
        for c in range(ROW_CHUNKS):
            sl = pl.ds(c, blk, stride=ROW_CHUNKS)
            y_ref[sl, :] = jnp.where(own, yw[:, c * LANES:(c + 1) * LANES], y_ref[sl, :])


def _grouped(pb, pe, pv, bounds, xs, wg, wu, wd, layer):
    p, half = xs.shape
    d = 2 * ROW_CHUNKS * LANES
    blk = EXPERT_BLK * ROW_CHUNKS
    grid_spec = pltpu.PrefetchScalarGridSpec(
        num_scalar_prefetch=4,
        grid=(pb.shape[0],),
        in_specs=[
            pl.BlockSpec((blk, half), lambda i, pb, pe, pv, bnd: (pb[i], 0)),
            pl.BlockSpec((None, None, d, D_EXPERT), lambda i, pb, pe, pv, bnd: (layer, pe[i], 0, 0)),
            pl.BlockSpec((None, None, d, D_EXPERT), lambda i, pb, pe, pv, bnd: (layer, pe[i], 0, 0)),
            pl.BlockSpec((None, None, D_EXPERT, d), lambda i, pb, pe, pv, bnd: (layer, pe[i], 0, 0)),
        ],
        out_specs=pl.BlockSpec((blk, half), lambda i, pb, pe, pv, bnd: (pb[i], 0)),
        scratch_shapes=[
            pltpu.VMEM((d, D_EXPERT), BF16),
            pltpu.VMEM((d, D_EXPERT), BF16),
            pltpu.VMEM((D_EXPERT, d), BF16),
        ],
    )
    return pl.pallas_call(
        _grouped_kernel,
        grid_spec=grid_spec,
        out_shape=jax.ShapeDtypeStruct((p, half), jnp.uint32),
        compiler_params=_cparams(("arbitrary",)),
        name="moe_grouped_experts",
    )(pb, pe, pv, bounds, xs, wg, wu, wd)


def _combine_kernel(*refs, final):
    pos_ref, y_ref, gate_ref, h_ref, x_ref, g2_ref, sg_ref, su_ref, sd_ref = refs[:9]
    o_ref, buf, sem = refs[-3:]
    tm = x_ref.shape[0]

    def issue(t, carry):
        for k in range(TOP_K):
            pltpu.make_async_copy(_token_row(y_ref, pos_ref[k, t]), _token_row(buf.at[k], t * ROW_CHUNKS),
                                  sem).start(priority=k % 2)
        return carry

    lax.fori_loop(0, tm, issue, 0)
    hw = _load_chunk_rows(h_ref, tm)
    a = _packed_dot(hw, sg_ref)
    u = _packed_dot(hw, su_ref)
    acc = jnp.dot((_silu(a) * u).astype(BF16), sd_ref[...], preferred_element_type=F32)
    done = pl.ds(0, tm * ROW_CHUNKS)
    for k in range(TOP_K):
        pltpu.make_async_copy(y_ref.at[done], buf.at[k, done], sem).wait()
    g = gate_ref[...]
    half = hw.shape[-1]
    acc_hi = acc[:, :half]
    acc_lo = acc[:, half:]
    for k in range(TOP_K):
        hi, lo = _unpack_bf16(_load_chunk_rows(buf.at[k], tm))
        acc_hi = acc_hi + g[:, k:k + 1] * hi
        acc_lo = acc_lo + g[:, k:k + 1] * lo
    x = x_ref[...] + g2_ref[...] * jnp.concatenate([acc_hi, acc_lo], axis=-1)
    if final:
        gf_ref = refs[9]
        ms = jnp.mean(x * x, axis=-1, keepdims=True)
        x = x * lax.rsqrt(ms + NORM_EPS) * gf_ref[...]
    o_ref[...] = x


def _combine(pos, y, gates, h2, x, mods3, layer, row_fn, sg, su, sd, g_final=None):
    b, r, d = x.shape
    tm = ROW_TILE
    nt = r // tm
    tile = pl.BlockSpec((None, tm, d), lambda bb, i: (bb, i, 0))
    in_specs = [
        pl.BlockSpec((TOP_K, tm), lambda bb, i: (0, bb * nt + i), memory_space=pltpu.SMEM),
        pl.BlockSpec(memory_space=pl.ANY),
        pl.BlockSpec((None, tm, LANES), lambda bb, i: (bb, i, 0)),
        pl.BlockSpec((tm * ROW_CHUNKS, LANES), lambda bb, i: (bb * nt + i, 0)),
        tile,
        _mod_spec(d, layer, 5, row_fn),
        pl.BlockSpec(sg.shape, lambda bb, i: (0, 0)),
        pl.BlockSpec(su.shape, lambda bb, i: (0, 0)),
        pl.BlockSpec(sd.shape, lambda bb, i: (0, 0)),
    ]
    args = [pos, y, gates, h2, x, mods3, sg, su, sd]
    if g_final is not None:
        in_specs.append(pl.BlockSpec((1, d), lambda bb, i: (0, 0)))
        args.append(g_final.reshape(1, d))
    return pl.pallas_call(
        functools.partial(_combine_kernel, final=g_final is not None),
        grid=(b, nt),
        in_specs=in_specs,
        out_specs=tile,
        out_shape=jax.ShapeDtypeStruct((b, r, d), F32),
        scratch_shapes=[pltpu.VMEM((TOP_K, tm * ROW_CHUNKS, LANES), jnp.uint32), pltpu.SemaphoreType.DMA],
        compiler_params=_cparams(("arbitrary", "arbitrary")),
        name="moe_combine",
    )(*args)


def _moe(x, g_ffn, mods3, layer, row_fn, w_router, b_router, w_e_gate, w_e_up, w_e_down,
         ws_gate, ws_up, ws_down, g_final=None):
    b, r, d = x.shape
    h2, eid, rnk, gates, cnt = _router(x, g_ffn, mods3, layer, row_fn, w_router.T, b_router)
    pb, pe, pv, bounds = _moe_plan(cnt[:, 0].astype(jnp.int32), b * r * TOP_K)
    pos = _positions(eid, rnk, bounds[:N_EXPERTS])
    xs = _dispatch(h2, pos)
    y = _grouped(pb, pe, pv, bounds, xs, w_e_gate, w_e_up, w_e_down, layer)
    return _combine(pos, y, gates, h2, x, mods3, layer, row_fn,
                    ws_gate.astype(BF16), ws_up.astype(BF16), ws_down.astype(BF16), g_final)


def kernel(x, c, ctx, c_ctx, w_mod, b_mod, g_mix, g_ffn, g_final, ab_w_in, ab_w_out, ab_conv_w, ab_conv_b, ab_w_r, ab_b_r, ab_w_i, ab_b_i, ab_lam, ab_sink, cd_w_in, cd_w_out, cd_lam, cd_subln_g, cd_conv_w, cd_conv_b, cd_dt_bias, cd_a_log, cd_d_skip, cd_norm_g, w_router, b_router, w_e_gate, w_e_up, w_e_down, ws_gate, ws_up, ws_down):
    bsz, s_len, d = x.shape
    c_len = ctx.shape[1]
    depth = w_mod.shape[0]
    assert depth == 2 and bsz == SUBLANES, "kernels are specialised to depth 2 and batch 8"
    assert c_len % ROW_TILE == 0 and s_len % ROW_TILE == 0
    nct_row = c_len // ROW_TILE
    nct_time = c_len // TIME_TILE

    c_all = jnp.concatenate([c, c_ctx[None], jnp.zeros((MOD_ROWS - bsz - 1, d), F32)], axis=0)
    mods3 = _modulations(c_all, w_mod, b_mod).reshape(depth * MOD_ROWS, 1, N_MOD * d)
    rope_tabs = _rope_tables(c_len, s_len)
    xc = jnp.concatenate([ctx, x], axis=1)
    row_mixed = lambda bb, i: jnp.where(i < nct_row, SUBLANES, bb)
    row_latent = lambda bb, i: bb

    w_in = ab_w_in[0].astype(BF16)
    q_hi = LRU_WIDTH + WIN_HEADS * HEAD_DIM
    x_hi = q_hi + LRU_WIDTH
    k_hi = x_hi + WIN_KV_HEADS * HEAD_DIM
    gate, q, xa, k, v = _project(xc, g_mix[0], mods3, 0, nct_row, rope_tabs, [
        (w_in[:, :LRU_WIDTH], None, F32, False),
        (w_in[:, LRU_WIDTH:q_hi], HEAD_DIM ** -0.5 * LOG2E, BF16, False),
        (w_in[:, q_hi:x_hi], None, F32, True),
        (w_in[:, x_hi:k_hi], 1.0, BF16, False),
        (w_in[:, k_hi:], None, BF16, False),
    ])
    l_len = c_len + s_len
    w_gates = jnp.stack([jnp.concatenate([_block_diag(ab_w_r[0, dd]), _block_diag(ab_w_i[0, dd])], axis=1)
                         for dd in range(2)]).astype(BF16)
    b_gates = jnp.concatenate([ab_b_r[0], ab_b_i[0]], axis=-1).reshape(2, 1, 2 * LRU_WIDTH)
    rec = _rglru(xa.reshape(l_len, bsz, LRU_WIDTH), ab_conv_w[0], ab_conv_b[0], w_gates, b_gates,
                 ab_lam[0].reshape(2, 1, LRU_WIDTH), nct_time)
    att = _win_attention(q, k, v, ab_sink[0], c_len)
    w_out = ab_w_out[0].astype(BF16)
    xc = _out_even(xc, rec.reshape(2, l_len, bsz * LRU_WIDTH), gate, att, w_out[:LRU_WIDTH], w_out[LRU_WIDTH:],
                   mods3, 0, nct_row)
    xc = _moe(xc, g_ffn[0], mods3, 0, row_mixed, w_router[0], b_router[0], w_e_gate, w_e_up, w_e_down,
              ws_gate[0], ws_up[0], ws_down[0])

    w_in = cd_w_in[0].astype(BF16)
    qk = DIFF_HEADS * 2 * DIFF_DH
    z_hi = qk + SSD_INNER
    k_hi = z_hi + qk
    v_hi = k_hi + qk
    x_hi = v_hi + SSD_CONV_DIM
    w_dt = jnp.pad(w_in[:, x_hi:], ((0, 0), (0, LANES - 2 * SSD_HEADS)))
    q, z, k, v, xbc, dt = _project(xc, g_mix[1], mods3, 1, nct_row, rope_tabs, [
        (w_in[:, :qk], DIFF_DH ** -0.5 * LOG2E, BF16, False),
        (w_in[:, qk:z_hi], None, F32, False),
        (w_in[:, z_hi:k_hi], 1.0, BF16, False),
        (w_in[:, k_hi:v_hi], None, BF16, False),
        (w_in[:, v_hi:x_hi], None, F32, False),
        (w_dt, None, F32, False),
    ])
    lam_init = 0.8 - 0.6 * math.exp(-0.3 * 1)
    diff = _diff_attention(q, k, v, cd_lam[0], cd_subln_g[0], lam_init, c_len)
    y2 = _ssd(xbc, dt, cd_conv_w[0], cd_conv_b[0], cd_dt_bias[0], cd_a_log[0], cd_d_skip[0], nct_time)
    w_out = cd_w_out[0].astype(BF16)
    xl = _out_odd(xc, diff, y2, z, cd_norm_g[0], w_out[:qk], w_out[qk:], mods3, 1, c_len)
    return _moe(xl, g_ffn[1], mods3, 1, row_latent, w_router[1], b_router[1], w_e_gate, w_e_up, w_e_down,
                ws_gate[1], ws_up[1], ws_down[1], g_final=g_final)
```

```python
import functools
import math

import jax
import jax.numpy as jnp
from jax import lax
from jax.experimental import pallas as pl
from jax.experimental.pallas import tpu as pltpu

F32 = jnp.float32
BF16 = jnp.bfloat16
HIGHEST = lax.Precision.HIGHEST

GRID_W = 64
N_MOD = 6
NORM_EPS = 1e-6
ROPE_BASE = 10000.0
CONV_W = 4

LRU_WIDTH = 512
LRU_BLOCKS = 8
LRU_C = 8.0

HEAD_DIM = 64
WIN_HEADS = 8
WIN_KV_HEADS = 2
WINDOW = 128

DIFF_HEADS = 4
DIFF_DH = 64

SSD_HEADS = 8
SSD_HEAD_DIM = 64
SSD_INNER = SSD_HEADS * SSD_HEAD_DIM
SSD_GROUPS = 2
SSD_STATE = 128
SSD_CONV_DIM = SSD_INNER + 2 * SSD_GROUPS * SSD_STATE

N_EXPERTS = 64
N_EXPERT_GROUPS = 8
TOPK_GROUPS = 4
TOP_K = 8
D_EXPERT = 256
ROUTED_SCALE = 2.5

LANES = 128
SUBLANES = 8
MOD_ROWS = 16
TIME_TILE = 128
ROW_TILE = 256
EXPERT_BLK = 1024
ROW_CHUNKS = 4
VMEM_LIMIT = 48 * 1024 * 1024
NEG_BIG = -1e30
LOG2E = math.log2(math.e)


def _cparams(sem):
    return pltpu.CompilerParams(dimension_semantics=sem, vmem_limit_bytes=VMEM_LIMIT)


def _nt_dot(a, b):
    return lax.dot_general(a, b, (((1,), (1,)), ((), ())), preferred_element_type=F32)


def _softplus(x):
    return jnp.maximum(x, 0.0) + jnp.log1p(jnp.exp(-jnp.abs(x)))


def _silu(x):
    return x * jax.nn.sigmoid(x)


def _pack_bf16(x):
    half = x.shape[-1] // 2
    bits = pltpu.bitcast(x.astype(BF16).astype(F32), jnp.uint32)
    return bits[:, :half] | (bits[:, half:] >> 16)


def _unpack_bf16(w):
    hi = pltpu.bitcast(w & jnp.uint32(0xFFFF0000), F32)
    lo = pltpu.bitcast(w << 16, F32)
    return hi, lo


def _store_chunk_rows(ref, w):
    n = w.shape[0]
    for j in range(ROW_CHUNKS):
        ref[pl.ds(j, n, stride=ROW_CHUNKS), :] = w[:, j * LANES:(j + 1) * LANES]


def _load_chunk_rows(ref, n):
    return jnp.concatenate([ref[pl.ds(j, n, stride=ROW_CHUNKS), :] for j in range(ROW_CHUNKS)], axis=1)


def _packed_dot(w, weight_ref):
    half = w.shape[-1]
    hi, lo = _unpack_bf16(w)
    return (jnp.dot(hi.astype(BF16), weight_ref[:half, :], preferred_element_type=F32)
            + jnp.dot(lo.astype(BF16), weight_ref[half:, :], preferred_element_type=F32))


def _mod_kernel(c_ref, w_ref, b_ref, o_ref):
    c = c_ref[...]
    s = _silu(c)
    o_ref[...] = jnp.dot(s, w_ref[...], preferred_element_type=F32, precision=HIGHEST) + b_ref[...]


def _modulations(c_all, w_mod, b_mod):
    depth, d, _ = w_mod.shape
    return pl.pallas_call(
        _mod_kernel,
        grid=(depth, N_MOD),
        in_specs=[
            pl.BlockSpec((MOD_ROWS, d), lambda l, k: (0, 0)),
            pl.BlockSpec((None, d, d), lambda l, k: (l, 0, k)),
            pl.BlockSpec((None, 1, d), lambda l, k: (l, 0, k)),
        ],
        out_specs=pl.BlockSpec((None, MOD_ROWS, d), lambda l, k: (l, 0, k)),
        out_shape=jax.ShapeDtypeStruct((depth, MOD_ROWS, N_MOD * d), F32),
        compiler_params=_cparams(("arbitrary", "arbitrary")),
        name="adaln_modulation",
    )(c_all, w_mod, b_mod.reshape(depth, 1, N_MOD * d))


def _mod_spec(d, layer, chunk, row_fn):
    return pl.BlockSpec((None, 1, d), lambda b, i: (layer * MOD_ROWS + row_fn(b, i), 0, chunk))


def _norm_mod(x, g, sh, sc):
    ms = jnp.mean(x * x, axis=-1, keepdims=True)
    return (x * lax.rsqrt(ms + NORM_EPS) * g) * (1.0 + sc) + sh


def _rope(y, cos, sa, sb):
    n = y.shape[-1]
    half = HEAD_DIM // 2
    return y * cos + pltpu.roll(y, n - half, 1) * sa + pltpu.roll(y, half, 1) * sb


def _proj_kernel(*refs, ropes):
    n = len(ropes)
    x_ref, g_ref, sh_ref, sc_ref, cos_ref, sa_ref, sb_ref = refs[:7]
    w_refs = refs[7:7 + n]
    o_refs = refs[7 + n:]
    h = _norm_mod(x_ref[...], g_ref[...], sh_ref[...], sc_ref[...]).astype(BF16)
    for w_ref, o_ref, rope in zip(w_refs, o_refs, ropes):
        y = jnp.dot(h, w_ref[...], preferred_element_type=F32)
        if rope is not None:
            w = y.shape[-1]
            y = _rope(y, cos_ref[:, :w], sa_ref[:, :w], sb_ref[:, :w])
            if rope != 1.0:
                y = y * rope
        o_ref[...] = y.astype(o_ref.dtype)


def _project(xc, g, mods3, layer, nct, rope_tabs, groups):
    b, l, d = xc.shape
    tm = ROW_TILE
    mod_row = lambda i, bb: layer * MOD_ROWS + jnp.where(i < nct, SUBLANES, bb)
    rw = rope_tabs[0].shape[-1]
    in_specs = [
        pl.BlockSpec((None, tm, d), lambda i, bb: (bb, i, 0)),
        pl.BlockSpec((1, d), lambda i, bb: (0, 0)),
        pl.BlockSpec((None, 1, d), lambda i, bb: (mod_row(i, bb), 0, 0)),
        pl.BlockSpec((None, 1, d), lambda i, bb: (mod_row(i, bb), 0, 1)),
    ] + [pl.BlockSpec((tm, rw), lambda i, bb: (i, 0))] * 3
    out_specs, out_shapes = [], []
    for w, _, dt, time_major in groups:
        n = w.shape[1]
        in_specs.append(pl.BlockSpec((d, n), lambda i, bb: (0, 0)))
        if time_major:
            out_specs.append(pl.BlockSpec((tm, n), lambda i, bb: (i, bb)))
            out_shapes.append(jax.ShapeDtypeStruct((l, b * n), dt))
        else:
            out_specs.append(pl.BlockSpec((None, tm, n), lambda i, bb: (bb, i, 0)))
            out_shapes.append(jax.ShapeDtypeStruct((b, l, n), dt))
    return pl.pallas_call(
        functools.partial(_proj_kernel, ropes=tuple(gp[1] for gp in groups)),
        grid=(l // tm, b),
        in_specs=in_specs,
        out_specs=out_specs,
        out_shape=out_shapes,
        compiler_params=_cparams(("arbitrary", "arbitrary")),
        name="norm_mod_project",
    )(xc, g.reshape(1, d), mods3, mods3, *rope_tabs, *[gp[0] for gp in groups])


def _rope_tables(c_len, s_len):
    rows = s_len // GRID_W
    row = jnp.repeat(jnp.arange(rows), GRID_W).astype(F32)
    col = jnp.tile(jnp.arange(GRID_W), rows).astype(F32)
    n = HEAD_DIM // 4
    inv = ROPE_BASE ** (-jnp.arange(n, dtype=F32) / n)
    ang = jnp.concatenate([row[:, None] * inv, col[:, None] * inv], axis=-1)
    cos, sin = jnp.cos(ang), jnp.sin(ang)
    zero = jnp.zeros_like(sin)
    reps = WIN_HEADS
    cos_t = jnp.tile(jnp.concatenate([cos, cos], axis=-1), (1, reps))
    sa_t = jnp.tile(jnp.concatenate([-sin, zero], axis=-1), (1, reps))
    sb_t = jnp.tile(jnp.concatenate([zero, sin], axis=-1), (1, reps))
    w = cos_t.shape[-1]
    pad1 = jnp.ones((c_len, w), F32)
    pad0 = jnp.zeros((c_len, w), F32)
    return (jnp.concatenate([pad1, cos_t], 0), jnp.concatenate([pad0, sa_t], 0),
            jnp.concatenate([pad0, sb_t], 0))


def _seq_tile(d, g, nct, nt):
    rev = jnp.where(g < nct, nct - 1 - g, nt - 1 - (g - nct))
    return jnp.where(d == 0, g, rev)


def _rglru_kernel(x_ref, xp_ref, xn_ref, cw_ref, cb_ref, w_ref, bias_ref, lam_ref, o_ref,
                  ext_scr, a_scr, b_scr, h_scr, *, ts, nct, nt, sub):
    d = pl.program_id(0)
    g = pl.program_id(1)
    tile = _seq_tile(d, g, nct, nt)
    bsz, width = h_scr.shape
    pv = jnp.where((tile == 0) | (tile == nct), 0.0, 1.0)
    nv = jnp.where((tile == nct - 1) | (tile == nt - 1), 0.0, 1.0)
    ext_scr[0:1] = xp_ref[...] * pv
    ext_scr[1:ts + 1] = x_ref[...]
    ext_scr[ts + 1:ts + 3] = xn_ref[...] * nv

    @pl.when(g == 0)
    def _():
        h_scr[...] = jnp.zeros_like(h_scr)

    neg_sp = -LRU_C * _softplus(-lam_ref[...])

    def prep(c, carry):
        r0 = pl.multiple_of(c * sub, sub)
        e = ext_scr[pl.ds(r0, sub + CONV_W - 1)]
        u = cb_ref[...] + cw_ref[0] * e[0:sub]
        for j in range(1, CONV_W):
            u = u + cw_ref[j] * e[j:j + sub]
        u2 = u.reshape(sub * bsz, width)
        gts = jnp.dot(u2.astype(BF16), w_ref[...], preferred_element_type=F32) + bias_ref[...]
        r = jax.nn.sigmoid(gts[:, :width])
        ig = jax.nn.sigmoid(gts[:, width:])
        log_a = neg_sp * r
        a = jnp.exp(log_a)
        mult = jnp.sqrt(1.0 - a * a)
        a_scr[pl.ds(r0, sub)] = a.reshape(sub, bsz, width)
        b_scr[pl.ds(r0, sub)] = (mult * ig * u2).reshape(sub, bsz, width)
        return carry

    lax.fori_loop(0, ts // sub, prep, 0)

    def step(t, h):
        tt = jnp.where(d == 0, t, ts - 1 - t)
        h = a_scr[tt] * h + b_scr[tt]
        o_ref[tt] = h
        return h

    h_scr[...] = lax.fori_loop(0, ts, step, h_scr[...], unroll=8)


def _rglru(xa_tm, conv_w, conv_b, w_gates, b_gates, lam, nct):
    l, bsz, width = xa_tm.shape
    ts = TIME_TILE
    nt = l // ts
    tile = lambda d, g: _seq_tile(d, g, nct, nt)
    kern = functools.partial(_rglru_kernel, ts=ts, nct=nct, nt=nt, sub=16)
    return pl.pallas_call(
        kern,
        grid=(2, nt),
        in_specs=[
            pl.BlockSpec((ts, bsz, width), lambda d, g: (tile(d, g), 0, 0)),
            pl.BlockSpec((1, bsz, width), lambda d, g: (jnp.maximum(tile(d, g) * ts - 1, 0), 0, 0)),
            pl.BlockSpec((2, bsz, width),
                         lambda d, g: (jnp.minimum((tile(d, g) + 1) * (ts // 2), l // 2 - 1), 0, 0)),
            pl.BlockSpec((CONV_W, 1, width), lambda d, g: (0, 0, 0)),
            pl.BlockSpec((1, width), lambda d, g: (0, 0)),
            pl.BlockSpec((None, width, 2 * width), lambda d, g: (d, 0, 0)),
            pl.BlockSpec((None, 1, 2 * width), lambda d, g: (d, 0, 0)),
            pl.BlockSpec((None, 1, width), lambda d, g: (d, 0, 0)),
        ],
        out_specs=pl.BlockSpec((None, ts, bsz, width), lambda d, g: (d, tile(d, g), 0, 0)),
        out_shape=jax.ShapeDtypeStruct((2, l, bsz, width), F32),
        scratch_shapes=[
            pltpu.VMEM((ts + CONV_W - 1, bsz, width), F32),
            pltpu.VMEM((ts, bsz, width), F32),
            pltpu.VMEM((ts, bsz, width), F32),
            pltpu.VMEM((bsz, width), F32),
        ],
        compiler_params=_cparams(("arbitrary", "arbitrary")),
        name="rglru_scan",
    )(xa_tm, xa_tm, xa_tm, conv_w.reshape(CONV_W, 1, width), conv_b.reshape(1, width),
      w_gates, b_gates, lam)


def _block_diag(w):
    nb, c, dd = w.shape
    eye = jnp.eye(nb, dtype=w.dtype)
    return (eye[:, None, :, None] * w[:, :, None, :]).reshape(nb * c, nb * dd)


def _win_attn_kernel(sink_ref, q_ref, k_ref, v_ref, o_ref, *, c_len, l_len, nqc):
    j = pl.program_id(1)
    blk = q_ref.shape[0]
    grp = WIN_HEADS // WIN_KV_HEADS
    band = 3 * blk
    heads = [(h, slice(h * HEAD_DIM, (h + 1) * HEAD_DIM),
              slice((h // grp) * HEAD_DIM, (h // grp + 1) * HEAD_DIM)) for h in range(WIN_HEADS)]

    @pl.when(j < nqc)
    def _():
        logits = [_nt_dot(q_ref[:, hsl], k_ref[0:c_len, ksl]) for _, hsl, ksl in heads]
        probs, dens = [], []
        for (h, _, _), s in zip(heads, logits):
            sink = sink_ref[h] * LOG2E
            m = jnp.maximum(jnp.max(s, axis=-1, keepdims=True), sink)
            p = jnp.exp2(s - m)
            dens.append(jnp.sum(p, axis=-1, keepdims=True) + jnp.exp2(sink - m))
            probs.append(p.astype(BF16))
        outs = [jnp.dot(p, v_ref[0:c_len, ksl], preferred_element_type=F32) / den
                for (_, _, ksl), p, den in zip(heads, probs, dens)]
        o_ref[...] = jnp.concatenate(outs, axis=-1).astype(o_ref.dtype)

    @pl.when(j >= nqc)
    def _():
        jb = j - nqc
        start = jnp.clip(c_len + (jb - 1) * blk, c_len - blk, l_len - band)
        start = pl.multiple_of(start, blk)
        qpos = jb * blk + lax.broadcasted_iota(jnp.int32, (blk, band), 0)
        kpos = start - c_len + lax.broadcasted_iota(jnp.int32, (blk, band), 1)
        valid = (jnp.abs(qpos - kpos) <= WINDOW) & (kpos >= 0)
        lc = [_nt_dot(q_ref[:, hsl], k_ref[0:c_len, ksl]) for _, hsl, ksl in heads]
        lb = [jnp.where(valid, _nt_dot(q_ref[:, hsl], k_ref[pl.ds(start, band), ksl]), NEG_BIG)
              for _, hsl, ksl in heads]
        pcs, pbs, dens = [], [], []
        for (h, _, _), sc, sb in zip(heads, lc, lb):
            sink = sink_ref[h] * LOG2E
            m = jnp.maximum(jnp.maximum(jnp.max(sc, axis=-1, keepdims=True),
                                        jnp.max(sb, axis=-1, keepdims=True)), sink)
            pc = jnp.exp2(sc - m)
            pb = jnp.exp2(sb - m)
            dens.append(jnp.sum(pc, axis=-1, keepdims=True) + jnp.sum(pb, axis=-1, keepdims=True)
                        + jnp.exp2(sink - m))
            pcs.append(pc.astype(BF16))
            pbs.append(pb.astype(BF16))
        outs = [(jnp.dot(pc, v_ref[0:c_len, ksl], preferred_element_type=F32)
                 + jnp.dot(pb, v_ref[pl.ds(start, band), ksl], preferred_element_type=F32)) / den
                for (_, _, ksl), pc, pb, den in zip(heads, pcs, pbs, dens)]
        o_ref[...] = jnp.concatenate(outs, axis=-1).astype(o_ref.dtype)


def _win_attention(q, k, v, sink, c_len):
    b, l, qw = q.shape
    kw = k.shape[-1]
    blk = TIME_TILE
    kern = functools.partial(_win_attn_kernel, c_len=c_len, l_len=l, nqc=c_len // blk)
    return pl.pallas_call(
        kern,
        grid=(b, l // blk),
        in_specs=[
            pl.BlockSpec(memory_space=pltpu.SMEM),
            pl.BlockSpec((None, blk, qw), lambda bb, j: (bb, j, 0)),
            pl.BlockSpec((None, l, kw), lambda bb, j: (bb, 0, 0)),
            pl.BlockSpec((None, l, kw), lambda bb, j: (bb, 0, 0)),
        ],
        out_specs=pl.BlockSpec((None, blk, qw), lambda bb, j: (bb, j, 0)),
        out_shape=jax.ShapeDtypeStruct((b, l, qw), BF16),
        compiler_params=_cparams(("arbitrary", "arbitrary")),
        name="window_attention",
    )(sink, q, k, v)


def _out_even_kernel(x_ref, rec_ref, gate_ref, att_ref, wa_ref, wb_ref, g1_ref, *refs):
    route_in, o_ref, route_out = refs[:5], refs[5], refs[6:]
    lru = (rec_ref[0] + rec_ref[1]) * jax.nn.gelu(gate_ref[...])
    y = (jnp.dot(lru.astype(BF16), wa_ref[...], preferred_element_type=F32)
         + jnp.dot(att_ref[...], wb_ref[...], preferred_element_type=F32))
    x = x_ref[...] + g1_ref[...] * y
    o_ref[...] = x
    _route(x, *route_in, *route_out)


def _out_even(xc, rec2, gate, att, w_a, w_b, mods3, layer, nct, g_ffn, w_router, b_router):
    b, l, d = xc.shape
    tm = ROW_TILE
    w = gate.shape[-1]
    row = lambda bb, i: jnp.where(i < nct, SUBLANES, bb)
    r_in, r_args, r_out, r_shapes, r_scratch = _route_plumbing(b, l, d, g_ffn, mods3, layer, row, w_router, b_router)
    return pl.pallas_call(
        _out_even_kernel,
        grid=(b, l // tm),
        in_specs=[
            pl.BlockSpec((None, tm, d), lambda bb, i: (bb, i, 0)),
            pl.BlockSpec((2, tm, w), lambda bb, i: (0, i, bb)),
            pl.BlockSpec((None, tm, w), lambda bb, i: (bb, i, 0)),
            pl.BlockSpec((None, tm, att.shape[-1]), lambda bb, i: (bb, i, 0)),
            pl.BlockSpec(w_a.shape, lambda bb, i: (0, 0)),
            pl.BlockSpec(w_b.shape, lambda bb, i: (0, 0)),
            _mod_spec(d, layer, 2, row),
        ] + r_in,
        out_specs=[pl.BlockSpec((None, tm, d), lambda bb, i: (bb, i, 0))] + r_out,
        out_shape=[jax.ShapeDtypeStruct((b, l, d), F32)] + r_shapes,
        scratch_shapes=r_scratch,
        compiler_params=_cparams(("arbitrary", "arbitrary")),
        name="out_proj_even",
    )(xc, rec2, gate, att, w_a, w_b, mods3, *r_args)


def _diff_attn_kernel(lam_ref, g_ref, q_ref, k_ref, v_ref, o_ref, *, lam_init):
    lv = lam_ref[...]
    lam = (jnp.exp(jnp.sum(lv[0:1] * lv[1:2], axis=-1, keepdims=True))
           - jnp.exp(jnp.sum(lv[2:3] * lv[3:4], axis=-1, keepdims=True)) + lam_init)
    vw = 2 * DIFF_DH

    def logits(h, mp):
        lo = h * vw + mp * DIFF_DH
        return _nt_dot(q_ref[:, lo:lo + DIFF_DH], k_ref[:, lo:lo + DIFF_DH])

    def softmax_parts(s):
        e = jnp.exp2(s - jnp.max(s, axis=-1, keepdims=True))
        return e, 1.0 / jnp.sum(e, axis=-1, keepdims=True)

    heads = range(DIFF_HEADS)
    ls = [(logits(h, 0), logits(h, 1)) for h in heads]
    ws = []
    for l0, l1 in ls:
        e0, r0 = softmax_parts(l0)
        e1, r1 = softmax_parts(l1)
        ws.append((e0 * r0 - e1 * (lam * r1)).astype(BF16))
    for h, w in zip(heads, ws):
        lo = h * vw
        o = jnp.dot(w, v_ref[:, lo:lo + vw], preferred_element_type=F32)
        ms = jnp.mean(o * o, axis=-1, keepdims=True)
        o = o * lax.rsqrt(ms + NORM_EPS) * g_ref[...]
        o_ref[:, lo:lo + vw] = (o * (1.0 - lam_init)).astype(o_ref.dtype)


def _diff_attention(q, k, v, lam_vecs, subln_g, lam_init, c_len):
    b, l, w = q.shape
    tq = ROW_TILE
    s_len = l - c_len
    off = c_len // tq
    return pl.pallas_call(
        functools.partial(_diff_attn_kernel, lam_init=lam_init),
        grid=(b, s_len // tq),
        in_specs=[
            pl.BlockSpec(lam_vecs.shape, lambda bb, j: (0, 0)),
            pl.BlockSpec((1, 2 * DIFF_DH), lambda bb, j: (0, 0)),
            pl.BlockSpec((None, tq, w), lambda bb, j: (bb, j + off, 0)),
            pl.BlockSpec((None, l, w), lambda bb, j: (bb, 0, 0)),
            pl.BlockSpec((None, l, w), lambda bb, j: (bb, 0, 0)),
        ],
        out_specs=pl.BlockSpec((None, tq, w), lambda bb, j: (bb, j, 0)),
        out_shape=jax.ShapeDtypeStruct((b, s_len, w), BF16),
        compiler_params=_cparams(("arbitrary", "arbitrary")),
        name="diff_attention",
    )(lam_vecs, subln_g.reshape(1, -1), q, k, v)


def _ssd_kernel(x_ref, xp_ref, xn_ref, dt_ref, cw_ref, cb_ref, dtb_ref, alog_ref, dsk_ref, o_ref,
                ext_scr, st_scr, *, q, nct, nt):
    d = pl.program_id(0)
    g = pl.program_id(2)
    tile = _seq_tile(d, g, nct, nt)
    pv = jnp.where((tile == 0) | (tile == nct), 0.0, 1.0)
    nv = jnp.where((tile == nct - 1) | (tile == nt - 1), 0.0, 1.0)
    ext_scr[0:SUBLANES] = xp_ref[...] * pv
    ext_scr[SUBLANES:SUBLANES + q] = x_ref[...]
    ext_scr[SUBLANES + q:2 * SUBLANES + q] = xn_ref[...] * nv

    @pl.when(g == 0)
    def _():
        st_scr[...] = jnp.zeros_like(st_scr)

    u = cb_ref[...] + cw_ref[0] * ext_scr[SUBLANES - 1:SUBLANES - 1 + q, :]
    for j in range(1, CONV_W):
        u = u + cw_ref[j] * ext_scr[SUBLANES - 1 + j:SUBLANES - 1 + j + q, :]
    act = _silu(u)

    dtr = dt_ref[...]
    dtr = jnp.where(d == 0, dtr, pltpu.roll(dtr, LANES - SSD_HEADS, 1))
    dtv = _softplus(dtr + dtb_ref[...])
    head_lane = lax.broadcasted_iota(jnp.int32, (1, LANES), 1) < SSD_HEADS
    dta = dtv * jnp.where(head_lane, -jnp.exp(alog_ref[...]), 0.0)
    ri = lax.broadcasted_iota(jnp.int32, (q, q), 0)
    ci = lax.broadcasted_iota(jnp.int32, (q, q), 1)
    keep = jnp.where(d == 0, ri - ci, ci - ri) >= 0
    cum = jnp.dot(keep.astype(F32), dta, preferred_element_type=F32, precision=HIGHEST)
    tot = jnp.sum(dta, axis=0, keepdims=True)
    cum_t = cum.T
    dt_t = dtv.T
    to_end = jnp.exp(tot - cum) * dtv
    e_cum = jnp.exp(cum)
    e_tot = jnp.exp(tot)
    dskip = dsk_ref[...] * jnp.where(d == 0, 1.0, 0.0)

    hpg = SSD_HEADS // SSD_GROUPS
    for gi in range(SSD_GROUPS):
        b_g = act[:, SSD_INNER + gi * SSD_STATE:SSD_INNER + (gi + 1) * SSD_STATE]
        c_lo = SSD_INNER + SSD_GROUPS * SSD_STATE + gi * SSD_STATE
        c_g = act[:, c_lo:c_lo + SSD_STATE].astype(BF16)
        cb = _nt_dot(c_g, b_g.astype(BF16))
        b_gt = b_g.T.astype(BF16)
        for hh in range(hpg):
            h = gi * hpg + hh
            xs = act[:, h * SSD_HEAD_DIM:(h + 1) * SSD_HEAD_DIM]
            seg = cum[:, h:h + 1] - cum_t[h:h + 1, :]
            decay = jnp.exp(jnp.where(keep, seg, NEG_BIG))
            w = (cb * decay * dt_t[h:h + 1, :]).astype(BF16)
            state = st_scr[h]
            y = jnp.dot(w, xs.astype(BF16), preferred_element_type=F32)
            y = y + jnp.dot(c_g, state.astype(BF16), preferred_element_type=F32) * e_cum[:, h:h + 1]
            y = y + dskip[:, h * SSD_HEAD_DIM:(h + 1) * SSD_HEAD_DIM] * xs
            o_ref[:, h * SSD_HEAD_DIM:(h + 1) * SSD_HEAD_DIM] = y
            s_new = jnp.dot(b_gt, (xs * to_end[:, h:h + 1]).astype(BF16), preferred_element_type=F32)
            st_scr[h] = e_tot[:, h:h + 1] * state + s_new


def _ssd(xbc, dt, conv_w, conv_b, dt_bias, a_log, d_skip, nct):
    b, l, cd = xbc.shape
    q = TIME_TILE
    nt = l // q
    tile = lambda d, bb, g: _seq_tile(d, g, nct, nt)
    r8 = q // SUBLANES
    pad = LANES - SSD_HEADS
    dtb = jnp.pad(dt_bias, ((0, 0), (0, pad))).reshape(2, 1, LANES)
    alog = jnp.pad(a_log, ((0, 0), (0, pad))).reshape(2, 1, LANES)
    dsk = jnp.repeat(d_skip, SSD_HEAD_DIM).reshape(1, SSD_INNER)
    return pl.pallas_call(
        functools.partial(_ssd_kernel, q=q, nct=nct, nt=nt),
        grid=(2, b, nt),
        in_specs=[
            pl.BlockSpec((None, q, cd), lambda d, bb, g: (bb, tile(d, bb, g), 0)),
            pl.BlockSpec((None, SUBLANES, cd),
                         lambda d, bb, g: (bb, jnp.maximum(tile(d, bb, g) * r8 - 1, 0), 0)),
            pl.BlockSpec((None, SUBLANES, cd),
                         lambda d, bb, g: (bb, jnp.minimum((tile(d, bb, g) + 1) * r8, l // SUBLANES - 1), 0)),
            pl.BlockSpec((None, q, LANES), lambda d, bb, g: (bb, tile(d, bb, g), 0)),
            pl.BlockSpec((CONV_W, 1, cd), lambda d, bb, g: (0, 0, 0)),
            pl.BlockSpec((1, cd), lambda d, bb, g: (0, 0)),
            pl.BlockSpec((None, 1, LANES), lambda d, bb, g: (d, 0, 0)),
            pl.BlockSpec((None, 1, LANES), lambda d, bb, g: (d, 0, 0)),
            pl.BlockSpec((1, SSD_INNER), lambda d, bb, g: (0, 0)),
        ],
        out_specs=pl.BlockSpec((None, None, q, SSD_INNER), lambda d, bb, g: (d, bb, tile(d, bb, g), 0)),
        out_shape=jax.ShapeDtypeStruct((2, b, l, SSD_INNER), F32),
        scratch_shapes=[
            pltpu.VMEM((q + 2 * SUBLANES, cd), F32),
            pltpu.VMEM((SSD_HEADS, SSD_STATE, SSD_HEAD_DIM), F32),
        ],
        compiler_params=_cparams(("arbitrary", "arbitrary", "arbitrary")),
        name="ssd_chunked",
    )(xbc, xbc, xbc, dt, conv_w.reshape(CONV_W, 1, cd), conv_b.reshape(1, cd), dtb, alog, dsk)


def _out_odd_kernel(x_ref, diff_ref, y_ref, z_ref, ng_ref, wa_ref, wb_ref, g1_ref, *refs):
    route_in, o_ref, route_out = refs[:5], refs[5], refs[6:]
    yz = (y_ref[0] + y_ref[1]) * _silu(z_ref[...])
    gs = SSD_INNER // SSD_GROUPS
    parts = []
    for gi in range(SSD_GROUPS):
        seg = yz[:, gi * gs:(gi + 1) * gs]
        ms = jnp.mean(seg * seg, axis=-1, keepdims=True)
        parts.append(seg * lax.rsqrt(ms + NORM_EPS) * ng_ref[:, gi * gs:(gi + 1) * gs])
    ssd = jnp.concatenate(parts, axis=-1).astype(BF16)
    y = (jnp.dot(diff_ref[...], wa_ref[...], preferred_element_type=F32)
         + jnp.dot(ssd, wb_ref[...], preferred_element_type=F32))
    x = x_ref[...] + g1_ref[...] * y
    o_ref[...] = x
    _route(x, *route_in, *route_out)


def _out_odd(xc, diff, y2, z, norm_g, w_a, w_b, mods3, layer, c_len, g_ffn, w_router, b_router):
    b, l, d = xc.shape
    s_len = l - c_len
    tm = ROW_TILE
    off = c_len // tm
    w = SSD_INNER
    row = lambda bb, i: bb
    r_in, r_args, r_out, r_shapes, r_scratch = _route_plumbing(b, s_len, d, g_ffn, mods3, layer, row, w_router,
                                                               b_router)
    return pl.pallas_call(
        _out_odd_kernel,
        grid=(b, s_len // tm),
        in_specs=[
            pl.BlockSpec((None, tm, d), lambda bb, i: (bb, i + off, 0)),
            pl.BlockSpec((None, tm, diff.shape[-1]), lambda bb, i: (bb, i, 0)),
            pl.BlockSpec((2, None, tm, w), lambda bb, i: (0, bb, i + off, 0)),
            pl.BlockSpec((None, tm, w), lambda bb, i: (bb, i + off, 0)),
            pl.BlockSpec((1, w), lambda bb, i: (0, 0)),
            pl.BlockSpec(w_a.shape, lambda bb, i: (0, 0)),
            pl.BlockSpec(w_b.shape, lambda bb, i: (0, 0)),
            _mod_spec(d, layer, 2, row),
        ] + r_in,
        out_specs=[pl.BlockSpec((None, tm, d), lambda bb, i: (bb, i, 0))] + r_out,
        out_shape=[jax.ShapeDtypeStruct((b, s_len, d), F32)] + r_shapes,
        scratch_shapes=r_scratch,
        compiler_params=_cparams(("arbitrary", "arbitrary")),
        name="out_proj_odd",
    )(xc, diff, y2, z, norm_g.reshape(1, w), w_a, w_b, mods3, *r_args)


def _route(x, g_ref, sh_ref, sc_ref, wr_ref, br_ref, h_ref, eid_ref, rnk_ref, gate_ref, cnt_ref, carry_scr):
    @pl.when((pl.program_id(0) == 0) & (pl.program_id(1) == 0))
    def _():
        carry_scr[...] = jnp.zeros_like(carry_scr)

    h = _norm_mod(x, g_ref[...], sh_ref[...], sc_ref[...])
    _store_chunk_rows(h_ref, _pack_bf16(h))
    tm = h.shape[0]
    per = N_EXPERTS // N_EXPERT_GROUPS
    logits = lax.dot_general(wr_ref[...], h, (((1,), (1,)), ((), ())),
                             preferred_element_type=F32, precision=HIGHEST)
    scores = jax.nn.sigmoid(logits)
    sel = scores + br_ref[...]
    sel3 = sel.reshape(N_EXPERT_GROUPS, per, tm)
    kio = lax.broadcasted_iota(jnp.int32, sel3.shape, 1)
    m1 = jnp.max(sel3, axis=1, keepdims=True)
    first = jnp.min(jnp.where(sel3 == m1, kio, per), axis=1, keepdims=True)
    m2 = jnp.max(jnp.where(kio == first, NEG_BIG, sel3), axis=1, keepdims=True)
    gs = m1 + m2
    gio = lax.broadcasted_iota(jnp.int32, gs.shape, 0)
    ahead = jnp.zeros(gs.shape, jnp.int32)
    for gp in range(N_EXPERT_GROUPS):
        other = gs[gp:gp + 1]
        ahead = ahead + jnp.where((other > gs) | ((other == gs) & (gp < gio)), 1, 0)
    grp_on = jnp.where(ahead < TOPK_GROUPS, 1.0, 0.0)
    selm = jnp.where(jnp.broadcast_to(grp_on, sel3.shape) > 0.5, sel3, NEG_BIG).reshape(N_EXPERTS, tm)
    eio = lax.broadcasted_iota(jnp.int32, selm.shape, 0)
    work = selm
    cf = jnp.zeros(selm.shape, F32)
    e_rows, s_rows = [], []
    for k in range(TOP_K):
        best = jnp.max(work, axis=0, keepdims=True)
        idx = jnp.min(jnp.where(work == best, eio, N_EXPERTS), axis=0, keepdims=True)
        hit = eio == idx
        cf = cf + jnp.where(hit, 1.0, 0.0)
        work = jnp.where(hit, NEG_BIG, work)
        e_rows.append(idx)
        s_rows.append(jnp.sum(jnp.where(hit, scores, 0.0), axis=0, keepdims=True))
    denom = s_rows[0]
    for s_k in s_rows[1:]:
        denom = denom + s_k
    g_rows = [s_k / denom * ROUTED_SCALE for s_k in s_rows]
    ti = lax.broadcasted_iota(jnp.int32, (tm, tm), 0)
    tj = lax.broadcasted_iota(jnp.int32, (tm, tm), 1)
    before = jnp.where(ti < tj, 1.0, 0.0).astype(BF16)
    in_expert = carry_scr[:, 0:1] + jnp.dot(cf.astype(BF16), before, preferred_element_type=F32)
    carry_scr[...] = carry_scr[...] + jnp.sum(cf, axis=1, keepdims=True)
    cnt_ref[...] = carry_scr[...]
    r_rows = [jnp.sum(jnp.where(eio == idx, in_expert, 0.0), axis=0, keepdims=True) for idx in e_rows]
    eid_ref[...] = jnp.concatenate(e_rows, axis=0)
    rnk_ref[...] = jnp.concatenate(r_rows, axis=0).astype(jnp.int32)
    padded = jnp.concatenate(g_rows + [jnp.zeros((LANES - TOP_K, tm), F32)], axis=0)
    gate_ref[...] = padded.T


def _route_plumbing(b, r, d, g, mods3, layer, row_fn, w_router, b_router):
    tm = ROW_TILE
    nt = r // tm
    w_router_t = w_router.T
    slot = pl.BlockSpec((TOP_K, tm), lambda bb, i: (0, bb * nt + i))
    slot_shape = jax.ShapeDtypeStruct((TOP_K, b * r), jnp.int32)
    in_specs = [
        pl.BlockSpec((1, d), lambda bb, i: (0, 0)),
        _mod_spec(d, layer, 3, row_fn),
        _mod_spec(d, layer, 4, row_fn),
        pl.BlockSpec(w_router_t.shape, lambda bb, i: (0, 0)),
        pl.BlockSpec((N_EXPERTS, 1), lambda bb, i: (0, 0)),
    ]
    args = [g.reshape(1, d), mods3, mods3, w_router_t, b_router.reshape(N_EXPERTS, 1)]
    out_specs = [
        pl.BlockSpec((tm * ROW_CHUNKS, LANES), lambda bb, i: (bb * nt + i, 0)),
        slot,
        slot,
        pl.BlockSpec((None, tm, LANES), lambda bb, i: (bb, i, 0)),
        pl.BlockSpec((N_EXPERTS, LANES), lambda bb, i: (0, 0)),
    ]
    out_shapes = [jax.ShapeDtypeStruct((b * r * ROW_CHUNKS, LANES), jnp.uint32), slot_shape, slot_shape,
                  jax.ShapeDtypeStruct((b, r, LANES), F32), jax.ShapeDtypeStruct((N_EXPERTS, LANES), F32)]
    scratch = [pltpu.VMEM((N_EXPERTS, LANES), F32)]
    return in_specs, args, out_specs, out_shapes, scratch


def _moe_plan(counts, n_rows):
    blk = EXPERT_BLK
    nb = n_rows // blk
    ends = jnp.cumsum(counts)
    starts = ends - counts
    count_le = lambda sorted_vals, q: jnp.sum(sorted_vals[None, :] <= q[:, None], axis=1, dtype=jnp.int32)
    first = jnp.arange(nb, dtype=jnp.int32) * blk
    e_lo = count_le(ends, first)
    e_hi = count_le(ends, first + (blk - 1))
    n_pair = e_hi - e_lo + 1
    p_end = jnp.cumsum(n_pair)
    p_start = p_end - n_pair
    i = jnp.arange(nb + N_EXPERTS - 1, dtype=jnp.int32)
    j = jnp.minimum(count_le(p_end, i), nb - 1)
    valid = i < p_end[-1]
    e = jnp.where(valid, e_lo[j] + i - p_start[j], e_hi[nb - 1]).astype(jnp.int32)
    bounds = jnp.concatenate([starts, ends[-1:]]).astype(jnp.int32)
    return j, e, valid.astype(jnp.int32), bounds


def _positions_kernel(starts_ref, eid_ref, rnk_ref, pos_ref):
    eid = eid_ref[...]
    pos = rnk_ref[...]
    for e in range(N_EXPERTS):
        pos = pos + jnp.where(eid == e, starts_ref[e], 0)
    pos_ref[...] = pos * ROW_CHUNKS


def _positions(eid, rnk, starts):
    full = pl.BlockSpec(eid.shape, lambda: (0, 0))
    return pl.pallas_call(
        _positions_kernel,
        in_specs=[pl.BlockSpec(memory_space=pltpu.SMEM), full, full],
        out_specs=full,
        out_shape=jax.ShapeDtypeStruct(eid.shape, jnp.int32),
        compiler_params=pltpu.CompilerParams(vmem_limit_bytes=VMEM_LIMIT),
        name="moe_positions",
    )(starts, eid, rnk)


def _token_row(ref, first):
    return ref.at[pl.ds(pl.multiple_of(first, ROW_CHUNKS), ROW_CHUNKS)]


def _dispatch_kernel(pos_ref, h_ref, xs_ref, sem):
    tm = h_ref.shape[0] // ROW_CHUNKS

    def issue(t, carry):
        src = _token_row(h_ref, t * ROW_CHUNKS)
        for k in range(TOP_K):
            pltpu.make_async_copy(src, _token_row(xs_ref, pos_ref[k, t]), sem).start(priority=k % 2)
        return carry

    lax.fori_loop(0, tm, issue, 0)
    done = pl.ds(0, tm * ROW_CHUNKS)
    for _ in range(TOP_K):
        pltpu.make_async_copy(h_ref.at[done], xs_ref.at[done], sem).wait()


def _dispatch(h2, pos):
    rows, w = h2.shape
    tm = ROW_TILE
    return pl.pallas_call(
        _dispatch_kernel,
        grid=(rows // (tm * ROW_CHUNKS),),
        in_specs=[
            pl.BlockSpec((TOP_K, tm), lambda i: (0, i), memory_space=pltpu.SMEM),
            pl.BlockSpec((tm * ROW_CHUNKS, w), lambda i: (i, 0)),
        ],
        out_specs=pl.BlockSpec(memory_space=pl.ANY),
        out_shape=jax.ShapeDtypeStruct((rows * TOP_K, w), h2.dtype),
        scratch_shapes=[pltpu.SemaphoreType.DMA],
        compiler_params=_cparams(("arbitrary",)),
        name="moe_dispatch",
    )(pos, h2)


def _grouped_kernel(pb_ref, pe_ref, pv_ref, bnd_ref, xs_ref, wg_ref, wu_ref, wd_ref, y_ref, wgb, wub, wdb):
    i = pl.program_id(0)
    prev = jnp.maximum(i - 1, 0)
    j = pb_ref[i]
    e = pe_ref[i]
    blk = xs_ref.shape[0] // ROW_CHUNKS

    @pl.when((i == 0) | (pb_ref[prev] != j))
    def _():
        y_ref[...] = jnp.zeros_like(y_ref)

    @pl.when((i == 0) | (pe_ref[prev] != e))
    def _():
        wgb[...] = wg_ref[...].astype(BF16)
        wub[...] = wu_ref[...].astype(BF16)
        wdb[...] = wd_ref[...].astype(BF16)

    @pl.when(pv_ref[i] == 1)
    def _():
        xw = _load_chunk_rows(xs_ref, blk)
        a = _packed_dot(xw, wgb)
        u = _packed_dot(xw, wub)
        yv = jnp.dot((_silu(a) * u).astype(BF16), wdb[...], preferred_element_type=F32)
        rows = j * blk + lax.broadcasted_iota(jnp.int32, (blk, 1), 0)
        own = (rows >= bnd_ref[e]) & (rows < bnd_ref[e + 1])
        yw = _pack_bf16(yv)
        for c in range(ROW_CHUNKS):
            sl = pl.ds(c, blk, stride=ROW_CHUNKS)
            y_ref[sl, :] = jnp.where(own, yw[:, c * LANES:(c + 1) * LANES], y_ref[sl, :])


def _grouped(pb, pe, pv, bounds, xs, wg, wu, wd, layer):
    p, half = xs.shape
    d = 2 * ROW_CHUNKS * LANES
    blk = EXPERT_BLK * ROW_CHUNKS
    grid_spec = pltpu.PrefetchScalarGridSpec(
        num_scalar_prefetch=4,
        grid=(pb.shape[0],),
        in_specs=[
            pl.BlockSpec((blk, half), lambda i, pb, pe, pv, bnd: (pb[i], 0)),
            pl.BlockSpec((None, None, d, D_EXPERT), lambda i, pb, pe, pv, bnd: (layer, pe[i], 0, 0)),
            pl.BlockSpec((None, None, d, D_EXPERT), lambda i, pb, pe, pv, bnd: (layer, pe[i], 0, 0)),
            pl.BlockSpec((None, None, D_EXPERT, d), lambda i, pb, pe, pv, bnd: (layer, pe[i], 0, 0)),
        ],
        out_specs=pl.BlockSpec((blk, half), lambda i, pb, pe, pv, bnd: (pb[i], 0)),
        scratch_shapes=[
            pltpu.VMEM((d, D_EXPERT), BF16),
            pltpu.VMEM((d, D_EXPERT), BF16),
            pltpu.VMEM((D_EXPERT, d), BF16),
        ],
    )
    return pl.pallas_call(
        _grouped_kernel,
        grid_spec=grid_spec,
        out_shape=jax.ShapeDtypeStruct((p, half), jnp.uint32),
        compiler_params=_cparams(("arbitrary",)),
        name="moe_grouped_experts",
    )(pb, pe, pv, bounds, xs, wg, wu, wd)


def _combine_kernel(*refs, final):
    pos_ref, y_ref, gate_ref, h_ref, x_ref, g2_ref, sg_ref, su_ref, sd_ref = refs[:9]
    o_ref, buf, sem = refs[-3:]
    tm = x_ref.shape[0]

    def issue(t, carry):
        for k in range(TOP_K):
            pltpu.make_async_copy(_token_row(y_ref, pos_ref[k, t]), _token_row(buf.at[k], t * ROW_CHUNKS),
                                  sem).start(priority=k % 2)
        return carry

    lax.fori_loop(0, tm, issue, 0)
    hw = _load_chunk_rows(h_ref, tm)
    a = _packed_dot(hw, sg_ref)
    u = _packed_dot(hw, su_ref)
    acc = jnp.dot((_silu(a) * u).astype(BF16), sd_ref[...], preferred_element_type=F32)
    done = pl.ds(0, tm * ROW_CHUNKS)
    for k in range(TOP_K):
        pltpu.make_async_copy(y_ref.at[done], buf.at[k, done], sem).wait()
    g = gate_ref[...]
    half = hw.shape[-1]
    acc_hi = acc[:, :half]
    acc_lo = acc[:, half:]
    for k in range(TOP_K):
        hi, lo = _unpack_bf16(_load_chunk_rows(buf.at[k], tm))
        acc_hi = acc_hi + g[:, k:k + 1] * hi
        acc_lo = acc_lo + g[:, k:k + 1] * lo
    x = x_ref[...] + g2_ref[...] * jnp.concatenate([acc_hi, acc_lo], axis=-1)
    if final:
        gf_ref = refs[9]
        ms = jnp.mean(x * x, axis=-1, keepdims=True)
        x = x * lax.rsqrt(ms + NORM_EPS) * gf_ref[...]
    o_ref[...] = x


def _combine(pos, y, gates, h2, x, mods3, layer, row_fn, sg, su, sd, g_final=None):
    b, r, d = x.shape
    tm = ROW_TILE
    nt = r // tm
    tile = pl.BlockSpec((None, tm, d), lambda bb, i: (bb, i, 0))
    in_specs = [
        pl.BlockSpec((TOP_K, tm), lambda bb, i: (0, bb * nt + i), memory_space=pltpu.SMEM),
        pl.BlockSpec(memory_space=pl.ANY),
        pl.BlockSpec((None, tm, LANES), lambda bb, i: (bb, i, 0)),
        pl.BlockSpec((tm * ROW_CHUNKS, LANES), lambda bb, i: (bb * nt + i, 0)),
        tile,
        _mod_spec(d, layer, 5, row_fn),
        pl.BlockSpec(sg.shape, lambda bb, i: (0, 0)),
        pl.BlockSpec(su.shape, lambda bb, i: (0, 0)),
        pl.BlockSpec(sd.shape, lambda bb, i: (0, 0)),
    ]
    args = [pos, y, gates, h2, x, mods3, sg, su, sd]
    if g_final is not None:
        in_specs.append(pl.BlockSpec((1, d), lambda bb, i: (0, 0)))
        args.append(g_final.reshape(1, d))
    return pl.pallas_call(
        functools.partial(_combine_kernel, final=g_final is not None),
        grid=(b, nt),
        in_specs=in_specs,
        out_specs=tile,
        out_shape=jax.ShapeDtypeStruct((b, r, d), F32),
        scratch_shapes=[pltpu.VMEM((TOP_K, tm * ROW_CHUNKS, LANES), jnp.uint32), pltpu.SemaphoreType.DMA],
        compiler_params=_cparams(("arbitrary", "arbitrary")),
        name="moe_combine",
    )(*args)


def _moe(x, routed, mods3, layer, row_fn, w_e_gate, w_e_up, w_e_down, ws_gate, ws_up, ws_down, g_final=None):
    b, r, d = x.shape
    h2, eid, rnk, gates, cnt = routed
    pb, pe, pv, bounds = _moe_plan(cnt[:, 0].astype(jnp.int32), b * r * TOP_K)
    pos = _positions(eid, rnk, bounds[:N_EXPERTS])
    xs = _dispatch(h2, pos)
    y = _grouped(pb, pe, pv, bounds, xs, w_e_gate, w_e_up, w_e_down, layer)
    return _combine(pos, y, gates, h2, x, mods3, layer, row_fn,
                    ws_gate.astype(BF16), ws_up.astype(BF16), ws_down.astype(BF16), g_final)


def kernel(x, c, ctx, c_ctx, w_mod, b_mod, g_mix, g_ffn, g_final, ab_w_in, ab_w_out, ab_conv_w, ab_conv_b, ab_w_r, ab_b_r, ab_w_i, ab_b_i, ab_lam, ab_sink, cd_w_in, cd_w_out, cd_lam, cd_subln_g, cd_conv_w, cd_conv_b, cd_dt_bias, cd_a_log, cd_d_skip, cd_norm_g, w_router, b_router, w_e_gate, w_e_up, w_e_down, ws_gate, ws_up, ws_down):
    bsz, s_len, d = x.shape
    c_len = ctx.shape[1]
    depth = w_mod.shape[0]
    assert depth == 2 and bsz == SUBLANES, "kernels are specialised to depth 2 and batch 8"
    assert c_len % ROW_TILE == 0 and s_len % ROW_TILE == 0
    nct_row = c_len // ROW_TILE
    nct_time = c_len // TIME_TILE

    c_all = jnp.concatenate([c, c_ctx[None], jnp.zeros((MOD_ROWS - bsz - 1, d), F32)], axis=0)
    mods3 = _modulations(c_all, w_mod, b_mod).reshape(depth * MOD_ROWS, 1, N_MOD * d)
    rope_tabs = _rope_tables(c_len, s_len)
    xc = jnp.concatenate([ctx, x], axis=1)
    row_mixed = lambda bb, i: jnp.where(i < nct_row, SUBLANES, bb)
    row_latent = lambda bb, i: bb

    w_in = ab_w_in[0].astype(BF16)
    q_hi = LRU_WIDTH + WIN_HEADS * HEAD_DIM
    x_hi = q_hi + LRU_WIDTH
    k_hi = x_hi + WIN_KV_HEADS * HEAD_DIM
    gate, q, xa, k, v = _project(xc, g_mix[0], mods3, 0, nct_row, rope_tabs, [
        (w_in[:, :LRU_WIDTH], None, F32, False),
        (w_in[:, LRU_WIDTH:q_hi], HEAD_DIM ** -0.5 * LOG2E, BF16, False),
        (w_in[:, q_hi:x_hi], None, F32, True),
        (w_in[:, x_hi:k_hi], 1.0, BF16, False),
        (w_in[:, k_hi:], None, BF16, False),
    ])
    l_len = c_len + s_len
    w_gates = jnp.stack([jnp.concatenate([_block_diag(ab_w_r[0, dd]), _block_diag(ab_w_i[0, dd])], axis=1)
                         for dd in range(2)]).astype(BF16)
    b_gates = jnp.concatenate([ab_b_r[0], ab_b_i[0]], axis=-1).reshape(2, 1, 2 * LRU_WIDTH)
    rec = _rglru(xa.reshape(l_len, bsz, LRU_WIDTH), ab_conv_w[0], ab_conv_b[0], w_gates, b_gates,
                 ab_lam[0].reshape(2, 1, LRU_WIDTH), nct_time)
    att = _win_attention(q, k, v, ab_sink[0], c_len)
    w_out = ab_w_out[0].astype(BF16)
    xc, *routed = _out_even(xc, rec.reshape(2, l_len, bsz * LRU_WIDTH), gate, att, w_out[:LRU_WIDTH],
                            w_out[LRU_WIDTH:], mods3, 0, nct_row, g_ffn[0], w_router[0], b_router[0])
    xc = _moe(xc, routed, mods3, 0, row_mixed, w_e_gate, w_e_up, w_e_down, ws_gate[0], ws_up[0], ws_down[0])

    w_in = cd_w_in[0].astype(BF16)
    qk = DIFF_HEADS * 2 * DIFF_DH
    z_hi = qk + SSD_INNER
    k_hi = z_hi + qk
    v_hi = k_hi + qk
    x_hi = v_hi + SSD_CONV_DIM
    w_dt = jnp.pad(w_in[:, x_hi:], ((0, 0), (0, LANES - 2 * SSD_HEADS)))
    q, z, k, v, xbc, dt = _project(xc, g_mix[1], mods3, 1, nct_row, rope_tabs, [
        (w_in[:, :qk], DIFF_DH ** -0.5 * LOG2E, BF16, False),
        (w_in[:, qk:z_hi], None, F32, False),
        (w_in[:, z_hi:k_hi], 1.0, BF16, False),
        (w_in[:, k_hi:v_hi], None, BF16, False),
        (w_in[:, v_hi:x_hi], None, F32, False),
        (w_dt, None, F32, False),
    ])
    lam_init = 0.8 - 0.6 * math.exp(-0.3 * 1)
    diff = _diff_attention(q, k, v, cd_lam[0], cd_subln_g[0], lam_init, c_len)
    y2 = _ssd(xbc, dt, cd_conv_w[0], cd_conv_b[0], cd_dt_bias[0], cd_a_log[0], cd_d_skip[0], nct_time)
    w_out = cd_w_out[0].astype(BF16)
    xl, *routed = _out_odd(xc, diff, y2, z, cd_norm_g[0], w_out[:qk], w_out[qk:], mods3, 1, c_len,
                           g_ffn[1], w_router[1], b_router[1])
    return _moe(xl, routed, mods3, 1, row_latent, w_e_gate, w_e_up, w_e_down, ws_gate[1], ws_up[1], ws_down[1],
                g_final=g_final)
```

```python
import functools
import math

import jax
import jax.numpy as jnp
from jax import lax
from jax.experimental import pallas as pl
from jax.experimental.pallas import tpu as pltpu

F32 = jnp.float32
BF16 = jnp.bfloat16
HIGHEST = lax.Precision.HIGHEST

GRID_W = 64
N_MOD = 6
NORM_EPS = 1e-6
ROPE_BASE = 10000.0
CONV_W = 4

LRU_WIDTH = 512
LRU_BLOCKS = 8
LRU_C = 8.0

HEAD_DIM = 64
WIN_HEADS = 8
WIN_KV_HEADS = 2
WINDOW = 128

DIFF_HEADS = 4
DIFF_DH = 64

SSD_HEADS = 8
SSD_HEAD_DIM = 64
SSD_INNER = SSD_HEADS * SSD_HEAD_DIM
SSD_GROUPS = 2
SSD_STATE = 128
SSD_CONV_DIM = SSD_INNER + 2 * SSD_GROUPS * SSD_STATE

N_EXPERTS = 64
N_EXPERT_GROUPS = 8
TOPK_GROUPS = 4
TOP_K = 8
D_EXPERT = 256
ROUTED_SCALE = 2.5

LANES = 128
SUBLANES = 8
MOD_ROWS = 16
TIME_TILE = 128
ROW_TILE = 256
DISPATCH_TILE = 512
EXPERT_BLK = 1024
ROW_CHUNKS = 4
VMEM_LIMIT = 48 * 1024 * 1024
NEG_BIG = -1e30
LOG2E = math.log2(math.e)


def _cparams(sem):
    return pltpu.CompilerParams(dimension_semantics=sem, vmem_limit_bytes=VMEM_LIMIT)


def _nt_dot(a, b):
    return lax.dot_general(a, b, (((1,), (1,)), ((), ())), preferred_element_type=F32)


def _softplus(x):
    return jnp.maximum(x, 0.0) + jnp.log1p(jnp.exp(-jnp.abs(x)))


def _silu(x):
    return x * jax.nn.sigmoid(x)


def _pack_bf16(x):
    half = x.shape[-1] // 2
    bits = pltpu.bitcast(x.astype(BF16).astype(F32), jnp.uint32)
    return bits[:, :half] | (bits[:, half:] >> 16)


def _unpack_bf16(w):
    hi = pltpu.bitcast(w & jnp.uint32(0xFFFF0000), F32)
    lo = pltpu.bitcast(w << 16, F32)
    return hi, lo


def _store_chunk_rows(ref, w):
    n = w.shape[0]
    for j in range(ROW_CHUNKS):
        ref[pl.ds(j, n, stride=ROW_CHUNKS), :] = w[:, j * LANES:(j + 1) * LANES]


def _load_chunk_rows(ref, n):
    return jnp.concatenate([ref[pl.ds(j, n, stride=ROW_CHUNKS), :] for j in range(ROW_CHUNKS)], axis=1)


def _packed_dot(w, weight_ref):
    half = w.shape[-1]
    hi, lo = _unpack_bf16(w)
    return (jnp.dot(hi.astype(BF16), weight_ref[:half, :], preferred_element_type=F32)
            + jnp.dot(lo.astype(BF16), weight_ref[half:, :], preferred_element_type=F32))


def _mod_kernel(c_ref, w_ref, b_ref, o_ref):
    c = c_ref[...]
    s = _silu(c)
    o_ref[...] = jnp.dot(s, w_ref[...], preferred_element_type=F32, precision=HIGHEST) + b_ref[...]


def _modulations(c_all, w_mod, b_mod):
    depth, d, _ = w_mod.shape
    return pl.pallas_call(
        _mod_kernel,
        grid=(depth, N_MOD),
        in_specs=[
            pl.BlockSpec((MOD_ROWS, d), lambda l, k: (0, 0)),
            pl.BlockSpec((None, d, d), lambda l, k: (l, 0, k)),
            pl.BlockSpec((None, 1, d), lambda l, k: (l, 0, k)),
        ],
        out_specs=pl.BlockSpec((None, MOD_ROWS, d), lambda l, k: (l, 0, k)),
        out_shape=jax.ShapeDtypeStruct((depth, MOD_ROWS, N_MOD * d), F32),
        compiler_params=_cparams(("arbitrary", "arbitrary")),
        name="adaln_modulation",
    )(c_all, w_mod, b_mod.reshape(depth, 1, N_MOD * d))


def _mod_spec(d, layer, chunk, row_fn):
    return pl.BlockSpec((None, 1, d), lambda b, i: (layer * MOD_ROWS + row_fn(b, i), 0, chunk))


def _norm_mod(x, g, sh, sc):
    ms = jnp.mean(x * x, axis=-1, keepdims=True)
    return (x * lax.rsqrt(ms + NORM_EPS) * g) * (1.0 + sc) + sh


def _rope(y, cos, sa, sb):
    n = y.shape[-1]
    half = HEAD_DIM // 2
    return y * cos + pltpu.roll(y, n - half, 1) * sa + pltpu.roll(y, half, 1) * sb


def _proj_kernel(*refs, ropes):
    n = len(ropes)
    x_ref, g_ref, sh_ref, sc_ref, cos_ref, sa_ref, sb_ref = refs[:7]
    w_refs = refs[7:7 + n]
    o_refs = refs[7 + n:]
    h = _norm_mod(x_ref[...], g_ref[...], sh_ref[...], sc_ref[...]).astype(BF16)
    for w_ref, o_ref, rope in zip(w_refs, o_refs, ropes):
        y = jnp.dot(h, w_ref[...], preferred_element_type=F32)
        if rope is not None:
            w = y.shape[-1]
            y = _rope(y, cos_ref[:, :w], sa_ref[:, :w], sb_ref[:, :w])
            if rope != 1.0:
                y = y * rope
        o_ref[...] = y.astype(o_ref.dtype)


def _project(xc, g, mods3, layer, nct, rope_tabs, groups):
    b, l, d = xc.shape
    tm = ROW_TILE
    mod_row = lambda i, bb: layer * MOD_ROWS + jnp.where(i < nct, SUBLANES, bb)
    rw = rope_tabs[0].shape[-1]
    in_specs = [
        pl.BlockSpec((None, tm, d), lambda i, bb: (bb, i, 0)),
        pl.BlockSpec((1, d), lambda i, bb: (0, 0)),
        pl.BlockSpec((None, 1, d), lambda i, bb: (mod_row(i, bb), 0, 0)),
        pl.BlockSpec((None, 1, d), lambda i, bb: (mod_row(i, bb), 0, 1)),
    ] + [pl.BlockSpec((tm, rw), lambda i, bb: (i, 0))] * 3
    out_specs, out_shapes = [], []
    for w, _, dt, time_major in groups:
        n = w.shape[1]
        in_specs.append(pl.BlockSpec((d, n), lambda i, bb: (0, 0)))
        if time_major:
            out_specs.append(pl.BlockSpec((tm, n), lambda i, bb: (i, bb)))
            out_shapes.append(jax.ShapeDtypeStruct((l, b * n), dt))
        else:
            out_specs.append(pl.BlockSpec((None, tm, n), lambda i, bb: (bb, i, 0)))
            out_shapes.append(jax.ShapeDtypeStruct((b, l, n), dt))
    return pl.pallas_call(
        functools.partial(_proj_kernel, ropes=tuple(gp[1] for gp in groups)),
        grid=(l // tm, b),
        in_specs=in_specs,
        out_specs=out_specs,
        out_shape=out_shapes,
        compiler_params=_cparams(("arbitrary", "arbitrary")),
        name="norm_mod_project",
    )(xc, g.reshape(1, d), mods3, mods3, *rope_tabs, *[gp[0] for gp in groups])


def _rope_tables(c_len, s_len):
    rows = s_len // GRID_W
    row = jnp.repeat(jnp.arange(rows), GRID_W).astype(F32)
    col = jnp.tile(jnp.arange(GRID_W), rows).astype(F32)
    n = HEAD_DIM // 4
    inv = ROPE_BASE ** (-jnp.arange(n, dtype=F32) / n)
    ang = jnp.concatenate([row[:, None] * inv, col[:, None] * inv], axis=-1)
    cos, sin = jnp.cos(ang), jnp.sin(ang)
    zero = jnp.zeros_like(sin)
    reps = WIN_HEADS
    cos_t = jnp.tile(jnp.concatenate([cos, cos], axis=-1), (1, reps))
    sa_t = jnp.tile(jnp.concatenate([-sin, zero], axis=-1), (1, reps))
    sb_t = jnp.tile(jnp.concatenate([zero, sin], axis=-1), (1, reps))
    w = cos_t.shape[-1]
    pad1 = jnp.ones((c_len, w), F32)
    pad0 = jnp.zeros((c_len, w), F32)
    return (jnp.concatenate([pad1, cos_t], 0), jnp.concatenate([pad0, sa_t], 0),
            jnp.concatenate([pad0, sb_t], 0))


def _seq_tile(d, g, nct, nt):
    rev = jnp.where(g < nct, nct - 1 - g, nt - 1 - (g - nct))
    return jnp.where(d == 0, g, rev)


def _rglru_kernel(x_ref, xp_ref, xn_ref, cw_ref, cb_ref, w_ref, bias_ref, lam_ref, o_ref,
                  ext_scr, a_scr, b_scr, h_scr, *, ts, nct, nt, sub):
    d = pl.program_id(0)
    g = pl.program_id(1)
    tile = _seq_tile(d, g, nct, nt)
    bsz, width = h_scr.shape
    pv = jnp.where((tile == 0) | (tile == nct), 0.0, 1.0)
    nv = jnp.where((tile == nct - 1) | (tile == nt - 1), 0.0, 1.0)
    ext_scr[0:1] = xp_ref[...] * pv
    ext_scr[1:ts + 1] = x_ref[...]
    ext_scr[ts + 1:ts + 3] = xn_ref[...] * nv

    @pl.when(g == 0)
    def _():
        h_scr[...] = jnp.zeros_like(h_scr)

    neg_sp = -LRU_C * _softplus(-lam_ref[...])

    def prep(c, carry):
        r0 = pl.multiple_of(c * sub, sub)
        e = ext_scr[pl.ds(r0, sub + CONV_W - 1)]
        u = cb_ref[...] + cw_ref[0] * e[0:sub]
        for j in range(1, CONV_W):
            u = u + cw_ref[j] * e[j:j + sub]
        u2 = u.reshape(sub * bsz, width)
        gts = jnp.dot(u2.astype(BF16), w_ref[...], preferred_element_type=F32) + bias_ref[...]
        r = jax.nn.sigmoid(gts[:, :width])
        ig = jax.nn.sigmoid(gts[:, width:])
        log_a = neg_sp * r
        a = jnp.exp(log_a)
        mult = jnp.sqrt(1.0 - a * a)
        a_scr[pl.ds(r0, sub)] = a.reshape(sub, bsz, width)
        b_scr[pl.ds(r0, sub)] = (mult * ig * u2).reshape(sub, bsz, width)
        return carry

    lax.fori_loop(0, ts // sub, prep, 0)

    def step(t, h):
        tt = jnp.where(d == 0, t, ts - 1 - t)
        h = a_scr[tt] * h + b_scr[tt]
        o_ref[tt] = h
        return h

    h_scr[...] = lax.fori_loop(0, ts, step, h_scr[...], unroll=8)


def _rglru(xa_tm, conv_w, conv_b, w_gates, b_gates, lam, nct):
    l, bsz, width = xa_tm.shape
    ts = TIME_TILE
    nt = l // ts
    tile = lambda d, g: _seq_tile(d, g, nct, nt)
    kern = functools.partial(_rglru_kernel, ts=ts, nct=nct, nt=nt, sub=16)
    return pl.pallas_call(
        kern,
        grid=(2, nt),
        in_specs=[
            pl.BlockSpec((ts, bsz, width), lambda d, g: (tile(d, g), 0, 0)),
            pl.BlockSpec((1, bsz, width), lambda d, g: (jnp.maximum(tile(d, g) * ts - 1, 0), 0, 0)),
            pl.BlockSpec((2, bsz, width),
                         lambda d, g: (jnp.minimum((tile(d, g) + 1) * (ts // 2), l // 2 - 1), 0, 0)),
            pl.BlockSpec((CONV_W, 1, width), lambda d, g: (0, 0, 0)),
            pl.BlockSpec((1, width), lambda d, g: (0, 0)),
            pl.BlockSpec((None, width, 2 * width), lambda d, g: (d, 0, 0)),
            pl.BlockSpec((None, 1, 2 * width), lambda d, g: (d, 0, 0)),
            pl.BlockSpec((None, 1, width), lambda d, g: (d, 0, 0)),
        ],
        out_specs=pl.BlockSpec((None, ts, bsz, width), lambda d, g: (d, tile(d, g), 0, 0)),
        out_shape=jax.ShapeDtypeStruct((2, l, bsz, width), F32),
        scratch_shapes=[
            pltpu.VMEM((ts + CONV_W - 1, bsz, width), F32),
            pltpu.VMEM((ts, bsz, width), F32),
            pltpu.VMEM((ts, bsz, width), F32),
            pltpu.VMEM((bsz, width), F32),
        ],
        compiler_params=_cparams(("arbitrary", "arbitrary")),
        name="rglru_scan",
    )(xa_tm, xa_tm, xa_tm, conv_w.reshape(CONV_W, 1, width), conv_b.reshape(1, width),
      w_gates, b_gates, lam)


def _block_diag(w):
    nb, c, dd = w.shape
    eye = jnp.eye(nb, dtype=w.dtype)
    return (eye[:, None, :, None] * w[:, :, None, :]).reshape(nb * c, nb * dd)


def _win_attn_kernel(sink_ref, q_ref, k_ref, v_ref, o_ref, *, c_len, l_len, nqc):
    j = pl.program_id(1)
    blk = q_ref.shape[0]
    grp = WIN_HEADS // WIN_KV_HEADS
    band = 3 * blk
    heads = [(h, slice(h * HEAD_DIM, (h + 1) * HEAD_DIM),
              slice((h // grp) * HEAD_DIM, (h // grp + 1) * HEAD_DIM)) for h in range(WIN_HEADS)]

    @pl.when(j < nqc)
    def _():
        logits = [_nt_dot(q_ref[:, hsl], k_ref[0:c_len, ksl]) for _, hsl, ksl in heads]
        probs, dens = [], []
        for (h, _, _), s in zip(heads, logits):
            sink = sink_ref[h] * LOG2E
            m = jnp.maximum(jnp.max(s, axis=-1, keepdims=True), sink)
            p = jnp.exp2(s - m)
            dens.append(jnp.sum(p, axis=-1, keepdims=True) + jnp.exp2(sink - m))
            probs.append(p.astype(BF16))
        outs = [jnp.dot(p, v_ref[0:c_len, ksl], preferred_element_type=F32) / den
                for (_, _, ksl), p, den in zip(heads, probs, dens)]
        o_ref[...] = jnp.concatenate(outs, axis=-1).astype(o_ref.dtype)

    @pl.when(j >= nqc)
    def _():
        jb = j - nqc
        start = jnp.clip(c_len + (jb - 1) * blk, c_len - blk, l_len - band)
        start = pl.multiple_of(start, blk)
        qpos = jb * blk + lax.broadcasted_iota(jnp.int32, (blk, band), 0)
        kpos = start - c_len + lax.broadcasted_iota(jnp.int32, (blk, band), 1)
        valid = (jnp.abs(qpos - kpos) <= WINDOW) & (kpos >= 0)
        lc = [_nt_dot(q_ref[:, hsl], k_ref[0:c_len, ksl]) for _, hsl, ksl in heads]
        lb = [jnp.where(valid, _nt_dot(q_ref[:, hsl], k_ref[pl.ds(start, band), ksl]), NEG_BIG)
              for _, hsl, ksl in heads]
        pcs, pbs, dens = [], [], []
        for (h, _, _), sc, sb in zip(heads, lc, lb):
            sink = sink_ref[h] * LOG2E
            m = jnp.maximum(jnp.maximum(jnp.max(sc, axis=-1, keepdims=True),
                                        jnp.max(sb, axis=-1, keepdims=True)), sink)
            pc = jnp.exp2(sc - m)
            pb = jnp.exp2(sb - m)
            dens.append(jnp.sum(pc, axis=-1, keepdims=True) + jnp.sum(pb, axis=-1, keepdims=True)
                        + jnp.exp2(sink - m))
            pcs.append(pc.astype(BF16))
            pbs.append(pb.astype(BF16))
        outs = [(jnp.dot(pc, v_ref[0:c_len, ksl], preferred_element_type=F32)
                 + jnp.dot(pb, v_ref[pl.ds(start, band), ksl], preferred_element_type=F32)) / den
                for (_, _, ksl), pc, pb, den in zip(heads, pcs, pbs, dens)]
        o_ref[...] = jnp.concatenate(outs, axis=-1).astype(o_ref.dtype)


def _win_attention(q, k, v, sink, c_len):
    b, l, qw = q.shape
    kw = k.shape[-1]
    blk = TIME_TILE
    kern = functools.partial(_win_attn_kernel, c_len=c_len, l_len=l, nqc=c_len // blk)
    return pl.pallas_call(
        kern,
        grid=(b, l // blk),
        in_specs=[
            pl.BlockSpec(memory_space=pltpu.SMEM),
            pl.BlockSpec((None, blk, qw), lambda bb, j: (bb, j, 0)),
            pl.BlockSpec((None, l, kw), lambda bb, j: (bb, 0, 0)),
            pl.BlockSpec((None, l, kw), lambda bb, j: (bb, 0, 0)),
        ],
        out_specs=pl.BlockSpec((None, blk, qw), lambda bb, j: (bb, j, 0)),
        out_shape=jax.ShapeDtypeStruct((b, l, qw), BF16),
        compiler_params=_cparams(("arbitrary", "arbitrary")),
        name="window_attention",
    )(sink, q, k, v)


def _out_even_kernel(x_ref, rec_ref, gate_ref, att_ref, wa_ref, wb_ref, g1_ref, *refs):
    route_in, o_ref, route_out = refs[:5], refs[5], refs[6:]
    lru = (rec_ref[0] + rec_ref[1]) * jax.nn.gelu(gate_ref[...].astype(F32))
    y = (jnp.dot(lru.astype(BF16), wa_ref[...], preferred_element_type=F32)
         + jnp.dot(att_ref[...], wb_ref[...], preferred_element_type=F32))
    x = x_ref[...] + g1_ref[...] * y
    o_ref[...] = x
    _route(x, *route_in, *route_out)


def _out_even(xc, rec2, gate, att, w_a, w_b, mods3, layer, nct, g_ffn, w_router, b_router):
    b, l, d = xc.shape
    tm = ROW_TILE
    w = gate.shape[-1]
    row = lambda bb, i: jnp.where(i < nct, SUBLANES, bb)
    r_in, r_args, r_out, r_shapes, r_scratch = _route_plumbing(b, l, d, g_ffn, mods3, layer, row, w_router, b_router)
    return pl.pallas_call(
        _out_even_kernel,
        grid=(b, l // tm),
        in_specs=[
            pl.BlockSpec((None, tm, d), lambda bb, i: (bb, i, 0)),
            pl.BlockSpec((2, tm, w), lambda bb, i: (0, i, bb)),
            pl.BlockSpec((None, tm, w), lambda bb, i: (bb, i, 0)),
            pl.BlockSpec((None, tm, att.shape[-1]), lambda bb, i: (bb, i, 0)),
            pl.BlockSpec(w_a.shape, lambda bb, i: (0, 0)),
            pl.BlockSpec(w_b.shape, lambda bb, i: (0, 0)),
            _mod_spec(d, layer, 2, row),
        ] + r_in,
        out_specs=[pl.BlockSpec((None, tm, d), lambda bb, i: (bb, i, 0))] + r_out,
        out_shape=[jax.ShapeDtypeStruct((b, l, d), F32)] + r_shapes,
        scratch_shapes=r_scratch,
        compiler_params=_cparams(("arbitrary", "arbitrary")),
        name="out_proj_even",
    )(xc, rec2, gate, att, w_a, w_b, mods3, *r_args)


def _diff_attn_kernel(lam_ref, g_ref, q_ref, k_ref, v_ref, o_ref, *, lam_init):
    lv = lam_ref[...]
    lam = (jnp.exp(jnp.sum(lv[0:1] * lv[1:2], axis=-1, keepdims=True))
           - jnp.exp(jnp.sum(lv[2:3] * lv[3:4], axis=-1, keepdims=True)) + lam_init)
    vw = 2 * DIFF_DH

    def logits(h, mp):
        lo = h * vw + mp * DIFF_DH
        return _nt_dot(q_ref[:, lo:lo + DIFF_DH], k_ref[:, lo:lo + DIFF_DH])

    def softmax_parts(s):
        e = jnp.exp2(s - jnp.max(s, axis=-1, keepdims=True))
        return e, 1.0 / jnp.sum(e, axis=-1, keepdims=True)

    heads = range(DIFF_HEADS)
    ls = [(logits(h, 0), logits(h, 1)) for h in heads]
    ws = []
    for l0, l1 in ls:
        e0, r0 = softmax_parts(l0)
        e1, r1 = softmax_parts(l1)
        ws.append((e0 * r0 - e1 * (lam * r1)).astype(BF16))
    for h, w in zip(heads, ws):
        lo = h * vw
        o = jnp.dot(w, v_ref[:, lo:lo + vw], preferred_element_type=F32)
        ms = jnp.mean(o * o, axis=-1, keepdims=True)
        o = o * lax.rsqrt(ms + NORM_EPS) * g_ref[...]
        o_ref[:, lo:lo + vw] = (o * (1.0 - lam_init)).astype(o_ref.dtype)


def _diff_attention(q, k, v, lam_vecs, subln_g, lam_init, c_len):
    b, l, w = q.shape
    tq = ROW_TILE
    s_len = l - c_len
    off = c_len // tq
    return pl.pallas_call(
        functools.partial(_diff_attn_kernel, lam_init=lam_init),
        grid=(b, s_len // tq),
        in_specs=[
            pl.BlockSpec(lam_vecs.shape, lambda bb, j: (0, 0)),
            pl.BlockSpec((1, 2 * DIFF_DH), lambda bb, j: (0, 0)),
            pl.BlockSpec((None, tq, w), lambda bb, j: (bb, j + off, 0)),
            pl.BlockSpec((None, l, w), lambda bb, j: (bb, 0, 0)),
            pl.BlockSpec((None, l, w), lambda bb, j: (bb, 0, 0)),
        ],
        out_specs=pl.BlockSpec((None, tq, w), lambda bb, j: (bb, j, 0)),
        out_shape=jax.ShapeDtypeStruct((b, s_len, w), BF16),
        compiler_params=_cparams(("arbitrary", "arbitrary")),
        name="diff_attention",
    )(lam_vecs, subln_g.reshape(1, -1), q, k, v)


def _ssd_kernel(x_ref, xp_ref, xn_ref, dt_ref, cw_ref, cb_ref, dtb_ref, alog_ref, dsk_ref, o_ref,
                ext_scr, st_scr, *, q, nct, nt):
    d = pl.program_id(0)
    g = pl.program_id(2)
    tile = _seq_tile(d, g, nct, nt)
    pv = jnp.where((tile == 0) | (tile == nct), 0.0, 1.0)
    nv = jnp.where((tile == nct - 1) | (tile == nt - 1), 0.0, 1.0)
    ext_scr[0:SUBLANES] = xp_ref[...] * pv
    ext_scr[SUBLANES:SUBLANES + q] = x_ref[...]
    ext_scr[SUBLANES + q:2 * SUBLANES + q] = xn_ref[...] * nv

    @pl.when(g == 0)
    def _():
        st_scr[...] = jnp.zeros_like(st_scr)

    u = cb_ref[...] + cw_ref[0] * ext_scr[SUBLANES - 1:SUBLANES - 1 + q, :]
    for j in range(1, CONV_W):
        u = u + cw_ref[j] * ext_scr[SUBLANES - 1 + j:SUBLANES - 1 + j + q, :]
    act = _silu(u)

    dtr = dt_ref[...]
    dtr = jnp.where(d == 0, dtr, pltpu.roll(dtr, LANES - SSD_HEADS, 1))
    dtv = _softplus(dtr + dtb_ref[...])
    head_lane = lax.broadcasted_iota(jnp.int32, (1, LANES), 1) < SSD_HEADS
    dta = dtv * jnp.where(head_lane, -jnp.exp(alog_ref[...]), 0.0)
    ri = lax.broadcasted_iota(jnp.int32, (q, q), 0)
    ci = lax.broadcasted_iota(jnp.int32, (q, q), 1)
    keep = jnp.where(d == 0, ri - ci, ci - ri) >= 0
    cum = jnp.dot(keep.astype(F32), dta, preferred_element_type=F32, precision=HIGHEST)
    tot = jnp.sum(dta, axis=0, keepdims=True)
    cum_t = cum.T
    dt_t = dtv.T
    to_end = jnp.exp(tot - cum) * dtv
    e_cum = jnp.exp(cum)
    e_tot = jnp.exp(tot)
    dskip = dsk_ref[...] * jnp.where(d == 0, 1.0, 0.0)

    hpg = SSD_HEADS // SSD_GROUPS
    for gi in range(SSD_GROUPS):
        b_g = act[:, SSD_INNER + gi * SSD_STATE:SSD_INNER + (gi + 1) * SSD_STATE]
        c_lo = SSD_INNER + SSD_GROUPS * SSD_STATE + gi * SSD_STATE
        c_g = act[:, c_lo:c_lo + SSD_STATE].astype(BF16)
        cb = _nt_dot(c_g, b_g.astype(BF16))
        b_gt = b_g.T.astype(BF16)
        for hh in range(hpg):
            h = gi * hpg + hh
            xs = act[:, h * SSD_HEAD_DIM:(h + 1) * SSD_HEAD_DIM]
            seg = cum[:, h:h + 1] - cum_t[h:h + 1, :]
            decay = jnp.exp(jnp.where(keep, seg, NEG_BIG))
            w = (cb * decay * dt_t[h:h + 1, :]).astype(BF16)
            state = st_scr[h]
            y = jnp.dot(w, xs.astype(BF16), preferred_element_type=F32)
            y = y + jnp.dot(c_g, state.astype(BF16), preferred_element_type=F32) * e_cum[:, h:h + 1]
            y = y + dskip[:, h * SSD_HEAD_DIM:(h + 1) * SSD_HEAD_DIM] * xs
            o_ref[:, h * SSD_HEAD_DIM:(h + 1) * SSD_HEAD_DIM] = y
            s_new = jnp.dot(b_gt, (xs * to_end[:, h:h + 1]).astype(BF16), preferred_element_type=F32)
            st_scr[h] = e_tot[:, h:h + 1] * state + s_new


def _ssd(xbc, dt, conv_w, conv_b, dt_bias, a_log, d_skip, nct):
    b, l, cd = xbc.shape
    q = TIME_TILE
    nt = l // q
    tile = lambda d, bb, g: _seq_tile(d, g, nct, nt)
    r8 = q // SUBLANES
    pad = LANES - SSD_HEADS
    dtb = jnp.pad(dt_bias, ((0, 0), (0, pad))).reshape(2, 1, LANES)
    alog = jnp.pad(a_log, ((0, 0), (0, pad))).reshape(2, 1, LANES)
    dsk = jnp.repeat(d_skip, SSD_HEAD_DIM).reshape(1, SSD_INNER)
    return pl.pallas_call(
        functools.partial(_ssd_kernel, q=q, nct=nct, nt=nt),
        grid=(2, b, nt),
        in_specs=[
            pl.BlockSpec((None, q, cd), lambda d, bb, g: (bb, tile(d, bb, g), 0)),
            pl.BlockSpec((None, SUBLANES, cd),
                         lambda d, bb, g: (bb, jnp.maximum(tile(d, bb, g) * r8 - 1, 0), 0)),
            pl.BlockSpec((None, SUBLANES, cd),
                         lambda d, bb, g: (bb, jnp.minimum((tile(d, bb, g) + 1) * r8, l // SUBLANES - 1), 0)),
            pl.BlockSpec((None, q, LANES), lambda d, bb, g: (bb, tile(d, bb, g), 0)),
            pl.BlockSpec((CONV_W, 1, cd), lambda d, bb, g: (0, 0, 0)),
            pl.BlockSpec((1, cd), lambda d, bb, g: (0, 0)),
            pl.BlockSpec((None, 1, LANES), lambda d, bb, g: (d, 0, 0)),
            pl.BlockSpec((None, 1, LANES), lambda d, bb, g: (d, 0, 0)),
            pl.BlockSpec((1, SSD_INNER), lambda d, bb, g: (0, 0)),
        ],
        out_specs=pl.BlockSpec((None, None, q, SSD_INNER), lambda d, bb, g: (d, bb, tile(d, bb, g), 0)),
        out_shape=jax.ShapeDtypeStruct((2, b, l, SSD_INNER), F32),
        scratch_shapes=[
            pltpu.VMEM((q + 2 * SUBLANES, cd), F32),
            pltpu.VMEM((SSD_HEADS, SSD_STATE, SSD_HEAD_DIM), F32),
        ],
        compiler_params=_cparams(("arbitrary", "arbitrary", "arbitrary")),
        name="ssd_chunked",
    )(xbc, xbc, xbc, dt, conv_w.reshape(CONV_W, 1, cd), conv_b.reshape(1, cd), dtb, alog, dsk)


def _out_odd_kernel(x_ref, diff_ref, y_ref, z_ref, ng_ref, wa_ref, wb_ref, g1_ref, *refs):
    route_in, o_ref, route_out = refs[:5], refs[5], refs[6:]
    yz = (y_ref[0] + y_ref[1]) * _silu(z_ref[...].astype(F32))
    gs = SSD_INNER // SSD_GROUPS
    parts = []
    for gi in range(SSD_GROUPS):
        seg = yz[:, gi * gs:(gi + 1) * gs]
        ms = jnp.mean(seg * seg, axis=-1, keepdims=True)
        parts.append(seg * lax.rsqrt(ms + NORM_EPS) * ng_ref[:, gi * gs:(gi + 1) * gs])
    ssd = jnp.concatenate(parts, axis=-1).astype(BF16)
    y = (jnp.dot(diff_ref[...], wa_ref[...], preferred_element_type=F32)
         + jnp.dot(ssd, wb_ref[...], preferred_element_type=F32))
    x = x_ref[...] + g1_ref[...] * y
    o_ref[...] = x
    _route(x, *route_in, *route_out)


def _out_odd(xc, diff, y2, z, norm_g, w_a, w_b, mods3, layer, c_len, g_ffn, w_router, b_router):
    b, l, d = xc.shape
    s_len = l - c_len
    tm = ROW_TILE
    off = c_len // tm
    w = SSD_INNER
    row = lambda bb, i: bb
    r_in, r_args, r_out, r_shapes, r_scratch = _route_plumbing(b, s_len, d, g_ffn, mods3, layer, row, w_router,
                                                               b_router)
    return pl.pallas_call(
        _out_odd_kernel,
        grid=(b, s_len // tm),
        in_specs=[
            pl.BlockSpec((None, tm, d), lambda bb, i: (bb, i + off, 0)),
            pl.BlockSpec((None, tm, diff.shape[-1]), lambda bb, i: (bb, i, 0)),
            pl.BlockSpec((2, None, tm, w), lambda bb, i: (0, bb, i + off, 0)),
            pl.BlockSpec((None, tm, w), lambda bb, i: (bb, i + off, 0)),
            pl.BlockSpec((1, w), lambda bb, i: (0, 0)),
            pl.BlockSpec(w_a.shape, lambda bb, i: (0, 0)),
            pl.BlockSpec(w_b.shape, lambda bb, i: (0, 0)),
            _mod_spec(d, layer, 2, row),
        ] + r_in,
        out_specs=[pl.BlockSpec((None, tm, d), lambda bb, i: (bb, i, 0))] + r_out,
        out_shape=[jax.ShapeDtypeStruct((b, s_len, d), F32)] + r_shapes,
        scratch_shapes=r_scratch,
        compiler_params=_cparams(("arbitrary", "arbitrary")),
        name="out_proj_odd",
    )(xc, diff, y2, z, norm_g.reshape(1, w), w_a, w_b, mods3, *r_args)


def _route(x, g_ref, sh_ref, sc_ref, wr_ref, br_ref, h_ref, eid_ref, rnk_ref, gate_ref, cnt_ref, carry_scr):
    @pl.when((pl.program_id(0) == 0) & (pl.program_id(1) == 0))
    def _():
        carry_scr[...] = jnp.zeros_like(carry_scr)

    h = _norm_mod(x, g_ref[...], sh_ref[...], sc_ref[...])
    _store_chunk_rows(h_ref, _pack_bf16(h))
    tm = h.shape[0]
    per = N_EXPERTS // N_EXPERT_GROUPS
    logits = lax.dot_general(wr_ref[...], h, (((1,), (1,)), ((), ())),
                             preferred_element_type=F32, precision=HIGHEST)
    scores = jax.nn.sigmoid(logits)
    sel = scores + br_ref[...]
    sel3 = sel.reshape(N_EXPERT_GROUPS, per, tm)
    kio = lax.broadcasted_iota(jnp.int32, sel3.shape, 1)
    m1 = jnp.max(sel3, axis=1, keepdims=True)
    first = jnp.min(jnp.where(sel3 == m1, kio, per), axis=1, keepdims=True)
    m2 = jnp.max(jnp.where(kio == first, NEG_BIG, sel3), axis=1, keepdims=True)
    gs = m1 + m2
    gio = lax.broadcasted_iota(jnp.int32, gs.shape, 0)
    ahead = jnp.zeros(gs.shape, jnp.int32)
    for gp in range(N_EXPERT_GROUPS):
        other = gs[gp:gp + 1]
        ahead = ahead + jnp.where((other > gs) | ((other == gs) & (gp < gio)), 1, 0)
    grp_on = jnp.where(ahead < TOPK_GROUPS, 1.0, 0.0)
    selm = jnp.where(jnp.broadcast_to(grp_on, sel3.shape) > 0.5, sel3, NEG_BIG).reshape(N_EXPERTS, tm)
    eio = lax.broadcasted_iota(jnp.int32, selm.shape, 0)
    work = selm
    cf = jnp.zeros(selm.shape, F32)
    e_rows, s_rows = [], []
    for k in range(TOP_K):
        best = jnp.max(work, axis=0, keepdims=True)
        idx = jnp.min(jnp.where(work == best, eio, N_EXPERTS), axis=0, keepdims=True)
        hit = eio == idx
        cf = cf + jnp.where(hit, 1.0, 0.0)
        work = jnp.where(hit, NEG_BIG, work)
        e_rows.append(idx)
        s_rows.append(jnp.sum(jnp.where(hit, scores, 0.0), axis=0, keepdims=True))
    denom = s_rows[0]
    for s_k in s_rows[1:]:
        denom = denom + s_k
    g_rows = [s_k / denom * ROUTED_SCALE for s_k in s_rows]
    ti = lax.broadcasted_iota(jnp.int32, (tm, tm), 0)
    tj = lax.broadcasted_iota(jnp.int32, (tm, tm), 1)
    before = jnp.where(ti < tj, 1.0, 0.0).astype(BF16)
    in_expert = carry_scr[:, 0:1] + jnp.dot(cf.astype(BF16), before, preferred_element_type=F32)
    carry_scr[...] = carry_scr[...] + jnp.sum(cf, axis=1, keepdims=True)
    cnt_ref[...] = carry_scr[...]
    r_rows = [jnp.sum(jnp.where(eio == idx, in_expert, 0.0), axis=0, keepdims=True) for idx in e_rows]
    eid_ref[...] = jnp.concatenate(e_rows, axis=0)
    rnk_ref[...] = jnp.concatenate(r_rows, axis=0).astype(jnp.int32)
    padded = jnp.concatenate(g_rows + [jnp.zeros((LANES - TOP_K, tm), F32)], axis=0)
    gate_ref[...] = padded.T


def _route_plumbing(b, r, d, g, mods3, layer, row_fn, w_router, b_router):
    tm = ROW_TILE
    nt = r // tm
    w_router_t = w_router.T
    slot = pl.BlockSpec((TOP_K, tm), lambda bb, i: (0, bb * nt + i))
    slot_shape = jax.ShapeDtypeStruct((TOP_K, b * r), jnp.int32)
    in_specs = [
        pl.BlockSpec((1, d), lambda bb, i: (0, 0)),
        _mod_spec(d, layer, 3, row_fn),
        _mod_spec(d, layer, 4, row_fn),
        pl.BlockSpec(w_router_t.shape, lambda bb, i: (0, 0)),
        pl.BlockSpec((N_EXPERTS, 1), lambda bb, i: (0, 0)),
    ]
    args = [g.reshape(1, d), mods3, mods3, w_router_t, b_router.reshape(N_EXPERTS, 1)]
    out_specs = [
        pl.BlockSpec((tm * ROW_CHUNKS, LANES), lambda bb, i: (bb * nt + i, 0)),
        slot,
        slot,
        pl.BlockSpec((None, tm, LANES), lambda bb, i: (bb, i, 0)),
        pl.BlockSpec((N_EXPERTS, LANES), lambda bb, i: (0, 0)),
    ]
    out_shapes = [jax.ShapeDtypeStruct((b * r * ROW_CHUNKS, LANES), jnp.uint32), slot_shape, slot_shape,
                  jax.ShapeDtypeStruct((b, r, LANES), F32), jax.ShapeDtypeStruct((N_EXPERTS, LANES), F32)]
    scratch = [pltpu.VMEM((N_EXPERTS, LANES), F32)]
    return in_specs, args, out_specs, out_shapes, scratch


def _moe_plan(counts, n_rows):
    blk = EXPERT_BLK
    nb = n_rows // blk
    ends = jnp.cumsum(counts)
    starts = ends - counts
    count_le = lambda sorted_vals, q: jnp.sum(sorted_vals[None, :] <= q[:, None], axis=1, dtype=jnp.int32)
    first = jnp.arange(nb, dtype=jnp.int32) * blk
    e_lo = count_le(ends, first)
    e_hi = count_le(ends, first + (blk - 1))
    n_pair = e_hi - e_lo + 1
    p_end = jnp.cumsum(n_pair)
    p_start = p_end - n_pair
    i = jnp.arange(nb + N_EXPERTS - 1, dtype=jnp.int32)
    j = jnp.minimum(count_le(p_end, i), nb - 1)
    valid = i < p_end[-1]
    e = jnp.where(valid, e_lo[j] + i - p_start[j], e_hi[nb - 1]).astype(jnp.int32)
    bounds = jnp.concatenate([starts, ends[-1:]]).astype(jnp.int32)
    return j, e, valid.astype(jnp.int32), bounds


def _positions_kernel(starts_ref, eid_ref, rnk_ref, pos_ref):
    eid = eid_ref[...]
    pos = rnk_ref[...]
    for e in range(N_EXPERTS):
        pos = pos + jnp.where(eid == e, starts_ref[e], 0)
    pos_ref[...] = pos * ROW_CHUNKS


def _positions(eid, rnk, starts):
    full = pl.BlockSpec(eid.shape, lambda: (0, 0))
    return pl.pallas_call(
        _positions_kernel,
        in_specs=[pl.BlockSpec(memory_space=pltpu.SMEM), full, full],
        out_specs=full,
        out_shape=jax.ShapeDtypeStruct(eid.shape, jnp.int32),
        compiler_params=pltpu.CompilerParams(vmem_limit_bytes=VMEM_LIMIT),
        name="moe_positions",
    )(starts, eid, rnk)


def _token_row(ref, first):
    return ref.at[pl.ds(pl.multiple_of(first, ROW_CHUNKS), ROW_CHUNKS)]


def _dispatch_kernel(pos_ref, h_ref, xs_ref, sem):
    tm = h_ref.shape[0] // ROW_CHUNKS

    def issue(t, carry):
        src = _token_row(h_ref, t * ROW_CHUNKS)
        for k in range(TOP_K):
            pltpu.make_async_copy(src, _token_row(xs_ref, pos_ref[k, t]), sem).start(priority=k % 2)
        return carry

    lax.fori_loop(0, tm, issue, 0)
    done = pl.ds(0, tm * ROW_CHUNKS)
    for _ in range(TOP_K):
        pltpu.make_async_copy(h_ref.at[done], xs_ref.at[done], sem).wait()


def _dispatch(h2, pos):
    rows, w = h2.shape
    tm = DISPATCH_TILE
    return pl.pallas_call(
        _dispatch_kernel,
        grid=(rows // (tm * ROW_CHUNKS),),
        in_specs=[
            pl.BlockSpec((TOP_K, tm), lambda i: (0, i), memory_space=pltpu.SMEM),
            pl.BlockSpec((tm * ROW_CHUNKS, w), lambda i: (i, 0)),
        ],
        out_specs=pl.BlockSpec(memory_space=pl.ANY),
        out_shape=jax.ShapeDtypeStruct((rows * TOP_K, w), h2.dtype),
        scratch_shapes=[pltpu.SemaphoreType.DMA],
        compiler_params=_cparams(("arbitrary",)),
        name="moe_dispatch",
    )(pos, h2)


def _grouped_kernel(pb_ref, pe_ref, pv_ref, bnd_ref, xs_ref, wg_ref, wu_ref, wd_ref, y_ref, wgb, wub, wdb):
    i = pl.program_id(0)
    prev = jnp.maximum(i - 1, 0)
    j = pb_ref[i]
    e = pe_ref[i]
    blk = xs_ref.shape[0] // ROW_CHUNKS

    @pl.when((i == 0) | (pb_ref[prev] != j))
    def _():
        y_ref[...] = jnp.zeros_like(y_ref)

    @pl.when((i == 0) | (pe_ref[prev] != e))
    def _():
        wgb[...] = wg_ref[...].astype(BF16)
        wub[...] = wu_ref[...].astype(BF16)
        wdb[...] = wd_ref[...].astype(BF16)

    @pl.when(pv_ref[i] == 1)
    def _():
        xw = _load_chunk_rows(xs_ref, blk)
        a = _packed_dot(xw, wgb)
        u = _packed_dot(xw, wub)
        yv = jnp.dot((_silu(a) * u).astype(BF16), wdb[...], preferred_element_type=F32)
        rows = j * blk + lax.broadcasted_iota(jnp.int32, (blk, 1), 0)
        own = (rows >= bnd_ref[e]) & (rows < bnd_ref[e + 1])
        yw = _pack_bf16(yv)
        for c in range(ROW_CHUNKS):
            sl = pl.ds(c, blk, stride=ROW_CHUNKS)
            y_ref[sl, :] = jnp.where(own, yw[:, c * LANES:(c + 1) * LANES], y_ref[sl, :])


def _grouped(pb, pe, pv, bounds, xs, wg, wu, wd, layer):
    p, half = xs.shape
    d = 2 * ROW_CHUNKS * LANES
    blk = EXPERT_BLK * ROW_CHUNKS
    grid_spec = pltpu.PrefetchScalarGridSpec(
        num_scalar_prefetch=4,
        grid=(pb.shape[0],),
        in_specs=[
            pl.BlockSpec((blk, half), lambda i, pb, pe, pv, bnd: (pb[i], 0)),
            pl.BlockSpec((None, None, d, D_EXPERT), lambda i, pb, pe, pv, bnd: (layer, pe[i], 0, 0)),
            pl.BlockSpec((None, None, d, D_EXPERT), lambda i, pb, pe, pv, bnd: (layer, pe[i], 0, 0)),
            pl.BlockSpec((None, None, D_EXPERT, d), lambda i, pb, pe, pv, bnd: (layer, pe[i], 0, 0)),
        ],
        out_specs=pl.BlockSpec((blk, half), lambda i, pb, pe, pv, bnd: (pb[i], 0)),
        scratch_shapes=[
            pltpu.VMEM((d, D_EXPERT), BF16),
            pltpu.VMEM((d, D_EXPERT), BF16),
            pltpu.VMEM((D_EXPERT, d), BF16),
        ],
    )
    return pl.pallas_call(
        _grouped_kernel,
        grid_spec=grid_spec,
        out_shape=jax.ShapeDtypeStruct((p, half), jnp.uint32),
        compiler_params=_cparams(("arbitrary",)),
        name="moe_grouped_experts",
    )(pb, pe, pv, bounds, xs, wg, wu, wd)


def _combine_kernel(*refs, final):
    pos_ref, y_ref, gate_ref, h_ref, x_ref, g2_ref, sg_ref, su_ref, sd_ref = refs[:9]
    o_ref, buf, sem = refs[-3:]
    tm = x_ref.shape[0]

    def issue(t, carry):
        for k in range(TOP_K):
            pltpu.make_async_copy(_token_row(y_ref, pos_ref[k, t]), _token_row(buf.at[k], t * ROW_CHUNKS),
                                  sem).start(priority=k % 2)
        return carry

    lax.fori_loop(0, tm, issue, 0)
    hw = _load_chunk_rows(h_ref, tm)
    a = _packed_dot(hw, sg_ref)
    u = _packed_dot(hw, su_ref)
    acc = jnp.dot((_silu(a) * u).astype(BF16), sd_ref[...], preferred_element_type=F32)
    done = pl.ds(0, tm * ROW_CHUNKS)
    for k in range(TOP_K):
        pltpu.make_async_copy(y_ref.at[done], buf.at[k, done], sem).wait()
    g = gate_ref[...]
    half = hw.shape[-1]
    acc_hi = acc[:, :half]
    acc_lo = acc[:, half:]
    for k in range(TOP_K):
        hi, lo = _unpack_bf16(_load_chunk_rows(buf.at[k], tm))
        acc_hi = acc_hi + g[:, k:k + 1] * hi
        acc_lo = acc_lo + g[:, k:k + 1] * lo
    x = x_ref[...] + g2_ref[...] * jnp.concatenate([acc_hi, acc_lo], axis=-1)
    if final:
        gf_ref = refs[9]
        ms = jnp.mean(x * x, axis=-1, keepdims=True)
        x = x * lax.rsqrt(ms + NORM_EPS) * gf_ref[...]
    o_ref[...] = x


def _combine(pos, y, gates, h2, x, mods3, layer, row_fn, sg, su, sd, g_final=None):
    b, r, d = x.shape
    tm = ROW_TILE
    nt = r // tm
    tile = pl.BlockSpec((None, tm, d), lambda bb, i: (bb, i, 0))
    in_specs = [
        pl.BlockSpec((TOP_K, tm), lambda bb, i: (0, bb * nt + i), memory_space=pltpu.SMEM),
        pl.BlockSpec(memory_space=pl.ANY),
        pl.BlockSpec((None, tm, LANES), lambda bb, i: (bb, i, 0)),
        pl.BlockSpec((tm * ROW_CHUNKS, LANES), lambda bb, i: (bb * nt + i, 0)),
        tile,
        _mod_spec(d, layer, 5, row_fn),
        pl.BlockSpec(sg.shape, lambda bb, i: (0, 0)),
        pl.BlockSpec(su.shape, lambda bb, i: (0, 0)),
        pl.BlockSpec(sd.shape, lambda bb, i: (0, 0)),
    ]
    args = [pos, y, gates, h2, x, mods3, sg, su, sd]
    if g_final is not None:
        in_specs.append(pl.BlockSpec((1, d), lambda bb, i: (0, 0)))
        args.append(g_final.reshape(1, d))
    return pl.pallas_call(
        functools.partial(_combine_kernel, final=g_final is not None),
        grid=(b, nt),
        in_specs=in_specs,
        out_specs=tile,
        out_shape=jax.ShapeDtypeStruct((b, r, d), F32),
        scratch_shapes=[pltpu.VMEM((TOP_K, tm * ROW_CHUNKS, LANES), jnp.uint32), pltpu.SemaphoreType.DMA],
        compiler_params=_cparams(("arbitrary", "arbitrary")),
        name="moe_combine",
    )(*args)


def _moe(x, routed, mods3, layer, row_fn, w_e_gate, w_e_up, w_e_down, ws_gate, ws_up, ws_down, g_final=None):
    b, r, d = x.shape
    h2, eid, rnk, gates, cnt = routed
    pb, pe, pv, bounds = _moe_plan(cnt[:, 0].astype(jnp.int32), b * r * TOP_K)
    pos = _positions(eid, rnk, bounds[:N_EXPERTS])
    xs = _dispatch(h2, pos)
    y = _grouped(pb, pe, pv, bounds, xs, w_e_gate, w_e_up, w_e_down, layer)
    return _combine(pos, y, gates, h2, x, mods3, layer, row_fn,
                    ws_gate.astype(BF16), ws_up.astype(BF16), ws_down.astype(BF16), g_final)


def kernel(x, c, ctx, c_ctx, w_mod, b_mod, g_mix, g_ffn, g_final, ab_w_in, ab_w_out, ab_conv_w, ab_conv_b, ab_w_r, ab_b_r, ab_w_i, ab_b_i, ab_lam, ab_sink, cd_w_in, cd_w_out, cd_lam, cd_subln_g, cd_conv_w, cd_conv_b, cd_dt_bias, cd_a_log, cd_d_skip, cd_norm_g, w_router, b_router, w_e_gate, w_e_up, w_e_down, ws_gate, ws_up, ws_down):
    bsz, s_len, d = x.shape
    c_len = ctx.shape[1]
    depth = w_mod.shape[0]
    assert depth == 2 and bsz == SUBLANES, "kernels are specialised to depth 2 and batch 8"
    assert c_len % ROW_TILE == 0 and s_len % ROW_TILE == 0
    nct_row = c_len // ROW_TILE
    nct_time = c_len // TIME_TILE

    c_all = jnp.concatenate([c, c_ctx[None], jnp.zeros((MOD_ROWS - bsz - 1, d), F32)], axis=0)
    mods3 = _modulations(c_all, w_mod, b_mod).reshape(depth * MOD_ROWS, 1, N_MOD * d)
    rope_tabs = _rope_tables(c_len, s_len)
    xc = jnp.concatenate([ctx, x], axis=1)
    row_mixed = lambda bb, i: jnp.where(i < nct_row, SUBLANES, bb)
    row_latent = lambda bb, i: bb

    w_in = ab_w_in[0].astype(BF16)
    q_hi = LRU_WIDTH + WIN_HEADS * HEAD_DIM
    x_hi = q_hi + LRU_WIDTH
    k_hi = x_hi + WIN_KV_HEADS * HEAD_DIM
    gate, q, xa, k, v = _project(xc, g_mix[0], mods3, 0, nct_row, rope_tabs, [
        (w_in[:, :LRU_WIDTH], None, BF16, False),
        (w_in[:, LRU_WIDTH:q_hi], HEAD_DIM ** -0.5 * LOG2E, BF16, False),
        (w_in[:, q_hi:x_hi], None, F32, True),
        (w_in[:, x_hi:k_hi], 1.0, BF16, False),
        (w_in[:, k_hi:], None, BF16, False),
    ])
    l_len = c_len + s_len
    w_gates = jnp.stack([jnp.concatenate([_block_diag(ab_w_r[0, dd]), _block_diag(ab_w_i[0, dd])], axis=1)
                         for dd in range(2)]).astype(BF16)
    b_gates = jnp.concatenate([ab_b_r[0], ab_b_i[0]], axis=-1).reshape(2, 1, 2 * LRU_WIDTH)
    rec = _rglru(xa.reshape(l_len, bsz, LRU_WIDTH), ab_conv_w[0], ab_conv_b[0], w_gates, b_gates,
                 ab_lam[0].reshape(2, 1, LRU_WIDTH), nct_time)
    att = _win_attention(q, k, v, ab_sink[0], c_len)
    w_out = ab_w_out[0].astype(BF16)
    xc, *routed = _out_even(xc, rec.reshape(2, l_len, bsz * LRU_WIDTH), gate, att, w_out[:LRU_WIDTH],
                            w_out[LRU_WIDTH:], mods3, 0, nct_row, g_ffn[0], w_router[0], b_router[0])
    xc = _moe(xc, routed, mods3, 0, row_mixed, w_e_gate, w_e_up, w_e_down, ws_gate[0], ws_up[0], ws_down[0])

    w_in = cd_w_in[0].astype(BF16)
    qk = DIFF_HEADS * 2 * DIFF_DH
    z_hi = qk + SSD_INNER
    k_hi = z_hi + qk
    v_hi = k_hi + qk
    x_hi = v_hi + SSD_CONV_DIM
    w_dt = jnp.pad(w_in[:, x_hi:], ((0, 0), (0, LANES - 2 * SSD_HEADS)))
    q, z, k, v, xbc, dt = _project(xc, g_mix[1], mods3, 1, nct_row, rope_tabs, [
        (w_in[:, :qk], DIFF_DH ** -0.5 * LOG2E, BF16, False),
        (w_in[:, qk:z_hi], None, BF16, False),
        (w_in[:, z_hi:k_hi], 1.0, BF16, False),
        (w_in[:, k_hi:v_hi], None, BF16, False),
        (w_in[:, v_hi:x_hi], None, F32, False),
        (w_dt, None, F32, False),
    ])
    lam_init = 0.8 - 0.6 * math.exp(-0.3 * 1)
    diff = _diff_attention(q, k, v, cd_lam[0], cd_subln_g[0], lam_init, c_len)
    y2 = _ssd(xbc, dt, cd_conv_w[0], cd_conv_b[0], cd_dt_bias[0], cd_a_log[0], cd_d_skip[0], nct_time)
    w_out = cd_w_out[0].astype(BF16)
    xl, *routed = _out_odd(xc, diff, y2, z, cd_norm_g[0], w_out[:qk], w_out[qk:], mods3, 1, c_len,
                           g_ffn[1], w_router[1], b_router[1])
    return _moe(xl, routed, mods3, 1, row_latent, w_e_gate, w_e_up, w_e_down, ws_gate[1], ws_up[1], ws_down[1],
                g_final=g_final)
```

```python
import functools
import math

import jax
import jax.numpy as jnp
from jax import lax
from jax.experimental import pallas as pl
from jax.experimental.pallas import tpu as pltpu

F32 = jnp.float32
BF16 = jnp.bfloat16
HIGHEST = lax.Precision.HIGHEST

GRID_W = 64
N_MOD = 6
NORM_EPS = 1e-6
ROPE_BASE = 10000.0
CONV_W = 4

LRU_WIDTH = 512
LRU_BLOCKS = 8
LRU_C = 8.0

HEAD_DIM = 64
WIN_HEADS = 8
WIN_KV_HEADS = 2
WINDOW = 128

DIFF_HEADS = 4
DIFF_DH = 64

SSD_HEADS = 8
SSD_HEAD_DIM = 64
SSD_INNER = SSD_HEADS * SSD_HEAD_DIM
SSD_GROUPS = 2
SSD_STATE = 128
SSD_CONV_DIM = SSD_INNER + 2 * SSD_GROUPS * SSD_STATE

N_EXPERTS = 64
N_EXPERT_GROUPS = 8
TOPK_GROUPS = 4
TOP_K = 8
D_EXPERT = 256
ROUTED_SCALE = 2.5

LANES = 128
SUBLANES = 8
MOD_ROWS = 16
TIME_TILE = 128
ROW_TILE = 256
DISPATCH_TILE = 2048
LATENT_COMBINE_TILE = 512
EXPERT_BLK = 1024
ROW_CHUNKS = 4
VMEM_LIMIT = 48 * 1024 * 1024
NEG_BIG = -1e30
LOG2E = math.log2(math.e)


def _cparams(sem):
    return pltpu.CompilerParams(dimension_semantics=sem, vmem_limit_bytes=VMEM_LIMIT)


def _nt_dot(a, b):
    return lax.dot_general(a, b, (((1,), (1,)), ((), ())), preferred_element_type=F32)


def _softplus(x):
    return jnp.maximum(x, 0.0) + jnp.log1p(jnp.exp(-jnp.abs(x)))


def _silu(x):
    return x * jax.nn.sigmoid(x)


def _pack_bf16(x):
    half = x.shape[-1] // 2
    bits = pltpu.bitcast(x.astype(BF16).astype(F32), jnp.uint32)
    return bits[:, :half] | (bits[:, half:] >> 16)


def _unpack_bf16(w):
    hi = pltpu.bitcast(w & jnp.uint32(0xFFFF0000), F32)
    lo = pltpu.bitcast(w << 16, F32)
    return hi, lo


def _store_chunk_rows(ref, w):
    n = w.shape[0]
    for j in range(ROW_CHUNKS):
        ref[pl.ds(j, n, stride=ROW_CHUNKS), :] = w[:, j * LANES:(j + 1) * LANES]


def _load_chunk_rows(ref, n):
    return jnp.concatenate([ref[pl.ds(j, n, stride=ROW_CHUNKS), :] for j in range(ROW_CHUNKS)], axis=1)


def _packed_dot(w, weight_ref):
    half = w.shape[-1]
    hi, lo = _unpack_bf16(w)
    return (jnp.dot(hi.astype(BF16), weight_ref[:half, :], preferred_element_type=F32)
            + jnp.dot(lo.astype(BF16), weight_ref[half:, :], preferred_element_type=F32))


def _mod_kernel(c_ref, w_ref, b_ref, o_ref):
    c = c_ref[...]
    s = _silu(c)
    o_ref[...] = jnp.dot(s, w_ref[...], preferred_element_type=F32, precision=HIGHEST) + b_ref[...]


def _modulations(c_all, w_mod, b_mod):
    depth, d, _ = w_mod.shape
    return pl.pallas_call(
        _mod_kernel,
        grid=(depth, N_MOD),
        in_specs=[
            pl.BlockSpec((MOD_ROWS, d), lambda l, k: (0, 0)),
            pl.BlockSpec((None, d, d), lambda l, k: (l, 0, k)),
            pl.BlockSpec((None, 1, d), lambda l, k: (l, 0, k)),
        ],
        out_specs=pl.BlockSpec((None, MOD_ROWS, d), lambda l, k: (l, 0, k)),
        out_shape=jax.ShapeDtypeStruct((depth, MOD_ROWS, N_MOD * d), F32),
        compiler_params=_cparams(("arbitrary", "arbitrary")),
        name="adaln_modulation",
    )(c_all, w_mod, b_mod.reshape(depth, 1, N_MOD * d))


def _mod_spec(d, layer, chunk, row_fn):
    return pl.BlockSpec((None, 1, d), lambda b, i: (layer * MOD_ROWS + row_fn(b, i), 0, chunk))


def _norm_mod(x, g, sh, sc):
    ms = jnp.mean(x * x, axis=-1, keepdims=True)
    return (x * lax.rsqrt(ms + NORM_EPS) * g) * (1.0 + sc) + sh


def _rope(y, cos, sa, sb):
    n = y.shape[-1]
    half = HEAD_DIM // 2
    return y * cos + pltpu.roll(y, n - half, 1) * sa + pltpu.roll(y, half, 1) * sb


def _proj_kernel(*refs, ropes):
    n = len(ropes)
    x_ref, g_ref, sh_ref, sc_ref, cos_ref, sa_ref, sb_ref = refs[:7]
    w_refs = refs[7:7 + n]
    o_refs = refs[7 + n:]
    h = _norm_mod(x_ref[...], g_ref[...], sh_ref[...], sc_ref[...]).astype(BF16)
    for w_ref, o_ref, rope in zip(w_refs, o_refs, ropes):
        y = jnp.dot(h, w_ref[...], preferred_element_type=F32)
        if rope is not None:
            w = y.shape[-1]
            y = _rope(y, cos_ref[:, :w], sa_ref[:, :w], sb_ref[:, :w])
            if rope != 1.0:
                y = y * rope
        o_ref[...] = y.astype(o_ref.dtype)


def _project(xc, g, mods3, layer, nct, rope_tabs, groups):
    b, l, d = xc.shape
    tm = ROW_TILE
    mod_row = lambda i, bb: layer * MOD_ROWS + jnp.where(i < nct, SUBLANES, bb)
    rw = rope_tabs[0].shape[-1]
    in_specs = [
        pl.BlockSpec((None, tm, d), lambda i, bb: (bb, i, 0)),
        pl.BlockSpec((1, d), lambda i, bb: (0, 0)),
        pl.BlockSpec((None, 1, d), lambda i, bb: (mod_row(i, bb), 0, 0)),
        pl.BlockSpec((None, 1, d), lambda i, bb: (mod_row(i, bb), 0, 1)),
    ] + [pl.BlockSpec((tm, rw), lambda i, bb: (i, 0))] * 3
    out_specs, out_shapes = [], []
    for w, _, dt, time_major in groups:
        n = w.shape[1]
        in_specs.append(pl.BlockSpec((d, n), lambda i, bb: (0, 0)))
        if time_major:
            out_specs.append(pl.BlockSpec((tm, n), lambda i, bb: (i, bb)))
            out_shapes.append(jax.ShapeDtypeStruct((l, b * n), dt))
        else:
            out_specs.append(pl.BlockSpec((None, tm, n), lambda i, bb: (bb, i, 0)))
            out_shapes.append(jax.ShapeDtypeStruct((b, l, n), dt))
    return pl.pallas_call(
        functools.partial(_proj_kernel, ropes=tuple(gp[1] for gp in groups)),
        grid=(l // tm, b),
        in_specs=in_specs,
        out_specs=out_specs,
        out_shape=out_shapes,
        compiler_params=_cparams(("arbitrary", "arbitrary")),
        name="norm_mod_project",
    )(xc, g.reshape(1, d), mods3, mods3, *rope_tabs, *[gp[0] for gp in groups])


def _rope_tables(c_len, s_len):
    rows = s_len // GRID_W
    row = jnp.repeat(jnp.arange(rows), GRID_W).astype(F32)
    col = jnp.tile(jnp.arange(GRID_W), rows).astype(F32)
    n = HEAD_DIM // 4
    inv = ROPE_BASE ** (-jnp.arange(n, dtype=F32) / n)
    ang = jnp.concatenate([row[:, None] * inv, col[:, None] * inv], axis=-1)
    cos, sin = jnp.cos(ang), jnp.sin(ang)
    zero = jnp.zeros_like(sin)
    reps = WIN_HEADS
    cos_t = jnp.tile(jnp.concatenate([cos, cos], axis=-1), (1, reps))
    sa_t = jnp.tile(jnp.concatenate([-sin, zero], axis=-1), (1, reps))
    sb_t = jnp.tile(jnp.concatenate([zero, sin], axis=-1), (1, reps))
    w = cos_t.shape[-1]
    pad1 = jnp.ones((c_len, w), F32)
    pad0 = jnp.zeros((c_len, w), F32)
    return (jnp.concatenate([pad1, cos_t], 0), jnp.concatenate([pad0, sa_t], 0),
            jnp.concatenate([pad0, sb_t], 0))


def _seq_tile(d, g, nct, nt):
    rev = jnp.where(g < nct, nct - 1 - g, nt - 1 - (g - nct))
    return jnp.where(d == 0, g, rev)


def _rglru_kernel(x_ref, xp_ref, xn_ref, cw_ref, cb_ref, w_ref, bias_ref, lam_ref, o_ref,
                  ext_scr, a_scr, b_scr, h_scr, *, ts, nct, nt, sub):
    d = pl.program_id(0)
    g = pl.program_id(1)
    tile = _seq_tile(d, g, nct, nt)
    bsz, width = h_scr.shape
    pv = jnp.where((tile == 0) | (tile == nct), 0.0, 1.0)
    nv = jnp.where((tile == nct - 1) | (tile == nt - 1), 0.0, 1.0)
    ext_scr[0:1] = xp_ref[...] * pv
    ext_scr[1:ts + 1] = x_ref[...]
    ext_scr[ts + 1:ts + 3] = xn_ref[...] * nv

    @pl.when(g == 0)
    def _():
        h_scr[...] = jnp.zeros_like(h_scr)

    neg_sp = -LRU_C * _softplus(-lam_ref[...])

    def prep(c, carry):
        r0 = pl.multiple_of(c * sub, sub)
        e = ext_scr[pl.ds(r0, sub + CONV_W - 1)]
        u = cb_ref[...] + cw_ref[0] * e[0:sub]
        for j in range(1, CONV_W):
            u = u + cw_ref[j] * e[j:j + sub]
        u2 = u.reshape(sub * bsz, width)
        gts = jnp.dot(u2.astype(BF16), w_ref[...], preferred_element_type=F32) + bias_ref[...]
        r = jax.nn.sigmoid(gts[:, :width])
        ig = jax.nn.sigmoid(gts[:, width:])
        log_a = neg_sp * r
        a = jnp.exp(log_a)
        mult = jnp.sqrt(1.0 - a * a)
        a_scr[pl.ds(r0, sub)] = a.reshape(sub, bsz, width)
        b_scr[pl.ds(r0, sub)] = (mult * ig * u2).reshape(sub, bsz, width)
        return carry

    lax.fori_loop(0, ts // sub, prep, 0)

    def step(t, h):
        tt = jnp.where(d == 0, t, ts - 1 - t)
        h = a_scr[tt] * h + b_scr[tt]
        o_ref[tt] = h
        return h

    h_scr[...] = lax.fori_loop(0, ts, step, h_scr[...], unroll=8)


def _rglru(xa_tm, conv_w, conv_b, w_gates, b_gates, lam, nct):
    l, bsz, width = xa_tm.shape
    ts = TIME_TILE
    nt = l // ts
    tile = lambda d, g: _seq_tile(d, g, nct, nt)
    kern = functools.partial(_rglru_kernel, ts=ts, nct=nct, nt=nt, sub=16)
    return pl.pallas_call(
        kern,
        grid=(2, nt),
        in_specs=[
            pl.BlockSpec((ts, bsz, width), lambda d, g: (tile(d, g), 0, 0)),
            pl.BlockSpec((1, bsz, width), lambda d, g: (jnp.maximum(tile(d, g) * ts - 1, 0), 0, 0)),
            pl.BlockSpec((2, bsz, width),
                         lambda d, g: (jnp.minimum((tile(d, g) + 1) * (ts // 2), l // 2 - 1), 0, 0)),
            pl.BlockSpec((CONV_W, 1, width), lambda d, g: (0, 0, 0)),
            pl.BlockSpec((1, width), lambda d, g: (0, 0)),
            pl.BlockSpec((None, width, 2 * width), lambda d, g: (d, 0, 0)),
            pl.BlockSpec((None, 1, 2 * width), lambda d, g: (d, 0, 0)),
            pl.BlockSpec((None, 1, width), lambda d, g: (d, 0, 0)),
        ],
        out_specs=pl.BlockSpec((None, ts, bsz, width), lambda d, g: (d, tile(d, g), 0, 0)),
        out_shape=jax.ShapeDtypeStruct((2, l, bsz, width), F32),
        scratch_shapes=[
            pltpu.VMEM((ts + CONV_W - 1, bsz, width), F32),
            pltpu.VMEM((ts, bsz, width), F32),
            pltpu.VMEM((ts, bsz, width), F32),
            pltpu.VMEM((bsz, width), F32),
        ],
        compiler_params=_cparams(("arbitrary", "arbitrary")),
        name="rglru_scan",
    )(xa_tm, xa_tm, xa_tm, conv_w.reshape(CONV_W, 1, width), conv_b.reshape(1, width),
      w_gates, b_gates, lam)


def _block_diag(w):
    nb, c, dd = w.shape
    eye = jnp.eye(nb, dtype=w.dtype)
    return (eye[:, None, :, None] * w[:, :, None, :]).reshape(nb * c, nb * dd)


def _win_attn_kernel(sink_ref, q_ref, k_ref, v_ref, o_ref, *, c_len, l_len, nqc):
    j = pl.program_id(1)
    blk = q_ref.shape[0]
    grp = WIN_HEADS // WIN_KV_HEADS
    band = 3 * blk
    heads = [(h, slice(h * HEAD_DIM, (h + 1) * HEAD_DIM),
              slice((h // grp) * HEAD_DIM, (h // grp + 1) * HEAD_DIM)) for h in range(WIN_HEADS)]

    @pl.when(j < nqc)
    def _():
        logits = [_nt_dot(q_ref[:, hsl], k_ref[0:c_len, ksl]) for _, hsl, ksl in heads]
        probs, dens = [], []
        for (h, _, _), s in zip(heads, logits):
            sink = sink_ref[h] * LOG2E
            m = jnp.maximum(jnp.max(s, axis=-1, keepdims=True), sink)
            p = jnp.exp2(s - m)
            dens.append(jnp.sum(p, axis=-1, keepdims=True) + jnp.exp2(sink - m))
            probs.append(p.astype(BF16))
        outs = [jnp.dot(p, v_ref[0:c_len, ksl], preferred_element_type=F32) / den
                for (_, _, ksl), p, den in zip(heads, probs, dens)]
        o_ref[...] = jnp.concatenate(outs, axis=-1).astype(o_ref.dtype)

    @pl.when(j >= nqc)
    def _():
        jb = j - nqc
        start = jnp.clip(c_len + (jb - 1) * blk, c_len - blk, l_len - band)
        start = pl.multiple_of(start, blk)
        qpos = jb * blk + lax.broadcasted_iota(jnp.int32, (blk, band), 0)
        kpos = start - c_len + lax.broadcasted_iota(jnp.int32, (blk, band), 1)
        valid = (jnp.abs(qpos - kpos) <= WINDOW) & (kpos >= 0)
        lc = [_nt_dot(q_ref[:, hsl], k_ref[0:c_len, ksl]) for _, hsl, ksl in heads]
        lb = [jnp.where(valid, _nt_dot(q_ref[:, hsl], k_ref[pl.ds(start, band), ksl]), NEG_BIG)
              for _, hsl, ksl in heads]
        pcs, pbs, dens = [], [], []
        for (h, _, _), sc, sb in zip(heads, lc, lb):
            sink = sink_ref[h] * LOG2E
            m = jnp.maximum(jnp.maximum(jnp.max(sc, axis=-1, keepdims=True),
                                        jnp.max(sb, axis=-1, keepdims=True)), sink)
            pc = jnp.exp2(sc - m)
            pb = jnp.exp2(sb - m)
            dens.append(jnp.sum(pc, axis=-1, keepdims=True) + jnp.sum(pb, axis=-1, keepdims=True)
                        + jnp.exp2(sink - m))
            pcs.append(pc.astype(BF16))
            pbs.append(pb.astype(BF16))
        outs = [(jnp.dot(pc, v_ref[0:c_len, ksl], preferred_element_type=F32)
                 + jnp.dot(pb, v_ref[pl.ds(start, band), ksl], preferred_element_type=F32)) / den
                for (_, _, ksl), pc, pb, den in zip(heads, pcs, pbs, dens)]
        o_ref[...] = jnp.concatenate(outs, axis=-1).astype(o_ref.dtype)


def _win_attention(q, k, v, sink, c_len):
    b, l, qw = q.shape
    kw = k.shape[-1]
    blk = TIME_TILE
    kern = functools.partial(_win_attn_kernel, c_len=c_len, l_len=l, nqc=c_len // blk)
    return pl.pallas_call(
        kern,
        grid=(b, l // blk),
        in_specs=[
            pl.BlockSpec(memory_space=pltpu.SMEM),
            pl.BlockSpec((None, blk, qw), lambda bb, j: (bb, j, 0)),
            pl.BlockSpec((None, l, kw), lambda bb, j: (bb, 0, 0)),
            pl.BlockSpec((None, l, kw), lambda bb, j: (bb, 0, 0)),
        ],
        out_specs=pl.BlockSpec((None, blk, qw), lambda bb, j: (bb, j, 0)),
        out_shape=jax.ShapeDtypeStruct((b, l, qw), BF16),
        compiler_params=_cparams(("arbitrary", "arbitrary")),
        name="window_attention",
    )(sink, q, k, v)


def _out_even_kernel(x_ref, rec_ref, gate_ref, att_ref, wa_ref, wb_ref, g1_ref, *refs):
    route_in, o_ref, route_out = refs[:5], refs[5], refs[6:]
    lru = (rec_ref[0] + rec_ref[1]) * jax.nn.gelu(gate_ref[...].astype(F32))
    y = (jnp.dot(lru.astype(BF16), wa_ref[...], preferred_element_type=F32)
         + jnp.dot(att_ref[...], wb_ref[...], preferred_element_type=F32))
    x = x_ref[...] + g1_ref[...] * y
    o_ref[...] = x
    _route(x, *route_in, *route_out)


def _out_even(xc, rec2, gate, att, w_a, w_b, mods3, layer, nct, g_ffn, w_router, b_router):
    b, l, d = xc.shape
    tm = ROW_TILE
    w = gate.shape[-1]
    row = lambda bb, i: jnp.where(i < nct, SUBLANES, bb)
    r_in, r_args, r_out, r_shapes, r_scratch = _route_plumbing(b, l, d, g_ffn, mods3, layer, row, w_router, b_router)
    return pl.pallas_call(
        _out_even_kernel,
        grid=(b, l // tm),
        in_specs=[
            pl.BlockSpec((None, tm, d), lambda bb, i: (bb, i, 0)),
            pl.BlockSpec((2, tm, w), lambda bb, i: (0, i, bb)),
            pl.BlockSpec((None, tm, w), lambda bb, i: (bb, i, 0)),
            pl.BlockSpec((None, tm, att.shape[-1]), lambda bb, i: (bb, i, 0)),
            pl.BlockSpec(w_a.shape, lambda bb, i: (0, 0)),
            pl.BlockSpec(w_b.shape, lambda bb, i: (0, 0)),
            _mod_spec(d, layer, 2, row),
        ] + r_in,
        out_specs=[pl.BlockSpec((None, tm, d), lambda bb, i: (bb, i, 0))] + r_out,
        out_shape=[jax.ShapeDtypeStruct((b, l, d), F32)] + r_shapes,
        scratch_shapes=r_scratch,
        compiler_params=_cparams(("arbitrary", "arbitrary")),
        name="out_proj_even",
    )(xc, rec2, gate, att, w_a, w_b, mods3, *r_args)


def _diff_attn_kernel(lam_ref, g_ref, q_ref, k_ref, v_ref, o_ref, *, lam_init):
    lv = lam_ref[...]
    lam = (jnp.exp(jnp.sum(lv[0:1] * lv[1:2], axis=-1, keepdims=True))
           - jnp.exp(jnp.sum(lv[2:3] * lv[3:4], axis=-1, keepdims=True)) + lam_init)
    vw = 2 * DIFF_DH

    def logits(h, mp):
        lo = h * vw + mp * DIFF_DH
        return _nt_dot(q_ref[:, lo:lo + DIFF_DH], k_ref[:, lo:lo + DIFF_DH])

    def softmax_parts(s):
        e = jnp.exp2(s - jnp.max(s, axis=-1, keepdims=True))
        return e, 1.0 / jnp.sum(e, axis=-1, keepdims=True)

    heads = range(DIFF_HEADS)
    ls = [(logits(h, 0), logits(h, 1)) for h in heads]
    ws = []
    for l0, l1 in ls:
        e0, r0 = softmax_parts(l0)
        e1, r1 = softmax_parts(l1)
        ws.append((e0 * r0 - e1 * (lam * r1)).astype(BF16))
    for h, w in zip(heads, ws):
        lo = h * vw
        o = jnp.dot(w, v_ref[:, lo:lo + vw], preferred_element_type=F32)
        ms = jnp.mean(o * o, axis=-1, keepdims=True)
        o = o * lax.rsqrt(ms + NORM_EPS) * g_ref[...]
        o_ref[:, lo:lo + vw] = (o * (1.0 - lam_init)).astype(o_ref.dtype)


def _diff_attention(q, k, v, lam_vecs, subln_g, lam_init, c_len):
    b, l, w = q.shape
    tq = ROW_TILE
    s_len = l - c_len
    off = c_len // tq
    return pl.pallas_call(
        functools.partial(_diff_attn_kernel, lam_init=lam_init),
        grid=(b, s_len // tq),
        in_specs=[
            pl.BlockSpec(lam_vecs.shape, lambda bb, j: (0, 0)),
            pl.BlockSpec((1, 2 * DIFF_DH), lambda bb, j: (0, 0)),
            pl.BlockSpec((None, tq, w), lambda bb, j: (bb, j + off, 0)),
            pl.BlockSpec((None, l, w), lambda bb, j: (bb, 0, 0)),
            pl.BlockSpec((None, l, w), lambda bb, j: (bb, 0, 0)),
        ],
        out_specs=pl.BlockSpec((None, tq, w), lambda bb, j: (bb, j, 0)),
        out_shape=jax.ShapeDtypeStruct((b, s_len, w), BF16),
        compiler_params=_cparams(("arbitrary", "arbitrary")),
        name="diff_attention",
    )(lam_vecs, subln_g.reshape(1, -1), q, k, v)


def _ssd_kernel(x_ref, xp_ref, xn_ref, dt_ref, cw_ref, cb_ref, dtb_ref, alog_ref, dsk_ref, o_ref,
                ext_scr, st_scr, *, q, nct, nt):
    d = pl.program_id(0)
    g = pl.program_id(2)
    tile = _seq_tile(d, g, nct, nt)
    pv = jnp.where((tile == 0) | (tile == nct), 0.0, 1.0)
    nv = jnp.where((tile == nct - 1) | (tile == nt - 1), 0.0, 1.0)
    ext_scr[0:SUBLANES] = xp_ref[...] * pv
    ext_scr[SUBLANES:SUBLANES + q] = x_ref[...]
    ext_scr[SUBLANES + q:2 * SUBLANES + q] = xn_ref[...] * nv

    @pl.when(g == 0)
    def _():
        st_scr[...] = jnp.zeros_like(st_scr)

    u = cb_ref[...] + cw_ref[0] * ext_scr[SUBLANES - 1:SUBLANES - 1 + q, :]
    for j in range(1, CONV_W):
        u = u + cw_ref[j] * ext_scr[SUBLANES - 1 + j:SUBLANES - 1 + j + q, :]
    act = _silu(u)

    dtr = dt_ref[...]
    dtr = jnp.where(d == 0, dtr, pltpu.roll(dtr, LANES - SSD_HEADS, 1))
    dtv = _softplus(dtr + dtb_ref[...])
    head_lane = lax.broadcasted_iota(jnp.int32, (1, LANES), 1) < SSD_HEADS
    dta = dtv * jnp.where(head_lane, -jnp.exp(alog_ref[...]), 0.0)
    ri = lax.broadcasted_iota(jnp.int32, (q, q), 0)
    ci = lax.broadcasted_iota(jnp.int32, (q, q), 1)
    keep = jnp.where(d == 0, ri - ci, ci - ri) >= 0
    cum = jnp.dot(keep.astype(F32), dta, preferred_element_type=F32, precision=HIGHEST)
    tot = jnp.sum(dta, axis=0, keepdims=True)
    cum_t = cum.T
    dt_t = dtv.T
    to_end = jnp.exp(tot - cum) * dtv
    e_cum = jnp.exp(cum)
    e_tot = jnp.exp(tot)
    dskip = dsk_ref[...] * jnp.where(d == 0, 1.0, 0.0)

    hpg = SSD_HEADS // SSD_GROUPS
    for gi in range(SSD_GROUPS):
        b_g = act[:, SSD_INNER + gi * SSD_STATE:SSD_INNER + (gi + 1) * SSD_STATE]
        c_lo = SSD_INNER + SSD_GROUPS * SSD_STATE + gi * SSD_STATE
        c_g = act[:, c_lo:c_lo + SSD_STATE].astype(BF16)
        cb = _nt_dot(c_g, b_g.astype(BF16))
        b_gt = b_g.T.astype(BF16)
        for hh in range(hpg):
            h = gi * hpg + hh
            xs = act[:, h * SSD_HEAD_DIM:(h + 1) * SSD_HEAD_DIM]
            seg = cum[:, h:h + 1] - cum_t[h:h + 1, :]
            decay = jnp.exp(jnp.where(keep, seg, NEG_BIG))
            w = (cb * decay * dt_t[h:h + 1, :]).astype(BF16)
            state = st_scr[h]
            y = jnp.dot(w, xs.astype(BF16), preferred_element_type=F32)
            y = y + jnp.dot(c_g, state.astype(BF16), preferred_element_type=F32) * e_cum[:, h:h + 1]
            y = y + dskip[:, h * SSD_HEAD_DIM:(h + 1) * SSD_HEAD_DIM] * xs
            o_ref[:, h * SSD_HEAD_DIM:(h + 1) * SSD_HEAD_DIM] = y
            s_new = jnp.dot(b_gt, (xs * to_end[:, h:h + 1]).astype(BF16), preferred_element_type=F32)
            st_scr[h] = e_tot[:, h:h + 1] * state + s_new


def _ssd(xbc, dt, conv_w, conv_b, dt_bias, a_log, d_skip, nct):
    b, l, cd = xbc.shape
    q = TIME_TILE
    nt = l // q
    tile = lambda d, bb, g: _seq_tile(d, g, nct, nt)
    r8 = q // SUBLANES
    pad = LANES - SSD_HEADS
    dtb = jnp.pad(dt_bias, ((0, 0), (0, pad))).reshape(2, 1, LANES)
    alog = jnp.pad(a_log, ((0, 0), (0, pad))).reshape(2, 1, LANES)
    dsk = jnp.repeat(d_skip, SSD_HEAD_DIM).reshape(1, SSD_INNER)
    return pl.pallas_call(
        functools.partial(_ssd_kernel, q=q, nct=nct, nt=nt),
        grid=(2, b, nt),
        in_specs=[
            pl.BlockSpec((None, q, cd), lambda d, bb, g: (bb, tile(d, bb, g), 0)),
            pl.BlockSpec((None, SUBLANES, cd),
                         lambda d, bb, g: (bb, jnp.maximum(tile(d, bb, g) * r8 - 1, 0), 0)),
            pl.BlockSpec((None, SUBLANES, cd),
                         lambda d, bb, g: (bb, jnp.minimum((tile(d, bb, g) + 1) * r8, l // SUBLANES - 1), 0)),
            pl.BlockSpec((None, q, LANES), lambda d, bb, g: (bb, tile(d, bb, g), 0)),
            pl.BlockSpec((CONV_W, 1, cd), lambda d, bb, g: (0, 0, 0)),
            pl.BlockSpec((1, cd), lambda d, bb, g: (0, 0)),
            pl.BlockSpec((None, 1, LANES), lambda d, bb, g: (d, 0, 0)),
            pl.BlockSpec((None, 1, LANES), lambda d, bb, g: (d, 0, 0)),
            pl.BlockSpec((1, SSD_INNER), lambda d, bb, g: (0, 0)),
        ],
        out_specs=pl.BlockSpec((None, None, q, SSD_INNER), lambda d, bb, g: (d, bb, tile(d, bb, g), 0)),
        out_shape=jax.ShapeDtypeStruct((2, b, l, SSD_INNER), F32),
        scratch_shapes=[
            pltpu.VMEM((q + 2 * SUBLANES, cd), F32),
            pltpu.VMEM((SSD_HEADS, SSD_STATE, SSD_HEAD_DIM), F32),
        ],
        compiler_params=_cparams(("arbitrary", "arbitrary", "arbitrary")),
        name="ssd_chunked",
    )(xbc, xbc, xbc, dt, conv_w.reshape(CONV_W, 1, cd), conv_b.reshape(1, cd), dtb, alog, dsk)


def _out_odd_kernel(x_ref, diff_ref, y_ref, z_ref, ng_ref, wa_ref, wb_ref, g1_ref, *refs):
    route_in, o_ref, route_out = refs[:5], refs[5], refs[6:]
    yz = (y_ref[0] + y_ref[1]) * _silu(z_ref[...].astype(F32))
    gs = SSD_INNER // SSD_GROUPS
    parts = []
    for gi in range(SSD_GROUPS):
        seg = yz[:, gi * gs:(gi + 1) * gs]
        ms = jnp.mean(seg * seg, axis=-1, keepdims=True)
        parts.append(seg * lax.rsqrt(ms + NORM_EPS) * ng_ref[:, gi * gs:(gi + 1) * gs])
    ssd = jnp.concatenate(parts, axis=-1).astype(BF16)
    y = (jnp.dot(diff_ref[...], wa_ref[...], preferred_element_type=F32)
         + jnp.dot(ssd, wb_ref[...], preferred_element_type=F32))
    x = x_ref[...] + g1_ref[...] * y
    o_ref[...] = x
    _route(x, *route_in, *route_out)


def _out_odd(xc, diff, y2, z, norm_g, w_a, w_b, mods3, layer, c_len, g_ffn, w_router, b_router):
    b, l, d = xc.shape
    s_len = l - c_len
    tm = ROW_TILE
    off = c_len // tm
    w = SSD_INNER
    row = lambda bb, i: bb
    r_in, r_args, r_out, r_shapes, r_scratch = _route_plumbing(b, s_len, d, g_ffn, mods3, layer, row, w_router,
                                                               b_router)
    return pl.pallas_call(
        _out_odd_kernel,
        grid=(b, s_len // tm),
        in_specs=[
            pl.BlockSpec((None, tm, d), lambda bb, i: (bb, i + off, 0)),
            pl.BlockSpec((None, tm, diff.shape[-1]), lambda bb, i: (bb, i, 0)),
            pl.BlockSpec((2, None, tm, w), lambda bb, i: (0, bb, i + off, 0)),
            pl.BlockSpec((None, tm, w), lambda bb, i: (bb, i + off, 0)),
            pl.BlockSpec((1, w), lambda bb, i: (0, 0)),
            pl.BlockSpec(w_a.shape, lambda bb, i: (0, 0)),
            pl.BlockSpec(w_b.shape, lambda bb, i: (0, 0)),
            _mod_spec(d, layer, 2, row),
        ] + r_in,
        out_specs=[pl.BlockSpec((None, tm, d), lambda bb, i: (bb, i, 0))] + r_out,
        out_shape=[jax.ShapeDtypeStruct((b, s_len, d), F32)] + r_shapes,
        scratch_shapes=r_scratch,
        compiler_params=_cparams(("arbitrary", "arbitrary")),
        name="out_proj_odd",
    )(xc, diff, y2, z, norm_g.reshape(1, w), w_a, w_b, mods3, *r_args)


def _route(x, g_ref, sh_ref, sc_ref, wr_ref, br_ref, h_ref, eid_ref, rnk_ref, gate_ref, cnt_ref, carry_scr):
    @pl.when((pl.program_id(0) == 0) & (pl.program_id(1) == 0))
    def _():
        carry_scr[...] = jnp.zeros_like(carry_scr)

    h = _norm_mod(x, g_ref[...], sh_ref[...], sc_ref[...])
    _store_chunk_rows(h_ref, _pack_bf16(h))
    tm = h.shape[0]
    per = N_EXPERTS // N_EXPERT_GROUPS
    logits = lax.dot_general(wr_ref[...], h, (((1,), (1,)), ((), ())),
                             preferred_element_type=F32, precision=HIGHEST)
    scores = jax.nn.sigmoid(logits)
    sel = scores + br_ref[...]
    sel3 = sel.reshape(N_EXPERT_GROUPS, per, tm)
    kio = lax.broadcasted_iota(jnp.int32, sel3.shape, 1)
    m1 = jnp.max(sel3, axis=1, keepdims=True)
    first = jnp.min(jnp.where(sel3 == m1, kio, per), axis=1, keepdims=True)
    m2 = jnp.max(jnp.where(kio == first, NEG_BIG, sel3), axis=1, keepdims=True)
    gs = m1 + m2
    gio = lax.broadcasted_iota(jnp.int32, gs.shape, 0)
    ahead = jnp.zeros(gs.shape, jnp.int32)
    for gp in range(N_EXPERT_GROUPS):
        other = gs[gp:gp + 1]
        ahead = ahead + jnp.where((other > gs) | ((other == gs) & (gp < gio)), 1, 0)
    grp_on = jnp.where(ahead < TOPK_GROUPS, 1.0, 0.0)
    selm = jnp.where(jnp.broadcast_to(grp_on, sel3.shape) > 0.5, sel3, NEG_BIG).reshape(N_EXPERTS, tm)
    eio = lax.broadcasted_iota(jnp.int32, selm.shape, 0)
    work = selm
    cf = jnp.zeros(selm.shape, F32)
    e_rows, s_rows = [], []
    for k in range(TOP_K):
        best = jnp.max(work, axis=0, keepdims=True)
        idx = jnp.min(jnp.where(work == best, eio, N_EXPERTS), axis=0, keepdims=True)
        hit = eio == idx
        cf = cf + jnp.where(hit, 1.0, 0.0)
        work = jnp.where(hit, NEG_BIG, work)
        e_rows.append(idx)
        s_rows.append(jnp.sum(jnp.where(hit, scores, 0.0), axis=0, keepdims=True))
    denom = s_rows[0]
    for s_k in s_rows[1:]:
        denom = denom + s_k
    g_rows = [s_k / denom * ROUTED_SCALE for s_k in s_rows]
    ti = lax.broadcasted_iota(jnp.int32, (tm, tm), 0)
    tj = lax.broadcasted_iota(jnp.int32, (tm, tm), 1)
    before = jnp.where(ti < tj, 1.0, 0.0).astype(BF16)
    in_expert = carry_scr[:, 0:1] + jnp.dot(cf.astype(BF16), before, preferred_element_type=F32)
    carry_scr[...] = carry_scr[...] + jnp.sum(cf, axis=1, keepdims=True)
    cnt_ref[...] = carry_scr[...]
    r_rows = [jnp.sum(jnp.where(eio == idx, in_expert, 0.0), axis=0, keepdims=True) for idx in e_rows]
    eid_ref[...] = jnp.concatenate(e_rows, axis=0)
    rnk_ref[...] = jnp.concatenate(r_rows, axis=0).astype(jnp.int32)
    padded = jnp.concatenate(g_rows + [jnp.zeros((LANES - TOP_K, tm), F32)], axis=0)
    gate_ref[...] = padded.T


def _route_plumbing(b, r, d, g, mods3, layer, row_fn, w_router, b_router):
    tm = ROW_TILE
    nt = r // tm
    w_router_t = w_router.T
    slot = pl.BlockSpec((TOP_K, tm), lambda bb, i: (0, bb * nt + i))
    slot_shape = jax.ShapeDtypeStruct((TOP_K, b * r), jnp.int32)
    in_specs = [
        pl.BlockSpec((1, d), lambda bb, i: (0, 0)),
        _mod_spec(d, layer, 3, row_fn),
        _mod_spec(d, layer, 4, row_fn),
        pl.BlockSpec(w_router_t.shape, lambda bb, i: (0, 0)),
        pl.BlockSpec((N_EXPERTS, 1), lambda bb, i: (0, 0)),
    ]
    args = [g.reshape(1, d), mods3, mods3, w_router_t, b_router.reshape(N_EXPERTS, 1)]
    out_specs = [
        pl.BlockSpec((tm * ROW_CHUNKS, LANES), lambda bb, i: (bb * nt + i, 0)),
        slot,
        slot,
        pl.BlockSpec((None, tm, LANES), lambda bb, i: (bb, i, 0)),
        pl.BlockSpec((N_EXPERTS, LANES), lambda bb, i: (0, 0)),
    ]
    out_shapes = [jax.ShapeDtypeStruct((b * r * ROW_CHUNKS, LANES), jnp.uint32), slot_shape, slot_shape,
                  jax.ShapeDtypeStruct((b, r, LANES), F32), jax.ShapeDtypeStruct((N_EXPERTS, LANES), F32)]
    scratch = [pltpu.VMEM((N_EXPERTS, LANES), F32)]
    return in_specs, args, out_specs, out_shapes, scratch


def _moe_plan(counts, n_rows):
    blk = EXPERT_BLK
    nb = n_rows // blk
    ends = jnp.cumsum(counts)
    starts = ends - counts
    count_le = lambda sorted_vals, q: jnp.sum(sorted_vals[None, :] <= q[:, None], axis=1, dtype=jnp.int32)
    first = jnp.arange(nb, dtype=jnp.int32) * blk
    e_lo = count_le(ends, first)
    e_hi = count_le(ends, first + (blk - 1))
    n_pair = e_hi - e_lo + 1
    p_end = jnp.cumsum(n_pair)
    p_start = p_end - n_pair
    i = jnp.arange(nb + N_EXPERTS - 1, dtype=jnp.int32)
    j = jnp.minimum(count_le(p_end, i), nb - 1)
    valid = i < p_end[-1]
    e = jnp.where(valid, e_lo[j] + i - p_start[j], e_hi[nb - 1]).astype(jnp.int32)
    bounds = jnp.concatenate([starts, ends[-1:]]).astype(jnp.int32)
    return j, e, valid.astype(jnp.int32), bounds


def _positions_kernel(starts_ref, eid_ref, rnk_ref, pos_ref):
    eid = eid_ref[...]
    pos = rnk_ref[...]
    for e in range(N_EXPERTS):
        pos = pos + jnp.where(eid == e, starts_ref[e], 0)
    pos_ref[...] = pos * ROW_CHUNKS


def _positions(eid, rnk, starts):
    full = pl.BlockSpec(eid.shape, lambda: (0, 0))
    return pl.pallas_call(
        _positions_kernel,
        in_specs=[pl.BlockSpec(memory_space=pltpu.SMEM), full, full],
        out_specs=full,
        out_shape=jax.ShapeDtypeStruct(eid.shape, jnp.int32),
        compiler_params=pltpu.CompilerParams(vmem_limit_bytes=VMEM_LIMIT),
        name="moe_positions",
    )(starts, eid, rnk)


def _token_row(ref, first):
    return ref.at[pl.ds(pl.multiple_of(first, ROW_CHUNKS), ROW_CHUNKS)]


def _dispatch_kernel(pos_ref, h_ref, xs_ref, sem):
    tm = h_ref.shape[0] // ROW_CHUNKS

    def issue(t, carry):
        src = _token_row(h_ref, t * ROW_CHUNKS)
        for k in range(TOP_K):
            pltpu.make_async_copy(src, _token_row(xs_ref, pos_ref[k, t]), sem).start(priority=k % 2)
        return carry

    lax.fori_loop(0, tm, issue, 0)
    done = pl.ds(0, tm * ROW_CHUNKS)
    for _ in range(TOP_K):
        pltpu.make_async_copy(h_ref.at[done], xs_ref.at[done], sem).wait()


def _dispatch(h2, pos):
    rows, w = h2.shape
    tm = math.gcd(DISPATCH_TILE, rows // ROW_CHUNKS)
    return pl.pallas_call(
        _dispatch_kernel,
        grid=(rows // (tm * ROW_CHUNKS),),
        in_specs=[
            pl.BlockSpec((TOP_K, tm), lambda i: (0, i), memory_space=pltpu.SMEM),
            pl.BlockSpec((tm * ROW_CHUNKS, w), lambda i: (i, 0)),
        ],
        out_specs=pl.BlockSpec(memory_space=pl.ANY),
        out_shape=jax.ShapeDtypeStruct((rows * TOP_K, w), h2.dtype),
        scratch_shapes=[pltpu.SemaphoreType.DMA],
        compiler_params=_cparams(("arbitrary",)),
        name="moe_dispatch",
    )(pos, h2)


def _grouped_kernel(pb_ref, pe_ref, pv_ref, bnd_ref, xs_ref, wg_ref, wu_ref, wd_ref, y_ref, wgb, wub, wdb):
    i = pl.program_id(0)
    prev = jnp.maximum(i - 1, 0)
    j = pb_ref[i]
    e = pe_ref[i]
    blk = xs_ref.shape[0] // ROW_CHUNKS

    @pl.when((i == 0) | (pb_ref[prev] != j))
    def _():
        y_ref[...] = jnp.zeros_like(y_ref)

    @pl.when((i == 0) | (pe_ref[prev] != e))
    def _():
        wgb[...] = wg_ref[...].astype(BF16)
        wub[...] = wu_ref[...].astype(BF16)
        wdb[...] = wd_ref[...].astype(BF16)

    @pl.when(pv_ref[i] == 1)
    def _():
        xw = _load_chunk_rows(xs_ref, blk)
        a = _packed_dot(xw, wgb)
        u = _packed_dot(xw, wub)
        yv = jnp.dot((_silu(a) * u).astype(BF16), wdb[...], preferred_element_type=F32)
        rows = j * blk + lax.broadcasted_iota(jnp.int32, (blk, 1), 0)
        own = (rows >= bnd_ref[e]) & (rows < bnd_ref[e + 1])
        yw = _pack_bf16(yv)
        for c in range(ROW_CHUNKS):
            sl = pl.ds(c, blk, stride=ROW_CHUNKS)
            y_ref[sl, :] = jnp.where(own, yw[:, c * LANES:(c + 1) * LANES], y_ref[sl, :])


def _grouped(pb, pe, pv, bounds, xs, wg, wu, wd, layer):
    p, half = xs.shape
    d = 2 * ROW_CHUNKS * LANES
    blk = EXPERT_BLK * ROW_CHUNKS
    grid_spec = pltpu.PrefetchScalarGridSpec(
        num_scalar_prefetch=4,
        grid=(pb.shape[0],),
        in_specs=[
            pl.BlockSpec((blk, half), lambda i, pb, pe, pv, bnd: (pb[i], 0)),
            pl.BlockSpec((None, None, d, D_EXPERT), lambda i, pb, pe, pv, bnd: (layer, pe[i], 0, 0)),
            pl.BlockSpec((None, None, d, D_EXPERT), lambda i, pb, pe, pv, bnd: (layer, pe[i], 0, 0)),
            pl.BlockSpec((None, None, D_EXPERT, d), lambda i, pb, pe, pv, bnd: (layer, pe[i], 0, 0)),
        ],
        out_specs=pl.BlockSpec((blk, half), lambda i, pb, pe, pv, bnd: (pb[i], 0)),
        scratch_shapes=[
            pltpu.VMEM((d, D_EXPERT), BF16),
            pltpu.VMEM((d, D_EXPERT), BF16),
            pltpu.VMEM((D_EXPERT, d), BF16),
        ],
    )
    return pl.pallas_call(
        _grouped_kernel,
        grid_spec=grid_spec,
        out_shape=jax.ShapeDtypeStruct((p, half), jnp.uint32),
        compiler_params=_cparams(("arbitrary",)),
        name="moe_grouped_experts",
    )(pb, pe, pv, bounds, xs, wg, wu, wd)


def _combine_kernel(*refs, final):
    pos_ref, y_ref, gate_ref, h_ref, x_ref, g2_ref, sg_ref, su_ref, sd_ref = refs[:9]
    o_ref, buf, sem = refs[-3:]
    tm = x_ref.shape[0]

    def issue(t, carry):
        for k in range(TOP_K):
            pltpu.make_async_copy(_token_row(y_ref, pos_ref[k, t]), _token_row(buf.at[k], t * ROW_CHUNKS),
                                  sem).start(priority=k % 2)
        return carry

    lax.fori_loop(0, tm, issue, 0)
    hw = _load_chunk_rows(h_ref, tm)
    a = _packed_dot(hw, sg_ref)
    u = _packed_dot(hw, su_ref)
    acc = jnp.dot((_silu(a) * u).astype(BF16), sd_ref[...], preferred_element_type=F32)
    done = pl.ds(0, tm * ROW_CHUNKS)
    for k in range(TOP_K):
        pltpu.make_async_copy(y_ref.at[done], buf.at[k, done], sem).wait()
    g = gate_ref[...]
    half = hw.shape[-1]
    acc_hi = acc[:, :half]
    acc_lo = acc[:, half:]
    for k in range(TOP_K):
        hi, lo = _unpack_bf16(_load_chunk_rows(buf.at[k], tm))
        acc_hi = acc_hi + g[:, k:k + 1] * hi
        acc_lo = acc_lo + g[:, k:k + 1] * lo
    x = x_ref[...] + g2_ref[...] * jnp.concatenate([acc_hi, acc_lo], axis=-1)
    if final:
        gf_ref = refs[9]
        ms = jnp.mean(x * x, axis=-1, keepdims=True)
        x = x * lax.rsqrt(ms + NORM_EPS) * gf_ref[...]
    o_ref[...] = x


def _combine(pos, y, gates, h2, x, mods3, layer, row_fn, tm, sg, su, sd, g_final=None):
    b, r, d = x.shape
    nt = r // tm
    tile = pl.BlockSpec((None, tm, d), lambda bb, i: (bb, i, 0))
    in_specs = [
        pl.BlockSpec((TOP_K, tm), lambda bb, i: (0, bb * nt + i), memory_space=pltpu.SMEM),
        pl.BlockSpec(memory_space=pl.ANY),
        pl.BlockSpec((None, tm, LANES), lambda bb, i: (bb, i, 0)),
        pl.BlockSpec((tm * ROW_CHUNKS, LANES), lambda bb, i: (bb * nt + i, 0)),
        tile,
        _mod_spec(d, layer, 5, row_fn),
        pl.BlockSpec(sg.shape, lambda bb, i: (0, 0)),
        pl.BlockSpec(su.shape, lambda bb, i: (0, 0)),
        pl.BlockSpec(sd.shape, lambda bb, i: (0, 0)),
    ]
    args = [pos, y, gates, h2, x, mods3, sg, su, sd]
    if g_final is not None:
        in_specs.append(pl.BlockSpec((1, d), lambda bb, i: (0, 0)))
        args.append(g_final.reshape(1, d))
    return pl.pallas_call(
        functools.partial(_combine_kernel, final=g_final is not None),
        grid=(b, nt),
        in_specs=in_specs,
        out_specs=tile,
        out_shape=jax.ShapeDtypeStruct((b, r, d), F32),
        scratch_shapes=[pltpu.VMEM((TOP_K, tm * ROW_CHUNKS, LANES), jnp.uint32), pltpu.SemaphoreType.DMA],
        compiler_params=_cparams(("arbitrary", "arbitrary")),
        name="moe_combine",
    )(*args)


def _moe(x, routed, mods3, layer, row_fn, combine_tile, w_e_gate, w_e_up, w_e_down, ws_gate, ws_up, ws_down,
         g_final=None):
    b, r, d = x.shape
    h2, eid, rnk, gates, cnt = routed
    pb, pe, pv, bounds = _moe_plan(cnt[:, 0].astype(jnp.int32), b * r * TOP_K)
    pos = _positions(eid, rnk, bounds[:N_EXPERTS])
    xs = _dispatch(h2, pos)
    y = _grouped(pb, pe, pv, bounds, xs, w_e_gate, w_e_up, w_e_down, layer)
    return _combine(pos, y, gates, h2, x, mods3, layer, row_fn, combine_tile,
                    ws_gate.astype(BF16), ws_up.astype(BF16), ws_down.astype(BF16), g_final)


def kernel(x, c, ctx, c_ctx, w_mod, b_mod, g_mix, g_ffn, g_final, ab_w_in, ab_w_out, ab_conv_w, ab_conv_b, ab_w_r, ab_b_r, ab_w_i, ab_b_i, ab_lam, ab_sink, cd_w_in, cd_w_out, cd_lam, cd_subln_g, cd_conv_w, cd_conv_b, cd_dt_bias, cd_a_log, cd_d_skip, cd_norm_g, w_router, b_router, w_e_gate, w_e_up, w_e_down, ws_gate, ws_up, ws_down):
    bsz, s_len, d = x.shape
    c_len = ctx.shape[1]
    depth = w_mod.shape[0]
    assert depth == 2 and bsz == SUBLANES, "kernels are specialised to depth 2 and batch 8"
    assert c_len % ROW_TILE == 0 and s_len % ROW_TILE == 0
    nct_row = c_len // ROW_TILE
    nct_time = c_len // TIME_TILE

    c_all = jnp.concatenate([c, c_ctx[None], jnp.zeros((MOD_ROWS - bsz - 1, d), F32)], axis=0)
    mods3 = _modulations(c_all, w_mod, b_mod).reshape(depth * MOD_ROWS, 1, N_MOD * d)
    rope_tabs = _rope_tables(c_len, s_len)
    xc = jnp.concatenate([ctx, x], axis=1)
    row_mixed = lambda bb, i: jnp.where(i < nct_row, SUBLANES, bb)
    row_latent = lambda bb, i: bb

    w_in = ab_w_in[0].astype(BF16)
    q_hi = LRU_WIDTH + WIN_HEADS * HEAD_DIM
    x_hi = q_hi + LRU_WIDTH
    k_hi = x_hi + WIN_KV_HEADS * HEAD_DIM
    gate, q, xa, k, v = _project(xc, g_mix[0], mods3, 0, nct_row, rope_tabs, [
        (w_in[:, :LRU_WIDTH], None, BF16, False),
        (w_in[:, LRU_WIDTH:q_hi], HEAD_DIM ** -0.5 * LOG2E, BF16, False),
        (w_in[:, q_hi:x_hi], None, F32, True),
        (w_in[:, x_hi:k_hi], 1.0, BF16, False),
        (w_in[:, k_hi:], None, BF16, False),
    ])
    l_len = c_len + s_len
    w_gates = jnp.stack([jnp.concatenate([_block_diag(ab_w_r[0, dd]), _block_diag(ab_w_i[0, dd])], axis=1)
                         for dd in range(2)]).astype(BF16)
    b_gates = jnp.concatenate([ab_b_r[0], ab_b_i[0]], axis=-1).reshape(2, 1, 2 * LRU_WIDTH)
    rec = _rglru(xa.reshape(l_len, bsz, LRU_WIDTH), ab_conv_w[0], ab_conv_b[0], w_gates, b_gates,
                 ab_lam[0].reshape(2, 1, LRU_WIDTH), nct_time)
    att = _win_attention(q, k, v, ab_sink[0], c_len)
    w_out = ab_w_out[0].astype(BF16)
    xc, *routed = _out_even(xc, rec.reshape(2, l_len, bsz * LRU_WIDTH), gate, att, w_out[:LRU_WIDTH],
                            w_out[LRU_WIDTH:], mods3, 0, nct_row, g_ffn[0], w_router[0], b_router[0])
    xc = _moe(xc, routed, mods3, 0, row_mixed, ROW_TILE, w_e_gate, w_e_up, w_e_down,
              ws_gate[0], ws_up[0], ws_down[0])

    w_in = cd_w_in[0].astype(BF16)
    qk = DIFF_HEADS * 2 * DIFF_DH
    z_hi = qk + SSD_INNER
    k_hi = z_hi + qk
    v_hi = k_hi + qk
    x_hi = v_hi + SSD_CONV_DIM
    w_dt = jnp.pad(w_in[:, x_hi:], ((0, 0), (0, LANES - 2 * SSD_HEADS)))
    q, z, k, v, xbc, dt = _project(xc, g_mix[1], mods3, 1, nct_row, rope_tabs, [
        (w_in[:, :qk], DIFF_DH ** -0.5 * LOG2E, BF16, False),
        (w_in[:, qk:z_hi], None, BF16, False),
        (w_in[:, z_hi:k_hi], 1.0, BF16, False),
        (w_in[:, k_hi:v_hi], None, BF16, False),
        (w_in[:, v_hi:x_hi], None, F32, False),
        (w_dt, None, F32, False),
    ])
    lam_init = 0.8 - 0.6 * math.exp(-0.3 * 1)
    diff = _diff_attention(q, k, v, cd_lam[0], cd_subln_g[0], lam_init, c_len)
    y2 = _ssd(xbc, dt, cd_conv_w[0], cd_conv_b[0], cd_dt_bias[0], cd_a_log[0], cd_d_skip[0], nct_time)
    w_out = cd_w_out[0].astype(BF16)
    xl, *routed = _out_odd(xc, diff, y2, z, cd_norm_g[0], w_out[:qk], w_out[qk:], mods3, 1, c_len,
                           g_ffn[1], w_router[1], b_router[1])
    return _moe(xl, routed, mods3, 1, row_latent, math.gcd(LATENT_COMBINE_TILE, s_len), w_e_gate, w_e_up, w_e_down,
                ws_gate[1], ws_up[1], ws_down[1], g_final=g_final)
```

```python
import functools
import math

import jax
import jax.numpy as jnp
from jax import lax
from jax.experimental import pallas as pl
from jax.experimental.pallas import tpu as pltpu

F32 = jnp.float32
BF16 = jnp.bfloat16
HIGHEST = lax.Precision.HIGHEST

GRID_W = 64
N_MOD = 6
NORM_EPS = 1e-6
ROPE_BASE = 10000.0
CONV_W = 4

LRU_WIDTH = 512
LRU_BLOCKS = 8
LRU_C = 8.0

HEAD_DIM = 64
WIN_HEADS = 8
WIN_KV_HEADS = 2
WINDOW = 128

DIFF_HEADS = 4
DIFF_DH = 64

SSD_HEADS = 8
SSD_HEAD_DIM = 64
SSD_INNER = SSD_HEADS * SSD_HEAD_DIM
SSD_GROUPS = 2
SSD_STATE = 128
SSD_CONV_DIM = SSD_INNER + 2 * SSD_GROUPS * SSD_STATE

N_EXPERTS = 64
N_EXPERT_GROUPS = 8
TOPK_GROUPS = 4
TOP_K = 8
D_EXPERT = 256
ROUTED_SCALE = 2.5

LANES = 128
SUBLANES = 8
MOD_ROWS = 16
TIME_TILE = 128
ROW_TILE = 256
DISPATCH_TILE = 2048
LATENT_COMBINE_TILE = 512
MIXED_COMBINE_TILE = 768
EXPERT_BLK = 1024
ROW_CHUNKS = 4
VMEM_LIMIT = 48 * 1024 * 1024
NEG_BIG = -1e30
LOG2E = math.log2(math.e)


def _cparams(sem):
    return pltpu.CompilerParams(dimension_semantics=sem, vmem_limit_bytes=VMEM_LIMIT)


def _nt_dot(a, b):
    return lax.dot_general(a, b, (((1,), (1,)), ((), ())), preferred_element_type=F32)


def _softplus(x):
    return jnp.maximum(x, 0.0) + jnp.log1p(jnp.exp(-jnp.abs(x)))


def _silu(x):
    return x * jax.nn.sigmoid(x)


def _pack_bf16(x):
    half = x.shape[-1] // 2
    bits = pltpu.bitcast(x.astype(BF16).astype(F32), jnp.uint32)
    return bits[:, :half] | (bits[:, half:] >> 16)


def _unpack_bf16(w):
    hi = pltpu.bitcast(w & jnp.uint32(0xFFFF0000), F32)
    lo = pltpu.bitcast(w << 16, F32)
    return hi, lo


def _store_chunk_rows(ref, w):
    n = w.shape[0]
    for j in range(ROW_CHUNKS):
        ref[pl.ds(j, n, stride=ROW_CHUNKS), :] = w[:, j * LANES:(j + 1) * LANES]


def _load_chunk_rows(ref, n):
    return jnp.concatenate([ref[pl.ds(j, n, stride=ROW_CHUNKS), :] for j in range(ROW_CHUNKS)], axis=1)


def _packed_dot(w, weight_ref):
    half = w.shape[-1]
    hi, lo = _unpack_bf16(w)
    return (jnp.dot(hi.astype(BF16), weight_ref[:half, :], preferred_element_type=F32)
            + jnp.dot(lo.astype(BF16), weight_ref[half:, :], preferred_element_type=F32))


def _mod_kernel(c_ref, w_ref, b_ref, o_ref):
    c = c_ref[...]
    s = _silu(c)
    o_ref[...] = jnp.dot(s, w_ref[...], preferred_element_type=F32, precision=HIGHEST) + b_ref[...]


def _modulations(c_all, w_mod, b_mod):
    depth, d, _ = w_mod.shape
    return pl.pallas_call(
        _mod_kernel,
        grid=(depth, N_MOD),
        in_specs=[
            pl.BlockSpec((MOD_ROWS, d), lambda l, k: (0, 0)),
            pl.BlockSpec((None, d, d), lambda l, k: (l, 0, k)),
            pl.BlockSpec((None, 1, d), lambda l, k: (l, 0, k)),
        ],
        out_specs=pl.BlockSpec((None, MOD_ROWS, d), lambda l, k: (l, 0, k)),
        out_shape=jax.ShapeDtypeStruct((depth, MOD_ROWS, N_MOD * d), F32),
        compiler_params=_cparams(("arbitrary", "arbitrary")),
        name="adaln_modulation",
    )(c_all, w_mod, b_mod.reshape(depth, 1, N_MOD * d))


def _mod_spec(d, layer, chunk, row_fn):
    return pl.BlockSpec((None, 1, d), lambda b, i: (layer * MOD_ROWS + row_fn(b, i), 0, chunk))


def _norm_mod(x, g, sh, sc):
    ms = jnp.mean(x * x, axis=-1, keepdims=True)
    return (x * lax.rsqrt(ms + NORM_EPS) * g) * (1.0 + sc) + sh


def _rope(y, cos, sa, sb):
    n = y.shape[-1]
    half = HEAD_DIM // 2
    return y * cos + pltpu.roll(y, n - half, 1) * sa + pltpu.roll(y, half, 1) * sb


def _proj_kernel(*refs, ropes):
    n = len(ropes)
    x_ref, g_ref, sh_ref, sc_ref, cos_ref, sa_ref, sb_ref = refs[:7]
    w_refs = refs[7:7 + n]
    o_refs = refs[7 + n:]
    h = _norm_mod(x_ref[...], g_ref[...], sh_ref[...], sc_ref[...]).astype(BF16)
    for w_ref, o_ref, rope in zip(w_refs, o_refs, ropes):
        y = jnp.dot(h, w_ref[...], preferred_element_type=F32)
        if rope is not None:
            w = y.shape[-1]
            y = _rope(y, cos_ref[:, :w], sa_ref[:, :w], sb_ref[:, :w])
            if rope != 1.0:
                y = y * rope
        o_ref[...] = y.astype(o_ref.dtype)


def _project(xc, g, mods3, layer, nct, rope_tabs, groups):
    b, l, d = xc.shape
    tm = ROW_TILE
    mod_row = lambda i, bb: layer * MOD_ROWS + jnp.where(i < nct, SUBLANES, bb)
    rw = rope_tabs[0].shape[-1]
    in_specs = [
        pl.BlockSpec((None, tm, d), lambda i, bb: (bb, i, 0)),
        pl.BlockSpec((1, d), lambda i, bb: (0, 0)),
        pl.BlockSpec((None, 1, d), lambda i, bb: (mod_row(i, bb), 0, 0)),
        pl.BlockSpec((None, 1, d), lambda i, bb: (mod_row(i, bb), 0, 1)),
    ] + [pl.BlockSpec((tm, rw), lambda i, bb: (i, 0))] * 3
    out_specs, out_shapes = [], []
    for w, _, dt, time_major in groups:
        n = w.shape[1]
        in_specs.append(pl.BlockSpec((d, n), lambda i, bb: (0, 0)))
        if time_major:
            out_specs.append(pl.BlockSpec((tm, n), lambda i, bb: (i, bb)))
            out_shapes.append(jax.ShapeDtypeStruct((l, b * n), dt))
        else:
            out_specs.append(pl.BlockSpec((None, tm, n), lambda i, bb: (bb, i, 0)))
            out_shapes.append(jax.ShapeDtypeStruct((b, l, n), dt))
    return pl.pallas_call(
        functools.partial(_proj_kernel, ropes=tuple(gp[1] for gp in groups)),
        grid=(l // tm, b),
        in_specs=in_specs,
        out_specs=out_specs,
        out_shape=out_shapes,
        compiler_params=_cparams(("arbitrary", "arbitrary")),
        name="norm_mod_project",
    )(xc, g.reshape(1, d), mods3, mods3, *rope_tabs, *[gp[0] for gp in groups])


def _rope_tables(c_len, s_len):
    rows = s_len // GRID_W
    row = jnp.repeat(jnp.arange(rows), GRID_W).astype(F32)
    col = jnp.tile(jnp.arange(GRID_W), rows).astype(F32)
    n = HEAD_DIM // 4
    inv = ROPE_BASE ** (-jnp.arange(n, dtype=F32) / n)
    ang = jnp.concatenate([row[:, None] * inv, col[:, None] * inv], axis=-1)
    cos, sin = jnp.cos(ang), jnp.sin(ang)
    zero = jnp.zeros_like(sin)
    reps = WIN_HEADS
    cos_t = jnp.tile(jnp.concatenate([cos, cos], axis=-1), (1, reps))
    sa_t = jnp.tile(jnp.concatenate([-sin, zero], axis=-1), (1, reps))
    sb_t = jnp.tile(jnp.concatenate([zero, sin], axis=-1), (1, reps))
    w = cos_t.shape[-1]
    pad1 = jnp.ones((c_len, w), F32)
    pad0 = jnp.zeros((c_len, w), F32)
    return (jnp.concatenate([pad1, cos_t], 0), jnp.concatenate([pad0, sa_t], 0),
            jnp.concatenate([pad0, sb_t], 0))


def _seq_tile(d, g, nct, nt):
    rev = jnp.where(g < nct, nct - 1 - g, nt - 1 - (g - nct))
    return jnp.where(d == 0, g, rev)


def _rglru_kernel(x_ref, xp_ref, xn_ref, cw_ref, cb_ref, w_ref, bias_ref, lam_ref, o_ref,
                  ext_scr, a_scr, b_scr, h_scr, *, ts, nct, nt, sub):
    d = pl.program_id(0)
    g = pl.program_id(1)
    tile = _seq_tile(d, g, nct, nt)
    bsz, width = h_scr.shape
    pv = jnp.where((tile == 0) | (tile == nct), 0.0, 1.0)
    nv = jnp.where((tile == nct - 1) | (tile == nt - 1), 0.0, 1.0)
    ext_scr[0:1] = xp_ref[...] * pv
    ext_scr[1:ts + 1] = x_ref[...]
    ext_scr[ts + 1:ts + 3] = xn_ref[...] * nv

    @pl.when(g == 0)
    def _():
        h_scr[...] = jnp.zeros_like(h_scr)

    neg_sp = -LRU_C * _softplus(-lam_ref[...])

    def prep(c, carry):
        r0 = pl.multiple_of(c * sub, sub)
        e = ext_scr[pl.ds(r0, sub + CONV_W - 1)]
        u = cb_ref[...] + cw_ref[0] * e[0:sub]
        for j in range(1, CONV_W):
            u = u + cw_ref[j] * e[j:j + sub]
        u2 = u.reshape(sub * bsz, width)
        gts = jnp.dot(u2.astype(BF16), w_ref[...], preferred_element_type=F32) + bias_ref[...]
        r = jax.nn.sigmoid(gts[:, :width])
        ig = jax.nn.sigmoid(gts[:, width:])
        log_a = neg_sp * r
        a = jnp.exp(log_a)
        mult = jnp.sqrt(1.0 - a * a)
        a_scr[pl.ds(r0, sub)] = a.reshape(sub, bsz, width)
        b_scr[pl.ds(r0, sub)] = (mult * ig * u2).reshape(sub, bsz, width)
        return carry

    lax.fori_loop(0, ts // sub, prep, 0)

    def step(t, h):
        tt = jnp.where(d == 0, t, ts - 1 - t)
        h = a_scr[tt] * h + b_scr[tt]
        o_ref[tt] = h
        return h

    h_scr[...] = lax.fori_loop(0, ts, step, h_scr[...], unroll=8)


def _rglru(xa_tm, conv_w, conv_b, w_gates, b_gates, lam, nct):
    l, bsz, width = xa_tm.shape
    ts = TIME_TILE
    nt = l // ts
    tile = lambda d, g: _seq_tile(d, g, nct, nt)
    kern = functools.partial(_rglru_kernel, ts=ts, nct=nct, nt=nt, sub=16)
    return pl.pallas_call(
        kern,
        grid=(2, nt),
        in_specs=[
            pl.BlockSpec((ts, bsz, width), lambda d, g: (tile(d, g), 0, 0)),
            pl.BlockSpec((1, bsz, width), lambda d, g: (jnp.maximum(tile(d, g) * ts - 1, 0), 0, 0)),
            pl.BlockSpec((2, bsz, width),
                         lambda d, g: (jnp.minimum((tile(d, g) + 1) * (ts // 2), l // 2 - 1), 0, 0)),
            pl.BlockSpec((CONV_W, 1, width), lambda d, g: (0, 0, 0)),
            pl.BlockSpec((1, width), lambda d, g: (0, 0)),
            pl.BlockSpec((None, width, 2 * width), lambda d, g: (d, 0, 0)),
            pl.BlockSpec((None, 1, 2 * width), lambda d, g: (d, 0, 0)),
            pl.BlockSpec((None, 1, width), lambda d, g: (d, 0, 0)),
        ],
        out_specs=pl.BlockSpec((None, ts, bsz, width), lambda d, g: (d, tile(d, g), 0, 0)),
        out_shape=jax.ShapeDtypeStruct((2, l, bsz, width), F32),
        scratch_shapes=[
            pltpu.VMEM((ts + CONV_W - 1, bsz, width), F32),
            pltpu.VMEM((ts, bsz, width), F32),
            pltpu.VMEM((ts, bsz, width), F32),
            pltpu.VMEM((bsz, width), F32),
        ],
        compiler_params=_cparams(("arbitrary", "arbitrary")),
        name="rglru_scan",
    )(xa_tm, xa_tm, xa_tm, conv_w.reshape(CONV_W, 1, width), conv_b.reshape(1, width),
      w_gates, b_gates, lam)


def _block_diag(w):
    nb, c, dd = w.shape
    eye = jnp.eye(nb, dtype=w.dtype)
    return (eye[:, None, :, None] * w[:, :, None, :]).reshape(nb * c, nb * dd)


def _win_attn_kernel(sink_ref, q_ref, k_ref, v_ref, o_ref, *, c_len, l_len, nqc):
    j = pl.program_id(1)
    blk = q_ref.shape[0]
    grp = WIN_HEADS // WIN_KV_HEADS
    band = 3 * blk
    heads = [(h, slice(h * HEAD_DIM, (h + 1) * HEAD_DIM),
              slice((h // grp) * HEAD_DIM, (h // grp + 1) * HEAD_DIM)) for h in range(WIN_HEADS)]

    @pl.when(j < nqc)
    def _():
        logits = [_nt_dot(q_ref[:, hsl], k_ref[0:c_len, ksl]) for _, hsl, ksl in heads]
        probs, dens = [], []
        for (h, _, _), s in zip(heads, logits):
            sink = sink_ref[h] * LOG2E
            m = jnp.maximum(jnp.max(s, axis=-1, keepdims=True), sink)
            p = jnp.exp2(s - m)
            dens.append(jnp.sum(p, axis=-1, keepdims=True) + jnp.exp2(sink - m))
            probs.append(p.astype(BF16))
        outs = [jnp.dot(p, v_ref[0:c_len, ksl], preferred_element_type=F32) / den
                for (_, _, ksl), p, den in zip(heads, probs, dens)]
        o_ref[...] = jnp.concatenate(outs, axis=-1).astype(o_ref.dtype)

    @pl.when(j >= nqc)
    def _():
        jb = j - nqc
        start = jnp.clip(c_len + (jb - 1) * blk, c_len - blk, l_len - band)
        start = pl.multiple_of(start, blk)
        qpos = jb * blk + lax.broadcasted_iota(jnp.int32, (blk, band), 0)
        kpos = start - c_len + lax.broadcasted_iota(jnp.int32, (blk, band), 1)
        valid = (jnp.abs(qpos - kpos) <= WINDOW) & (kpos >= 0)
        lc = [_nt_dot(q_ref[:, hsl], k_ref[0:c_len, ksl]) for _, hsl, ksl in heads]
        lb = [jnp.where(valid, _nt_dot(q_ref[:, hsl], k_ref[pl.ds(start, band), ksl]), NEG_BIG)
              for _, hsl, ksl in heads]
        pcs, pbs, dens = [], [], []
        for (h, _, _), sc, sb in zip(heads, lc, lb):
            sink = sink_ref[h] * LOG2E
            m = jnp.maximum(jnp.maximum(jnp.max(sc, axis=-1, keepdims=True),
                                        jnp.max(sb, axis=-1, keepdims=True)), sink)
            pc = jnp.exp2(sc - m)
            pb = jnp.exp2(sb - m)
            dens.append(jnp.sum(pc, axis=-1, keepdims=True) + jnp.sum(pb, axis=-1, keepdims=True)
                        + jnp.exp2(sink - m))
            pcs.append(pc.astype(BF16))
            pbs.append(pb.astype(BF16))
        outs = [(jnp.dot(pc, v_ref[0:c_len, ksl], preferred_element_type=F32)
                 + jnp.dot(pb, v_ref[pl.ds(start, band), ksl], preferred_element_type=F32)) / den
                for (_, _, ksl), pc, pb, den in zip(heads, pcs, pbs, dens)]
        o_ref[...] = jnp.concatenate(outs, axis=-1).astype(o_ref.dtype)


def _win_attention(q, k, v, sink, c_len):
    b, l, qw = q.shape
    kw = k.shape[-1]
    blk = TIME_TILE
    kern = functools.partial(_win_attn_kernel, c_len=c_len, l_len=l, nqc=c_len // blk)
    return pl.pallas_call(
        kern,
        grid=(b, l // blk),
        in_specs=[
            pl.BlockSpec(memory_space=pltpu.SMEM),
            pl.BlockSpec((None, blk, qw), lambda bb, j: (bb, j, 0)),
            pl.BlockSpec((None, l, kw), lambda bb, j: (bb, 0, 0)),
            pl.BlockSpec((None, l, kw), lambda bb, j: (bb, 0, 0)),
        ],
        out_specs=pl.BlockSpec((None, blk, qw), lambda bb, j: (bb, j, 0)),
        out_shape=jax.ShapeDtypeStruct((b, l, qw), BF16),
        compiler_params=_cparams(("arbitrary", "arbitrary")),
        name="window_attention",
    )(sink, q, k, v)


def _out_even_kernel(x_ref, rec_ref, gate_ref, att_ref, wa_ref, wb_ref, g1_ref, *refs):
    route_in, o_ref, route_out = refs[:5], refs[5], refs[6:]
    lru = (rec_ref[0] + rec_ref[1]) * jax.nn.gelu(gate_ref[...].astype(F32))
    y = (jnp.dot(lru.astype(BF16), wa_ref[...], preferred_element_type=F32)
         + jnp.dot(att_ref[...], wb_ref[...], preferred_element_type=F32))
    x = x_ref[...] + g1_ref[...] * y
    o_ref[...] = x
    _route(x, *route_in, *route_out)


def _out_even(xc, rec2, gate, att, w_a, w_b, mods3, layer, nct, g_ffn, w_router, b_router):
    b, l, d = xc.shape
    tm = ROW_TILE
    w = gate.shape[-1]
    row = lambda bb, i: jnp.where(i < nct, SUBLANES, bb)
    r_in, r_args, r_out, r_shapes, r_scratch = _route_plumbing(b, l, d, g_ffn, mods3, layer, row, w_router, b_router)
    return pl.pallas_call(
        _out_even_kernel,
        grid=(b, l // tm),
        in_specs=[
            pl.BlockSpec((None, tm, d), lambda bb, i: (bb, i, 0)),
            pl.BlockSpec((2, tm, w), lambda bb, i: (0, i, bb)),
            pl.BlockSpec((None, tm, w), lambda bb, i: (bb, i, 0)),
            pl.BlockSpec((None, tm, att.shape[-1]), lambda bb, i: (bb, i, 0)),
            pl.BlockSpec(w_a.shape, lambda bb, i: (0, 0)),
            pl.BlockSpec(w_b.shape, lambda bb, i: (0, 0)),
            _mod_spec(d, layer, 2, row),
        ] + r_in,
        out_specs=[pl.BlockSpec((None, tm, d), lambda bb, i: (bb, i, 0))] + r_out,
        out_shape=[jax.ShapeDtypeStruct((b, l, d), F32)] + r_shapes,
        scratch_shapes=r_scratch,
        compiler_params=_cparams(("arbitrary", "arbitrary")),
        name="out_proj_even",
    )(xc, rec2, gate, att, w_a, w_b, mods3, *r_args)


def _diff_attn_kernel(lam_ref, g_ref, q_ref, k_ref, v_ref, o_ref, *, lam_init):
    lv = lam_ref[...]
    lam = (jnp.exp(jnp.sum(lv[0:1] * lv[1:2], axis=-1, keepdims=True))
           - jnp.exp(jnp.sum(lv[2:3] * lv[3:4], axis=-1, keepdims=True)) + lam_init)
    vw = 2 * DIFF_DH

    def logits(h, mp):
        lo = h * vw + mp * DIFF_DH
        return _nt_dot(q_ref[:, lo:lo + DIFF_DH], k_ref[:, lo:lo + DIFF_DH])

    def softmax_parts(s):
        e = jnp.exp2(s - jnp.max(s, axis=-1, keepdims=True))
        return e, 1.0 / jnp.sum(e, axis=-1, keepdims=True)

    heads = range(DIFF_HEADS)
    ls = [(logits(h, 0), logits(h, 1)) for h in heads]
    ws = []
    for l0, l1 in ls:
        e0, r0 = softmax_parts(l0)
        e1, r1 = softmax_parts(l1)
        ws.append((e0 * r0 - e1 * (lam * r1)).astype(BF16))
    for h, w in zip(heads, ws):
        lo = h * vw
        o = jnp.dot(w, v_ref[:, lo:lo + vw], preferred_element_type=F32)
        ms = jnp.mean(o * o, axis=-1, keepdims=True)
        o = o * lax.rsqrt(ms + NORM_EPS) * g_ref[...]
        o_ref[:, lo:lo + vw] = (o * (1.0 - lam_init)).astype(o_ref.dtype)


def _diff_attention(q, k, v, lam_vecs, subln_g, lam_init, c_len):
    b, l, w = q.shape
    tq = ROW_TILE
    s_len = l - c_len
    off = c_len // tq
    return pl.pallas_call(
        functools.partial(_diff_attn_kernel, lam_init=lam_init),
        grid=(b, s_len // tq),
        in_specs=[
            pl.BlockSpec(lam_vecs.shape, lambda bb, j: (0, 0)),
            pl.BlockSpec((1, 2 * DIFF_DH), lambda bb, j: (0, 0)),
            pl.BlockSpec((None, tq, w), lambda bb, j: (bb, j + off, 0)),
            pl.BlockSpec((None, l, w), lambda bb, j: (bb, 0, 0)),
            pl.BlockSpec((None, l, w), lambda bb, j: (bb, 0, 0)),
        ],
        out_specs=pl.BlockSpec((None, tq, w), lambda bb, j: (bb, j, 0)),
        out_shape=jax.ShapeDtypeStruct((b, s_len, w), BF16),
        compiler_params=_cparams(("arbitrary", "arbitrary")),
        name="diff_attention",
    )(lam_vecs, subln_g.reshape(1, -1), q, k, v)


def _ssd_kernel(x_ref, xp_ref, xn_ref, dt_ref, cw_ref, cb_ref, dtb_ref, alog_ref, dsk_ref, o_ref,
                ext_scr, st_scr, *, q, nct, nt):
    d = pl.program_id(0)
    g = pl.program_id(2)
    tile = _seq_tile(d, g, nct, nt)
    pv = jnp.where((tile == 0) | (tile == nct), 0.0, 1.0)
    nv = jnp.where((tile == nct - 1) | (tile == nt - 1), 0.0, 1.0)
    ext_scr[0:SUBLANES] = xp_ref[...] * pv
    ext_scr[SUBLANES:SUBLANES + q] = x_ref[...]
    ext_scr[SUBLANES + q:2 * SUBLANES + q] = xn_ref[...] * nv

    @pl.when(g == 0)
    def _():
        st_scr[...] = jnp.zeros_like(st_scr)

    u = cb_ref[...] + cw_ref[0] * ext_scr[SUBLANES - 1:SUBLANES - 1 + q, :]
    for j in range(1, CONV_W):
        u = u + cw_ref[j] * ext_scr[SUBLANES - 1 + j:SUBLANES - 1 + j + q, :]
    act = _silu(u)

    dtr = dt_ref[...]
    dtr = jnp.where(d == 0, dtr, pltpu.roll(dtr, LANES - SSD_HEADS, 1))
    dtv = _softplus(dtr + dtb_ref[...])
    head_lane = lax.broadcasted_iota(jnp.int32, (1, LANES), 1) < SSD_HEADS
    dta = dtv * jnp.where(head_lane, -jnp.exp(alog_ref[...]) * LOG2E, 0.0)
    ri = lax.broadcasted_iota(jnp.int32, (q, q), 0)
    ci = lax.broadcasted_iota(jnp.int32, (q, q), 1)
    keep = jnp.where(d == 0, ri - ci, ci - ri) >= 0
    keep_b = jnp.where(keep, 1.0, 0.0).astype(BF16)
    cum = jnp.zeros((q, LANES), F32)
    rest = dta
    for _ in range(3):
        part = rest.astype(BF16)
        cum = cum + jnp.dot(keep_b, part, preferred_element_type=F32)
        rest = rest - part.astype(F32)
    tot = jnp.sum(dta, axis=0, keepdims=True)
    cum_t = cum.T
    dt_t = dtv.T
    to_end = jnp.exp2(tot - cum) * dtv
    e_cum = jnp.exp2(cum)
    e_tot = jnp.exp2(tot)
    dskip = dsk_ref[...] * jnp.where(d == 0, 1.0, 0.0)

    hpg = SSD_HEADS // SSD_GROUPS
    for gi in range(SSD_GROUPS):
        b_g = act[:, SSD_INNER + gi * SSD_STATE:SSD_INNER + (gi + 1) * SSD_STATE]
        c_lo = SSD_INNER + SSD_GROUPS * SSD_STATE + gi * SSD_STATE
        c_g = act[:, c_lo:c_lo + SSD_STATE].astype(BF16)
        cb = _nt_dot(c_g, b_g.astype(BF16))
        b_gt = b_g.T.astype(BF16)
        for hh in range(hpg):
            h = gi * hpg + hh
            xs = act[:, h * SSD_HEAD_DIM:(h + 1) * SSD_HEAD_DIM]
            seg = cum[:, h:h + 1] - cum_t[h:h + 1, :]
            decay = jnp.exp2(jnp.where(keep, seg, NEG_BIG))
            w = (cb * decay * dt_t[h:h + 1, :]).astype(BF16)
            state = st_scr[h]
            y = jnp.dot(w, xs.astype(BF16), preferred_element_type=F32)
            y = y + jnp.dot(c_g, state.astype(BF16), preferred_element_type=F32) * e_cum[:, h:h + 1]
            y = y + dskip[:, h * SSD_HEAD_DIM:(h + 1) * SSD_HEAD_DIM] * xs
            o_ref[:, h * SSD_HEAD_DIM:(h + 1) * SSD_HEAD_DIM] = y
            s_new = jnp.dot(b_gt, (xs * to_end[:, h:h + 1]).astype(BF16), preferred_element_type=F32)
            st_scr[h] = e_tot[:, h:h + 1] * state + s_new


def _ssd(xbc, dt, conv_w, conv_b, dt_bias, a_log, d_skip, nct):
    b, l, cd = xbc.shape
    q = TIME_TILE
    nt = l // q
    tile = lambda d, bb, g: _seq_tile(d, g, nct, nt)
    r8 = q // SUBLANES
    pad = LANES - SSD_HEADS
    dtb = jnp.pad(dt_bias, ((0, 0), (0, pad))).reshape(2, 1, LANES)
    alog = jnp.pad(a_log, ((0, 0), (0, pad))).reshape(2, 1, LANES)
    dsk = jnp.repeat(d_skip, SSD_HEAD_DIM).reshape(1, SSD_INNER)
    return pl.pallas_call(
        functools.partial(_ssd_kernel, q=q, nct=nct, nt=nt),
        grid=(2, b, nt),
        in_specs=[
            pl.BlockSpec((None, q, cd), lambda d, bb, g: (bb, tile(d, bb, g), 0)),
            pl.BlockSpec((None, SUBLANES, cd),
                         lambda d, bb, g: (bb, jnp.maximum(tile(d, bb, g) * r8 - 1, 0), 0)),
            pl.BlockSpec((None, SUBLANES, cd),
                         lambda d, bb, g: (bb, jnp.minimum((tile(d, bb, g) + 1) * r8, l // SUBLANES - 1), 0)),
            pl.BlockSpec((None, q, LANES), lambda d, bb, g: (bb, tile(d, bb, g), 0)),
            pl.BlockSpec((CONV_W, 1, cd), lambda d, bb, g: (0, 0, 0)),
            pl.BlockSpec((1, cd), lambda d, bb, g: (0, 0)),
            pl.BlockSpec((None, 1, LANES), lambda d, bb, g: (d, 0, 0)),
            pl.BlockSpec((None, 1, LANES), lambda d, bb, g: (d, 0, 0)),
            pl.BlockSpec((1, SSD_INNER), lambda d, bb, g: (0, 0)),
        ],
        out_specs=pl.BlockSpec((None, None, q, SSD_INNER), lambda d, bb, g: (d, bb, tile(d, bb, g), 0)),
        out_shape=jax.ShapeDtypeStruct((2, b, l, SSD_INNER), F32),
        scratch_shapes=[
            pltpu.VMEM((q + 2 * SUBLANES, cd), F32),
            pltpu.VMEM((SSD_HEADS, SSD_STATE, SSD_HEAD_DIM), F32),
        ],
        compiler_params=_cparams(("arbitrary", "arbitrary", "arbitrary")),
        name="ssd_chunked",
    )(xbc, xbc, xbc, dt, conv_w.reshape(CONV_W, 1, cd), conv_b.reshape(1, cd), dtb, alog, dsk)


def _out_odd_kernel(x_ref, diff_ref, y_ref, z_ref, ng_ref, wa_ref, wb_ref, g1_ref, *refs):
    route_in, o_ref, route_out = refs[:5], refs[5], refs[6:]
    yz = (y_ref[0] + y_ref[1]) * _silu(z_ref[...].astype(F32))
    gs = SSD_INNER // SSD_GROUPS
    parts = []
    for gi in range(SSD_GROUPS):
        seg = yz[:, gi * gs:(gi + 1) * gs]
        ms = jnp.mean(seg * seg, axis=-1, keepdims=True)
        parts.append(seg * lax.rsqrt(ms + NORM_EPS) * ng_ref[:, gi * gs:(gi + 1) * gs])
    ssd = jnp.concatenate(parts, axis=-1).astype(BF16)
    y = (jnp.dot(diff_ref[...], wa_ref[...], preferred_element_type=F32)
         + jnp.dot(ssd, wb_ref[...], preferred_element_type=F32))
    x = x_ref[...] + g1_ref[...] * y
    o_ref[...] = x
    _route(x, *route_in, *route_out)


def _out_odd(xc, diff, y2, z, norm_g, w_a, w_b, mods3, layer, c_len, g_ffn, w_router, b_router):
    b, l, d = xc.shape
    s_len = l - c_len
    tm = ROW_TILE
    off = c_len // tm
    w = SSD_INNER
    row = lambda bb, i: bb
    r_in, r_args, r_out, r_shapes, r_scratch = _route_plumbing(b, s_len, d, g_ffn, mods3, layer, row, w_router,
                                                               b_router)
    return pl.pallas_call(
        _out_odd_kernel,
        grid=(b, s_len // tm),
        in_specs=[
            pl.BlockSpec((None, tm, d), lambda bb, i: (bb, i + off, 0)),
            pl.BlockSpec((None, tm, diff.shape[-1]), lambda bb, i: (bb, i, 0)),
            pl.BlockSpec((2, None, tm, w), lambda bb, i: (0, bb, i + off, 0)),
            pl.BlockSpec((None, tm, w), lambda bb, i: (bb, i + off, 0)),
            pl.BlockSpec((1, w), lambda bb, i: (0, 0)),
            pl.BlockSpec(w_a.shape, lambda bb, i: (0, 0)),
            pl.BlockSpec(w_b.shape, lambda bb, i: (0, 0)),
            _mod_spec(d, layer, 2, row),
        ] + r_in,
        out_specs=[pl.BlockSpec((None, tm, d), lambda bb, i: (bb, i, 0))] + r_out,
        out_shape=[jax.ShapeDtypeStruct((b, s_len, d), F32)] + r_shapes,
        scratch_shapes=r_scratch,
        compiler_params=_cparams(("arbitrary", "arbitrary")),
        name="out_proj_odd",
    )(xc, diff, y2, z, norm_g.reshape(1, w), w_a, w_b, mods3, *r_args)


def _route(x, g_ref, sh_ref, sc_ref, wr_ref, br_ref, h_ref, eid_ref, rnk_ref, gate_ref, cnt_ref, carry_scr):
    @pl.when((pl.program_id(0) == 0) & (pl.program_id(1) == 0))
    def _():
        carry_scr[...] = jnp.zeros_like(carry_scr)

    h = _norm_mod(x, g_ref[...], sh_ref[...], sc_ref[...])
    _store_chunk_rows(h_ref, _pack_bf16(h))
    tm = h.shape[0]
    per = N_EXPERTS // N_EXPERT_GROUPS
    logits = lax.dot_general(wr_ref[...], h, (((1,), (1,)), ((), ())),
                             preferred_element_type=F32, precision=HIGHEST)
    scores = jax.nn.sigmoid(logits)
    sel = scores + br_ref[...]
    sel3 = sel.reshape(N_EXPERT_GROUPS, per, tm)
    kio = lax.broadcasted_iota(jnp.int32, sel3.shape, 1)
    m1 = jnp.max(sel3, axis=1, keepdims=True)
    first = jnp.min(jnp.where(sel3 == m1, kio, per), axis=1, keepdims=True)
    m2 = jnp.max(jnp.where(kio == first, NEG_BIG, sel3), axis=1, keepdims=True)
    gs = m1 + m2
    gio = lax.broadcasted_iota(jnp.int32, gs.shape, 0)
    ahead = jnp.zeros(gs.shape, jnp.int32)
    for gp in range(N_EXPERT_GROUPS):
        other = gs[gp:gp + 1]
        ahead = ahead + jnp.where((other > gs) | ((other == gs) & (gp < gio)), 1, 0)
    grp_on = jnp.where(ahead < TOPK_GROUPS, 1.0, 0.0)
    selm = jnp.where(jnp.broadcast_to(grp_on, sel3.shape) > 0.5, sel3, NEG_BIG).reshape(N_EXPERTS, tm)
    eio = lax.broadcasted_iota(jnp.int32, selm.shape, 0)
    work = selm
    cf = jnp.zeros(selm.shape, F32)
    e_rows, s_rows = [], []
    for k in range(TOP_K):
        best = jnp.max(work, axis=0, keepdims=True)
        idx = jnp.min(jnp.where(work == best, eio, N_EXPERTS), axis=0, keepdims=True)
        hit = eio == idx
        cf = cf + jnp.where(hit, 1.0, 0.0)
        work = jnp.where(hit, NEG_BIG, work)
        e_rows.append(idx)
        s_rows.append(jnp.sum(jnp.where(hit, scores, 0.0), axis=0, keepdims=True))
    denom = s_rows[0]
    for s_k in s_rows[1:]:
        denom = denom + s_k
    g_rows = [s_k / denom * ROUTED_SCALE for s_k in s_rows]
    ti = lax.broadcasted_iota(jnp.int32, (tm, tm), 0)
    tj = lax.broadcasted_iota(jnp.int32, (tm, tm), 1)
    before = jnp.where(ti < tj, 1.0, 0.0).astype(BF16)
    in_expert = carry_scr[:, 0:1] + jnp.dot(cf.astype(BF16), before, preferred_element_type=F32)
    carry_scr[...] = carry_scr[...] + jnp.sum(cf, axis=1, keepdims=True)
    cnt_ref[...] = carry_scr[...]
    r_rows = [jnp.sum(jnp.where(eio == idx, in_expert, 0.0), axis=0, keepdims=True) for idx in e_rows]
    eid_ref[...] = jnp.concatenate(e_rows, axis=0)
    rnk_ref[...] = jnp.concatenate(r_rows, axis=0).astype(jnp.int32)
    padded = jnp.concatenate(g_rows + [jnp.zeros((LANES - TOP_K, tm), F32)], axis=0)
    gate_ref[...] = padded.T


def _route_plumbing(b, r, d, g, mods3, layer, row_fn, w_router, b_router):
    tm = ROW_TILE
    nt = r // tm
    w_router_t = w_router.T
    slot = pl.BlockSpec((TOP_K, tm), lambda bb, i: (0, bb * nt + i))
    slot_shape = jax.ShapeDtypeStruct((TOP_K, b * r), jnp.int32)
    in_specs = [
        pl.BlockSpec((1, d), lambda bb, i: (0, 0)),
        _mod_spec(d, layer, 3, row_fn),
        _mod_spec(d, layer, 4, row_fn),
        pl.BlockSpec(w_router_t.shape, lambda bb, i: (0, 0)),
        pl.BlockSpec((N_EXPERTS, 1), lambda bb, i: (0, 0)),
    ]
    args = [g.reshape(1, d), mods3, mods3, w_router_t, b_router.reshape(N_EXPERTS, 1)]
    out_specs = [
        pl.BlockSpec((tm * ROW_CHUNKS, LANES), lambda bb, i: (bb * nt + i, 0)),
        slot,
        slot,
        pl.BlockSpec((None, tm, LANES), lambda bb, i: (bb, i, 0)),
        pl.BlockSpec((N_EXPERTS, LANES), lambda bb, i: (0, 0)),
    ]
    out_shapes = [jax.ShapeDtypeStruct((b * r * ROW_CHUNKS, LANES), jnp.uint32), slot_shape, slot_shape,
                  jax.ShapeDtypeStruct((b, r, LANES), F32), jax.ShapeDtypeStruct((N_EXPERTS, LANES), F32)]
    scratch = [pltpu.VMEM((N_EXPERTS, LANES), F32)]
    return in_specs, args, out_specs, out_shapes, scratch


def _moe_plan(counts, n_rows):
    blk = EXPERT_BLK
    nb = n_rows // blk
    ends = jnp.cumsum(counts)
    starts = ends - counts
    count_le = lambda sorted_vals, q: jnp.sum(sorted_vals[None, :] <= q[:, None], axis=1, dtype=jnp.int32)
    first = jnp.arange(nb, dtype=jnp.int32) * blk
    e_lo = count_le(ends, first)
    e_hi = count_le(ends, first + (blk - 1))
    n_pair = e_hi - e_lo + 1
    p_end = jnp.cumsum(n_pair)
    p_start = p_end - n_pair
    i = jnp.arange(nb + N_EXPERTS - 1, dtype=jnp.int32)
    j = jnp.minimum(count_le(p_end, i), nb - 1)
    valid = i < p_end[-1]
    e = jnp.where(valid, e_lo[j] + i - p_start[j], e_hi[nb - 1]).astype(jnp.int32)
    bounds = jnp.concatenate([starts, ends[-1:]]).astype(jnp.int32)
    return j, e, valid.astype(jnp.int32), bounds


def _positions_kernel(starts_ref, eid_ref, rnk_ref, pos_ref):
    eid = eid_ref[...]
    pos = rnk_ref[...]
    for e in range(N_EXPERTS):
        pos = pos + jnp.where(eid == e, starts_ref[e], 0)
    pos_ref[...] = pos * ROW_CHUNKS


def _positions(eid, rnk, starts):
    full = pl.BlockSpec(eid.shape, lambda: (0, 0))
    return pl.pallas_call(
        _positions_kernel,
        in_specs=[pl.BlockSpec(memory_space=pltpu.SMEM), full, full],
        out_specs=full,
        out_shape=jax.ShapeDtypeStruct(eid.shape, jnp.int32),
        compiler_params=pltpu.CompilerParams(vmem_limit_bytes=VMEM_LIMIT),
        name="moe_positions",
    )(starts, eid, rnk)


def _token_row(ref, first):
    return ref.at[pl.ds(pl.multiple_of(first, ROW_CHUNKS), ROW_CHUNKS)]


def _dispatch_kernel(pos_ref, h_ref, xs_ref, sem):
    tm = h_ref.shape[0] // ROW_CHUNKS

    def issue(t, carry):
        src = _token_row(h_ref, t * ROW_CHUNKS)
        for k in range(TOP_K):
            pltpu.make_async_copy(src, _token_row(xs_ref, pos_ref[k, t]), sem).start(priority=k % 2)
        return carry

    lax.fori_loop(0, tm, issue, 0)
    done = pl.ds(0, tm * ROW_CHUNKS)
    for _ in range(TOP_K):
        pltpu.make_async_copy(h_ref.at[done], xs_ref.at[done], sem).wait()


def _dispatch(h2, pos):
    rows, w = h2.shape
    tm = math.gcd(DISPATCH_TILE, rows // ROW_CHUNKS)
    return pl.pallas_call(
        _dispatch_kernel,
        grid=(rows // (tm * ROW_CHUNKS),),
        in_specs=[
            pl.BlockSpec((TOP_K, tm), lambda i: (0, i), memory_space=pltpu.SMEM),
            pl.BlockSpec((tm * ROW_CHUNKS, w), lambda i: (i, 0)),
        ],
        out_specs=pl.BlockSpec(memory_space=pl.ANY),
        out_shape=jax.ShapeDtypeStruct((rows * TOP_K, w), h2.dtype),
        scratch_shapes=[pltpu.SemaphoreType.DMA],
        compiler_params=_cparams(("arbitrary",)),
        name="moe_dispatch",
    )(pos, h2)


def _grouped_kernel(pb_ref, pe_ref, pv_ref, bnd_ref, xs_ref, wg_ref, wu_ref, wd_ref, y_ref, wgb, wub, wdb):
    i = pl.program_id(0)
    prev = jnp.maximum(i - 1, 0)
    j = pb_ref[i]
    e = pe_ref[i]
    blk = xs_ref.shape[0] // ROW_CHUNKS

    @pl.when((i == 0) | (pb_ref[prev] != j))
    def _():
        y_ref[...] = jnp.zeros_like(y_ref)

    @pl.when((i == 0) | (pe_ref[prev] != e))
    def _():
        wgb[...] = wg_ref[...].astype(BF16)
        wub[...] = wu_ref[...].astype(BF16)
        wdb[...] = wd_ref[...].astype(BF16)

    @pl.when(pv_ref[i] == 1)
    def _():
        xw = _load_chunk_rows(xs_ref, blk)
        a = _packed_dot(xw, wgb)
        u = _packed_dot(xw, wub)
        yv = jnp.dot((_silu(a) * u).astype(BF16), wdb[...], preferred_element_type=F32)
        rows = j * blk + lax.broadcasted_iota(jnp.int32, (blk, 1), 0)
        own = (rows >= bnd_ref[e]) & (rows < bnd_ref[e + 1])
        yw = _pack_bf16(yv)
        for c in range(ROW_CHUNKS):
            sl = pl.ds(c, blk, stride=ROW_CHUNKS)
            y_ref[sl, :] = jnp.where(own, yw[:, c * LANES:(c + 1) * LANES], y_ref[sl, :])


def _grouped(pb, pe, pv, bounds, xs, wg, wu, wd, layer):
    p, half = xs.shape
    d = 2 * ROW_CHUNKS * LANES
    blk = EXPERT_BLK * ROW_CHUNKS
    grid_spec = pltpu.PrefetchScalarGridSpec(
        num_scalar_prefetch=4,
        grid=(pb.shape[0],),
        in_specs=[
            pl.BlockSpec((blk, half), lambda i, pb, pe, pv, bnd: (pb[i], 0)),
            pl.BlockSpec((None, None, d, D_EXPERT), lambda i, pb, pe, pv, bnd: (layer, pe[i], 0, 0)),
            pl.BlockSpec((None, None, d, D_EXPERT), lambda i, pb, pe, pv, bnd: (layer, pe[i], 0, 0)),
            pl.BlockSpec((None, None, D_EXPERT, d), lambda i, pb, pe, pv, bnd: (layer, pe[i], 0, 0)),
        ],
        out_specs=pl.BlockSpec((blk, half), lambda i, pb, pe, pv, bnd: (pb[i], 0)),
        scratch_shapes=[
            pltpu.VMEM((d, D_EXPERT), BF16),
            pltpu.VMEM((d, D_EXPERT), BF16),
            pltpu.VMEM((D_EXPERT, d), BF16),
        ],
    )
    return pl.pallas_call(
        _grouped_kernel,
        grid_spec=grid_spec,
        out_shape=jax.ShapeDtypeStruct((p, half), jnp.uint32),
        compiler_params=_cparams(("arbitrary",)),
        name="moe_grouped_experts",
    )(pb, pe, pv, bounds, xs, wg, wu, wd)


def _combine_kernel(*refs, final, c_len):
    pos_ref, y_ref, gate_ref, h_ref, x_ref, g2c_ref, g2l_ref, sg_ref, su_ref, sd_ref = refs[:10]
    o_ref, buf, sem = refs[-3:]
    tm = x_ref.shape[0]

    def issue(t, carry):
        for k in range(TOP_K):
            pltpu.make_async_copy(_token_row(y_ref, pos_ref[k, t]), _token_row(buf.at[k], t * ROW_CHUNKS),
                                  sem).start(priority=k % 2)
        return carry

    lax.fori_loop(0, tm, issue, 0)
    hw = _load_chunk_rows(h_ref, tm)
    a = _packed_dot(hw, sg_ref)
    u = _packed_dot(hw, su_ref)
    acc = jnp.dot((_silu(a) * u).astype(BF16), sd_ref[...], preferred_element_type=F32)
    done = pl.ds(0, tm * ROW_CHUNKS)
    for k in range(TOP_K):
        pltpu.make_async_copy(y_ref.at[done], buf.at[k, done], sem).wait()
    g = gate_ref[...]
    half = hw.shape[-1]
    acc_hi = acc[:, :half]
    acc_lo = acc[:, half:]
    for k in range(TOP_K):
        hi, lo = _unpack_bf16(_load_chunk_rows(buf.at[k], tm))
        acc_hi = acc_hi + g[:, k:k + 1] * hi
        acc_lo = acc_lo + g[:, k:k + 1] * lo
    is_ctx = pl.program_id(1) * tm + lax.broadcasted_iota(jnp.int32, (tm, 1), 0) < c_len
    g2 = jnp.where(is_ctx, g2c_ref[...], g2l_ref[...])
    x = x_ref[...] + g2 * jnp.concatenate([acc_hi, acc_lo], axis=-1)
    if final:
        gf_ref = refs[10]
        ms = jnp.mean(x * x, axis=-1, keepdims=True)
        x = x * lax.rsqrt(ms + NORM_EPS) * gf_ref[...]
    o_ref[...] = x


def _combine(pos, y, gates, h2, x, mods3, layer, c_len, tm, sg, su, sd, g_final=None):
    b, r, d = x.shape
    nt = r // tm
    tile = pl.BlockSpec((None, tm, d), lambda bb, i: (bb, i, 0))
    in_specs = [
        pl.BlockSpec((TOP_K, tm), lambda bb, i: (0, bb * nt + i), memory_space=pltpu.SMEM),
        pl.BlockSpec(memory_space=pl.ANY),
        pl.BlockSpec((None, tm, LANES), lambda bb, i: (bb, i, 0)),
        pl.BlockSpec((tm * ROW_CHUNKS, LANES), lambda bb, i: (bb * nt + i, 0)),
        tile,
        _mod_spec(d, layer, 5, lambda bb, i: SUBLANES),
        _mod_spec(d, layer, 5, lambda bb, i: bb),
        pl.BlockSpec(sg.shape, lambda bb, i: (0, 0)),
        pl.BlockSpec(su.shape, lambda bb, i: (0, 0)),
        pl.BlockSpec(sd.shape, lambda bb, i: (0, 0)),
    ]
    args = [pos, y, gates, h2, x, mods3, mods3, sg, su, sd]
    if g_final is not None:
        in_specs.append(pl.BlockSpec((1, d), lambda bb, i: (0, 0)))
        args.append(g_final.reshape(1, d))
    return pl.pallas_call(
        functools.partial(_combine_kernel, final=g_final is not None, c_len=c_len),
        grid=(b, nt),
        in_specs=in_specs,
        out_specs=tile,
        out_shape=jax.ShapeDtypeStruct((b, r, d), F32),
        scratch_shapes=[pltpu.VMEM((TOP_K, tm * ROW_CHUNKS, LANES), jnp.uint32), pltpu.SemaphoreType.DMA],
        compiler_params=_cparams(("arbitrary", "arbitrary")),
        name="moe_combine",
    )(*args)


def _moe(x, routed, mods3, layer, c_len, combine_tile, w_e_gate, w_e_up, w_e_down, ws_gate, ws_up, ws_down,
         g_final=None):
    b, r, d = x.shape
    h2, eid, rnk, gates, cnt = routed
    pb, pe, pv, bounds = _moe_plan(cnt[:, 0].astype(jnp.int32), b * r * TOP_K)
    pos = _positions(eid, rnk, bounds[:N_EXPERTS])
    xs = _dispatch(h2, pos)
    y = _grouped(pb, pe, pv, bounds, xs, w_e_gate, w_e_up, w_e_down, layer)
    return _combine(pos, y, gates, h2, x, mods3, layer, c_len, combine_tile,
                    ws_gate.astype(BF16), ws_up.astype(BF16), ws_down.astype(BF16), g_final)


def kernel(x, c, ctx, c_ctx, w_mod, b_mod, g_mix, g_ffn, g_final, ab_w_in, ab_w_out, ab_conv_w, ab_conv_b, ab_w_r, ab_b_r, ab_w_i, ab_b_i, ab_lam, ab_sink, cd_w_in, cd_w_out, cd_lam, cd_subln_g, cd_conv_w, cd_conv_b, cd_dt_bias, cd_a_log, cd_d_skip, cd_norm_g, w_router, b_router, w_e_gate, w_e_up, w_e_down, ws_gate, ws_up, ws_down):
    bsz, s_len, d = x.shape
    c_len = ctx.shape[1]
    depth = w_mod.shape[0]
    assert depth == 2 and bsz == SUBLANES, "kernels are specialised to depth 2 and batch 8"
    assert c_len % ROW_TILE == 0 and s_len % ROW_TILE == 0
    nct_row = c_len // ROW_TILE
    nct_time = c_len // TIME_TILE

    c_all = jnp.concatenate([c, c_ctx[None], jnp.zeros((MOD_ROWS - bsz - 1, d), F32)], axis=0)
    mods3 = _modulations(c_all, w_mod, b_mod).reshape(depth * MOD_ROWS, 1, N_MOD * d)
    rope_tabs = _rope_tables(c_len, s_len)
    xc = jnp.concatenate([ctx, x], axis=1)

    w_in = ab_w_in[0].astype(BF16)
    q_hi = LRU_WIDTH + WIN_HEADS * HEAD_DIM
    x_hi = q_hi + LRU_WIDTH
    k_hi = x_hi + WIN_KV_HEADS * HEAD_DIM
    gate, q, xa, k, v = _project(xc, g_mix[0], mods3, 0, nct_row, rope_tabs, [
        (w_in[:, :LRU_WIDTH], None, BF16, False),
        (w_in[:, LRU_WIDTH:q_hi], HEAD_DIM ** -0.5 * LOG2E, BF16, False),
        (w_in[:, q_hi:x_hi], None, F32, True),
        (w_in[:, x_hi:k_hi], 1.0, BF16, False),
        (w_in[:, k_hi:], None, BF16, False),
    ])
    l_len = c_len + s_len
    w_gates = jnp.stack([jnp.concatenate([_block_diag(ab_w_r[0, dd]), _block_diag(ab_w_i[0, dd])], axis=1)
                         for dd in range(2)]).astype(BF16)
    b_gates = jnp.concatenate([ab_b_r[0], ab_b_i[0]], axis=-1).reshape(2, 1, 2 * LRU_WIDTH)
    rec = _rglru(xa.reshape(l_len, bsz, LRU_WIDTH), ab_conv_w[0], ab_conv_b[0], w_gates, b_gates,
                 ab_lam[0].reshape(2, 1, LRU_WIDTH), nct_time)
    att = _win_attention(q, k, v, ab_sink[0], c_len)
    w_out = ab_w_out[0].astype(BF16)
    xc, *routed = _out_even(xc, rec.reshape(2, l_len, bsz * LRU_WIDTH), gate, att, w_out[:LRU_WIDTH],
                            w_out[LRU_WIDTH:], mods3, 0, nct_row, g_ffn[0], w_router[0], b_router[0])
    xc = _moe(xc, routed, mods3, 0, c_len, math.gcd(MIXED_COMBINE_TILE, l_len), w_e_gate, w_e_up, w_e_down,
              ws_gate[0], ws_up[0], ws_down[0])

    w_in = cd_w_in[0].astype(BF16)
    qk = DIFF_HEADS * 2 * DIFF_DH
    z_hi = qk + SSD_INNER
    k_hi = z_hi + qk
    v_hi = k_hi + qk
    x_hi = v_hi + SSD_CONV_DIM
    w_dt = jnp.pad(w_in[:, x_hi:], ((0, 0), (0, LANES - 2 * SSD_HEADS)))
    q, z, k, v, xbc, dt = _project(xc, g_mix[1], mods3, 1, nct_row, rope_tabs, [
        (w_in[:, :qk], DIFF_DH ** -0.5 * LOG2E, BF16, False),
        (w_in[:, qk:z_hi], None, BF16, False),
        (w_in[:, z_hi:k_hi], 1.0, BF16, False),
        (w_in[:, k_hi:v_hi], None, BF16, False),
        (w_in[:, v_hi:x_hi], None, F32, False),
        (w_dt, None, F32, False),
    ])
    lam_init = 0.8 - 0.6 * math.exp(-0.3 * 1)
    diff = _diff_attention(q, k, v, cd_lam[0], cd_subln_g[0], lam_init, c_len)
    y2 = _ssd(xbc, dt, cd_conv_w[0], cd_conv_b[0], cd_dt_bias[0], cd_a_log[0], cd_d_skip[0], nct_time)
    w_out = cd_w_out[0].astype(BF16)
    xl, *routed = _out_odd(xc, diff, y2, z, cd_norm_g[0], w_out[:qk], w_out[qk:], mods3, 1, c_len,
                           g_ffn[1], w_router[1], b_router[1])
    return _moe(xl, routed, mods3, 1, 0, math.gcd(LATENT_COMBINE_TILE, s_len), w_e_gate, w_e_up, w_e_down,
                ws_gate[1], ws_up[1], ws_down[1], g_final=g_final)
```

```python
import functools
import math

import jax
import jax.numpy as jnp
from jax import lax
from jax.experimental import pallas as pl
from jax.experimental.pallas import tpu as pltpu

F32 = jnp.float32
BF16 = jnp.bfloat16
HIGHEST = lax.Precision.HIGHEST

GRID_W = 64
N_MOD = 6
NORM_EPS = 1e-6
ROPE_BASE = 10000.0
CONV_W = 4

LRU_WIDTH = 512
LRU_BLOCKS = 8
LRU_C = 8.0

HEAD_DIM = 64
WIN_HEADS = 8
WIN_KV_HEADS = 2
WINDOW = 128

DIFF_HEADS = 4
DIFF_DH = 64

SSD_HEADS = 8
SSD_HEAD_DIM = 64
SSD_INNER = SSD_HEADS * SSD_HEAD_DIM
SSD_GROUPS = 2
SSD_STATE = 128
SSD_CONV_DIM = SSD_INNER + 2 * SSD_GROUPS * SSD_STATE

N_EXPERTS = 64
N_EXPERT_GROUPS = 8
TOPK_GROUPS = 4
TOP_K = 8
D_EXPERT = 256
ROUTED_SCALE = 2.5

LANES = 128
SUBLANES = 8
MOD_ROWS = 16
TIME_TILE = 128
ROW_TILE = 256
DISPATCH_TILE = 2048
LATENT_COMBINE_TILE = 512
MIXED_COMBINE_TILE = 768
EXPERT_BLK = 1024
ROW_CHUNKS = 4
VMEM_LIMIT = 48 * 1024 * 1024
NEG_BIG = -1e30
LOG2E = math.log2(math.e)


def _cparams(sem):
    return pltpu.CompilerParams(dimension_semantics=sem, vmem_limit_bytes=VMEM_LIMIT)


def _nt_dot(a, b):
    return lax.dot_general(a, b, (((1,), (1,)), ((), ())), preferred_element_type=F32)


def _softplus(x):
    return jnp.maximum(x, 0.0) + jnp.log1p(jnp.exp(-jnp.abs(x)))


def _sigmoid(x):
    return 0.5 * jnp.tanh(0.5 * x) + 0.5


def _silu(x):
    return x * _sigmoid(x)


def _pack_bf16(x):
    half = x.shape[-1] // 2
    bits = pltpu.bitcast(x.astype(BF16).astype(F32), jnp.uint32)
    return bits[:, :half] | (bits[:, half:] >> 16)


def _unpack_bf16(w):
    hi = pltpu.bitcast(w & jnp.uint32(0xFFFF0000), F32)
    lo = pltpu.bitcast(w << 16, F32)
    return hi, lo


def _store_chunk_rows(ref, w):
    n = w.shape[0]
    for j in range(ROW_CHUNKS):
        ref[pl.ds(j, n, stride=ROW_CHUNKS), :] = w[:, j * LANES:(j + 1) * LANES]


def _load_chunk_rows(ref, n):
    return jnp.concatenate([ref[pl.ds(j, n, stride=ROW_CHUNKS), :] for j in range(ROW_CHUNKS)], axis=1)


def _packed_dot(w, weight_ref):
    half = w.shape[-1]
    hi, lo = _unpack_bf16(w)
    return (jnp.dot(hi.astype(BF16), weight_ref[:half, :], preferred_element_type=F32)
            + jnp.dot(lo.astype(BF16), weight_ref[half:, :], preferred_element_type=F32))


def _mod_kernel(c_ref, w_ref, b_ref, o_ref):
    c = c_ref[...]
    s = _silu(c)
    o_ref[...] = jnp.dot(s, w_ref[...], preferred_element_type=F32, precision=HIGHEST) + b_ref[...]


def _modulations(c_all, w_mod, b_mod):
    depth, d, _ = w_mod.shape
    return pl.pallas_call(
        _mod_kernel,
        grid=(depth, N_MOD),
        in_specs=[
            pl.BlockSpec((MOD_ROWS, d), lambda l, k: (0, 0)),
            pl.BlockSpec((None, d, d), lambda l, k: (l, 0, k)),
            pl.BlockSpec((None, 1, d), lambda l, k: (l, 0, k)),
        ],
        out_specs=pl.BlockSpec((None, MOD_ROWS, d), lambda l, k: (l, 0, k)),
        out_shape=jax.ShapeDtypeStruct((depth, MOD_ROWS, N_MOD * d), F32),
        compiler_params=_cparams(("arbitrary", "arbitrary")),
        name="adaln_modulation",
    )(c_all, w_mod, b_mod.reshape(depth, 1, N_MOD * d))


def _mod_spec(d, layer, chunk, row_fn):
    return pl.BlockSpec((None, 1, d), lambda b, i: (layer * MOD_ROWS + row_fn(b, i), 0, chunk))


def _norm_mod(x, g, sh, sc):
    ms = jnp.mean(x * x, axis=-1, keepdims=True)
    return (x * lax.rsqrt(ms + NORM_EPS) * g) * (1.0 + sc) + sh


def _rope(y, cos, sa, sb):
    n = y.shape[-1]
    half = HEAD_DIM // 2
    return y * cos + pltpu.roll(y, n - half, 1) * sa + pltpu.roll(y, half, 1) * sb


def _proj_kernel(*refs, ropes):
    n = len(ropes)
    x_ref, g_ref, sh_ref, sc_ref, cos_ref, sa_ref, sb_ref = refs[:7]
    w_refs = refs[7:7 + n]
    o_refs = refs[7 + n:]
    h = _norm_mod(x_ref[...], g_ref[...], sh_ref[...], sc_ref[...]).astype(BF16)
    for w_ref, o_ref, rope in zip(w_refs, o_refs, ropes):
        y = jnp.dot(h, w_ref[...], preferred_element_type=F32)
        if rope is not None:
            w = y.shape[-1]
            y = _rope(y, cos_ref[:, :w], sa_ref[:, :w], sb_ref[:, :w])
            if rope != 1.0:
                y = y * rope
        o_ref[...] = y.astype(o_ref.dtype)


def _project(xc, g, mods3, layer, nct, rope_tabs, groups):
    b, l, d = xc.shape
    tm = ROW_TILE
    mod_row = lambda i, bb: layer * MOD_ROWS + jnp.where(i < nct, SUBLANES, bb)
    rw = rope_tabs[0].shape[-1]
    in_specs = [
        pl.BlockSpec((None, tm, d), lambda i, bb: (bb, i, 0)),
        pl.BlockSpec((1, d), lambda i, bb: (0, 0)),
        pl.BlockSpec((None, 1, d), lambda i, bb: (mod_row(i, bb), 0, 0)),
        pl.BlockSpec((None, 1, d), lambda i, bb: (mod_row(i, bb), 0, 1)),
    ] + [pl.BlockSpec((tm, rw), lambda i, bb: (i, 0))] * 3
    out_specs, out_shapes = [], []
    for w, _, dt, time_major in groups:
        n = w.shape[1]
        in_specs.append(pl.BlockSpec((d, n), lambda i, bb: (0, 0)))
        if time_major:
            out_specs.append(pl.BlockSpec((tm, n), lambda i, bb: (i, bb)))
            out_shapes.append(jax.ShapeDtypeStruct((l, b * n), dt))
        else:
            out_specs.append(pl.BlockSpec((None, tm, n), lambda i, bb: (bb, i, 0)))
            out_shapes.append(jax.ShapeDtypeStruct((b, l, n), dt))
    return pl.pallas_call(
        functools.partial(_proj_kernel, ropes=tuple(gp[1] for gp in groups)),
        grid=(l // tm, b),
        in_specs=in_specs,
        out_specs=out_specs,
        out_shape=out_shapes,
        compiler_params=_cparams(("arbitrary", "arbitrary")),
        name="norm_mod_project",
    )(xc, g.reshape(1, d), mods3, mods3, *rope_tabs, *[gp[0] for gp in groups])


def _rope_tables(c_len, s_len):
    rows = s_len // GRID_W
    row = jnp.repeat(jnp.arange(rows), GRID_W).astype(F32)
    col = jnp.tile(jnp.arange(GRID_W), rows).astype(F32)
    n = HEAD_DIM // 4
    inv = ROPE_BASE ** (-jnp.arange(n, dtype=F32) / n)
    ang = jnp.concatenate([row[:, None] * inv, col[:, None] * inv], axis=-1)
    cos, sin = jnp.cos(ang), jnp.sin(ang)
    zero = jnp.zeros_like(sin)
    reps = WIN_HEADS
    cos_t = jnp.tile(jnp.concatenate([cos, cos], axis=-1), (1, reps))
    sa_t = jnp.tile(jnp.concatenate([-sin, zero], axis=-1), (1, reps))
    sb_t = jnp.tile(jnp.concatenate([zero, sin], axis=-1), (1, reps))
    w = cos_t.shape[-1]
    pad1 = jnp.ones((c_len, w), F32)
    pad0 = jnp.zeros((c_len, w), F32)
    return (jnp.concatenate([pad1, cos_t], 0), jnp.concatenate([pad0, sa_t], 0),
            jnp.concatenate([pad0, sb_t], 0))


def _seq_tile(d, g, nct, nt):
    rev = jnp.where(g < nct, nct - 1 - g, nt - 1 - (g - nct))
    return jnp.where(d == 0, g, rev)


def _rglru_kernel(x_ref, xp_ref, xn_ref, cw_ref, cb_ref, w_ref, bias_ref, lam_ref, o_ref,
                  ext_scr, a_scr, b_scr, h_scr, *, ts, nct, nt, sub):
    d = pl.program_id(0)
    g = pl.program_id(1)
    tile = _seq_tile(d, g, nct, nt)
    bsz, width = h_scr.shape
    pv = jnp.where((tile == 0) | (tile == nct), 0.0, 1.0)
    nv = jnp.where((tile == nct - 1) | (tile == nt - 1), 0.0, 1.0)
    ext_scr[0:1] = xp_ref[...] * pv
    ext_scr[1:ts + 1] = x_ref[...]
    ext_scr[ts + 1:ts + 3] = xn_ref[...] * nv

    @pl.when(g == 0)
    def _():
        h_scr[...] = jnp.zeros_like(h_scr)

    neg_sp = -LRU_C * _softplus(-lam_ref[...])

    def prep(c, carry):
        r0 = pl.multiple_of(c * sub, sub)
        e = ext_scr[pl.ds(r0, sub + CONV_W - 1)]
        u = cb_ref[...] + cw_ref[0] * e[0:sub]
        for j in range(1, CONV_W):
            u = u + cw_ref[j] * e[j:j + sub]
        u2 = u.reshape(sub * bsz, width)
        gts = jnp.dot(u2.astype(BF16), w_ref[...], preferred_element_type=F32) + bias_ref[...]
        r = _sigmoid(gts[:, :width])
        ig = _sigmoid(gts[:, width:])
        log_a = neg_sp * r
        a = jnp.exp(log_a)
        mult = jnp.sqrt(1.0 - a * a)
        a_scr[pl.ds(r0, sub)] = a.reshape(sub, bsz, width)
        b_scr[pl.ds(r0, sub)] = (mult * ig * u2).reshape(sub, bsz, width)
        return carry

    lax.fori_loop(0, ts // sub, prep, 0)

    def step(t, h):
        tt = jnp.where(d == 0, t, ts - 1 - t)
        h = a_scr[tt] * h + b_scr[tt]
        o_ref[tt] = h
        return h

    h_scr[...] = lax.fori_loop(0, ts, step, h_scr[...], unroll=8)


def _rglru(xa_tm, conv_w, conv_b, w_gates, b_gates, lam, nct):
    l, bsz, width = xa_tm.shape
    ts = TIME_TILE
    nt = l // ts
    tile = lambda d, g: _seq_tile(d, g, nct, nt)
    kern = functools.partial(_rglru_kernel, ts=ts, nct=nct, nt=nt, sub=16)
    return pl.pallas_call(
        kern,
        grid=(2, nt),
        in_specs=[
            pl.BlockSpec((ts, bsz, width), lambda d, g: (tile(d, g), 0, 0)),
            pl.BlockSpec((1, bsz, width), lambda d, g: (jnp.maximum(tile(d, g) * ts - 1, 0), 0, 0)),
            pl.BlockSpec((2, bsz, width),
                         lambda d, g: (jnp.minimum((tile(d, g) + 1) * (ts // 2), l // 2 - 1), 0, 0)),
            pl.BlockSpec((CONV_W, 1, width), lambda d, g: (0, 0, 0)),
            pl.BlockSpec((1, width), lambda d, g: (0, 0)),
            pl.BlockSpec((None, width, 2 * width), lambda d, g: (d, 0, 0)),
            pl.BlockSpec((None, 1, 2 * width), lambda d, g: (d, 0, 0)),
            pl.BlockSpec((None, 1, width), lambda d, g: (d, 0, 0)),
        ],
        out_specs=pl.BlockSpec((None, ts, bsz, width), lambda d, g: (d, tile(d, g), 0, 0)),
        out_shape=jax.ShapeDtypeStruct((2, l, bsz, width), F32),
        scratch_shapes=[
            pltpu.VMEM((ts + CONV_W - 1, bsz, width), F32),
            pltpu.VMEM((ts, bsz, width), F32),
            pltpu.VMEM((ts, bsz, width), F32),
            pltpu.VMEM((bsz, width), F32),
        ],
        compiler_params=_cparams(("arbitrary", "arbitrary")),
        name="rglru_scan",
    )(xa_tm, xa_tm, xa_tm, conv_w.reshape(CONV_W, 1, width), conv_b.reshape(1, width),
      w_gates, b_gates, lam)


def _block_diag(w):
    nb, c, dd = w.shape
    eye = jnp.eye(nb, dtype=w.dtype)
    return (eye[:, None, :, None] * w[:, :, None, :]).reshape(nb * c, nb * dd)


def _win_attn_kernel(sink_ref, q_ref, k_ref, v_ref, o_ref, *, c_len, l_len, nqc):
    j = pl.program_id(1)
    blk = q_ref.shape[0]
    grp = WIN_HEADS // WIN_KV_HEADS
    band = 3 * blk
    heads = [(h, slice(h * HEAD_DIM, (h + 1) * HEAD_DIM),
              slice((h // grp) * HEAD_DIM, (h // grp + 1) * HEAD_DIM)) for h in range(WIN_HEADS)]

    @pl.when(j < nqc)
    def _():
        logits = [_nt_dot(q_ref[:, hsl], k_ref[0:c_len, ksl]) for _, hsl, ksl in heads]
        probs, dens = [], []
        for (h, _, _), s in zip(heads, logits):
            sink = sink_ref[h] * LOG2E
            m = jnp.maximum(jnp.max(s, axis=-1, keepdims=True), sink)
            p = jnp.exp2(s - m)
            dens.append(jnp.sum(p, axis=-1, keepdims=True) + jnp.exp2(sink - m))
            probs.append(p.astype(BF16))
        outs = [jnp.dot(p, v_ref[0:c_len, ksl], preferred_element_type=F32) / den
                for (_, _, ksl), p, den in zip(heads, probs, dens)]
        o_ref[...] = jnp.concatenate(outs, axis=-1).astype(o_ref.dtype)

    @pl.when(j >= nqc)
    def _():
        jb = j - nqc
        start = jnp.clip(c_len + (jb - 1) * blk, c_len - blk, l_len - band)
        start = pl.multiple_of(start, blk)
        qpos = jb * blk + lax.broadcasted_iota(jnp.int32, (blk, band), 0)
        kpos = start - c_len + lax.broadcasted_iota(jnp.int32, (blk, band), 1)
        valid = (jnp.abs(qpos - kpos) <= WINDOW) & (kpos >= 0)
        lc = [_nt_dot(q_ref[:, hsl], k_ref[0:c_len, ksl]) for _, hsl, ksl in heads]
        lb = [jnp.where(valid, _nt_dot(q_ref[:, hsl], k_ref[pl.ds(start, band), ksl]), NEG_BIG)
              for _, hsl, ksl in heads]
        pcs, pbs, dens = [], [], []
        for (h, _, _), sc, sb in zip(heads, lc, lb):
            sink = sink_ref[h] * LOG2E
            m = jnp.maximum(jnp.maximum(jnp.max(sc, axis=-1, keepdims=True),
                                        jnp.max(sb, axis=-1, keepdims=True)), sink)
            pc = jnp.exp2(sc - m)
            pb = jnp.exp2(sb - m)
            dens.append(jnp.sum(pc, axis=-1, keepdims=True) + jnp.sum(pb, axis=-1, keepdims=True)
                        + jnp.exp2(sink - m))
            pcs.append(pc.astype(BF16))
            pbs.append(pb.astype(BF16))
        outs = [(jnp.dot(pc, v_ref[0:c_len, ksl], preferred_element_type=F32)
                 + jnp.dot(pb, v_ref[pl.ds(start, band), ksl], preferred_element_type=F32)) / den
                for (_, _, ksl), pc, pb, den in zip(heads, pcs, pbs, dens)]
        o_ref[...] = jnp.concatenate(outs, axis=-1).astype(o_ref.dtype)


def _win_attention(q, k, v, sink, c_len):
    b, l, qw = q.shape
    kw = k.shape[-1]
    blk = TIME_TILE
    kern = functools.partial(_win_attn_kernel, c_len=c_len, l_len=l, nqc=c_len // blk)
    return pl.pallas_call(
        kern,
        grid=(b, l // blk),
        in_specs=[
            pl.BlockSpec(memory_space=pltpu.SMEM),
            pl.BlockSpec((None, blk, qw), lambda bb, j: (bb, j, 0)),
            pl.BlockSpec((None, l, kw), lambda bb, j: (bb, 0, 0)),
            pl.BlockSpec((None, l, kw), lambda bb, j: (bb, 0, 0)),
        ],
        out_specs=pl.BlockSpec((None, blk, qw), lambda bb, j: (bb, j, 0)),
        out_shape=jax.ShapeDtypeStruct((b, l, qw), BF16),
        compiler_params=_cparams(("arbitrary", "arbitrary")),
        name="window_attention",
    )(sink, q, k, v)


def _out_even_kernel(x_ref, rec_ref, gate_ref, att_ref, wa_ref, wb_ref, g1_ref, *refs):
    route_in, o_ref, route_out = refs[:5], refs[5], refs[6:]
    lru = (rec_ref[0] + rec_ref[1]) * jax.nn.gelu(gate_ref[...].astype(F32))
    y = (jnp.dot(lru.astype(BF16), wa_ref[...], preferred_element_type=F32)
         + jnp.dot(att_ref[...], wb_ref[...], preferred_element_type=F32))
    x = x_ref[...] + g1_ref[...] * y
    o_ref[...] = x
    _route(x, *route_in, *route_out)


def _out_even(xc, rec2, gate, att, w_a, w_b, mods3, layer, nct, g_ffn, w_router, b_router):
    b, l, d = xc.shape
    tm = ROW_TILE
    w = gate.shape[-1]
    row = lambda bb, i: jnp.where(i < nct, SUBLANES, bb)
    r_in, r_args, r_out, r_shapes, r_scratch = _route_plumbing(b, l, d, g_ffn, mods3, layer, row, w_router, b_router)
    return pl.pallas_call(
        _out_even_kernel,
        grid=(b, l // tm),
        in_specs=[
            pl.BlockSpec((None, tm, d), lambda bb, i: (bb, i, 0)),
            pl.BlockSpec((2, tm, w), lambda bb, i: (0, i, bb)),
            pl.BlockSpec((None, tm, w), lambda bb, i: (bb, i, 0)),
            pl.BlockSpec((None, tm, att.shape[-1]), lambda bb, i: (bb, i, 0)),
            pl.BlockSpec(w_a.shape, lambda bb, i: (0, 0)),
            pl.BlockSpec(w_b.shape, lambda bb, i: (0, 0)),
            _mod_spec(d, layer, 2, row),
        ] + r_in,
        out_specs=[pl.BlockSpec((None, tm, d), lambda bb, i: (bb, i, 0))] + r_out,
        out_shape=[jax.ShapeDtypeStruct((b, l, d), F32)] + r_shapes,
        scratch_shapes=r_scratch,
        compiler_params=_cparams(("arbitrary", "arbitrary")),
        name="out_proj_even",
    )(xc, rec2, gate, att, w_a, w_b, mods3, *r_args)


def _diff_attn_kernel(lam_ref, g_ref, q_ref, k_ref, v_ref, o_ref, *, lam_init):
    lv = lam_ref[...]
    lam = (jnp.exp(jnp.sum(lv[0:1] * lv[1:2], axis=-1, keepdims=True))
           - jnp.exp(jnp.sum(lv[2:3] * lv[3:4], axis=-1, keepdims=True)) + lam_init)
    vw = 2 * DIFF_DH

    def logits(h, mp):
        lo = h * vw + mp * DIFF_DH
        return _nt_dot(q_ref[:, lo:lo + DIFF_DH], k_ref[:, lo:lo + DIFF_DH])

    def softmax_parts(s):
        e = jnp.exp2(s - jnp.max(s, axis=-1, keepdims=True))
        return e, 1.0 / jnp.sum(e, axis=-1, keepdims=True)

    heads = range(DIFF_HEADS)
    ls = [(logits(h, 0), logits(h, 1)) for h in heads]
    ws = []
    for l0, l1 in ls:
        e0, r0 = softmax_parts(l0)
        e1, r1 = softmax_parts(l1)
        ws.append(((e0 - e1 * (lam * r1 / r0)).astype(BF16), r0))
    for h, (w, r0) in zip(heads, ws):
        lo = h * vw
        o = jnp.dot(w, v_ref[:, lo:lo + vw], preferred_element_type=F32) * r0
        ms = jnp.mean(o * o, axis=-1, keepdims=True)
        o = o * lax.rsqrt(ms + NORM_EPS) * g_ref[...]
        o_ref[:, lo:lo + vw] = (o * (1.0 - lam_init)).astype(o_ref.dtype)


def _diff_attention(q, k, v, lam_vecs, subln_g, lam_init, c_len):
    b, l, w = q.shape
    tq = ROW_TILE
    s_len = l - c_len
    off = c_len // tq
    return pl.pallas_call(
        functools.partial(_diff_attn_kernel, lam_init=lam_init),
        grid=(b, s_len // tq),
        in_specs=[
            pl.BlockSpec(lam_vecs.shape, lambda bb, j: (0, 0)),
            pl.BlockSpec((1, 2 * DIFF_DH), lambda bb, j: (0, 0)),
            pl.BlockSpec((None, tq, w), lambda bb, j: (bb, j + off, 0)),
            pl.BlockSpec((None, l, w), lambda bb, j: (bb, 0, 0)),
            pl.BlockSpec((None, l, w), lambda bb, j: (bb, 0, 0)),
        ],
        out_specs=pl.BlockSpec((None, tq, w), lambda bb, j: (bb, j, 0)),
        out_shape=jax.ShapeDtypeStruct((b, s_len, w), BF16),
        compiler_params=_cparams(("arbitrary", "arbitrary")),
        name="diff_attention",
    )(lam_vecs, subln_g.reshape(1, -1), q, k, v)


def _ssd_kernel(x_ref, xp_ref, xn_ref, dt_ref, cw_ref, cb_ref, dtb_ref, alog_ref, dsk_ref, o_ref,
                ext_scr, st_scr, *, q, nct, nt):
    d = pl.program_id(0)
    g = pl.program_id(2)
    tile = _seq_tile(d, g, nct, nt)
    pv = jnp.where((tile == 0) | (tile == nct), 0.0, 1.0)
    nv = jnp.where((tile == nct - 1) | (tile == nt - 1), 0.0, 1.0)
    ext_scr[0:SUBLANES] = xp_ref[...] * pv
    ext_scr[SUBLANES:SUBLANES + q] = x_ref[...]
    ext_scr[SUBLANES + q:2 * SUBLANES + q] = xn_ref[...] * nv

    @pl.when(g == 0)
    def _():
        st_scr[...] = jnp.zeros_like(st_scr)

    u = cb_ref[...] + cw_ref[0] * ext_scr[SUBLANES - 1:SUBLANES - 1 + q, :]
    for j in range(1, CONV_W):
        u = u + cw_ref[j] * ext_scr[SUBLANES - 1 + j:SUBLANES - 1 + j + q, :]
    act = _silu(u)

    dtr = dt_ref[...]
    dtr = jnp.where(d == 0, dtr, pltpu.roll(dtr, LANES - SSD_HEADS, 1))
    dtv = _softplus(dtr + dtb_ref[...])
    head_lane = lax.broadcasted_iota(jnp.int32, (1, LANES), 1) < SSD_HEADS
    dta = dtv * jnp.where(head_lane, -jnp.exp(alog_ref[...]) * LOG2E, 0.0)
    ri = lax.broadcasted_iota(jnp.int32, (q, q), 0)
    ci = lax.broadcasted_iota(jnp.int32, (q, q), 1)
    keep = jnp.where(d == 0, ri - ci, ci - ri) >= 0
    keep_b = jnp.where(keep, 1.0, 0.0).astype(BF16)
    cum = jnp.zeros((q, LANES), F32)
    rest = dta
    for _ in range(3):
        part = rest.astype(BF16)
        cum = cum + jnp.dot(keep_b, part, preferred_element_type=F32)
        rest = rest - part.astype(F32)
    tot = jnp.sum(dta, axis=0, keepdims=True)
    cum_t = cum.T
    dt_t = dtv.T
    to_end = jnp.exp2(tot - cum) * dtv
    e_cum = jnp.exp2(cum)
    e_tot = jnp.exp2(tot)
    dskip = dsk_ref[...] * jnp.where(d == 0, 1.0, 0.0)

    hpg = SSD_HEADS // SSD_GROUPS
    for gi in range(SSD_GROUPS):
        b_g = act[:, SSD_INNER + gi * SSD_STATE:SSD_INNER + (gi + 1) * SSD_STATE]
        c_lo = SSD_INNER + SSD_GROUPS * SSD_STATE + gi * SSD_STATE
        c_g = act[:, c_lo:c_lo + SSD_STATE].astype(BF16)
        cb = _nt_dot(c_g, b_g.astype(BF16))
        b_gt = b_g.T.astype(BF16)
        for hh in range(hpg):
            h = gi * hpg + hh
            xs = act[:, h * SSD_HEAD_DIM:(h + 1) * SSD_HEAD_DIM]
            seg = cum[:, h:h + 1] - cum_t[h:h + 1, :]
            decay = jnp.exp2(jnp.where(keep, seg, NEG_BIG))
            w = (cb * decay * dt_t[h:h + 1, :]).astype(BF16)
            state = st_scr[h]
            y = jnp.dot(w, xs.astype(BF16), preferred_element_type=F32)
            y = y + jnp.dot(c_g, state.astype(BF16), preferred_element_type=F32) * e_cum[:, h:h + 1]
            y = y + dskip[:, h * SSD_HEAD_DIM:(h + 1) * SSD_HEAD_DIM] * xs
            o_ref[:, h * SSD_HEAD_DIM:(h + 1) * SSD_HEAD_DIM] = y
            s_new = jnp.dot(b_gt, (xs * to_end[:, h:h + 1]).astype(BF16), preferred_element_type=F32)
            st_scr[h] = e_tot[:, h:h + 1] * state + s_new


def _ssd(xbc, dt, conv_w, conv_b, dt_bias, a_log, d_skip, nct):
    b, l, cd = xbc.shape
    q = TIME_TILE
    nt = l // q
    tile = lambda d, bb, g: _seq_tile(d, g, nct, nt)
    r8 = q // SUBLANES
    pad = LANES - SSD_HEADS
    dtb = jnp.pad(dt_bias, ((0, 0), (0, pad))).reshape(2, 1, LANES)
    alog = jnp.pad(a_log, ((0, 0), (0, pad))).reshape(2, 1, LANES)
    dsk = jnp.repeat(d_skip, SSD_HEAD_DIM).reshape(1, SSD_INNER)
    return pl.pallas_call(
        functools.partial(_ssd_kernel, q=q, nct=nct, nt=nt),
        grid=(2, b, nt),
        in_specs=[
            pl.BlockSpec((None, q, cd), lambda d, bb, g: (bb, tile(d, bb, g), 0)),
            pl.BlockSpec((None, SUBLANES, cd),
                         lambda d, bb, g: (bb, jnp.maximum(tile(d, bb, g) * r8 - 1, 0), 0)),
            pl.BlockSpec((None, SUBLANES, cd),
                         lambda d, bb, g: (bb, jnp.minimum((tile(d, bb, g) + 1) * r8, l // SUBLANES - 1), 0)),
            pl.BlockSpec((None, q, LANES), lambda d, bb, g: (bb, tile(d, bb, g), 0)),
            pl.BlockSpec((CONV_W, 1, cd), lambda d, bb, g: (0, 0, 0)),
            pl.BlockSpec((1, cd), lambda d, bb, g: (0, 0)),
            pl.BlockSpec((None, 1, LANES), lambda d, bb, g: (d, 0, 0)),
            pl.BlockSpec((None, 1, LANES), lambda d, bb, g: (d, 0, 0)),
            pl.BlockSpec((1, SSD_INNER), lambda d, bb, g: (0, 0)),
        ],
        out_specs=pl.BlockSpec((None, None, q, SSD_INNER), lambda d, bb, g: (d, bb, tile(d, bb, g), 0)),
        out_shape=jax.ShapeDtypeStruct((2, b, l, SSD_INNER), F32),
        scratch_shapes=[
            pltpu.VMEM((q + 2 * SUBLANES, cd), F32),
            pltpu.VMEM((SSD_HEADS, SSD_STATE, SSD_HEAD_DIM), F32),
        ],
        compiler_params=_cparams(("arbitrary", "arbitrary", "arbitrary")),
        name="ssd_chunked",
    )(xbc, xbc, xbc, dt, conv_w.reshape(CONV_W, 1, cd), conv_b.reshape(1, cd), dtb, alog, dsk)


def _out_odd_kernel(x_ref, diff_ref, y_ref, z_ref, ng_ref, wa_ref, wb_ref, g1_ref, *refs):
    route_in, o_ref, route_out = refs[:5], refs[5], refs[6:]
    yz = (y_ref[0] + y_ref[1]) * _silu(z_ref[...].astype(F32))
    gs = SSD_INNER // SSD_GROUPS
    parts = []
    for gi in range(SSD_GROUPS):
        seg = yz[:, gi * gs:(gi + 1) * gs]
        ms = jnp.mean(seg * seg, axis=-1, keepdims=True)
        parts.append(seg * lax.rsqrt(ms + NORM_EPS) * ng_ref[:, gi * gs:(gi + 1) * gs])
    ssd = jnp.concatenate(parts, axis=-1).astype(BF16)
    y = (jnp.dot(diff_ref[...], wa_ref[...], preferred_element_type=F32)
         + jnp.dot(ssd, wb_ref[...], preferred_element_type=F32))
    x = x_ref[...] + g1_ref[...] * y
    o_ref[...] = x
    _route(x, *route_in, *route_out)


def _out_odd(xc, diff, y2, z, norm_g, w_a, w_b, mods3, layer, c_len, g_ffn, w_router, b_router):
    b, l, d = xc.shape
    s_len = l - c_len
    tm = ROW_TILE
    off = c_len // tm
    w = SSD_INNER
    row = lambda bb, i: bb
    r_in, r_args, r_out, r_shapes, r_scratch = _route_plumbing(b, s_len, d, g_ffn, mods3, layer, row, w_router,
                                                               b_router)
    return pl.pallas_call(
        _out_odd_kernel,
        grid=(b, s_len // tm),
        in_specs=[
            pl.BlockSpec((None, tm, d), lambda bb, i: (bb, i + off, 0)),
            pl.BlockSpec((None, tm, diff.shape[-1]), lambda bb, i: (bb, i, 0)),
            pl.BlockSpec((2, None, tm, w), lambda bb, i: (0, bb, i + off, 0)),
            pl.BlockSpec((None, tm, w), lambda bb, i: (bb, i + off, 0)),
            pl.BlockSpec((1, w), lambda bb, i: (0, 0)),
            pl.BlockSpec(w_a.shape, lambda bb, i: (0, 0)),
            pl.BlockSpec(w_b.shape, lambda bb, i: (0, 0)),
            _mod_spec(d, layer, 2, row),
        ] + r_in,
        out_specs=[pl.BlockSpec((None, tm, d), lambda bb, i: (bb, i, 0))] + r_out,
        out_shape=[jax.ShapeDtypeStruct((b, s_len, d), F32)] + r_shapes,
        scratch_shapes=r_scratch,
        compiler_params=_cparams(("arbitrary", "arbitrary")),
        name="out_proj_odd",
    )(xc, diff, y2, z, norm_g.reshape(1, w), w_a, w_b, mods3, *r_args)


def _route(x, g_ref, sh_ref, sc_ref, wr_ref, br_ref, h_ref, eid_ref, rnk_ref, gate_ref, cnt_ref, carry_scr):
    @pl.when((pl.program_id(0) == 0) & (pl.program_id(1) == 0))
    def _():
        carry_scr[...] = jnp.zeros_like(carry_scr)

    h = _norm_mod(x, g_ref[...], sh_ref[...], sc_ref[...])
    _store_chunk_rows(h_ref, _pack_bf16(h))
    tm = h.shape[0]
    per = N_EXPERTS // N_EXPERT_GROUPS
    h_hi = h.astype(BF16)
    h_lo = (h - h_hi.astype(F32)).astype(BF16)
    logits = _nt_dot(wr_ref[0], h_hi) + _nt_dot(wr_ref[0], h_lo) + _nt_dot(wr_ref[1], h_hi)
    scores = jax.nn.sigmoid(logits)
    sel = scores + br_ref[...]
    sel3 = sel.reshape(N_EXPERT_GROUPS, per, tm)
    kio = lax.broadcasted_iota(jnp.int32, sel3.shape, 1)
    m1 = jnp.max(sel3, axis=1, keepdims=True)
    first = jnp.min(jnp.where(sel3 == m1, kio, per), axis=1, keepdims=True)
    m2 = jnp.max(jnp.where(kio == first, NEG_BIG, sel3), axis=1, keepdims=True)
    gs = m1 + m2
    gio = lax.broadcasted_iota(jnp.int32, gs.shape, 0)
    ahead = jnp.zeros(gs.shape, jnp.int32)
    for gp in range(N_EXPERT_GROUPS):
        other = gs[gp:gp + 1]
        ahead = ahead + jnp.where((other > gs) | ((other == gs) & (gp < gio)), 1, 0)
    grp_on = jnp.where(ahead < TOPK_GROUPS, 1.0, 0.0)
    selm = jnp.where(jnp.broadcast_to(grp_on, sel3.shape) > 0.5, sel3, NEG_BIG).reshape(N_EXPERTS, tm)
    eio = lax.broadcasted_iota(jnp.int32, selm.shape, 0)
    work = selm
    cf = jnp.zeros(selm.shape, F32)
    e_rows, s_rows = [], []
    for k in range(TOP_K):
        best = jnp.max(work, axis=0, keepdims=True)
        idx = jnp.min(jnp.where(work == best, eio, N_EXPERTS), axis=0, keepdims=True)
        hit = eio == idx
        cf = cf + jnp.where(hit, 1.0, 0.0)
        work = jnp.where(hit, NEG_BIG, work)
        e_rows.append(idx)
        s_rows.append(jnp.sum(jnp.where(hit, scores, 0.0), axis=0, keepdims=True))
    denom = s_rows[0]
    for s_k in s_rows[1:]:
        denom = denom + s_k
    g_rows = [s_k / denom * ROUTED_SCALE for s_k in s_rows]
    ti = lax.broadcasted_iota(jnp.int32, (tm, tm), 0)
    tj = lax.broadcasted_iota(jnp.int32, (tm, tm), 1)
    before = jnp.where(ti < tj, 1.0, 0.0).astype(BF16)
    in_expert = carry_scr[:, 0:1] + jnp.dot(cf.astype(BF16), before, preferred_element_type=F32)
    carry_scr[...] = carry_scr[...] + jnp.sum(cf, axis=1, keepdims=True)
    cnt_ref[...] = carry_scr[...]
    r_rows = [jnp.sum(jnp.where(eio == idx, in_expert, 0.0), axis=0, keepdims=True) for idx in e_rows]
    eid_ref[...] = jnp.concatenate(e_rows, axis=0)
    rnk_ref[...] = jnp.concatenate(r_rows, axis=0).astype(jnp.int32)
    padded = jnp.concatenate(g_rows + [jnp.zeros((LANES - TOP_K, tm), F32)], axis=0)
    gate_ref[...] = padded.T


def _route_plumbing(b, r, d, g, mods3, layer, row_fn, w_router, b_router):
    tm = ROW_TILE
    nt = r // tm
    w_t = w_router.T
    w_hi = w_t.astype(BF16)
    w_router_t = jnp.stack([w_hi, (w_t - w_hi.astype(F32)).astype(BF16)])
    slot = pl.BlockSpec((TOP_K, tm), lambda bb, i: (0, bb * nt + i))
    slot_shape = jax.ShapeDtypeStruct((TOP_K, b * r), jnp.int32)
    in_specs = [
        pl.BlockSpec((1, d), lambda bb, i: (0, 0)),
        _mod_spec(d, layer, 3, row_fn),
        _mod_spec(d, layer, 4, row_fn),
        pl.BlockSpec(w_router_t.shape, lambda bb, i: (0, 0, 0)),
        pl.BlockSpec((N_EXPERTS, 1), lambda bb, i: (0, 0)),
    ]
    args = [g.reshape(1, d), mods3, mods3, w_router_t, b_router.reshape(N_EXPERTS, 1)]
    out_specs = [
        pl.BlockSpec((tm * ROW_CHUNKS, LANES), lambda bb, i: (bb * nt + i, 0)),
        slot,
        slot,
        pl.BlockSpec((None, tm, LANES), lambda bb, i: (bb, i, 0)),
        pl.BlockSpec((N_EXPERTS, LANES), lambda bb, i: (0, 0)),
    ]
    out_shapes = [jax.ShapeDtypeStruct((b * r * ROW_CHUNKS, LANES), jnp.uint32), slot_shape, slot_shape,
                  jax.ShapeDtypeStruct((b, r, LANES), F32), jax.ShapeDtypeStruct((N_EXPERTS, LANES), F32)]
    scratch = [pltpu.VMEM((N_EXPERTS, LANES), F32)]
    return in_specs, args, out_specs, out_shapes, scratch


def _moe_plan(counts, n_rows):
    blk = EXPERT_BLK
    nb = n_rows // blk
    ends = jnp.cumsum(counts)
    starts = ends - counts
    count_le = lambda sorted_vals, q: jnp.sum(sorted_vals[None, :] <= q[:, None], axis=1, dtype=jnp.int32)
    first = jnp.arange(nb, dtype=jnp.int32) * blk
    e_lo = count_le(ends, first)
    e_hi = count_le(ends, first + (blk - 1))
    n_pair = e_hi - e_lo + 1
    p_end = jnp.cumsum(n_pair)
    p_start = p_end - n_pair
    i = jnp.arange(nb + N_EXPERTS - 1, dtype=jnp.int32)
    j = jnp.minimum(count_le(p_end, i), nb - 1)
    valid = i < p_end[-1]
    e = jnp.where(valid, e_lo[j] + i - p_start[j], e_hi[nb - 1]).astype(jnp.int32)
    bounds = jnp.concatenate([starts, ends[-1:]]).astype(jnp.int32)
    return j, e, valid.astype(jnp.int32), bounds


def _positions_kernel(starts_ref, eid_ref, rnk_ref, pos_ref):
    eid = eid_ref[...]
    pos = rnk_ref[...]
    for e in range(N_EXPERTS):
        pos = pos + jnp.where(eid == e, starts_ref[e], 0)
    pos_ref[...] = pos * ROW_CHUNKS


def _positions(eid, rnk, starts):
    full = pl.BlockSpec(eid.shape, lambda: (0, 0))
    return pl.pallas_call(
        _positions_kernel,
        in_specs=[pl.BlockSpec(memory_space=pltpu.SMEM), full, full],
        out_specs=full,
        out_shape=jax.ShapeDtypeStruct(eid.shape, jnp.int32),
        compiler_params=pltpu.CompilerParams(vmem_limit_bytes=VMEM_LIMIT),
        name="moe_positions",
    )(starts, eid, rnk)


def _token_row(ref, first):
    return ref.at[pl.ds(pl.multiple_of(first, ROW_CHUNKS), ROW_CHUNKS)]


def _dispatch_kernel(pos_ref, h_ref, xs_ref, sem):
    tm = h_ref.shape[0] // ROW_CHUNKS

    def issue(t, carry):
        src = _token_row(h_ref, t * ROW_CHUNKS)
        for k in range(TOP_K):
            pltpu.make_async_copy(src, _token_row(xs_ref, pos_ref[k, t]), sem).start(priority=k % 2)
        return carry

    lax.fori_loop(0, tm, issue, 0)
    done = pl.ds(0, tm * ROW_CHUNKS)
    for _ in range(TOP_K):
        pltpu.make_async_copy(h_ref.at[done], xs_ref.at[done], sem).wait()


def _dispatch(h2, pos):
    rows, w = h2.shape
    tm = math.gcd(DISPATCH_TILE, rows // ROW_CHUNKS)
    return pl.pallas_call(
        _dispatch_kernel,
        grid=(rows // (tm * ROW_CHUNKS),),
        in_specs=[
            pl.BlockSpec((TOP_K, tm), lambda i: (0, i), memory_space=pltpu.SMEM),
            pl.BlockSpec((tm * ROW_CHUNKS, w), lambda i: (i, 0)),
        ],
        out_specs=pl.BlockSpec(memory_space=pl.ANY),
        out_shape=jax.ShapeDtypeStruct((rows * TOP_K, w), h2.dtype),
        scratch_shapes=[pltpu.SemaphoreType.DMA],
        compiler_params=_cparams(("arbitrary",)),
        name="moe_dispatch",
    )(pos, h2)


def _grouped_kernel(pb_ref, pe_ref, pv_ref, bnd_ref, xs_ref, wg_ref, wu_ref, wd_ref, y_ref, wgb, wub, wdb):
    i = pl.program_id(0)
    prev = jnp.maximum(i - 1, 0)
    j = pb_ref[i]
    e = pe_ref[i]
    blk = xs_ref.shape[0] // ROW_CHUNKS

    @pl.when((i == 0) | (pb_ref[prev] != j))
    def _():
        y_ref[...] = jnp.zeros_like(y_ref)

    @pl.when((i == 0) | (pe_ref[prev] != e))
    def _():
        wgb[...] = wg_ref[...].astype(BF16)
        wub[...] = wu_ref[...].astype(BF16)
        wdb[...] = wd_ref[...].astype(BF16)

    @pl.when(pv_ref[i] == 1)
    def _():
        xw = _load_chunk_rows(xs_ref, blk)
        a = _packed_dot(xw, wgb)
        u = _packed_dot(xw, wub)
        yv = jnp.dot((_silu(a) * u).astype(BF16), wdb[...], preferred_element_type=F32)
        rows = j * blk + lax.broadcasted_iota(jnp.int32, (blk, 1), 0)
        own = (rows >= bnd_ref[e]) & (rows < bnd_ref[e + 1])
        yw = _pack_bf16(yv)
        for c in range(ROW_CHUNKS):
            sl = pl.ds(c, blk, stride=ROW_CHUNKS)
            y_ref[sl, :] = jnp.where(own, yw[:, c * LANES:(c + 1) * LANES], y_ref[sl, :])


def _grouped(pb, pe, pv, bounds, xs, wg, wu, wd, layer):
    p, half = xs.shape
    d = 2 * ROW_CHUNKS * LANES
    blk = EXPERT_BLK * ROW_CHUNKS
    grid_spec = pltpu.PrefetchScalarGridSpec(
        num_scalar_prefetch=4,
        grid=(pb.shape[0],),
        in_specs=[
            pl.BlockSpec((blk, half), lambda i, pb, pe, pv, bnd: (pb[i], 0)),
            pl.BlockSpec((None, None, d, D_EXPERT), lambda i, pb, pe, pv, bnd: (layer, pe[i], 0, 0)),
            pl.BlockSpec((None, None, d, D_EXPERT), lambda i, pb, pe, pv, bnd: (layer, pe[i], 0, 0)),
            pl.BlockSpec((None, None, D_EXPERT, d), lambda i, pb, pe, pv, bnd: (layer, pe[i], 0, 0)),
        ],
        out_specs=pl.BlockSpec((blk, half), lambda i, pb, pe, pv, bnd: (pb[i], 0)),
        scratch_shapes=[
            pltpu.VMEM((d, D_EXPERT), BF16),
            pltpu.VMEM((d, D_EXPERT), BF16),
            pltpu.VMEM((D_EXPERT, d), BF16),
        ],
    )
    return pl.pallas_call(
        _grouped_kernel,
        grid_spec=grid_spec,
        out_shape=jax.ShapeDtypeStruct((p, half), jnp.uint32),
        compiler_params=_cparams(("arbitrary",)),
        name="moe_grouped_experts",
    )(pb, pe, pv, bounds, xs, wg, wu, wd)


def _combine_kernel(*refs, final, c_len):
    pos_ref, y_ref, gate_ref, h_ref, x_ref, g2c_ref, g2l_ref, sg_ref, su_ref, sd_ref = refs[:10]
    o_ref, buf, sem = refs[-3:]
    tm = x_ref.shape[0]

    def issue(t, carry):
        for k in range(TOP_K):
            pltpu.make_async_copy(_token_row(y_ref, pos_ref[k, t]), _token_row(buf.at[k], t * ROW_CHUNKS),
                                  sem).start(priority=k % 2)
        return carry

    lax.fori_loop(0, tm, issue, 0)
    hw = _load_chunk_rows(h_ref, tm)
    a = _packed_dot(hw, sg_ref)
    u = _packed_dot(hw, su_ref)
    acc = jnp.dot((_silu(a) * u).astype(BF16), sd_ref[...], preferred_element_type=F32)
    done = pl.ds(0, tm * ROW_CHUNKS)
    for k in range(TOP_K):
        pltpu.make_async_copy(y_ref.at[done], buf.at[k, done], sem).wait()
    g = gate_ref[...]
    half = hw.shape[-1]
    acc_hi = acc[:, :half]
    acc_lo = acc[:, half:]
    for k in range(TOP_K):
        hi, lo = _unpack_bf16(_load_chunk_rows(buf.at[k], tm))
        acc_hi = acc_hi + g[:, k:k + 1] * hi
        acc_lo = acc_lo + g[:, k:k + 1] * lo
    is_ctx = pl.program_id(1) * tm + lax.broadcasted_iota(jnp.int32, (tm, 1), 0) < c_len
    g2 = jnp.where(is_ctx, g2c_ref[...], g2l_ref[...])
    x = x_ref[...] + g2 * jnp.concatenate([acc_hi, acc_lo], axis=-1)
    if final:
        gf_ref = refs[10]
        ms = jnp.mean(x * x, axis=-1, keepdims=True)
        x = x * lax.rsqrt(ms + NORM_EPS) * gf_ref[...]
    o_ref[...] = x


def _combine(pos, y, gates, h2, x, mods3, layer, c_len, tm, sg, su, sd, g_final=None):
    b, r, d = x.shape
    nt = r // tm
    tile = pl.BlockSpec((None, tm, d), lambda bb, i: (bb, i, 0))
    in_specs = [
        pl.BlockSpec((TOP_K, tm), lambda bb, i: (0, bb * nt + i), memory_space=pltpu.SMEM),
        pl.BlockSpec(memory_space=pl.ANY),
        pl.BlockSpec((None, tm, LANES), lambda bb, i: (bb, i, 0)),
        pl.BlockSpec((tm * ROW_CHUNKS, LANES), lambda bb, i: (bb * nt + i, 0)),
        tile,
        _mod_spec(d, layer, 5, lambda bb, i: SUBLANES),
        _mod_spec(d, layer, 5, lambda bb, i: bb),
        pl.BlockSpec(sg.shape, lambda bb, i: (0, 0)),
        pl.BlockSpec(su.shape, lambda bb, i: (0, 0)),
        pl.BlockSpec(sd.shape, lambda bb, i: (0, 0)),
    ]
    args = [pos, y, gates, h2, x, mods3, mods3, sg, su, sd]
    if g_final is not None:
        in_specs.append(pl.BlockSpec((1, d), lambda bb, i: (0, 0)))
        args.append(g_final.reshape(1, d))
    return pl.pallas_call(
        functools.partial(_combine_kernel, final=g_final is not None, c_len=c_len),
        grid=(b, nt),
        in_specs=in_specs,
        out_specs=tile,
        out_shape=jax.ShapeDtypeStruct((b, r, d), F32),
        scratch_shapes=[pltpu.VMEM((TOP_K, tm * ROW_CHUNKS, LANES), jnp.uint32), pltpu.SemaphoreType.DMA],
        compiler_params=_cparams(("arbitrary", "arbitrary")),
        name="moe_combine",
    )(*args)


def _moe(x, routed, mods3, layer, c_len, combine_tile, w_e_gate, w_e_up, w_e_down, ws_gate, ws_up, ws_down,
         g_final=None):
    b, r, d = x.shape
    h2, eid, rnk, gates, cnt = routed
    pb, pe, pv, bounds = _moe_plan(cnt[:, 0].astype(jnp.int32), b * r * TOP_K)
    pos = _positions(eid, rnk, bounds[:N_EXPERTS])
    xs = _dispatch(h2, pos)
    y = _grouped(pb, pe, pv, bounds, xs, w_e_gate, w_e_up, w_e_down, layer)
    return _combine(pos, y, gates, h2, x, mods3, layer, c_len, combine_tile,
                    ws_gate.astype(BF16), ws_up.astype(BF16), ws_down.astype(BF16), g_final)


def kernel(x, c, ctx, c_ctx, w_mod, b_mod, g_mix, g_ffn, g_final, ab_w_in, ab_w_out, ab_conv_w, ab_conv_b, ab_w_r, ab_b_r, ab_w_i, ab_b_i, ab_lam, ab_sink, cd_w_in, cd_w_out, cd_lam, cd_subln_g, cd_conv_w, cd_conv_b, cd_dt_bias, cd_a_log, cd_d_skip, cd_norm_g, w_router, b_router, w_e_gate, w_e_up, w_e_down, ws_gate, ws_up, ws_down):
    bsz, s_len, d = x.shape
    c_len = ctx.shape[1]
    depth = w_mod.shape[0]
    assert depth == 2 and bsz == SUBLANES, "kernels are specialised to depth 2 and batch 8"
    assert c_len % ROW_TILE == 0 and s_len % ROW_TILE == 0
    nct_row = c_len // ROW_TILE
    nct_time = c_len // TIME_TILE

    c_all = jnp.concatenate([c, c_ctx[None], jnp.zeros((MOD_ROWS - bsz - 1, d), F32)], axis=0)
    mods3 = _modulations(c_all, w_mod, b_mod).reshape(depth * MOD_ROWS, 1, N_MOD * d)
    rope_tabs = _rope_tables(c_len, s_len)
    xc = jnp.concatenate([ctx, x], axis=1)

    w_in = ab_w_in[0].astype(BF16)
    q_hi = LRU_WIDTH + WIN_HEADS * HEAD_DIM
    x_hi = q_hi + LRU_WIDTH
    k_hi = x_hi + WIN_KV_HEADS * HEAD_DIM
    gate, q, xa, k, v = _project(xc, g_mix[0], mods3, 0, nct_row, rope_tabs, [
        (w_in[:, :LRU_WIDTH], None, BF16, False),
        (w_in[:, LRU_WIDTH:q_hi], HEAD_DIM ** -0.5 * LOG2E, BF16, False),
        (w_in[:, q_hi:x_hi], None, F32, True),
        (w_in[:, x_hi:k_hi], 1.0, BF16, False),
        (w_in[:, k_hi:], None, BF16, False),
    ])
    l_len = c_len + s_len
    w_gates = jnp.stack([jnp.concatenate([_block_diag(ab_w_r[0, dd]), _block_diag(ab_w_i[0, dd])], axis=1)
                         for dd in range(2)]).astype(BF16)
    b_gates = jnp.concatenate([ab_b_r[0], ab_b_i[0]], axis=-1).reshape(2, 1, 2 * LRU_WIDTH)
    rec = _rglru(xa.reshape(l_len, bsz, LRU_WIDTH), ab_conv_w[0], ab_conv_b[0], w_gates, b_gates,
                 ab_lam[0].reshape(2, 1, LRU_WIDTH), nct_time)
    att = _win_attention(q, k, v, ab_sink[0], c_len)
    w_out = ab_w_out[0].astype(BF16)
    xc, *routed = _out_even(xc, rec.reshape(2, l_len, bsz * LRU_WIDTH), gate, att, w_out[:LRU_WIDTH],
                            w_out[LRU_WIDTH:], mods3, 0, nct_row, g_ffn[0], w_router[0], b_router[0])
    xc = _moe(xc, routed, mods3, 0, c_len, math.gcd(MIXED_COMBINE_TILE, l_len), w_e_gate, w_e_up, w_e_down,
              ws_gate[0], ws_up[0], ws_down[0])

    w_in = cd_w_in[0].astype(BF16)
    qk = DIFF_HEADS * 2 * DIFF_DH
    z_hi = qk + SSD_INNER
    k_hi = z_hi + qk
    v_hi = k_hi + qk
    x_hi = v_hi + SSD_CONV_DIM
    w_dt = jnp.pad(w_in[:, x_hi:], ((0, 0), (0, LANES - 2 * SSD_HEADS)))
    q, z, k, v, xbc, dt = _project(xc, g_mix[1], mods3, 1, nct_row, rope_tabs, [
        (w_in[:, :qk], DIFF_DH ** -0.5 * LOG2E, BF16, False),
        (w_in[:, qk:z_hi], None, BF16, False),
        (w_in[:, z_hi:k_hi], 1.0, BF16, False),
        (w_in[:, k_hi:v_hi], None, BF16, False),
        (w_in[:, v_hi:x_hi], None, F32, False),
        (w_dt, None, F32, False),
    ])
    lam_init = 0.8 - 0.6 * math.exp(-0.3 * 1)
    diff = _diff_attention(q, k, v, cd_lam[0], cd_subln_g[0], lam_init, c_len)
    y2 = _ssd(xbc, dt, cd_conv_w[0], cd_conv_b[0], cd_dt_bias[0], cd_a_log[0], cd_d_skip[0], nct_time)
    w_out = cd_w_out[0].astype(BF16)
    xl, *routed = _out_odd(xc, diff, y2, z, cd_norm_g[0], w_out[:qk], w_out[qk:], mods3, 1, c_len,
                           g_ffn[1], w_router[1], b_router[1])
    return _moe(xl, routed, mods3, 1, 0, math.gcd(LATENT_COMBINE_TILE, s_len), w_e_gate, w_e_up, w_e_down,
                ws_gate[1], ws_up[1], ws_down[1], g_final=g_final)
```

```python
import functools
import math

import jax
import jax.numpy as jnp
from jax import lax
from jax.experimental import pallas as pl
from jax.experimental.pallas import tpu as pltpu

F32 = jnp.float32
BF16 = jnp.bfloat16
HIGHEST = lax.Precision.HIGHEST

GRID_W = 64
N_MOD = 6
NORM_EPS = 1e-6
ROPE_BASE = 10000.0
CONV_W = 4

LRU_WIDTH = 512
LRU_BLOCKS = 8
LRU_C = 8.0

HEAD_DIM = 64
WIN_HEADS = 8
WIN_KV_HEADS = 2
WINDOW = 128

DIFF_HEADS = 4
DIFF_DH = 64

SSD_HEADS = 8
SSD_HEAD_DIM = 64
SSD_INNER = SSD_HEADS * SSD_HEAD_DIM
SSD_GROUPS = 2
SSD_STATE = 128
SSD_CONV_DIM = SSD_INNER + 2 * SSD_GROUPS * SSD_STATE

N_EXPERTS = 64
N_EXPERT_GROUPS = 8
TOPK_GROUPS = 4
TOP_K = 8
D_EXPERT = 256
ROUTED_SCALE = 2.5

LANES = 128
SUBLANES = 8
MOD_ROWS = 16
TIME_TILE = 128
ROW_TILE = 256
DISPATCH_TILE = 2048
LATENT_COMBINE_TILE = 512
MIXED_COMBINE_TILE = 768
EXPERT_BLK = 1024
ROW_CHUNKS = 4
VMEM_LIMIT = 48 * 1024 * 1024
NEG_BIG = -1e30
LOG2E = math.log2(math.e)


def _cparams(sem):
    return pltpu.CompilerParams(dimension_semantics=sem, vmem_limit_bytes=VMEM_LIMIT)


def _nt_dot(a, b):
    return lax.dot_general(a, b, (((1,), (1,)), ((), ())), preferred_element_type=F32)


def _softplus(x):
    return jnp.maximum(x, 0.0) + jnp.log1p(jnp.exp(-jnp.abs(x)))


def _sigmoid(x):
    return 0.5 * jnp.tanh(0.5 * x) + 0.5


def _silu(x):
    return x * _sigmoid(x)


def _pack_bf16(x):
    half = x.shape[-1] // 2
    bits = pltpu.bitcast(x.astype(BF16).astype(F32), jnp.uint32)
    return bits[:, :half] | (bits[:, half:] >> 16)


def _unpack_bf16(w):
    hi = pltpu.bitcast(w & jnp.uint32(0xFFFF0000), F32)
    lo = pltpu.bitcast(w << 16, F32)
    return hi, lo


def _store_chunk_rows(ref, w):
    n = w.shape[0]
    for j in range(ROW_CHUNKS):
        ref[pl.ds(j, n, stride=ROW_CHUNKS), :] = w[:, j * LANES:(j + 1) * LANES]


def _load_chunk_rows(ref, n):
    return jnp.concatenate([ref[pl.ds(j, n, stride=ROW_CHUNKS), :] for j in range(ROW_CHUNKS)], axis=1)


def _packed_dot(w, weight_ref):
    half = w.shape[-1]
    hi, lo = _unpack_bf16(w)
    return (jnp.dot(hi.astype(BF16), weight_ref[:half, :], preferred_element_type=F32)
            + jnp.dot(lo.astype(BF16), weight_ref[half:, :], preferred_element_type=F32))


def _mod_kernel(c_ref, w_ref, b_ref, o_ref):
    c = c_ref[...]
    s = _silu(c)
    o_ref[...] = jnp.dot(s, w_ref[...], preferred_element_type=F32, precision=HIGHEST) + b_ref[...]


def _modulations(c_all, w_mod, b_mod):
    depth, d, _ = w_mod.shape
    return pl.pallas_call(
        _mod_kernel,
        grid=(depth, N_MOD),
        in_specs=[
            pl.BlockSpec((MOD_ROWS, d), lambda l, k: (0, 0)),
            pl.BlockSpec((None, d, d), lambda l, k: (l, 0, k)),
            pl.BlockSpec((None, 1, d), lambda l, k: (l, 0, k)),
        ],
        out_specs=pl.BlockSpec((None, MOD_ROWS, d), lambda l, k: (l, 0, k)),
        out_shape=jax.ShapeDtypeStruct((depth, MOD_ROWS, N_MOD * d), F32),
        compiler_params=_cparams(("arbitrary", "arbitrary")),
        name="adaln_modulation",
    )(c_all, w_mod, b_mod.reshape(depth, 1, N_MOD * d))


def _mod_spec(d, layer, chunk, row_fn):
    return pl.BlockSpec((None, 1, d), lambda b, i: (layer * MOD_ROWS + row_fn(b, i), 0, chunk))


def _norm_mod(x, g, sh, sc):
    ms = jnp.mean(x * x, axis=-1, keepdims=True)
    return (x * lax.rsqrt(ms + NORM_EPS) * g) * (1.0 + sc) + sh


def _rope(y, cos, sa, sb):
    n = y.shape[-1]
    half = HEAD_DIM // 2
    return y * cos + pltpu.roll(y, n - half, 1) * sa + pltpu.roll(y, half, 1) * sb


def _proj_kernel(*refs, ropes):
    n = len(ropes)
    x_ref, g_ref, sh_ref, sc_ref, cos_ref, sa_ref, sb_ref = refs[:7]
    w_refs = refs[7:7 + n]
    o_refs = refs[7 + n:]
    h = _norm_mod(x_ref[...], g_ref[...], sh_ref[...], sc_ref[...]).astype(BF16)
    for w_ref, o_ref, rope in zip(w_refs, o_refs, ropes):
        y = jnp.dot(h, w_ref[...], preferred_element_type=F32)
        if rope is not None:
            w = y.shape[-1]
            y = _rope(y, cos_ref[:, :w], sa_ref[:, :w], sb_ref[:, :w])
            if rope != 1.0:
                y = y * rope
        o_ref[...] = y.astype(o_ref.dtype)


def _project(xc, g, mods3, layer, nct, rope_tabs, groups):
    b, l, d = xc.shape
    tm = ROW_TILE
    mod_row = lambda i, bb: layer * MOD_ROWS + jnp.where(i < nct, SUBLANES, bb)
    rw = rope_tabs[0].shape[-1]
    in_specs = [
        pl.BlockSpec((None, tm, d), lambda i, bb: (bb, i, 0)),
        pl.BlockSpec((1, d), lambda i, bb: (0, 0)),
        pl.BlockSpec((None, 1, d), lambda i, bb: (mod_row(i, bb), 0, 0)),
        pl.BlockSpec((None, 1, d), lambda i, bb: (mod_row(i, bb), 0, 1)),
    ] + [pl.BlockSpec((tm, rw), lambda i, bb: (i, 0))] * 3
    out_specs, out_shapes = [], []
    for w, _, dt, time_major in groups:
        n = w.shape[1]
        in_specs.append(pl.BlockSpec((d, n), lambda i, bb: (0, 0)))
        if time_major:
            out_specs.append(pl.BlockSpec((tm, n), lambda i, bb: (i, bb)))
            out_shapes.append(jax.ShapeDtypeStruct((l, b * n), dt))
        else:
            out_specs.append(pl.BlockSpec((None, tm, n), lambda i, bb: (bb, i, 0)))
            out_shapes.append(jax.ShapeDtypeStruct((b, l, n), dt))
    return pl.pallas_call(
        functools.partial(_proj_kernel, ropes=tuple(gp[1] for gp in groups)),
        grid=(l // tm, b),
        in_specs=in_specs,
        out_specs=out_specs,
        out_shape=out_shapes,
        compiler_params=_cparams(("arbitrary", "arbitrary")),
        name="norm_mod_project",
    )(xc, g.reshape(1, d), mods3, mods3, *rope_tabs, *[gp[0] for gp in groups])


def _rope_tables(c_len, s_len):
    rows = s_len // GRID_W
    row = jnp.repeat(jnp.arange(rows), GRID_W).astype(F32)
    col = jnp.tile(jnp.arange(GRID_W), rows).astype(F32)
    n = HEAD_DIM // 4
    inv = ROPE_BASE ** (-jnp.arange(n, dtype=F32) / n)
    ang = jnp.concatenate([row[:, None] * inv, col[:, None] * inv], axis=-1)
    cos, sin = jnp.cos(ang), jnp.sin(ang)
    zero = jnp.zeros_like(sin)
    reps = WIN_HEADS
    cos_t = jnp.tile(jnp.concatenate([cos, cos], axis=-1), (1, reps))
    sa_t = jnp.tile(jnp.concatenate([-sin, zero], axis=-1), (1, reps))
    sb_t = jnp.tile(jnp.concatenate([zero, sin], axis=-1), (1, reps))
    w = cos_t.shape[-1]
    pad1 = jnp.ones((c_len, w), F32)
    pad0 = jnp.zeros((c_len, w), F32)
    return (jnp.concatenate([pad1, cos_t], 0), jnp.concatenate([pad0, sa_t], 0),
            jnp.concatenate([pad0, sb_t], 0))


def _seq_tile(d, g, nct, nt):
    rev = jnp.where(g < nct, nct - 1 - g, nt - 1 - (g - nct))
    return jnp.where(d == 0, g, rev)


def _rglru_kernel(x_ref, xp_ref, xn_ref, cw_ref, cb_ref, w_ref, bias_ref, lam_ref, o_ref,
                  ext_scr, a_scr, b_scr, h_scr, *, ts, nct, nt, sub):
    d = pl.program_id(0)
    g = pl.program_id(1)
    tile = _seq_tile(d, g, nct, nt)
    bsz, width = h_scr.shape
    pv = jnp.where((tile == 0) | (tile == nct), 0.0, 1.0)
    nv = jnp.where((tile == nct - 1) | (tile == nt - 1), 0.0, 1.0)
    ext_scr[0:1] = xp_ref[...] * pv
    ext_scr[1:ts + 1] = x_ref[...]
    ext_scr[ts + 1:ts + 3] = xn_ref[...] * nv

    @pl.when(g == 0)
    def _():
        h_scr[...] = jnp.zeros_like(h_scr)

    neg_sp = -LRU_C * _softplus(-lam_ref[...])

    def prep(c, carry):
        r0 = pl.multiple_of(c * sub, sub)
        e = ext_scr[pl.ds(r0, sub + CONV_W - 1)]
        u = cb_ref[...] + cw_ref[0] * e[0:sub]
        for j in range(1, CONV_W):
            u = u + cw_ref[j] * e[j:j + sub]
        u2 = u.reshape(sub * bsz, width)
        gts = jnp.dot(u2.astype(BF16), w_ref[...], preferred_element_type=F32) + bias_ref[...]
        r = _sigmoid(gts[:, :width])
        ig = _sigmoid(gts[:, width:])
        log_a = neg_sp * r
        a = jnp.exp(log_a)
        mult = jnp.sqrt(1.0 - a * a)
        a_scr[pl.ds(r0, sub)] = a.reshape(sub, bsz, width)
        b_scr[pl.ds(r0, sub)] = (mult * ig * u2).reshape(sub, bsz, width)
        return carry

    lax.fori_loop(0, ts // sub, prep, 0)

    def step(t, h):
        tt = jnp.where(d == 0, t, ts - 1 - t)
        h = a_scr[tt] * h + b_scr[tt]
        o_ref[tt] = h
        return h

    h_scr[...] = lax.fori_loop(0, ts, step, h_scr[...], unroll=8)


def _rglru(xa_tm, conv_w, conv_b, w_gates, b_gates, lam, nct):
    l, bsz, width = xa_tm.shape
    ts = TIME_TILE
    nt = l // ts
    tile = lambda d, g: _seq_tile(d, g, nct, nt)
    kern = functools.partial(_rglru_kernel, ts=ts, nct=nct, nt=nt, sub=16)
    return pl.pallas_call(
        kern,
        grid=(2, nt),
        in_specs=[
            pl.BlockSpec((ts, bsz, width), lambda d, g: (tile(d, g), 0, 0)),
            pl.BlockSpec((1, bsz, width), lambda d, g: (jnp.maximum(tile(d, g) * ts - 1, 0), 0, 0)),
            pl.BlockSpec((2, bsz, width),
                         lambda d, g: (jnp.minimum((tile(d, g) + 1) * (ts // 2), l // 2 - 1), 0, 0)),
            pl.BlockSpec((CONV_W, 1, width), lambda d, g: (0, 0, 0)),
            pl.BlockSpec((1, width), lambda d, g: (0, 0)),
            pl.BlockSpec((None, width, 2 * width), lambda d, g: (d, 0, 0)),
            pl.BlockSpec((None, 1, 2 * width), lambda d, g: (d, 0, 0)),
            pl.BlockSpec((None, 1, width), lambda d, g: (d, 0, 0)),
        ],
        out_specs=pl.BlockSpec((None, ts, bsz, width), lambda d, g: (d, tile(d, g), 0, 0)),
        out_shape=jax.ShapeDtypeStruct((2, l, bsz, width), F32),
        scratch_shapes=[
            pltpu.VMEM((ts + CONV_W - 1, bsz, width), F32),
            pltpu.VMEM((ts, bsz, width), F32),
            pltpu.VMEM((ts, bsz, width), F32),
            pltpu.VMEM((bsz, width), F32),
        ],
        compiler_params=_cparams(("arbitrary", "arbitrary")),
        name="rglru_scan",
    )(xa_tm, xa_tm, xa_tm, conv_w.reshape(CONV_W, 1, width), conv_b.reshape(1, width),
      w_gates, b_gates, lam)


def _block_diag(w):
    nb, c, dd = w.shape
    eye = jnp.eye(nb, dtype=w.dtype)
    return (eye[:, None, :, None] * w[:, :, None, :]).reshape(nb * c, nb * dd)


def _win_attn_kernel(sink_ref, q_ref, k_ref, v_ref, o_ref, *, c_len, l_len, nqc):
    j = pl.program_id(1)
    blk = q_ref.shape[0]
    grp = WIN_HEADS // WIN_KV_HEADS
    band = 3 * blk
    heads = [(h, slice(h * HEAD_DIM, (h + 1) * HEAD_DIM),
              slice((h // grp) * HEAD_DIM, (h // grp + 1) * HEAD_DIM)) for h in range(WIN_HEADS)]

    @pl.when(j < nqc)
    def _():
        logits = [_nt_dot(q_ref[:, hsl], k_ref[0:c_len, ksl]) for _, hsl, ksl in heads]
        probs, dens = [], []
        for (h, _, _), s in zip(heads, logits):
            sink = sink_ref[h] * LOG2E
            m = jnp.maximum(jnp.max(s, axis=-1, keepdims=True), sink)
            p = jnp.exp2(s - m)
            dens.append(jnp.sum(p, axis=-1, keepdims=True) + jnp.exp2(sink - m))
            probs.append(p.astype(BF16))
        outs = [jnp.dot(p, v_ref[0:c_len, ksl], preferred_element_type=F32) / den
                for (_, _, ksl), p, den in zip(heads, probs, dens)]
        o_ref[...] = jnp.concatenate(outs, axis=-1).astype(o_ref.dtype)

    @pl.when(j >= nqc)
    def _():
        jb = j - nqc
        start = jnp.clip(c_len + (jb - 1) * blk, c_len - blk, l_len - band)
        start = pl.multiple_of(start, blk)
        qpos = jb * blk + lax.broadcasted_iota(jnp.int32, (blk, band), 0)
        kpos = start - c_len + lax.broadcasted_iota(jnp.int32, (blk, band), 1)
        valid = (jnp.abs(qpos - kpos) <= WINDOW) & (kpos >= 0)
        lc = [_nt_dot(q_ref[:, hsl], k_ref[0:c_len, ksl]) for _, hsl, ksl in heads]
        lb = [jnp.where(valid, _nt_dot(q_ref[:, hsl], k_ref[pl.ds(start, band), ksl]), NEG_BIG)
              for _, hsl, ksl in heads]
        pcs, pbs, dens = [], [], []
        for (h, _, _), sc, sb in zip(heads, lc, lb):
            sink = sink_ref[h] * LOG2E
            m = jnp.maximum(jnp.maximum(jnp.max(sc, axis=-1, keepdims=True),
                                        jnp.max(sb, axis=-1, keepdims=True)), sink)
            pc = jnp.exp2(sc - m)
            pb = jnp.exp2(sb - m)
            dens.append(jnp.sum(pc, axis=-1, keepdims=True) + jnp.sum(pb, axis=-1, keepdims=True)
                        + jnp.exp2(sink - m))
            pcs.append(pc.astype(BF16))
            pbs.append(pb.astype(BF16))
        outs = [(jnp.dot(pc, v_ref[0:c_len, ksl], preferred_element_type=F32)
                 + jnp.dot(pb, v_ref[pl.ds(start, band), ksl], preferred_element_type=F32)) / den
                for (_, _, ksl), pc, pb, den in zip(heads, pcs, pbs, dens)]
        o_ref[...] = jnp.concatenate(outs, axis=-1).astype(o_ref.dtype)


def _win_attention(q, k, v, sink, c_len):
    b, l, qw = q.shape
    kw = k.shape[-1]
    blk = TIME_TILE
    kern = functools.partial(_win_attn_kernel, c_len=c_len, l_len=l, nqc=c_len // blk)
    return pl.pallas_call(
        kern,
        grid=(b, l // blk),
        in_specs=[
            pl.BlockSpec(memory_space=pltpu.SMEM),
            pl.BlockSpec((None, blk, qw), lambda bb, j: (bb, j, 0)),
            pl.BlockSpec((None, l, kw), lambda bb, j: (bb, 0, 0)),
            pl.BlockSpec((None, l, kw), lambda bb, j: (bb, 0, 0)),
        ],
        out_specs=pl.BlockSpec((None, blk, qw), lambda bb, j: (bb, j, 0)),
        out_shape=jax.ShapeDtypeStruct((b, l, qw), BF16),
        compiler_params=_cparams(("arbitrary", "arbitrary")),
        name="window_attention",
    )(sink, q, k, v)


def _out_even_kernel(x_ref, rec_ref, gate_ref, att_ref, wa_ref, wb_ref, g1_ref, *refs):
    route_in, o_ref, route_out = refs[:5], refs[5], refs[6:]
    lru = (rec_ref[0] + rec_ref[1]) * jax.nn.gelu(gate_ref[...].astype(F32))
    y = (jnp.dot(lru.astype(BF16), wa_ref[...], preferred_element_type=F32)
         + jnp.dot(att_ref[...], wb_ref[...], preferred_element_type=F32))
    x = x_ref[...] + g1_ref[...] * y
    o_ref[...] = x
    _route(x, *route_in, *route_out)


def _out_even(xc, rec2, gate, att, w_a, w_b, mods3, layer, nct, g_ffn, w_router, b_router):
    b, l, d = xc.shape
    tm = ROW_TILE
    w = gate.shape[-1]
    row = lambda bb, i: jnp.where(i < nct, SUBLANES, bb)
    r_in, r_args, r_out, r_shapes, r_scratch = _route_plumbing(b, l, d, g_ffn, mods3, layer, row, w_router, b_router)
    return pl.pallas_call(
        _out_even_kernel,
        grid=(b, l // tm),
        in_specs=[
            pl.BlockSpec((None, tm, d), lambda bb, i: (bb, i, 0)),
            pl.BlockSpec((2, tm, w), lambda bb, i: (0, i, bb)),
            pl.BlockSpec((None, tm, w), lambda bb, i: (bb, i, 0)),
            pl.BlockSpec((None, tm, att.shape[-1]), lambda bb, i: (bb, i, 0)),
            pl.BlockSpec(w_a.shape, lambda bb, i: (0, 0)),
            pl.BlockSpec(w_b.shape, lambda bb, i: (0, 0)),
            _mod_spec(d, layer, 2, row),
        ] + r_in,
        out_specs=[pl.BlockSpec((None, tm, d), lambda bb, i: (bb, i, 0))] + r_out,
        out_shape=[jax.ShapeDtypeStruct((b, l, d), F32)] + r_shapes,
        scratch_shapes=r_scratch,
        compiler_params=_cparams(("arbitrary", "arbitrary")),
        name="out_proj_even",
    )(xc, rec2, gate, att, w_a, w_b, mods3, *r_args)


def _diff_attn_kernel(lam_ref, g_ref, q_ref, k_ref, v_ref, o_ref, *, lam_init):
    lv = lam_ref[...]
    lam = (jnp.exp(jnp.sum(lv[0:1] * lv[1:2], axis=-1, keepdims=True))
           - jnp.exp(jnp.sum(lv[2:3] * lv[3:4], axis=-1, keepdims=True)) + lam_init)
    vw = 2 * DIFF_DH

    def logits(h, mp):
        lo = h * vw + mp * DIFF_DH
        return _nt_dot(q_ref[:, lo:lo + DIFF_DH], k_ref[:, lo:lo + DIFF_DH])

    def softmax_parts(s):
        e = jnp.exp2(s - jnp.max(s, axis=-1, keepdims=True))
        return e, 1.0 / jnp.sum(e, axis=-1, keepdims=True)

    heads = range(DIFF_HEADS)
    ls = [(logits(h, 0), logits(h, 1)) for h in heads]
    ws = []
    for l0, l1 in ls:
        e0, r0 = softmax_parts(l0)
        e1, r1 = softmax_parts(l1)
        ws.append(((e0 - e1 * (lam * r1 / r0)).astype(BF16), r0))
    for h, (w, r0) in zip(heads, ws):
        lo = h * vw
        o = jnp.dot(w, v_ref[:, lo:lo + vw], preferred_element_type=F32) * r0
        ms = jnp.mean(o * o, axis=-1, keepdims=True)
        o = o * lax.rsqrt(ms + NORM_EPS) * g_ref[...]
        o_ref[:, lo:lo + vw] = (o * (1.0 - lam_init)).astype(o_ref.dtype)


def _diff_attention(q, k, v, lam_vecs, subln_g, lam_init, c_len):
    b, l, w = q.shape
    tq = ROW_TILE
    s_len = l - c_len
    off = c_len // tq
    return pl.pallas_call(
        functools.partial(_diff_attn_kernel, lam_init=lam_init),
        grid=(b, s_len // tq),
        in_specs=[
            pl.BlockSpec(lam_vecs.shape, lambda bb, j: (0, 0)),
            pl.BlockSpec((1, 2 * DIFF_DH), lambda bb, j: (0, 0)),
            pl.BlockSpec((None, tq, w), lambda bb, j: (bb, j + off, 0)),
            pl.BlockSpec((None, l, w), lambda bb, j: (bb, 0, 0)),
            pl.BlockSpec((None, l, w), lambda bb, j: (bb, 0, 0)),
        ],
        out_specs=pl.BlockSpec((None, tq, w), lambda bb, j: (bb, j, 0)),
        out_shape=jax.ShapeDtypeStruct((b, s_len, w), BF16),
        compiler_params=_cparams(("arbitrary", "arbitrary")),
        name="diff_attention",
    )(lam_vecs, subln_g.reshape(1, -1), q, k, v)


def _ssd_kernel(x_ref, xp_ref, xn_ref, dt_ref, cw_ref, cb_ref, dtb_ref, alog_ref, dsk_ref, o_ref,
                ext_scr, st_scr, *, q, nct, nt):
    d = pl.program_id(0)
    g = pl.program_id(2)
    tile = _seq_tile(d, g, nct, nt)
    pv = jnp.where((tile == 0) | (tile == nct), 0.0, 1.0)
    nv = jnp.where((tile == nct - 1) | (tile == nt - 1), 0.0, 1.0)
    ext_scr[0:SUBLANES] = xp_ref[...] * pv
    ext_scr[SUBLANES:SUBLANES + q] = x_ref[...]
    ext_scr[SUBLANES + q:2 * SUBLANES + q] = xn_ref[...] * nv

    @pl.when(g == 0)
    def _():
        st_scr[...] = jnp.zeros_like(st_scr)

    u = cb_ref[...] + cw_ref[0] * ext_scr[SUBLANES - 1:SUBLANES - 1 + q, :]
    for j in range(1, CONV_W):
        u = u + cw_ref[j] * ext_scr[SUBLANES - 1 + j:SUBLANES - 1 + j + q, :]
    act = _silu(u)

    dtr = dt_ref[...]
    dtr = jnp.where(d == 0, dtr, pltpu.roll(dtr, LANES - SSD_HEADS, 1))
    dtv = _softplus(dtr + dtb_ref[...])
    head_lane = lax.broadcasted_iota(jnp.int32, (1, LANES), 1) < SSD_HEADS
    dta = dtv * jnp.where(head_lane, -jnp.exp(alog_ref[...]) * LOG2E, 0.0)
    ri = lax.broadcasted_iota(jnp.int32, (q, q), 0)
    ci = lax.broadcasted_iota(jnp.int32, (q, q), 1)
    keep = jnp.where(d == 0, ri - ci, ci - ri) >= 0
    keep_b = jnp.where(keep, 1.0, 0.0).astype(BF16)
    cum = jnp.zeros((q, LANES), F32)
    rest = dta
    for _ in range(3):
        part = rest.astype(BF16)
        cum = cum + jnp.dot(keep_b, part, preferred_element_type=F32)
        rest = rest - part.astype(F32)
    tot = jnp.sum(dta, axis=0, keepdims=True)
    cum_t = cum.T
    dt_t = dtv.T
    to_end = jnp.exp2(tot - cum) * dtv
    e_cum = jnp.exp2(cum)
    e_tot = jnp.exp2(tot)
    dskip = dsk_ref[...] * jnp.where(d == 0, 1.0, 0.0)

    def spread(v, width):
        rows = lax.broadcasted_iota(jnp.int32, (LANES, SSD_HEADS * width), 0)
        cols = lax.broadcasted_iota(jnp.int32, (LANES, SSD_HEADS * width), 1)
        pick = jnp.where(rows * width <= cols, jnp.where(cols < (rows + 1) * width, 1.0, 0.0), 0.0).astype(BF16)
        out = jnp.zeros((q, SSD_HEADS * width), F32)
        rest = v
        for _ in range(3):
            part = rest.astype(BF16)
            out = out + jnp.dot(part, pick, preferred_element_type=F32)
            rest = rest - part.astype(F32)
        return out

    e_cum_b = spread(e_cum, SSD_HEAD_DIM)
    to_end_b = spread(to_end, SSD_HEAD_DIM)

    hpg = SSD_HEADS // SSD_GROUPS
    for gi in range(SSD_GROUPS):
        b_g = act[:, SSD_INNER + gi * SSD_STATE:SSD_INNER + (gi + 1) * SSD_STATE]
        c_lo = SSD_INNER + SSD_GROUPS * SSD_STATE + gi * SSD_STATE
        c_g = act[:, c_lo:c_lo + SSD_STATE].astype(BF16)
        cb = _nt_dot(c_g, b_g.astype(BF16))
        b_gt = b_g.T.astype(BF16)
        for hh in range(hpg):
            h = gi * hpg + hh
            xs = act[:, h * SSD_HEAD_DIM:(h + 1) * SSD_HEAD_DIM]
            seg = cum[:, h:h + 1] - cum_t[h:h + 1, :]
            decay = jnp.exp2(jnp.where(keep, seg, NEG_BIG))
            w = (cb * decay * dt_t[h:h + 1, :]).astype(BF16)
            state = st_scr[h]
            y = jnp.dot(w, xs.astype(BF16), preferred_element_type=F32)
            y = y + (jnp.dot(c_g, state.astype(BF16), preferred_element_type=F32)
                     * e_cum_b[:, h * SSD_HEAD_DIM:(h + 1) * SSD_HEAD_DIM])
            y = y + dskip[:, h * SSD_HEAD_DIM:(h + 1) * SSD_HEAD_DIM] * xs
            o_ref[:, h * SSD_HEAD_DIM:(h + 1) * SSD_HEAD_DIM] = y
            s_new = jnp.dot(b_gt, (xs * to_end_b[:, h * SSD_HEAD_DIM:(h + 1) * SSD_HEAD_DIM]).astype(BF16),
                            preferred_element_type=F32)
            st_scr[h] = e_tot[:, h:h + 1] * state + s_new


def _ssd(xbc, dt, conv_w, conv_b, dt_bias, a_log, d_skip, nct):
    b, l, cd = xbc.shape
    q = TIME_TILE
    nt = l // q
    tile = lambda d, bb, g: _seq_tile(d, g, nct, nt)
    r8 = q // SUBLANES
    pad = LANES - SSD_HEADS
    dtb = jnp.pad(dt_bias, ((0, 0), (0, pad))).reshape(2, 1, LANES)
    alog = jnp.pad(a_log, ((0, 0), (0, pad))).reshape(2, 1, LANES)
    dsk = jnp.repeat(d_skip, SSD_HEAD_DIM).reshape(1, SSD_INNER)
    return pl.pallas_call(
        functools.partial(_ssd_kernel, q=q, nct=nct, nt=nt),
        grid=(2, b, nt),
        in_specs=[
            pl.BlockSpec((None, q, cd), lambda d, bb, g: (bb, tile(d, bb, g), 0)),
            pl.BlockSpec((None, SUBLANES, cd),
                         lambda d, bb, g: (bb, jnp.maximum(tile(d, bb, g) * r8 - 1, 0), 0)),
            pl.BlockSpec((None, SUBLANES, cd),
                         lambda d, bb, g: (bb, jnp.minimum((tile(d, bb, g) + 1) * r8, l // SUBLANES - 1), 0)),
            pl.BlockSpec((None, q, LANES), lambda d, bb, g: (bb, tile(d, bb, g), 0)),
            pl.BlockSpec((CONV_W, 1, cd), lambda d, bb, g: (0, 0, 0)),
            pl.BlockSpec((1, cd), lambda d, bb, g: (0, 0)),
            pl.BlockSpec((None, 1, LANES), lambda d, bb, g: (d, 0, 0)),
            pl.BlockSpec((None, 1, LANES), lambda d, bb, g: (d, 0, 0)),
            pl.BlockSpec((1, SSD_INNER), lambda d, bb, g: (0, 0)),
        ],
        out_specs=pl.BlockSpec((None, None, q, SSD_INNER), lambda d, bb, g: (d, bb, tile(d, bb, g), 0)),
        out_shape=jax.ShapeDtypeStruct((2, b, l, SSD_INNER), F32),
        scratch_shapes=[
            pltpu.VMEM((q + 2 * SUBLANES, cd), F32),
            pltpu.VMEM((SSD_HEADS, SSD_STATE, SSD_HEAD_DIM), F32),
        ],
        compiler_params=_cparams(("arbitrary", "arbitrary", "arbitrary")),
        name="ssd_chunked",
    )(xbc, xbc, xbc, dt, conv_w.reshape(CONV_W, 1, cd), conv_b.reshape(1, cd), dtb, alog, dsk)


def _out_odd_kernel(x_ref, diff_ref, y_ref, z_ref, ng_ref, wa_ref, wb_ref, g1_ref, *refs):
    route_in, o_ref, route_out = refs[:5], refs[5], refs[6:]
    yz = (y_ref[0] + y_ref[1]) * _silu(z_ref[...].astype(F32))
    gs = SSD_INNER // SSD_GROUPS
    parts = []
    for gi in range(SSD_GROUPS):
        seg = yz[:, gi * gs:(gi + 1) * gs]
        ms = jnp.mean(seg * seg, axis=-1, keepdims=True)
        parts.append(seg * lax.rsqrt(ms + NORM_EPS) * ng_ref[:, gi * gs:(gi + 1) * gs])
    ssd = jnp.concatenate(parts, axis=-1).astype(BF16)
    y = (jnp.dot(diff_ref[...], wa_ref[...], preferred_element_type=F32)
         + jnp.dot(ssd, wb_ref[...], preferred_element_type=F32))
    x = x_ref[...] + g1_ref[...] * y
    o_ref[...] = x
    _route(x, *route_in, *route_out)


def _out_odd(xc, diff, y2, z, norm_g, w_a, w_b, mods3, layer, c_len, g_ffn, w_router, b_router):
    b, l, d = xc.shape
    s_len = l - c_len
    tm = ROW_TILE
    off = c_len // tm
    w = SSD_INNER
    row = lambda bb, i: bb
    r_in, r_args, r_out, r_shapes, r_scratch = _route_plumbing(b, s_len, d, g_ffn, mods3, layer, row, w_router,
                                                               b_router)
    return pl.pallas_call(
        _out_odd_kernel,
        grid=(b, s_len // tm),
        in_specs=[
            pl.BlockSpec((None, tm, d), lambda bb, i: (bb, i + off, 0)),
            pl.BlockSpec((None, tm, diff.shape[-1]), lambda bb, i: (bb, i, 0)),
            pl.BlockSpec((2, None, tm, w), lambda bb, i: (0, bb, i + off, 0)),
            pl.BlockSpec((None, tm, w), lambda bb, i: (bb, i + off, 0)),
            pl.BlockSpec((1, w), lambda bb, i: (0, 0)),
            pl.BlockSpec(w_a.shape, lambda bb, i: (0, 0)),
            pl.BlockSpec(w_b.shape, lambda bb, i: (0, 0)),
            _mod_spec(d, layer, 2, row),
        ] + r_in,
        out_specs=[pl.BlockSpec((None, tm, d), lambda bb, i: (bb, i, 0))] + r_out,
        out_shape=[jax.ShapeDtypeStruct((b, s_len, d), F32)] + r_shapes,
        scratch_shapes=r_scratch,
        compiler_params=_cparams(("arbitrary", "arbitrary")),
        name="out_proj_odd",
    )(xc, diff, y2, z, norm_g.reshape(1, w), w_a, w_b, mods3, *r_args)


def _route(x, g_ref, sh_ref, sc_ref, wr_ref, br_ref, h_ref, eid_ref, rnk_ref, gate_ref, cnt_ref, carry_scr):
    @pl.when((pl.program_id(0) == 0) & (pl.program_id(1) == 0))
    def _():
        carry_scr[...] = jnp.zeros_like(carry_scr)

    h = _norm_mod(x, g_ref[...], sh_ref[...], sc_ref[...])
    _store_chunk_rows(h_ref, _pack_bf16(h))
    tm = h.shape[0]
    per = N_EXPERTS // N_EXPERT_GROUPS
    h_hi = h.astype(BF16)
    h_lo = (h - h_hi.astype(F32)).astype(BF16)
    logits = _nt_dot(wr_ref[0], h_hi) + _nt_dot(wr_ref[0], h_lo) + _nt_dot(wr_ref[1], h_hi)
    scores = jax.nn.sigmoid(logits)
    sel = scores + br_ref[...]
    sel3 = sel.reshape(N_EXPERT_GROUPS, per, tm)
    kio = lax.broadcasted_iota(jnp.int32, sel3.shape, 1)
    m1 = jnp.max(sel3, axis=1, keepdims=True)
    first = jnp.min(jnp.where(sel3 == m1, kio, per), axis=1, keepdims=True)
    m2 = jnp.max(jnp.where(kio == first, NEG_BIG, sel3), axis=1, keepdims=True)
    gs = m1 + m2
    gio = lax.broadcasted_iota(jnp.int32, gs.shape, 0)
    ahead = jnp.zeros(gs.shape, jnp.int32)
    for gp in range(N_EXPERT_GROUPS):
        other = gs[gp:gp + 1]
        ahead = ahead + jnp.where((other > gs) | ((other == gs) & (gp < gio)), 1, 0)
    grp_on = jnp.where(ahead < TOPK_GROUPS, 1.0, 0.0)
    selm = jnp.where(jnp.broadcast_to(grp_on, sel3.shape) > 0.5, sel3, NEG_BIG).reshape(N_EXPERTS, tm)
    eio = lax.broadcasted_iota(jnp.int32, selm.shape, 0)
    work = selm
    cf = jnp.zeros(selm.shape, F32)
    e_rows, s_rows = [], []
    for k in range(TOP_K):
        best = jnp.max(work, axis=0, keepdims=True)
        idx = jnp.min(jnp.where(work == best, eio, N_EXPERTS), axis=0, keepdims=True)
        hit = eio == idx
        cf = cf + jnp.where(hit, 1.0, 0.0)
        work = jnp.where(hit, NEG_BIG, work)
        e_rows.append(idx)
        s_rows.append(jnp.sum(jnp.where(hit, scores, 0.0), axis=0, keepdims=True))
    denom = s_rows[0]
    for s_k in s_rows[1:]:
        denom = denom + s_k
    g_rows = [s_k / denom * ROUTED_SCALE for s_k in s_rows]
    ti = lax.broadcasted_iota(jnp.int32, (tm, tm), 0)
    tj = lax.broadcasted_iota(jnp.int32, (tm, tm), 1)
    before = jnp.where(ti < tj, 1.0, 0.0).astype(BF16)
    in_expert = carry_scr[:, 0:1] + jnp.dot(cf.astype(BF16), before, preferred_element_type=F32)
    carry_scr[...] = carry_scr[...] + jnp.sum(cf, axis=1, keepdims=True)
    cnt_ref[...] = carry_scr[...]
    r_rows = [jnp.sum(jnp.where(eio == idx, in_expert, 0.0), axis=0, keepdims=True) for idx in e_rows]
    eid_ref[...] = jnp.concatenate(e_rows, axis=0)
    rnk_ref[...] = jnp.concatenate(r_rows, axis=0).astype(jnp.int32)
    padded = jnp.concatenate(g_rows + [jnp.zeros((LANES - TOP_K, tm), F32)], axis=0)
    gate_ref[...] = padded.T


def _route_plumbing(b, r, d, g, mods3, layer, row_fn, w_router, b_router):
    tm = ROW_TILE
    nt = r // tm
    w_t = w_router.T
    w_hi = w_t.astype(BF16)
    w_router_t = jnp.stack([w_hi, (w_t - w_hi.astype(F32)).astype(BF16)])
    slot = pl.BlockSpec((TOP_K, tm), lambda bb, i: (0, bb * nt + i))
    slot_shape = jax.ShapeDtypeStruct((TOP_K, b * r), jnp.int32)
    in_specs = [
        pl.BlockSpec((1, d), lambda bb, i: (0, 0)),
        _mod_spec(d, layer, 3, row_fn),
        _mod_spec(d, layer, 4, row_fn),
        pl.BlockSpec(w_router_t.shape, lambda bb, i: (0, 0, 0)),
        pl.BlockSpec((N_EXPERTS, 1), lambda bb, i: (0, 0)),
    ]
    args = [g.reshape(1, d), mods3, mods3, w_router_t, b_router.reshape(N_EXPERTS, 1)]
    out_specs = [
        pl.BlockSpec((tm * ROW_CHUNKS, LANES), lambda bb, i: (bb * nt + i, 0)),
        slot,
        slot,
        pl.BlockSpec((None, tm, LANES), lambda bb, i: (bb, i, 0)),
        pl.BlockSpec((N_EXPERTS, LANES), lambda bb, i: (0, 0)),
    ]
    out_shapes = [jax.ShapeDtypeStruct((b * r * ROW_CHUNKS, LANES), jnp.uint32), slot_shape, slot_shape,
                  jax.ShapeDtypeStruct((b, r, LANES), F32), jax.ShapeDtypeStruct((N_EXPERTS, LANES), F32)]
    scratch = [pltpu.VMEM((N_EXPERTS, LANES), F32)]
    return in_specs, args, out_specs, out_shapes, scratch


def _moe_plan(counts, n_rows):
    blk = EXPERT_BLK
    nb = n_rows // blk
    ends = jnp.cumsum(counts)
    starts = ends - counts
    count_le = lambda sorted_vals, q: jnp.sum(sorted_vals[None, :] <= q[:, None], axis=1, dtype=jnp.int32)
    first = jnp.arange(nb, dtype=jnp.int32) * blk
    e_lo = count_le(ends, first)
    e_hi = count_le(ends, first + (blk - 1))
    n_pair = e_hi - e_lo + 1
    p_end = jnp.cumsum(n_pair)
    p_start = p_end - n_pair
    i = jnp.arange(nb + N_EXPERTS - 1, dtype=jnp.int32)
    j = jnp.minimum(count_le(p_end, i), nb - 1)
    valid = i < p_end[-1]
    e = jnp.where(valid, e_lo[j] + i - p_start[j], e_hi[nb - 1]).astype(jnp.int32)
    bounds = jnp.concatenate([starts, ends[-1:]]).astype(jnp.int32)
    return j, e, valid.astype(jnp.int32), bounds


def _positions_kernel(starts_ref, eid_ref, rnk_ref, pos_ref):
    eid = eid_ref[...]
    pos = rnk_ref[...]
    for e in range(N_EXPERTS):
        pos = pos + jnp.where(eid == e, starts_ref[e], 0)
    pos_ref[...] = pos * ROW_CHUNKS


def _positions(eid, rnk, starts):
    full = pl.BlockSpec(eid.shape, lambda: (0, 0))
    return pl.pallas_call(
        _positions_kernel,
        in_specs=[pl.BlockSpec(memory_space=pltpu.SMEM), full, full],
        out_specs=full,
        out_shape=jax.ShapeDtypeStruct(eid.shape, jnp.int32),
        compiler_params=pltpu.CompilerParams(vmem_limit_bytes=VMEM_LIMIT),
        name="moe_positions",
    )(starts, eid, rnk)


def _token_row(ref, first):
    return ref.at[pl.ds(pl.multiple_of(first, ROW_CHUNKS), ROW_CHUNKS)]


def _dispatch_kernel(pos_ref, h_ref, xs_ref, sem):
    tm = h_ref.shape[0] // ROW_CHUNKS

    def issue(t, carry):
        src = _token_row(h_ref, t * ROW_CHUNKS)
        for k in range(TOP_K):
            pltpu.make_async_copy(src, _token_row(xs_ref, pos_ref[k, t]), sem).start(priority=k % 2)
        return carry

    lax.fori_loop(0, tm, issue, 0)
    done = pl.ds(0, tm * ROW_CHUNKS)
    for _ in range(TOP_K):
        pltpu.make_async_copy(h_ref.at[done], xs_ref.at[done], sem).wait()


def _dispatch(h2, pos):
    rows, w = h2.shape
    tm = math.gcd(DISPATCH_TILE, rows // ROW_CHUNKS)
    return pl.pallas_call(
        _dispatch_kernel,
        grid=(rows // (tm * ROW_CHUNKS),),
        in_specs=[
            pl.BlockSpec((TOP_K, tm), lambda i: (0, i), memory_space=pltpu.SMEM),
            pl.BlockSpec((tm * ROW_CHUNKS, w), lambda i: (i, 0)),
        ],
        out_specs=pl.BlockSpec(memory_space=pl.ANY),
        out_shape=jax.ShapeDtypeStruct((rows * TOP_K, w), h2.dtype),
        scratch_shapes=[pltpu.SemaphoreType.DMA],
        compiler_params=_cparams(("arbitrary",)),
        name="moe_dispatch",
    )(pos, h2)


def _grouped_kernel(pb_ref, pe_ref, pv_ref, bnd_ref, xs_ref, wg_ref, wu_ref, wd_ref, y_ref, wgb, wub, wdb):
    i = pl.program_id(0)
    prev = jnp.maximum(i - 1, 0)
    j = pb_ref[i]
    e = pe_ref[i]
    blk = xs_ref.shape[0] // ROW_CHUNKS

    @pl.when((i == 0) | (pb_ref[prev] != j))
    def _():
        y_ref[...] = jnp.zeros_like(y_ref)

    @pl.when((i == 0) | (pe_ref[prev] != e))
    def _():
        wgb[...] = wg_ref[...].astype(BF16)
        wub[...] = wu_ref[...].astype(BF16)
        wdb[...] = wd_ref[...].astype(BF16)

    @pl.when(pv_ref[i] == 1)
    def _():
        xw = _load_chunk_rows(xs_ref, blk)
        a = _packed_dot(xw, wgb)
        u = _packed_dot(xw, wub)
        yv = jnp.dot((_silu(a) * u).astype(BF16), wdb[...], preferred_element_type=F32)
        rows = j * blk + lax.broadcasted_iota(jnp.int32, (blk, 1), 0)
        own = (rows >= bnd_ref[e]) & (rows < bnd_ref[e + 1])
        yw = _pack_bf16(yv)
        for c in range(ROW_CHUNKS):
            sl = pl.ds(c, blk, stride=ROW_CHUNKS)
            y_ref[sl, :] = jnp.where(own, yw[:, c * LANES:(c + 1) * LANES], y_ref[sl, :])


def _grouped(pb, pe, pv, bounds, xs, wg, wu, wd, layer):
    p, half = xs.shape
    d = 2 * ROW_CHUNKS * LANES
    blk = EXPERT_BLK * ROW_CHUNKS
    grid_spec = pltpu.PrefetchScalarGridSpec(
        num_scalar_prefetch=4,
        grid=(pb.shape[0],),
        in_specs=[
            pl.BlockSpec((blk, half), lambda i, pb, pe, pv, bnd: (pb[i], 0)),
            pl.BlockSpec((None, None, d, D_EXPERT), lambda i, pb, pe, pv, bnd: (layer, pe[i], 0, 0)),
            pl.BlockSpec((None, None, d, D_EXPERT), lambda i, pb, pe, pv, bnd: (layer, pe[i], 0, 0)),
            pl.BlockSpec((None, None, D_EXPERT, d), lambda i, pb, pe, pv, bnd: (layer, pe[i], 0, 0)),
        ],
        out_specs=pl.BlockSpec((blk, half), lambda i, pb, pe, pv, bnd: (pb[i], 0)),
        scratch_shapes=[
            pltpu.VMEM((d, D_EXPERT), BF16),
            pltpu.VMEM((d, D_EXPERT), BF16),
            pltpu.VMEM((D_EXPERT, d), BF16),
        ],
    )
    return pl.pallas_call(
        _grouped_kernel,
        grid_spec=grid_spec,
        out_shape=jax.ShapeDtypeStruct((p, half), jnp.uint32),
        compiler_params=_cparams(("arbitrary",)),
        name="moe_grouped_experts",
    )(pb, pe, pv, bounds, xs, wg, wu, wd)


def _combine_kernel(*refs, final, c_len):
    pos_ref, y_ref, gate_ref, h_ref, x_ref, g2c_ref, g2l_ref, sg_ref, su_ref, sd_ref = refs[:10]
    o_ref, buf, sem = refs[-3:]
    tm = x_ref.shape[0]

    def issue(t, carry):
        for k in range(TOP_K):
            pltpu.make_async_copy(_token_row(y_ref, pos_ref[k, t]), _token_row(buf.at[k], t * ROW_CHUNKS),
                                  sem).start(priority=k % 2)
        return carry

    lax.fori_loop(0, tm, issue, 0)
    hw = _load_chunk_rows(h_ref, tm)
    a = _packed_dot(hw, sg_ref)
    u = _packed_dot(hw, su_ref)
    acc = jnp.dot((_silu(a) * u).astype(BF16), sd_ref[...], preferred_element_type=F32)
    done = pl.ds(0, tm * ROW_CHUNKS)
    for k in range(TOP_K):
        pltpu.make_async_copy(y_ref.at[done], buf.at[k, done], sem).wait()
    g = gate_ref[...]
    half = hw.shape[-1]
    acc_hi = acc[:, :half]
    acc_lo = acc[:, half:]
    for k in range(TOP_K):
        hi, lo = _unpack_bf16(_load_chunk_rows(buf.at[k], tm))
        acc_hi = acc_hi + g[:, k:k + 1] * hi
        acc_lo = acc_lo + g[:, k:k + 1] * lo
    is_ctx = pl.program_id(1) * tm + lax.broadcasted_iota(jnp.int32, (tm, 1), 0) < c_len
    g2 = jnp.where(is_ctx, g2c_ref[...], g2l_ref[...])
    x = x_ref[...] + g2 * jnp.concatenate([acc_hi, acc_lo], axis=-1)
    if final:
        gf_ref = refs[10]
        ms = jnp.mean(x * x, axis=-1, keepdims=True)
        x = x * lax.rsqrt(ms + NORM_EPS) * gf_ref[...]
    o_ref[...] = x


def _combine(pos, y, gates, h2, x, mods3, layer, c_len, tm, sg, su, sd, g_final=None):
    b, r, d = x.shape
    nt = r // tm
    tile = pl.BlockSpec((None, tm, d), lambda bb, i: (bb, i, 0))
    in_specs = [
        pl.BlockSpec((TOP_K, tm), lambda bb, i: (0, bb * nt + i), memory_space=pltpu.SMEM),
        pl.BlockSpec(memory_space=pl.ANY),
        pl.BlockSpec((None, tm, LANES), lambda bb, i: (bb, i, 0)),
        pl.BlockSpec((tm * ROW_CHUNKS, LANES), lambda bb, i: (bb * nt + i, 0)),
        tile,
        _mod_spec(d, layer, 5, lambda bb, i: SUBLANES),
        _mod_spec(d, layer, 5, lambda bb, i: bb),
        pl.BlockSpec(sg.shape, lambda bb, i: (0, 0)),
        pl.BlockSpec(su.shape, lambda bb, i: (0, 0)),
        pl.BlockSpec(sd.shape, lambda bb, i: (0, 0)),
    ]
    args = [pos, y, gates, h2, x, mods3, mods3, sg, su, sd]
    if g_final is not None:
        in_specs.append(pl.BlockSpec((1, d), lambda bb, i: (0, 0)))
        args.append(g_final.reshape(1, d))
    return pl.pallas_call(
        functools.partial(_combine_kernel, final=g_final is not None, c_len=c_len),
        grid=(b, nt),
        in_specs=in_specs,
        out_specs=tile,
        out_shape=jax.ShapeDtypeStruct((b, r, d), F32),
        scratch_shapes=[pltpu.VMEM((TOP_K, tm * ROW_CHUNKS, LANES), jnp.uint32), pltpu.SemaphoreType.DMA],
        compiler_params=_cparams(("arbitrary", "arbitrary")),
        name="moe_combine",
    )(*args)


def _moe(x, routed, mods3, layer, c_len, combine_tile, w_e_gate, w_e_up, w_e_down, ws_gate, ws_up, ws_down,
         g_final=None):
    b, r, d = x.shape
    h2, eid, rnk, gates, cnt = routed
    pb, pe, pv, bounds = _moe_plan(cnt[:, 0].astype(jnp.int32), b * r * TOP_K)
    pos = _positions(eid, rnk, bounds[:N_EXPERTS])
    xs = _dispatch(h2, pos)
    y = _grouped(pb, pe, pv, bounds, xs, w_e_gate, w_e_up, w_e_down, layer)
    return _combine(pos, y, gates, h2, x, mods3, layer, c_len, combine_tile,
                    ws_gate.astype(BF16), ws_up.astype(BF16), ws_down.astype(BF16), g_final)


def kernel(x, c, ctx, c_ctx, w_mod, b_mod, g_mix, g_ffn, g_final, ab_w_in, ab_w_out, ab_conv_w, ab_conv_b, ab_w_r, ab_b_r, ab_w_i, ab_b_i, ab_lam, ab_sink, cd_w_in, cd_w_out, cd_lam, cd_subln_g, cd_conv_w, cd_conv_b, cd_dt_bias, cd_a_log, cd_d_skip, cd_norm_g, w_router, b_router, w_e_gate, w_e_up, w_e_down, ws_gate, ws_up, ws_down):
    bsz, s_len, d = x.shape
    c_len = ctx.shape[1]
    depth = w_mod.shape[0]
    assert depth == 2 and bsz == SUBLANES, "kernels are specialised to depth 2 and batch 8"
    assert c_len % ROW_TILE == 0 and s_len % ROW_TILE == 0
    nct_row = c_len // ROW_TILE
    nct_time = c_len // TIME_TILE

    c_all = jnp.concatenate([c, c_ctx[None], jnp.zeros((MOD_ROWS - bsz - 1, d), F32)], axis=0)
    mods3 = _modulations(c_all, w_mod, b_mod).reshape(depth * MOD_ROWS, 1, N_MOD * d)
    rope_tabs = _rope_tables(c_len, s_len)
    xc = jnp.concatenate([ctx, x], axis=1)

    w_in = ab_w_in[0].astype(BF16)
    q_hi = LRU_WIDTH + WIN_HEADS * HEAD_DIM
    x_hi = q_hi + LRU_WIDTH
    k_hi = x_hi + WIN_KV_HEADS * HEAD_DIM
    gate, q, xa, k, v = _project(xc, g_mix[0], mods3, 0, nct_row, rope_tabs, [
        (w_in[:, :LRU_WIDTH], None, BF16, False),
        (w_in[:, LRU_WIDTH:q_hi], HEAD_DIM ** -0.5 * LOG2E, BF16, False),
        (w_in[:, q_hi:x_hi], None, F32, True),
        (w_in[:, x_hi:k_hi], 1.0, BF16, False),
        (w_in[:, k_hi:], None, BF16, False),
    ])
    l_len = c_len + s_len
    w_gates = jnp.stack([jnp.concatenate([_block_diag(ab_w_r[0, dd]), _block_diag(ab_w_i[0, dd])], axis=1)
                         for dd in range(2)]).astype(BF16)
    b_gates = jnp.concatenate([ab_b_r[0], ab_b_i[0]], axis=-1).reshape(2, 1, 2 * LRU_WIDTH)
    rec = _rglru(xa.reshape(l_len, bsz, LRU_WIDTH), ab_conv_w[0], ab_conv_b[0], w_gates, b_gates,
                 ab_lam[0].reshape(2, 1, LRU_WIDTH), nct_time)
    att = _win_attention(q, k, v, ab_sink[0], c_len)
    w_out = ab_w_out[0].astype(BF16)
    xc, *routed = _out_even(xc, rec.reshape(2, l_len, bsz * LRU_WIDTH), gate, att, w_out[:LRU_WIDTH],
                            w_out[LRU_WIDTH:], mods3, 0, nct_row, g_ffn[0], w_router[0], b_router[0])
    xc = _moe(xc, routed, mods3, 0, c_len, math.gcd(MIXED_COMBINE_TILE, l_len), w_e_gate, w_e_up, w_e_down,
              ws_gate[0], ws_up[0], ws_down[0])

    w_in = cd_w_in[0].astype(BF16)
    qk = DIFF_HEADS * 2 * DIFF_DH
    z_hi = qk + SSD_INNER
    k_hi = z_hi + qk
    v_hi = k_hi + qk
    x_hi = v_hi + SSD_CONV_DIM
    w_dt = jnp.pad(w_in[:, x_hi:], ((0, 0), (0, LANES - 2 * SSD_HEADS)))
    q, z, k, v, xbc, dt = _project(xc, g_mix[1], mods3, 1, nct_row, rope_tabs, [
        (w_in[:, :qk], DIFF_DH ** -0.5 * LOG2E, BF16, False),
        (w_in[:, qk:z_hi], None, BF16, False),
        (w_in[:, z_hi:k_hi], 1.0, BF16, False),
        (w_in[:, k_hi:v_hi], None, BF16, False),
        (w_in[:, v_hi:x_hi], None, F32, False),
        (w_dt, None, F32, False),
    ])
    lam_init = 0.8 - 0.6 * math.exp(-0.3 * 1)
    diff = _diff_attention(q, k, v, cd_lam[0], cd_subln_g[0], lam_init, c_len)
    y2 = _ssd(xbc, dt, cd_conv_w[0], cd_conv_b[0], cd_dt_bias[0], cd_a_log[0], cd_d_skip[0], nct_time)
    w_out = cd_w_out[0].astype(BF16)
    xl, *routed = _out_odd(xc, diff, y2, z, cd_norm_g[0], w_out[:qk], w_out[qk:], mods3, 1, c_len,
                           g_ffn[1], w_router[1], b_router[1])
    return _moe(xl, routed, mods3, 1, 0, math.gcd(LATENT_COMBINE_TILE, s_len), w_e_gate, w_e_up, w_e_down,
                ws_gate[1], ws_up[1], ws_down[1], g_final=g_final)
```

```python
import functools
import math

import jax
import jax.numpy as jnp
from jax import lax
from jax.experimental import pallas as pl
from jax.experimental.pallas import tpu as pltpu

F32 = jnp.float32
BF16 = jnp.bfloat16
HIGHEST = lax.Precision.HIGHEST

GRID_W = 64
N_MOD = 6
NORM_EPS = 1e-6
ROPE_BASE = 10000.0
CONV_W = 4

LRU_WIDTH = 512
LRU_BLOCKS = 8
LRU_C = 8.0

HEAD_DIM = 64
WIN_HEADS = 8
WIN_KV_HEADS = 2
WINDOW = 128

DIFF_HEADS = 4
DIFF_DH = 64

SSD_HEADS = 8
SSD_HEAD_DIM = 64
SSD_INNER = SSD_HEADS * SSD_HEAD_DIM
SSD_GROUPS = 2
SSD_STATE = 128
SSD_CONV_DIM = SSD_INNER + 2 * SSD_GROUPS * SSD_STATE

N_EXPERTS = 64
N_EXPERT_GROUPS = 8
TOPK_GROUPS = 4
TOP_K = 8
D_EXPERT = 256
ROUTED_SCALE = 2.5

LANES = 128
SUBLANES = 8
MOD_ROWS = 16
TIME_TILE = 128
ROW_TILE = 256
DISPATCH_TILE = 2048
LATENT_COMBINE_TILE = 512
MIXED_COMBINE_TILE = 768
EXPERT_BLK = 1024
ROW_CHUNKS = 4
VMEM_LIMIT = 48 * 1024 * 1024
NEG_BIG = -1e30
LOG2E = math.log2(math.e)


def _cparams(sem):
    return pltpu.CompilerParams(dimension_semantics=sem, vmem_limit_bytes=VMEM_LIMIT)


def _nt_dot(a, b):
    return lax.dot_general(a, b, (((1,), (1,)), ((), ())), preferred_element_type=F32)


def _softplus(x):
    return jnp.maximum(x, 0.0) + jnp.log1p(jnp.exp(-jnp.abs(x)))


def _sigmoid(x):
    return 0.5 * jnp.tanh(0.5 * x) + 0.5


def _silu(x):
    return x * _sigmoid(x)


def _pack_bf16(x):
    half = x.shape[-1] // 2
    bits = pltpu.bitcast(x.astype(BF16).astype(F32), jnp.uint32)
    return bits[:, :half] | (bits[:, half:] >> 16)


def _unpack_bf16(w):
    hi = pltpu.bitcast(w & jnp.uint32(0xFFFF0000), F32)
    lo = pltpu.bitcast(w << 16, F32)
    return hi, lo


def _store_chunk_rows(ref, w):
    n = w.shape[0]
    for j in range(ROW_CHUNKS):
        ref[pl.ds(j, n, stride=ROW_CHUNKS), :] = w[:, j * LANES:(j + 1) * LANES]


def _load_chunk_rows(ref, n):
    return jnp.concatenate([ref[pl.ds(j, n, stride=ROW_CHUNKS), :] for j in range(ROW_CHUNKS)], axis=1)


def _packed_dot(w, weight_ref):
    half = w.shape[-1]
    hi, lo = _unpack_bf16(w)
    return (jnp.dot(hi.astype(BF16), weight_ref[:half, :], preferred_element_type=F32)
            + jnp.dot(lo.astype(BF16), weight_ref[half:, :], preferred_element_type=F32))


def _mod_kernel(c_ref, w_ref, b_ref, o_ref):
    c = c_ref[...]
    s = _silu(c)
    o_ref[...] = jnp.dot(s, w_ref[...], preferred_element_type=F32, precision=HIGHEST) + b_ref[...]


def _modulations(c_all, w_mod, b_mod):
    depth, d, _ = w_mod.shape
    return pl.pallas_call(
        _mod_kernel,
        grid=(depth, N_MOD),
        in_specs=[
            pl.BlockSpec((MOD_ROWS, d), lambda l, k: (0, 0)),
            pl.BlockSpec((None, d, d), lambda l, k: (l, 0, k)),
            pl.BlockSpec((None, 1, d), lambda l, k: (l, 0, k)),
        ],
        out_specs=pl.BlockSpec((None, MOD_ROWS, d), lambda l, k: (l, 0, k)),
        out_shape=jax.ShapeDtypeStruct((depth, MOD_ROWS, N_MOD * d), F32),
        compiler_params=_cparams(("arbitrary", "arbitrary")),
        name="adaln_modulation",
    )(c_all, w_mod, b_mod.reshape(depth, 1, N_MOD * d))


def _mod_spec(d, layer, chunk, row_fn):
    return pl.BlockSpec((None, 1, d), lambda b, i: (layer * MOD_ROWS + row_fn(b, i), 0, chunk))


def _norm_mod(x, g, sh, sc):
    ms = jnp.mean(x * x, axis=-1, keepdims=True)
    return (x * lax.rsqrt(ms + NORM_EPS) * g) * (1.0 + sc) + sh


def _rope(y, cos, sa, sb):
    n = y.shape[-1]
    half = HEAD_DIM // 2
    return y * cos + pltpu.roll(y, n - half, 1) * sa + pltpu.roll(y, half, 1) * sb


def _proj_kernel(*refs, ropes):
    n = len(ropes)
    x_ref, g_ref, sh_ref, sc_ref, cos_ref, sa_ref, sb_ref = refs[:7]
    w_refs = refs[7:7 + n]
    o_refs = refs[7 + n:]
    h = _norm_mod(x_ref[...], g_ref[...], sh_ref[...], sc_ref[...]).astype(BF16)
    for w_ref, o_ref, rope in zip(w_refs, o_refs, ropes):
        y = jnp.dot(h, w_ref[...], preferred_element_type=F32)
        if rope is not None:
            w = y.shape[-1]
            y = _rope(y, cos_ref[:, :w], sa_ref[:, :w], sb_ref[:, :w])
            if rope != 1.0:
                y = y * rope
        o_ref[...] = y.astype(o_ref.dtype)


def _project(xc, g, mods3, layer, nct, rope_tabs, groups):
    b, l, d = xc.shape
    tm = ROW_TILE
    mod_row = lambda i, bb: layer * MOD_ROWS + jnp.where(i < nct, SUBLANES, bb)
    rw = rope_tabs[0].shape[-1]
    in_specs = [
        pl.BlockSpec((None, tm, d), lambda i, bb: (bb, i, 0)),
        pl.BlockSpec((1, d), lambda i, bb: (0, 0)),
        pl.BlockSpec((None, 1, d), lambda i, bb: (mod_row(i, bb), 0, 0)),
        pl.BlockSpec((None, 1, d), lambda i, bb: (mod_row(i, bb), 0, 1)),
    ] + [pl.BlockSpec((tm, rw), lambda i, bb: (i, 0))] * 3
    out_specs, out_shapes = [], []
    for w, _, dt, time_major in groups:
        n = w.shape[1]
        in_specs.append(pl.BlockSpec((d, n), lambda i, bb: (0, 0)))
        if time_major:
            out_specs.append(pl.BlockSpec((tm, n), lambda i, bb: (i, bb)))
            out_shapes.append(jax.ShapeDtypeStruct((l, b * n), dt))
        else:
            out_specs.append(pl.BlockSpec((None, tm, n), lambda i, bb: (bb, i, 0)))
            out_shapes.append(jax.ShapeDtypeStruct((b, l, n), dt))
    return pl.pallas_call(
        functools.partial(_proj_kernel, ropes=tuple(gp[1] for gp in groups)),
        grid=(l // tm, b),
        in_specs=in_specs,
        out_specs=out_specs,
        out_shape=out_shapes,
        compiler_params=_cparams(("arbitrary", "arbitrary")),
        name="norm_mod_project",
    )(xc, g.reshape(1, d), mods3, mods3, *rope_tabs, *[gp[0] for gp in groups])


def _rope_tables(c_len, s_len):
    rows = s_len // GRID_W
    row = jnp.repeat(jnp.arange(rows), GRID_W).astype(F32)
    col = jnp.tile(jnp.arange(GRID_W), rows).astype(F32)
    n = HEAD_DIM // 4
    inv = ROPE_BASE ** (-jnp.arange(n, dtype=F32) / n)
    ang = jnp.concatenate([row[:, None] * inv, col[:, None] * inv], axis=-1)
    cos, sin = jnp.cos(ang), jnp.sin(ang)
    zero = jnp.zeros_like(sin)
    reps = WIN_HEADS
    cos_t = jnp.tile(jnp.concatenate([cos, cos], axis=-1), (1, reps))
    sa_t = jnp.tile(jnp.concatenate([-sin, zero], axis=-1), (1, reps))
    sb_t = jnp.tile(jnp.concatenate([zero, sin], axis=-1), (1, reps))
    w = cos_t.shape[-1]
    pad1 = jnp.ones((c_len, w), F32)
    pad0 = jnp.zeros((c_len, w), F32)
    return (jnp.concatenate([pad1, cos_t], 0), jnp.concatenate([pad0, sa_t], 0),
            jnp.concatenate([pad0, sb_t], 0))


def _seq_tile(d, g, nct, nt):
    rev = jnp.where(g < nct, nct - 1 - g, nt - 1 - (g - nct))
    return jnp.where(d == 0, g, rev)


def _rglru_kernel(x_ref, xp_ref, xn_ref, cw_ref, cb_ref, w_ref, bias_ref, lam_ref, o_ref,
                  ext_scr, a_scr, b_scr, h_scr, *, ts, nct, nt, sub):
    d = pl.program_id(0)
    g = pl.program_id(1)
    tile = _seq_tile(d, g, nct, nt)
    bsz, width = h_scr.shape
    pv = jnp.where((tile == 0) | (tile == nct), 0.0, 1.0)
    nv = jnp.where((tile == nct - 1) | (tile == nt - 1), 0.0, 1.0)
    ext_scr[0:1] = xp_ref[...] * pv
    ext_scr[1:ts + 1] = x_ref[...]
    ext_scr[ts + 1:ts + 3] = xn_ref[...] * nv

    @pl.when(g == 0)
    def _():
        h_scr[...] = jnp.zeros_like(h_scr)

    neg_sp = -LRU_C * _softplus(-lam_ref[...])

    def prep(c, carry):
        r0 = pl.multiple_of(c * sub, sub)
        e = ext_scr[pl.ds(r0, sub + CONV_W - 1)]
        u = cb_ref[...] + cw_ref[0] * e[0:sub]
        for j in range(1, CONV_W):
            u = u + cw_ref[j] * e[j:j + sub]
        u2 = u.reshape(sub * bsz, width)
        gts = jnp.dot(u2.astype(BF16), w_ref[...], preferred_element_type=F32) + bias_ref[...]
        r = _sigmoid(gts[:, :width])
        ig = _sigmoid(gts[:, width:])
        log_a = neg_sp * r
        a = jnp.exp(log_a)
        mult = jnp.sqrt(1.0 - a * a)
        a_scr[pl.ds(r0, sub)] = a.reshape(sub, bsz, width)
        b_scr[pl.ds(r0, sub)] = (mult * ig * u2).reshape(sub, bsz, width)
        return carry

    lax.fori_loop(0, ts // sub, prep, 0)

    def step(t, h):
        tt = jnp.where(d == 0, t, ts - 1 - t)
        h = a_scr[tt] * h + b_scr[tt]
        o_ref[tt] = h
        return h

    h_scr[...] = lax.fori_loop(0, ts, step, h_scr[...], unroll=8)


def _rglru(xa_tm, conv_w, conv_b, w_gates, b_gates, lam, nct):
    l, bsz, width = xa_tm.shape
    ts = TIME_TILE
    nt = l // ts
    tile = lambda d, g: _seq_tile(d, g, nct, nt)
    kern = functools.partial(_rglru_kernel, ts=ts, nct=nct, nt=nt, sub=16)
    return pl.pallas_call(
        kern,
        grid=(2, nt),
        in_specs=[
            pl.BlockSpec((ts, bsz, width), lambda d, g: (tile(d, g), 0, 0)),
            pl.BlockSpec((1, bsz, width), lambda d, g: (jnp.maximum(tile(d, g) * ts - 1, 0), 0, 0)),
            pl.BlockSpec((2, bsz, width),
                         lambda d, g: (jnp.minimum((tile(d, g) + 1) * (ts // 2), l // 2 - 1), 0, 0)),
            pl.BlockSpec((CONV_W, 1, width), lambda d, g: (0, 0, 0)),
            pl.BlockSpec((1, width), lambda d, g: (0, 0)),
            pl.BlockSpec((None, width, 2 * width), lambda d, g: (d, 0, 0)),
            pl.BlockSpec((None, 1, 2 * width), lambda d, g: (d, 0, 0)),
            pl.BlockSpec((None, 1, width), lambda d, g: (d, 0, 0)),
        ],
        out_specs=pl.BlockSpec((None, ts, bsz, width), lambda d, g: (d, tile(d, g), 0, 0)),
        out_shape=jax.ShapeDtypeStruct((2, l, bsz, width), F32),
        scratch_shapes=[
            pltpu.VMEM((ts + CONV_W - 1, bsz, width), F32),
            pltpu.VMEM((ts, bsz, width), F32),
            pltpu.VMEM((ts, bsz, width), F32),
            pltpu.VMEM((bsz, width), F32),
        ],
        compiler_params=_cparams(("arbitrary", "arbitrary")),
        name="rglru_scan",
    )(xa_tm, xa_tm, xa_tm, conv_w.reshape(CONV_W, 1, width), conv_b.reshape(1, width),
      w_gates, b_gates, lam)


def _block_diag(w):
    nb, c, dd = w.shape
    eye = jnp.eye(nb, dtype=w.dtype)
    return (eye[:, None, :, None] * w[:, :, None, :]).reshape(nb * c, nb * dd)


def _win_attn_kernel(sink_ref, q_ref, k_ref, v_ref, o_ref, *, c_len, l_len, nqc):
    j = pl.program_id(1)
    blk = q_ref.shape[0]
    grp = WIN_HEADS // WIN_KV_HEADS
    band = blk + 2 * WINDOW
    heads = [(h, slice(h * HEAD_DIM, (h + 1) * HEAD_DIM),
              slice((h // grp) * HEAD_DIM, (h // grp + 1) * HEAD_DIM)) for h in range(WIN_HEADS)]

    @pl.when(j < nqc)
    def _():
        logits = [_nt_dot(q_ref[:, hsl], k_ref[0:c_len, ksl]) for _, hsl, ksl in heads]
        probs, dens = [], []
        for (h, _, _), s in zip(heads, logits):
            sink = sink_ref[h] * LOG2E
            m = jnp.maximum(jnp.max(s, axis=-1, keepdims=True), sink)
            p = jnp.exp2(s - m)
            dens.append(jnp.sum(p, axis=-1, keepdims=True) + jnp.exp2(sink - m))
            probs.append(p.astype(BF16))
        outs = [jnp.dot(p, v_ref[0:c_len, ksl], preferred_element_type=F32) / den
                for (_, _, ksl), p, den in zip(heads, probs, dens)]
        o_ref[...] = jnp.concatenate(outs, axis=-1).astype(o_ref.dtype)

    @pl.when(j >= nqc)
    def _():
        jb = j - nqc
        start = jnp.clip(c_len + jb * blk - WINDOW, c_len - WINDOW, l_len - band)
        start = pl.multiple_of(start, WINDOW)
        qpos = jb * blk + lax.broadcasted_iota(jnp.int32, (blk, band), 0)
        kpos = start - c_len + lax.broadcasted_iota(jnp.int32, (blk, band), 1)
        valid = (jnp.abs(qpos - kpos) <= WINDOW) & (kpos >= 0)
        lc = [_nt_dot(q_ref[:, hsl], k_ref[0:c_len, ksl]) for _, hsl, ksl in heads]
        lb = [jnp.where(valid, _nt_dot(q_ref[:, hsl], k_ref[pl.ds(start, band), ksl]), NEG_BIG)
              for _, hsl, ksl in heads]
        pcs, pbs, dens = [], [], []
        for (h, _, _), sc, sb in zip(heads, lc, lb):
            sink = sink_ref[h] * LOG2E
            m = jnp.maximum(jnp.maximum(jnp.max(sc, axis=-1, keepdims=True),
                                        jnp.max(sb, axis=-1, keepdims=True)), sink)
            pc = jnp.exp2(sc - m)
            pb = jnp.exp2(sb - m)
            dens.append(jnp.sum(pc, axis=-1, keepdims=True) + jnp.sum(pb, axis=-1, keepdims=True)
                        + jnp.exp2(sink - m))
            pcs.append(pc.astype(BF16))
            pbs.append(pb.astype(BF16))
        outs = [(jnp.dot(pc, v_ref[0:c_len, ksl], preferred_element_type=F32)
                 + jnp.dot(pb, v_ref[pl.ds(start, band), ksl], preferred_element_type=F32)) / den
                for (_, _, ksl), pc, pb, den in zip(heads, pcs, pbs, dens)]
        o_ref[...] = jnp.concatenate(outs, axis=-1).astype(o_ref.dtype)


def _win_attention(q, k, v, sink, c_len):
    b, l, qw = q.shape
    kw = k.shape[-1]
    blk = ROW_TILE
    kern = functools.partial(_win_attn_kernel, c_len=c_len, l_len=l, nqc=c_len // blk)
    return pl.pallas_call(
        kern,
        grid=(b, l // blk),
        in_specs=[
            pl.BlockSpec(memory_space=pltpu.SMEM),
            pl.BlockSpec((None, blk, qw), lambda bb, j: (bb, j, 0)),
            pl.BlockSpec((None, l, kw), lambda bb, j: (bb, 0, 0)),
            pl.BlockSpec((None, l, kw), lambda bb, j: (bb, 0, 0)),
        ],
        out_specs=pl.BlockSpec((None, blk, qw), lambda bb, j: (bb, j, 0)),
        out_shape=jax.ShapeDtypeStruct((b, l, qw), BF16),
        compiler_params=_cparams(("arbitrary", "arbitrary")),
        name="window_attention",
    )(sink, q, k, v)


def _out_even_kernel(x_ref, rec_ref, gate_ref, att_ref, wa_ref, wb_ref, g1_ref, *refs):
    route_in, o_ref, route_out = refs[:5], refs[5], refs[6:]
    lru = (rec_ref[0] + rec_ref[1]) * jax.nn.gelu(gate_ref[...].astype(F32))
    y = (jnp.dot(lru.astype(BF16), wa_ref[...], preferred_element_type=F32)
         + jnp.dot(att_ref[...], wb_ref[...], preferred_element_type=F32))
    x = x_ref[...] + g1_ref[...] * y
    o_ref[...] = x
    _route(x, *route_in, *route_out)


def _out_even(xc, rec2, gate, att, w_a, w_b, mods3, layer, nct, g_ffn, w_router, b_router):
    b, l, d = xc.shape
    tm = ROW_TILE
    w = gate.shape[-1]
    row = lambda bb, i: jnp.where(i < nct, SUBLANES, bb)
    r_in, r_args, r_out, r_shapes, r_scratch = _route_plumbing(b, l, d, g_ffn, mods3, layer, row, w_router, b_router)
    return pl.pallas_call(
        _out_even_kernel,
        grid=(b, l // tm),
        in_specs=[
            pl.BlockSpec((None, tm, d), lambda bb, i: (bb, i, 0)),
            pl.BlockSpec((2, tm, w), lambda bb, i: (0, i, bb)),
            pl.BlockSpec((None, tm, w), lambda bb, i: (bb, i, 0)),
            pl.BlockSpec((None, tm, att.shape[-1]), lambda bb, i: (bb, i, 0)),
            pl.BlockSpec(w_a.shape, lambda bb, i: (0, 0)),
            pl.BlockSpec(w_b.shape, lambda bb, i: (0, 0)),
            _mod_spec(d, layer, 2, row),
        ] + r_in,
        out_specs=[pl.BlockSpec((None, tm, d), lambda bb, i: (bb, i, 0))] + r_out,
        out_shape=[jax.ShapeDtypeStruct((b, l, d), F32)] + r_shapes,
        scratch_shapes=r_scratch,
        compiler_params=_cparams(("arbitrary", "arbitrary")),
        name="out_proj_even",
    )(xc, rec2, gate, att, w_a, w_b, mods3, *r_args)


def _diff_attn_kernel(lam_ref, g_ref, q_ref, k_ref, v_ref, o_ref, *, lam_init):
    lv = lam_ref[...]
    lam = (jnp.exp(jnp.sum(lv[0:1] * lv[1:2], axis=-1, keepdims=True))
           - jnp.exp(jnp.sum(lv[2:3] * lv[3:4], axis=-1, keepdims=True)) + lam_init)
    vw = 2 * DIFF_DH

    def logits(h, mp):
        lo = h * vw + mp * DIFF_DH
        return _nt_dot(q_ref[:, lo:lo + DIFF_DH], k_ref[:, lo:lo + DIFF_DH])

    def softmax_parts(s):
        e = jnp.exp2(s - jnp.max(s, axis=-1, keepdims=True))
        return e, 1.0 / jnp.sum(e, axis=-1, keepdims=True)

    heads = range(DIFF_HEADS)
    ls = [(logits(h, 0), logits(h, 1)) for h in heads]
    ws = []
    for l0, l1 in ls:
        e0, r0 = softmax_parts(l0)
        e1, r1 = softmax_parts(l1)
        ws.append(((e0 - e1 * (lam * r1 / r0)).astype(BF16), r0))
    for h, (w, r0) in zip(heads, ws):
        lo = h * vw
        o = jnp.dot(w, v_ref[:, lo:lo + vw], preferred_element_type=F32) * r0
        ms = jnp.mean(o * o, axis=-1, keepdims=True)
        o = o * lax.rsqrt(ms + NORM_EPS) * g_ref[...]
        o_ref[:, lo:lo + vw] = (o * (1.0 - lam_init)).astype(o_ref.dtype)


def _diff_attention(q, k, v, lam_vecs, subln_g, lam_init, c_len):
    b, l, w = q.shape
    tq = ROW_TILE
    s_len = l - c_len
    off = c_len // tq
    return pl.pallas_call(
        functools.partial(_diff_attn_kernel, lam_init=lam_init),
        grid=(b, s_len // tq),
        in_specs=[
            pl.BlockSpec(lam_vecs.shape, lambda bb, j: (0, 0)),
            pl.BlockSpec((1, 2 * DIFF_DH), lambda bb, j: (0, 0)),
            pl.BlockSpec((None, tq, w), lambda bb, j: (bb, j + off, 0)),
            pl.BlockSpec((None, l, w), lambda bb, j: (bb, 0, 0)),
            pl.BlockSpec((None, l, w), lambda bb, j: (bb, 0, 0)),
        ],
        out_specs=pl.BlockSpec((None, tq, w), lambda bb, j: (bb, j, 0)),
        out_shape=jax.ShapeDtypeStruct((b, s_len, w), BF16),
        compiler_params=_cparams(("arbitrary", "arbitrary")),
        name="diff_attention",
    )(lam_vecs, subln_g.reshape(1, -1), q, k, v)


def _ssd_kernel(x_ref, xp_ref, xn_ref, dt_ref, cw_ref, cb_ref, dtb_ref, alog_ref, dsk_ref, o_ref,
                ext_scr, st_scr, *, q, nct, nt):
    d = pl.program_id(0)
    g = pl.program_id(2)
    tile = _seq_tile(d, g, nct, nt)
    pv = jnp.where((tile == 0) | (tile == nct), 0.0, 1.0)
    nv = jnp.where((tile == nct - 1) | (tile == nt - 1), 0.0, 1.0)
    ext_scr[0:SUBLANES] = xp_ref[...] * pv
    ext_scr[SUBLANES:SUBLANES + q] = x_ref[...]
    ext_scr[SUBLANES + q:2 * SUBLANES + q] = xn_ref[...] * nv

    @pl.when(g == 0)
    def _():
        st_scr[...] = jnp.zeros_like(st_scr)

    u = cb_ref[...] + cw_ref[0] * ext_scr[SUBLANES - 1:SUBLANES - 1 + q, :]
    for j in range(1, CONV_W):
        u = u + cw_ref[j] * ext_scr[SUBLANES - 1 + j:SUBLANES - 1 + j + q, :]
    act = _silu(u)

    dtr = dt_ref[...]
    dtr = jnp.where(d == 0, dtr, pltpu.roll(dtr, LANES - SSD_HEADS, 1))
    dtv = _softplus(dtr + dtb_ref[...])
    head_lane = lax.broadcasted_iota(jnp.int32, (1, LANES), 1) < SSD_HEADS
    dta = dtv * jnp.where(head_lane, -jnp.exp(alog_ref[...]) * LOG2E, 0.0)
    ri = lax.broadcasted_iota(jnp.int32, (q, q), 0)
    ci = lax.broadcasted_iota(jnp.int32, (q, q), 1)
    keep = jnp.where(d == 0, ri - ci, ci - ri) >= 0
    keep_b = jnp.where(keep, 1.0, 0.0).astype(BF16)
    cum = jnp.zeros((q, LANES), F32)
    rest = dta
    for _ in range(3):
        part = rest.astype(BF16)
        cum = cum + jnp.dot(keep_b, part, preferred_element_type=F32)
        rest = rest - part.astype(F32)
    tot = jnp.sum(dta, axis=0, keepdims=True)
    cum_t = cum.T
    dt_t = dtv.T
    to_end = jnp.exp2(tot - cum) * dtv
    e_cum = jnp.exp2(cum)
    e_tot = jnp.exp2(tot)
    dskip = dsk_ref[...] * jnp.where(d == 0, 1.0, 0.0)

    def spread(v, width):
        rows = lax.broadcasted_iota(jnp.int32, (LANES, SSD_HEADS * width), 0)
        cols = lax.broadcasted_iota(jnp.int32, (LANES, SSD_HEADS * width), 1)
        pick = jnp.where(rows * width <= cols, jnp.where(cols < (rows + 1) * width, 1.0, 0.0), 0.0).astype(BF16)
        out = jnp.zeros((q, SSD_HEADS * width), F32)
        rest = v
        for _ in range(3):
            part = rest.astype(BF16)
            out = out + jnp.dot(part, pick, preferred_element_type=F32)
            rest = rest - part.astype(F32)
        return out

    e_cum_b = spread(e_cum, SSD_HEAD_DIM)
    to_end_b = spread(to_end, SSD_HEAD_DIM)

    hpg = SSD_HEADS // SSD_GROUPS
    for gi in range(SSD_GROUPS):
        b_g = act[:, SSD_INNER + gi * SSD_STATE:SSD_INNER + (gi + 1) * SSD_STATE]
        c_lo = SSD_INNER + SSD_GROUPS * SSD_STATE + gi * SSD_STATE
        c_g = act[:, c_lo:c_lo + SSD_STATE].astype(BF16)
        cb = _nt_dot(c_g, b_g.astype(BF16))
        b_gt = b_g.T.astype(BF16)
        for hh in range(hpg):
            h = gi * hpg + hh
            xs = act[:, h * SSD_HEAD_DIM:(h + 1) * SSD_HEAD_DIM]
            seg = cum[:, h:h + 1] - cum_t[h:h + 1, :]
            decay = jnp.exp2(jnp.where(keep, seg, NEG_BIG))
            w = (cb * decay * dt_t[h:h + 1, :]).astype(BF16)
            state = st_scr[h]
            y = jnp.dot(w, xs.astype(BF16), preferred_element_type=F32)
            y = y + (jnp.dot(c_g, state.astype(BF16), preferred_element_type=F32)
                     * e_cum_b[:, h * SSD_HEAD_DIM:(h + 1) * SSD_HEAD_DIM])
            y = y + dskip[:, h * SSD_HEAD_DIM:(h + 1) * SSD_HEAD_DIM] * xs
            o_ref[:, h * SSD_HEAD_DIM:(h + 1) * SSD_HEAD_DIM] = y
            s_new = jnp.dot(b_gt, (xs * to_end_b[:, h * SSD_HEAD_DIM:(h + 1) * SSD_HEAD_DIM]).astype(BF16),
                            preferred_element_type=F32)
            st_scr[h] = e_tot[:, h:h + 1] * state + s_new


def _ssd(xbc, dt, conv_w, conv_b, dt_bias, a_log, d_skip, nct):
    b, l, cd = xbc.shape
    q = TIME_TILE
    nt = l // q
    tile = lambda d, bb, g: _seq_tile(d, g, nct, nt)
    r8 = q // SUBLANES
    pad = LANES - SSD_HEADS
    dtb = jnp.pad(dt_bias, ((0, 0), (0, pad))).reshape(2, 1, LANES)
    alog = jnp.pad(a_log, ((0, 0), (0, pad))).reshape(2, 1, LANES)
    dsk = jnp.repeat(d_skip, SSD_HEAD_DIM).reshape(1, SSD_INNER)
    return pl.pallas_call(
        functools.partial(_ssd_kernel, q=q, nct=nct, nt=nt),
        grid=(2, b, nt),
        in_specs=[
            pl.BlockSpec((None, q, cd), lambda d, bb, g: (bb, tile(d, bb, g), 0)),
            pl.BlockSpec((None, SUBLANES, cd),
                         lambda d, bb, g: (bb, jnp.maximum(tile(d, bb, g) * r8 - 1, 0), 0)),
            pl.BlockSpec((None, SUBLANES, cd),
                         lambda d, bb, g: (bb, jnp.minimum((tile(d, bb, g) + 1) * r8, l // SUBLANES - 1), 0)),
            pl.BlockSpec((None, q, LANES), lambda d, bb, g: (bb, tile(d, bb, g), 0)),
            pl.BlockSpec((CONV_W, 1, cd), lambda d, bb, g: (0, 0, 0)),
            pl.BlockSpec((1, cd), lambda d, bb, g: (0, 0)),
            pl.BlockSpec((None, 1, LANES), lambda d, bb, g: (d, 0, 0)),
            pl.BlockSpec((None, 1, LANES), lambda d, bb, g: (d, 0, 0)),
            pl.BlockSpec((1, SSD_INNER), lambda d, bb, g: (0, 0)),
        ],
        out_specs=pl.BlockSpec((None, None, q, SSD_INNER), lambda d, bb, g: (d, bb, tile(d, bb, g), 0)),
        out_shape=jax.ShapeDtypeStruct((2, b, l, SSD_INNER), F32),
        scratch_shapes=[
            pltpu.VMEM((q + 2 * SUBLANES, cd), F32),
            pltpu.VMEM((SSD_HEADS, SSD_STATE, SSD_HEAD_DIM), F32),
        ],
        compiler_params=_cparams(("arbitrary", "arbitrary", "arbitrary")),
        name="ssd_chunked",
    )(xbc, xbc, xbc, dt, conv_w.reshape(CONV_W, 1, cd), conv_b.reshape(1, cd), dtb, alog, dsk)


def _out_odd_kernel(x_ref, diff_ref, y_ref, z_ref, ng_ref, wa_ref, wb_ref, g1_ref, *refs):
    route_in, o_ref, route_out = refs[:5], refs[5], refs[6:]
    yz = (y_ref[0] + y_ref[1]) * _silu(z_ref[...].astype(F32))
    gs = SSD_INNER // SSD_GROUPS
    parts = []
    for gi in range(SSD_GROUPS):
        seg = yz[:, gi * gs:(gi + 1) * gs]
        ms = jnp.mean(seg * seg, axis=-1, keepdims=True)
        parts.append(seg * lax.rsqrt(ms + NORM_EPS) * ng_ref[:, gi * gs:(gi + 1) * gs])
    ssd = jnp.concatenate(parts, axis=-1).astype(BF16)
    y = (jnp.dot(diff_ref[...], wa_ref[...], preferred_element_type=F32)
         + jnp.dot(ssd, wb_ref[...], preferred_element_type=F32))
    x = x_ref[...] + g1_ref[...] * y
    o_ref[...] = x
    _route(x, *route_in, *route_out)


def _out_odd(xc, diff, y2, z, norm_g, w_a, w_b, mods3, layer, c_len, g_ffn, w_router, b_router):
    b, l, d = xc.shape
    s_len = l - c_len
    tm = ROW_TILE
    off = c_len // tm
    w = SSD_INNER
    row = lambda bb, i: bb
    r_in, r_args, r_out, r_shapes, r_scratch = _route_plumbing(b, s_len, d, g_ffn, mods3, layer, row, w_router,
                                                               b_router)
    return pl.pallas_call(
        _out_odd_kernel,
        grid=(b, s_len // tm),
        in_specs=[
            pl.BlockSpec((None, tm, d), lambda bb, i: (bb, i + off, 0)),
            pl.BlockSpec((None, tm, diff.shape[-1]), lambda bb, i: (bb, i, 0)),
            pl.BlockSpec((2, None, tm, w), lambda bb, i: (0, bb, i + off, 0)),
            pl.BlockSpec((None, tm, w), lambda bb, i: (bb, i + off, 0)),
            pl.BlockSpec((1, w), lambda bb, i: (0, 0)),
            pl.BlockSpec(w_a.shape, lambda bb, i: (0, 0)),
            pl.BlockSpec(w_b.shape, lambda bb, i: (0, 0)),
            _mod_spec(d, layer, 2, row),
        ] + r_in,
        out_specs=[pl.BlockSpec((None, tm, d), lambda bb, i: (bb, i, 0))] + r_out,
        out_shape=[jax.ShapeDtypeStruct((b, s_len, d), F32)] + r_shapes,
        scratch_shapes=r_scratch,
        compiler_params=_cparams(("arbitrary", "arbitrary")),
        name="out_proj_odd",
    )(xc, diff, y2, z, norm_g.reshape(1, w), w_a, w_b, mods3, *r_args)


def _route(x, g_ref, sh_ref, sc_ref, wr_ref, br_ref, h_ref, eid_ref, rnk_ref, gate_ref, cnt_ref, carry_scr):
    @pl.when((pl.program_id(0) == 0) & (pl.program_id(1) == 0))
    def _():
        carry_scr[...] = jnp.zeros_like(carry_scr)

    h = _norm_mod(x, g_ref[...], sh_ref[...], sc_ref[...])
    _store_chunk_rows(h_ref, _pack_bf16(h))
    tm = h.shape[0]
    per = N_EXPERTS // N_EXPERT_GROUPS
    h_hi = h.astype(BF16)
    h_lo = (h - h_hi.astype(F32)).astype(BF16)
    logits = _nt_dot(wr_ref[0], h_hi) + _nt_dot(wr_ref[0], h_lo) + _nt_dot(wr_ref[1], h_hi)
    scores = jax.nn.sigmoid(logits)
    sel = scores + br_ref[...]
    sel3 = sel.reshape(N_EXPERT_GROUPS, per, tm)
    kio = lax.broadcasted_iota(jnp.int32, sel3.shape, 1)
    m1 = jnp.max(sel3, axis=1, keepdims=True)
    first = jnp.min(jnp.where(sel3 == m1, kio, per), axis=1, keepdims=True)
    m2 = jnp.max(jnp.where(kio == first, NEG_BIG, sel3), axis=1, keepdims=True)
    gs = m1 + m2
    gio = lax.broadcasted_iota(jnp.int32, gs.shape, 0)
    ahead = jnp.zeros(gs.shape, jnp.int32)
    for gp in range(N_EXPERT_GROUPS):
        other = gs[gp:gp + 1]
        ahead = ahead + jnp.where((other > gs) | ((other == gs) & (gp < gio)), 1, 0)
    grp_on = jnp.where(ahead < TOPK_GROUPS, 1.0, 0.0)
    selm = jnp.where(jnp.broadcast_to(grp_on, sel3.shape) > 0.5, sel3, NEG_BIG).reshape(N_EXPERTS, tm)
    eio = lax.broadcasted_iota(jnp.int32, selm.shape, 0)
    work = selm
    cf = jnp.zeros(selm.shape, F32)
    e_rows, s_rows = [], []
    for k in range(TOP_K):
        best = jnp.max(work, axis=0, keepdims=True)
        idx = jnp.min(jnp.where(work == best, eio, N_EXPERTS), axis=0, keepdims=True)
        hit = eio == idx
        cf = cf + jnp.where(hit, 1.0, 0.0)
        work = jnp.where(hit, NEG_BIG, work)
        e_rows.append(idx)
        s_rows.append(jnp.sum(jnp.where(hit, scores, 0.0), axis=0, keepdims=True))
    denom = s_rows[0]
    for s_k in s_rows[1:]:
        denom = denom + s_k
    g_rows = [s_k / denom * ROUTED_SCALE for s_k in s_rows]
    ti = lax.broadcasted_iota(jnp.int32, (tm, tm), 0)
    tj = lax.broadcasted_iota(jnp.int32, (tm, tm), 1)
    before = jnp.where(ti < tj, 1.0, 0.0).astype(BF16)
    in_expert = carry_scr[:, 0:1] + jnp.dot(cf.astype(BF16), before, preferred_element_type=F32)
    carry_scr[...] = carry_scr[...] + jnp.sum(cf, axis=1, keepdims=True)
    cnt_ref[...] = carry_scr[...]
    r_rows = [jnp.sum(jnp.where(eio == idx, in_expert, 0.0), axis=0, keepdims=True) for idx in e_rows]
    eid_ref[...] = jnp.concatenate(e_rows, axis=0)
    rnk_ref[...] = jnp.concatenate(r_rows, axis=0).astype(jnp.int32)
    padded = jnp.concatenate(g_rows + [jnp.zeros((LANES - TOP_K, tm), F32)], axis=0)
    gate_ref[...] = padded.T


def _route_plumbing(b, r, d, g, mods3, layer, row_fn, w_router, b_router):
    tm = ROW_TILE
    nt = r // tm
    w_t = w_router.T
    w_hi = w_t.astype(BF16)
    w_router_t = jnp.stack([w_hi, (w_t - w_hi.astype(F32)).astype(BF16)])
    slot = pl.BlockSpec((TOP_K, tm), lambda bb, i: (0, bb * nt + i))
    slot_shape = jax.ShapeDtypeStruct((TOP_K, b * r), jnp.int32)
    in_specs = [
        pl.BlockSpec((1, d), lambda bb, i: (0, 0)),
        _mod_spec(d, layer, 3, row_fn),
        _mod_spec(d, layer, 4, row_fn),
        pl.BlockSpec(w_router_t.shape, lambda bb, i: (0, 0, 0)),
        pl.BlockSpec((N_EXPERTS, 1), lambda bb, i: (0, 0)),
    ]
    args = [g.reshape(1, d), mods3, mods3, w_router_t, b_router.reshape(N_EXPERTS, 1)]
    out_specs = [
        pl.BlockSpec((tm * ROW_CHUNKS, LANES), lambda bb, i: (bb * nt + i, 0)),
        slot,
        slot,
        pl.BlockSpec((None, tm, LANES), lambda bb, i: (bb, i, 0)),
        pl.BlockSpec((N_EXPERTS, LANES), lambda bb, i: (0, 0)),
    ]
    out_shapes = [jax.ShapeDtypeStruct((b * r * ROW_CHUNKS, LANES), jnp.uint32), slot_shape, slot_shape,
                  jax.ShapeDtypeStruct((b, r, LANES), F32), jax.ShapeDtypeStruct((N_EXPERTS, LANES), F32)]
    scratch = [pltpu.VMEM((N_EXPERTS, LANES), F32)]
    return in_specs, args, out_specs, out_shapes, scratch


def _moe_plan(counts, n_rows):
    blk = EXPERT_BLK
    nb = n_rows // blk
    ends = jnp.cumsum(counts)
    starts = ends - counts
    count_le = lambda sorted_vals, q: jnp.sum(sorted_vals[None, :] <= q[:, None], axis=1, dtype=jnp.int32)
    first = jnp.arange(nb, dtype=jnp.int32) * blk
    e_lo = count_le(ends, first)
    e_hi = count_le(ends, first + (blk - 1))
    n_pair = e_hi - e_lo + 1
    p_end = jnp.cumsum(n_pair)
    p_start = p_end - n_pair
    i = jnp.arange(nb + N_EXPERTS - 1, dtype=jnp.int32)
    j = jnp.minimum(count_le(p_end, i), nb - 1)
    valid = i < p_end[-1]
    e = jnp.where(valid, e_lo[j] + i - p_start[j], e_hi[nb - 1]).astype(jnp.int32)
    bounds = jnp.concatenate([starts, ends[-1:]]).astype(jnp.int32)
    return j, e, valid.astype(jnp.int32), bounds


def _positions_kernel(starts_ref, eid_ref, rnk_ref, pos_ref):
    eid = eid_ref[...]
    pos = rnk_ref[...]
    for e in range(N_EXPERTS):
        pos = pos + jnp.where(eid == e, starts_ref[e], 0)
    pos_ref[...] = pos * ROW_CHUNKS


def _positions(eid, rnk, starts):
    full = pl.BlockSpec(eid.shape, lambda: (0, 0))
    return pl.pallas_call(
        _positions_kernel,
        in_specs=[pl.BlockSpec(memory_space=pltpu.SMEM), full, full],
        out_specs=full,
        out_shape=jax.ShapeDtypeStruct(eid.shape, jnp.int32),
        compiler_params=pltpu.CompilerParams(vmem_limit_bytes=VMEM_LIMIT),
        name="moe_positions",
    )(starts, eid, rnk)


def _token_row(ref, first):
    return ref.at[pl.ds(pl.multiple_of(first, ROW_CHUNKS), ROW_CHUNKS)]


def _dispatch_kernel(pos_ref, h_ref, xs_ref, sem):
    tm = h_ref.shape[0] // ROW_CHUNKS

    def issue(t, carry):
        src = _token_row(h_ref, t * ROW_CHUNKS)
        for k in range(TOP_K):
            pltpu.make_async_copy(src, _token_row(xs_ref, pos_ref[k, t]), sem).start(priority=k % 2)
        return carry

    lax.fori_loop(0, tm, issue, 0)
    done = pl.ds(0, tm * ROW_CHUNKS)
    for _ in range(TOP_K):
        pltpu.make_async_copy(h_ref.at[done], xs_ref.at[done], sem).wait()


def _dispatch(h2, pos):
    rows, w = h2.shape
    tm = math.gcd(DISPATCH_TILE, rows // ROW_CHUNKS)
    return pl.pallas_call(
        _dispatch_kernel,
        grid=(rows // (tm * ROW_CHUNKS),),
        in_specs=[
            pl.BlockSpec((TOP_K, tm), lambda i: (0, i), memory_space=pltpu.SMEM),
            pl.BlockSpec((tm * ROW_CHUNKS, w), lambda i: (i, 0)),
        ],
        out_specs=pl.BlockSpec(memory_space=pl.ANY),
        out_shape=jax.ShapeDtypeStruct((rows * TOP_K, w), h2.dtype),
        scratch_shapes=[pltpu.SemaphoreType.DMA],
        compiler_params=_cparams(("arbitrary",)),
        name="moe_dispatch",
    )(pos, h2)


def _grouped_kernel(pb_ref, pe_ref, pv_ref, bnd_ref, xs_ref, wg_ref, wu_ref, wd_ref, y_ref, wgb, wub, wdb):
    i = pl.program_id(0)
    prev = jnp.maximum(i - 1, 0)
    j = pb_ref[i]
    e = pe_ref[i]
    blk = xs_ref.shape[0] // ROW_CHUNKS

    @pl.when((i == 0) | (pb_ref[prev] != j))
    def _():
        y_ref[...] = jnp.zeros_like(y_ref)

    @pl.when((i == 0) | (pe_ref[prev] != e))
    def _():
        wgb[...] = wg_ref[...].astype(BF16)
        wub[...] = wu_ref[...].astype(BF16)
        wdb[...] = wd_ref[...].astype(BF16)

    @pl.when(pv_ref[i] == 1)
    def _():
        xw = _load_chunk_rows(xs_ref, blk)
        a = _packed_dot(xw, wgb)
        u = _packed_dot(xw, wub)
        yv = jnp.dot((_silu(a) * u).astype(BF16), wdb[...], preferred_element_type=F32)
        rows = j * blk + lax.broadcasted_iota(jnp.int32, (blk, 1), 0)
        own = (rows >= bnd_ref[e]) & (rows < bnd_ref[e + 1])
        yw = _pack_bf16(yv)
        for c in range(ROW_CHUNKS):
            sl = pl.ds(c, blk, stride=ROW_CHUNKS)
            y_ref[sl, :] = jnp.where(own, yw[:, c * LANES:(c + 1) * LANES], y_ref[sl, :])


def _grouped(pb, pe, pv, bounds, xs, wg, wu, wd, layer):
    p, half = xs.shape
    d = 2 * ROW_CHUNKS * LANES
    blk = EXPERT_BLK * ROW_CHUNKS
    grid_spec = pltpu.PrefetchScalarGridSpec(
        num_scalar_prefetch=4,
        grid=(pb.shape[0],),
        in_specs=[
            pl.BlockSpec((blk, half), lambda i, pb, pe, pv, bnd: (pb[i], 0)),
            pl.BlockSpec((None, None, d, D_EXPERT), lambda i, pb, pe, pv, bnd: (layer, pe[i], 0, 0)),
            pl.BlockSpec((None, None, d, D_EXPERT), lambda i, pb, pe, pv, bnd: (layer, pe[i], 0, 0)),
            pl.BlockSpec((None, None, D_EXPERT, d), lambda i, pb, pe, pv, bnd: (layer, pe[i], 0, 0)),
        ],
        out_specs=pl.BlockSpec((blk, half), lambda i, pb, pe, pv, bnd: (pb[i], 0)),
        scratch_shapes=[
            pltpu.VMEM((d, D_EXPERT), BF16),
            pltpu.VMEM((d, D_EXPERT), BF16),
            pltpu.VMEM((D_EXPERT, d), BF16),
        ],
    )
    return pl.pallas_call(
        _grouped_kernel,
        grid_spec=grid_spec,
        out_shape=jax.ShapeDtypeStruct((p, half), jnp.uint32),
        compiler_params=_cparams(("arbitrary",)),
        name="moe_grouped_experts",
    )(pb, pe, pv, bounds, xs, wg, wu, wd)


def _combine_kernel(*refs, final, c_len):
    pos_ref, y_ref, gate_ref, h_ref, x_ref, g2c_ref, g2l_ref, sg_ref, su_ref, sd_ref = refs[:10]
    o_ref, buf, sem = refs[-3:]
    tm = x_ref.shape[0]

    def issue(t, carry):
        for k in range(TOP_K):
            pltpu.make_async_copy(_token_row(y_ref, pos_ref[k, t]), _token_row(buf.at[k], t * ROW_CHUNKS),
                                  sem).start(priority=k % 2)
        return carry

    lax.fori_loop(0, tm, issue, 0)
    hw = _load_chunk_rows(h_ref, tm)
    a = _packed_dot(hw, sg_ref)
    u = _packed_dot(hw, su_ref)
    acc = jnp.dot((_silu(a) * u).astype(BF16), sd_ref[...], preferred_element_type=F32)
    done = pl.ds(0, tm * ROW_CHUNKS)
    for k in range(TOP_K):
        pltpu.make_async_copy(y_ref.at[done], buf.at[k, done], sem).wait()
    g = gate_ref[...]
    half = hw.shape[-1]
    acc_hi = acc[:, :half]
    acc_lo = acc[:, half:]
    for k in range(TOP_K):
        hi, lo = _unpack_bf16(_load_chunk_rows(buf.at[k], tm))
        acc_hi = acc_hi + g[:, k:k + 1] * hi
        acc_lo = acc_lo + g[:, k:k + 1] * lo
    is_ctx = pl.program_id(1) * tm + lax.broadcasted_iota(jnp.int32, (tm, 1), 0) < c_len
    g2 = jnp.where(is_ctx, g2c_ref[...], g2l_ref[...])
    x = x_ref[...] + g2 * jnp.concatenate([acc_hi, acc_lo], axis=-1)
    if final:
        gf_ref = refs[10]
        ms = jnp.mean(x * x, axis=-1, keepdims=True)
        x = x * lax.rsqrt(ms + NORM_EPS) * gf_ref[...]
    o_ref[...] = x


def _combine(pos, y, gates, h2, x, mods3, layer, c_len, tm, sg, su, sd, g_final=None):
    b, r, d = x.shape
    nt = r // tm
    tile = pl.BlockSpec((None, tm, d), lambda bb, i: (bb, i, 0))
    in_specs = [
        pl.BlockSpec((TOP_K, tm), lambda bb, i: (0, bb * nt + i), memory_space=pltpu.SMEM),
        pl.BlockSpec(memory_space=pl.ANY),
        pl.BlockSpec((None, tm, LANES), lambda bb, i: (bb, i, 0)),
        pl.BlockSpec((tm * ROW_CHUNKS, LANES), lambda bb, i: (bb * nt + i, 0)),
        tile,
        _mod_spec(d, layer, 5, lambda bb, i: SUBLANES),
        _mod_spec(d, layer, 5, lambda bb, i: bb),
        pl.BlockSpec(sg.shape, lambda bb, i: (0, 0)),
        pl.BlockSpec(su.shape, lambda bb, i: (0, 0)),
        pl.BlockSpec(sd.shape, lambda bb, i: (0, 0)),
    ]
    args = [pos, y, gates, h2, x, mods3, mods3, sg, su, sd]
    if g_final is not None:
        in_specs.append(pl.BlockSpec((1, d), lambda bb, i: (0, 0)))
        args.append(g_final.reshape(1, d))
    return pl.pallas_call(
        functools.partial(_combine_kernel, final=g_final is not None, c_len=c_len),
        grid=(b, nt),
        in_specs=in_specs,
        out_specs=tile,
        out_shape=jax.ShapeDtypeStruct((b, r, d), F32),
        scratch_shapes=[pltpu.VMEM((TOP_K, tm * ROW_CHUNKS, LANES), jnp.uint32), pltpu.SemaphoreType.DMA],
        compiler_params=_cparams(("arbitrary", "arbitrary")),
        name="moe_combine",
    )(*args)


def _moe(x, routed, mods3, layer, c_len, combine_tile, w_e_gate, w_e_up, w_e_down, ws_gate, ws_up, ws_down,
         g_final=None):
    b, r, d = x.shape
    h2, eid, rnk, gates, cnt = routed
    pb, pe, pv, bounds = _moe_plan(cnt[:, 0].astype(jnp.int32), b * r * TOP_K)
    pos = _positions(eid, rnk, bounds[:N_EXPERTS])
    xs = _dispatch(h2, pos)
    y = _grouped(pb, pe, pv, bounds, xs, w_e_gate, w_e_up, w_e_down, layer)
    return _combine(pos, y, gates, h2, x, mods3, layer, c_len, combine_tile,
                    ws_gate.astype(BF16), ws_up.astype(BF16), ws_down.astype(BF16), g_final)


def kernel(x, c, ctx, c_ctx, w_mod, b_mod, g_mix, g_ffn, g_final, ab_w_in, ab_w_out, ab_conv_w, ab_conv_b, ab_w_r, ab_b_r, ab_w_i, ab_b_i, ab_lam, ab_sink, cd_w_in, cd_w_out, cd_lam, cd_subln_g, cd_conv_w, cd_conv_b, cd_dt_bias, cd_a_log, cd_d_skip, cd_norm_g, w_router, b_router, w_e_gate, w_e_up, w_e_down, ws_gate, ws_up, ws_down):
    bsz, s_len, d = x.shape
    c_len = ctx.shape[1]
    depth = w_mod.shape[0]
    assert depth == 2 and bsz == SUBLANES, "kernels are specialised to depth 2 and batch 8"
    assert c_len % ROW_TILE == 0 and s_len % ROW_TILE == 0
    nct_row = c_len // ROW_TILE
    nct_time = c_len // TIME_TILE

    c_all = jnp.concatenate([c, c_ctx[None], jnp.zeros((MOD_ROWS - bsz - 1, d), F32)], axis=0)
    mods3 = _modulations(c_all, w_mod, b_mod).reshape(depth * MOD_ROWS, 1, N_MOD * d)
    rope_tabs = _rope_tables(c_len, s_len)
    xc = jnp.concatenate([ctx, x], axis=1)

    w_in = ab_w_in[0].astype(BF16)
    q_hi = LRU_WIDTH + WIN_HEADS * HEAD_DIM
    x_hi = q_hi + LRU_WIDTH
    k_hi = x_hi + WIN_KV_HEADS * HEAD_DIM
    gate, q, xa, k, v = _project(xc, g_mix[0], mods3, 0, nct_row, rope_tabs, [
        (w_in[:, :LRU_WIDTH], None, BF16, False),
        (w_in[:, LRU_WIDTH:q_hi], HEAD_DIM ** -0.5 * LOG2E, BF16, False),
        (w_in[:, q_hi:x_hi], None, F32, True),
        (w_in[:, x_hi:k_hi], 1.0, BF16, False),
        (w_in[:, k_hi:], None, BF16, False),
    ])
    l_len = c_len + s_len
    w_gates = jnp.stack([jnp.concatenate([_block_diag(ab_w_r[0, dd]), _block_diag(ab_w_i[0, dd])], axis=1)
                         for dd in range(2)]).astype(BF16)
    b_gates = jnp.concatenate([ab_b_r[0], ab_b_i[0]], axis=-1).reshape(2, 1, 2 * LRU_WIDTH)
    rec = _rglru(xa.reshape(l_len, bsz, LRU_WIDTH), ab_conv_w[0], ab_conv_b[0], w_gates, b_gates,
                 ab_lam[0].reshape(2, 1, LRU_WIDTH), nct_time)
    att = _win_attention(q, k, v, ab_sink[0], c_len)
    w_out = ab_w_out[0].astype(BF16)
    xc, *routed = _out_even(xc, rec.reshape(2, l_len, bsz * LRU_WIDTH), gate, att, w_out[:LRU_WIDTH],
                            w_out[LRU_WIDTH:], mods3, 0, nct_row, g_ffn[0], w_router[0], b_router[0])
    xc = _moe(xc, routed, mods3, 0, c_len, math.gcd(MIXED_COMBINE_TILE, l_len), w_e_gate, w_e_up, w_e_down,
              ws_gate[0], ws_up[0], ws_down[0])

    w_in = cd_w_in[0].astype(BF16)
    qk = DIFF_HEADS * 2 * DIFF_DH
    z_hi = qk + SSD_INNER
    k_hi = z_hi + qk
    v_hi = k_hi + qk
    x_hi = v_hi + SSD_CONV_DIM
    w_dt = jnp.pad(w_in[:, x_hi:], ((0, 0), (0, LANES - 2 * SSD_HEADS)))
    q, z, k, v, xbc, dt = _project(xc, g_mix[1], mods3, 1, nct_row, rope_tabs, [
        (w_in[:, :qk], DIFF_DH ** -0.5 * LOG2E, BF16, False),
        (w_in[:, qk:z_hi], None, BF16, False),
        (w_in[:, z_hi:k_hi], 1.0, BF16, False),
        (w_in[:, k_hi:v_hi], None, BF16, False),
        (w_in[:, v_hi:x_hi], None, F32, False),
        (w_dt, None, F32, False),
    ])
    lam_init = 0.8 - 0.6 * math.exp(-0.3 * 1)
    diff = _diff_attention(q, k, v, cd_lam[0], cd_subln_g[0], lam_init, c_len)
    y2 = _ssd(xbc, dt, cd_conv_w[0], cd_conv_b[0], cd_dt_bias[0], cd_a_log[0], cd_d_skip[0], nct_time)
    w_out = cd_w_out[0].astype(BF16)
    xl, *routed = _out_odd(xc, diff, y2, z, cd_norm_g[0], w_out[:qk], w_out[qk:], mods3, 1, c_len,
                           g_ffn[1], w_router[1], b_router[1])
    return _moe(xl, routed, mods3, 1, 0, math.gcd(LATENT_COMBINE_TILE, s_len), w_e_gate, w_e_up, w_e_down,
                ws_gate[1], ws_up[1], ws_down[1], g_final=g_final)
```

```python
import functools
import math

import jax
import jax.numpy as jnp
from jax import lax
from jax.experimental import pallas as pl
from jax.experimental.pallas import tpu as pltpu

F32 = jnp.float32
BF16 = jnp.bfloat16
HIGHEST = lax.Precision.HIGHEST

GRID_W = 64
N_MOD = 6
NORM_EPS = 1e-6
ROPE_BASE = 10000.0
CONV_W = 4

LRU_WIDTH = 512
LRU_BLOCKS = 8
LRU_C = 8.0

HEAD_DIM = 64
WIN_HEADS = 8
WIN_KV_HEADS = 2
WINDOW = 128

DIFF_HEADS = 4
DIFF_DH = 64

SSD_HEADS = 8
SSD_HEAD_DIM = 64
SSD_INNER = SSD_HEADS * SSD_HEAD_DIM
SSD_GROUPS = 2
SSD_STATE = 128
SSD_CONV_DIM = SSD_INNER + 2 * SSD_GROUPS * SSD_STATE

N_EXPERTS = 64
N_EXPERT_GROUPS = 8
TOPK_GROUPS = 4
TOP_K = 8
D_EXPERT = 256
ROUTED_SCALE = 2.5

LANES = 128
SUBLANES = 8
MOD_ROWS = 16
TIME_TILE = 128
ROW_TILE = 256
DISPATCH_TILE = 2048
LATENT_COMBINE_TILE = 512
MIXED_COMBINE_TILE = 768
SSD_BATCH_PER_STEP = 2
EXPERT_BLK = 1024
ROW_CHUNKS = 4
VMEM_LIMIT = 48 * 1024 * 1024
NEG_BIG = -1e30
LOG2E = math.log2(math.e)


def _cparams(sem):
    return pltpu.CompilerParams(dimension_semantics=sem, vmem_limit_bytes=VMEM_LIMIT)


def _nt_dot(a, b):
    return lax.dot_general(a, b, (((1,), (1,)), ((), ())), preferred_element_type=F32)


def _softplus(x):
    return jnp.maximum(x, 0.0) + jnp.log1p(jnp.exp(-jnp.abs(x)))


def _sigmoid(x):
    return 0.5 * jnp.tanh(0.5 * x) + 0.5


def _silu(x):
    return x * _sigmoid(x)


def _pack_bf16(x):
    half = x.shape[-1] // 2
    bits = pltpu.bitcast(x.astype(BF16).astype(F32), jnp.uint32)
    return bits[:, :half] | (bits[:, half:] >> 16)


def _unpack_bf16(w):
    hi = pltpu.bitcast(w & jnp.uint32(0xFFFF0000), F32)
    lo = pltpu.bitcast(w << 16, F32)
    return hi, lo


def _store_chunk_rows(ref, w):
    n = w.shape[0]
    for j in range(ROW_CHUNKS):
        ref[pl.ds(j, n, stride=ROW_CHUNKS), :] = w[:, j * LANES:(j + 1) * LANES]


def _load_chunk_rows(ref, n):
    return jnp.concatenate([ref[pl.ds(j, n, stride=ROW_CHUNKS), :] for j in range(ROW_CHUNKS)], axis=1)


def _packed_dot(w, weight_ref):
    half = w.shape[-1]
    hi, lo = _unpack_bf16(w)
    return (jnp.dot(hi.astype(BF16), weight_ref[:half, :], preferred_element_type=F32)
            + jnp.dot(lo.astype(BF16), weight_ref[half:, :], preferred_element_type=F32))


def _mod_kernel(c_ref, w_ref, b_ref, o_ref):
    c = c_ref[...]
    s = _silu(c)
    o_ref[...] = jnp.dot(s, w_ref[...], preferred_element_type=F32, precision=HIGHEST) + b_ref[...]


def _modulations(c_all, w_mod, b_mod):
    depth, d, _ = w_mod.shape
    return pl.pallas_call(
        _mod_kernel,
        grid=(depth, N_MOD),
        in_specs=[
            pl.BlockSpec((MOD_ROWS, d), lambda l, k: (0, 0)),
            pl.BlockSpec((None, d, d), lambda l, k: (l, 0, k)),
            pl.BlockSpec((None, 1, d), lambda l, k: (l, 0, k)),
        ],
        out_specs=pl.BlockSpec((None, MOD_ROWS, d), lambda l, k: (l, 0, k)),
        out_shape=jax.ShapeDtypeStruct((depth, MOD_ROWS, N_MOD * d), F32),
        compiler_params=_cparams(("arbitrary", "arbitrary")),
        name="adaln_modulation",
    )(c_all, w_mod, b_mod.reshape(depth, 1, N_MOD * d))


def _mod_spec(d, layer, chunk, row_fn):
    return pl.BlockSpec((None, 1, d), lambda b, i: (layer * MOD_ROWS + row_fn(b, i), 0, chunk))


def _norm_mod(x, g, sh, sc):
    ms = jnp.mean(x * x, axis=-1, keepdims=True)
    return (x * lax.rsqrt(ms + NORM_EPS) * g) * (1.0 + sc) + sh


def _rope(y, cos, sa, sb):
    n = y.shape[-1]
    half = HEAD_DIM // 2
    return y * cos + pltpu.roll(y, n - half, 1) * sa + pltpu.roll(y, half, 1) * sb


def _proj_kernel(*refs, ropes):
    n = len(ropes)
    x_ref, g_ref, sh_ref, sc_ref, cos_ref, sa_ref, sb_ref = refs[:7]
    w_refs = refs[7:7 + n]
    o_refs = refs[7 + n:]
    h = _norm_mod(x_ref[...], g_ref[...], sh_ref[...], sc_ref[...]).astype(BF16)
    for w_ref, o_ref, rope in zip(w_refs, o_refs, ropes):
        y = jnp.dot(h, w_ref[...], preferred_element_type=F32)
        if rope is not None:
            w = y.shape[-1]
            y = _rope(y, cos_ref[:, :w], sa_ref[:, :w], sb_ref[:, :w])
            if rope != 1.0:
                y = y * rope
        o_ref[...] = y.astype(o_ref.dtype)


def _project(xc, g, mods3, layer, nct, rope_tabs, groups):
    b, l, d = xc.shape
    tm = ROW_TILE
    mod_row = lambda i, bb: layer * MOD_ROWS + jnp.where(i < nct, SUBLANES, bb)
    rw = rope_tabs[0].shape[-1]
    in_specs = [
        pl.BlockSpec((None, tm, d), lambda i, bb: (bb, i, 0)),
        pl.BlockSpec((1, d), lambda i, bb: (0, 0)),
        pl.BlockSpec((None, 1, d), lambda i, bb: (mod_row(i, bb), 0, 0)),
        pl.BlockSpec((None, 1, d), lambda i, bb: (mod_row(i, bb), 0, 1)),
    ] + [pl.BlockSpec((tm, rw), lambda i, bb: (i, 0))] * 3
    out_specs, out_shapes = [], []
    for w, _, dt, time_major in groups:
        n = w.shape[1]
        in_specs.append(pl.BlockSpec((d, n), lambda i, bb: (0, 0)))
        if time_major:
            out_specs.append(pl.BlockSpec((tm, n), lambda i, bb: (i, bb)))
            out_shapes.append(jax.ShapeDtypeStruct((l, b * n), dt))
        else:
            out_specs.append(pl.BlockSpec((None, tm, n), lambda i, bb: (bb, i, 0)))
            out_shapes.append(jax.ShapeDtypeStruct((b, l, n), dt))
    return pl.pallas_call(
        functools.partial(_proj_kernel, ropes=tuple(gp[1] for gp in groups)),
        grid=(l // tm, b),
        in_specs=in_specs,
        out_specs=out_specs,
        out_shape=out_shapes,
        compiler_params=_cparams(("arbitrary", "arbitrary")),
        name="norm_mod_project",
    )(xc, g.reshape(1, d), mods3, mods3, *rope_tabs, *[gp[0] for gp in groups])


def _rope_tables(c_len, s_len):
    rows = s_len // GRID_W
    row = jnp.repeat(jnp.arange(rows), GRID_W).astype(F32)
    col = jnp.tile(jnp.arange(GRID_W), rows).astype(F32)
    n = HEAD_DIM // 4
    inv = ROPE_BASE ** (-jnp.arange(n, dtype=F32) / n)
    ang = jnp.concatenate([row[:, None] * inv, col[:, None] * inv], axis=-1)
    cos, sin = jnp.cos(ang), jnp.sin(ang)
    zero = jnp.zeros_like(sin)
    reps = WIN_HEADS
    cos_t = jnp.tile(jnp.concatenate([cos, cos], axis=-1), (1, reps))
    sa_t = jnp.tile(jnp.concatenate([-sin, zero], axis=-1), (1, reps))
    sb_t = jnp.tile(jnp.concatenate([zero, sin], axis=-1), (1, reps))
    w = cos_t.shape[-1]
    pad1 = jnp.ones((c_len, w), F32)
    pad0 = jnp.zeros((c_len, w), F32)
    return (jnp.concatenate([pad1, cos_t], 0), jnp.concatenate([pad0, sa_t], 0),
            jnp.concatenate([pad0, sb_t], 0))


def _seq_tile(d, g, nct, nt):
    rev = jnp.where(g < nct, nct - 1 - g, nt - 1 - (g - nct))
    return jnp.where(d == 0, g, rev)


def _rglru_kernel(x_ref, xp_ref, xn_ref, cw_ref, cb_ref, w_ref, bias_ref, lam_ref, o_ref,
                  ext_scr, a_scr, b_scr, h_scr, *, ts, nct, nt, sub):
    d = pl.program_id(0)
    g = pl.program_id(1)
    tile = _seq_tile(d, g, nct, nt)
    bsz, width = h_scr.shape
    pv = jnp.where((tile == 0) | (tile == nct), 0.0, 1.0)
    nv = jnp.where((tile == nct - 1) | (tile == nt - 1), 0.0, 1.0)
    ext_scr[0:1] = xp_ref[...] * pv
    ext_scr[1:ts + 1] = x_ref[...]
    ext_scr[ts + 1:ts + 3] = xn_ref[...] * nv

    @pl.when(g == 0)
    def _():
        h_scr[...] = jnp.zeros_like(h_scr)

    neg_sp = -LRU_C * _softplus(-lam_ref[...])

    def prep(c, carry):
        r0 = pl.multiple_of(c * sub, sub)
        e = ext_scr[pl.ds(r0, sub + CONV_W - 1)]
        u = cb_ref[...] + cw_ref[0] * e[0:sub]
        for j in range(1, CONV_W):
            u = u + cw_ref[j] * e[j:j + sub]
        u2 = u.reshape(sub * bsz, width)
        gts = jnp.dot(u2.astype(BF16), w_ref[...], preferred_element_type=F32) + bias_ref[...]
        r = _sigmoid(gts[:, :width])
        ig = _sigmoid(gts[:, width:])
        log_a = neg_sp * r
        a = jnp.exp(log_a)
        mult = jnp.sqrt(1.0 - a * a)
        a_scr[pl.ds(r0, sub)] = a.reshape(sub, bsz, width)
        b_scr[pl.ds(r0, sub)] = (mult * ig * u2).reshape(sub, bsz, width)
        return carry

    lax.fori_loop(0, ts // sub, prep, 0)

    def step(t, h):
        tt = jnp.where(d == 0, t, ts - 1 - t)
        h = a_scr[tt] * h + b_scr[tt]
        o_ref[tt] = h
        return h

    h_scr[...] = lax.fori_loop(0, ts, step, h_scr[...], unroll=8)


def _rglru(xa_tm, conv_w, conv_b, w_gates, b_gates, lam, nct):
    l, bsz, width = xa_tm.shape
    ts = TIME_TILE
    nt = l // ts
    tile = lambda d, g: _seq_tile(d, g, nct, nt)
    kern = functools.partial(_rglru_kernel, ts=ts, nct=nct, nt=nt, sub=16)
    return pl.pallas_call(
        kern,
        grid=(2, nt),
        in_specs=[
            pl.BlockSpec((ts, bsz, width), lambda d, g: (tile(d, g), 0, 0)),
            pl.BlockSpec((1, bsz, width), lambda d, g: (jnp.maximum(tile(d, g) * ts - 1, 0), 0, 0)),
            pl.BlockSpec((2, bsz, width),
                         lambda d, g: (jnp.minimum((tile(d, g) + 1) * (ts // 2), l // 2 - 1), 0, 0)),
            pl.BlockSpec((CONV_W, 1, width), lambda d, g: (0, 0, 0)),
            pl.BlockSpec((1, width), lambda d, g: (0, 0)),
            pl.BlockSpec((None, width, 2 * width), lambda d, g: (d, 0, 0)),
            pl.BlockSpec((None, 1, 2 * width), lambda d, g: (d, 0, 0)),
            pl.BlockSpec((None, 1, width), lambda d, g: (d, 0, 0)),
        ],
        out_specs=pl.BlockSpec((None, ts, bsz, width), lambda d, g: (d, tile(d, g), 0, 0)),
        out_shape=jax.ShapeDtypeStruct((2, l, bsz, width), F32),
        scratch_shapes=[
            pltpu.VMEM((ts + CONV_W - 1, bsz, width), F32),
            pltpu.VMEM((ts, bsz, width), F32),
            pltpu.VMEM((ts, bsz, width), F32),
            pltpu.VMEM((bsz, width), F32),
        ],
        compiler_params=_cparams(("arbitrary", "arbitrary")),
        name="rglru_scan",
    )(xa_tm, xa_tm, xa_tm, conv_w.reshape(CONV_W, 1, width), conv_b.reshape(1, width),
      w_gates, b_gates, lam)


def _block_diag(w):
    nb, c, dd = w.shape
    eye = jnp.eye(nb, dtype=w.dtype)
    return (eye[:, None, :, None] * w[:, :, None, :]).reshape(nb * c, nb * dd)


def _win_attn_kernel(sink_ref, q_ref, k_ref, v_ref, o_ref, *, c_len, l_len, nqc):
    j = pl.program_id(1)
    blk = q_ref.shape[0]
    grp = WIN_HEADS // WIN_KV_HEADS
    band = blk + 2 * WINDOW
    heads = [(h, slice(h * HEAD_DIM, (h + 1) * HEAD_DIM),
              slice((h // grp) * HEAD_DIM, (h // grp + 1) * HEAD_DIM)) for h in range(WIN_HEADS)]

    @pl.when(j < nqc)
    def _():
        logits = [_nt_dot(q_ref[:, hsl], k_ref[0:c_len, ksl]) for _, hsl, ksl in heads]
        probs, dens = [], []
        for (h, _, _), s in zip(heads, logits):
            sink = sink_ref[h] * LOG2E
            m = jnp.maximum(jnp.max(s, axis=-1, keepdims=True), sink)
            p = jnp.exp2(s - m)
            dens.append(jnp.sum(p, axis=-1, keepdims=True) + jnp.exp2(sink - m))
            probs.append(p.astype(BF16))
        outs = [jnp.dot(p, v_ref[0:c_len, ksl], preferred_element_type=F32) / den
                for (_, _, ksl), p, den in zip(heads, probs, dens)]
        o_ref[...] = jnp.concatenate(outs, axis=-1).astype(o_ref.dtype)

    @pl.when(j >= nqc)
    def _():
        jb = j - nqc
        start = jnp.clip(c_len + jb * blk - WINDOW, c_len - WINDOW, l_len - band)
        start = pl.multiple_of(start, WINDOW)
        qpos = jb * blk + lax.broadcasted_iota(jnp.int32, (blk, band), 0)
        kpos = start - c_len + lax.broadcasted_iota(jnp.int32, (blk, band), 1)
        valid = (jnp.abs(qpos - kpos) <= WINDOW) & (kpos >= 0)
        lc = [_nt_dot(q_ref[:, hsl], k_ref[0:c_len, ksl]) for _, hsl, ksl in heads]
        lb = [jnp.where(valid, _nt_dot(q_ref[:, hsl], k_ref[pl.ds(start, band), ksl]), NEG_BIG)
              for _, hsl, ksl in heads]
        pcs, pbs, dens = [], [], []
        for (h, _, _), sc, sb in zip(heads, lc, lb):
            sink = sink_ref[h] * LOG2E
            m = jnp.maximum(jnp.maximum(jnp.max(sc, axis=-1, keepdims=True),
                                        jnp.max(sb, axis=-1, keepdims=True)), sink)
            pc = jnp.exp2(sc - m)
            pb = jnp.exp2(sb - m)
            dens.append(jnp.sum(pc, axis=-1, keepdims=True) + jnp.sum(pb, axis=-1, keepdims=True)
                        + jnp.exp2(sink - m))
            pcs.append(pc.astype(BF16))
            pbs.append(pb.astype(BF16))
        outs = [(jnp.dot(pc, v_ref[0:c_len, ksl], preferred_element_type=F32)
                 + jnp.dot(pb, v_ref[pl.ds(start, band), ksl], preferred_element_type=F32)) / den
                for (_, _, ksl), pc, pb, den in zip(heads, pcs, pbs, dens)]
        o_ref[...] = jnp.concatenate(outs, axis=-1).astype(o_ref.dtype)


def _win_attention(q, k, v, sink, c_len):
    b, l, qw = q.shape
    kw = k.shape[-1]
    blk = ROW_TILE
    kern = functools.partial(_win_attn_kernel, c_len=c_len, l_len=l, nqc=c_len // blk)
    return pl.pallas_call(
        kern,
        grid=(b, l // blk),
        in_specs=[
            pl.BlockSpec(memory_space=pltpu.SMEM),
            pl.BlockSpec((None, blk, qw), lambda bb, j: (bb, j, 0)),
            pl.BlockSpec((None, l, kw), lambda bb, j: (bb, 0, 0)),
            pl.BlockSpec((None, l, kw), lambda bb, j: (bb, 0, 0)),
        ],
        out_specs=pl.BlockSpec((None, blk, qw), lambda bb, j: (bb, j, 0)),
        out_shape=jax.ShapeDtypeStruct((b, l, qw), BF16),
        compiler_params=_cparams(("arbitrary", "arbitrary")),
        name="window_attention",
    )(sink, q, k, v)


def _out_even_kernel(x_ref, rec_ref, gate_ref, att_ref, wa_ref, wb_ref, g1_ref, *refs):
    route_in, o_ref, route_out = refs[:5], refs[5], refs[6:]
    lru = (rec_ref[0] + rec_ref[1]) * jax.nn.gelu(gate_ref[...].astype(F32))
    y = (jnp.dot(lru.astype(BF16), wa_ref[...], preferred_element_type=F32)
         + jnp.dot(att_ref[...], wb_ref[...], preferred_element_type=F32))
    x = x_ref[...] + g1_ref[...] * y
    o_ref[...] = x
    _route(x, *route_in, *route_out)


def _out_even(xc, rec2, gate, att, w_a, w_b, mods3, layer, nct, g_ffn, w_router, b_router):
    b, l, d = xc.shape
    tm = ROW_TILE
    w = gate.shape[-1]
    row = lambda bb, i: jnp.where(i < nct, SUBLANES, bb)
    r_in, r_args, r_out, r_shapes, r_scratch = _route_plumbing(b, l, d, g_ffn, mods3, layer, row, w_router, b_router)
    return pl.pallas_call(
        _out_even_kernel,
        grid=(b, l // tm),
        in_specs=[
            pl.BlockSpec((None, tm, d), lambda bb, i: (bb, i, 0)),
            pl.BlockSpec((2, tm, w), lambda bb, i: (0, i, bb)),
            pl.BlockSpec((None, tm, w), lambda bb, i: (bb, i, 0)),
            pl.BlockSpec((None, tm, att.shape[-1]), lambda bb, i: (bb, i, 0)),
            pl.BlockSpec(w_a.shape, lambda bb, i: (0, 0)),
            pl.BlockSpec(w_b.shape, lambda bb, i: (0, 0)),
            _mod_spec(d, layer, 2, row),
        ] + r_in,
        out_specs=[pl.BlockSpec((None, tm, d), lambda bb, i: (bb, i, 0))] + r_out,
        out_shape=[jax.ShapeDtypeStruct((b, l, d), F32)] + r_shapes,
        scratch_shapes=r_scratch,
        compiler_params=_cparams(("arbitrary", "arbitrary")),
        name="out_proj_even",
    )(xc, rec2, gate, att, w_a, w_b, mods3, *r_args)


def _diff_attn_kernel(lam_ref, g_ref, q_ref, k_ref, v_ref, o_ref, *, lam_init):
    lv = lam_ref[...]
    lam = (jnp.exp(jnp.sum(lv[0:1] * lv[1:2], axis=-1, keepdims=True))
           - jnp.exp(jnp.sum(lv[2:3] * lv[3:4], axis=-1, keepdims=True)) + lam_init)
    vw = 2 * DIFF_DH

    def logits(h, mp):
        lo = h * vw + mp * DIFF_DH
        return _nt_dot(q_ref[:, lo:lo + DIFF_DH], k_ref[:, lo:lo + DIFF_DH])

    def softmax_parts(s):
        e = jnp.exp2(s - jnp.max(s, axis=-1, keepdims=True))
        return e, 1.0 / jnp.sum(e, axis=-1, keepdims=True)

    heads = range(DIFF_HEADS)
    ls = [(logits(h, 0), logits(h, 1)) for h in heads]
    ws = []
    for l0, l1 in ls:
        e0, r0 = softmax_parts(l0)
        e1, r1 = softmax_parts(l1)
        ws.append(((e0 - e1 * (lam * r1 / r0)).astype(BF16), r0))
    for h, (w, r0) in zip(heads, ws):
        lo = h * vw
        o = jnp.dot(w, v_ref[:, lo:lo + vw], preferred_element_type=F32) * r0
        ms = jnp.mean(o * o, axis=-1, keepdims=True)
        o = o * lax.rsqrt(ms + NORM_EPS) * g_ref[...]
        o_ref[:, lo:lo + vw] = (o * (1.0 - lam_init)).astype(o_ref.dtype)


def _diff_attention(q, k, v, lam_vecs, subln_g, lam_init, c_len):
    b, l, w = q.shape
    tq = ROW_TILE
    s_len = l - c_len
    off = c_len // tq
    return pl.pallas_call(
        functools.partial(_diff_attn_kernel, lam_init=lam_init),
        grid=(b, s_len // tq),
        in_specs=[
            pl.BlockSpec(lam_vecs.shape, lambda bb, j: (0, 0)),
            pl.BlockSpec((1, 2 * DIFF_DH), lambda bb, j: (0, 0)),
            pl.BlockSpec((None, tq, w), lambda bb, j: (bb, j + off, 0)),
            pl.BlockSpec((None, l, w), lambda bb, j: (bb, 0, 0)),
            pl.BlockSpec((None, l, w), lambda bb, j: (bb, 0, 0)),
        ],
        out_specs=pl.BlockSpec((None, tq, w), lambda bb, j: (bb, j, 0)),
        out_shape=jax.ShapeDtypeStruct((b, s_len, w), BF16),
        compiler_params=_cparams(("arbitrary", "arbitrary")),
        name="diff_attention",
    )(lam_vecs, subln_g.reshape(1, -1), q, k, v)


def _ssd_kernel(x_ref, xp_ref, xn_ref, dt_ref, cw_ref, cb_ref, dtb_ref, alog_ref, dsk_ref, o_ref,
                ext_scr, st_scr, *, q, nct, nt):
    d = pl.program_id(0)
    g = pl.program_id(2)
    tile = _seq_tile(d, g, nct, nt)

    @pl.when(g == 0)
    def _():
        st_scr[...] = jnp.zeros_like(st_scr)

    for bi in range(x_ref.shape[0]):
        _ssd_chunk(d, tile, x_ref.at[bi], xp_ref.at[bi], xn_ref.at[bi], dt_ref.at[bi], cw_ref, cb_ref, dtb_ref,
                   alog_ref, dsk_ref, o_ref.at[bi], ext_scr.at[bi],
                   st_scr.at[pl.ds(bi * SSD_HEADS, SSD_HEADS)], q=q, nct=nct, nt=nt)


def _ssd_chunk(d, tile, x_ref, xp_ref, xn_ref, dt_ref, cw_ref, cb_ref, dtb_ref, alog_ref, dsk_ref, o_ref,
               ext_scr, st_scr, *, q, nct, nt):
    pv = jnp.where((tile == 0) | (tile == nct), 0.0, 1.0)
    nv = jnp.where((tile == nct - 1) | (tile == nt - 1), 0.0, 1.0)
    ext_scr[0:SUBLANES] = xp_ref[...] * pv
    ext_scr[SUBLANES:SUBLANES + q] = x_ref[...]
    ext_scr[SUBLANES + q:2 * SUBLANES + q] = xn_ref[...] * nv

    u = cb_ref[...] + cw_ref[0] * ext_scr[SUBLANES - 1:SUBLANES - 1 + q, :]
    for j in range(1, CONV_W):
        u = u + cw_ref[j] * ext_scr[SUBLANES - 1 + j:SUBLANES - 1 + j + q, :]
    act = _silu(u)

    dtr = dt_ref[...]
    dtr = jnp.where(d == 0, dtr, pltpu.roll(dtr, LANES - SSD_HEADS, 1))
    dtv = _softplus(dtr + dtb_ref[...])
    head_lane = lax.broadcasted_iota(jnp.int32, (1, LANES), 1) < SSD_HEADS
    dta = dtv * jnp.where(head_lane, -jnp.exp(alog_ref[...]) * LOG2E, 0.0)
    ri = lax.broadcasted_iota(jnp.int32, (q, q), 0)
    ci = lax.broadcasted_iota(jnp.int32, (q, q), 1)
    keep = jnp.where(d == 0, ri - ci, ci - ri) >= 0
    keep_b = jnp.where(keep, 1.0, 0.0).astype(BF16)
    cum = jnp.zeros((q, LANES), F32)
    rest = dta
    for _ in range(3):
        part = rest.astype(BF16)
        cum = cum + jnp.dot(keep_b, part, preferred_element_type=F32)
        rest = rest - part.astype(F32)
    tot = jnp.sum(dta, axis=0, keepdims=True)
    cum_t = cum.T
    dt_t = dtv.T
    to_end = jnp.exp2(tot - cum) * dtv
    e_cum = jnp.exp2(cum)
    e_tot = jnp.exp2(tot)
    dskip = dsk_ref[...] * jnp.where(d == 0, 1.0, 0.0)

    def spread(v, width):
        rows = lax.broadcasted_iota(jnp.int32, (LANES, SSD_HEADS * width), 0)
        cols = lax.broadcasted_iota(jnp.int32, (LANES, SSD_HEADS * width), 1)
        pick = jnp.where(rows * width <= cols, jnp.where(cols < (rows + 1) * width, 1.0, 0.0), 0.0).astype(BF16)
        out = jnp.zeros((q, SSD_HEADS * width), F32)
        rest = v
        for _ in range(3):
            part = rest.astype(BF16)
            out = out + jnp.dot(part, pick, preferred_element_type=F32)
            rest = rest - part.astype(F32)
        return out

    e_cum_b = spread(e_cum, SSD_HEAD_DIM)
    to_end_b = spread(to_end, SSD_HEAD_DIM)

    hpg = SSD_HEADS // SSD_GROUPS
    for gi in range(SSD_GROUPS):
        b_g = act[:, SSD_INNER + gi * SSD_STATE:SSD_INNER + (gi + 1) * SSD_STATE]
        c_lo = SSD_INNER + SSD_GROUPS * SSD_STATE + gi * SSD_STATE
        c_g = act[:, c_lo:c_lo + SSD_STATE].astype(BF16)
        cb = _nt_dot(c_g, b_g.astype(BF16))
        b_gt = b_g.T.astype(BF16)
        for hh in range(hpg):
            h = gi * hpg + hh
            xs = act[:, h * SSD_HEAD_DIM:(h + 1) * SSD_HEAD_DIM]
            seg = cum[:, h:h + 1] - cum_t[h:h + 1, :]
            decay = jnp.exp2(jnp.where(keep, seg, NEG_BIG))
            w = (cb * decay * dt_t[h:h + 1, :]).astype(BF16)
            state = st_scr[h]
            y = jnp.dot(w, xs.astype(BF16), preferred_element_type=F32)
            y = y + (jnp.dot(c_g, state.astype(BF16), preferred_element_type=F32)
                     * e_cum_b[:, h * SSD_HEAD_DIM:(h + 1) * SSD_HEAD_DIM])
            y = y + dskip[:, h * SSD_HEAD_DIM:(h + 1) * SSD_HEAD_DIM] * xs
            o_ref[:, h * SSD_HEAD_DIM:(h + 1) * SSD_HEAD_DIM] = y
            s_new = jnp.dot(b_gt, (xs * to_end_b[:, h * SSD_HEAD_DIM:(h + 1) * SSD_HEAD_DIM]).astype(BF16),
                            preferred_element_type=F32)
            st_scr[h] = e_tot[:, h:h + 1] * state + s_new


def _ssd(xbc, dt, conv_w, conv_b, dt_bias, a_log, d_skip, nct):
    b, l, cd = xbc.shape
    q = TIME_TILE
    nt = l // q
    nb = SSD_BATCH_PER_STEP
    tile = lambda d, bb, g: _seq_tile(d, g, nct, nt)
    r8 = q // SUBLANES
    pad = LANES - SSD_HEADS
    dtb = jnp.pad(dt_bias, ((0, 0), (0, pad))).reshape(2, 1, LANES)
    alog = jnp.pad(a_log, ((0, 0), (0, pad))).reshape(2, 1, LANES)
    dsk = jnp.repeat(d_skip, SSD_HEAD_DIM).reshape(1, SSD_INNER)
    return pl.pallas_call(
        functools.partial(_ssd_kernel, q=q, nct=nct, nt=nt),
        grid=(2, b // nb, nt),
        in_specs=[
            pl.BlockSpec((nb, q, cd), lambda d, bb, g: (bb, tile(d, bb, g), 0)),
            pl.BlockSpec((nb, SUBLANES, cd),
                         lambda d, bb, g: (bb, jnp.maximum(tile(d, bb, g) * r8 - 1, 0), 0)),
            pl.BlockSpec((nb, SUBLANES, cd),
                         lambda d, bb, g: (bb, jnp.minimum((tile(d, bb, g) + 1) * r8, l // SUBLANES - 1), 0)),
            pl.BlockSpec((nb, q, LANES), lambda d, bb, g: (bb, tile(d, bb, g), 0)),
            pl.BlockSpec((CONV_W, 1, cd), lambda d, bb, g: (0, 0, 0)),
            pl.BlockSpec((1, cd), lambda d, bb, g: (0, 0)),
            pl.BlockSpec((None, 1, LANES), lambda d, bb, g: (d, 0, 0)),
            pl.BlockSpec((None, 1, LANES), lambda d, bb, g: (d, 0, 0)),
            pl.BlockSpec((1, SSD_INNER), lambda d, bb, g: (0, 0)),
        ],
        out_specs=pl.BlockSpec((None, nb, q, SSD_INNER), lambda d, bb, g: (d, bb, tile(d, bb, g), 0)),
        out_shape=jax.ShapeDtypeStruct((2, b, l, SSD_INNER), F32),
        scratch_shapes=[
            pltpu.VMEM((nb, q + 2 * SUBLANES, cd), F32),
            pltpu.VMEM((nb * SSD_HEADS, SSD_STATE, SSD_HEAD_DIM), F32),
        ],
        compiler_params=_cparams(("arbitrary", "arbitrary", "arbitrary")),
        name="ssd_chunked",
    )(xbc, xbc, xbc, dt, conv_w.reshape(CONV_W, 1, cd), conv_b.reshape(1, cd), dtb, alog, dsk)


def _out_odd_kernel(x_ref, diff_ref, y_ref, z_ref, ng_ref, wa_ref, wb_ref, g1_ref, *refs):
    route_in, o_ref, route_out = refs[:5], refs[5], refs[6:]
    yz = (y_ref[0] + y_ref[1]) * _silu(z_ref[...].astype(F32))
    gs = SSD_INNER // SSD_GROUPS
    parts = []
    for gi in range(SSD_GROUPS):
        seg = yz[:, gi * gs:(gi + 1) * gs]
        ms = jnp.mean(seg * seg, axis=-1, keepdims=True)
        parts.append(seg * lax.rsqrt(ms + NORM_EPS) * ng_ref[:, gi * gs:(gi + 1) * gs])
    ssd = jnp.concatenate(parts, axis=-1).astype(BF16)
    y = (jnp.dot(diff_ref[...], wa_ref[...], preferred_element_type=F32)
         + jnp.dot(ssd, wb_ref[...], preferred_element_type=F32))
    x = x_ref[...] + g1_ref[...] * y
    o_ref[...] = x
    _route(x, *route_in, *route_out)


def _out_odd(xc, diff, y2, z, norm_g, w_a, w_b, mods3, layer, c_len, g_ffn, w_router, b_router):
    b, l, d = xc.shape
    s_len = l - c_len
    tm = ROW_TILE
    off = c_len // tm
    w = SSD_INNER
    row = lambda bb, i: bb
    r_in, r_args, r_out, r_shapes, r_scratch = _route_plumbing(b, s_len, d, g_ffn, mods3, layer, row, w_router,
                                                               b_router)
    return pl.pallas_call(
        _out_odd_kernel,
        grid=(b, s_len // tm),
        in_specs=[
            pl.BlockSpec((None, tm, d), lambda bb, i: (bb, i + off, 0)),
            pl.BlockSpec((None, tm, diff.shape[-1]), lambda bb, i: (bb, i, 0)),
            pl.BlockSpec((2, None, tm, w), lambda bb, i: (0, bb, i + off, 0)),
            pl.BlockSpec((None, tm, w), lambda bb, i: (bb, i + off, 0)),
            pl.BlockSpec((1, w), lambda bb, i: (0, 0)),
            pl.BlockSpec(w_a.shape, lambda bb, i: (0, 0)),
            pl.BlockSpec(w_b.shape, lambda bb, i: (0, 0)),
            _mod_spec(d, layer, 2, row),
        ] + r_in,
        out_specs=[pl.BlockSpec((None, tm, d), lambda bb, i: (bb, i, 0))] + r_out,
        out_shape=[jax.ShapeDtypeStruct((b, s_len, d), F32)] + r_shapes,
        scratch_shapes=r_scratch,
        compiler_params=_cparams(("arbitrary", "arbitrary")),
        name="out_proj_odd",
    )(xc, diff, y2, z, norm_g.reshape(1, w), w_a, w_b, mods3, *r_args)


def _route(x, g_ref, sh_ref, sc_ref, wr_ref, br_ref, h_ref, eid_ref, rnk_ref, gate_ref, cnt_ref, carry_scr):
    @pl.when((pl.program_id(0) == 0) & (pl.program_id(1) == 0))
    def _():
        carry_scr[...] = jnp.zeros_like(carry_scr)

    h = _norm_mod(x, g_ref[...], sh_ref[...], sc_ref[...])
    _store_chunk_rows(h_ref, _pack_bf16(h))
    tm = h.shape[0]
    per = N_EXPERTS // N_EXPERT_GROUPS
    h_hi = h.astype(BF16)
    h_lo = (h - h_hi.astype(F32)).astype(BF16)
    logits = _nt_dot(wr_ref[0], h_hi) + _nt_dot(wr_ref[0], h_lo) + _nt_dot(wr_ref[1], h_hi)
    scores = jax.nn.sigmoid(logits)
    sel = scores + br_ref[...]
    sel3 = sel.reshape(N_EXPERT_GROUPS, per, tm)
    kio = lax.broadcasted_iota(jnp.int32, sel3.shape, 1)
    m1 = jnp.max(sel3, axis=1, keepdims=True)
    first = jnp.min(jnp.where(sel3 == m1, kio, per), axis=1, keepdims=True)
    m2 = jnp.max(jnp.where(kio == first, NEG_BIG, sel3), axis=1, keepdims=True)
    gs = m1 + m2
    gio = lax.broadcasted_iota(jnp.int32, gs.shape, 0)
    ahead = jnp.zeros(gs.shape, jnp.int32)
    for gp in range(N_EXPERT_GROUPS):
        other = gs[gp:gp + 1]
        ahead = ahead + jnp.where((other > gs) | ((other == gs) & (gp < gio)), 1, 0)
    grp_on = jnp.where(ahead < TOPK_GROUPS, 1.0, 0.0)
    selm = jnp.where(jnp.broadcast_to(grp_on, sel3.shape) > 0.5, sel3, NEG_BIG).reshape(N_EXPERTS, tm)
    eio = lax.broadcasted_iota(jnp.int32, selm.shape, 0)
    work = selm
    cf = jnp.zeros(selm.shape, F32)
    e_rows, s_rows = [], []
    for k in range(TOP_K):
        best = jnp.max(work, axis=0, keepdims=True)
        idx = jnp.min(jnp.where(work == best, eio, N_EXPERTS), axis=0, keepdims=True)
        hit = eio == idx
        cf = cf + jnp.where(hit, 1.0, 0.0)
        work = jnp.where(hit, NEG_BIG, work)
        e_rows.append(idx)
        s_rows.append(jnp.sum(jnp.where(hit, scores, 0.0), axis=0, keepdims=True))
    denom = s_rows[0]
    for s_k in s_rows[1:]:
        denom = denom + s_k
    g_rows = [s_k / denom * ROUTED_SCALE for s_k in s_rows]
    ti = lax.broadcasted_iota(jnp.int32, (tm, tm), 0)
    tj = lax.broadcasted_iota(jnp.int32, (tm, tm), 1)
    before = jnp.where(ti < tj, 1.0, 0.0).astype(BF16)
    in_expert = carry_scr[:, 0:1] + jnp.dot(cf.astype(BF16), before, preferred_element_type=F32)
    carry_scr[...] = carry_scr[...] + jnp.sum(cf, axis=1, keepdims=True)
    cnt_ref[...] = carry_scr[...]
    r_rows = [jnp.sum(jnp.where(eio == idx, in_expert, 0.0), axis=0, keepdims=True) for idx in e_rows]
    eid_ref[...] = jnp.concatenate(e_rows, axis=0)
    rnk_ref[...] = jnp.concatenate(r_rows, axis=0).astype(jnp.int32)
    padded = jnp.concatenate(g_rows + [jnp.zeros((LANES - TOP_K, tm), F32)], axis=0)
    gate_ref[...] = padded.T


def _route_plumbing(b, r, d, g, mods3, layer, row_fn, w_router, b_router):
    tm = ROW_TILE
    nt = r // tm
    w_t = w_router.T
    w_hi = w_t.astype(BF16)
    w_router_t = jnp.stack([w_hi, (w_t - w_hi.astype(F32)).astype(BF16)])
    slot = pl.BlockSpec((TOP_K, tm), lambda bb, i: (0, bb * nt + i))
    slot_shape = jax.ShapeDtypeStruct((TOP_K, b * r), jnp.int32)
    in_specs = [
        pl.BlockSpec((1, d), lambda bb, i: (0, 0)),
        _mod_spec(d, layer, 3, row_fn),
        _mod_spec(d, layer, 4, row_fn),
        pl.BlockSpec(w_router_t.shape, lambda bb, i: (0, 0, 0)),
        pl.BlockSpec((N_EXPERTS, 1), lambda bb, i: (0, 0)),
    ]
    args = [g.reshape(1, d), mods3, mods3, w_router_t, b_router.reshape(N_EXPERTS, 1)]
    out_specs = [
        pl.BlockSpec((tm * ROW_CHUNKS, LANES), lambda bb, i: (bb * nt + i, 0)),
        slot,
        slot,
        pl.BlockSpec((None, tm, LANES), lambda bb, i: (bb, i, 0)),
        pl.BlockSpec((N_EXPERTS, LANES), lambda bb, i: (0, 0)),
    ]
    out_shapes = [jax.ShapeDtypeStruct((b * r * ROW_CHUNKS, LANES), jnp.uint32), slot_shape, slot_shape,
                  jax.ShapeDtypeStruct((b, r, LANES), F32), jax.ShapeDtypeStruct((N_EXPERTS, LANES), F32)]
    scratch = [pltpu.VMEM((N_EXPERTS, LANES), F32)]
    return in_specs, args, out_specs, out_shapes, scratch


def _moe_plan(counts, n_rows):
    blk = EXPERT_BLK
    nb = n_rows // blk
    ends = jnp.cumsum(counts)
    starts = ends - counts
    count_le = lambda sorted_vals, q: jnp.sum(sorted_vals[None, :] <= q[:, None], axis=1, dtype=jnp.int32)
    first = jnp.arange(nb, dtype=jnp.int32) * blk
    e_lo = count_le(ends, first)
    e_hi = count_le(ends, first + (blk - 1))
    n_pair = e_hi - e_lo + 1
    p_end = jnp.cumsum(n_pair)
    p_start = p_end - n_pair
    i = jnp.arange(nb + N_EXPERTS - 1, dtype=jnp.int32)
    j = jnp.minimum(count_le(p_end, i), nb - 1)
    valid = i < p_end[-1]
    e = jnp.where(valid, e_lo[j] + i - p_start[j], e_hi[nb - 1]).astype(jnp.int32)
    bounds = jnp.concatenate([starts, ends[-1:]]).astype(jnp.int32)
    return j, e, valid.astype(jnp.int32), bounds


def _positions_kernel(starts_ref, eid_ref, rnk_ref, pos_ref):
    eid = eid_ref[...]
    pos = rnk_ref[...]
    for e in range(N_EXPERTS):
        pos = pos + jnp.where(eid == e, starts_ref[e], 0)
    pos_ref[...] = pos * ROW_CHUNKS


def _positions(eid, rnk, starts):
    full = pl.BlockSpec(eid.shape, lambda: (0, 0))
    return pl.pallas_call(
        _positions_kernel,
        in_specs=[pl.BlockSpec(memory_space=pltpu.SMEM), full, full],
        out_specs=full,
        out_shape=jax.ShapeDtypeStruct(eid.shape, jnp.int32),
        compiler_params=pltpu.CompilerParams(vmem_limit_bytes=VMEM_LIMIT),
        name="moe_positions",
    )(starts, eid, rnk)


def _token_row(ref, first):
    return ref.at[pl.ds(pl.multiple_of(first, ROW_CHUNKS), ROW_CHUNKS)]


def _dispatch_kernel(pos_ref, h_ref, xs_ref, sem):
    tm = h_ref.shape[0] // ROW_CHUNKS

    def issue(t, carry):
        src = _token_row(h_ref, t * ROW_CHUNKS)
        for k in range(TOP_K):
            pltpu.make_async_copy(src, _token_row(xs_ref, pos_ref[k, t]), sem).start(priority=k % 2)
        return carry

    lax.fori_loop(0, tm, issue, 0)
    done = pl.ds(0, tm * ROW_CHUNKS)
    for _ in range(TOP_K):
        pltpu.make_async_copy(h_ref.at[done], xs_ref.at[done], sem).wait()


def _dispatch(h2, pos):
    rows, w = h2.shape
    tm = math.gcd(DISPATCH_TILE, rows // ROW_CHUNKS)
    return pl.pallas_call(
        _dispatch_kernel,
        grid=(rows // (tm * ROW_CHUNKS),),
        in_specs=[
            pl.BlockSpec((TOP_K, tm), lambda i: (0, i), memory_space=pltpu.SMEM),
            pl.BlockSpec((tm * ROW_CHUNKS, w), lambda i: (i, 0)),
        ],
        out_specs=pl.BlockSpec(memory_space=pl.ANY),
        out_shape=jax.ShapeDtypeStruct((rows * TOP_K, w), h2.dtype),
        scratch_shapes=[pltpu.SemaphoreType.DMA],
        compiler_params=_cparams(("arbitrary",)),
        name="moe_dispatch",
    )(pos, h2)


def _grouped_kernel(pb_ref, pe_ref, pv_ref, bnd_ref, xs_ref, wg_ref, wu_ref, wd_ref, y_ref, wgb, wub, wdb):
    i = pl.program_id(0)
    prev = jnp.maximum(i - 1, 0)
    j = pb_ref[i]
    e = pe_ref[i]
    blk = xs_ref.shape[0] // ROW_CHUNKS

    @pl.when((i == 0) | (pb_ref[prev] != j))
    def _():
        y_ref[...] = jnp.zeros_like(y_ref)

    @pl.when((i == 0) | (pe_ref[prev] != e))
    def _():
        wgb[...] = wg_ref[...].astype(BF16)
        wub[...] = wu_ref[...].astype(BF16)
        wdb[...] = wd_ref[...].astype(BF16)

    @pl.when(pv_ref[i] == 1)
    def _():
        xw = _load_chunk_rows(xs_ref, blk)
        a = _packed_dot(xw, wgb)
        u = _packed_dot(xw, wub)
        yv = jnp.dot((_silu(a) * u).astype(BF16), wdb[...], preferred_element_type=F32)
        rows = j * blk + lax.broadcasted_iota(jnp.int32, (blk, 1), 0)
        own = (rows >= bnd_ref[e]) & (rows < bnd_ref[e + 1])
        yw = _pack_bf16(yv)
        for c in range(ROW_CHUNKS):
            sl = pl.ds(c, blk, stride=ROW_CHUNKS)
            y_ref[sl, :] = jnp.where(own, yw[:, c * LANES:(c + 1) * LANES], y_ref[sl, :])


def _grouped(pb, pe, pv, bounds, xs, wg, wu, wd, layer):
    p, half = xs.shape
    d = 2 * ROW_CHUNKS * LANES
    blk = EXPERT_BLK * ROW_CHUNKS
    grid_spec = pltpu.PrefetchScalarGridSpec(
        num_scalar_prefetch=4,
        grid=(pb.shape[0],),
        in_specs=[
            pl.BlockSpec((blk, half), lambda i, pb, pe, pv, bnd: (pb[i], 0)),
            pl.BlockSpec((None, None, d, D_EXPERT), lambda i, pb, pe, pv, bnd: (layer, pe[i], 0, 0)),
            pl.BlockSpec((None, None, d, D_EXPERT), lambda i, pb, pe, pv, bnd: (layer, pe[i], 0, 0)),
            pl.BlockSpec((None, None, D_EXPERT, d), lambda i, pb, pe, pv, bnd: (layer, pe[i], 0, 0)),
        ],
        out_specs=pl.BlockSpec((blk, half), lambda i, pb, pe, pv, bnd: (pb[i], 0)),
        scratch_shapes=[
            pltpu.VMEM((d, D_EXPERT), BF16),
            pltpu.VMEM((d, D_EXPERT), BF16),
            pltpu.VMEM((D_EXPERT, d), BF16),
        ],
    )
    return pl.pallas_call(
        _grouped_kernel,
        grid_spec=grid_spec,
        out_shape=jax.ShapeDtypeStruct((p, half), jnp.uint32),
        compiler_params=_cparams(("arbitrary",)),
        name="moe_grouped_experts",
    )(pb, pe, pv, bounds, xs, wg, wu, wd)


def _combine_kernel(*refs, final, c_len):
    pos_ref, y_ref, gate_ref, h_ref, x_ref, g2c_ref, g2l_ref, sg_ref, su_ref, sd_ref = refs[:10]
    o_ref, buf, sem = refs[-3:]
    tm = x_ref.shape[0]

    def issue(t, carry):
        for k in range(TOP_K):
            pltpu.make_async_copy(_token_row(y_ref, pos_ref[k, t]), _token_row(buf.at[k], t * ROW_CHUNKS),
                                  sem).start(priority=k % 2)
        return carry

    lax.fori_loop(0, tm, issue, 0)
    hw = _load_chunk_rows(h_ref, tm)
    a = _packed_dot(hw, sg_ref)
    u = _packed_dot(hw, su_ref)
    acc = jnp.dot((_silu(a) * u).astype(BF16), sd_ref[...], preferred_element_type=F32)
    done = pl.ds(0, tm * ROW_CHUNKS)
    for k in range(TOP_K):
        pltpu.make_async_copy(y_ref.at[done], buf.at[k, done], sem).wait()
    g = gate_ref[...]
    half = hw.shape[-1]
    acc_hi = acc[:, :half]
    acc_lo = acc[:, half:]
    for k in range(TOP_K):
        hi, lo = _unpack_bf16(_load_chunk_rows(buf.at[k], tm))
        acc_hi = acc_hi + g[:, k:k + 1] * hi
        acc_lo = acc_lo + g[:, k:k + 1] * lo
    is_ctx = pl.program_id(1) * tm + lax.broadcasted_iota(jnp.int32, (tm, 1), 0) < c_len
    g2 = jnp.where(is_ctx, g2c_ref[...], g2l_ref[...])
    x = x_ref[...] + g2 * jnp.concatenate([acc_hi, acc_lo], axis=-1)
    if final:
        gf_ref = refs[10]
        ms = jnp.mean(x * x, axis=-1, keepdims=True)
        x = x * lax.rsqrt(ms + NORM_EPS) * gf_ref[...]
    o_ref[...] = x


def _combine(pos, y, gates, h2, x, mods3, layer, c_len, tm, sg, su, sd, g_final=None):
    b, r, d = x.shape
    nt = r // tm
    tile = pl.BlockSpec((None, tm, d), lambda bb, i: (bb, i, 0))
    in_specs = [
        pl.BlockSpec((TOP_K, tm), lambda bb, i: (0, bb * nt + i), memory_space=pltpu.SMEM),
        pl.BlockSpec(memory_space=pl.ANY),
        pl.BlockSpec((None, tm, LANES), lambda bb, i: (bb, i, 0)),
        pl.BlockSpec((tm * ROW_CHUNKS, LANES), lambda bb, i: (bb * nt + i, 0)),
        tile,
        _mod_spec(d, layer, 5, lambda bb, i: SUBLANES),
        _mod_spec(d, layer, 5, lambda bb, i: bb),
        pl.BlockSpec(sg.shape, lambda bb, i: (0, 0)),
        pl.BlockSpec(su.shape, lambda bb, i: (0, 0)),
        pl.BlockSpec(sd.shape, lambda bb, i: (0, 0)),
    ]
    args = [pos, y, gates, h2, x, mods3, mods3, sg, su, sd]
    if g_final is not None:
        in_specs.append(pl.BlockSpec((1, d), lambda bb, i: (0, 0)))
        args.append(g_final.reshape(1, d))
    return pl.pallas_call(
        functools.partial(_combine_kernel, final=g_final is not None, c_len=c_len),
        grid=(b, nt),
        in_specs=in_specs,
        out_specs=tile,
        out_shape=jax.ShapeDtypeStruct((b, r, d), F32),
        scratch_shapes=[pltpu.VMEM((TOP_K, tm * ROW_CHUNKS, LANES), jnp.uint32), pltpu.SemaphoreType.DMA],
        compiler_params=_cparams(("arbitrary", "arbitrary")),
        name="moe_combine",
    )(*args)


def _moe(x, routed, mods3, layer, c_len, combine_tile, w_e_gate, w_e_up, w_e_down, ws_gate, ws_up, ws_down,
         g_final=None):
    b, r, d = x.shape
    h2, eid, rnk, gates, cnt = routed
    pb, pe, pv, bounds = _moe_plan(cnt[:, 0].astype(jnp.int32), b * r * TOP_K)
    pos = _positions(eid, rnk, bounds[:N_EXPERTS])
    xs = _dispatch(h2, pos)
    y = _grouped(pb, pe, pv, bounds, xs, w_e_gate, w_e_up, w_e_down, layer)
    return _combine(pos, y, gates, h2, x, mods3, layer, c_len, combine_tile,
                    ws_gate.astype(BF16), ws_up.astype(BF16), ws_down.astype(BF16), g_final)


def kernel(x, c, ctx, c_ctx, w_mod, b_mod, g_mix, g_ffn, g_final, ab_w_in, ab_w_out, ab_conv_w, ab_conv_b, ab_w_r, ab_b_r, ab_w_i, ab_b_i, ab_lam, ab_sink, cd_w_in, cd_w_out, cd_lam, cd_subln_g, cd_conv_w, cd_conv_b, cd_dt_bias, cd_a_log, cd_d_skip, cd_norm_g, w_router, b_router, w_e_gate, w_e_up, w_e_down, ws_gate, ws_up, ws_down):
    bsz, s_len, d = x.shape
    c_len = ctx.shape[1]
    depth = w_mod.shape[0]
    assert depth == 2 and bsz == SUBLANES, "kernels are specialised to depth 2 and batch 8"
    assert c_len % ROW_TILE == 0 and s_len % ROW_TILE == 0
    nct_row = c_len // ROW_TILE
    nct_time = c_len // TIME_TILE

    c_all = jnp.concatenate([c, c_ctx[None], jnp.zeros((MOD_ROWS - bsz - 1, d), F32)], axis=0)
    mods3 = _modulations(c_all, w_mod, b_mod).reshape(depth * MOD_ROWS, 1, N_MOD * d)
    rope_tabs = _rope_tables(c_len, s_len)
    xc = jnp.concatenate([ctx, x], axis=1)

    w_in = ab_w_in[0].astype(BF16)
    q_hi = LRU_WIDTH + WIN_HEADS * HEAD_DIM
    x_hi = q_hi + LRU_WIDTH
    k_hi = x_hi + WIN_KV_HEADS * HEAD_DIM
    gate, q, xa, k, v = _project(xc, g_mix[0], mods3, 0, nct_row, rope_tabs, [
        (w_in[:, :LRU_WIDTH], None, BF16, False),
        (w_in[:, LRU_WIDTH:q_hi], HEAD_DIM ** -0.5 * LOG2E, BF16, False),
        (w_in[:, q_hi:x_hi], None, F32, True),
        (w_in[:, x_hi:k_hi], 1.0, BF16, False),
        (w_in[:, k_hi:], None, BF16, False),
    ])
    l_len = c_len + s_len
    w_gates = jnp.stack([jnp.concatenate([_block_diag(ab_w_r[0, dd]), _block_diag(ab_w_i[0, dd])], axis=1)
                         for dd in range(2)]).astype(BF16)
    b_gates = jnp.concatenate([ab_b_r[0], ab_b_i[0]], axis=-1).reshape(2, 1, 2 * LRU_WIDTH)
    rec = _rglru(xa.reshape(l_len, bsz, LRU_WIDTH), ab_conv_w[0], ab_conv_b[0], w_gates, b_gates,
                 ab_lam[0].reshape(2, 1, LRU_WIDTH), nct_time)
    att = _win_attention(q, k, v, ab_sink[0], c_len)
    w_out = ab_w_out[0].astype(BF16)
    xc, *routed = _out_even(xc, rec.reshape(2, l_len, bsz * LRU_WIDTH), gate, att, w_out[:LRU_WIDTH],
                            w_out[LRU_WIDTH:], mods3, 0, nct_row, g_ffn[0], w_router[0], b_router[0])
    xc = _moe(xc, routed, mods3, 0, c_len, math.gcd(MIXED_COMBINE_TILE, l_len), w_e_gate, w_e_up, w_e_down,
              ws_gate[0], ws_up[0], ws_down[0])

    w_in = cd_w_in[0].astype(BF16)
    qk = DIFF_HEADS * 2 * DIFF_DH
    z_hi = qk + SSD_INNER
    k_hi = z_hi + qk
    v_hi = k_hi + qk
    x_hi = v_hi + SSD_CONV_DIM
    w_dt = jnp.pad(w_in[:, x_hi:], ((0, 0), (0, LANES - 2 * SSD_HEADS)))
    q, z, k, v, xbc, dt = _project(xc, g_mix[1], mods3, 1, nct_row, rope_tabs, [
        (w_in[:, :qk], DIFF_DH ** -0.5 * LOG2E, BF16, False),
        (w_in[:, qk:z_hi], None, BF16, False),
        (w_in[:, z_hi:k_hi], 1.0, BF16, False),
        (w_in[:, k_hi:v_hi], None, BF16, False),
        (w_in[:, v_hi:x_hi], None, F32, False),
        (w_dt, None, F32, False),
    ])
    lam_init = 0.8 - 0.6 * math.exp(-0.3 * 1)
    diff = _diff_attention(q, k, v, cd_lam[0], cd_subln_g[0], lam_init, c_len)
    y2 = _ssd(xbc, dt, cd_conv_w[0], cd_conv_b[0], cd_dt_bias[0], cd_a_log[0], cd_d_skip[0], nct_time)
    w_out = cd_w_out[0].astype(BF16)
    xl, *routed = _out_odd(xc, diff, y2, z, cd_norm_g[0], w_out[:qk], w_out[qk:], mods3, 1, c_len,
                           g_ffn[1], w_router[1], b_router[1])
    return _moe(xl, routed, mods3, 1, 0, math.gcd(LATENT_COMBINE_TILE, s_len), w_e_gate, w_e_up, w_e_down,
                ws_gate[1], ws_up[1], ws_down[1], g_final=g_final)
```

```python
import functools
import math

import jax
import jax.numpy as jnp
from jax import lax
from jax.experimental import pallas as pl
from jax.experimental.pallas import tpu as pltpu

F32 = jnp.float32
BF16 = jnp.bfloat16
HIGHEST = lax.Precision.HIGHEST

GRID_W = 64
N_MOD = 6
NORM_EPS = 1e-6
ROPE_BASE = 10000.0
CONV_W = 4

LRU_WIDTH = 512
LRU_BLOCKS = 8
LRU_C = 8.0

HEAD_DIM = 64
WIN_HEADS = 8
WIN_KV_HEADS = 2
WINDOW = 128

DIFF_HEADS = 4
DIFF_DH = 64

SSD_HEADS = 8
SSD_HEAD_DIM = 64
SSD_INNER = SSD_HEADS * SSD_HEAD_DIM
SSD_GROUPS = 2
SSD_STATE = 128
SSD_CONV_DIM = SSD_INNER + 2 * SSD_GROUPS * SSD_STATE

N_EXPERTS = 64
N_EXPERT_GROUPS = 8
TOPK_GROUPS = 4
TOP_K = 8
D_EXPERT = 256
ROUTED_SCALE = 2.5

LANES = 128
SUBLANES = 8
MOD_ROWS = 16
TIME_TILE = 128
ROW_TILE = 256
DISPATCH_TILE = 2048
LATENT_COMBINE_TILE = 512
MIXED_COMBINE_TILE = 768
SSD_BATCH_PER_STEP = 4
EXPERT_BLK = 1024
ROW_CHUNKS = 4
VMEM_LIMIT = 48 * 1024 * 1024
NEG_BIG = -1e30
LOG2E = math.log2(math.e)


def _cparams(sem):
    return pltpu.CompilerParams(dimension_semantics=sem, vmem_limit_bytes=VMEM_LIMIT)


def _nt_dot(a, b):
    return lax.dot_general(a, b, (((1,), (1,)), ((), ())), preferred_element_type=F32)


def _softplus(x):
    return jnp.maximum(x, 0.0) + jnp.log1p(jnp.exp(-jnp.abs(x)))


def _sigmoid(x):
    return 0.5 * jnp.tanh(0.5 * x) + 0.5


def _silu(x):
    return x * _sigmoid(x)


def _pack_bf16(x):
    half = x.shape[-1] // 2
    bits = pltpu.bitcast(x.astype(BF16).astype(F32), jnp.uint32)
    return bits[:, :half] | (bits[:, half:] >> 16)


def _unpack_bf16(w):
    hi = pltpu.bitcast(w & jnp.uint32(0xFFFF0000), F32)
    lo = pltpu.bitcast(w << 16, F32)
    return hi, lo


def _store_chunk_rows(ref, w):
    n = w.shape[0]
    for j in range(ROW_CHUNKS):
        ref[pl.ds(j, n, stride=ROW_CHUNKS), :] = w[:, j * LANES:(j + 1) * LANES]


def _load_chunk_rows(ref, n):
    return jnp.concatenate([ref[pl.ds(j, n, stride=ROW_CHUNKS), :] for j in range(ROW_CHUNKS)], axis=1)


def _packed_dot(w, weight_ref):
    half = w.shape[-1]
    hi, lo = _unpack_bf16(w)
    return (jnp.dot(hi.astype(BF16), weight_ref[:half, :], preferred_element_type=F32)
            + jnp.dot(lo.astype(BF16), weight_ref[half:, :], preferred_element_type=F32))


def _mod_kernel(c_ref, w_ref, b_ref, o_ref):
    c = c_ref[...]
    s = _silu(c)
    o_ref[...] = jnp.dot(s, w_ref[...], preferred_element_type=F32, precision=HIGHEST) + b_ref[...]


def _modulations(c_all, w_mod, b_mod):
    depth, d, _ = w_mod.shape
    return pl.pallas_call(
        _mod_kernel,
        grid=(depth, N_MOD),
        in_specs=[
            pl.BlockSpec((MOD_ROWS, d), lambda l, k: (0, 0)),
            pl.BlockSpec((None, d, d), lambda l, k: (l, 0, k)),
            pl.BlockSpec((None, 1, d), lambda l, k: (l, 0, k)),
        ],
        out_specs=pl.BlockSpec((None, MOD_ROWS, d), lambda l, k: (l, 0, k)),
        out_shape=jax.ShapeDtypeStruct((depth, MOD_ROWS, N_MOD * d), F32),
        compiler_params=_cparams(("arbitrary", "arbitrary")),
        name="adaln_modulation",
    )(c_all, w_mod, b_mod.reshape(depth, 1, N_MOD * d))


def _mod_spec(d, layer, chunk, row_fn):
    return pl.BlockSpec((None, 1, d), lambda b, i: (layer * MOD_ROWS + row_fn(b, i), 0, chunk))


def _norm_mod(x, g, sh, sc):
    ms = jnp.mean(x * x, axis=-1, keepdims=True)
    return (x * lax.rsqrt(ms + NORM_EPS) * g) * (1.0 + sc) + sh


def _rope(y, cos, sa, sb):
    n = y.shape[-1]
    half = HEAD_DIM // 2
    return y * cos + pltpu.roll(y, n - half, 1) * sa + pltpu.roll(y, half, 1) * sb


def _proj_kernel(*refs, ropes):
    n = len(ropes)
    x_ref, g_ref, sh_ref, sc_ref, cos_ref, sa_ref, sb_ref = refs[:7]
    w_refs = refs[7:7 + n]
    o_refs = refs[7 + n:]
    h = _norm_mod(x_ref[...], g_ref[...], sh_ref[...], sc_ref[...]).astype(BF16)
    for w_ref, o_ref, rope in zip(w_refs, o_refs, ropes):
        y = jnp.dot(h, w_ref[...], preferred_element_type=F32)
        if rope is not None:
            w = y.shape[-1]
            y = _rope(y, cos_ref[:, :w], sa_ref[:, :w], sb_ref[:, :w])
            if rope != 1.0:
                y = y * rope
        o_ref[...] = y.astype(o_ref.dtype)


def _project(xc, g, mods3, layer, nct, rope_tabs, groups):
    b, l, d = xc.shape
    tm = ROW_TILE
    mod_row = lambda i, bb: layer * MOD_ROWS + jnp.where(i < nct, SUBLANES, bb)
    rw = rope_tabs[0].shape[-1]
    in_specs = [
        pl.BlockSpec((None, tm, d), lambda i, bb: (bb, i, 0)),
        pl.BlockSpec((1, d), lambda i, bb: (0, 0)),
        pl.BlockSpec((None, 1, d), lambda i, bb: (mod_row(i, bb), 0, 0)),
        pl.BlockSpec((None, 1, d), lambda i, bb: (mod_row(i, bb), 0, 1)),
    ] + [pl.BlockSpec((tm, rw), lambda i, bb: (i, 0))] * 3
    out_specs, out_shapes = [], []
    for w, _, dt, time_major in groups:
        n = w.shape[1]
        in_specs.append(pl.BlockSpec((d, n), lambda i, bb: (0, 0)))
        if time_major:
            out_specs.append(pl.BlockSpec((tm, n), lambda i, bb: (i, bb)))
            out_shapes.append(jax.ShapeDtypeStruct((l, b * n), dt))
        else:
            out_specs.append(pl.BlockSpec((None, tm, n), lambda i, bb: (bb, i, 0)))
            out_shapes.append(jax.ShapeDtypeStruct((b, l, n), dt))
    return pl.pallas_call(
        functools.partial(_proj_kernel, ropes=tuple(gp[1] for gp in groups)),
        grid=(l // tm, b),
        in_specs=in_specs,
        out_specs=out_specs,
        out_shape=out_shapes,
        compiler_params=_cparams(("arbitrary", "arbitrary")),
        name="norm_mod_project",
    )(xc, g.reshape(1, d), mods3, mods3, *rope_tabs, *[gp[0] for gp in groups])


def _rope_tables(c_len, s_len):
    rows = s_len // GRID_W
    row = jnp.repeat(jnp.arange(rows), GRID_W).astype(F32)
    col = jnp.tile(jnp.arange(GRID_W), rows).astype(F32)
    n = HEAD_DIM // 4
    inv = ROPE_BASE ** (-jnp.arange(n, dtype=F32) / n)
    ang = jnp.concatenate([row[:, None] * inv, col[:, None] * inv], axis=-1)
    cos, sin = jnp.cos(ang), jnp.sin(ang)
    zero = jnp.zeros_like(sin)
    reps = WIN_HEADS
    cos_t = jnp.tile(jnp.concatenate([cos, cos], axis=-1), (1, reps))
    sa_t = jnp.tile(jnp.concatenate([-sin, zero], axis=-1), (1, reps))
    sb_t = jnp.tile(jnp.concatenate([zero, sin], axis=-1), (1, reps))
    w = cos_t.shape[-1]
    pad1 = jnp.ones((c_len, w), F32)
    pad0 = jnp.zeros((c_len, w), F32)
    return (jnp.concatenate([pad1, cos_t], 0), jnp.concatenate([pad0, sa_t], 0),
            jnp.concatenate([pad0, sb_t], 0))


def _seq_tile(d, g, nct, nt):
    rev = jnp.where(g < nct, nct - 1 - g, nt - 1 - (g - nct))
    return jnp.where(d == 0, g, rev)


def _rglru_kernel(x_ref, xp_ref, xn_ref, cw_ref, cb_ref, w_ref, bias_ref, lam_ref, o_ref,
                  ext_scr, a_scr, b_scr, h_scr, *, ts, nct, nt, sub):
    d = pl.program_id(0)
    g = pl.program_id(1)
    tile = _seq_tile(d, g, nct, nt)
    bsz, width = h_scr.shape
    pv = jnp.where((tile == 0) | (tile == nct), 0.0, 1.0)
    nv = jnp.where((tile == nct - 1) | (tile == nt - 1), 0.0, 1.0)
    ext_scr[0:1] = xp_ref[...] * pv
    ext_scr[1:ts + 1] = x_ref[...]
    ext_scr[ts + 1:ts + 3] = xn_ref[...] * nv

    @pl.when(g == 0)
    def _():
        h_scr[...] = jnp.zeros_like(h_scr)

    neg_sp = -LRU_C * _softplus(-lam_ref[...])

    def prep(c, carry):
        r0 = pl.multiple_of(c * sub, sub)
        e = ext_scr[pl.ds(r0, sub + CONV_W - 1)]
        u = cb_ref[...] + cw_ref[0] * e[0:sub]
        for j in range(1, CONV_W):
            u = u + cw_ref[j] * e[j:j + sub]
        u2 = u.reshape(sub * bsz, width)
        gts = jnp.dot(u2.astype(BF16), w_ref[...], preferred_element_type=F32) + bias_ref[...]
        r = _sigmoid(gts[:, :width])
        ig = _sigmoid(gts[:, width:])
        log_a = neg_sp * r
        a = jnp.exp(log_a)
        mult = jnp.sqrt(1.0 - a * a)
        a_scr[pl.ds(r0, sub)] = a.reshape(sub, bsz, width)
        b_scr[pl.ds(r0, sub)] = (mult * ig * u2).reshape(sub, bsz, width)
        return carry

    lax.fori_loop(0, ts // sub, prep, 0)

    def step(t, h):
        tt = jnp.where(d == 0, t, ts - 1 - t)
        h = a_scr[tt] * h + b_scr[tt]
        o_ref[tt] = h
        return h

    h_scr[...] = lax.fori_loop(0, ts, step, h_scr[...], unroll=8)


def _rglru(xa_tm, conv_w, conv_b, w_gates, b_gates, lam, nct):
    l, bsz, width = xa_tm.shape
    ts = TIME_TILE
    nt = l // ts
    tile = lambda d, g: _seq_tile(d, g, nct, nt)
    kern = functools.partial(_rglru_kernel, ts=ts, nct=nct, nt=nt, sub=16)
    return pl.pallas_call(
        kern,
        grid=(2, nt),
        in_specs=[
            pl.BlockSpec((ts, bsz, width), lambda d, g: (tile(d, g), 0, 0)),
            pl.BlockSpec((1, bsz, width), lambda d, g: (jnp.maximum(tile(d, g) * ts - 1, 0), 0, 0)),
            pl.BlockSpec((2, bsz, width),
                         lambda d, g: (jnp.minimum((tile(d, g) + 1) * (ts // 2), l // 2 - 1), 0, 0)),
            pl.BlockSpec((CONV_W, 1, width), lambda d, g: (0, 0, 0)),
            pl.BlockSpec((1, width), lambda d, g: (0, 0)),
            pl.BlockSpec((None, width, 2 * width), lambda d, g: (d, 0, 0)),
            pl.BlockSpec((None, 1, 2 * width), lambda d, g: (d, 0, 0)),
            pl.BlockSpec((None, 1, width), lambda d, g: (d, 0, 0)),
        ],
        out_specs=pl.BlockSpec((None, ts, bsz, width), lambda d, g: (d, tile(d, g), 0, 0)),
        out_shape=jax.ShapeDtypeStruct((2, l, bsz, width), F32),
        scratch_shapes=[
            pltpu.VMEM((ts + CONV_W - 1, bsz, width), F32),
            pltpu.VMEM((ts, bsz, width), F32),
            pltpu.VMEM((ts, bsz, width), F32),
            pltpu.VMEM((bsz, width), F32),
        ],
        compiler_params=_cparams(("arbitrary", "arbitrary")),
        name="rglru_scan",
    )(xa_tm, xa_tm, xa_tm, conv_w.reshape(CONV_W, 1, width), conv_b.reshape(1, width),
      w_gates, b_gates, lam)


def _block_diag(w):
    nb, c, dd = w.shape
    eye = jnp.eye(nb, dtype=w.dtype)
    return (eye[:, None, :, None] * w[:, :, None, :]).reshape(nb * c, nb * dd)


def _win_attn_kernel(sink_ref, q_ref, k_ref, v_ref, o_ref, *, c_len, l_len, nqc):
    j = pl.program_id(1)
    blk = q_ref.shape[0]
    grp = WIN_HEADS // WIN_KV_HEADS
    band = blk + 2 * WINDOW
    heads = [(h, slice(h * HEAD_DIM, (h + 1) * HEAD_DIM),
              slice((h // grp) * HEAD_DIM, (h // grp + 1) * HEAD_DIM)) for h in range(WIN_HEADS)]

    @pl.when(j < nqc)
    def _():
        logits = [_nt_dot(q_ref[:, hsl], k_ref[0:c_len, ksl]) for _, hsl, ksl in heads]
        probs, dens = [], []
        for (h, _, _), s in zip(heads, logits):
            sink = sink_ref[h] * LOG2E
            m = jnp.maximum(jnp.max(s, axis=-1, keepdims=True), sink)
            p = jnp.exp2(s - m)
            dens.append(jnp.sum(p, axis=-1, keepdims=True) + jnp.exp2(sink - m))
            probs.append(p.astype(BF16))
        outs = [jnp.dot(p, v_ref[0:c_len, ksl], preferred_element_type=F32) / den
                for (_, _, ksl), p, den in zip(heads, probs, dens)]
        o_ref[...] = jnp.concatenate(outs, axis=-1).astype(o_ref.dtype)

    @pl.when(j >= nqc)
    def _():
        jb = j - nqc
        start = jnp.clip(c_len + jb * blk - WINDOW, c_len - WINDOW, l_len - band)
        start = pl.multiple_of(start, WINDOW)
        qpos = jb * blk + lax.broadcasted_iota(jnp.int32, (blk, band), 0)
        kpos = start - c_len + lax.broadcasted_iota(jnp.int32, (blk, band), 1)
        valid = (jnp.abs(qpos - kpos) <= WINDOW) & (kpos >= 0)
        lc = [_nt_dot(q_ref[:, hsl], k_ref[0:c_len, ksl]) for _, hsl, ksl in heads]
        lb = [jnp.where(valid, _nt_dot(q_ref[:, hsl], k_ref[pl.ds(start, band), ksl]), NEG_BIG)
              for _, hsl, ksl in heads]
        pcs, pbs, dens = [], [], []
        for (h, _, _), sc, sb in zip(heads, lc, lb):
            sink = sink_ref[h] * LOG2E
            m = jnp.maximum(jnp.maximum(jnp.max(sc, axis=-1, keepdims=True),
                                        jnp.max(sb, axis=-1, keepdims=True)), sink)
            pc = jnp.exp2(sc - m)
            pb = jnp.exp2(sb - m)
            dens.append(jnp.sum(pc, axis=-1, keepdims=True) + jnp.sum(pb, axis=-1, keepdims=True)
                        + jnp.exp2(sink - m))
            pcs.append(pc.astype(BF16))
            pbs.append(pb.astype(BF16))
        outs = [(jnp.dot(pc, v_ref[0:c_len, ksl], preferred_element_type=F32)
                 + jnp.dot(pb, v_ref[pl.ds(start, band), ksl], preferred_element_type=F32)) / den
                for (_, _, ksl), pc, pb, den in zip(heads, pcs, pbs, dens)]
        o_ref[...] = jnp.concatenate(outs, axis=-1).astype(o_ref.dtype)


def _win_attention(q, k, v, sink, c_len):
    b, l, qw = q.shape
    kw = k.shape[-1]
    blk = ROW_TILE
    kern = functools.partial(_win_attn_kernel, c_len=c_len, l_len=l, nqc=c_len // blk)
    return pl.pallas_call(
        kern,
        grid=(b, l // blk),
        in_specs=[
            pl.BlockSpec(memory_space=pltpu.SMEM),
            pl.BlockSpec((None, blk, qw), lambda bb, j: (bb, j, 0)),
            pl.BlockSpec((None, l, kw), lambda bb, j: (bb, 0, 0)),
            pl.BlockSpec((None, l, kw), lambda bb, j: (bb, 0, 0)),
        ],
        out_specs=pl.BlockSpec((None, blk, qw), lambda bb, j: (bb, j, 0)),
        out_shape=jax.ShapeDtypeStruct((b, l, qw), BF16),
        compiler_params=_cparams(("arbitrary", "arbitrary")),
        name="window_attention",
    )(sink, q, k, v)


def _out_even_kernel(x_ref, rec_ref, gate_ref, att_ref, wa_ref, wb_ref, g1_ref, *refs):
    route_in, o_ref, route_out = refs[:5], refs[5], refs[6:]
    lru = (rec_ref[0] + rec_ref[1]) * jax.nn.gelu(gate_ref[...].astype(F32))
    y = (jnp.dot(lru.astype(BF16), wa_ref[...], preferred_element_type=F32)
         + jnp.dot(att_ref[...], wb_ref[...], preferred_element_type=F32))
    x = x_ref[...] + g1_ref[...] * y
    o_ref[...] = x
    _route(x, *route_in, *route_out)


def _out_even(xc, rec2, gate, att, w_a, w_b, mods3, layer, nct, g_ffn, w_router, b_router):
    b, l, d = xc.shape
    tm = ROW_TILE
    w = gate.shape[-1]
    row = lambda bb, i: jnp.where(i < nct, SUBLANES, bb)
    r_in, r_args, r_out, r_shapes, r_scratch = _route_plumbing(b, l, d, g_ffn, mods3, layer, row, w_router, b_router)
    return pl.pallas_call(
        _out_even_kernel,
        grid=(b, l // tm),
        in_specs=[
            pl.BlockSpec((None, tm, d), lambda bb, i: (bb, i, 0)),
            pl.BlockSpec((2, tm, w), lambda bb, i: (0, i, bb)),
            pl.BlockSpec((None, tm, w), lambda bb, i: (bb, i, 0)),
            pl.BlockSpec((None, tm, att.shape[-1]), lambda bb, i: (bb, i, 0)),
            pl.BlockSpec(w_a.shape, lambda bb, i: (0, 0)),
            pl.BlockSpec(w_b.shape, lambda bb, i: (0, 0)),
            _mod_spec(d, layer, 2, row),
        ] + r_in,
        out_specs=[pl.BlockSpec((None, tm, d), lambda bb, i: (bb, i, 0))] + r_out,
        out_shape=[jax.ShapeDtypeStruct((b, l, d), F32)] + r_shapes,
        scratch_shapes=r_scratch,
        compiler_params=_cparams(("arbitrary", "arbitrary")),
        name="out_proj_even",
    )(xc, rec2, gate, att, w_a, w_b, mods3, *r_args)


def _diff_attn_kernel(lam_ref, g_ref, q_ref, k_ref, v_ref, o_ref, *, lam_init):
    lv = lam_ref[...]
    lam = (jnp.exp(jnp.sum(lv[0:1] * lv[1:2], axis=-1, keepdims=True))
           - jnp.exp(jnp.sum(lv[2:3] * lv[3:4], axis=-1, keepdims=True)) + lam_init)
    vw = 2 * DIFF_DH

    def logits(h, mp):
        lo = h * vw + mp * DIFF_DH
        return _nt_dot(q_ref[:, lo:lo + DIFF_DH], k_ref[:, lo:lo + DIFF_DH])

    def softmax_parts(s):
        e = jnp.exp2(s - jnp.max(s, axis=-1, keepdims=True))
        return e, 1.0 / jnp.sum(e, axis=-1, keepdims=True)

    heads = range(DIFF_HEADS)
    ls = [(logits(h, 0), logits(h, 1)) for h in heads]
    ws = []
    for l0, l1 in ls:
        e0, r0 = softmax_parts(l0)
        e1, r1 = softmax_parts(l1)
        ws.append(((e0 - e1 * (lam * r1 / r0)).astype(BF16), r0))
    for h, (w, r0) in zip(heads, ws):
        lo = h * vw
        o = jnp.dot(w, v_ref[:, lo:lo + vw], preferred_element_type=F32) * r0
        ms = jnp.mean(o * o, axis=-1, keepdims=True)
        o = o * lax.rsqrt(ms + NORM_EPS) * g_ref[...]
        o_ref[:, lo:lo + vw] = (o * (1.0 - lam_init)).astype(o_ref.dtype)


def _diff_attention(q, k, v, lam_vecs, subln_g, lam_init, c_len):
    b, l, w = q.shape
    tq = ROW_TILE
    s_len = l - c_len
    off = c_len // tq
    return pl.pallas_call(
        functools.partial(_diff_attn_kernel, lam_init=lam_init),
        grid=(b, s_len // tq),
        in_specs=[
            pl.BlockSpec(lam_vecs.shape, lambda bb, j: (0, 0)),
            pl.BlockSpec((1, 2 * DIFF_DH), lambda bb, j: (0, 0)),
            pl.BlockSpec((None, tq, w), lambda bb, j: (bb, j + off, 0)),
            pl.BlockSpec((None, l, w), lambda bb, j: (bb, 0, 0)),
            pl.BlockSpec((None, l, w), lambda bb, j: (bb, 0, 0)),
        ],
        out_specs=pl.BlockSpec((None, tq, w), lambda bb, j: (bb, j, 0)),
        out_shape=jax.ShapeDtypeStruct((b, s_len, w), BF16),
        compiler_params=_cparams(("arbitrary", "arbitrary")),
        name="diff_attention",
    )(lam_vecs, subln_g.reshape(1, -1), q, k, v)


def _ssd_kernel(x_ref, xp_ref, xn_ref, dt_ref, cw_ref, cb_ref, dtb_ref, alog_ref, dsk_ref, o_ref,
                ext_scr, st_scr, *, q, nct, nt):
    d = pl.program_id(0)
    g = pl.program_id(2)
    tile = _seq_tile(d, g, nct, nt)

    @pl.when(g == 0)
    def _():
        st_scr[...] = jnp.zeros_like(st_scr)

    for bi in range(x_ref.shape[0]):
        _ssd_chunk(d, tile, x_ref.at[bi], xp_ref.at[bi], xn_ref.at[bi], dt_ref.at[bi], cw_ref, cb_ref, dtb_ref,
                   alog_ref, dsk_ref, o_ref.at[bi], ext_scr.at[bi],
                   st_scr.at[pl.ds(bi * SSD_HEADS, SSD_HEADS)], q=q, nct=nct, nt=nt)


def _ssd_chunk(d, tile, x_ref, xp_ref, xn_ref, dt_ref, cw_ref, cb_ref, dtb_ref, alog_ref, dsk_ref, o_ref,
               ext_scr, st_scr, *, q, nct, nt):
    pv = jnp.where((tile == 0) | (tile == nct), 0.0, 1.0)
    nv = jnp.where((tile == nct - 1) | (tile == nt - 1), 0.0, 1.0)
    ext_scr[0:SUBLANES] = xp_ref[...] * pv
    ext_scr[SUBLANES:SUBLANES + q] = x_ref[...]
    ext_scr[SUBLANES + q:2 * SUBLANES + q] = xn_ref[...] * nv

    u = cb_ref[...] + cw_ref[0] * ext_scr[SUBLANES - 1:SUBLANES - 1 + q, :]
    for j in range(1, CONV_W):
        u = u + cw_ref[j] * ext_scr[SUBLANES - 1 + j:SUBLANES - 1 + j + q, :]
    act = _silu(u)

    dtr = dt_ref[...]
    dtr = jnp.where(d == 0, dtr, pltpu.roll(dtr, LANES - SSD_HEADS, 1))
    dtv = _softplus(dtr + dtb_ref[...])
    head_lane = lax.broadcasted_iota(jnp.int32, (1, LANES), 1) < SSD_HEADS
    dta = dtv * jnp.where(head_lane, -jnp.exp(alog_ref[...]) * LOG2E, 0.0)
    ri = lax.broadcasted_iota(jnp.int32, (q, q), 0)
    ci = lax.broadcasted_iota(jnp.int32, (q, q), 1)
    keep = jnp.where(d == 0, ri - ci, ci - ri) >= 0
    keep_b = jnp.where(keep, 1.0, 0.0).astype(BF16)
    cum = jnp.zeros((q, LANES), F32)
    rest = dta
    for _ in range(3):
        part = rest.astype(BF16)
        cum = cum + jnp.dot(keep_b, part, preferred_element_type=F32)
        rest = rest - part.astype(F32)
    tot = jnp.sum(dta, axis=0, keepdims=True)
    cum_t = cum.T
    dt_t = dtv.T
    to_end = jnp.exp2(tot - cum) * dtv
    e_cum = jnp.exp2(cum)
    e_tot = jnp.exp2(tot)
    dskip = dsk_ref[...] * jnp.where(d == 0, 1.0, 0.0)

    def spread(v, width):
        rows = lax.broadcasted_iota(jnp.int32, (LANES, SSD_HEADS * width), 0)
        cols = lax.broadcasted_iota(jnp.int32, (LANES, SSD_HEADS * width), 1)
        pick = jnp.where(rows * width <= cols, jnp.where(cols < (rows + 1) * width, 1.0, 0.0), 0.0).astype(BF16)
        out = jnp.zeros((q, SSD_HEADS * width), F32)
        rest = v
        for _ in range(3):
            part = rest.astype(BF16)
            out = out + jnp.dot(part, pick, preferred_element_type=F32)
            rest = rest - part.astype(F32)
        return out

    e_cum_b = spread(e_cum, SSD_HEAD_DIM)
    to_end_b = spread(to_end, SSD_HEAD_DIM)

    hpg = SSD_HEADS // SSD_GROUPS
    for gi in range(SSD_GROUPS):
        b_g = act[:, SSD_INNER + gi * SSD_STATE:SSD_INNER + (gi + 1) * SSD_STATE]
        c_lo = SSD_INNER + SSD_GROUPS * SSD_STATE + gi * SSD_STATE
        c_g = act[:, c_lo:c_lo + SSD_STATE].astype(BF16)
        cb = _nt_dot(c_g, b_g.astype(BF16))
        b_gt = b_g.T.astype(BF16)
        for hh in range(hpg):
            h = gi * hpg + hh
            xs = act[:, h * SSD_HEAD_DIM:(h + 1) * SSD_HEAD_DIM]
            seg = cum[:, h:h + 1] - cum_t[h:h + 1, :]
            decay = jnp.exp2(jnp.where(keep, seg, NEG_BIG))
            w = (cb * decay * dt_t[h:h + 1, :]).astype(BF16)
            state = st_scr[h]
            y = jnp.dot(w, xs.astype(BF16), preferred_element_type=F32)
            y = y + (jnp.dot(c_g, state.astype(BF16), preferred_element_type=F32)
                     * e_cum_b[:, h * SSD_HEAD_DIM:(h + 1) * SSD_HEAD_DIM])
            y = y + dskip[:, h * SSD_HEAD_DIM:(h + 1) * SSD_HEAD_DIM] * xs
            o_ref[:, h * SSD_HEAD_DIM:(h + 1) * SSD_HEAD_DIM] = y
            s_new = jnp.dot(b_gt, (xs * to_end_b[:, h * SSD_HEAD_DIM:(h + 1) * SSD_HEAD_DIM]).astype(BF16),
                            preferred_element_type=F32)
            st_scr[h] = e_tot[:, h:h + 1] * state + s_new


def _ssd(xbc, dt, conv_w, conv_b, dt_bias, a_log, d_skip, nct):
    b, l, cd = xbc.shape
    q = TIME_TILE
    nt = l // q
    nb = SSD_BATCH_PER_STEP
    tile = lambda d, bb, g: _seq_tile(d, g, nct, nt)
    r8 = q // SUBLANES
    pad = LANES - SSD_HEADS
    dtb = jnp.pad(dt_bias, ((0, 0), (0, pad))).reshape(2, 1, LANES)
    alog = jnp.pad(a_log, ((0, 0), (0, pad))).reshape(2, 1, LANES)
    dsk = jnp.repeat(d_skip, SSD_HEAD_DIM).reshape(1, SSD_INNER)
    return pl.pallas_call(
        functools.partial(_ssd_kernel, q=q, nct=nct, nt=nt),
        grid=(2, b // nb, nt),
        in_specs=[
            pl.BlockSpec((nb, q, cd), lambda d, bb, g: (bb, tile(d, bb, g), 0)),
            pl.BlockSpec((nb, SUBLANES, cd),
                         lambda d, bb, g: (bb, jnp.maximum(tile(d, bb, g) * r8 - 1, 0), 0)),
            pl.BlockSpec((nb, SUBLANES, cd),
                         lambda d, bb, g: (bb, jnp.minimum((tile(d, bb, g) + 1) * r8, l // SUBLANES - 1), 0)),
            pl.BlockSpec((nb, q, LANES), lambda d, bb, g: (bb, tile(d, bb, g), 0)),
            pl.BlockSpec((CONV_W, 1, cd), lambda d, bb, g: (0, 0, 0)),
            pl.BlockSpec((1, cd), lambda d, bb, g: (0, 0)),
            pl.BlockSpec((None, 1, LANES), lambda d, bb, g: (d, 0, 0)),
            pl.BlockSpec((None, 1, LANES), lambda d, bb, g: (d, 0, 0)),
            pl.BlockSpec((1, SSD_INNER), lambda d, bb, g: (0, 0)),
        ],
        out_specs=pl.BlockSpec((None, nb, q, SSD_INNER), lambda d, bb, g: (d, bb, tile(d, bb, g), 0)),
        out_shape=jax.ShapeDtypeStruct((2, b, l, SSD_INNER), F32),
        scratch_shapes=[
            pltpu.VMEM((nb, q + 2 * SUBLANES, cd), F32),
            pltpu.VMEM((nb * SSD_HEADS, SSD_STATE, SSD_HEAD_DIM), F32),
        ],
        compiler_params=_cparams(("arbitrary", "arbitrary", "arbitrary")),
        name="ssd_chunked",
    )(xbc, xbc, xbc, dt, conv_w.reshape(CONV_W, 1, cd), conv_b.reshape(1, cd), dtb, alog, dsk)


def _out_odd_kernel(x_ref, diff_ref, y_ref, z_ref, ng_ref, wa_ref, wb_ref, g1_ref, *refs):
    route_in, o_ref, route_out = refs[:5], refs[5], refs[6:]
    yz = (y_ref[0] + y_ref[1]) * _silu(z_ref[...].astype(F32))
    gs = SSD_INNER // SSD_GROUPS
    parts = []
    for gi in range(SSD_GROUPS):
        seg = yz[:, gi * gs:(gi + 1) * gs]
        ms = jnp.mean(seg * seg, axis=-1, keepdims=True)
        parts.append(seg * lax.rsqrt(ms + NORM_EPS) * ng_ref[:, gi * gs:(gi + 1) * gs])
    ssd = jnp.concatenate(parts, axis=-1).astype(BF16)
    y = (jnp.dot(diff_ref[...], wa_ref[...], preferred_element_type=F32)
         + jnp.dot(ssd, wb_ref[...], preferred_element_type=F32))
    x = x_ref[...] + g1_ref[...] * y
    o_ref[...] = x
    _route(x, *route_in, *route_out)


def _out_odd(xc, diff, y2, z, norm_g, w_a, w_b, mods3, layer, c_len, g_ffn, w_router, b_router):
    b, l, d = xc.shape
    s_len = l - c_len
    tm = ROW_TILE
    off = c_len // tm
    w = SSD_INNER
    row = lambda bb, i: bb
    r_in, r_args, r_out, r_shapes, r_scratch = _route_plumbing(b, s_len, d, g_ffn, mods3, layer, row, w_router,
                                                               b_router)
    return pl.pallas_call(
        _out_odd_kernel,
        grid=(b, s_len // tm),
        in_specs=[
            pl.BlockSpec((None, tm, d), lambda bb, i: (bb, i + off, 0)),
            pl.BlockSpec((None, tm, diff.shape[-1]), lambda bb, i: (bb, i, 0)),
            pl.BlockSpec((2, None, tm, w), lambda bb, i: (0, bb, i + off, 0)),
            pl.BlockSpec((None, tm, w), lambda bb, i: (bb, i + off, 0)),
            pl.BlockSpec((1, w), lambda bb, i: (0, 0)),
            pl.BlockSpec(w_a.shape, lambda bb, i: (0, 0)),
            pl.BlockSpec(w_b.shape, lambda bb, i: (0, 0)),
            _mod_spec(d, layer, 2, row),
        ] + r_in,
        out_specs=[pl.BlockSpec((None, tm, d), lambda bb, i: (bb, i, 0))] + r_out,
        out_shape=[jax.ShapeDtypeStruct((b, s_len, d), F32)] + r_shapes,
        scratch_shapes=r_scratch,
        compiler_params=_cparams(("arbitrary", "arbitrary")),
        name="out_proj_odd",
    )(xc, diff, y2, z, norm_g.reshape(1, w), w_a, w_b, mods3, *r_args)


def _route(x, g_ref, sh_ref, sc_ref, wr_ref, br_ref, h_ref, eid_ref, rnk_ref, gate_ref, cnt_ref, carry_scr):
    @pl.when((pl.program_id(0) == 0) & (pl.program_id(1) == 0))
    def _():
        carry_scr[...] = jnp.zeros_like(carry_scr)

    h = _norm_mod(x, g_ref[...], sh_ref[...], sc_ref[...])
    _store_chunk_rows(h_ref, _pack_bf16(h))
    tm = h.shape[0]
    per = N_EXPERTS // N_EXPERT_GROUPS
    h_hi = h.astype(BF16)
    h_lo = (h - h_hi.astype(F32)).astype(BF16)
    logits = _nt_dot(wr_ref[0], h_hi) + _nt_dot(wr_ref[0], h_lo) + _nt_dot(wr_ref[1], h_hi)
    scores = jax.nn.sigmoid(logits)
    sel = scores + br_ref[...]
    sel3 = sel.reshape(N_EXPERT_GROUPS, per, tm)
    kio = lax.broadcasted_iota(jnp.int32, sel3.shape, 1)
    m1 = jnp.max(sel3, axis=1, keepdims=True)
    first = jnp.min(jnp.where(sel3 == m1, kio, per), axis=1, keepdims=True)
    m2 = jnp.max(jnp.where(kio == first, NEG_BIG, sel3), axis=1, keepdims=True)
    gs = m1 + m2
    gio = lax.broadcasted_iota(jnp.int32, gs.shape, 0)
    ahead = jnp.zeros(gs.shape, jnp.int32)
    for gp in range(N_EXPERT_GROUPS):
        other = gs[gp:gp + 1]
        ahead = ahead + jnp.where((other > gs) | ((other == gs) & (gp < gio)), 1, 0)
    grp_on = jnp.where(ahead < TOPK_GROUPS, 1.0, 0.0)
    selm = jnp.where(jnp.broadcast_to(grp_on, sel3.shape) > 0.5, sel3, NEG_BIG).reshape(N_EXPERTS, tm)
    eio = lax.broadcasted_iota(jnp.int32, selm.shape, 0)
    work = selm
    cf = jnp.zeros(selm.shape, F32)
    e_rows, s_rows = [], []
    for k in range(TOP_K):
        best = jnp.max(work, axis=0, keepdims=True)
        idx = jnp.min(jnp.where(work == best, eio, N_EXPERTS), axis=0, keepdims=True)
        hit = eio == idx
        cf = cf + jnp.where(hit, 1.0, 0.0)
        work = jnp.where(hit, NEG_BIG, work)
        e_rows.append(idx)
        s_rows.append(jnp.sum(jnp.where(hit, scores, 0.0), axis=0, keepdims=True))
    denom = s_rows[0]
    for s_k in s_rows[1:]:
        denom = denom + s_k
    g_rows = [s_k / denom * ROUTED_SCALE for s_k in s_rows]
    ti = lax.broadcasted_iota(jnp.int32, (tm, tm), 0)
    tj = lax.broadcasted_iota(jnp.int32, (tm, tm), 1)
    before = jnp.where(ti < tj, 1.0, 0.0).astype(BF16)
    in_expert = carry_scr[:, 0:1] + jnp.dot(cf.astype(BF16), before, preferred_element_type=F32)
    carry_scr[...] = carry_scr[...] + jnp.sum(cf, axis=1, keepdims=True)
    cnt_ref[...] = carry_scr[...]
    r_rows = [jnp.sum(jnp.where(eio == idx, in_expert, 0.0), axis=0, keepdims=True) for idx in e_rows]
    eid_ref[...] = jnp.concatenate(e_rows, axis=0)
    rnk_ref[...] = jnp.concatenate(r_rows, axis=0).astype(jnp.int32)
    padded = jnp.concatenate(g_rows + [jnp.zeros((LANES - TOP_K, tm), F32)], axis=0)
    gate_ref[...] = padded.T


def _route_plumbing(b, r, d, g, mods3, layer, row_fn, w_router, b_router):
    tm = ROW_TILE
    nt = r // tm
    w_t = w_router.T
    w_hi = w_t.astype(BF16)
    w_router_t = jnp.stack([w_hi, (w_t - w_hi.astype(F32)).astype(BF16)])
    slot = pl.BlockSpec((TOP_K, tm), lambda bb, i: (0, bb * nt + i))
    slot_shape = jax.ShapeDtypeStruct((TOP_K, b * r), jnp.int32)
    in_specs = [
        pl.BlockSpec((1, d), lambda bb, i: (0, 0)),
        _mod_spec(d, layer, 3, row_fn),
        _mod_spec(d, layer, 4, row_fn),
        pl.BlockSpec(w_router_t.shape, lambda bb, i: (0, 0, 0)),
        pl.BlockSpec((N_EXPERTS, 1), lambda bb, i: (0, 0)),
    ]
    args = [g.reshape(1, d), mods3, mods3, w_router_t, b_router.reshape(N_EXPERTS, 1)]
    out_specs = [
        pl.BlockSpec((tm * ROW_CHUNKS, LANES), lambda bb, i: (bb * nt + i, 0)),
        slot,
        slot,
        pl.BlockSpec((None, tm, LANES), lambda bb, i: (bb, i, 0)),
        pl.BlockSpec((N_EXPERTS, LANES), lambda bb, i: (0, 0)),
    ]
    out_shapes = [jax.ShapeDtypeStruct((b * r * ROW_CHUNKS, LANES), jnp.uint32), slot_shape, slot_shape,
                  jax.ShapeDtypeStruct((b, r, LANES), F32), jax.ShapeDtypeStruct((N_EXPERTS, LANES), F32)]
    scratch = [pltpu.VMEM((N_EXPERTS, LANES), F32)]
    return in_specs, args, out_specs, out_shapes, scratch


def _moe_plan(counts, n_rows):
    blk = EXPERT_BLK
    nb = n_rows // blk
    ends = jnp.cumsum(counts)
    starts = ends - counts
    count_le = lambda sorted_vals, q: jnp.sum(sorted_vals[None, :] <= q[:, None], axis=1, dtype=jnp.int32)
    first = jnp.arange(nb, dtype=jnp.int32) * blk
    e_lo = count_le(ends, first)
    e_hi = count_le(ends, first + (blk - 1))
    n_pair = e_hi - e_lo + 1
    p_end = jnp.cumsum(n_pair)
    p_start = p_end - n_pair
    i = jnp.arange(nb + N_EXPERTS - 1, dtype=jnp.int32)
    j = jnp.minimum(count_le(p_end, i), nb - 1)
    valid = i < p_end[-1]
    e = jnp.where(valid, e_lo[j] + i - p_start[j], e_hi[nb - 1]).astype(jnp.int32)
    bounds = jnp.concatenate([starts, ends[-1:]]).astype(jnp.int32)
    return j, e, valid.astype(jnp.int32), bounds


def _positions_kernel(starts_ref, eid_ref, rnk_ref, pos_ref):
    eid = eid_ref[...]
    pos = rnk_ref[...]
    for e in range(N_EXPERTS):
        pos = pos + jnp.where(eid == e, starts_ref[e], 0)
    pos_ref[...] = pos * ROW_CHUNKS


def _positions(eid, rnk, starts):
    full = pl.BlockSpec(eid.shape, lambda: (0, 0))
    return pl.pallas_call(
        _positions_kernel,
        in_specs=[pl.BlockSpec(memory_space=pltpu.SMEM), full, full],
        out_specs=full,
        out_shape=jax.ShapeDtypeStruct(eid.shape, jnp.int32),
        compiler_params=pltpu.CompilerParams(vmem_limit_bytes=VMEM_LIMIT),
        name="moe_positions",
    )(starts, eid, rnk)


def _token_row(ref, first):
    return ref.at[pl.ds(pl.multiple_of(first, ROW_CHUNKS), ROW_CHUNKS)]


def _dispatch_kernel(pos_ref, h_ref, xs_ref, sem):
    tm = h_ref.shape[0] // ROW_CHUNKS

    def issue(t, carry):
        src = _token_row(h_ref, t * ROW_CHUNKS)
        for k in range(TOP_K):
            pltpu.make_async_copy(src, _token_row(xs_ref, pos_ref[k, t]), sem).start(priority=k % 2)
        return carry

    lax.fori_loop(0, tm, issue, 0)
    done = pl.ds(0, tm * ROW_CHUNKS)
    for _ in range(TOP_K):
        pltpu.make_async_copy(h_ref.at[done], xs_ref.at[done], sem).wait()


def _dispatch(h2, pos):
    rows, w = h2.shape
    tm = math.gcd(DISPATCH_TILE, rows // ROW_CHUNKS)
    return pl.pallas_call(
        _dispatch_kernel,
        grid=(rows // (tm * ROW_CHUNKS),),
        in_specs=[
            pl.BlockSpec((TOP_K, tm), lambda i: (0, i), memory_space=pltpu.SMEM),
            pl.BlockSpec((tm * ROW_CHUNKS, w), lambda i: (i, 0)),
        ],
        out_specs=pl.BlockSpec(memory_space=pl.ANY),
        out_shape=jax.ShapeDtypeStruct((rows * TOP_K, w), h2.dtype),
        scratch_shapes=[pltpu.SemaphoreType.DMA],
        compiler_params=_cparams(("arbitrary",)),
        name="moe_dispatch",
    )(pos, h2)


def _grouped_kernel(pb_ref, pe_ref, pv_ref, bnd_ref, xs_ref, wg_ref, wu_ref, wd_ref, y_ref, wgb, wub, wdb):
    i = pl.program_id(0)
    prev = jnp.maximum(i - 1, 0)
    j = pb_ref[i]
    e = pe_ref[i]
    blk = xs_ref.shape[0] // ROW_CHUNKS

    @pl.when((i == 0) | (pb_ref[prev] != j))
    def _():
        y_ref[...] = jnp.zeros_like(y_ref)

    @pl.when((i == 0) | (pe_ref[prev] != e))
    def _():
        wgb[...] = wg_ref[...].astype(BF16)
        wub[...] = wu_ref[...].astype(BF16)
        wdb[...] = wd_ref[...].astype(BF16)

    @pl.when(pv_ref[i] == 1)
    def _():
        xw = _load_chunk_rows(xs_ref, blk)
        a = _packed_dot(xw, wgb)
        u = _packed_dot(xw, wub)
        yv = jnp.dot((_silu(a) * u).astype(BF16), wdb[...], preferred_element_type=F32)
        rows = j * blk + lax.broadcasted_iota(jnp.int32, (blk, 1), 0)
        own = (rows >= bnd_ref[e]) & (rows < bnd_ref[e + 1])
        yw = _pack_bf16(yv)
        for c in range(ROW_CHUNKS):
            sl = pl.ds(c, blk, stride=ROW_CHUNKS)
            y_ref[sl, :] = jnp.where(own, yw[:, c * LANES:(c + 1) * LANES], y_ref[sl, :])


def _grouped(pb, pe, pv, bounds, xs, wg, wu, wd, layer):
    p, half = xs.shape
    d = 2 * ROW_CHUNKS * LANES
    blk = EXPERT_BLK * ROW_CHUNKS
    grid_spec = pltpu.PrefetchScalarGridSpec(
        num_scalar_prefetch=4,
        grid=(pb.shape[0],),
        in_specs=[
            pl.BlockSpec((blk, half), lambda i, pb, pe, pv, bnd: (pb[i], 0)),
            pl.BlockSpec((None, None, d, D_EXPERT), lambda i, pb, pe, pv, bnd: (layer, pe[i], 0, 0)),
            pl.BlockSpec((None, None, d, D_EXPERT), lambda i, pb, pe, pv, bnd: (layer, pe[i], 0, 0)),
            pl.BlockSpec((None, None, D_EXPERT, d), lambda i, pb, pe, pv, bnd: (layer, pe[i], 0, 0)),
        ],
        out_specs=pl.BlockSpec((blk, half), lambda i, pb, pe, pv, bnd: (pb[i], 0)),
        scratch_shapes=[
            pltpu.VMEM((d, D_EXPERT), BF16),
            pltpu.VMEM((d, D_EXPERT), BF16),
            pltpu.VMEM((D_EXPERT, d), BF16),
        ],
    )
    return pl.pallas_call(
        _grouped_kernel,
        grid_spec=grid_spec,
        out_shape=jax.ShapeDtypeStruct((p, half), jnp.uint32),
        compiler_params=_cparams(("arbitrary",)),
        name="moe_grouped_experts",
    )(pb, pe, pv, bounds, xs, wg, wu, wd)


def _combine_kernel(*refs, final, c_len):
    pos_ref, y_ref, gate_ref, h_ref, x_ref, g2c_ref, g2l_ref, sg_ref, su_ref, sd_ref = refs[:10]
    o_ref, buf, sem = refs[-3:]
    tm = x_ref.shape[0]

    def issue(t, carry):
        for k in range(TOP_K):
            pltpu.make_async_copy(_token_row(y_ref, pos_ref[k, t]), _token_row(buf.at[k], t * ROW_CHUNKS),
                                  sem).start(priority=k % 2)
        return carry

    lax.fori_loop(0, tm, issue, 0)
    hw = _load_chunk_rows(h_ref, tm)
    a = _packed_dot(hw, sg_ref)
    u = _packed_dot(hw, su_ref)
    acc = jnp.dot((_silu(a) * u).astype(BF16), sd_ref[...], preferred_element_type=F32)
    done = pl.ds(0, tm * ROW_CHUNKS)
    for k in range(TOP_K):
        pltpu.make_async_copy(y_ref.at[done], buf.at[k, done], sem).wait()
    g = gate_ref[...]
    half = hw.shape[-1]
    acc_hi = acc[:, :half]
    acc_lo = acc[:, half:]
    for k in range(TOP_K):
        hi, lo = _unpack_bf16(_load_chunk_rows(buf.at[k], tm))
        acc_hi = acc_hi + g[:, k:k + 1] * hi
        acc_lo = acc_lo + g[:, k:k + 1] * lo
    is_ctx = pl.program_id(1) * tm + lax.broadcasted_iota(jnp.int32, (tm, 1), 0) < c_len
    g2 = jnp.where(is_ctx, g2c_ref[...], g2l_ref[...])
    x = x_ref[...] + g2 * jnp.concatenate([acc_hi, acc_lo], axis=-1)
    if final:
        gf_ref = refs[10]
        ms = jnp.mean(x * x, axis=-1, keepdims=True)
        x = x * lax.rsqrt(ms + NORM_EPS) * gf_ref[...]
    o_ref[...] = x


def _combine(pos, y, gates, h2, x, mods3, layer, c_len, tm, sg, su, sd, g_final=None):
    b, r, d = x.shape
    nt = r // tm
    tile = pl.BlockSpec((None, tm, d), lambda bb, i: (bb, i, 0))
    in_specs = [
        pl.BlockSpec((TOP_K, tm), lambda bb, i: (0, bb * nt + i), memory_space=pltpu.SMEM),
        pl.BlockSpec(memory_space=pl.ANY),
        pl.BlockSpec((None, tm, LANES), lambda bb, i: (bb, i, 0)),
        pl.BlockSpec((tm * ROW_CHUNKS, LANES), lambda bb, i: (bb * nt + i, 0)),
        tile,
        _mod_spec(d, layer, 5, lambda bb, i: SUBLANES),
        _mod_spec(d, layer, 5, lambda bb, i: bb),
        pl.BlockSpec(sg.shape, lambda bb, i: (0, 0)),
        pl.BlockSpec(su.shape, lambda bb, i: (0, 0)),
        pl.BlockSpec(sd.shape, lambda bb, i: (0, 0)),
    ]
    args = [pos, y, gates, h2, x, mods3, mods3, sg, su, sd]
    if g_final is not None:
        in_specs.append(pl.BlockSpec((1, d), lambda bb, i: (0, 0)))
        args.append(g_final.reshape(1, d))
    return pl.pallas_call(
        functools.partial(_combine_kernel, final=g_final is not None, c_len=c_len),
        grid=(b, nt),
        in_specs=in_specs,
        out_specs=tile,
        out_shape=jax.ShapeDtypeStruct((b, r, d), F32),
        scratch_shapes=[pltpu.VMEM((TOP_K, tm * ROW_CHUNKS, LANES), jnp.uint32), pltpu.SemaphoreType.DMA],
        compiler_params=_cparams(("arbitrary", "arbitrary")),
        name="moe_combine",
    )(*args)


def _moe(x, routed, mods3, layer, c_len, combine_tile, w_e_gate, w_e_up, w_e_down, ws_gate, ws_up, ws_down,
         g_final=None):
    b, r, d = x.shape
    h2, eid, rnk, gates, cnt = routed
    pb, pe, pv, bounds = _moe_plan(cnt[:, 0].astype(jnp.int32), b * r * TOP_K)
    pos = _positions(eid, rnk, bounds[:N_EXPERTS])
    xs = _dispatch(h2, pos)
    y = _grouped(pb, pe, pv, bounds, xs, w_e_gate, w_e_up, w_e_down, layer)
    return _combine(pos, y, gates, h2, x, mods3, layer, c_len, combine_tile,
                    ws_gate.astype(BF16), ws_up.astype(BF16), ws_down.astype(BF16), g_final)


def kernel(x, c, ctx, c_ctx, w_mod, b_mod, g_mix, g_ffn, g_final, ab_w_in, ab_w_out, ab_conv_w, ab_conv_b, ab_w_r, ab_b_r, ab_w_i, ab_b_i, ab_lam, ab_sink, cd_w_in, cd_w_out, cd_lam, cd_subln_g, cd_conv_w, cd_conv_b, cd_dt_bias, cd_a_log, cd_d_skip, cd_norm_g, w_router, b_router, w_e_gate, w_e_up, w_e_down, ws_gate, ws_up, ws_down):
    bsz, s_len, d = x.shape
    c_len = ctx.shape[1]
    depth = w_mod.shape[0]
    assert depth == 2 and bsz == SUBLANES, "kernels are specialised to depth 2 and batch 8"
    assert c_len % ROW_TILE == 0 and s_len % ROW_TILE == 0
    nct_row = c_len // ROW_TILE
    nct_time = c_len // TIME_TILE

    c_all = jnp.concatenate([c, c_ctx[None], jnp.zeros((MOD_ROWS - bsz - 1, d), F32)], axis=0)
    mods3 = _modulations(c_all, w_mod, b_mod).reshape(depth * MOD_ROWS, 1, N_MOD * d)
    rope_tabs = _rope_tables(c_len, s_len)
    xc = jnp.concatenate([ctx, x], axis=1)

    w_in = ab_w_in[0].astype(BF16)
    q_hi = LRU_WIDTH + WIN_HEADS * HEAD_DIM
    x_hi = q_hi + LRU_WIDTH
    k_hi = x_hi + WIN_KV_HEADS * HEAD_DIM
    gate, q, xa, k, v = _project(xc, g_mix[0], mods3, 0, nct_row, rope_tabs, [
        (w_in[:, :LRU_WIDTH], None, BF16, False),
        (w_in[:, LRU_WIDTH:q_hi], HEAD_DIM ** -0.5 * LOG2E, BF16, False),
        (w_in[:, q_hi:x_hi], None, F32, True),
        (w_in[:, x_hi:k_hi], 1.0, BF16, False),
        (w_in[:, k_hi:], None, BF16, False),
    ])
    l_len = c_len + s_len
    w_gates = jnp.stack([jnp.concatenate([_block_diag(ab_w_r[0, dd]), _block_diag(ab_w_i[0, dd])], axis=1)
                         for dd in range(2)]).astype(BF16)
    b_gates = jnp.concatenate([ab_b_r[0], ab_b_i[0]], axis=-1).reshape(2, 1, 2 * LRU_WIDTH)
    rec = _rglru(xa.reshape(l_len, bsz, LRU_WIDTH), ab_conv_w[0], ab_conv_b[0], w_gates, b_gates,
                 ab_lam[0].reshape(2, 1, LRU_WIDTH), nct_time)
    att = _win_attention(q, k, v, ab_sink[0], c_len)
    w_out = ab_w_out[0].astype(BF16)
    xc, *routed = _out_even(xc, rec.reshape(2, l_len, bsz * LRU_WIDTH), gate, att, w_out[:LRU_WIDTH],
                            w_out[LRU_WIDTH:], mods3, 0, nct_row, g_ffn[0], w_router[0], b_router[0])
    xc = _moe(xc, routed, mods3, 0, c_len, math.gcd(MIXED_COMBINE_TILE, l_len), w_e_gate, w_e_up, w_e_down,
              ws_gate[0], ws_up[0], ws_down[0])

    w_in = cd_w_in[0].astype(BF16)
    qk = DIFF_HEADS * 2 * DIFF_DH
    z_hi = qk + SSD_INNER
    k_hi = z_hi + qk
    v_hi = k_hi + qk
    x_hi = v_hi + SSD_CONV_DIM
    w_dt = jnp.pad(w_in[:, x_hi:], ((0, 0), (0, LANES - 2 * SSD_HEADS)))
    q, z, k, v, xbc, dt = _project(xc, g_mix[1], mods3, 1, nct_row, rope_tabs, [
        (w_in[:, :qk], DIFF_DH ** -0.5 * LOG2E, BF16, False),
        (w_in[:, qk:z_hi], None, BF16, False),
        (w_in[:, z_hi:k_hi], 1.0, BF16, False),
        (w_in[:, k_hi:v_hi], None, BF16, False),
        (w_in[:, v_hi:x_hi], None, F32, False),
        (w_dt, None, F32, False),
    ])
    lam_init = 0.8 - 0.6 * math.exp(-0.3 * 1)
    diff = _diff_attention(q, k, v, cd_lam[0], cd_subln_g[0], lam_init, c_len)
    y2 = _ssd(xbc, dt, cd_conv_w[0], cd_conv_b[0], cd_dt_bias[0], cd_a_log[0], cd_d_skip[0], nct_time)
    w_out = cd_w_out[0].astype(BF16)
    xl, *routed = _out_odd(xc, diff, y2, z, cd_norm_g[0], w_out[:qk], w_out[qk:], mods3, 1, c_len,
                           g_ffn[1], w_router[1], b_router[1])
    return _moe(xl, routed, mods3, 1, 0, math.gcd(LATENT_COMBINE_TILE, s_len), w_e_gate, w_e_up, w_e_down,
                ws_gate[1], ws_up[1], ws_down[1], g_final=g_final)
```
